```python
import jax, jax.numpy as jnp
from jax import lax
import numpy as np

D_MODEL = 1024
BATCH = 8
SEQ = 2048
DEPTH = 2

GRID_W = 64
CTX_LEN = 256
EPS = 1e-6
N_MIXERS = 2
NEG_INF = -1e30

A_HEADS = 16
A_KV_HEADS = 2
A_HEAD_DIM = 64
A_GROUP = A_HEADS // A_KV_HEADS
A_WIDTH = A_HEADS * A_HEAD_DIM
A_KV_WIDTH = A_KV_HEADS * A_HEAD_DIM
A_SPLITS = (A_WIDTH, A_WIDTH + A_KV_WIDTH, A_WIDTH + 2 * A_KV_WIDTH)
A_IN_WIDTH = 2 * A_WIDTH + 2 * A_KV_WIDTH
WINDOW = 128
BLOCK = 128
ROPE_BASE = 10000.0
ROPE_FREQS = A_HEAD_DIM // 4

B_HEADS = 4
B_K_WIDTH = D_MODEL // 2
B_V_WIDTH = D_MODEL
B_KEY_DIM = B_K_WIDTH // B_HEADS
B_VAL_DIM = B_V_WIDTH // B_HEADS
B_SPLITS = (B_K_WIDTH, 2 * B_K_WIDTH, 2 * B_K_WIDTH + B_V_WIDTH)
B_IN_WIDTH = 2 * B_K_WIDTH + 2 * B_V_WIDTH
GATE_RANK = 16
GATE_TEMP = 16.0
CHUNK = 64

kernel_name = 'hybrid_swa_sink_gla_prefix_dit'


def rmsnorm(x, g):
    xf = x.astype(jnp.float32)
    y = xf * lax.rsqrt(jnp.mean(xf * xf, axis=-1, keepdims=True) + EPS)
    return (y * g.astype(jnp.float32)).astype(x.dtype)


def modulation(cvec, w_ada, b_ada, n):
    m = jax.nn.silu(cvec) @ w_ada[:, :n * D_MODEL] + b_ada[:n * D_MODEL]
    return jnp.split(m, n, axis=-1)


def modulate(h, shift, scale):
    return h * (1 + scale[..., None, :]) + shift[..., None, :]


def heads(t, n_heads):
    return t.reshape(t.shape[0], t.shape[1], n_heads, -1)


def axial_rope_tables(n_tokens):
    rows_n = n_tokens // GRID_W
    row = jnp.repeat(jnp.arange(rows_n, dtype=jnp.float32), GRID_W)
    col = jnp.tile(jnp.arange(GRID_W, dtype=jnp.float32), rows_n)
    inv_freq = ROPE_BASE ** (-jnp.arange(ROPE_FREQS, dtype=jnp.float32) / ROPE_FREQS)
    ang = jnp.stack([row[:, None] * inv_freq, col[:, None] * inv_freq], axis=1)
    return jnp.cos(ang), jnp.sin(ang)


def apply_axial_rope(x, cos, sin):
    b_, s_, h_, _ = x.shape
    xr = x.reshape(b_, s_, h_, 2, 2, ROPE_FREQS)
    x1, x2 = xr[..., 0, :], xr[..., 1, :]
    c = cos[None, :, None]
    s = sin[None, :, None]
    y = jnp.stack([x1 * c - x2 * s, x1 * s + x2 * c], axis=-2)
    return y.reshape(x.shape).astype(x.dtype)


def softmax_with_sink(parts, sink_logit):
    logits = jnp.concatenate(parts + [sink_logit], axis=-1)
    return jax.nn.softmax(logits, axis=-1)[..., :-1]


def attn_layer(x, xc, c, c_ctx, norm_g, w_ada, b_ada, w_in, sink, w_out, last):
    f32 = jnp.float32
    b_, s_, _ = x.shape
    n_ctx = xc.shape[1]
    nb = s_ // BLOCK
    qscale = A_HEAD_DIM ** -0.5
    sink_kg = sink.astype(f32).reshape(A_KV_HEADS, A_GROUP)

    shift, scl, gate = modulation(c, w_ada, b_ada, 3)
    h = modulate(rmsnorm(x, norm_g), shift, scl)
    q, k, v, g = jnp.split(h @ w_in, A_SPLITS, axis=-1)
    cos, sin = axial_rope_tables(s_)
    q = apply_axial_rope(heads(q, A_HEADS), cos, sin)
    k = apply_axial_rope(heads(k, A_KV_HEADS), cos, sin)
    v = heads(v, A_KV_HEADS)

    if last:
        shift_c, scl_c = modulation(c_ctx, w_ada, b_ada, 2)
        hc = modulate(rmsnorm(xc, norm_g), shift_c, scl_c)
        kc, vc = jnp.split(hc @ w_in[:, A_WIDTH:A_WIDTH + 2 * A_KV_WIDTH], 2, axis=-1)
    else:
        shift_c, scl_c, gate_c = modulation(c_ctx, w_ada, b_ada, 3)
        hc = modulate(rmsnorm(xc, norm_g), shift_c, scl_c)
        qc, kc, vc, gc = jnp.split(hc @ w_in, A_SPLITS, axis=-1)
    kc = heads(kc, A_KV_HEADS)
    vc = heads(vc, A_KV_HEADS)

    qb = q.reshape(b_, nb, BLOCK, A_KV_HEADS, A_GROUP, A_HEAD_DIM) * qscale
    pad = ((0, 0), (BLOCK, BLOCK), (0, 0), (0, 0))
    kp = jnp.pad(k, pad).reshape(b_, nb + 2, BLOCK, A_KV_HEADS, A_HEAD_DIM)
    vp = jnp.pad(v, pad).reshape(b_, nb + 2, BLOCK, A_KV_HEADS, A_HEAD_DIM)
    kwin = jnp.concatenate([kp[:, :-2], kp[:, 1:-1], kp[:, 2:]], axis=2)
    vwin = jnp.concatenate([vp[:, :-2], vp[:, 1:-1], vp[:, 2:]], axis=2)
    s_loc = jnp.einsum('bnqkgd,bnskd->bnkgqs', qb, kwin).astype(f32)
    s_ctx = jnp.einsum('bnqkgd,bckd->bnkgqc', qb, kc).astype(f32)
    qpos = jnp.arange(s_).reshape(nb, BLOCK)
    kpos = (jnp.arange(nb) * BLOCK - BLOCK)[:, None] + jnp.arange(3 * BLOCK)[None, :]
    kk = kpos[:, None, :]
    valid = (kk >= 0) & (kk < s_) & (jnp.abs(kk - qpos[:, :, None]) <= WINDOW)
    s_loc = jnp.where(valid[None, :, None, None], s_loc, NEG_INF)
    sink_b = jnp.broadcast_to(sink_kg[None, None, :, :, None, None], s_loc.shape[:-1] + (1,))
    p = softmax_with_sink([s_loc, s_ctx], sink_b)
    p_loc = p[..., :3 * BLOCK].astype(v.dtype)
    p_ctx = p[..., 3 * BLOCK:].astype(v.dtype)
    o = (jnp.einsum('bnkgqs,bnskd->bnqkgd', p_loc, vwin)
         + jnp.einsum('bnkgqc,bckd->bnqkgd', p_ctx, vc))
    o = o.reshape(b_, s_, A_WIDTH) * jax.nn.silu(g)
    x_new = x + gate[:, None, :] * (o @ w_out)

    if last:
        return x_new, None
    qcb = qc.reshape(b_, n_ctx, A_KV_HEADS, A_GROUP, A_HEAD_DIM) * qscale
    sc = jnp.einsum('bqkgd,bckd->bkgqc', qcb, kc).astype(f32)
    sink_c = jnp.broadcast_to(sink_kg[None, :, :, None, None], sc.shape[:-1] + (1,))
    pc = softmax_with_sink([sc], sink_c).astype(vc.dtype)
    oc = jnp.einsum('bkgqc,bckd->bqkgd', pc, vc).reshape(b_, n_ctx, A_WIDTH) * jax.nn.silu(gc)
    xc_new = xc + gate_c[..., None, :] * (oc @ w_out)
    return x_new, xc_new


def log_decay(h, wa1, wa2, ba):
    z = (h @ wa1) @ wa2 + ba
    return heads(jax.nn.log_sigmoid(z.astype(jnp.float32)) / GATE_TEMP, B_HEADS)


def gla_chunked(q, k, v, log_a, s0):
    b_, t_, h_, _ = q.shape
    nc = t_ // CHUNK
    rs = lambda t: t.reshape(b_, nc, CHUNK, h_, t.shape[-1]).astype(jnp.float32)
    qc, kc, vc, la = rs(q), rs(k), rs(v), rs(log_a)
    cum = jnp.cumsum(la, axis=2)
    total = cum[:, :, -1:]
    q_dec = qc * jnp.exp(cum)
    k_inv = kc * jnp.exp(-cum)
    k_end = kc * jnp.exp(total - cum)
    causal = jnp.tril(jnp.ones((CHUNK, CHUNK), dtype=bool))
    a = jnp.einsum('bnthd,bnshd->bnhts', q_dec, k_inv)
    a = jnp.where(causal, a, 0.0)
    o_intra = jnp.einsum('bnhts,bnshe->bnthe', a, vc)
    kv_chunk = jnp.einsum('bnshd,bnshe->bnhde', k_end, vc)
    decay_chunk = jnp.exp(total[:, :, 0])

    def step(s, inp):
        qd, kv, dec = inp
        o_inter = jnp.einsum('bthd,bhde->bthe', qd, s)
        return s * dec[..., None] + kv, o_inter

    s_final, o_inter = lax.scan(step, s0.astype(jnp.float32),
                                (jnp.swapaxes(q_dec, 0, 1), jnp.swapaxes(kv_chunk, 0, 1),
                                 jnp.swapaxes(decay_chunk, 0, 1)))
    o = o_intra + jnp.swapaxes(o_inter, 0, 1)
    return o.reshape(b_, t_, h_, v.shape[-1]), s_final


def gla_final_state(k, v, log_a):
    cum = jnp.cumsum(log_a.astype(jnp.float32), axis=1)
    kd = k.astype(jnp.float32) * jnp.exp(cum[:, -1:] - cum)
    return jnp.einsum('bthd,bthe->bhde', kd, v.astype(jnp.float32))


def gla_output(o, g, head_norm_g, w_out):
    of = o * lax.rsqrt(jnp.mean(o * o, axis=-1, keepdims=True) + EPS)
    of = of.reshape(o.shape[0], o.shape[1], B_V_WIDTH) * head_norm_g.astype(jnp.float32)
    return (of.astype(g.dtype) * jax.nn.silu(g)) @ w_out


def gla_layer(x, xc, c, c_ctx, norm_g, w_ada, b_ada, w_in, wa1_f, wa2_f, ba_f,
              wa1_b, wa2_b, ba_b, head_norm_g, w_out, last):
    flip = lambda t: jnp.flip(t, axis=1)
    kscale = B_KEY_DIM ** -0.5
    b_ = x.shape[0]

    shift, scl, gate = modulation(c, w_ada, b_ada, 3)
    h = modulate(rmsnorm(x, norm_g), shift, scl)
    q, k, v, g = jnp.split(h @ w_in, B_SPLITS, axis=-1)
    q = heads(q, B_HEADS) * kscale
    k = heads(k, B_HEADS)
    v = heads(v, B_HEADS)
    la_f = log_decay(h, wa1_f, wa2_f, ba_f)
    la_b = log_decay(h, wa1_b, wa2_b, ba_b)

    if last:
        shift_c, scl_c = modulation(c_ctx, w_ada, b_ada, 2)
        hc = modulate(rmsnorm(xc, norm_g), shift_c, scl_c)
        kc, vc = jnp.split(hc @ w_in[:, B_K_WIDTH:2 * B_K_WIDTH + B_V_WIDTH], [B_K_WIDTH], axis=-1)
        kc, vc = heads(kc, B_HEADS), heads(vc, B_HEADS)
        s_f = gla_final_state(kc, vc, log_decay(hc, wa1_f, wa2_f, ba_f))
        s_b = gla_final_state(flip(kc), flip(vc), flip(log_decay(hc, wa1_b, wa2_b, ba_b)))
        xc_new = None
    else:
        shift_c, scl_c, gate_c = modulation(c_ctx, w_ada, b_ada, 3)
        hc = modulate(rmsnorm(xc, norm_g), shift_c, scl_c)
        qc, kc, vc, gc = jnp.split(hc @ w_in, B_SPLITS, axis=-1)
        qc = heads(qc, B_HEADS) * kscale
        kc, vc = heads(kc, B_HEADS), heads(vc, B_HEADS)
        zeros = jnp.zeros((b_, B_HEADS, B_KEY_DIM, B_VAL_DIM), jnp.float32)
        oc_f, s_f = gla_chunked(qc, kc, vc, log_decay(hc, wa1_f, wa2_f, ba_f), zeros)
        oc_b, s_b = gla_chunked(flip(qc), flip(kc), flip(vc),
                                flip(log_decay(hc, wa1_b, wa2_b, ba_b)), zeros)
        oc = gla_output(oc_f + flip(oc_b), gc, head_norm_g, w_out)
        xc_new = xc + gate_c[..., None, :] * oc

    o_f, _ = gla_chunked(q, k, v, la_f, s_f)
    o_b, _ = gla_chunked(flip(q), flip(k), flip(v), flip(la_b), s_b)
    o = gla_output(o_f + flip(o_b), g, head_norm_g, w_out)
    x_new = x + gate[:, None, :] * o
    return x_new, xc_new


def setup_inputs(seed: int = 0) -> dict:
    key = jax.random.key(seed)
    ks = iter(jax.random.split(key, 32))
    D = D_MODEL
    nrm = lambda shape, s: jax.random.normal(next(ks), shape, jnp.float32) * s
    return {
        'x': nrm((BATCH, SEQ, D), 1.0),
        'c': nrm((BATCH, D), 1.0),
        'ctx': nrm((BATCH, CTX_LEN, D), 1.0),
        'c_ctx': nrm((D,), 1.0),
        'l0_norm_g': 1.0 + nrm((D,), 0.02),
        'l0_w_ada': nrm((D, 3 * D), D ** -0.5),
        'l0_b_ada': nrm((3 * D,), 0.02),
        'l0_w_in': nrm((D, A_IN_WIDTH), D ** -0.5),
        'l0_sink': nrm((A_HEADS,), 0.5),
        'l0_w_out': nrm((A_WIDTH, D), A_WIDTH ** -0.5),
        'l1_norm_g': 1.0 + nrm((D,), 0.02),
        'l1_w_ada': nrm((D, 3 * D), D ** -0.5),
        'l1_b_ada': nrm((3 * D,), 0.02),
        'l1_w_in': nrm((D, B_IN_WIDTH), D ** -0.5),
        'l1_wa1_f': nrm((D, GATE_RANK), D ** -0.5),
        'l1_wa2_f': nrm((GATE_RANK, B_K_WIDTH), GATE_RANK ** -0.5),
        'l1_ba_f': nrm((B_K_WIDTH,), 0.1),
        'l1_wa1_b': nrm((D, GATE_RANK), D ** -0.5),
        'l1_wa2_b': nrm((GATE_RANK, B_K_WIDTH), GATE_RANK ** -0.5),
        'l1_ba_b': nrm((B_K_WIDTH,), 0.1),
        'l1_head_norm_g': 1.0 + nrm((B_V_WIDTH,), 0.02),
        'l1_w_out': nrm((B_V_WIDTH, D), B_V_WIDTH ** -0.5),
        'final_norm_g': 1.0 + nrm((D,), 0.02),
    }


def reference(x, c, ctx, c_ctx,
              l0_norm_g, l0_w_ada, l0_b_ada, l0_w_in, l0_sink, l0_w_out,
              l1_norm_g, l1_w_ada, l1_b_ada, l1_w_in, l1_wa1_f, l1_wa2_f, l1_ba_f,
              l1_wa1_b, l1_wa2_b, l1_ba_b, l1_head_norm_g, l1_w_out,
              final_norm_g):
    mixers = (attn_layer, gla_layer)
    layer_params = (
        (l0_norm_g, l0_w_ada, l0_b_ada, l0_w_in, l0_sink, l0_w_out),
        (l1_norm_g, l1_w_ada, l1_b_ada, l1_w_in, l1_wa1_f, l1_wa2_f, l1_ba_f,
         l1_wa1_b, l1_wa2_b, l1_ba_b, l1_head_norm_g, l1_w_out),
    )
    xc = ctx
    for i in range(DEPTH):
        fn = mixers[i % N_MIXERS]
        x, xc = fn(x, xc, c, c_ctx, *layer_params[i], last=(i == DEPTH - 1))
    return rmsnorm(x, final_norm_g)
```

```python
import functools

import jax
import jax.numpy as jnp
from jax import lax
from jax.experimental import pallas as pl
from jax.experimental.pallas import tpu as pltpu

F32 = jnp.float32
BF16 = jnp.bfloat16

D = 1024
GRID_W = 64
EPS = 1e-6
NEG_INF = -1e30

A_HEADS = 16
A_KV_HEADS = 2
A_GROUP = A_HEADS // A_KV_HEADS
A_HEAD_DIM = 64
A_WIDTH = A_HEADS * A_HEAD_DIM
A_KV_WIDTH = A_KV_HEADS * A_HEAD_DIM
BLOCK = 128
ROPE_BASE = 10000.0
ROPE_FREQS = A_HEAD_DIM // 4
Q_SCALE = A_HEAD_DIM ** -0.5

B_HEADS = 4
B_K_WIDTH = D // 2
B_V_WIDTH = D
B_KEY_DIM = B_K_WIDTH // B_HEADS
B_VAL_DIM = B_V_WIDTH // B_HEADS
GATE_RANK = 16
GATE_TEMP = 16.0
CHUNK = 64
K_SCALE = B_KEY_DIM ** -0.5

LANES = 128
MOD_ROWS = 16
ROW_TILE = 256
SCAN_TILE = 256
VMEM_LIMIT = 48 * 1024 * 1024


def _params(*sem):
    return pltpu.CompilerParams(dimension_semantics=sem, vmem_limit_bytes=VMEM_LIMIT)


def _silu(x):
    return x / (1.0 + jnp.exp(-x))


def _dot(a, b):
    return jnp.dot(a, b, preferred_element_type=F32)


def _dot_nt(a, b):
    return lax.dot_general(a, b, (((1,), (1,)), ((), ())), preferred_element_type=F32)


def _dot_tn(a, b):
    return lax.dot_general(a, b, (((0,), (0,)), ((), ())), preferred_element_type=F32)


def _norm_mod(x, g, m):
    ms = jnp.mean(x * x, axis=-1, keepdims=True)
    y = x * lax.rsqrt(ms + EPS) * g
    return y * (1.0 + m[:, D:2 * D]) + m[:, 0:D]


def _mod_kernel(c_ref, w_ref, b_ref, o_ref):
    s = _silu(c_ref[...])
    o_ref[...] = _dot(s.astype(BF16), w_ref[...].astype(BF16)) + b_ref[...]


def _modulation(cvec, w_ada, b_ada):
    n = w_ada.shape[1] // D
    return pl.pallas_call(
        _mod_kernel,
        grid=(n,),
        in_specs=[
            pl.BlockSpec((MOD_ROWS, D), lambda j: (0, 0)),
            pl.BlockSpec((D, D), lambda j: (0, j)),
            pl.BlockSpec((1, D), lambda j: (0, j)),
        ],
        out_specs=pl.BlockSpec((MOD_ROWS, D), lambda j: (0, j)),
        out_shape=jax.ShapeDtypeStruct((MOD_ROWS, n * D), F32),
        compiler_params=_params("arbitrary"),
        name="modulation",
    )(cvec, w_ada, b_ada.reshape(1, n * D))


def _rope(t, cos, sin_hi, sin_lo):
    return t * cos + pltpu.roll(t, LANES - ROPE_FREQS, 1) * sin_hi + pltpu.roll(t, ROPE_FREQS, 1) * sin_lo


def _inproj_attn_kernel(rope, x_ref, m_ref, g_ref, w_ref, *rest):
    if rope:
        cos_ref, shi_ref, slo_ref, q_ref, k_ref, v_ref, sg_ref = rest
        cos, shi, slo = cos_ref[...], shi_ref[...], slo_ref[...]
    else:
        q_ref, k_ref, v_ref, sg_ref = rest
    hb = _norm_mod(x_ref[0], g_ref[...], m_ref[0]).astype(BF16)
    q = _dot(hb, w_ref[:, 0:A_WIDTH])
    for j in range(A_WIDTH // LANES):
        qj = q[:, j * LANES:(j + 1) * LANES]
        if rope:
            qj = _rope(qj, cos, shi, slo)
        q_ref[0, :, j * LANES:(j + 1) * LANES] = (qj * Q_SCALE).astype(BF16)
    kv = _dot(hb, w_ref[:, A_WIDTH:A_WIDTH + 2 * A_KV_WIDTH])
    k = kv[:, 0:A_KV_WIDTH]
    if rope:
        k = _rope(k, cos, shi, slo)
    k_ref[0] = k.astype(BF16)
    v_ref[0] = kv[:, A_KV_WIDTH:].astype(BF16)
    g = _dot(hb, w_ref[:, A_WIDTH + 2 * A_KV_WIDTH:])
    sg_ref[0] = _silu(g).astype(BF16)


def _inproj_attn(x, mod3, mod_row, norm_g, w, tables):
    b, s, _ = x.shape
    tm = min(ROW_TILE, s)
    rope = tables is not None
    n_w = w.shape[1]
    if mod_row is None:
        mod_map = lambda bi, i: (bi, 0, 0)
    else:
        mod_map = lambda bi, i: (mod_row, 0, 0)
    in_specs = [
        pl.BlockSpec((1, tm, D), lambda bi, i: (bi, i, 0)),
        pl.BlockSpec((1, 1, 3 * D), mod_map),
        pl.BlockSpec((1, D), lambda bi, i: (0, 0)),
        pl.BlockSpec((D, n_w), lambda bi, i: (0, 0)),
    ]
    args = [x, mod3, norm_g.reshape(1, D), w]
    if rope:
        in_specs += [pl.BlockSpec((tm, LANES), lambda bi, i: (i, 0))] * 3
        args += list(tables)
    out_shape = (
        jax.ShapeDtypeStruct((b, s, A_WIDTH), BF16),
        jax.ShapeDtypeStruct((b, s, A_KV_WIDTH), BF16),
        jax.ShapeDtypeStruct((b, s, A_KV_WIDTH), BF16),
        jax.ShapeDtypeStruct((b, s, A_WIDTH), BF16),
    )
    out_specs = (
        pl.BlockSpec((1, tm, A_WIDTH), lambda bi, i: (bi, i, 0)),
        pl.BlockSpec((1, tm, A_KV_WIDTH), lambda bi, i: (bi, i, 0)),
        pl.BlockSpec((1, tm, A_KV_WIDTH), lambda bi, i: (bi, i, 0)),
        pl.BlockSpec((1, tm, A_WIDTH), lambda bi, i: (bi, i, 0)),
    )
    return pl.pallas_call(
        functools.partial(_inproj_attn_kernel, rope),
        grid=(b, s // tm),
        in_specs=in_specs,
        out_specs=out_specs,
        out_shape=out_shape,
        compiler_params=_params("parallel", "parallel"),
        name="inproj_attn_rope" if rope else "inproj_attn_ctx",
    )(*args)


def _attn_kernel(local, n_blocks, sink_ref, q_ref, *rest):
    if local:
        (kp_ref, kc_ref, kn_ref, vp_ref, vc_ref, vn_ref, kx_ref, vx_ref,
         sg_ref, x_ref, m_ref, w_ref, o_ref) = rest
        k_parts = [kp_ref[0], kc_ref[0], kn_ref[0], kx_ref[0]]
        v_parts = [vp_ref[0], vc_ref[0], vn_ref[0], vx_ref[0]]
    else:
        kx_ref, vx_ref, sg_ref, x_ref, m_ref, w_ref, o_ref = rest
        k_parts = [kx_ref[0]]
        v_parts = [vx_ref[0]]
    n = pl.program_id(1)
    kwin = jnp.concatenate(k_parts, axis=0) if local else k_parts[0]
    vwin = jnp.concatenate(v_parts, axis=0) if local else v_parts[0]
    n_keys = kwin.shape[0]
    lane = lax.broadcasted_iota(jnp.int32, kwin.shape, 1)
    zero = jnp.zeros_like(kwin)
    kbd = jnp.concatenate([jnp.where(lane < A_HEAD_DIM, kwin, zero),
                           jnp.where(lane >= A_HEAD_DIM, kwin, zero)], axis=0)
    vbd = jnp.concatenate([jnp.where(lane < A_HEAD_DIM, vwin, zero),
                           jnp.where(lane >= A_HEAD_DIM, vwin, zero)], axis=0)
    if local:
        qi = lax.broadcasted_iota(jnp.int32, (BLOCK, BLOCK), 0)
        kj = lax.broadcasted_iota(jnp.int32, (BLOCK, BLOCK), 1)
        ok_prev = (kj >= qi) & (n > 0)
        ok_next = (kj <= qi) & (n < n_blocks - 1)
    out_lane = lax.broadcasted_iota(jnp.int32, (BLOCK, LANES), 1)

    outs = []
    for p in range(A_GROUP):
        qp = q_ref[0, :, p * LANES:(p + 1) * LANES]
        s = _dot_nt(qp, kbd)
        probs, inv = [], []
        for hh in range(2):
            sh = s[:, hh * n_keys:(hh + 1) * n_keys]
            if local:
                parts = [jnp.where(ok_prev, sh[:, 0:BLOCK], NEG_INF),
                         sh[:, BLOCK:2 * BLOCK],
                         jnp.where(ok_next, sh[:, 2 * BLOCK:3 * BLOCK], NEG_INF),
                         sh[:, 3 * BLOCK:]]
            else:
                parts = [sh]
            sink = sink_ref[2 * p + hh]
            mx = functools.reduce(jnp.maximum, [jnp.max(t, axis=-1, keepdims=True) for t in parts])
            mx = jnp.maximum(mx, sink)
            es = [jnp.exp(t - mx) for t in parts]
            den = functools.reduce(jnp.add, [jnp.sum(t, axis=-1, keepdims=True) for t in es])
            den = den + jnp.exp(sink - mx)
            inv.append(1.0 / den)
            probs += [t.astype(BF16) for t in es]
        pv = _dot(jnp.concatenate(probs, axis=1), vbd)
        scale = jnp.where(out_lane < A_HEAD_DIM, inv[0], inv[1])
        sg = sg_ref[0, :, p * LANES:(p + 1) * LANES].astype(F32)
        outs.append((pv * scale * sg).astype(BF16))
    y = _dot(jnp.concatenate(outs, axis=1), w_ref[...])
    gate = m_ref[0][:, 2 * D:3 * D]
    o_ref[0] = x_ref[0] + gate * y


def _attention(sink, q, k, v, kx, vx, sg, x, mod3, mod_row, w_out, local):
    b, s, _ = q.shape
    nb = s // BLOCK
    n_ctx = kx.shape[1]
    blk = lambda w: pl.BlockSpec((1, BLOCK, w), lambda bi, i: (bi, i, 0))
    if mod_row is None:
        mod_map = lambda bi, i: (bi, 0, 0)
    else:
        mod_map = lambda bi, i: (mod_row, 0, 0)
    in_specs = [pl.BlockSpec(memory_space=pltpu.SMEM), blk(A_WIDTH)]
    args = [sink, q]
    if local:
        prev = pl.BlockSpec((1, BLOCK, A_KV_WIDTH), lambda bi, i: (bi, jnp.maximum(i - 1, 0), 0))
        nxt = pl.BlockSpec((1, BLOCK, A_KV_WIDTH), lambda bi, i: (bi, jnp.minimum(i + 1, nb - 1), 0))
        in_specs += [prev, blk(A_KV_WIDTH), nxt, prev, blk(A_KV_WIDTH), nxt]
        args += [k, k, k, v, v, v]
    ctx_spec = pl.BlockSpec((1, n_ctx, A_KV_WIDTH), lambda bi, i: (bi, 0, 0))
    in_specs += [ctx_spec, ctx_spec, blk(A_WIDTH), blk(D),
                 pl.BlockSpec((1, 1, 3 * D), mod_map),
                 pl.BlockSpec((A_WIDTH, D), lambda bi, i: (0, 0))]
    args += [kx, vx, sg, x, mod3, w_out]
    return pl.pallas_call(
        functools.partial(_attn_kernel, local, nb),
        grid=(b, nb),
        in_specs=in_specs,
        out_specs=blk(D),
        out_shape=jax.ShapeDtypeStruct((b, s, D), F32),
        compiler_params=_params("parallel", "parallel"),
        name="attn_local" if local else "attn_ctx",
    )(*args)


def _scan_rows(x, reverse):
    c = x.shape[0]
    row = lax.broadcasted_iota(jnp.int32, x.shape, 0)
    s = 1
    while s < c:
        if reverse:
            x = x + jnp.where(row < c - s, pltpu.roll(x, c - s, 0), 0.0)
        else:
            x = x + jnp.where(row >= s, pltpu.roll(x, s, 0), 0.0)
        s *= 2
    return x


def _inproj_gla_kernel(chunk, with_q, x_ref, m_ref, g_ref, w_ref, wa1_ref, wa2_ref, ba_ref, *outs):
    if with_q:
        (qd_f, ki_f, ke_f, dec_f, qd_b, ki_b, ke_b, dec_b, v_ref, sg_ref) = outs
    else:
        ke_f, ke_b, v_ref = outs
    tm = x_ref.shape[1]
    hb = _norm_mod(x_ref[0], g_ref[...], m_ref[0]).astype(BF16)
    r = _dot(hb, wa1_ref[...])
    z = _dot(r.astype(BF16), wa2_ref[...]) + ba_ref[...]
    la = (jnp.minimum(z, 0.0) - jnp.log1p(jnp.exp(-jnp.abs(z)))) / GATE_TEMP
    k = _dot(hb, w_ref[:, B_K_WIDTH:2 * B_K_WIDTH])
    if with_q:
        q = _dot(hb, w_ref[:, 0:B_K_WIDTH]) * K_SCALE
    v_ref[0] = _dot(hb, w_ref[:, 2 * B_K_WIDTH:2 * B_K_WIDTH + B_V_WIDTH]).astype(BF16)
    if with_q:
        sg_ref[0] = _silu(_dot(hb, w_ref[:, 2 * B_K_WIDTH + B_V_WIDTH:])).astype(BF16)
    for c in range(tm // chunk):
        rows = slice(c * chunk, (c + 1) * chunk)
        kc = k[rows]
        for reverse in (False, True):
            lac = la[rows, B_K_WIDTH:] if reverse else la[rows, 0:B_K_WIDTH]
            cum = _scan_rows(lac, reverse)
            tot = cum[0:1] if reverse else cum[chunk - 1:chunk]
            ke = (kc * jnp.exp(tot - cum)).astype(BF16)
            if reverse:
                ke_b[0, rows] = ke
            else:
                ke_f[0, rows] = ke
            if with_q:
                qd = (q[rows] * jnp.exp(cum)).astype(BF16)
                ki = (kc * jnp.exp(-cum)).astype(BF16)
                dec = jnp.exp(tot)
                if reverse:
                    qd_b[0, rows], ki_b[0, rows], dec_b[0, 0, c:c + 1] = qd, ki, dec
                else:
                    qd_f[0, rows], ki_f[0, rows], dec_f[0, 0, c:c + 1] = qd, ki, dec


def _inproj_gla(x, mod3, mod_row, norm_g, w, wa1, wa2, ba, chunk, with_q):
    b, s, _ = x.shape
    tm = min(ROW_TILE, s)
    nt = s // tm
    cpt = tm // chunk
    if mod_row is None:
        mod_map = lambda bi, i: (bi, 0, 0)
    else:
        mod_map = lambda bi, i: (mod_row, 0, 0)
    const = lambda shape: pl.BlockSpec(shape, lambda bi, i: (0,) * len(shape))
    in_specs = [
        pl.BlockSpec((1, tm, D), lambda bi, i: (bi, i, 0)),
        pl.BlockSpec((1, 1, 3 * D), mod_map),
        const((1, D)), const(w.shape), const(wa1.shape), const(wa2.shape), const(ba.shape),
    ]
    rows = lambda w_: (jax.ShapeDtypeStruct((b, s, w_), BF16),
                       pl.BlockSpec((1, tm, w_), lambda bi, i: (bi, i, 0)))
    decs = (jax.ShapeDtypeStruct((b, nt, cpt, B_K_WIDTH), F32),
            pl.BlockSpec((1, 1, cpt, B_K_WIDTH), lambda bi, i: (bi, i, 0, 0)))
    if with_q:
        outs = [rows(B_K_WIDTH)] * 3 + [decs] + [rows(B_K_WIDTH)] * 3 + [decs] + [rows(B_V_WIDTH)] * 2
    else:
        outs = [rows(B_K_WIDTH)] * 2 + [rows(B_V_WIDTH)]
    return pl.pallas_call(
        functools.partial(_inproj_gla_kernel, chunk, with_q),
        grid=(b, nt),
        in_specs=in_specs,
        out_specs=tuple(o[1] for o in outs),
        out_shape=tuple(o[0] for o in outs),
        compiler_params=_params("parallel", "parallel"),
        name="inproj_gla" if with_q else "inproj_gla_ctx",
    )(x, mod3, norm_g.reshape(1, D), w, wa1, wa2, ba)


def _gla_state_kernel(kf_ref, kb_ref, v_ref, sf_ref, sb_ref):
    for h in range(B_HEADS):
        vh = v_ref[0, :, h * B_VAL_DIM:(h + 1) * B_VAL_DIM]
        ks = slice(h * B_KEY_DIM, (h + 1) * B_KEY_DIM)
        sf_ref[0, h] = _dot_tn(vh, kf_ref[0, :, ks])
        sb_ref[0, h] = _dot_tn(vh, kb_ref[0, :, ks])


def _gla_state(ke_f, ke_b, v):
    b, n, _ = v.shape
    st = jax.ShapeDtypeStruct((b, B_HEADS, B_VAL_DIM, B_KEY_DIM), F32)
    st_spec = pl.BlockSpec((1, B_HEADS, B_VAL_DIM, B_KEY_DIM), lambda bi: (bi, 0, 0, 0))
    return pl.pallas_call(
        _gla_state_kernel,
        grid=(b,),
        in_specs=[pl.BlockSpec((1, n, B_K_WIDTH), lambda bi: (bi, 0, 0))] * 2
        + [pl.BlockSpec((1, n, B_V_WIDTH), lambda bi: (bi, 0, 0))],
        out_specs=(st_spec, st_spec),
        out_shape=(st, st),
        compiler_params=_params("parallel"),
        name="gla_ctx_state",
    )(ke_f, ke_b, v)


def _gla_scan_kernel(reverse, qd_ref, ki_ref, ke_ref, dec_ref, v_ref, s0_ref, *rest):
    if reverse:
        (of_ref, sg_ref, x_ref, m_ref, hn_ref, w_ref, fn_ref, o_ref, st_ref) = rest
    else:
        o_ref, st_ref = rest

    @pl.when(pl.program_id(1) == 0)
    def _():
        st_ref[...] = s0_ref[0]

    ti = lax.broadcasted_iota(jnp.int32, (CHUNK, CHUNK), 0)
    si = lax.broadcasted_iota(jnp.int32, (CHUNK, CHUNK), 1)
    causal = (ti <= si) if reverse else (ti >= si)
    n_chunks = SCAN_TILE // CHUNK
    order = range(n_chunks - 1, -1, -1) if reverse else range(n_chunks)
    dec = dec_ref[0, 0]
    o_rows = [None] * n_chunks
    for c in order:
        rows = slice(c * CHUNK, (c + 1) * CHUNK)
        heads = []
        for h in range(B_HEADS):
            ks = slice(h * B_KEY_DIM, (h + 1) * B_KEY_DIM)
            qd = qd_ref[0, rows, ks]
            vh = v_ref[0, rows, h * B_VAL_DIM:(h + 1) * B_VAL_DIM]
            a = jnp.where(causal, _dot_nt(qd, ki_ref[0, rows, ks]), 0.0).astype(BF16)
            st = st_ref[h]
            heads.append(_dot(a, vh) + _dot_nt(qd, st.astype(BF16)))
            st_ref[h] = st * dec[c:c + 1, ks] + _dot_tn(vh, ke_ref[0, rows, ks])
        o_rows[c] = jnp.concatenate(heads, axis=1)
    o = jnp.concatenate(o_rows, axis=0)
    if not reverse:
        o_ref[0] = o
        return
    o = o + of_ref[0]
    normed = []
    for h in range(B_HEADS):
        oh = o[:, h * B_VAL_DIM:(h + 1) * B_VAL_DIM]
        normed.append(oh * lax.rsqrt(jnp.mean(oh * oh, axis=-1, keepdims=True) + EPS))
    of = jnp.concatenate(normed, axis=1) * hn_ref[...]
    y = _dot((of * sg_ref[0].astype(F32)).astype(BF16), w_ref[...])
    xn = x_ref[0] + m_ref[0][:, 2 * D:3 * D] * y
    ms = jnp.mean(xn * xn, axis=-1, keepdims=True)
    o_ref[0] = xn * lax.rsqrt(ms + EPS) * fn_ref[...]


def _gla_scan(reverse, qd, ki, ke, dec, v, s0, extra=None):
    b, s, _ = qd.shape
    nt = s // SCAN_TILE
    if reverse:
        idx = lambda i: nt - 1 - i
    else:
        idx = lambda i: i
    rows = lambda w_: pl.BlockSpec((1, SCAN_TILE, w_), lambda bi, i: (bi, idx(i), 0))
    const = lambda shape: pl.BlockSpec(shape, lambda bi, i: (0,) * len(shape))
    dec = dec.reshape(b, nt, SCAN_TILE // CHUNK, B_K_WIDTH)
    in_specs = [rows(B_K_WIDTH)] * 3 + [
        pl.BlockSpec((1, 1, SCAN_TILE // CHUNK, B_K_WIDTH), lambda bi, i: (bi, idx(i), 0, 0)),
        rows(B_V_WIDTH),
        pl.BlockSpec((1, B_HEADS, B_VAL_DIM, B_KEY_DIM), lambda bi, i: (bi, 0, 0, 0)),
    ]
    args = [qd, ki, ke, dec, v, s0]
    if reverse:
        o_f, sg, x, mod3, head_g, w_out, final_g = extra
        in_specs += [rows(B_V_WIDTH), rows(B_V_WIDTH), rows(D),
                     pl.BlockSpec((1, 1, 3 * D), lambda bi, i: (bi, 0, 0)),
                     const((1, B_V_WIDTH)), const((B_V_WIDTH, D)), const((1, D))]
        args += [o_f, sg, x, mod3, head_g.reshape(1, B_V_WIDTH), w_out, final_g.reshape(1, D)]
    return pl.pallas_call(
        functools.partial(_gla_scan_kernel, reverse),
        grid=(b, nt),
        in_specs=in_specs,
        out_specs=rows(D),
        out_shape=jax.ShapeDtypeStruct((b, s, D), F32),
        scratch_shapes=[pltpu.VMEM((B_HEADS, B_VAL_DIM, B_KEY_DIM), F32)],
        compiler_params=_params("parallel", "arbitrary"),
        name="gla_scan_bwd" if reverse else "gla_scan_fwd",
    )(*args)


def _rope_tables(n_tokens):
    rows_n = n_tokens // GRID_W
    row = jnp.repeat(jnp.arange(rows_n, dtype=F32), GRID_W)
    col = jnp.tile(jnp.arange(GRID_W, dtype=F32), rows_n)
    inv_freq = ROPE_BASE ** (-jnp.arange(ROPE_FREQS, dtype=F32) / ROPE_FREQS)
    ang = jnp.stack([row[:, None] * inv_freq, col[:, None] * inv_freq], axis=1)
    cos, sin = jnp.cos(ang), jnp.sin(ang)
    zero = jnp.zeros_like(sin)
    tile = lambda t: jnp.tile(t.reshape(n_tokens, A_HEAD_DIM), (1, LANES // A_HEAD_DIM))
    return (tile(jnp.stack([cos, cos], axis=2)),
            tile(jnp.stack([-sin, zero], axis=2)),
            tile(jnp.stack([zero, sin], axis=2)))


def _head_pair_perm():
    cols = []
    for p in range(A_GROUP):
        for h in (p, p + A_GROUP):
            cols.extend(range(h * A_HEAD_DIM, (h + 1) * A_HEAD_DIM))
    heads = [h for p in range(A_GROUP) for h in (p, p + A_GROUP)]
    return jnp.asarray(cols, dtype=jnp.int32), jnp.asarray(heads, dtype=jnp.int32)


def kernel(x, c, ctx, c_ctx, l0_norm_g, l0_w_ada, l0_b_ada, l0_w_in, l0_sink, l0_w_out, l1_norm_g, l1_w_ada, l1_b_ada, l1_w_in, l1_wa1_f, l1_wa2_f, l1_ba_f, l1_wa1_b, l1_wa2_b, l1_ba_b, l1_head_norm_g, l1_w_out, final_norm_g):
    b, s, _ = x.shape
    ctx_row = b

    cvec = jnp.concatenate([c, c_ctx[None, :], jnp.zeros((MOD_ROWS - b - 1, D), F32)], axis=0)
    mod0 = _modulation(cvec, l0_w_ada, l0_b_ada).reshape(MOD_ROWS, 1, 3 * D)
    mod1 = _modulation(cvec, l1_w_ada, l1_b_ada).reshape(MOD_ROWS, 1, 3 * D)

    cols, heads = _head_pair_perm()
    q_end, kv_end = A_WIDTH, A_WIDTH + 2 * A_KV_WIDTH
    w0 = jnp.concatenate([l0_w_in[:, :q_end][:, cols], l0_w_in[:, q_end:kv_end],
                          l0_w_in[:, kv_end:][:, cols]], axis=1).astype(BF16)
    w0_out = l0_w_out[cols, :].astype(BF16)
    sink = l0_sink.astype(F32)[heads]
    q, k, v, sg = _inproj_attn(x, mod0, None, l0_norm_g, w0, _rope_tables(s))
    qc, kc, vc, sgc = _inproj_attn(ctx, mod0, ctx_row, l0_norm_g, w0, None)
    x1 = _attention(sink, q, k, v, kc, vc, sg, x, mod0, None, w0_out, True)
    xc1 = _attention(sink, qc, None, None, kc, vc, sgc, ctx, mod0, ctx_row, w0_out, False)

    w1 = l1_w_in.astype(BF16)
    wa1 = jnp.concatenate([l1_wa1_f, l1_wa1_b, jnp.zeros((D, LANES - 2 * GATE_RANK), F32)], axis=1).astype(BF16)
    wa2 = jnp.zeros((LANES, 2 * B_K_WIDTH), F32)
    wa2 = wa2.at[0:GATE_RANK, 0:B_K_WIDTH].set(l1_wa2_f)
    wa2 = wa2.at[GATE_RANK:2 * GATE_RANK, B_K_WIDTH:].set(l1_wa2_b).astype(BF16)
    ba = jnp.concatenate([l1_ba_f, l1_ba_b]).reshape(1, 2 * B_K_WIDTH)
    kec_f, kec_b, vcx = _inproj_gla(xc1, mod1, ctx_row, l1_norm_g, w1, wa1, wa2, ba, ctx.shape[1], False)
    s_f, s_b = _gla_state(kec_f, kec_b, vcx)
    qd_f, ki_f, ke_f, dec_f, qd_b, ki_b, ke_b, dec_b, v1, sg1 = _inproj_gla(
        x1, mod1, None, l1_norm_g, w1, wa1, wa2, ba, CHUNK, True)
    o_f = _gla_scan(False, qd_f, ki_f, ke_f, dec_f, v1, s_f)
    return _gla_scan(True, qd_b, ki_b, ke_b, dec_b, v1, s_b,
                     (o_f, sg1, x1, mod1, l1_head_norm_g, l1_w_out.astype(BF16), final_norm_g))
```

```python
import functools

import jax
import jax.numpy as jnp
from jax import lax
from jax.experimental import pallas as pl
from jax.experimental.pallas import tpu as pltpu

F32 = jnp.float32
BF16 = jnp.bfloat16

D = 1024
GRID_W = 64
EPS = 1e-6
NEG_INF = -1e30

A_HEADS = 16
A_KV_HEADS = 2
A_GROUP = A_HEADS // A_KV_HEADS
A_HEAD_DIM = 64
A_WIDTH = A_HEADS * A_HEAD_DIM
A_KV_WIDTH = A_KV_HEADS * A_HEAD_DIM
BLOCK = 128
ROPE_BASE = 10000.0
ROPE_FREQS = A_HEAD_DIM // 4
Q_SCALE = A_HEAD_DIM ** -0.5
LOG2_E = 1.4426950408889634

B_HEADS = 4
B_K_WIDTH = D // 2
B_V_WIDTH = D
B_KEY_DIM = B_K_WIDTH // B_HEADS
B_VAL_DIM = B_V_WIDTH // B_HEADS
GATE_RANK = 16
GATE_TEMP = 16.0
CHUNK = 64
K_SCALE = B_KEY_DIM ** -0.5

LANES = 128
MOD_ROWS = 16
ROW_TILE = 256
SCAN_TILE = 256
SUM_ROWS = 16
A_CHUNKS_PER_DOT = 8
VMEM_LIMIT = 48 * 1024 * 1024


def _params(*sem):
    return pltpu.CompilerParams(dimension_semantics=sem, vmem_limit_bytes=VMEM_LIMIT)


def _silu(x):
    return x / (1.0 + jnp.exp(-x))


def _dot(a, b):
    return jnp.dot(a, b, preferred_element_type=F32)


def _dot_nt(a, b):
    return lax.dot_general(a, b, (((1,), (1,)), ((), ())), preferred_element_type=F32)


def _dot_tn(a, b):
    return lax.dot_general(a, b, (((0,), (0,)), ((), ())), preferred_element_type=F32)


def _norm_mod(x, g, m):
    ms = jnp.mean(x * x, axis=-1, keepdims=True)
    y = x * lax.rsqrt(ms + EPS) * g
    return y * (1.0 + m[:, D:2 * D]) + m[:, 0:D]


def _mod_kernel(c_ref, w_ref, b_ref, o_ref):
    s = _silu(c_ref[...])
    o_ref[...] = _dot(s.astype(BF16), w_ref[...].astype(BF16)) + b_ref[...]


def _modulation(cvec, w_ada, b_ada):
    n = w_ada.shape[1] // D
    return pl.pallas_call(
        _mod_kernel,
        grid=(n,),
        in_specs=[
            pl.BlockSpec((MOD_ROWS, D), lambda j: (0, 0)),
            pl.BlockSpec((D, D), lambda j: (0, j)),
            pl.BlockSpec((1, D), lambda j: (0, j)),
        ],
        out_specs=pl.BlockSpec((MOD_ROWS, D), lambda j: (0, j)),
        out_shape=jax.ShapeDtypeStruct((MOD_ROWS, n * D), F32),
        compiler_params=_params("arbitrary"),
        name="modulation",
    )(cvec, w_ada, b_ada.reshape(1, n * D))


def _rope(t, cos, sin_hi, sin_lo):
    return t * cos + pltpu.roll(t, LANES - ROPE_FREQS, 1) * sin_hi + pltpu.roll(t, ROPE_FREQS, 1) * sin_lo


def _inproj_attn_kernel(rope, x_ref, m_ref, g_ref, w_ref, *rest):
    if rope:
        cos_ref, shi_ref, slo_ref, qt_ref, k_ref, vt_ref, sg_ref = rest
        cos, shi, slo = cos_ref[...], shi_ref[...], slo_ref[...]
    else:
        qt_ref, k_ref, vt_ref, sg_ref = rest
    hb = _norm_mod(x_ref[0], g_ref[...], m_ref[0]).astype(BF16)
    q = _dot(hb, w_ref[:, 0:A_WIDTH])
    for j in range(A_WIDTH // LANES):
        qj = q[:, j * LANES:(j + 1) * LANES]
        if rope:
            qj = _rope(qj, cos, shi, slo)
        qt_ref[0, j * LANES:(j + 1) * LANES, :] = (qj * (Q_SCALE * LOG2_E)).T.astype(BF16)
    kv = _dot(hb, w_ref[:, A_WIDTH:A_WIDTH + 2 * A_KV_WIDTH])
    k = kv[:, 0:A_KV_WIDTH]
    if rope:
        k = _rope(k, cos, shi, slo)
    k_ref[0] = k.astype(BF16)
    vt_ref[0] = kv[:, A_KV_WIDTH:].T.astype(BF16)
    g = _dot(hb, w_ref[:, A_WIDTH + 2 * A_KV_WIDTH:])
    sg_ref[0] = _silu(g).astype(BF16)


def _inproj_attn(x, mod3, mod_row, norm_g, w, tables):
    b, s, _ = x.shape
    tm = min(ROW_TILE, s)
    rope = tables is not None
    n_w = w.shape[1]
    if mod_row is None:
        mod_map = lambda bi, i: (bi, 0, 0)
    else:
        mod_map = lambda bi, i: (mod_row, 0, 0)
    in_specs = [
        pl.BlockSpec((1, tm, D), lambda bi, i: (bi, i, 0)),
        pl.BlockSpec((1, 1, 3 * D), mod_map),
        pl.BlockSpec((1, D), lambda bi, i: (0, 0)),
        pl.BlockSpec((D, n_w), lambda bi, i: (0, 0)),
    ]
    args = [x, mod3, norm_g.reshape(1, D), w]
    if rope:
        in_specs += [pl.BlockSpec((tm, LANES), lambda bi, i: (i, 0))] * 3
        args += list(tables)
    out_shape = (
        jax.ShapeDtypeStruct((b, A_WIDTH, s), BF16),
        jax.ShapeDtypeStruct((b, s, A_KV_WIDTH), BF16),
        jax.ShapeDtypeStruct((b, A_KV_WIDTH, s), BF16),
        jax.ShapeDtypeStruct((b, s, A_WIDTH), BF16),
    )
    out_specs = (
        pl.BlockSpec((1, A_WIDTH, tm), lambda bi, i: (bi, 0, i)),
        pl.BlockSpec((1, tm, A_KV_WIDTH), lambda bi, i: (bi, i, 0)),
        pl.BlockSpec((1, A_KV_WIDTH, tm), lambda bi, i: (bi, 0, i)),
        pl.BlockSpec((1, tm, A_WIDTH), lambda bi, i: (bi, i, 0)),
    )
    return pl.pallas_call(
        functools.partial(_inproj_attn_kernel, rope),
        grid=(b, s // tm),
        in_specs=in_specs,
        out_specs=out_specs,
        out_shape=out_shape,
        compiler_params=_params("parallel", "parallel"),
        name="inproj_attn_rope" if rope else "inproj_attn_ctx",
    )(*args)


def _attn_kernel(local, n_blocks, sink_ref, qt_ref, *rest):
    if local:
        (kp_ref, kc_ref, kn_ref, vp_ref, vc_ref, vn_ref, kx_ref, vx_ref,
         sg_ref, x_ref, m_ref, w_ref, o_ref) = rest
        kwin = jnp.concatenate([kp_ref[0], kc_ref[0], kn_ref[0], kx_ref[0]], axis=0)
        vtw = jnp.concatenate([vp_ref[0], vc_ref[0], vn_ref[0], vx_ref[0]], axis=1)
    else:
        kx_ref, vx_ref, sg_ref, x_ref, m_ref, w_ref, o_ref = rest
        kwin = kx_ref[0]
        vtw = vx_ref[0]
    n = pl.program_id(1)
    n_keys = kwin.shape[0]
    lane = lax.broadcasted_iota(jnp.int32, kwin.shape, 1)
    kbd = jnp.concatenate([jnp.where(lane < A_HEAD_DIM, kwin, jnp.zeros_like(kwin)),
                           jnp.where(lane >= A_HEAD_DIM, kwin, jnp.zeros_like(kwin))], axis=0)
    vrow = lax.broadcasted_iota(jnp.int32, vtw.shape, 0)
    vbd = jnp.concatenate([jnp.where(vrow < A_HEAD_DIM, vtw, jnp.zeros_like(vtw)),
                           jnp.where(vrow >= A_HEAD_DIM, vtw, jnp.zeros_like(vtw))], axis=1)
    srow = lax.broadcasted_iota(jnp.int32, (SUM_ROWS, 2 * n_keys), 0)
    scol = lax.broadcasted_iota(jnp.int32, (SUM_ROWS, 2 * n_keys), 1) // n_keys
    vbd = jnp.concatenate([vbd, jnp.where(srow == scol, 1.0, 0.0).astype(BF16)], axis=0)
    nq = A_CHUNKS_PER_DOT * BLOCK
    if local:
        kj = lax.broadcasted_iota(jnp.int32, (BLOCK, nq), 0)
        qi = lax.broadcasted_iota(jnp.int32, (BLOCK, nq), 1) % BLOCK
        ok_prev = (kj >= qi) & (n > 0)
        ok_next = (kj <= qi) & (n < n_blocks - 1)
    chunk_of = lax.broadcasted_iota(jnp.int32, (1, nq), 1) // BLOCK
    head_a = lax.broadcasted_iota(jnp.int32, (LANES, nq), 0) < A_HEAD_DIM

    outs = []
    for p in range(0, A_GROUP, A_CHUNKS_PER_DOT):
        qts = jnp.concatenate([qt_ref[0, (p + c) * LANES:(p + c + 1) * LANES, :]
                               for c in range(A_CHUNKS_PER_DOT)], axis=1)
        st = _dot(kbd, qts)
        probs, esink = [], []
        for hh in range(2):
            sh = st[hh * n_keys:(hh + 1) * n_keys]
            if local:
                parts = [jnp.where(ok_prev, sh[0:BLOCK], NEG_INF),
                         sh[BLOCK:2 * BLOCK],
                         jnp.where(ok_next, sh[2 * BLOCK:3 * BLOCK], NEG_INF),
                         sh[3 * BLOCK:]]
            else:
                parts = [sh]
            sink = jnp.full((1, nq), sink_ref[2 * p + hh] * LOG2_E, F32)
            for c in range(1, A_CHUNKS_PER_DOT):
                sink = jnp.where(chunk_of == c, sink_ref[2 * (p + c) + hh] * LOG2_E, sink)
            mx = functools.reduce(jnp.maximum, [jnp.max(t, axis=0, keepdims=True) for t in parts])
            mx = jnp.maximum(mx, sink)
            probs += [jnp.exp2(t - mx).astype(BF16) for t in parts]
            esink.append(jnp.exp2(sink - mx))
        ot = _dot(vbd, jnp.concatenate(probs, axis=0))
        inv = [1.0 / (ot[LANES + hh:LANES + hh + 1] + esink[hh]) for hh in range(2)]
        ot = ot[0:LANES] * jnp.where(head_a, inv[0], inv[1])
        for c in range(A_CHUNKS_PER_DOT):
            sg = sg_ref[0, :, (p + c) * LANES:(p + c + 1) * LANES].astype(F32)
            outs.append((ot[:, c * BLOCK:(c + 1) * BLOCK].T * sg).astype(BF16))
    y = _dot(jnp.concatenate(outs, axis=1), w_ref[...])
    gate = m_ref[0][:, 2 * D:3 * D]
    o_ref[0] = x_ref[0] + gate * y


def _attention(sink, qt, k, vt, kx, vxt, sg, x, mod3, mod_row, w_out, local):
    b, s, _ = sg.shape
    nb = s // BLOCK
    n_ctx = kx.shape[1]
    blk = lambda w: pl.BlockSpec((1, BLOCK, w), lambda bi, i: (bi, i, 0))
    blk_t = lambda w: pl.BlockSpec((1, w, BLOCK), lambda bi, i: (bi, 0, i))
    if mod_row is None:
        mod_map = lambda bi, i: (bi, 0, 0)
    else:
        mod_map = lambda bi, i: (mod_row, 0, 0)
    in_specs = [pl.BlockSpec(memory_space=pltpu.SMEM), blk_t(A_WIDTH)]
    args = [sink, qt]
    if local:
        lo = lambda i: jnp.maximum(i - 1, 0)
        hi = lambda i: jnp.minimum(i + 1, nb - 1)
        in_specs += [pl.BlockSpec((1, BLOCK, A_KV_WIDTH), lambda bi, i: (bi, lo(i), 0)),
                     blk(A_KV_WIDTH),
                     pl.BlockSpec((1, BLOCK, A_KV_WIDTH), lambda bi, i: (bi, hi(i), 0)),
                     pl.BlockSpec((1, A_KV_WIDTH, BLOCK), lambda bi, i: (bi, 0, lo(i))),
                     blk_t(A_KV_WIDTH),
                     pl.BlockSpec((1, A_KV_WIDTH, BLOCK), lambda bi, i: (bi, 0, hi(i)))]
        args += [k, k, k, vt, vt, vt]
    in_specs += [pl.BlockSpec((1, n_ctx, A_KV_WIDTH), lambda bi, i: (bi, 0, 0)),
                 pl.BlockSpec((1, A_KV_WIDTH, n_ctx), lambda bi, i: (bi, 0, 0)),
                 blk(A_WIDTH), blk(D),
                 pl.BlockSpec((1, 1, 3 * D), mod_map),
                 pl.BlockSpec((A_WIDTH, D), lambda bi, i: (0, 0))]
    args += [kx, vxt, sg, x, mod3, w_out]
    return pl.pallas_call(
        functools.partial(_attn_kernel, local, nb),
        grid=(b, nb),
        in_specs=in_specs,
        out_specs=blk(D),
        out_shape=jax.ShapeDtypeStruct((b, s, D), F32),
        compiler_params=_params("parallel", "parallel"),
        name="attn_local" if local else "attn_ctx",
    )(*args)


def _scan_rows(x, reverse):
    c = x.shape[0]
    row = lax.broadcasted_iota(jnp.int32, x.shape, 0)
    s = 1
    while s < c:
        if reverse:
            x = x + jnp.where(row < c - s, pltpu.roll(x, c - s, 0), 0.0)
        else:
            x = x + jnp.where(row >= s, pltpu.roll(x, s, 0), 0.0)
        s *= 2
    return x


def _inproj_gla_kernel(chunk, with_q, x_ref, m_ref, g_ref, w_ref, wa1_ref, wa2_ref, ba_ref, *outs):
    if with_q:
        (qd_f, ki_f, ke_f, dec_f, qd_b, ki_b, ke_b, dec_b, v_ref, sg_ref) = outs
    else:
        ke_f, ke_b, v_ref = outs
    tm = x_ref.shape[1]
    hb = _norm_mod(x_ref[0], g_ref[...], m_ref[0]).astype(BF16)
    r = _dot(hb, wa1_ref[...])
    z = _dot(r.astype(BF16), wa2_ref[...]) + ba_ref[...]
    la = (jnp.minimum(z, 0.0) - jnp.log1p(jnp.exp(-jnp.abs(z)))) / GATE_TEMP
    k = _dot(hb, w_ref[:, B_K_WIDTH:2 * B_K_WIDTH])
    if with_q:
        q = _dot(hb, w_ref[:, 0:B_K_WIDTH]) * K_SCALE
    v_ref[0] = _dot(hb, w_ref[:, 2 * B_K_WIDTH:2 * B_K_WIDTH + B_V_WIDTH]).astype(BF16)
    if with_q:
        sg_ref[0] = _silu(_dot(hb, w_ref[:, 2 * B_K_WIDTH + B_V_WIDTH:])).astype(BF16)
    for c in range(tm // chunk):
        rows = slice(c * chunk, (c + 1) * chunk)
        kc = k[rows]
        for reverse in (False, True):
            lac = la[rows, B_K_WIDTH:] if reverse else la[rows, 0:B_K_WIDTH]
            cum = _scan_rows(lac, reverse)
            tot = cum[0:1] if reverse else cum[chunk - 1:chunk]
            ke = (kc * jnp.exp(tot - cum)).astype(BF16)
            if reverse:
                ke_b[0, rows] = ke
            else:
                ke_f[0, rows] = ke
            if with_q:
                qd = (q[rows] * jnp.exp(cum)).astype(BF16)
                ki = (kc * jnp.exp(-cum)).astype(BF16)
                dec = jnp.exp(tot)
                if reverse:
                    qd_b[0, rows], ki_b[0, rows], dec_b[0, 0, c:c + 1] = qd, ki, dec
                else:
                    qd_f[0, rows], ki_f[0, rows], dec_f[0, 0, c:c + 1] = qd, ki, dec


def _inproj_gla(x, mod3, mod_row, norm_g, w, wa1, wa2, ba, chunk, with_q):
    b, s, _ = x.shape
    tm = min(ROW_TILE, s)
    nt = s // tm
    cpt = tm // chunk
    if mod_row is None:
        mod_map = lambda bi, i: (bi, 0, 0)
    else:
        mod_map = lambda bi, i: (mod_row, 0, 0)
    const = lambda shape: pl.BlockSpec(shape, lambda bi, i: (0,) * len(shape))
    in_specs = [
        pl.BlockSpec((1, tm, D), lambda bi, i: (bi, i, 0)),
        pl.BlockSpec((1, 1, 3 * D), mod_map),
        const((1, D)), const(w.shape), const(wa1.shape), const(wa2.shape), const(ba.shape),
    ]
    rows = lambda w_: (jax.ShapeDtypeStruct((b, s, w_), BF16),
                       pl.BlockSpec((1, tm, w_), lambda bi, i: (bi, i, 0)))
    decs = (jax.ShapeDtypeStruct((b, nt, cpt, B_K_WIDTH), F32),
            pl.BlockSpec((1, 1, cpt, B_K_WIDTH), lambda bi, i: (bi, i, 0, 0)))
    if with_q:
        outs = [rows(B_K_WIDTH)] * 3 + [decs] + [rows(B_K_WIDTH)] * 3 + [decs] + [rows(B_V_WIDTH)] * 2
    else:
        outs = [rows(B_K_WIDTH)] * 2 + [rows(B_V_WIDTH)]
    return pl.pallas_call(
        functools.partial(_inproj_gla_kernel, chunk, with_q),
        grid=(b, nt),
        in_specs=in_specs,
        out_specs=tuple(o[1] for o in outs),
        out_shape=tuple(o[0] for o in outs),
        compiler_params=_params("parallel", "parallel"),
        name="inproj_gla" if with_q else "inproj_gla_ctx",
    )(x, mod3, norm_g.reshape(1, D), w, wa1, wa2, ba)


def _gla_state_kernel(kf_ref, kb_ref, v_ref, sf_ref, sb_ref):
    for h in range(B_HEADS):
        vh = v_ref[0, :, h * B_VAL_DIM:(h + 1) * B_VAL_DIM]
        ks = slice(h * B_KEY_DIM, (h + 1) * B_KEY_DIM)
        sf_ref[0, h] = _dot_tn(vh, kf_ref[0, :, ks])
        sb_ref[0, h] = _dot_tn(vh, kb_ref[0, :, ks])


def _gla_state(ke_f, ke_b, v):
    b, n, _ = v.shape
    st = jax.ShapeDtypeStruct((b, B_HEADS, B_VAL_DIM, B_KEY_DIM), F32)
    st_spec = pl.BlockSpec((1, B_HEADS, B_VAL_DIM, B_KEY_DIM), lambda bi: (bi, 0, 0, 0))
    return pl.pallas_call(
        _gla_state_kernel,
        grid=(b,),
        in_specs=[pl.BlockSpec((1, n, B_K_WIDTH), lambda bi: (bi, 0, 0))] * 2
        + [pl.BlockSpec((1, n, B_V_WIDTH), lambda bi: (bi, 0, 0))],
        out_specs=(st_spec, st_spec),
        out_shape=(st, st),
        compiler_params=_params("parallel"),
        name="gla_ctx_state",
    )(ke_f, ke_b, v)


def _gla_scan_kernel(reverse, qd_ref, ki_ref, ke_ref, dec_ref, v_ref, s0_ref, *rest):
    if reverse:
        (of_ref, sg_ref, x_ref, m_ref, hn_ref, w_ref, fn_ref, o_ref, st_ref) = rest
    else:
        o_ref, st_ref = rest

    @pl.when(pl.program_id(1) == 0)
    def _():
        st_ref[...] = s0_ref[0]

    ti = lax.broadcasted_iota(jnp.int32, (CHUNK, CHUNK), 0)
    si = lax.broadcasted_iota(jnp.int32, (CHUNK, CHUNK), 1)
    causal = (ti <= si) if reverse else (ti >= si)
    n_chunks = SCAN_TILE // CHUNK
    order = range(n_chunks - 1, -1, -1) if reverse else range(n_chunks)
    dec = dec_ref[0, 0]
    o_rows = [None] * n_chunks
    for c in order:
        rows = slice(c * CHUNK, (c + 1) * CHUNK)
        heads = []
        for h in range(B_HEADS):
            ks = slice(h * B_KEY_DIM, (h + 1) * B_KEY_DIM)
            qd = qd_ref[0, rows, ks]
            vh = v_ref[0, rows, h * B_VAL_DIM:(h + 1) * B_VAL_DIM]
            a = jnp.where(causal, _dot_nt(qd, ki_ref[0, rows, ks]), 0.0).astype(BF16)
            st = st_ref[h]
            heads.append(_dot(a, vh) + _dot_nt(qd, st.astype(BF16)))
            st_ref[h] = st * dec[c:c + 1, ks] + _dot_tn(vh, ke_ref[0, rows, ks])
        o_rows[c] = jnp.concatenate(heads, axis=1)
    o = jnp.concatenate(o_rows, axis=0)
    if not reverse:
        o_ref[0] = o
        return
    o = o + of_ref[0]
    normed = []
    for h in range(B_HEADS):
        oh = o[:, h * B_VAL_DIM:(h + 1) * B_VAL_DIM]
        normed.append(oh * lax.rsqrt(jnp.mean(oh * oh, axis=-1, keepdims=True) + EPS))
    of = jnp.concatenate(normed, axis=1) * hn_ref[...]
    y = _dot((of * sg_ref[0].astype(F32)).astype(BF16), w_ref[...])
    xn = x_ref[0] + m_ref[0][:, 2 * D:3 * D] * y
    ms = jnp.mean(xn * xn, axis=-1, keepdims=True)
    o_ref[0] = xn * lax.rsqrt(ms + EPS) * fn_ref[...]


def _gla_scan(reverse, qd, ki, ke, dec, v, s0, extra=None):
    b, s, _ = qd.shape
    nt = s // SCAN_TILE
    if reverse:
        idx = lambda i: nt - 1 - i
    else:
        idx = lambda i: i
    rows = lambda w_: pl.BlockSpec((1, SCAN_TILE, w_), lambda bi, i: (bi, idx(i), 0))
    const = lambda shape: pl.BlockSpec(shape, lambda bi, i: (0,) * len(shape))
    dec = dec.reshape(b, nt, SCAN_TILE // CHUNK, B_K_WIDTH)
    in_specs = [rows(B_K_WIDTH)] * 3 + [
        pl.BlockSpec((1, 1, SCAN_TILE // CHUNK, B_K_WIDTH), lambda bi, i: (bi, idx(i), 0, 0)),
        rows(B_V_WIDTH),
        pl.BlockSpec((1, B_HEADS, B_VAL_DIM, B_KEY_DIM), lambda bi, i: (bi, 0, 0, 0)),
    ]
    args = [qd, ki, ke, dec, v, s0]
    if reverse:
        o_f, sg, x, mod3, head_g, w_out, final_g = extra
        in_specs += [rows(B_V_WIDTH), rows(B_V_WIDTH), rows(D),
                     pl.BlockSpec((1, 1, 3 * D), lambda bi, i: (bi, 0, 0)),
                     const((1, B_V_WIDTH)), const((B_V_WIDTH, D)), const((1, D))]
        args += [o_f, sg, x, mod3, head_g.reshape(1, B_V_WIDTH), w_out, final_g.reshape(1, D)]
    return pl.pallas_call(
        functools.partial(_gla_scan_kernel, reverse),
        grid=(b, nt),
        in_specs=in_specs,
        out_specs=rows(D),
        out_shape=jax.ShapeDtypeStruct((b, s, D), F32),
        scratch_shapes=[pltpu.VMEM((B_HEADS, B_VAL_DIM, B_KEY_DIM), F32)],
        compiler_params=_params("parallel", "arbitrary"),
        name="gla_scan_bwd" if reverse else "gla_scan_fwd",
    )(*args)


def _rope_tables(n_tokens):
    rows_n = n_tokens // GRID_W
    row = jnp.repeat(jnp.arange(rows_n, dtype=F32), GRID_W)
    col = jnp.tile(jnp.arange(GRID_W, dtype=F32), rows_n)
    inv_freq = ROPE_BASE ** (-jnp.arange(ROPE_FREQS, dtype=F32) / ROPE_FREQS)
    ang = jnp.stack([row[:, None] * inv_freq, col[:, None] * inv_freq], axis=1)
    cos, sin = jnp.cos(ang), jnp.sin(ang)
    zero = jnp.zeros_like(sin)
    tile = lambda t: jnp.tile(t.reshape(n_tokens, A_HEAD_DIM), (1, LANES // A_HEAD_DIM))
    return (tile(jnp.stack([cos, cos], axis=2)),
            tile(jnp.stack([-sin, zero], axis=2)),
            tile(jnp.stack([zero, sin], axis=2)))


def _head_pair_perm():
    cols = []
    for p in range(A_GROUP):
        for h in (p, p + A_GROUP):
            cols.extend(range(h * A_HEAD_DIM, (h + 1) * A_HEAD_DIM))
    heads = [h for p in range(A_GROUP) for h in (p, p + A_GROUP)]
    return jnp.asarray(cols, dtype=jnp.int32), jnp.asarray(heads, dtype=jnp.int32)


def kernel(x, c, ctx, c_ctx, l0_norm_g, l0_w_ada, l0_b_ada, l0_w_in, l0_sink, l0_w_out, l1_norm_g, l1_w_ada, l1_b_ada, l1_w_in, l1_wa1_f, l1_wa2_f, l1_ba_f, l1_wa1_b, l1_wa2_b, l1_ba_b, l1_head_norm_g, l1_w_out, final_norm_g):
    b, s, _ = x.shape
    ctx_row = b

    cvec = jnp.concatenate([c, c_ctx[None, :], jnp.zeros((MOD_ROWS - b - 1, D), F32)], axis=0)
    mod0 = _modulation(cvec, l0_w_ada, l0_b_ada).reshape(MOD_ROWS, 1, 3 * D)
    mod1 = _modulation(cvec, l1_w_ada, l1_b_ada).reshape(MOD_ROWS, 1, 3 * D)

    cols, heads = _head_pair_perm()
    q_end, kv_end = A_WIDTH, A_WIDTH + 2 * A_KV_WIDTH
    w0 = jnp.concatenate([l0_w_in[:, :q_end][:, cols], l0_w_in[:, q_end:kv_end],
                          l0_w_in[:, kv_end:][:, cols]], axis=1).astype(BF16)
    w0_out = l0_w_out[cols, :].astype(BF16)
    sink = l0_sink.astype(F32)[heads]
    qt, k, vt, sg = _inproj_attn(x, mod0, None, l0_norm_g, w0, _rope_tables(s))
    qct, kc, vct, sgc = _inproj_attn(ctx, mod0, ctx_row, l0_norm_g, w0, None)
    x1 = _attention(sink, qt, k, vt, kc, vct, sg, x, mod0, None, w0_out, True)
    xc1 = _attention(sink, qct, None, None, kc, vct, sgc, ctx, mod0, ctx_row, w0_out, False)

    w1 = l1_w_in.astype(BF16)
    wa1 = jnp.concatenate([l1_wa1_f, l1_wa1_b, jnp.zeros((D, LANES - 2 * GATE_RANK), F32)], axis=1).astype(BF16)
    wa2 = jnp.zeros((LANES, 2 * B_K_WIDTH), F32)
    wa2 = wa2.at[0:GATE_RANK, 0:B_K_WIDTH].set(l1_wa2_f)
    wa2 = wa2.at[GATE_RANK:2 * GATE_RANK, B_K_WIDTH:].set(l1_wa2_b).astype(BF16)
    ba = jnp.concatenate([l1_ba_f, l1_ba_b]).reshape(1, 2 * B_K_WIDTH)
    kec_f, kec_b, vcx = _inproj_gla(xc1, mod1, ctx_row, l1_norm_g, w1, wa1, wa2, ba, ctx.shape[1], False)
    s_f, s_b = _gla_state(kec_f, kec_b, vcx)
    qd_f, ki_f, ke_f, dec_f, qd_b, ki_b, ke_b, dec_b, v1, sg1 = _inproj_gla(
        x1, mod1, None, l1_norm_g, w1, wa1, wa2, ba, CHUNK, True)
    o_f = _gla_scan(False, qd_f, ki_f, ke_f, dec_f, v1, s_f)
    return _gla_scan(True, qd_b, ki_b, ke_b, dec_b, v1, s_b,
                     (o_f, sg1, x1, mod1, l1_head_norm_g, l1_w_out.astype(BF16), final_norm_g))
```

```python
import functools

import jax
import jax.numpy as jnp
from jax import lax
from jax.experimental import pallas as pl
from jax.experimental.pallas import tpu as pltpu

F32 = jnp.float32
BF16 = jnp.bfloat16

D = 1024
GRID_W = 64
EPS = 1e-6
NEG_INF = -1e30

A_HEADS = 16
A_KV_HEADS = 2
A_GROUP = A_HEADS // A_KV_HEADS
A_HEAD_DIM = 64
A_WIDTH = A_HEADS * A_HEAD_DIM
A_KV_WIDTH = A_KV_HEADS * A_HEAD_DIM
BLOCK = 128
ROPE_BASE = 10000.0
ROPE_FREQS = A_HEAD_DIM // 4
Q_SCALE = A_HEAD_DIM ** -0.5
LOG2_E = 1.4426950408889634

B_HEADS = 4
B_K_WIDTH = D // 2
B_V_WIDTH = D
B_KEY_DIM = B_K_WIDTH // B_HEADS
B_VAL_DIM = B_V_WIDTH // B_HEADS
GATE_RANK = 16
GATE_TEMP = 16.0
CHUNK = 64
K_SCALE = B_KEY_DIM ** -0.5

LANES = 128
MOD_ROWS = 16
ROW_TILE = 256
SCAN_TILE = 256
DEC_ROWS = 8
SUM_ROWS = 16
A_CHUNKS_PER_DOT = 8
VMEM_LIMIT = 48 * 1024 * 1024


def _params(*sem):
    return pltpu.CompilerParams(dimension_semantics=sem, vmem_limit_bytes=VMEM_LIMIT)


def _silu(x):
    return x / (1.0 + jnp.exp(-x))


def _dot(a, b):
    return jnp.dot(a, b, preferred_element_type=F32)


def _dot_nt(a, b):
    return lax.dot_general(a, b, (((1,), (1,)), ((), ())), preferred_element_type=F32)


def _dot_tn(a, b):
    return lax.dot_general(a, b, (((0,), (0,)), ((), ())), preferred_element_type=F32)


def _norm_mod(x, g, m):
    ms = jnp.mean(x * x, axis=-1, keepdims=True)
    y = x * lax.rsqrt(ms + EPS) * g
    return y * (1.0 + m[:, D:2 * D]) + m[:, 0:D]


def _mod_kernel(c_ref, w_ref, b_ref, o_ref):
    s = _silu(c_ref[...])
    o_ref[...] = _dot(s.astype(BF16), w_ref[...].astype(BF16)) + b_ref[...]


def _modulation(cvec, w_ada, b_ada):
    n = w_ada.shape[1] // D
    return pl.pallas_call(
        _mod_kernel,
        grid=(n,),
        in_specs=[
            pl.BlockSpec((MOD_ROWS, D), lambda j: (0, 0)),
            pl.BlockSpec((D, D), lambda j: (0, j)),
            pl.BlockSpec((1, D), lambda j: (0, j)),
        ],
        out_specs=pl.BlockSpec((MOD_ROWS, D), lambda j: (0, j)),
        out_shape=jax.ShapeDtypeStruct((MOD_ROWS, n * D), F32),
        compiler_params=_params("arbitrary"),
        name="modulation",
    )(cvec, w_ada, b_ada.reshape(1, n * D))


def _rope(t, cos, sin_hi, sin_lo):
    return t * cos + pltpu.roll(t, LANES - ROPE_FREQS, 1) * sin_hi + pltpu.roll(t, ROPE_FREQS, 1) * sin_lo


def _inproj_attn_kernel(rope, x_ref, m_ref, g_ref, w_ref, *rest):
    if rope:
        cos_ref, shi_ref, slo_ref, qt_ref, k_ref, vt_ref, sg_ref = rest
        cos, shi, slo = cos_ref[...], shi_ref[...], slo_ref[...]
    else:
        qt_ref, k_ref, vt_ref, sg_ref = rest
    hb = _norm_mod(x_ref[0], g_ref[...], m_ref[0]).astype(BF16)
    q = _dot(hb, w_ref[:, 0:A_WIDTH])
    for j in range(A_WIDTH // LANES):
        qj = q[:, j * LANES:(j + 1) * LANES]
        if rope:
            qj = _rope(qj, cos, shi, slo)
        qt_ref[0, j * LANES:(j + 1) * LANES, :] = (qj * (Q_SCALE * LOG2_E)).T.astype(BF16)
    kv = _dot(hb, w_ref[:, A_WIDTH:A_WIDTH + 2 * A_KV_WIDTH])
    k = kv[:, 0:A_KV_WIDTH]
    if rope:
        k = _rope(k, cos, shi, slo)
    k_ref[0] = k.astype(BF16)
    vt_ref[0] = kv[:, A_KV_WIDTH:].T.astype(BF16)
    g = _dot(hb, w_ref[:, A_WIDTH + 2 * A_KV_WIDTH:])
    sg_ref[0] = _silu(g).astype(BF16)


def _inproj_attn(x, mod3, mod_row, norm_g, w, tables):
    b, s, _ = x.shape
    tm = min(ROW_TILE, s)
    rope = tables is not None
    n_w = w.shape[1]
    if mod_row is None:
        mod_map = lambda bi, i: (bi, 0, 0)
    else:
        mod_map = lambda bi, i: (mod_row, 0, 0)
    in_specs = [
        pl.BlockSpec((1, tm, D), lambda bi, i: (bi, i, 0)),
        pl.BlockSpec((1, 1, 3 * D), mod_map),
        pl.BlockSpec((1, D), lambda bi, i: (0, 0)),
        pl.BlockSpec((D, n_w), lambda bi, i: (0, 0)),
    ]
    args = [x, mod3, norm_g.reshape(1, D), w]
    if rope:
        in_specs += [pl.BlockSpec((tm, LANES), lambda bi, i: (i, 0))] * 3
        args += list(tables)
    out_shape = (
        jax.ShapeDtypeStruct((b, A_WIDTH, s), BF16),
        jax.ShapeDtypeStruct((b, s, A_KV_WIDTH), BF16),
        jax.ShapeDtypeStruct((b, A_KV_WIDTH, s), BF16),
        jax.ShapeDtypeStruct((b, s, A_WIDTH), BF16),
    )
    out_specs = (
        pl.BlockSpec((1, A_WIDTH, tm), lambda bi, i: (bi, 0, i)),
        pl.BlockSpec((1, tm, A_KV_WIDTH), lambda bi, i: (bi, i, 0)),
        pl.BlockSpec((1, A_KV_WIDTH, tm), lambda bi, i: (bi, 0, i)),
        pl.BlockSpec((1, tm, A_WIDTH), lambda bi, i: (bi, i, 0)),
    )
    return pl.pallas_call(
        functools.partial(_inproj_attn_kernel, rope),
        grid=(b, s // tm),
        in_specs=in_specs,
        out_specs=out_specs,
        out_shape=out_shape,
        compiler_params=_params("parallel", "parallel"),
        name="inproj_attn_rope" if rope else "inproj_attn_ctx",
    )(*args)


def _attn_kernel(local, n_blocks, sink_ref, qt_ref, *rest):
    if local:
        (kp_ref, kc_ref, kn_ref, vp_ref, vc_ref, vn_ref, kx_ref, vx_ref,
         sg_ref, x_ref, m_ref, w_ref, o_ref) = rest
        kwin = jnp.concatenate([kp_ref[0], kc_ref[0], kn_ref[0], kx_ref[0]], axis=0)
        vtw = jnp.concatenate([vp_ref[0], vc_ref[0], vn_ref[0], vx_ref[0]], axis=1)
    else:
        kx_ref, vx_ref, sg_ref, x_ref, m_ref, w_ref, o_ref = rest
        kwin = kx_ref[0]
        vtw = vx_ref[0]
    n = pl.program_id(1)
    n_keys = kwin.shape[0]
    lane = lax.broadcasted_iota(jnp.int32, kwin.shape, 1)
    kbd = jnp.concatenate([jnp.where(lane < A_HEAD_DIM, kwin, jnp.zeros_like(kwin)),
                           jnp.where(lane >= A_HEAD_DIM, kwin, jnp.zeros_like(kwin))], axis=0)
    vrow = lax.broadcasted_iota(jnp.int32, vtw.shape, 0)
    vbd = jnp.concatenate([jnp.where(vrow < A_HEAD_DIM, vtw, jnp.zeros_like(vtw)),
                           jnp.where(vrow >= A_HEAD_DIM, vtw, jnp.zeros_like(vtw))], axis=1)
    srow = lax.broadcasted_iota(jnp.int32, (SUM_ROWS, 2 * n_keys), 0)
    scol = lax.broadcasted_iota(jnp.int32, (SUM_ROWS, 2 * n_keys), 1) // n_keys
    vbd = jnp.concatenate([vbd, jnp.where(srow == scol, 1.0, 0.0).astype(BF16)], axis=0)
    nq = A_CHUNKS_PER_DOT * BLOCK
    if local:
        kj = lax.broadcasted_iota(jnp.int32, (BLOCK, nq), 0)
        qi = lax.broadcasted_iota(jnp.int32, (BLOCK, nq), 1) % BLOCK
        ok_prev = (kj >= qi) & (n > 0)
        ok_next = (kj <= qi) & (n < n_blocks - 1)
    chunk_of = lax.broadcasted_iota(jnp.int32, (1, nq), 1) // BLOCK
    head_a = lax.broadcasted_iota(jnp.int32, (LANES, nq), 0) < A_HEAD_DIM

    outs = []
    for p in range(0, A_GROUP, A_CHUNKS_PER_DOT):
        qts = jnp.concatenate([qt_ref[0, (p + c) * LANES:(p + c + 1) * LANES, :]
                               for c in range(A_CHUNKS_PER_DOT)], axis=1)
        st = _dot(kbd, qts)
        probs, esink = [], []
        for hh in range(2):
            sh = st[hh * n_keys:(hh + 1) * n_keys]
            if local:
                parts = [jnp.where(ok_prev, sh[0:BLOCK], NEG_INF),
                         sh[BLOCK:2 * BLOCK],
                         jnp.where(ok_next, sh[2 * BLOCK:3 * BLOCK], NEG_INF),
                         sh[3 * BLOCK:]]
            else:
                parts = [sh]
            sink = jnp.full((1, nq), sink_ref[2 * p + hh] * LOG2_E, F32)
            for c in range(1, A_CHUNKS_PER_DOT):
                sink = jnp.where(chunk_of == c, sink_ref[2 * (p + c) + hh] * LOG2_E, sink)
            mx = functools.reduce(jnp.maximum, [jnp.max(t, axis=0, keepdims=True) for t in parts])
            mx = jnp.maximum(mx, sink)
            probs += [jnp.exp2(t - mx).astype(BF16) for t in parts]
            esink.append(jnp.exp2(sink - mx))
        ot = _dot(vbd, jnp.concatenate(probs, axis=0))
        inv = [1.0 / (ot[LANES + hh:LANES + hh + 1] + esink[hh]) for hh in range(2)]
        ot = ot[0:LANES] * jnp.where(head_a, inv[0], inv[1])
        for c in range(A_CHUNKS_PER_DOT):
            sg = sg_ref[0, :, (p + c) * LANES:(p + c + 1) * LANES].astype(F32)
            outs.append((ot[:, c * BLOCK:(c + 1) * BLOCK].T * sg).astype(BF16))
    y = _dot(jnp.concatenate(outs, axis=1), w_ref[...])
    gate = m_ref[0][:, 2 * D:3 * D]
    o_ref[0] = x_ref[0] + gate * y


def _attention(sink, qt, k, vt, kx, vxt, sg, x, mod3, mod_row, w_out, local):
    b, s, _ = sg.shape
    nb = s // BLOCK
    n_ctx = kx.shape[1]
    blk = lambda w: pl.BlockSpec((1, BLOCK, w), lambda bi, i: (bi, i, 0))
    blk_t = lambda w: pl.BlockSpec((1, w, BLOCK), lambda bi, i: (bi, 0, i))
    if mod_row is None:
        mod_map = lambda bi, i: (bi, 0, 0)
    else:
        mod_map = lambda bi, i: (mod_row, 0, 0)
    in_specs = [pl.BlockSpec(memory_space=pltpu.SMEM), blk_t(A_WIDTH)]
    args = [sink, qt]
    if local:
        lo = lambda i: jnp.maximum(i - 1, 0)
        hi = lambda i: jnp.minimum(i + 1, nb - 1)
        in_specs += [pl.BlockSpec((1, BLOCK, A_KV_WIDTH), lambda bi, i: (bi, lo(i), 0)),
                     blk(A_KV_WIDTH),
                     pl.BlockSpec((1, BLOCK, A_KV_WIDTH), lambda bi, i: (bi, hi(i), 0)),
                     pl.BlockSpec((1, A_KV_WIDTH, BLOCK), lambda bi, i: (bi, 0, lo(i))),
                     blk_t(A_KV_WIDTH),
                     pl.BlockSpec((1, A_KV_WIDTH, BLOCK), lambda bi, i: (bi, 0, hi(i)))]
        args += [k, k, k, vt, vt, vt]
    in_specs += [pl.BlockSpec((1, n_ctx, A_KV_WIDTH), lambda bi, i: (bi, 0, 0)),
                 pl.BlockSpec((1, A_KV_WIDTH, n_ctx), lambda bi, i: (bi, 0, 0)),
                 blk(A_WIDTH), blk(D),
                 pl.BlockSpec((1, 1, 3 * D), mod_map),
                 pl.BlockSpec((A_WIDTH, D), lambda bi, i: (0, 0))]
    args += [kx, vxt, sg, x, mod3, w_out]
    return pl.pallas_call(
        functools.partial(_attn_kernel, local, nb),
        grid=(b, nb),
        in_specs=in_specs,
        out_specs=blk(D),
        out_shape=jax.ShapeDtypeStruct((b, s, D), F32),
        compiler_params=_params("parallel", "parallel"),
        name="attn_local" if local else "attn_ctx",
    )(*args)


def _chunk_cumsum(x, tri):
    hi = x.astype(BF16)
    lo = (x - hi.astype(F32)).astype(BF16)
    return _dot(tri, hi) + _dot(tri, lo)


def _inproj_gla_kernel(chunk, with_q, x_ref, m_ref, g_ref, w_ref, wa1_ref, wa2_ref, ba_ref, tri_ref, *outs):
    if with_q:
        per_dir = (outs[0:5], outs[5:10])
        v_ref, sg_ref = outs[10:]
    else:
        per_dir = ((outs[0],), (outs[1],))
        v_ref = outs[2]
    tm = x_ref.shape[1]
    hb = _norm_mod(x_ref[0], g_ref[...], m_ref[0]).astype(BF16)
    r = _dot(hb, wa1_ref[...])
    z = _dot(r.astype(BF16), wa2_ref[...]) + ba_ref[...]
    e = jnp.exp2(jnp.abs(z) * -LOG2_E)
    la = (jnp.minimum(z, 0.0) * LOG2_E - jnp.log2(1.0 + e)) * (1.0 / GATE_TEMP)
    k = _dot(hb, w_ref[:, B_K_WIDTH:2 * B_K_WIDTH])
    if with_q:
        q = _dot(hb, w_ref[:, 0:B_K_WIDTH]) * K_SCALE
    n_chunks = tm // chunk
    for reverse in (False, True):
        lad = la[:, B_K_WIDTH:] if reverse else la[:, 0:B_K_WIDTH]
        refs = per_dir[1] if reverse else per_dir[0]
        tots, kis, kes = [], [], []
        cum_all = _chunk_cumsum(lad, tri_ref[1 if reverse else 0])
        for c in range(n_chunks):
            rows = slice(c * chunk, (c + 1) * chunk)
            cum = cum_all[rows]
            tot = cum[0:1] if reverse else cum[chunk - 1:chunk]
            ke = k[rows] * jnp.exp2(tot - cum)
            if not with_q:
                refs[0][0, rows] = ke.astype(BF16)
                continue
            refs[0][0, rows] = (q[rows] * jnp.exp2(cum)).astype(BF16)
            kis.append(k[rows] * jnp.exp2(-cum))
            kes.append(ke)
            tots.append(tot)
            if c % 2 == 1:
                pair = slice((c - 1) * chunk, (c + 1) * chunk)
                refs[1][0, :, pair] = jnp.concatenate(kis[-2:], axis=0).T.astype(BF16)
                refs[2][0, :, pair] = jnp.concatenate(kes[-2:], axis=0).T.astype(BF16)
        if with_q:
            pad = jnp.zeros((LANES - n_chunks, B_K_WIDTH), F32)
            dec = jnp.exp2(jnp.concatenate(tots + [pad], axis=0))
            refs[3][0, 0] = dec[0:DEC_ROWS]
            refs[4][0, 0] = dec.T
    if with_q:
        sg_ref[0] = _silu(_dot(hb, w_ref[:, 2 * B_K_WIDTH + B_V_WIDTH:])).astype(BF16)
    v_ref[0] = _dot(hb, w_ref[:, 2 * B_K_WIDTH:2 * B_K_WIDTH + B_V_WIDTH]).astype(BF16)


def _inproj_gla(x, mod3, mod_row, norm_g, w, wa1, wa2, ba, chunk, with_q):
    b, s, _ = x.shape
    tm = min(ROW_TILE, s)
    nt = s // tm
    cpt = tm // chunk
    if mod_row is None:
        mod_map = lambda bi, i: (bi, 0, 0)
    else:
        mod_map = lambda bi, i: (mod_row, 0, 0)
    const = lambda shape: pl.BlockSpec(shape, lambda bi, i: (0,) * len(shape))
    in_specs = [
        pl.BlockSpec((1, tm, D), lambda bi, i: (bi, i, 0)),
        pl.BlockSpec((1, 1, 3 * D), mod_map),
        const((1, D)), const(w.shape), const(wa1.shape), const(wa2.shape), const(ba.shape),
        const((2, tm, tm)),
    ]
    t_idx = jnp.arange(tm)
    same = (t_idx[:, None] // chunk) == (t_idx[None, :] // chunk)
    lower = same & (t_idx[None, :] <= t_idx[:, None])
    tri = jnp.stack([lower, lower.T]).astype(BF16)
    rows = lambda w_: (jax.ShapeDtypeStruct((b, s, w_), BF16),
                       pl.BlockSpec((1, tm, w_), lambda bi, i: (bi, i, 0)))
    cols = (jax.ShapeDtypeStruct((b, B_K_WIDTH, s), BF16),
            pl.BlockSpec((1, B_K_WIDTH, tm), lambda bi, i: (bi, 0, i)))
    decs = (jax.ShapeDtypeStruct((b, nt, DEC_ROWS, B_K_WIDTH), F32),
            pl.BlockSpec((1, 1, DEC_ROWS, B_K_WIDTH), lambda bi, i: (bi, i, 0, 0)))
    dect = (jax.ShapeDtypeStruct((b, nt, B_K_WIDTH, LANES), F32),
            pl.BlockSpec((1, 1, B_K_WIDTH, LANES), lambda bi, i: (bi, i, 0, 0)))
    if with_q:
        assert cpt <= DEC_ROWS and cpt % 2 == 0
        outs = [rows(B_K_WIDTH), cols, cols, decs, dect] * 2 + [rows(B_V_WIDTH)] * 2
    else:
        outs = [rows(B_K_WIDTH)] * 2 + [rows(B_V_WIDTH)]
    return pl.pallas_call(
        functools.partial(_inproj_gla_kernel, chunk, with_q),
        grid=(b, nt),
        in_specs=in_specs,
        out_specs=tuple(o[1] for o in outs),
        out_shape=tuple(o[0] for o in outs),
        compiler_params=_params("parallel", "parallel"),
        name="inproj_gla" if with_q else "inproj_gla_ctx",
    )(x, mod3, norm_g.reshape(1, D), w, wa1, wa2, ba, tri)


def _gla_state_kernel(kf_ref, kb_ref, v_ref, sf_ref, sb_ref):
    for h in range(B_HEADS):
        vh = v_ref[0, :, h * B_VAL_DIM:(h + 1) * B_VAL_DIM]
        ks = slice(h * B_KEY_DIM, (h + 1) * B_KEY_DIM)
        sf_ref[0, h] = _dot_tn(kf_ref[0, :, ks], vh)
        sb_ref[0, h] = _dot_tn(kb_ref[0, :, ks], vh)


def _gla_state(ke_f, ke_b, v):
    b, n, _ = v.shape
    st = jax.ShapeDtypeStruct((b, B_HEADS, B_KEY_DIM, B_VAL_DIM), F32)
    st_spec = pl.BlockSpec((1, B_HEADS, B_KEY_DIM, B_VAL_DIM), lambda bi: (bi, 0, 0, 0))
    return pl.pallas_call(
        _gla_state_kernel,
        grid=(b,),
        in_specs=[pl.BlockSpec((1, n, B_K_WIDTH), lambda bi: (bi, 0, 0))] * 2
        + [pl.BlockSpec((1, n, B_V_WIDTH), lambda bi: (bi, 0, 0))],
        out_specs=(st_spec, st_spec),
        out_shape=(st, st),
        compiler_params=_params("parallel"),
        name="gla_ctx_state",
    )(ke_f, ke_b, v)


def _gla_scan_kernel(reverse, qd_ref, kit_ref, ket_ref, dec_ref, dect_ref, v_ref, s0_ref, *rest):
    if reverse:
        (of_ref, sg_ref, x_ref, m_ref, hn_ref, w_ref, fn_ref, o_ref, st_ref) = rest
    else:
        o_ref, st_ref = rest

    @pl.when(pl.program_id(1) == 0)
    def _():
        st_ref[...] = s0_ref[0]

    pair_rows = 2 * CHUNK
    ti = lax.broadcasted_iota(jnp.int32, (pair_rows, pair_rows), 0)
    si = lax.broadcasted_iota(jnp.int32, (pair_rows, pair_rows), 1)
    same = (ti // CHUNK) == (si // CHUNK)
    if reverse:
        use_inv = same & (ti <= si)
        use_end = (ti < CHUNK) & (si >= CHUNK)
        second_row = lax.broadcasted_iota(jnp.int32, (pair_rows, B_KEY_DIM), 0) < CHUNK
        first_col = lax.broadcasted_iota(jnp.int32, (B_KEY_DIM, pair_rows), 1) >= CHUNK
    else:
        use_inv = same & (ti >= si)
        use_end = (ti >= CHUNK) & (si < CHUNK)
        second_row = lax.broadcasted_iota(jnp.int32, (pair_rows, B_KEY_DIM), 0) >= CHUNK
        first_col = lax.broadcasted_iota(jnp.int32, (B_KEY_DIM, pair_rows), 1) < CHUNK
    n_pairs = SCAN_TILE // pair_rows
    order = range(n_pairs - 1, -1, -1) if reverse else range(n_pairs)
    dec = dec_ref[0, 0]
    dect = dect_ref[0, 0]
    heads = []
    for h in range(B_HEADS):
        ks = slice(h * B_KEY_DIM, (h + 1) * B_KEY_DIM)
        vs = slice(h * B_VAL_DIM, (h + 1) * B_VAL_DIM)
        st = st_ref[h]
        o_rows = [None] * n_pairs
        for p in order:
            rows = slice(p * pair_rows, (p + 1) * pair_rows)
            c_first, c_second = (2 * p + 1, 2 * p) if reverse else (2 * p, 2 * p + 1)
            qd = qd_ref[0, rows, ks]
            ket = ket_ref[0, ks, rows]
            vh = v_ref[0, rows, vs]
            a = jnp.where(use_inv, _dot(qd, kit_ref[0, ks, rows]),
                          jnp.where(use_end, _dot(qd, ket), 0.0)).astype(BF16)
            q_pair = jnp.where(second_row, qd.astype(F32) * dec[c_first:c_first + 1, ks], qd.astype(F32))
            k_pair = jnp.where(first_col, ket.astype(F32) * dect[ks, c_second:c_second + 1], ket.astype(F32))
            o_rows[p] = _dot(a, vh) + _dot(q_pair.astype(BF16), st.astype(BF16))
            st = (st * (dect[ks, c_first:c_first + 1] * dect[ks, c_second:c_second + 1])
                  + _dot(k_pair.astype(BF16), vh))
        st_ref[h] = st
        heads.append(jnp.concatenate(o_rows, axis=0))
    o = jnp.concatenate(heads, axis=1)
    if not reverse:
        o_ref[0] = o
        return
    o = o + of_ref[0]
    normed = []
    for h in range(B_HEADS):
        oh = o[:, h * B_VAL_DIM:(h + 1) * B_VAL_DIM]
        normed.append(oh * lax.rsqrt(jnp.mean(oh * oh, axis=-1, keepdims=True) + EPS))
    of = jnp.concatenate(normed, axis=1) * hn_ref[...]
    y = _dot((of * sg_ref[0].astype(F32)).astype(BF16), w_ref[...])
    xn = x_ref[0] + m_ref[0][:, 2 * D:3 * D] * y
    ms = jnp.mean(xn * xn, axis=-1, keepdims=True)
    o_ref[0] = xn * lax.rsqrt(ms + EPS) * fn_ref[...]


def _gla_scan(reverse, qd, kit, ket, dec, dect, v, s0, extra=None):
    b, s, _ = qd.shape
    assert SCAN_TILE == ROW_TILE
    nt = s // SCAN_TILE
    if reverse:
        idx = lambda i: nt - 1 - i
    else:
        idx = lambda i: i
    rows = lambda w_: pl.BlockSpec((1, SCAN_TILE, w_), lambda bi, i: (bi, idx(i), 0))
    cols = pl.BlockSpec((1, B_K_WIDTH, SCAN_TILE), lambda bi, i: (bi, 0, idx(i)))
    const = lambda shape: pl.BlockSpec(shape, lambda bi, i: (0,) * len(shape))
    in_specs = [rows(B_K_WIDTH), cols, cols,
                pl.BlockSpec((1, 1, DEC_ROWS, B_K_WIDTH), lambda bi, i: (bi, idx(i), 0, 0)),
                pl.BlockSpec((1, 1, B_K_WIDTH, LANES), lambda bi, i: (bi, idx(i), 0, 0)),
                rows(B_V_WIDTH),
                pl.BlockSpec((1, B_HEADS, B_KEY_DIM, B_VAL_DIM), lambda bi, i: (bi, 0, 0, 0))]
    args = [qd, kit, ket, dec, dect, v, s0]
    if reverse:
        o_f, sg, x, mod3, head_g, w_out, final_g = extra
        in_specs += [rows(B_V_WIDTH), rows(B_V_WIDTH), rows(D),
                     pl.BlockSpec((1, 1, 3 * D), lambda bi, i: (bi, 0, 0)),
                     const((1, B_V_WIDTH)), const((B_V_WIDTH, D)), const((1, D))]
        args += [o_f, sg, x, mod3, head_g.reshape(1, B_V_WIDTH), w_out, final_g.reshape(1, D)]
    return pl.pallas_call(
        functools.partial(_gla_scan_kernel, reverse),
        grid=(b, nt),
        in_specs=in_specs,
        out_specs=rows(D),
        out_shape=jax.ShapeDtypeStruct((b, s, D), F32),
        scratch_shapes=[pltpu.VMEM((B_HEADS, B_KEY_DIM, B_VAL_DIM), F32)],
        compiler_params=_params("parallel", "arbitrary"),
        name="gla_scan_bwd" if reverse else "gla_scan_fwd",
    )(*args)


def _rope_tables(n_tokens):
    rows_n = n_tokens // GRID_W
    row = jnp.repeat(jnp.arange(rows_n, dtype=F32), GRID_W)
    col = jnp.tile(jnp.arange(GRID_W, dtype=F32), rows_n)
    inv_freq = ROPE_BASE ** (-jnp.arange(ROPE_FREQS, dtype=F32) / ROPE_FREQS)
    ang = jnp.stack([row[:, None] * inv_freq, col[:, None] * inv_freq], axis=1)
    cos, sin = jnp.cos(ang), jnp.sin(ang)
    zero = jnp.zeros_like(sin)
    tile = lambda t: jnp.tile(t.reshape(n_tokens, A_HEAD_DIM), (1, LANES // A_HEAD_DIM))
    return (tile(jnp.stack([cos, cos], axis=2)),
            tile(jnp.stack([-sin, zero], axis=2)),
            tile(jnp.stack([zero, sin], axis=2)))


def _head_pair_perm():
    cols = []
    for p in range(A_GROUP):
        for h in (p, p + A_GROUP):
            cols.extend(range(h * A_HEAD_DIM, (h + 1) * A_HEAD_DIM))
    heads = [h for p in range(A_GROUP) for h in (p, p + A_GROUP)]
    return jnp.asarray(cols, dtype=jnp.int32), jnp.asarray(heads, dtype=jnp.int32)


def kernel(x, c, ctx, c_ctx, l0_norm_g, l0_w_ada, l0_b_ada, l0_w_in, l0_sink, l0_w_out, l1_norm_g, l1_w_ada, l1_b_ada, l1_w_in, l1_wa1_f, l1_wa2_f, l1_ba_f, l1_wa1_b, l1_wa2_b, l1_ba_b, l1_head_norm_g, l1_w_out, final_norm_g):
    b, s, _ = x.shape
    ctx_row = b

    cvec = jnp.concatenate([c, c_ctx[None, :], jnp.zeros((MOD_ROWS - b - 1, D), F32)], axis=0)
    mod0 = _modulation(cvec, l0_w_ada, l0_b_ada).reshape(MOD_ROWS, 1, 3 * D)
    mod1 = _modulation(cvec, l1_w_ada, l1_b_ada).reshape(MOD_ROWS, 1, 3 * D)

    cols, heads = _head_pair_perm()
    q_end, kv_end = A_WIDTH, A_WIDTH + 2 * A_KV_WIDTH
    w0 = jnp.concatenate([l0_w_in[:, :q_end][:, cols], l0_w_in[:, q_end:kv_end],
                          l0_w_in[:, kv_end:][:, cols]], axis=1).astype(BF16)
    w0_out = l0_w_out[cols, :].astype(BF16)
    sink = l0_sink.astype(F32)[heads]
    qt, k, vt, sg = _inproj_attn(x, mod0, None, l0_norm_g, w0, _rope_tables(s))
    qct, kc, vct, sgc = _inproj_attn(ctx, mod0, ctx_row, l0_norm_g, w0, None)
    x1 = _attention(sink, qt, k, vt, kc, vct, sg, x, mod0, None, w0_out, True)
    xc1 = _attention(sink, qct, None, None, kc, vct, sgc, ctx, mod0, ctx_row, w0_out, False)

    w1 = l1_w_in.astype(BF16)
    wa1 = jnp.concatenate([l1_wa1_f, l1_wa1_b, jnp.zeros((D, LANES - 2 * GATE_RANK), F32)], axis=1).astype(BF16)
    wa2 = jnp.zeros((LANES, 2 * B_K_WIDTH), F32)
    wa2 = wa2.at[0:GATE_RANK, 0:B_K_WIDTH].set(l1_wa2_f)
    wa2 = wa2.at[GATE_RANK:2 * GATE_RANK, B_K_WIDTH:].set(l1_wa2_b).astype(BF16)
    ba = jnp.concatenate([l1_ba_f, l1_ba_b]).reshape(1, 2 * B_K_WIDTH)
    kec_f, kec_b, vcx = _inproj_gla(xc1, mod1, ctx_row, l1_norm_g, w1, wa1, wa2, ba, ctx.shape[1], False)
    s_f, s_b = _gla_state(kec_f, kec_b, vcx)
    outs = _inproj_gla(x1, mod1, None, l1_norm_g, w1, wa1, wa2, ba, CHUNK, True)
    v1, sg1 = outs[10:]
    o_f = _gla_scan(False, *outs[0:5], v1, s_f)
    return _gla_scan(True, *outs[5:10], v1, s_b,
                     (o_f, sg1, x1, mod1, l1_head_norm_g, l1_w_out.astype(BF16), final_norm_g))
```

```python
import functools

import jax
import jax.numpy as jnp
from jax import lax
from jax.experimental import pallas as pl
from jax.experimental.pallas import tpu as pltpu

F32 = jnp.float32
BF16 = jnp.bfloat16

D = 1024
GRID_W = 64
EPS = 1e-6
NEG_INF = -1e30

A_HEADS = 16
A_KV_HEADS = 2
A_GROUP = A_HEADS // A_KV_HEADS
A_HEAD_DIM = 64
A_WIDTH = A_HEADS * A_HEAD_DIM
A_KV_WIDTH = A_KV_HEADS * A_HEAD_DIM
BLOCK = 128
ROPE_BASE = 10000.0
ROPE_FREQS = A_HEAD_DIM // 4
Q_SCALE = A_HEAD_DIM ** -0.5
LOG2_E = 1.4426950408889634

B_HEADS = 4
B_K_WIDTH = D // 2
B_V_WIDTH = D
B_KEY_DIM = B_K_WIDTH // B_HEADS
B_VAL_DIM = B_V_WIDTH // B_HEADS
GATE_RANK = 16
GATE_TEMP = 16.0
CHUNK = 64
K_SCALE = B_KEY_DIM ** -0.5

LANES = 128
MOD_ROWS = 16
ROW_TILE = 256
SCAN_TILE = 256
DEC_ROWS = 8
SUM_ROWS = 16
A_CHUNKS_PER_DOT = 8
VMEM_LIMIT = 48 * 1024 * 1024


def _params(*sem):
    return pltpu.CompilerParams(dimension_semantics=sem, vmem_limit_bytes=VMEM_LIMIT)


def _silu(x):
    return x / (1.0 + jnp.exp(-x))


def _dot(a, b):
    return jnp.dot(a, b, preferred_element_type=F32)


def _dot_nt(a, b):
    return lax.dot_general(a, b, (((1,), (1,)), ((), ())), preferred_element_type=F32)


def _dot_tn(a, b):
    return lax.dot_general(a, b, (((0,), (0,)), ((), ())), preferred_element_type=F32)


def _norm_mod(x, g, m):
    ms = jnp.mean(x * x, axis=-1, keepdims=True)
    y = x * lax.rsqrt(ms + EPS) * g
    return y * (1.0 + m[:, D:2 * D]) + m[:, 0:D]


def _mod_kernel(c_ref, w_ref, b_ref, o_ref):
    s = _silu(c_ref[...])
    o_ref[...] = _dot(s.astype(BF16), w_ref[...].astype(BF16)) + b_ref[...]


def _modulation(cvec, w_ada, b_ada):
    n = w_ada.shape[1] // D
    return pl.pallas_call(
        _mod_kernel,
        grid=(n,),
        in_specs=[
            pl.BlockSpec((MOD_ROWS, D), lambda j: (0, 0)),
            pl.BlockSpec((D, D), lambda j: (0, j)),
            pl.BlockSpec((1, D), lambda j: (0, j)),
        ],
        out_specs=pl.BlockSpec((MOD_ROWS, D), lambda j: (0, j)),
        out_shape=jax.ShapeDtypeStruct((MOD_ROWS, n * D), F32),
        compiler_params=_params("arbitrary"),
        name="modulation",
    )(cvec, w_ada, b_ada.reshape(1, n * D))


def _rope(t, cos, sin_hi, sin_lo):
    return t * cos + pltpu.roll(t, LANES - ROPE_FREQS, 1) * sin_hi + pltpu.roll(t, ROPE_FREQS, 1) * sin_lo


def _inproj_attn_kernel(rope, x_ref, m_ref, g_ref, w_ref, *rest):
    if rope:
        cos_ref, shi_ref, slo_ref, qt_ref, k_ref, vt_ref, sg_ref = rest
        cos, shi, slo = cos_ref[...], shi_ref[...], slo_ref[...]
    else:
        qt_ref, k_ref, vt_ref, sg_ref = rest
    hb = _norm_mod(x_ref[0], g_ref[...], m_ref[0]).astype(BF16)
    q = _dot(hb, w_ref[:, 0:A_WIDTH])
    for j in range(A_WIDTH // LANES):
        qj = q[:, j * LANES:(j + 1) * LANES]
        if rope:
            qj = _rope(qj, cos, shi, slo)
        qt_ref[0, j * LANES:(j + 1) * LANES, :] = (qj * (Q_SCALE * LOG2_E)).T.astype(BF16)
    kv = _dot(hb, w_ref[:, A_WIDTH:A_WIDTH + 2 * A_KV_WIDTH])
    k = kv[:, 0:A_KV_WIDTH]
    if rope:
        k = _rope(k, cos, shi, slo)
    k_ref[0] = k.astype(BF16)
    vt_ref[0] = kv[:, A_KV_WIDTH:].T.astype(BF16)
    g = _dot(hb, w_ref[:, A_WIDTH + 2 * A_KV_WIDTH:])
    sg_ref[0] = _silu(g).astype(BF16)


def _inproj_attn(x, mod3, mod_row, norm_g, w, tables):
    b, s, _ = x.shape
    tm = min(ROW_TILE, s)
    rope = tables is not None
    n_w = w.shape[1]
    if mod_row is None:
        mod_map = lambda bi, i: (bi, 0, 0)
    else:
        mod_map = lambda bi, i: (mod_row, 0, 0)
    in_specs = [
        pl.BlockSpec((1, tm, D), lambda bi, i: (bi, i, 0)),
        pl.BlockSpec((1, 1, 3 * D), mod_map),
        pl.BlockSpec((1, D), lambda bi, i: (0, 0)),
        pl.BlockSpec((D, n_w), lambda bi, i: (0, 0)),
    ]
    args = [x, mod3, norm_g.reshape(1, D), w]
    if rope:
        in_specs += [pl.BlockSpec((tm, LANES), lambda bi, i: (i, 0))] * 3
        args += list(tables)
    out_shape = (
        jax.ShapeDtypeStruct((b, A_WIDTH, s), BF16),
        jax.ShapeDtypeStruct((b, s, A_KV_WIDTH), BF16),
        jax.ShapeDtypeStruct((b, A_KV_WIDTH, s), BF16),
        jax.ShapeDtypeStruct((b, s, A_WIDTH), BF16),
    )
    out_specs = (
        pl.BlockSpec((1, A_WIDTH, tm), lambda bi, i: (bi, 0, i)),
        pl.BlockSpec((1, tm, A_KV_WIDTH), lambda bi, i: (bi, i, 0)),
        pl.BlockSpec((1, A_KV_WIDTH, tm), lambda bi, i: (bi, 0, i)),
        pl.BlockSpec((1, tm, A_WIDTH), lambda bi, i: (bi, i, 0)),
    )
    return pl.pallas_call(
        functools.partial(_inproj_attn_kernel, rope),
        grid=(b, s // tm),
        in_specs=in_specs,
        out_specs=out_specs,
        out_shape=out_shape,
        compiler_params=_params("parallel", "parallel"),
        name="inproj_attn_rope" if rope else "inproj_attn_ctx",
    )(*args)


def _attn_kernel(local, n_blocks, sink_ref, qt_ref, *rest):
    if local:
        (kp_ref, kc_ref, kn_ref, vp_ref, vc_ref, vn_ref, kx_ref, vx_ref,
         sg_ref, x_ref, m_ref, w_ref, o_ref) = rest
        kwin = jnp.concatenate([kp_ref[0], kc_ref[0], kn_ref[0], kx_ref[0]], axis=0)
        vtw = jnp.concatenate([vp_ref[0], vc_ref[0], vn_ref[0], vx_ref[0]], axis=1)
    else:
        kx_ref, vx_ref, sg_ref, x_ref, m_ref, w_ref, o_ref = rest
        kwin = kx_ref[0]
        vtw = vx_ref[0]
    n = pl.program_id(1)
    n_keys = kwin.shape[0]
    lane = lax.broadcasted_iota(jnp.int32, kwin.shape, 1)
    kbd = jnp.concatenate([jnp.where(lane < A_HEAD_DIM, kwin, jnp.zeros_like(kwin)),
                           jnp.where(lane >= A_HEAD_DIM, kwin, jnp.zeros_like(kwin))], axis=0)
    vrow = lax.broadcasted_iota(jnp.int32, vtw.shape, 0)
    vbd = jnp.concatenate([jnp.where(vrow < A_HEAD_DIM, vtw, jnp.zeros_like(vtw)),
                           jnp.where(vrow >= A_HEAD_DIM, vtw, jnp.zeros_like(vtw))], axis=1)
    srow = lax.broadcasted_iota(jnp.int32, (SUM_ROWS, 2 * n_keys), 0)
    scol = lax.broadcasted_iota(jnp.int32, (SUM_ROWS, 2 * n_keys), 1) // n_keys
    vbd = jnp.concatenate([vbd, jnp.where(srow == scol, 1.0, 0.0).astype(BF16)], axis=0)
    nq = A_CHUNKS_PER_DOT * BLOCK
    if local:
        kj = lax.broadcasted_iota(jnp.int32, (BLOCK, nq), 0)
        qi = lax.broadcasted_iota(jnp.int32, (BLOCK, nq), 1) % BLOCK
        ok_prev = (kj >= qi) & (n > 0)
        ok_next = (kj <= qi) & (n < n_blocks - 1)
    chunk_of = lax.broadcasted_iota(jnp.int32, (1, nq), 1) // BLOCK
    head_a = lax.broadcasted_iota(jnp.int32, (LANES, nq), 0) < A_HEAD_DIM

    outs = []
    for p in range(0, A_GROUP, A_CHUNKS_PER_DOT):
        qts = jnp.concatenate([qt_ref[0, (p + c) * LANES:(p + c + 1) * LANES, :]
                               for c in range(A_CHUNKS_PER_DOT)], axis=1)
        st = _dot(kbd, qts)
        probs, esink = [], []
        for hh in range(2):
            sh = st[hh * n_keys:(hh + 1) * n_keys]
            if local:
                parts = [jnp.where(ok_prev, sh[0:BLOCK], NEG_INF),
                         sh[BLOCK:2 * BLOCK],
                         jnp.where(ok_next, sh[2 * BLOCK:3 * BLOCK], NEG_INF),
                         sh[3 * BLOCK:]]
            else:
                parts = [sh]
            sink = jnp.full((1, nq), sink_ref[2 * p + hh] * LOG2_E, F32)
            for c in range(1, A_CHUNKS_PER_DOT):
                sink = jnp.where(chunk_of == c, sink_ref[2 * (p + c) + hh] * LOG2_E, sink)
            mx = functools.reduce(jnp.maximum, [jnp.max(t, axis=0, keepdims=True) for t in parts])
            mx = jnp.maximum(mx, sink)
            probs += [jnp.exp2(t - mx).astype(BF16) for t in parts]
            esink.append(jnp.exp2(sink - mx))
        ot = _dot(vbd, jnp.concatenate(probs, axis=0))
        inv = [1.0 / (ot[LANES + hh:LANES + hh + 1] + esink[hh]) for hh in range(2)]
        ot = ot[0:LANES] * jnp.where(head_a, inv[0], inv[1])
        for c in range(A_CHUNKS_PER_DOT):
            sg = sg_ref[0, :, (p + c) * LANES:(p + c + 1) * LANES].astype(F32)
            outs.append((ot[:, c * BLOCK:(c + 1) * BLOCK].T * sg).astype(BF16))
    y = _dot(jnp.concatenate(outs, axis=1), w_ref[...])
    gate = m_ref[0][:, 2 * D:3 * D]
    o_ref[0] = x_ref[0] + gate * y


def _attention(sink, qt, k, vt, kx, vxt, sg, x, mod3, mod_row, w_out, local):
    b, s, _ = sg.shape
    nb = s // BLOCK
    n_ctx = kx.shape[1]
    blk = lambda w: pl.BlockSpec((1, BLOCK, w), lambda bi, i: (bi, i, 0))
    blk_t = lambda w: pl.BlockSpec((1, w, BLOCK), lambda bi, i: (bi, 0, i))
    if mod_row is None:
        mod_map = lambda bi, i: (bi, 0, 0)
    else:
        mod_map = lambda bi, i: (mod_row, 0, 0)
    in_specs = [pl.BlockSpec(memory_space=pltpu.SMEM), blk_t(A_WIDTH)]
    args = [sink, qt]
    if local:
        lo = lambda i: jnp.maximum(i - 1, 0)
        hi = lambda i: jnp.minimum(i + 1, nb - 1)
        in_specs += [pl.BlockSpec((1, BLOCK, A_KV_WIDTH), lambda bi, i: (bi, lo(i), 0)),
                     blk(A_KV_WIDTH),
                     pl.BlockSpec((1, BLOCK, A_KV_WIDTH), lambda bi, i: (bi, hi(i), 0)),
                     pl.BlockSpec((1, A_KV_WIDTH, BLOCK), lambda bi, i: (bi, 0, lo(i))),
                     blk_t(A_KV_WIDTH),
                     pl.BlockSpec((1, A_KV_WIDTH, BLOCK), lambda bi, i: (bi, 0, hi(i)))]
        args += [k, k, k, vt, vt, vt]
    in_specs += [pl.BlockSpec((1, n_ctx, A_KV_WIDTH), lambda bi, i: (bi, 0, 0)),
                 pl.BlockSpec((1, A_KV_WIDTH, n_ctx), lambda bi, i: (bi, 0, 0)),
                 blk(A_WIDTH), blk(D),
                 pl.BlockSpec((1, 1, 3 * D), mod_map),
                 pl.BlockSpec((A_WIDTH, D), lambda bi, i: (0, 0))]
    args += [kx, vxt, sg, x, mod3, w_out]
    return pl.pallas_call(
        functools.partial(_attn_kernel, local, nb),
        grid=(b, nb),
        in_specs=in_specs,
        out_specs=blk(D),
        out_shape=jax.ShapeDtypeStruct((b, s, D), F32),
        compiler_params=_params("parallel", "parallel"),
        name="attn_local" if local else "attn_ctx",
    )(*args)


def _chunk_cumsum(x, tri):
    hi = x.astype(BF16)
    lo = (x - hi.astype(F32)).astype(BF16)
    return _dot(tri, hi) + _dot(tri, lo)


def _inproj_gla_kernel(chunk, with_q, x_ref, m_ref, g_ref, w_ref, wa1_ref, wa2_ref, ba_ref, tri_ref, *outs):
    if with_q:
        per_dir = (outs[0:4], outs[4:8])
        v_ref, sg_ref = outs[8:]
    else:
        per_dir = ((outs[0],), (outs[1],))
        v_ref = outs[2]
    tm = x_ref.shape[1]
    hb = _norm_mod(x_ref[0], g_ref[...], m_ref[0]).astype(BF16)
    r = _dot(hb, wa1_ref[...])
    z = _dot(r.astype(BF16), wa2_ref[...]) + ba_ref[...]
    e = jnp.exp2(jnp.abs(z) * -LOG2_E)
    la = (jnp.minimum(z, 0.0) * LOG2_E - jnp.log2(1.0 + e)) * (1.0 / GATE_TEMP)
    k = _dot(hb, w_ref[:, B_K_WIDTH:2 * B_K_WIDTH])
    if with_q:
        q = _dot(hb, w_ref[:, 0:B_K_WIDTH]) * K_SCALE
    n_chunks = tm // chunk
    for reverse in (False, True):
        lad = la[:, B_K_WIDTH:] if reverse else la[:, 0:B_K_WIDTH]
        refs = per_dir[1] if reverse else per_dir[0]
        tots, kis, kes = [], [], []
        cum_all = _chunk_cumsum(lad, tri_ref[1 if reverse else 0])
        for c in range(n_chunks):
            rows = slice(c * chunk, (c + 1) * chunk)
            cum = cum_all[rows]
            tot = cum[0:1] if reverse else cum[chunk - 1:chunk]
            ke = k[rows] * jnp.exp2(tot - cum)
            if not with_q:
                refs[0][0, rows] = ke.astype(BF16)
                continue
            refs[0][0, rows] = (q[rows] * jnp.exp2(cum)).astype(BF16)
            kis.append(k[rows] * jnp.exp2(-cum))
            kes.append(ke)
            tots.append(tot)
            if c % 2 == 1:
                pair = slice((c - 1) * chunk, (c + 1) * chunk)
                refs[1][0, :, pair] = jnp.concatenate(kis[-2:], axis=0).T.astype(BF16)
                refs[2][0, :, pair] = jnp.concatenate(kes[-2:], axis=0).T.astype(BF16)
        if with_q:
            pad = jnp.zeros((DEC_ROWS - n_chunks, B_K_WIDTH), F32)
            refs[3][0, 0] = jnp.exp2(jnp.concatenate(tots + [pad], axis=0))
    if with_q:
        sg_ref[0] = _silu(_dot(hb, w_ref[:, 2 * B_K_WIDTH + B_V_WIDTH:])).astype(BF16)
    v_ref[0] = _dot(hb, w_ref[:, 2 * B_K_WIDTH:2 * B_K_WIDTH + B_V_WIDTH]).astype(BF16)


def _inproj_gla(x, mod3, mod_row, norm_g, w, wa1, wa2, ba, chunk, with_q):
    b, s, _ = x.shape
    tm = min(ROW_TILE, s)
    nt = s // tm
    cpt = tm // chunk
    if mod_row is None:
        mod_map = lambda bi, i: (bi, 0, 0)
    else:
        mod_map = lambda bi, i: (mod_row, 0, 0)
    const = lambda shape: pl.BlockSpec(shape, lambda bi, i: (0,) * len(shape))
    in_specs = [
        pl.BlockSpec((1, tm, D), lambda bi, i: (bi, i, 0)),
        pl.BlockSpec((1, 1, 3 * D), mod_map),
        const((1, D)), const(w.shape), const(wa1.shape), const(wa2.shape), const(ba.shape),
        const((2, tm, tm)),
    ]
    t_idx = jnp.arange(tm)
    same = (t_idx[:, None] // chunk) == (t_idx[None, :] // chunk)
    lower = same & (t_idx[None, :] <= t_idx[:, None])
    tri = jnp.stack([lower, lower.T]).astype(BF16)
    rows = lambda w_: (jax.ShapeDtypeStruct((b, s, w_), BF16),
                       pl.BlockSpec((1, tm, w_), lambda bi, i: (bi, i, 0)))
    cols = (jax.ShapeDtypeStruct((b, B_K_WIDTH, s), BF16),
            pl.BlockSpec((1, B_K_WIDTH, tm), lambda bi, i: (bi, 0, i)))
    decs = (jax.ShapeDtypeStruct((b, nt, DEC_ROWS, B_K_WIDTH), F32),
            pl.BlockSpec((1, 1, DEC_ROWS, B_K_WIDTH), lambda bi, i: (bi, i, 0, 0)))
    if with_q:
        assert cpt <= DEC_ROWS and cpt % 2 == 0
        outs = [rows(B_K_WIDTH), cols, cols, decs] * 2 + [rows(B_V_WIDTH)] * 2
    else:
        outs = [rows(B_K_WIDTH)] * 2 + [rows(B_V_WIDTH)]
    return pl.pallas_call(
        functools.partial(_inproj_gla_kernel, chunk, with_q),
        grid=(b, nt),
        in_specs=in_specs,
        out_specs=tuple(o[1] for o in outs),
        out_shape=tuple(o[0] for o in outs),
        compiler_params=_params("parallel", "parallel"),
        name="inproj_gla" if with_q else "inproj_gla_ctx",
    )(x, mod3, norm_g.reshape(1, D), w, wa1, wa2, ba, tri)


def _gla_state_kernel(kf_ref, kb_ref, v_ref, sf_ref, sb_ref):
    for h in range(B_HEADS):
        vh = v_ref[0, :, h * B_VAL_DIM:(h + 1) * B_VAL_DIM]
        ks = slice(h * B_KEY_DIM, (h + 1) * B_KEY_DIM)
        sf_ref[0, h] = _dot_tn(kf_ref[0, :, ks], vh)
        sb_ref[0, h] = _dot_tn(kb_ref[0, :, ks], vh)


def _gla_state(ke_f, ke_b, v):
    b, n, _ = v.shape
    st = jax.ShapeDtypeStruct((b, B_HEADS, B_KEY_DIM, B_VAL_DIM), F32)
    st_spec = pl.BlockSpec((1, B_HEADS, B_KEY_DIM, B_VAL_DIM), lambda bi: (bi, 0, 0, 0))
    return pl.pallas_call(
        _gla_state_kernel,
        grid=(b,),
        in_specs=[pl.BlockSpec((1, n, B_K_WIDTH), lambda bi: (bi, 0, 0))] * 2
        + [pl.BlockSpec((1, n, B_V_WIDTH), lambda bi: (bi, 0, 0))],
        out_specs=(st_spec, st_spec),
        out_shape=(st, st),
        compiler_params=_params("parallel"),
        name="gla_ctx_state",
    )(ke_f, ke_b, v)


def _scan_tile(reverse, qd_ref, kit_ref, ket_ref, dec_ref, v_ref, st_ref):
    pair_rows = 2 * CHUNK
    ti = lax.broadcasted_iota(jnp.int32, (pair_rows, pair_rows), 0)
    si = lax.broadcasted_iota(jnp.int32, (pair_rows, pair_rows), 1)
    same = (ti // CHUNK) == (si // CHUNK)
    if reverse:
        use_inv = same & (ti <= si)
        use_end = (ti < CHUNK) & (si >= CHUNK)
        second_row = lax.broadcasted_iota(jnp.int32, (pair_rows, B_KEY_DIM), 0) < CHUNK
        first_col = lax.broadcasted_iota(jnp.int32, (B_KEY_DIM, pair_rows), 1) >= CHUNK
    else:
        use_inv = same & (ti >= si)
        use_end = (ti >= CHUNK) & (si < CHUNK)
        second_row = lax.broadcasted_iota(jnp.int32, (pair_rows, B_KEY_DIM), 0) >= CHUNK
        first_col = lax.broadcasted_iota(jnp.int32, (B_KEY_DIM, pair_rows), 1) < CHUNK
    n_pairs = SCAN_TILE // pair_rows
    order = range(n_pairs - 1, -1, -1) if reverse else range(n_pairs)
    dec = dec_ref[0, 0]
    dect = jnp.concatenate([dec, jnp.zeros((LANES - DEC_ROWS, B_K_WIDTH), F32)], axis=0).T
    heads = []
    for h in range(B_HEADS):
        ks = slice(h * B_KEY_DIM, (h + 1) * B_KEY_DIM)
        vs = slice(h * B_VAL_DIM, (h + 1) * B_VAL_DIM)
        st = st_ref[h]
        o_rows = [None] * n_pairs
        for p in order:
            rows = slice(p * pair_rows, (p + 1) * pair_rows)
            c_first, c_second = (2 * p + 1, 2 * p) if reverse else (2 * p, 2 * p + 1)
            qd = qd_ref[0, rows, ks]
            ket = ket_ref[0, ks, rows]
            vh = v_ref[0, rows, vs]
            a = jnp.where(use_inv, _dot(qd, kit_ref[0, ks, rows]),
                          jnp.where(use_end, _dot(qd, ket), 0.0)).astype(BF16)
            q_pair = jnp.where(second_row, qd.astype(F32) * dec[c_first:c_first + 1, ks], qd.astype(F32))
            k_pair = jnp.where(first_col, ket.astype(F32) * dect[ks, c_second:c_second + 1], ket.astype(F32))
            o_rows[p] = _dot(a, vh) + _dot(q_pair.astype(BF16), st.astype(BF16))
            st = (st * (dect[ks, c_first:c_first + 1] * dect[ks, c_second:c_second + 1])
                  + _dot(k_pair.astype(BF16), vh))
        st_ref[h] = st
        heads.append(jnp.concatenate(o_rows, axis=0))
    return jnp.concatenate(heads, axis=1)


def _gla_scan_kernel(nt, qdf_ref, kitf_ref, ketf_ref, decf_ref, qdb_ref, kitb_ref, ketb_ref, decb_ref,
                     v_ref, sf_ref, sb_ref, sg_ref, x_ref, m_ref, hn_ref, w_ref, fn_ref,
                     o_ref, st_ref, of_ref):
    j = pl.program_id(1)

    @pl.when(j == 0)
    def _():
        st_ref[...] = sf_ref[0]

    @pl.when(j == nt)
    def _():
        st_ref[...] = sb_ref[0]

    @pl.when(j < nt)
    def _():
        of_ref[j] = _scan_tile(False, qdf_ref, kitf_ref, ketf_ref, decf_ref, v_ref, st_ref)

    @pl.when(j >= nt)
    def _():
        o = _scan_tile(True, qdb_ref, kitb_ref, ketb_ref, decb_ref, v_ref, st_ref) + of_ref[2 * nt - 1 - j]
        normed = []
        for h in range(B_HEADS):
            oh = o[:, h * B_VAL_DIM:(h + 1) * B_VAL_DIM]
            normed.append(oh * lax.rsqrt(jnp.mean(oh * oh, axis=-1, keepdims=True) + EPS))
        of = jnp.concatenate(normed, axis=1) * hn_ref[...]
        y = _dot((of * sg_ref[0].astype(F32)).astype(BF16), w_ref[...])
        xn = x_ref[0] + m_ref[0][:, 2 * D:3 * D] * y
        ms = jnp.mean(xn * xn, axis=-1, keepdims=True)
        o_ref[0] = xn * lax.rsqrt(ms + EPS) * fn_ref[...]


def _gla_scan(fwd, bwd, v, s_f, s_b, sg, x, mod3, head_g, w_out, final_g):
    b, s, _ = v.shape
    assert SCAN_TILE == ROW_TILE
    nt = s // SCAN_TILE
    t_fwd = lambda j: jnp.minimum(j, nt - 1)
    t_bwd = lambda j: 2 * nt - 1 - jnp.maximum(j, nt)
    t_both = lambda j: jnp.where(j < nt, j, 2 * nt - 1 - j)

    def direction(t):
        rows = pl.BlockSpec((1, SCAN_TILE, B_K_WIDTH), lambda bi, j: (bi, t(j), 0))
        cols = pl.BlockSpec((1, B_K_WIDTH, SCAN_TILE), lambda bi, j: (bi, 0, t(j)))
        decs = pl.BlockSpec((1, 1, DEC_ROWS, B_K_WIDTH), lambda bi, j: (bi, t(j), 0, 0))
        return [rows, cols, cols, decs]

    const = lambda shape: pl.BlockSpec(shape, lambda bi, j: (0,) * len(shape))
    state = pl.BlockSpec((1, B_HEADS, B_KEY_DIM, B_VAL_DIM), lambda bi, j: (bi, 0, 0, 0))
    out_rows = lambda w_: pl.BlockSpec((1, SCAN_TILE, w_), lambda bi, j: (bi, t_bwd(j), 0))
    in_specs = direction(t_fwd) + direction(t_bwd) + [
        pl.BlockSpec((1, SCAN_TILE, B_V_WIDTH), lambda bi, j: (bi, t_both(j), 0)),
        state, state, out_rows(B_V_WIDTH), out_rows(D),
        pl.BlockSpec((1, 1, 3 * D), lambda bi, j: (bi, 0, 0)),
        const((1, B_V_WIDTH)), const((B_V_WIDTH, D)), const((1, D))]
    return pl.pallas_call(
        functools.partial(_gla_scan_kernel, nt),
        grid=(b, 2 * nt),
        in_specs=in_specs,
        out_specs=out_rows(D),
        out_shape=jax.ShapeDtypeStruct((b, s, D), F32),
        scratch_shapes=[pltpu.VMEM((B_HEADS, B_KEY_DIM, B_VAL_DIM), F32),
                        pltpu.VMEM((nt, SCAN_TILE, B_V_WIDTH), F32)],
        compiler_params=_params("parallel", "arbitrary"),
        name="gla_scan",
    )(*fwd, *bwd, v, s_f, s_b, sg, x, mod3, head_g.reshape(1, B_V_WIDTH), w_out, final_g.reshape(1, D))


def _rope_tables(n_tokens):
    rows_n = n_tokens // GRID_W
    row = jnp.repeat(jnp.arange(rows_n, dtype=F32), GRID_W)
    col = jnp.tile(jnp.arange(GRID_W, dtype=F32), rows_n)
    inv_freq = ROPE_BASE ** (-jnp.arange(ROPE_FREQS, dtype=F32) / ROPE_FREQS)
    ang = jnp.stack([row[:, None] * inv_freq, col[:, None] * inv_freq], axis=1)
    cos, sin = jnp.cos(ang), jnp.sin(ang)
    zero = jnp.zeros_like(sin)
    tile = lambda t: jnp.tile(t.reshape(n_tokens, A_HEAD_DIM), (1, LANES // A_HEAD_DIM))
    return (tile(jnp.stack([cos, cos], axis=2)),
            tile(jnp.stack([-sin, zero], axis=2)),
            tile(jnp.stack([zero, sin], axis=2)))


def _head_pair_perm():
    cols = []
    for p in range(A_GROUP):
        for h in (p, p + A_GROUP):
            cols.extend(range(h * A_HEAD_DIM, (h + 1) * A_HEAD_DIM))
    heads = [h for p in range(A_GROUP) for h in (p, p + A_GROUP)]
    return jnp.asarray(cols, dtype=jnp.int32), jnp.asarray(heads, dtype=jnp.int32)


def kernel(x, c, ctx, c_ctx, l0_norm_g, l0_w_ada, l0_b_ada, l0_w_in, l0_sink, l0_w_out, l1_norm_g, l1_w_ada, l1_b_ada, l1_w_in, l1_wa1_f, l1_wa2_f, l1_ba_f, l1_wa1_b, l1_wa2_b, l1_ba_b, l1_head_norm_g, l1_w_out, final_norm_g):
    b, s, _ = x.shape
    ctx_row = b

    cvec = jnp.concatenate([c, c_ctx[None, :], jnp.zeros((MOD_ROWS - b - 1, D), F32)], axis=0)
    mod0 = _modulation(cvec, l0_w_ada, l0_b_ada).reshape(MOD_ROWS, 1, 3 * D)
    mod1 = _modulation(cvec, l1_w_ada, l1_b_ada).reshape(MOD_ROWS, 1, 3 * D)

    cols, heads = _head_pair_perm()
    q_end, kv_end = A_WIDTH, A_WIDTH + 2 * A_KV_WIDTH
    w0 = jnp.concatenate([l0_w_in[:, :q_end][:, cols], l0_w_in[:, q_end:kv_end],
                          l0_w_in[:, kv_end:][:, cols]], axis=1).astype(BF16)
    w0_out = l0_w_out[cols, :].astype(BF16)
    sink = l0_sink.astype(F32)[heads]
    qt, k, vt, sg = _inproj_attn(x, mod0, None, l0_norm_g, w0, _rope_tables(s))
    qct, kc, vct, sgc = _inproj_attn(ctx, mod0, ctx_row, l0_norm_g, w0, None)
    x1 = _attention(sink, qt, k, vt, kc, vct, sg, x, mod0, None, w0_out, True)
    xc1 = _attention(sink, qct, None, None, kc, vct, sgc, ctx, mod0, ctx_row, w0_out, False)

    w1 = l1_w_in.astype(BF16)
    wa1 = jnp.concatenate([l1_wa1_f, l1_wa1_b, jnp.zeros((D, LANES - 2 * GATE_RANK), F32)], axis=1).astype(BF16)
    wa2 = jnp.zeros((LANES, 2 * B_K_WIDTH), F32)
    wa2 = wa2.at[0:GATE_RANK, 0:B_K_WIDTH].set(l1_wa2_f)
    wa2 = wa2.at[GATE_RANK:2 * GATE_RANK, B_K_WIDTH:].set(l1_wa2_b).astype(BF16)
    ba = jnp.concatenate([l1_ba_f, l1_ba_b]).reshape(1, 2 * B_K_WIDTH)
    kec_f, kec_b, vcx = _inproj_gla(xc1, mod1, ctx_row, l1_norm_g, w1, wa1, wa2, ba, ctx.shape[1], False)
    s_f, s_b = _gla_state(kec_f, kec_b, vcx)
    outs = _inproj_gla(x1, mod1, None, l1_norm_g, w1, wa1, wa2, ba, CHUNK, True)
    v1, sg1 = outs[8:]
    return _gla_scan(outs[0:4], outs[4:8], v1, s_f, s_b, sg1, x1, mod1,
                     l1_head_norm_g, l1_w_out.astype(BF16), final_norm_g)
```

```python
import functools

import jax
import jax.numpy as jnp
from jax import lax
from jax.experimental import pallas as pl
from jax.experimental.pallas import tpu as pltpu

F32 = jnp.float32
BF16 = jnp.bfloat16

D = 1024
GRID_W = 64
EPS = 1e-6
NEG_INF = -1e30

A_HEADS = 16
A_KV_HEADS = 2
A_GROUP = A_HEADS // A_KV_HEADS
A_HEAD_DIM = 64
A_WIDTH = A_HEADS * A_HEAD_DIM
A_KV_WIDTH = A_KV_HEADS * A_HEAD_DIM
BLOCK = 128
ROPE_BASE = 10000.0
ROPE_FREQS = A_HEAD_DIM // 4
Q_SCALE = A_HEAD_DIM ** -0.5
LOG2_E = 1.4426950408889634

B_HEADS = 4
B_K_WIDTH = D // 2
B_V_WIDTH = D
B_KEY_DIM = B_K_WIDTH // B_HEADS
B_VAL_DIM = B_V_WIDTH // B_HEADS
GATE_RANK = 16
GATE_TEMP = 16.0
CHUNK = 64
K_SCALE = B_KEY_DIM ** -0.5

LANES = 128
MXU_DIM = 256
MOD_ROWS = 16
ROW_TILE = 512
SCAN_TILE = 512
DEC_ROWS = 8
SUM_ROWS = 16
A_CHUNKS_PER_DOT = 8
VMEM_LIMIT = 48 * 1024 * 1024


def _params(*sem):
    return pltpu.CompilerParams(dimension_semantics=sem, vmem_limit_bytes=VMEM_LIMIT)


def _silu(x):
    return x / (1.0 + jnp.exp(-x))


def _dot(a, b):
    return jnp.dot(a, b, preferred_element_type=F32)


def _dot_nt(a, b):
    return lax.dot_general(a, b, (((1,), (1,)), ((), ())), preferred_element_type=F32)


def _dot_tn(a, b):
    return lax.dot_general(a, b, (((0,), (0,)), ((), ())), preferred_element_type=F32)


def _norm_mod(x, g, m):
    ms = jnp.mean(x * x, axis=-1, keepdims=True)
    y = x * lax.rsqrt(ms + EPS) * g
    return y * (1.0 + m[:, D:2 * D]) + m[:, 0:D]


def _mod_kernel(c_ref, w_ref, b_ref, o_ref):
    s = _silu(c_ref[...])
    o_ref[...] = _dot(s.astype(BF16), w_ref[...].astype(BF16)) + b_ref[...]


def _modulation(cvec, w_ada, b_ada):
    n = w_ada.shape[1] // D
    return pl.pallas_call(
        _mod_kernel,
        grid=(n,),
        in_specs=[
            pl.BlockSpec((MOD_ROWS, D), lambda j: (0, 0)),
            pl.BlockSpec((D, D), lambda j: (0, j)),
            pl.BlockSpec((1, D), lambda j: (0, j)),
        ],
        out_specs=pl.BlockSpec((MOD_ROWS, D), lambda j: (0, j)),
        out_shape=jax.ShapeDtypeStruct((MOD_ROWS, n * D), F32),
        compiler_params=_params("arbitrary"),
        name="modulation",
    )(cvec, w_ada, b_ada.reshape(1, n * D))


def _rope(t, cos, sin_hi, sin_lo):
    return t * cos + pltpu.roll(t, LANES - ROPE_FREQS, 1) * sin_hi + pltpu.roll(t, ROPE_FREQS, 1) * sin_lo


def _inproj_attn_kernel(rope, x_ref, m_ref, g_ref, w_ref, *rest):
    if rope:
        cos_ref, shi_ref, slo_ref, qt_ref, k_ref, vt_ref, sg_ref = rest
        cos, shi, slo = cos_ref[...], shi_ref[...], slo_ref[...]
    else:
        qt_ref, k_ref, vt_ref, sg_ref = rest
    hb = _norm_mod(x_ref[0], g_ref[...], m_ref[0]).astype(BF16)
    q = _dot(hb, w_ref[:, 0:A_WIDTH])
    for j in range(A_WIDTH // LANES):
        qj = q[:, j * LANES:(j + 1) * LANES]
        if rope:
            qj = _rope(qj, cos, shi, slo)
        qt_ref[0, j * LANES:(j + 1) * LANES, :] = (qj * (Q_SCALE * LOG2_E)).T.astype(BF16)
    kv = _dot(hb, w_ref[:, A_WIDTH:A_WIDTH + 2 * A_KV_WIDTH])
    k = kv[:, 0:A_KV_WIDTH]
    if rope:
        k = _rope(k, cos, shi, slo)
    k_ref[0] = k.astype(BF16)
    vt_ref[0] = kv[:, A_KV_WIDTH:].T.astype(BF16)
    g = _dot(hb, w_ref[:, A_WIDTH + 2 * A_KV_WIDTH:])
    sg_ref[0] = _silu(g).astype(BF16)


def _inproj_attn(x, mod3, mod_row, norm_g, w, tables):
    b, s, _ = x.shape
    tm = min(ROW_TILE, s)
    rope = tables is not None
    n_w = w.shape[1]
    if mod_row is None:
        mod_map = lambda bi, i: (bi, 0, 0)
    else:
        mod_map = lambda bi, i: (mod_row, 0, 0)
    in_specs = [
        pl.BlockSpec((1, tm, D), lambda bi, i: (bi, i, 0)),
        pl.BlockSpec((1, 1, 3 * D), mod_map),
        pl.BlockSpec((1, D), lambda bi, i: (0, 0)),
        pl.BlockSpec((D, n_w), lambda bi, i: (0, 0)),
    ]
    args = [x, mod3, norm_g.reshape(1, D), w]
    if rope:
        in_specs += [pl.BlockSpec((tm, LANES), lambda bi, i: (i, 0))] * 3
        args += list(tables)
    out_shape = (
        jax.ShapeDtypeStruct((b, A_WIDTH, s), BF16),
        jax.ShapeDtypeStruct((b, s, A_KV_WIDTH), BF16),
        jax.ShapeDtypeStruct((b, A_KV_WIDTH, s), BF16),
        jax.ShapeDtypeStruct((b, s, A_WIDTH), BF16),
    )
    out_specs = (
        pl.BlockSpec((1, A_WIDTH, tm), lambda bi, i: (bi, 0, i)),
        pl.BlockSpec((1, tm, A_KV_WIDTH), lambda bi, i: (bi, i, 0)),
        pl.BlockSpec((1, A_KV_WIDTH, tm), lambda bi, i: (bi, 0, i)),
        pl.BlockSpec((1, tm, A_WIDTH), lambda bi, i: (bi, i, 0)),
    )
    return pl.pallas_call(
        functools.partial(_inproj_attn_kernel, rope),
        grid=(b, s // tm),
        in_specs=in_specs,
        out_specs=out_specs,
        out_shape=out_shape,
        compiler_params=_params("parallel", "parallel"),
        name="inproj_attn_rope" if rope else "inproj_attn_ctx",
    )(*args)


def _attn_kernel(local, n_blocks, sink_ref, qt_ref, *rest):
    if local:
        (kp_ref, kc_ref, kn_ref, vp_ref, vc_ref, vn_ref, kx_ref, vx_ref,
         sg_ref, x_ref, m_ref, w_ref, o_ref) = rest
        kwin = jnp.concatenate([kp_ref[0], kc_ref[0], kn_ref[0], kx_ref[0]], axis=0)
        vtw = jnp.concatenate([vp_ref[0], vc_ref[0], vn_ref[0], vx_ref[0]], axis=1)
    else:
        kx_ref, vx_ref, sg_ref, x_ref, m_ref, w_ref, o_ref = rest
        kwin = kx_ref[0]
        vtw = vx_ref[0]
    n = pl.program_id(1)
    n_keys = kwin.shape[0]
    lane = lax.broadcasted_iota(jnp.int32, kwin.shape, 1)
    kbd = jnp.concatenate([jnp.where(lane < A_HEAD_DIM, kwin, jnp.zeros_like(kwin)),
                           jnp.where(lane >= A_HEAD_DIM, kwin, jnp.zeros_like(kwin))], axis=0)
    vrow = lax.broadcasted_iota(jnp.int32, vtw.shape, 0)
    vbd = jnp.concatenate([jnp.where(vrow < A_HEAD_DIM, vtw, jnp.zeros_like(vtw)),
                           jnp.where(vrow >= A_HEAD_DIM, vtw, jnp.zeros_like(vtw))], axis=1)
    srow = lax.broadcasted_iota(jnp.int32, (SUM_ROWS, 2 * n_keys), 0)
    scol = lax.broadcasted_iota(jnp.int32, (SUM_ROWS, 2 * n_keys), 1) // n_keys
    vbd = jnp.concatenate([vbd, jnp.where(srow == scol, 1.0, 0.0).astype(BF16)], axis=0)
    nq = A_CHUNKS_PER_DOT * BLOCK
    if local:
        kj = lax.broadcasted_iota(jnp.int32, (BLOCK, nq), 0)
        qi = lax.broadcasted_iota(jnp.int32, (BLOCK, nq), 1) % BLOCK
        ok_prev = (kj >= qi) & (n > 0)
        ok_next = (kj <= qi) & (n < n_blocks - 1)
    chunk_of = lax.broadcasted_iota(jnp.int32, (1, nq), 1) // BLOCK
    head_a = lax.broadcasted_iota(jnp.int32, (LANES, nq), 0) < A_HEAD_DIM

    outs = []
    for p in range(0, A_GROUP, A_CHUNKS_PER_DOT):
        qts = jnp.concatenate([qt_ref[0, (p + c) * LANES:(p + c + 1) * LANES, :]
                               for c in range(A_CHUNKS_PER_DOT)], axis=1)
        st = _dot(kbd, qts)
        probs, esink = [], []
        for hh in range(2):
            sh = st[hh * n_keys:(hh + 1) * n_keys]
            if local:
                parts = [jnp.where(ok_prev, sh[0:BLOCK], NEG_INF),
                         sh[BLOCK:2 * BLOCK],
                         jnp.where(ok_next, sh[2 * BLOCK:3 * BLOCK], NEG_INF),
                         sh[3 * BLOCK:]]
            else:
                parts = [sh]
            sink = jnp.full((1, nq), sink_ref[2 * p + hh] * LOG2_E, F32)
            for c in range(1, A_CHUNKS_PER_DOT):
                sink = jnp.where(chunk_of == c, sink_ref[2 * (p + c) + hh] * LOG2_E, sink)
            mx = functools.reduce(jnp.maximum, [jnp.max(t, axis=0, keepdims=True) for t in parts])
            mx = jnp.maximum(mx, sink)
            probs += [jnp.exp2(t - mx).astype(BF16) for t in parts]
            esink.append(jnp.exp2(sink - mx))
        ot = _dot(vbd, jnp.concatenate(probs, axis=0))
        inv = [1.0 / (ot[LANES + hh:LANES + hh + 1] + esink[hh]) for hh in range(2)]
        ot = ot[0:LANES] * jnp.where(head_a, inv[0], inv[1])
        for c in range(A_CHUNKS_PER_DOT):
            sg = sg_ref[0, :, (p + c) * LANES:(p + c + 1) * LANES].astype(F32)
            outs.append((ot[:, c * BLOCK:(c + 1) * BLOCK].T * sg).astype(BF16))
    y = _dot(jnp.concatenate(outs, axis=1), w_ref[...])
    gate = m_ref[0][:, 2 * D:3 * D]
    o_ref[0] = x_ref[0] + gate * y


def _attention(sink, qt, k, vt, kx, vxt, sg, x, mod3, mod_row, w_out, local):
    b, s, _ = sg.shape
    nb = s // BLOCK
    n_ctx = kx.shape[1]
    blk = lambda w: pl.BlockSpec((1, BLOCK, w), lambda bi, i: (bi, i, 0))
    blk_t = lambda w: pl.BlockSpec((1, w, BLOCK), lambda bi, i: (bi, 0, i))
    if mod_row is None:
        mod_map = lambda bi, i: (bi, 0, 0)
    else:
        mod_map = lambda bi, i: (mod_row, 0, 0)
    in_specs = [pl.BlockSpec(memory_space=pltpu.SMEM), blk_t(A_WIDTH)]
    args = [sink, qt]
    if local:
        lo = lambda i: jnp.maximum(i - 1, 0)
        hi = lambda i: jnp.minimum(i + 1, nb - 1)
        in_specs += [pl.BlockSpec((1, BLOCK, A_KV_WIDTH), lambda bi, i: (bi, lo(i), 0)),
                     blk(A_KV_WIDTH),
                     pl.BlockSpec((1, BLOCK, A_KV_WIDTH), lambda bi, i: (bi, hi(i), 0)),
                     pl.BlockSpec((1, A_KV_WIDTH, BLOCK), lambda bi, i: (bi, 0, lo(i))),
                     blk_t(A_KV_WIDTH),
                     pl.BlockSpec((1, A_KV_WIDTH, BLOCK), lambda bi, i: (bi, 0, hi(i)))]
        args += [k, k, k, vt, vt, vt]
    in_specs += [pl.BlockSpec((1, n_ctx, A_KV_WIDTH), lambda bi, i: (bi, 0, 0)),
                 pl.BlockSpec((1, A_KV_WIDTH, n_ctx), lambda bi, i: (bi, 0, 0)),
                 blk(A_WIDTH), blk(D),
                 pl.BlockSpec((1, 1, 3 * D), mod_map),
                 pl.BlockSpec((A_WIDTH, D), lambda bi, i: (0, 0))]
    args += [kx, vxt, sg, x, mod3, w_out]
    return pl.pallas_call(
        functools.partial(_attn_kernel, local, nb),
        grid=(b, nb),
        in_specs=in_specs,
        out_specs=blk(D),
        out_shape=jax.ShapeDtypeStruct((b, s, D), F32),
        compiler_params=_params("parallel", "parallel"),
        name="attn_local" if local else "attn_ctx",
    )(*args)


def _chunk_cumsum(x, tri):
    hi = x.astype(BF16)
    lo = (x - hi.astype(F32)).astype(BF16)
    return _dot(tri, hi) + _dot(tri, lo)


def _inproj_gla_kernel(chunk, with_q, x_ref, m_ref, g_ref, w_ref, wa1_ref, wa2_ref, ba_ref, tri_ref, *outs):
    if with_q:
        per_dir = (outs[0:4], outs[4:8])
        v_ref, sg_ref = outs[8:]
    else:
        per_dir = ((outs[0],), (outs[1],))
        v_ref = outs[2]
    tm = x_ref.shape[1]
    hb = _norm_mod(x_ref[0], g_ref[...], m_ref[0]).astype(BF16)
    r = _dot(hb, wa1_ref[...])
    z = _dot(r.astype(BF16), wa2_ref[...]) + ba_ref[...]
    e = jnp.exp2(jnp.abs(z) * -LOG2_E)
    la = (jnp.minimum(z, 0.0) * LOG2_E - jnp.log2(1.0 + e)) * (1.0 / GATE_TEMP)
    k = _dot(hb, w_ref[:, B_K_WIDTH:2 * B_K_WIDTH])
    if with_q:
        q = _dot(hb, w_ref[:, 0:B_K_WIDTH]) * K_SCALE
    n_chunks = tm // chunk
    for reverse in (False, True):
        lad = la[:, B_K_WIDTH:] if reverse else la[:, 0:B_K_WIDTH]
        refs = per_dir[1] if reverse else per_dir[0]
        tots, kis, kes = [], [], []
        tri = tri_ref[1 if reverse else 0]
        tb = tri.shape[0]
        cum_all = jnp.concatenate([_chunk_cumsum(lad[r * tb:(r + 1) * tb], tri) for r in range(tm // tb)], axis=0)
        for c in range(n_chunks):
            rows = slice(c * chunk, (c + 1) * chunk)
            cum = cum_all[rows]
            tot = cum[0:1] if reverse else cum[chunk - 1:chunk]
            ke = k[rows] * jnp.exp2(tot - cum)
            if not with_q:
                refs[0][0, rows] = ke.astype(BF16)
                continue
            refs[0][0, rows] = (q[rows] * jnp.exp2(cum)).astype(BF16)
            kis.append(k[rows] * jnp.exp2(-cum))
            kes.append(ke)
            tots.append(tot)
            if c % 2 == 1:
                pair = slice((c - 1) * chunk, (c + 1) * chunk)
                refs[1][0, :, pair] = jnp.concatenate(kis[-2:], axis=0).T.astype(BF16)
                refs[2][0, :, pair] = jnp.concatenate(kes[-2:], axis=0).T.astype(BF16)
        if with_q:
            pad = [jnp.zeros((DEC_ROWS - n_chunks, B_K_WIDTH), F32)] if n_chunks < DEC_ROWS else []
            refs[3][0, 0] = jnp.exp2(jnp.concatenate(tots + pad, axis=0))
    if with_q:
        sg_ref[0] = _silu(_dot(hb, w_ref[:, 2 * B_K_WIDTH + B_V_WIDTH:])).astype(BF16)
    v_ref[0] = _dot(hb, w_ref[:, 2 * B_K_WIDTH:2 * B_K_WIDTH + B_V_WIDTH]).astype(BF16)


def _inproj_gla(x, mod3, mod_row, norm_g, w, wa1, wa2, ba, chunk, with_q):
    b, s, _ = x.shape
    tm = min(ROW_TILE, s)
    nt = s // tm
    cpt = tm // chunk
    tb = max(chunk, MXU_DIM)
    assert tm % tb == 0 and tb % chunk == 0
    if mod_row is None:
        mod_map = lambda bi, i: (bi, 0, 0)
    else:
        mod_map = lambda bi, i: (mod_row, 0, 0)
    const = lambda shape: pl.BlockSpec(shape, lambda bi, i: (0,) * len(shape))
    in_specs = [
        pl.BlockSpec((1, tm, D), lambda bi, i: (bi, i, 0)),
        pl.BlockSpec((1, 1, 3 * D), mod_map),
        const((1, D)), const(w.shape), const(wa1.shape), const(wa2.shape), const(ba.shape),
        const((2, tb, tb)),
    ]
    t_idx = jnp.arange(tb)
    same = (t_idx[:, None] // chunk) == (t_idx[None, :] // chunk)
    lower = same & (t_idx[None, :] <= t_idx[:, None])
    tri = jnp.stack([lower, lower.T]).astype(BF16)
    rows = lambda w_: (jax.ShapeDtypeStruct((b, s, w_), BF16),
                       pl.BlockSpec((1, tm, w_), lambda bi, i: (bi, i, 0)))
    cols = (jax.ShapeDtypeStruct((b, B_K_WIDTH, s), BF16),
            pl.BlockSpec((1, B_K_WIDTH, tm), lambda bi, i: (bi, 0, i)))
    decs = (jax.ShapeDtypeStruct((b, nt, DEC_ROWS, B_K_WIDTH), F32),
            pl.BlockSpec((1, 1, DEC_ROWS, B_K_WIDTH), lambda bi, i: (bi, i, 0, 0)))
    if with_q:
        assert cpt <= DEC_ROWS and cpt % 2 == 0
        outs = [rows(B_K_WIDTH), cols, cols, decs] * 2 + [rows(B_V_WIDTH)] * 2
    else:
        outs = [rows(B_K_WIDTH)] * 2 + [rows(B_V_WIDTH)]
    return pl.pallas_call(
        functools.partial(_inproj_gla_kernel, chunk, with_q),
        grid=(b, nt),
        in_specs=in_specs,
        out_specs=tuple(o[1] for o in outs),
        out_shape=tuple(o[0] for o in outs),
        compiler_params=_params("parallel", "parallel"),
        name="inproj_gla" if with_q else "inproj_gla_ctx",
    )(x, mod3, norm_g.reshape(1, D), w, wa1, wa2, ba, tri)


def _gla_state_kernel(kf_ref, kb_ref, v_ref, sf_ref, sb_ref):
    for h in range(B_HEADS):
        vh = v_ref[0, :, h * B_VAL_DIM:(h + 1) * B_VAL_DIM]
        ks = slice(h * B_KEY_DIM, (h + 1) * B_KEY_DIM)
        sf_ref[0, h] = _dot_tn(kf_ref[0, :, ks], vh)
        sb_ref[0, h] = _dot_tn(kb_ref[0, :, ks], vh)


def _gla_state(ke_f, ke_b, v):
    b, n, _ = v.shape
    st = jax.ShapeDtypeStruct((b, B_HEADS, B_KEY_DIM, B_VAL_DIM), F32)
    st_spec = pl.BlockSpec((1, B_HEADS, B_KEY_DIM, B_VAL_DIM), lambda bi: (bi, 0, 0, 0))
    return pl.pallas_call(
        _gla_state_kernel,
        grid=(b,),
        in_specs=[pl.BlockSpec((1, n, B_K_WIDTH), lambda bi: (bi, 0, 0))] * 2
        + [pl.BlockSpec((1, n, B_V_WIDTH), lambda bi: (bi, 0, 0))],
        out_specs=(st_spec, st_spec),
        out_shape=(st, st),
        compiler_params=_params("parallel"),
        name="gla_ctx_state",
    )(ke_f, ke_b, v)


def _scan_tile(reverse, qd_ref, kit_ref, ket_ref, dec_ref, v_ref, st_ref):
    pair_rows = 2 * CHUNK
    ti = lax.broadcasted_iota(jnp.int32, (pair_rows, pair_rows), 0)
    si = lax.broadcasted_iota(jnp.int32, (pair_rows, pair_rows), 1)
    same = (ti // CHUNK) == (si // CHUNK)
    if reverse:
        use_inv = same & (ti <= si)
        use_end = (ti < CHUNK) & (si >= CHUNK)
        second_row = lax.broadcasted_iota(jnp.int32, (pair_rows, B_KEY_DIM), 0) < CHUNK
        first_col = lax.broadcasted_iota(jnp.int32, (B_KEY_DIM, pair_rows), 1) >= CHUNK
    else:
        use_inv = same & (ti >= si)
        use_end = (ti >= CHUNK) & (si < CHUNK)
        second_row = lax.broadcasted_iota(jnp.int32, (pair_rows, B_KEY_DIM), 0) >= CHUNK
        first_col = lax.broadcasted_iota(jnp.int32, (B_KEY_DIM, pair_rows), 1) < CHUNK
    n_pairs = SCAN_TILE // pair_rows
    order = range(n_pairs - 1, -1, -1) if reverse else range(n_pairs)
    dec = dec_ref[0, 0]
    dect = jnp.concatenate([dec, jnp.zeros((LANES - DEC_ROWS, B_K_WIDTH), F32)], axis=0).T
    heads = []
    for h in range(B_HEADS):
        ks = slice(h * B_KEY_DIM, (h + 1) * B_KEY_DIM)
        vs = slice(h * B_VAL_DIM, (h + 1) * B_VAL_DIM)
        st = st_ref[h]
        o_rows = [None] * n_pairs
        for p in order:
            rows = slice(p * pair_rows, (p + 1) * pair_rows)
            c_first, c_second = (2 * p + 1, 2 * p) if reverse else (2 * p, 2 * p + 1)
            qd = qd_ref[0, rows, ks]
            ket = ket_ref[0, ks, rows]
            vh = v_ref[0, rows, vs]
            a = jnp.where(use_inv, _dot(qd, kit_ref[0, ks, rows]),
                          jnp.where(use_end, _dot(qd, ket), 0.0)).astype(BF16)
            q_pair = jnp.where(second_row, qd.astype(F32) * dec[c_first:c_first + 1, ks], qd.astype(F32))
            k_pair = jnp.where(first_col, ket.astype(F32) * dect[ks, c_second:c_second + 1], ket.astype(F32))
            o_rows[p] = _dot(a, vh) + _dot(q_pair.astype(BF16), st.astype(BF16))
            st = (st * (dect[ks, c_first:c_first + 1] * dect[ks, c_second:c_second + 1])
                  + _dot(k_pair.astype(BF16), vh))
        st_ref[h] = st
        heads.append(jnp.concatenate(o_rows, axis=0))
    return jnp.concatenate(heads, axis=1)


def _gla_scan_kernel(nt, qdf_ref, kitf_ref, ketf_ref, decf_ref, qdb_ref, kitb_ref, ketb_ref, decb_ref,
                     v_ref, sf_ref, sb_ref, sg_ref, x_ref, m_ref, hn_ref, w_ref, fn_ref,
                     o_ref, st_ref, of_ref):
    j = pl.program_id(1)

    @pl.when(j == 0)
    def _():
        st_ref[...] = sf_ref[0]

    @pl.when(j == nt)
    def _():
        st_ref[...] = sb_ref[0]

    @pl.when(j < nt)
    def _():
        of_ref[j] = _scan_tile(False, qdf_ref, kitf_ref, ketf_ref, decf_ref, v_ref, st_ref)

    @pl.when(j >= nt)
    def _():
        o = _scan_tile(True, qdb_ref, kitb_ref, ketb_ref, decb_ref, v_ref, st_ref) + of_ref[2 * nt - 1 - j]
        normed = []
        for h in range(B_HEADS):
            oh = o[:, h * B_VAL_DIM:(h + 1) * B_VAL_DIM]
            normed.append(oh * lax.rsqrt(jnp.mean(oh * oh, axis=-1, keepdims=True) + EPS))
        of = jnp.concatenate(normed, axis=1) * hn_ref[...]
        y = _dot((of * sg_ref[0].astype(F32)).astype(BF16), w_ref[...])
        xn = x_ref[0] + m_ref[0][:, 2 * D:3 * D] * y
        ms = jnp.mean(xn * xn, axis=-1, keepdims=True)
        o_ref[0] = xn * lax.rsqrt(ms + EPS) * fn_ref[...]


def _gla_scan(fwd, bwd, v, s_f, s_b, sg, x, mod3, head_g, w_out, final_g):
    b, s, _ = v.shape
    assert SCAN_TILE == ROW_TILE
    nt = s // SCAN_TILE
    t_fwd = lambda j: jnp.minimum(j, nt - 1)
    t_bwd = lambda j: 2 * nt - 1 - jnp.maximum(j, nt)
    t_both = lambda j: jnp.where(j < nt, j, 2 * nt - 1 - j)

    def direction(t):
        rows = pl.BlockSpec((1, SCAN_TILE, B_K_WIDTH), lambda bi, j: (bi, t(j), 0))
        cols = pl.BlockSpec((1, B_K_WIDTH, SCAN_TILE), lambda bi, j: (bi, 0, t(j)))
        decs = pl.BlockSpec((1, 1, DEC_ROWS, B_K_WIDTH), lambda bi, j: (bi, t(j), 0, 0))
        return [rows, cols, cols, decs]

    const = lambda shape: pl.BlockSpec(shape, lambda bi, j: (0,) * len(shape))
    state = pl.BlockSpec((1, B_HEADS, B_KEY_DIM, B_VAL_DIM), lambda bi, j: (bi, 0, 0, 0))
    out_rows = lambda w_: pl.BlockSpec((1, SCAN_TILE, w_), lambda bi, j: (bi, t_bwd(j), 0))
    in_specs = direction(t_fwd) + direction(t_bwd) + [
        pl.BlockSpec((1, SCAN_TILE, B_V_WIDTH), lambda bi, j: (bi, t_both(j), 0)),
        state, state, out_rows(B_V_WIDTH), out_rows(D),
        pl.BlockSpec((1, 1, 3 * D), lambda bi, j: (bi, 0, 0)),
        const((1, B_V_WIDTH)), const((B_V_WIDTH, D)), const((1, D))]
    return pl.pallas_call(
        functools.partial(_gla_scan_kernel, nt),
        grid=(b, 2 * nt),
        in_specs=in_specs,
        out_specs=out_rows(D),
        out_shape=jax.ShapeDtypeStruct((b, s, D), F32),
        scratch_shapes=[pltpu.VMEM((B_HEADS, B_KEY_DIM, B_VAL_DIM), F32),
                        pltpu.VMEM((nt, SCAN_TILE, B_V_WIDTH), F32)],
        compiler_params=_params("parallel", "arbitrary"),
        name="gla_scan",
    )(*fwd, *bwd, v, s_f, s_b, sg, x, mod3, head_g.reshape(1, B_V_WIDTH), w_out, final_g.reshape(1, D))


def _rope_tables(n_tokens):
    rows_n = n_tokens // GRID_W
    row = jnp.repeat(jnp.arange(rows_n, dtype=F32), GRID_W)
    col = jnp.tile(jnp.arange(GRID_W, dtype=F32), rows_n)
    inv_freq = ROPE_BASE ** (-jnp.arange(ROPE_FREQS, dtype=F32) / ROPE_FREQS)
    ang = jnp.stack([row[:, None] * inv_freq, col[:, None] * inv_freq], axis=1)
    cos, sin = jnp.cos(ang), jnp.sin(ang)
    zero = jnp.zeros_like(sin)
    tile = lambda t: jnp.tile(t.reshape(n_tokens, A_HEAD_DIM), (1, LANES // A_HEAD_DIM))
    return (tile(jnp.stack([cos, cos], axis=2)),
            tile(jnp.stack([-sin, zero], axis=2)),
            tile(jnp.stack([zero, sin], axis=2)))


def _head_pair_perm():
    cols = []
    for p in range(A_GROUP):
        for h in (p, p + A_GROUP):
            cols.extend(range(h * A_HEAD_DIM, (h + 1) * A_HEAD_DIM))
    heads = [h for p in range(A_GROUP) for h in (p, p + A_GROUP)]
    return jnp.asarray(cols, dtype=jnp.int32), jnp.asarray(heads, dtype=jnp.int32)


def kernel(x, c, ctx, c_ctx, l0_norm_g, l0_w_ada, l0_b_ada, l0_w_in, l0_sink, l0_w_out, l1_norm_g, l1_w_ada, l1_b_ada, l1_w_in, l1_wa1_f, l1_wa2_f, l1_ba_f, l1_wa1_b, l1_wa2_b, l1_ba_b, l1_head_norm_g, l1_w_out, final_norm_g):
    b, s, _ = x.shape
    ctx_row = b

    cvec = jnp.concatenate([c, c_ctx[None, :], jnp.zeros((MOD_ROWS - b - 1, D), F32)], axis=0)
    mod0 = _modulation(cvec, l0_w_ada, l0_b_ada).reshape(MOD_ROWS, 1, 3 * D)
    mod1 = _modulation(cvec, l1_w_ada, l1_b_ada).reshape(MOD_ROWS, 1, 3 * D)

    cols, heads = _head_pair_perm()
    q_end, kv_end = A_WIDTH, A_WIDTH + 2 * A_KV_WIDTH
    w0 = jnp.concatenate([l0_w_in[:, :q_end][:, cols], l0_w_in[:, q_end:kv_end],
                          l0_w_in[:, kv_end:][:, cols]], axis=1).astype(BF16)
    w0_out = l0_w_out[cols, :].astype(BF16)
    sink = l0_sink.astype(F32)[heads]
    qt, k, vt, sg = _inproj_attn(x, mod0, None, l0_norm_g, w0, _rope_tables(s))
    qct, kc, vct, sgc = _inproj_attn(ctx, mod0, ctx_row, l0_norm_g, w0, None)
    x1 = _attention(sink, qt, k, vt, kc, vct, sg, x, mod0, None, w0_out, True)
    xc1 = _attention(sink, qct, None, None, kc, vct, sgc, ctx, mod0, ctx_row, w0_out, False)

    w1 = l1_w_in.astype(BF16)
    wa1 = jnp.concatenate([l1_wa1_f, l1_wa1_b, jnp.zeros((D, LANES - 2 * GATE_RANK), F32)], axis=1).astype(BF16)
    wa2 = jnp.zeros((LANES, 2 * B_K_WIDTH), F32)
    wa2 = wa2.at[0:GATE_RANK, 0:B_K_WIDTH].set(l1_wa2_f)
    wa2 = wa2.at[GATE_RANK:2 * GATE_RANK, B_K_WIDTH:].set(l1_wa2_b).astype(BF16)
    ba = jnp.concatenate([l1_ba_f, l1_ba_b]).reshape(1, 2 * B_K_WIDTH)
    kec_f, kec_b, vcx = _inproj_gla(xc1, mod1, ctx_row, l1_norm_g, w1, wa1, wa2, ba, ctx.shape[1], False)
    s_f, s_b = _gla_state(kec_f, kec_b, vcx)
    outs = _inproj_gla(x1, mod1, None, l1_norm_g, w1, wa1, wa2, ba, CHUNK, True)
    v1, sg1 = outs[8:]
    return _gla_scan(outs[0:4], outs[4:8], v1, s_f, s_b, sg1, x1, mod1,
                     l1_head_norm_g, l1_w_out.astype(BF16), final_norm_g)
```

```python
import functools

import jax
import jax.numpy as jnp
from jax import lax
from jax.experimental import pallas as pl
from jax.experimental.pallas import tpu as pltpu

F32 = jnp.float32
BF16 = jnp.bfloat16

D = 1024
GRID_W = 64
EPS = 1e-6
NEG_INF = -1e30

A_HEADS = 16
A_KV_HEADS = 2
A_GROUP = A_HEADS // A_KV_HEADS
A_HEAD_DIM = 64
A_WIDTH = A_HEADS * A_HEAD_DIM
A_KV_WIDTH = A_KV_HEADS * A_HEAD_DIM
BLOCK = 128
ROPE_BASE = 10000.0
ROPE_FREQS = A_HEAD_DIM // 4
Q_SCALE = A_HEAD_DIM ** -0.5
LOG2_E = 1.4426950408889634

B_HEADS = 4
B_K_WIDTH = D // 2
B_V_WIDTH = D
B_KEY_DIM = B_K_WIDTH // B_HEADS
B_VAL_DIM = B_V_WIDTH // B_HEADS
GATE_RANK = 16
GATE_TEMP = 16.0
CHUNK = 64
K_SCALE = B_KEY_DIM ** -0.5

LANES = 128
MXU_DIM = 256
MOD_ROWS = 16
ROW_TILE = 512
SCAN_TILE = 512
DEC_ROWS = 8
SUM_ROWS = 16
A_BLOCKS_PER_STEP = 2
VMEM_LIMIT = 48 * 1024 * 1024


def _params(*sem):
    return pltpu.CompilerParams(dimension_semantics=sem, vmem_limit_bytes=VMEM_LIMIT)


def _silu(x):
    return x / (1.0 + jnp.exp(-x))


def _dot(a, b):
    return jnp.dot(a, b, preferred_element_type=F32)


def _dot_nt(a, b):
    return lax.dot_general(a, b, (((1,), (1,)), ((), ())), preferred_element_type=F32)


def _dot_tn(a, b):
    return lax.dot_general(a, b, (((0,), (0,)), ((), ())), preferred_element_type=F32)


def _norm_mod(x, g, m):
    ms = jnp.mean(x * x, axis=-1, keepdims=True)
    y = x * lax.rsqrt(ms + EPS) * g
    return y * (1.0 + m[:, D:2 * D]) + m[:, 0:D]


def _mod_kernel(c_ref, w_ref, b_ref, o_ref):
    s = _silu(c_ref[...])
    o_ref[...] = _dot(s.astype(BF16), w_ref[...].astype(BF16)) + b_ref[...]


def _modulation(cvec, w_ada, b_ada):
    n = w_ada.shape[1] // D
    return pl.pallas_call(
        _mod_kernel,
        grid=(n,),
        in_specs=[
            pl.BlockSpec((MOD_ROWS, D), lambda j: (0, 0)),
            pl.BlockSpec((D, D), lambda j: (0, j)),
            pl.BlockSpec((1, D), lambda j: (0, j)),
        ],
        out_specs=pl.BlockSpec((MOD_ROWS, D), lambda j: (0, j)),
        out_shape=jax.ShapeDtypeStruct((MOD_ROWS, n * D), F32),
        compiler_params=_params("arbitrary"),
        name="modulation",
    )(cvec, w_ada, b_ada.reshape(1, n * D))


def _rope(t, cos, sin_hi, sin_lo):
    return t * cos + pltpu.roll(t, LANES - ROPE_FREQS, 1) * sin_hi + pltpu.roll(t, ROPE_FREQS, 1) * sin_lo


def _inproj_attn_kernel(rope, x_ref, m_ref, g_ref, w_ref, *rest):
    if rope:
        cos_ref, shi_ref, slo_ref, qt_ref, k_ref, vt_ref, sg_ref = rest
        cos, shi, slo = cos_ref[...], shi_ref[...], slo_ref[...]
    else:
        qt_ref, k_ref, vt_ref, sg_ref = rest
    hb = _norm_mod(x_ref[0], g_ref[...], m_ref[0]).astype(BF16)
    q = _dot(hb, w_ref[:, 0:A_WIDTH])
    for j in range(A_WIDTH // LANES):
        qj = q[:, j * LANES:(j + 1) * LANES]
        if rope:
            qj = _rope(qj, cos, shi, slo)
        qt_ref[0, j * LANES:(j + 1) * LANES, :] = (qj * (Q_SCALE * LOG2_E)).T.astype(BF16)
    kv = _dot(hb, w_ref[:, A_WIDTH:A_WIDTH + 2 * A_KV_WIDTH])
    k = kv[:, 0:A_KV_WIDTH]
    if rope:
        k = _rope(k, cos, shi, slo)
    k_ref[0] = k.astype(BF16)
    vt_ref[0] = kv[:, A_KV_WIDTH:].T.astype(BF16)
    g = _dot(hb, w_ref[:, A_WIDTH + 2 * A_KV_WIDTH:])
    sg_ref[0] = _silu(g).astype(BF16)


def _inproj_attn(x, mod3, mod_row, norm_g, w, tables):
    b, s, _ = x.shape
    tm = min(ROW_TILE, s)
    rope = tables is not None
    n_w = w.shape[1]
    if mod_row is None:
        mod_map = lambda bi, i: (bi, 0, 0)
    else:
        mod_map = lambda bi, i: (mod_row, 0, 0)
    in_specs = [
        pl.BlockSpec((1, tm, D), lambda bi, i: (bi, i, 0)),
        pl.BlockSpec((1, 1, 3 * D), mod_map),
        pl.BlockSpec((1, D), lambda bi, i: (0, 0)),
        pl.BlockSpec((D, n_w), lambda bi, i: (0, 0)),
    ]
    args = [x, mod3, norm_g.reshape(1, D), w]
    if rope:
        in_specs += [pl.BlockSpec((tm, LANES), lambda bi, i: (i, 0))] * 3
        args += list(tables)
    out_shape = (
        jax.ShapeDtypeStruct((b, A_WIDTH, s), BF16),
        jax.ShapeDtypeStruct((b, s, A_KV_WIDTH), BF16),
        jax.ShapeDtypeStruct((b, A_KV_WIDTH, s), BF16),
        jax.ShapeDtypeStruct((b, s, A_WIDTH), BF16),
    )
    out_specs = (
        pl.BlockSpec((1, A_WIDTH, tm), lambda bi, i: (bi, 0, i)),
        pl.BlockSpec((1, tm, A_KV_WIDTH), lambda bi, i: (bi, i, 0)),
        pl.BlockSpec((1, A_KV_WIDTH, tm), lambda bi, i: (bi, 0, i)),
        pl.BlockSpec((1, tm, A_WIDTH), lambda bi, i: (bi, i, 0)),
    )
    return pl.pallas_call(
        functools.partial(_inproj_attn_kernel, rope),
        grid=(b, s // tm),
        in_specs=in_specs,
        out_specs=out_specs,
        out_shape=out_shape,
        compiler_params=_params("parallel", "parallel"),
        name="inproj_attn_rope" if rope else "inproj_attn_ctx",
    )(*args)


def _attn_block(kwin, vtw, qts, sinks, ok_prev, ok_next):
    n_keys, nq = kwin.shape[0], qts.shape[1]
    lane = lax.broadcasted_iota(jnp.int32, kwin.shape, 1)
    kbd = jnp.concatenate([jnp.where(lane < A_HEAD_DIM, kwin, jnp.zeros_like(kwin)),
                           jnp.where(lane >= A_HEAD_DIM, kwin, jnp.zeros_like(kwin))], axis=0)
    vrow = lax.broadcasted_iota(jnp.int32, vtw.shape, 0)
    vbd = jnp.concatenate([jnp.where(vrow < A_HEAD_DIM, vtw, jnp.zeros_like(vtw)),
                           jnp.where(vrow >= A_HEAD_DIM, vtw, jnp.zeros_like(vtw))], axis=1)
    srow = lax.broadcasted_iota(jnp.int32, (SUM_ROWS, 2 * n_keys), 0)
    scol = lax.broadcasted_iota(jnp.int32, (SUM_ROWS, 2 * n_keys), 1) // n_keys
    vbd = jnp.concatenate([vbd, jnp.where(srow == scol, 1.0, 0.0).astype(BF16)], axis=0)
    head_a = lax.broadcasted_iota(jnp.int32, (LANES, nq), 0) < A_HEAD_DIM

    st = _dot(kbd, qts)
    probs, esink = [], []
    for hh in range(2):
        sh = st[hh * n_keys:(hh + 1) * n_keys]
        if ok_prev is not None:
            parts = [jnp.where(ok_prev, sh[0:BLOCK], NEG_INF),
                     sh[BLOCK:2 * BLOCK],
                     jnp.where(ok_next, sh[2 * BLOCK:3 * BLOCK], NEG_INF),
                     sh[3 * BLOCK:]]
        else:
            parts = [sh]
        mx = functools.reduce(jnp.maximum, [jnp.max(t, axis=0, keepdims=True) for t in parts])
        mx = jnp.maximum(mx, sinks[hh])
        probs += [jnp.exp2(t - mx).astype(BF16) for t in parts]
        esink.append(jnp.exp2(sinks[hh] - mx))
    ot = _dot(vbd, jnp.concatenate(probs, axis=0))
    inv = [1.0 / (ot[LANES + hh:LANES + hh + 1] + esink[hh]) for hh in range(2)]
    return ot[0:LANES] * jnp.where(head_a, inv[0], inv[1])


def _attn_kernel(local, n_steps, sink_ref, qt_ref, *rest):
    if local:
        (kp_ref, kc_ref, kn_ref, vp_ref, vc_ref, vn_ref, kx_ref, vx_ref,
         sg_ref, x_ref, m_ref, w_ref, o_ref) = rest
        k_blocks = [kp_ref[0, BLOCK:], kc_ref[0, 0:BLOCK], kc_ref[0, BLOCK:], kn_ref[0, 0:BLOCK]]
        v_blocks = [vp_ref[0, :, BLOCK:], vc_ref[0, :, 0:BLOCK], vc_ref[0, :, BLOCK:], vn_ref[0, :, 0:BLOCK]]
    else:
        kx_ref, vx_ref, sg_ref, x_ref, m_ref, w_ref, o_ref = rest
    step = pl.program_id(1)
    nq = A_GROUP * BLOCK
    chunk_of = lax.broadcasted_iota(jnp.int32, (1, nq), 1) // BLOCK
    sinks = []
    for hh in range(2):
        sk = jnp.full((1, nq), sink_ref[hh] * LOG2_E, F32)
        for c in range(1, A_GROUP):
            sk = jnp.where(chunk_of == c, sink_ref[2 * c + hh] * LOG2_E, sk)
        sinks.append(sk)
    if local:
        kj = lax.broadcasted_iota(jnp.int32, (BLOCK, nq), 0)
        qi = lax.broadcasted_iota(jnp.int32, (BLOCK, nq), 1) % BLOCK
        edge = [(step > 0, None), (None, step < n_steps - 1)]

    rows_out = []
    for blk in range(A_BLOCKS_PER_STEP):
        qts = jnp.concatenate([qt_ref[0, c * LANES:(c + 1) * LANES, blk * BLOCK:(blk + 1) * BLOCK]
                               for c in range(A_GROUP)], axis=1)
        if local:
            kwin = jnp.concatenate(k_blocks[blk:blk + 3] + [kx_ref[0]], axis=0)
            vtw = jnp.concatenate(v_blocks[blk:blk + 3] + [vx_ref[0]], axis=1)
            has_prev, has_next = edge[blk]
            ok_prev = (kj >= qi) if has_prev is None else (kj >= qi) & has_prev
            ok_next = (kj <= qi) if has_next is None else (kj <= qi) & has_next
        else:
            kwin, vtw, ok_prev, ok_next = kx_ref[0], vx_ref[0], None, None
        ot = _attn_block(kwin, vtw, qts, sinks, ok_prev, ok_next)
        rows = slice(blk * BLOCK, (blk + 1) * BLOCK)
        outs = []
        for c in range(A_GROUP):
            sg = sg_ref[0, rows, c * LANES:(c + 1) * LANES].astype(F32)
            outs.append((ot[:, c * BLOCK:(c + 1) * BLOCK].T * sg).astype(BF16))
        rows_out.append(jnp.concatenate(outs, axis=1))
    y = _dot(jnp.concatenate(rows_out, axis=0), w_ref[...])
    gate = m_ref[0][:, 2 * D:3 * D]
    o_ref[0] = x_ref[0] + gate * y


def _attention(sink, qt, k, vt, kx, vxt, sg, x, mod3, mod_row, w_out, local):
    b, s, _ = sg.shape
    rows = A_BLOCKS_PER_STEP * BLOCK
    assert A_BLOCKS_PER_STEP == 2 and s % rows == 0
    ns = s // rows
    n_ctx = kx.shape[1]
    blk = lambda w: pl.BlockSpec((1, rows, w), lambda bi, i: (bi, i, 0))
    blk_t = lambda w: pl.BlockSpec((1, w, rows), lambda bi, i: (bi, 0, i))
    if mod_row is None:
        mod_map = lambda bi, i: (bi, 0, 0)
    else:
        mod_map = lambda bi, i: (mod_row, 0, 0)
    in_specs = [pl.BlockSpec(memory_space=pltpu.SMEM), blk_t(A_WIDTH)]
    args = [sink, qt]
    if local:
        lo = lambda i: jnp.maximum(i - 1, 0)
        hi = lambda i: jnp.minimum(i + 1, ns - 1)
        in_specs += [pl.BlockSpec((1, rows, A_KV_WIDTH), lambda bi, i: (bi, lo(i), 0)),
                     blk(A_KV_WIDTH),
                     pl.BlockSpec((1, rows, A_KV_WIDTH), lambda bi, i: (bi, hi(i), 0)),
                     pl.BlockSpec((1, A_KV_WIDTH, rows), lambda bi, i: (bi, 0, lo(i))),
                     blk_t(A_KV_WIDTH),
                     pl.BlockSpec((1, A_KV_WIDTH, rows), lambda bi, i: (bi, 0, hi(i)))]
        args += [k, k, k, vt, vt, vt]
    in_specs += [pl.BlockSpec((1, n_ctx, A_KV_WIDTH), lambda bi, i: (bi, 0, 0)),
                 pl.BlockSpec((1, A_KV_WIDTH, n_ctx), lambda bi, i: (bi, 0, 0)),
                 blk(A_WIDTH), blk(D),
                 pl.BlockSpec((1, 1, 3 * D), mod_map),
                 pl.BlockSpec((A_WIDTH, D), lambda bi, i: (0, 0))]
    args += [kx, vxt, sg, x, mod3, w_out]
    return pl.pallas_call(
        functools.partial(_attn_kernel, local, ns),
        grid=(b, ns),
        in_specs=in_specs,
        out_specs=blk(D),
        out_shape=jax.ShapeDtypeStruct((b, s, D), F32),
        compiler_params=_params("parallel", "parallel"),
        name="attn_local" if local else "attn_ctx",
    )(*args)


def _chunk_cumsum(x, tri):
    hi = x.astype(BF16)
    lo = (x - hi.astype(F32)).astype(BF16)
    return _dot(tri, hi) + _dot(tri, lo)


def _inproj_gla_kernel(chunk, with_q, x_ref, m_ref, g_ref, w_ref, wa1_ref, wa2_ref, ba_ref, tri_ref, *outs):
    if with_q:
        per_dir = (outs[0:4], outs[4:8])
        v_ref, sg_ref = outs[8:]
    else:
        per_dir = ((outs[0],), (outs[1],))
        v_ref = outs[2]
    tm = x_ref.shape[1]
    hb = _norm_mod(x_ref[0], g_ref[...], m_ref[0]).astype(BF16)
    r = _dot(hb, wa1_ref[...])
    z = _dot(r.astype(BF16), wa2_ref[...]) + ba_ref[...]
    e = jnp.exp2(jnp.abs(z) * -LOG2_E)
    la = (jnp.minimum(z, 0.0) * LOG2_E - jnp.log2(1.0 + e)) * (1.0 / GATE_TEMP)
    k = _dot(hb, w_ref[:, B_K_WIDTH:2 * B_K_WIDTH])
    if with_q:
        q = _dot(hb, w_ref[:, 0:B_K_WIDTH]) * K_SCALE
    n_chunks = tm // chunk
    for reverse in (False, True):
        lad = la[:, B_K_WIDTH:] if reverse else la[:, 0:B_K_WIDTH]
        refs = per_dir[1] if reverse else per_dir[0]
        tots, kis, kes = [], [], []
        tri = tri_ref[1 if reverse else 0]
        tb = tri.shape[0]
        cum_all = jnp.concatenate([_chunk_cumsum(lad[r * tb:(r + 1) * tb], tri) for r in range(tm // tb)], axis=0)
        for c in range(n_chunks):
            rows = slice(c * chunk, (c + 1) * chunk)
            cum = cum_all[rows]
            tot = cum[0:1] if reverse else cum[chunk - 1:chunk]
            ke = k[rows] * jnp.exp2(tot - cum)
            if not with_q:
                refs[0][0, rows] = ke.astype(BF16)
                continue
            refs[0][0, rows] = (q[rows] * jnp.exp2(cum)).astype(BF16)
            kis.append(k[rows] * jnp.exp2(-cum))
            kes.append(ke)
            tots.append(tot)
            if c % 2 == 1:
                pair = slice((c - 1) * chunk, (c + 1) * chunk)
                refs[1][0, :, pair] = jnp.concatenate(kis[-2:], axis=0).T.astype(BF16)
                refs[2][0, :, pair] = jnp.concatenate(kes[-2:], axis=0).T.astype(BF16)
        if with_q:
            pad = [jnp.zeros((DEC_ROWS - n_chunks, B_K_WIDTH), F32)] if n_chunks < DEC_ROWS else []
            refs[3][0, 0] = jnp.exp2(jnp.concatenate(tots + pad, axis=0))
    if with_q:
        sg_ref[0] = _silu(_dot(hb, w_ref[:, 2 * B_K_WIDTH + B_V_WIDTH:])).astype(BF16)
    v_ref[0] = _dot(hb, w_ref[:, 2 * B_K_WIDTH:2 * B_K_WIDTH + B_V_WIDTH]).astype(BF16)


def _inproj_gla(x, mod3, mod_row, norm_g, w, wa1, wa2, ba, chunk, with_q):
    b, s, _ = x.shape
    tm = min(ROW_TILE, s)
    nt = s // tm
    cpt = tm // chunk
    tb = max(chunk, MXU_DIM)
    assert tm % tb == 0 and tb % chunk == 0
    if mod_row is None:
        mod_map = lambda bi, i: (bi, 0, 0)
    else:
        mod_map = lambda bi, i: (mod_row, 0, 0)
    const = lambda shape: pl.BlockSpec(shape, lambda bi, i: (0,) * len(shape))
    in_specs = [
        pl.BlockSpec((1, tm, D), lambda bi, i: (bi, i, 0)),
        pl.BlockSpec((1, 1, 3 * D), mod_map),
        const((1, D)), const(w.shape), const(wa1.shape), const(wa2.shape), const(ba.shape),
        const((2, tb, tb)),
    ]
    t_idx = jnp.arange(tb)
    same = (t_idx[:, None] // chunk) == (t_idx[None, :] // chunk)
    lower = same & (t_idx[None, :] <= t_idx[:, None])
    tri = jnp.stack([lower, lower.T]).astype(BF16)
    rows = lambda w_: (jax.ShapeDtypeStruct((b, s, w_), BF16),
                       pl.BlockSpec((1, tm, w_), lambda bi, i: (bi, i, 0)))
    cols = (jax.ShapeDtypeStruct((b, B_K_WIDTH, s), BF16),
            pl.BlockSpec((1, B_K_WIDTH, tm), lambda bi, i: (bi, 0, i)))
    decs = (jax.ShapeDtypeStruct((b, nt, DEC_ROWS, B_K_WIDTH), F32),
            pl.BlockSpec((1, 1, DEC_ROWS, B_K_WIDTH), lambda bi, i: (bi, i, 0, 0)))
    if with_q:
        assert cpt <= DEC_ROWS and cpt % 2 == 0
        outs = [rows(B_K_WIDTH), cols, cols, decs] * 2 + [rows(B_V_WIDTH)] * 2
    else:
        outs = [rows(B_K_WIDTH)] * 2 + [rows(B_V_WIDTH)]
    return pl.pallas_call(
        functools.partial(_inproj_gla_kernel, chunk, with_q),
        grid=(b, nt),
        in_specs=in_specs,
        out_specs=tuple(o[1] for o in outs),
        out_shape=tuple(o[0] for o in outs),
        compiler_params=_params("parallel", "parallel"),
        name="inproj_gla" if with_q else "inproj_gla_ctx",
    )(x, mod3, norm_g.reshape(1, D), w, wa1, wa2, ba, tri)


def _gla_state_kernel(kf_ref, kb_ref, v_ref, sf_ref, sb_ref):
    for h in range(B_HEADS):
        vh = v_ref[0, :, h * B_VAL_DIM:(h + 1) * B_VAL_DIM]
        ks = slice(h * B_KEY_DIM, (h + 1) * B_KEY_DIM)
        sf_ref[0, h] = _dot_tn(kf_ref[0, :, ks], vh)
        sb_ref[0, h] = _dot_tn(kb_ref[0, :, ks], vh)


def _gla_state(ke_f, ke_b, v):
    b, n, _ = v.shape
    st = jax.ShapeDtypeStruct((b, B_HEADS, B_KEY_DIM, B_VAL_DIM), F32)
    st_spec = pl.BlockSpec((1, B_HEADS, B_KEY_DIM, B_VAL_DIM), lambda bi: (bi, 0, 0, 0))
    return pl.pallas_call(
        _gla_state_kernel,
        grid=(b,),
        in_specs=[pl.BlockSpec((1, n, B_K_WIDTH), lambda bi: (bi, 0, 0))] * 2
        + [pl.BlockSpec((1, n, B_V_WIDTH), lambda bi: (bi, 0, 0))],
        out_specs=(st_spec, st_spec),
        out_shape=(st, st),
        compiler_params=_params("parallel"),
        name="gla_ctx_state",
    )(ke_f, ke_b, v)


def _scan_tile(reverse, qd_ref, kit_ref, ket_ref, dec_ref, v_ref, st_ref):
    pair_rows = 2 * CHUNK
    ti = lax.broadcasted_iota(jnp.int32, (pair_rows, pair_rows), 0)
    si = lax.broadcasted_iota(jnp.int32, (pair_rows, pair_rows), 1)
    same = (ti // CHUNK) == (si // CHUNK)
    if reverse:
        use_inv = same & (ti <= si)
        use_end = (ti < CHUNK) & (si >= CHUNK)
        second_row = lax.broadcasted_iota(jnp.int32, (pair_rows, B_KEY_DIM), 0) < CHUNK
        first_col = lax.broadcasted_iota(jnp.int32, (B_KEY_DIM, pair_rows), 1) >= CHUNK
    else:
        use_inv = same & (ti >= si)
        use_end = (ti >= CHUNK) & (si < CHUNK)
        second_row = lax.broadcasted_iota(jnp.int32, (pair_rows, B_KEY_DIM), 0) >= CHUNK
        first_col = lax.broadcasted_iota(jnp.int32, (B_KEY_DIM, pair_rows), 1) < CHUNK
    n_pairs = SCAN_TILE // pair_rows
    order = range(n_pairs - 1, -1, -1) if reverse else range(n_pairs)
    dec = dec_ref[0, 0]
    dect = jnp.concatenate([dec, jnp.zeros((LANES - DEC_ROWS, B_K_WIDTH), F32)], axis=0).T
    heads = []
    for h in range(B_HEADS):
        ks = slice(h * B_KEY_DIM, (h + 1) * B_KEY_DIM)
        vs = slice(h * B_VAL_DIM, (h + 1) * B_VAL_DIM)
        st = st_ref[h]
        o_rows = [None] * n_pairs
        for p in order:
            rows = slice(p * pair_rows, (p + 1) * pair_rows)
            c_first, c_second = (2 * p + 1, 2 * p) if reverse else (2 * p, 2 * p + 1)
            qd = qd_ref[0, rows, ks]
            ket = ket_ref[0, ks, rows]
            vh = v_ref[0, rows, vs]
            a = jnp.where(use_inv, _dot(qd, kit_ref[0, ks, rows]),
                          jnp.where(use_end, _dot(qd, ket), 0.0)).astype(BF16)
            q_pair = jnp.where(second_row, qd.astype(F32) * dec[c_first:c_first + 1, ks], qd.astype(F32))
            k_pair = jnp.where(first_col, ket.astype(F32) * dect[ks, c_second:c_second + 1], ket.astype(F32))
            o_rows[p] = _dot(a, vh) + _dot(q_pair.astype(BF16), st.astype(BF16))
            st = (st * (dect[ks, c_first:c_first + 1] * dect[ks, c_second:c_second + 1])
                  + _dot(k_pair.astype(BF16), vh))
        st_ref[h] = st
        heads.append(jnp.concatenate(o_rows, axis=0))
    return jnp.concatenate(heads, axis=1)


def _gla_scan_kernel(nt, qdf_ref, kitf_ref, ketf_ref, decf_ref, qdb_ref, kitb_ref, ketb_ref, decb_ref,
                     v_ref, sf_ref, sb_ref, sg_ref, x_ref, m_ref, hn_ref, w_ref, fn_ref,
                     o_ref, st_ref, of_ref):
    j = pl.program_id(1)

    @pl.when(j == 0)
    def _():
        st_ref[...] = sf_ref[0]

    @pl.when(j == nt)
    def _():
        st_ref[...] = sb_ref[0]

    @pl.when(j < nt)
    def _():
        of_ref[j] = _scan_tile(False, qdf_ref, kitf_ref, ketf_ref, decf_ref, v_ref, st_ref)

    @pl.when(j >= nt)
    def _():
        o = _scan_tile(True, qdb_ref, kitb_ref, ketb_ref, decb_ref, v_ref, st_ref) + of_ref[2 * nt - 1 - j]
        normed = []
        for h in range(B_HEADS):
            oh = o[:, h * B_VAL_DIM:(h + 1) * B_VAL_DIM]
            normed.append(oh * lax.rsqrt(jnp.mean(oh * oh, axis=-1, keepdims=True) + EPS))
        of = jnp.concatenate(normed, axis=1) * hn_ref[...]
        y = _dot((of * sg_ref[0].astype(F32)).astype(BF16), w_ref[...])
        xn = x_ref[0] + m_ref[0][:, 2 * D:3 * D] * y
        ms = jnp.mean(xn * xn, axis=-1, keepdims=True)
        o_ref[0] = xn * lax.rsqrt(ms + EPS) * fn_ref[...]


def _gla_scan(fwd, bwd, v, s_f, s_b, sg, x, mod3, head_g, w_out, final_g):
    b, s, _ = v.shape
    assert SCAN_TILE == ROW_TILE
    nt = s // SCAN_TILE
    t_fwd = lambda j: jnp.minimum(j, nt - 1)
    t_bwd = lambda j: 2 * nt - 1 - jnp.maximum(j, nt)
    t_both = lambda j: jnp.where(j < nt, j, 2 * nt - 1 - j)

    def direction(t):
        rows = pl.BlockSpec((1, SCAN_TILE, B_K_WIDTH), lambda bi, j: (bi, t(j), 0))
        cols = pl.BlockSpec((1, B_K_WIDTH, SCAN_TILE), lambda bi, j: (bi, 0, t(j)))
        decs = pl.BlockSpec((1, 1, DEC_ROWS, B_K_WIDTH), lambda bi, j: (bi, t(j), 0, 0))
        return [rows, cols, cols, decs]

    const = lambda shape: pl.BlockSpec(shape, lambda bi, j: (0,) * len(shape))
    state = pl.BlockSpec((1, B_HEADS, B_KEY_DIM, B_VAL_DIM), lambda bi, j: (bi, 0, 0, 0))
    out_rows = lambda w_: pl.BlockSpec((1, SCAN_TILE, w_), lambda bi, j: (bi, t_bwd(j), 0))
    in_specs = direction(t_fwd) + direction(t_bwd) + [
        pl.BlockSpec((1, SCAN_TILE, B_V_WIDTH), lambda bi, j: (bi, t_both(j), 0)),
        state, state, out_rows(B_V_WIDTH), out_rows(D),
        pl.BlockSpec((1, 1, 3 * D), lambda bi, j: (bi, 0, 0)),
        const((1, B_V_WIDTH)), const((B_V_WIDTH, D)), const((1, D))]
    return pl.pallas_call(
        functools.partial(_gla_scan_kernel, nt),
        grid=(b, 2 * nt),
        in_specs=in_specs,
        out_specs=out_rows(D),
        out_shape=jax.ShapeDtypeStruct((b, s, D), F32),
        scratch_shapes=[pltpu.VMEM((B_HEADS, B_KEY_DIM, B_VAL_DIM), F32),
                        pltpu.VMEM((nt, SCAN_TILE, B_V_WIDTH), F32)],
        compiler_params=_params("parallel", "arbitrary"),
        name="gla_scan",
    )(*fwd, *bwd, v, s_f, s_b, sg, x, mod3, head_g.reshape(1, B_V_WIDTH), w_out, final_g.reshape(1, D))


def _rope_tables(n_tokens):
    rows_n = n_tokens // GRID_W
    row = jnp.repeat(jnp.arange(rows_n, dtype=F32), GRID_W)
    col = jnp.tile(jnp.arange(GRID_W, dtype=F32), rows_n)
    inv_freq = ROPE_BASE ** (-jnp.arange(ROPE_FREQS, dtype=F32) / ROPE_FREQS)
    ang = jnp.stack([row[:, None] * inv_freq, col[:, None] * inv_freq], axis=1)
    cos, sin = jnp.cos(ang), jnp.sin(ang)
    zero = jnp.zeros_like(sin)
    tile = lambda t: jnp.tile(t.reshape(n_tokens, A_HEAD_DIM), (1, LANES // A_HEAD_DIM))
    return (tile(jnp.stack([cos, cos], axis=2)),
            tile(jnp.stack([-sin, zero], axis=2)),
            tile(jnp.stack([zero, sin], axis=2)))


def _head_pair_perm():
    cols = []
    for p in range(A_GROUP):
        for h in (p, p + A_GROUP):
            cols.extend(range(h * A_HEAD_DIM, (h + 1) * A_HEAD_DIM))
    heads = [h for p in range(A_GROUP) for h in (p, p + A_GROUP)]
    return jnp.asarray(cols, dtype=jnp.int32), jnp.asarray(heads, dtype=jnp.int32)


def kernel(x, c, ctx, c_ctx, l0_norm_g, l0_w_ada, l0_b_ada, l0_w_in, l0_sink, l0_w_out, l1_norm_g, l1_w_ada, l1_b_ada, l1_w_in, l1_wa1_f, l1_wa2_f, l1_ba_f, l1_wa1_b, l1_wa2_b, l1_ba_b, l1_head_norm_g, l1_w_out, final_norm_g):
    b, s, _ = x.shape
    ctx_row = b

    cvec = jnp.concatenate([c, c_ctx[None, :], jnp.zeros((MOD_ROWS - b - 1, D), F32)], axis=0)
    mod0 = _modulation(cvec, l0_w_ada, l0_b_ada).reshape(MOD_ROWS, 1, 3 * D)
    mod1 = _modulation(cvec, l1_w_ada, l1_b_ada).reshape(MOD_ROWS, 1, 3 * D)

    cols, heads = _head_pair_perm()
    q_end, kv_end = A_WIDTH, A_WIDTH + 2 * A_KV_WIDTH
    w0 = jnp.concatenate([l0_w_in[:, :q_end][:, cols], l0_w_in[:, q_end:kv_end],
                          l0_w_in[:, kv_end:][:, cols]], axis=1).astype(BF16)
    w0_out = l0_w_out[cols, :].astype(BF16)
    sink = l0_sink.astype(F32)[heads]
    qt, k, vt, sg = _inproj_attn(x, mod0, None, l0_norm_g, w0, _rope_tables(s))
    qct, kc, vct, sgc = _inproj_attn(ctx, mod0, ctx_row, l0_norm_g, w0, None)
    x1 = _attention(sink, qt, k, vt, kc, vct, sg, x, mod0, None, w0_out, True)
    xc1 = _attention(sink, qct, None, None, kc, vct, sgc, ctx, mod0, ctx_row, w0_out, False)

    w1 = l1_w_in.astype(BF16)
    wa1 = jnp.concatenate([l1_wa1_f, l1_wa1_b, jnp.zeros((D, LANES - 2 * GATE_RANK), F32)], axis=1).astype(BF16)
    wa2 = jnp.zeros((LANES, 2 * B_K_WIDTH), F32)
    wa2 = wa2.at[0:GATE_RANK, 0:B_K_WIDTH].set(l1_wa2_f)
    wa2 = wa2.at[GATE_RANK:2 * GATE_RANK, B_K_WIDTH:].set(l1_wa2_b).astype(BF16)
    ba = jnp.concatenate([l1_ba_f, l1_ba_b]).reshape(1, 2 * B_K_WIDTH)
    kec_f, kec_b, vcx = _inproj_gla(xc1, mod1, ctx_row, l1_norm_g, w1, wa1, wa2, ba, ctx.shape[1], False)
    s_f, s_b = _gla_state(kec_f, kec_b, vcx)
    outs = _inproj_gla(x1, mod1, None, l1_norm_g, w1, wa1, wa2, ba, CHUNK, True)
    v1, sg1 = outs[8:]
    return _gla_scan(outs[0:4], outs[4:8], v1, s_f, s_b, sg1, x1, mod1,
                     l1_head_norm_g, l1_w_out.astype(BF16), final_norm_g)
```

```python
import functools

import jax
import jax.numpy as jnp
import numpy as np
from jax import lax
from jax.experimental import pallas as pl
from jax.experimental.pallas import tpu as pltpu

F32 = jnp.float32
BF16 = jnp.bfloat16

D = 1024
GRID_W = 64
EPS = 1e-6
NEG_INF = -1e30

A_HEADS = 16
A_KV_HEADS = 2
A_GROUP = A_HEADS // A_KV_HEADS
A_HEAD_DIM = 64
A_WIDTH = A_HEADS * A_HEAD_DIM
A_KV_WIDTH = A_KV_HEADS * A_HEAD_DIM
BLOCK = 128
ROPE_BASE = 10000.0
ROPE_FREQS = A_HEAD_DIM // 4
Q_SCALE = A_HEAD_DIM ** -0.5
LOG2_E = 1.4426950408889634

B_HEADS = 4
B_K_WIDTH = D // 2
B_V_WIDTH = D
B_KEY_DIM = B_K_WIDTH // B_HEADS
B_VAL_DIM = B_V_WIDTH // B_HEADS
GATE_RANK = 16
GATE_TEMP = 16.0
CHUNK = 64
K_SCALE = B_KEY_DIM ** -0.5

LANES = 128
MXU_DIM = 256
MOD_ROWS = 16
ROW_TILE = 512
SCAN_TILE = 512
DEC_ROWS = 8
SUM_ROWS = 16
A_BLOCKS_PER_STEP = 4
VMEM_LIMIT = 48 * 1024 * 1024


def _params(*sem):
    return pltpu.CompilerParams(dimension_semantics=sem, vmem_limit_bytes=VMEM_LIMIT)


def _silu(x):
    return x / (1.0 + jnp.exp(-x))


def _dot(a, b):
    return jnp.dot(a, b, preferred_element_type=F32)


def _dot_nt(a, b):
    return lax.dot_general(a, b, (((1,), (1,)), ((), ())), preferred_element_type=F32)


def _dot_tn(a, b):
    return lax.dot_general(a, b, (((0,), (0,)), ((), ())), preferred_element_type=F32)


def _norm_mod(x, g, m):
    ms = jnp.mean(x * x, axis=-1, keepdims=True)
    y = x * lax.rsqrt(ms + EPS) * g
    return y * (1.0 + m[:, D:2 * D]) + m[:, 0:D]


def _mod_kernel(c_ref, w_ref, b_ref, o_ref):
    s = _silu(c_ref[...])
    o_ref[...] = _dot(s.astype(BF16), w_ref[...].astype(BF16)) + b_ref[...]


def _modulation(cvec, w_ada, b_ada):
    n = w_ada.shape[1] // D
    return pl.pallas_call(
        _mod_kernel,
        grid=(n,),
        in_specs=[
            pl.BlockSpec((MOD_ROWS, D), lambda j: (0, 0)),
            pl.BlockSpec((D, D), lambda j: (0, j)),
            pl.BlockSpec((1, D), lambda j: (0, j)),
        ],
        out_specs=pl.BlockSpec((MOD_ROWS, D), lambda j: (0, j)),
        out_shape=jax.ShapeDtypeStruct((MOD_ROWS, n * D), F32),
        compiler_params=_params("arbitrary"),
        name="modulation",
    )(cvec, w_ada, b_ada.reshape(1, n * D))


def _rope(t, cos, sin_hi, sin_lo):
    return t * cos + pltpu.roll(t, LANES - ROPE_FREQS, 1) * sin_hi + pltpu.roll(t, ROPE_FREQS, 1) * sin_lo


def _inproj_attn_kernel(rope, x_ref, m_ref, g_ref, wq_ref, wkv_ref, wg_ref, *rest):
    if rope:
        cos_ref, shi_ref, slo_ref, qt_ref, k_ref, vt_ref, sg_ref = rest
        cos, shi, slo = cos_ref[...], shi_ref[...], slo_ref[...]
    else:
        qt_ref, k_ref, vt_ref, sg_ref = rest
    hb = _norm_mod(x_ref[0], g_ref[...], m_ref[0]).astype(BF16)
    q = _dot(hb, wq_ref[...])
    for j in range(A_WIDTH // LANES):
        qj = q[:, j * LANES:(j + 1) * LANES]
        if rope:
            qj = _rope(qj, cos, shi, slo)
        qt_ref[0, j * LANES:(j + 1) * LANES, :] = (qj * (Q_SCALE * LOG2_E)).T.astype(BF16)
    kv = _dot(hb, wkv_ref[...])
    k = kv[:, 0:A_KV_WIDTH]
    if rope:
        k = _rope(k, cos, shi, slo)
    k_ref[0] = k.astype(BF16)
    vt_ref[0] = kv[:, A_KV_WIDTH:].T.astype(BF16)
    g = _dot(hb, wg_ref[...])
    sg_ref[0] = _silu(g).astype(BF16)


def _inproj_attn(x, mod3, mod_row, norm_g, weights, tables):
    b, s, _ = x.shape
    tm = min(ROW_TILE, s)
    rope = tables is not None
    if mod_row is None:
        mod_map = lambda bi, i: (bi, 0, 0)
    else:
        mod_map = lambda bi, i: (mod_row, 0, 0)
    in_specs = [
        pl.BlockSpec((1, tm, D), lambda bi, i: (bi, i, 0)),
        pl.BlockSpec((1, 1, 3 * D), mod_map),
        pl.BlockSpec((1, D), lambda bi, i: (0, 0)),
    ] + [pl.BlockSpec(w.shape, lambda bi, i: (0, 0)) for w in weights]
    args = [x, mod3, norm_g.reshape(1, D), *weights]
    if rope:
        in_specs += [pl.BlockSpec((tm, LANES), lambda bi, i: (i, 0))] * 3
        args += list(tables)
    out_shape = (
        jax.ShapeDtypeStruct((b, A_WIDTH, s), BF16),
        jax.ShapeDtypeStruct((b, s, A_KV_WIDTH), BF16),
        jax.ShapeDtypeStruct((b, A_KV_WIDTH, s), BF16),
        jax.ShapeDtypeStruct((b, s, A_WIDTH), BF16),
    )
    out_specs = (
        pl.BlockSpec((1, A_WIDTH, tm), lambda bi, i: (bi, 0, i)),
        pl.BlockSpec((1, tm, A_KV_WIDTH), lambda bi, i: (bi, i, 0)),
        pl.BlockSpec((1, A_KV_WIDTH, tm), lambda bi, i: (bi, 0, i)),
        pl.BlockSpec((1, tm, A_WIDTH), lambda bi, i: (bi, i, 0)),
    )
    return pl.pallas_call(
        functools.partial(_inproj_attn_kernel, rope),
        grid=(b, s // tm),
        in_specs=in_specs,
        out_specs=out_specs,
        out_shape=out_shape,
        compiler_params=_params("parallel", "parallel"),
        name="inproj_attn_rope" if rope else "inproj_attn_ctx",
    )(*args)


def _attn_block(kwin, vtw, qts, sinks, ok_prev, ok_next):
    n_keys, nq = kwin.shape[0], qts.shape[1]
    lane = lax.broadcasted_iota(jnp.int32, kwin.shape, 1)
    kbd = jnp.concatenate([jnp.where(lane < A_HEAD_DIM, kwin, jnp.zeros_like(kwin)),
                           jnp.where(lane >= A_HEAD_DIM, kwin, jnp.zeros_like(kwin))], axis=0)
    vrow = lax.broadcasted_iota(jnp.int32, vtw.shape, 0)
    vbd = jnp.concatenate([jnp.where(vrow < A_HEAD_DIM, vtw, jnp.zeros_like(vtw)),
                           jnp.where(vrow >= A_HEAD_DIM, vtw, jnp.zeros_like(vtw))], axis=1)
    srow = lax.broadcasted_iota(jnp.int32, (SUM_ROWS, 2 * n_keys), 0)
    scol = lax.broadcasted_iota(jnp.int32, (SUM_ROWS, 2 * n_keys), 1) // n_keys
    vbd = jnp.concatenate([vbd, jnp.where(srow == scol, 1.0, 0.0).astype(BF16)], axis=0)
    head_a = lax.broadcasted_iota(jnp.int32, (LANES, nq), 0) < A_HEAD_DIM

    st = _dot(kbd, qts)
    probs, esink = [], []
    for hh in range(2):
        sh = st[hh * n_keys:(hh + 1) * n_keys]
        if ok_prev is not None:
            parts = [jnp.where(ok_prev, sh[0:BLOCK], NEG_INF),
                     sh[BLOCK:2 * BLOCK],
                     jnp.where(ok_next, sh[2 * BLOCK:3 * BLOCK], NEG_INF),
                     sh[3 * BLOCK:]]
        else:
            parts = [sh]
        mx = functools.reduce(jnp.maximum, [jnp.max(t, axis=0, keepdims=True) for t in parts])
        mx = jnp.maximum(mx, sinks[hh])
        probs += [jnp.exp2(t - mx).astype(BF16) for t in parts]
        esink.append(jnp.exp2(sinks[hh] - mx))
    ot = _dot(vbd, jnp.concatenate(probs, axis=0))
    inv = [1.0 / (ot[LANES + hh:LANES + hh + 1] + esink[hh]) for hh in range(2)]
    return ot[0:LANES] * jnp.where(head_a, inv[0], inv[1])


def _attn_kernel(local, n_steps, sink_ref, qt_ref, *rest):
    n_blk = rest[-1].shape[1] // BLOCK
    if local:
        (kp_ref, kc_ref, kn_ref, vp_ref, vc_ref, vn_ref, kx_ref, vx_ref,
         sg_ref, x_ref, m_ref, w_ref, o_ref) = rest
        last = slice((n_blk - 1) * BLOCK, n_blk * BLOCK)
        k_blocks = ([kp_ref[0, last]] + [kc_ref[0, i * BLOCK:(i + 1) * BLOCK] for i in range(n_blk)]
                    + [kn_ref[0, 0:BLOCK]])
        v_blocks = ([vp_ref[0, :, last]] + [vc_ref[0, :, i * BLOCK:(i + 1) * BLOCK] for i in range(n_blk)]
                    + [vn_ref[0, :, 0:BLOCK]])
    else:
        kx_ref, vx_ref, sg_ref, x_ref, m_ref, w_ref, o_ref = rest
    step = pl.program_id(1)
    nq = A_GROUP * BLOCK
    chunk_of = lax.broadcasted_iota(jnp.int32, (1, nq), 1) // BLOCK
    sinks = []
    for hh in range(2):
        sk = jnp.full((1, nq), sink_ref[hh] * LOG2_E, F32)
        for c in range(1, A_GROUP):
            sk = jnp.where(chunk_of == c, sink_ref[2 * c + hh] * LOG2_E, sk)
        sinks.append(sk)
    if local:
        kj = lax.broadcasted_iota(jnp.int32, (BLOCK, nq), 0)
        qi = lax.broadcasted_iota(jnp.int32, (BLOCK, nq), 1) % BLOCK

    rows_out = []
    for blk in range(n_blk):
        qts = jnp.concatenate([qt_ref[0, c * LANES:(c + 1) * LANES, blk * BLOCK:(blk + 1) * BLOCK]
                               for c in range(A_GROUP)], axis=1)
        if local:
            kwin = jnp.concatenate(k_blocks[blk:blk + 3] + [kx_ref[0]], axis=0)
            vtw = jnp.concatenate(v_blocks[blk:blk + 3] + [vx_ref[0]], axis=1)
            ok_prev = (kj >= qi) & (step > 0) if blk == 0 else (kj >= qi)
            ok_next = (kj <= qi) & (step < n_steps - 1) if blk == n_blk - 1 else (kj <= qi)
        else:
            kwin, vtw, ok_prev, ok_next = kx_ref[0], vx_ref[0], None, None
        ot = _attn_block(kwin, vtw, qts, sinks, ok_prev, ok_next)
        rows = slice(blk * BLOCK, (blk + 1) * BLOCK)
        outs = []
        for c in range(A_GROUP):
            sg = sg_ref[0, rows, c * LANES:(c + 1) * LANES].astype(F32)
            outs.append((ot[:, c * BLOCK:(c + 1) * BLOCK].T * sg).astype(BF16))
        rows_out.append(jnp.concatenate(outs, axis=1))
    y = _dot(jnp.concatenate(rows_out, axis=0), w_ref[...])
    gate = m_ref[0][:, 2 * D:3 * D]
    o_ref[0] = x_ref[0] + gate * y


def _attention(sink, qt, k, vt, kx, vxt, sg, x, mod3, mod_row, w_out, local):
    b, s, _ = sg.shape
    rows = min(A_BLOCKS_PER_STEP * BLOCK, s)
    assert s % rows == 0
    ns = s // rows
    n_ctx = kx.shape[1]
    blk = lambda w: pl.BlockSpec((1, rows, w), lambda bi, i: (bi, i, 0))
    blk_t = lambda w: pl.BlockSpec((1, w, rows), lambda bi, i: (bi, 0, i))
    if mod_row is None:
        mod_map = lambda bi, i: (bi, 0, 0)
    else:
        mod_map = lambda bi, i: (mod_row, 0, 0)
    in_specs = [pl.BlockSpec(memory_space=pltpu.SMEM), blk_t(A_WIDTH)]
    args = [sink, qt]
    if local:
        lo = lambda i: jnp.maximum(i - 1, 0)
        hi = lambda i: jnp.minimum(i + 1, ns - 1)
        in_specs += [pl.BlockSpec((1, rows, A_KV_WIDTH), lambda bi, i: (bi, lo(i), 0)),
                     blk(A_KV_WIDTH),
                     pl.BlockSpec((1, rows, A_KV_WIDTH), lambda bi, i: (bi, hi(i), 0)),
                     pl.BlockSpec((1, A_KV_WIDTH, rows), lambda bi, i: (bi, 0, lo(i))),
                     blk_t(A_KV_WIDTH),
                     pl.BlockSpec((1, A_KV_WIDTH, rows), lambda bi, i: (bi, 0, hi(i)))]
        args += [k, k, k, vt, vt, vt]
    in_specs += [pl.BlockSpec((1, n_ctx, A_KV_WIDTH), lambda bi, i: (bi, 0, 0)),
                 pl.BlockSpec((1, A_KV_WIDTH, n_ctx), lambda bi, i: (bi, 0, 0)),
                 blk(A_WIDTH), blk(D),
                 pl.BlockSpec((1, 1, 3 * D), mod_map),
                 pl.BlockSpec((A_WIDTH, D), lambda bi, i: (0, 0))]
    args += [kx, vxt, sg, x, mod3, w_out]
    return pl.pallas_call(
        functools.partial(_attn_kernel, local, ns),
        grid=(b, ns),
        in_specs=in_specs,
        out_specs=blk(D),
        out_shape=jax.ShapeDtypeStruct((b, s, D), F32),
        compiler_params=_params("parallel", "parallel"),
        name="attn_local" if local else "attn_ctx",
    )(*args)


def _chunk_cumsum(x, tri):
    hi = x.astype(BF16)
    lo = (x - hi.astype(F32)).astype(BF16)
    return _dot(tri, hi) + _dot(tri, lo)


def _inproj_gla_kernel(chunk, with_q, x_ref, m_ref, g_ref, w_ref, wa1_ref, wa2_ref, ba_ref, tri_ref, *outs):
    if with_q:
        per_dir = (outs[0:4], outs[4:8])
        v_ref, sg_ref = outs[8:]
    else:
        per_dir = ((outs[0],), (outs[1],))
        v_ref = outs[2]
    tm = x_ref.shape[1]
    hb = _norm_mod(x_ref[0], g_ref[...], m_ref[0]).astype(BF16)
    r = _dot(hb, wa1_ref[...])
    z = _dot(r.astype(BF16), wa2_ref[...]) + ba_ref[...]
    e = jnp.exp2(jnp.abs(z) * -LOG2_E)
    la = (jnp.minimum(z, 0.0) * LOG2_E - jnp.log2(1.0 + e)) * (1.0 / GATE_TEMP)
    k = _dot(hb, w_ref[:, B_K_WIDTH:2 * B_K_WIDTH])
    if with_q:
        q = _dot(hb, w_ref[:, 0:B_K_WIDTH]) * K_SCALE
    n_chunks = tm // chunk
    for reverse in (False, True):
        lad = la[:, B_K_WIDTH:] if reverse else la[:, 0:B_K_WIDTH]
        refs = per_dir[1] if reverse else per_dir[0]
        tots, kis, kes = [], [], []
        tri = tri_ref[1 if reverse else 0]
        tb = tri.shape[0]
        cum_all = jnp.concatenate([_chunk_cumsum(lad[r * tb:(r + 1) * tb], tri) for r in range(tm // tb)], axis=0)
        for c in range(n_chunks):
            rows = slice(c * chunk, (c + 1) * chunk)
            cum = cum_all[rows]
            tot = cum[0:1] if reverse else cum[chunk - 1:chunk]
            ke = k[rows] * jnp.exp2(tot - cum)
            if not with_q:
                refs[0][0, rows] = ke.astype(BF16)
                continue
            refs[0][0, rows] = (q[rows] * jnp.exp2(cum)).astype(BF16)
            kis.append(k[rows] * jnp.exp2(-cum))
            kes.append(ke)
            tots.append(tot)
            if c % 2 == 1:
                pair = slice((c - 1) * chunk, (c + 1) * chunk)
                refs[1][0, :, pair] = jnp.concatenate(kis[-2:], axis=0).T.astype(BF16)
                refs[2][0, :, pair] = jnp.concatenate(kes[-2:], axis=0).T.astype(BF16)
        if with_q:
            pad = [jnp.zeros((DEC_ROWS - n_chunks, B_K_WIDTH), F32)] if n_chunks < DEC_ROWS else []
            refs[3][0, 0] = jnp.exp2(jnp.concatenate(tots + pad, axis=0))
    if with_q:
        sg_ref[0] = _silu(_dot(hb, w_ref[:, 2 * B_K_WIDTH + B_V_WIDTH:])).astype(BF16)
    v_ref[0] = _dot(hb, w_ref[:, 2 * B_K_WIDTH:2 * B_K_WIDTH + B_V_WIDTH]).astype(BF16)


def _inproj_gla(x, mod3, mod_row, norm_g, w, wa1, wa2, ba, chunk, with_q):
    b, s, _ = x.shape
    tm = min(ROW_TILE, s)
    nt = s // tm
    cpt = tm // chunk
    tb = max(chunk, MXU_DIM)
    assert tm % tb == 0 and tb % chunk == 0
    if mod_row is None:
        mod_map = lambda bi, i: (bi, 0, 0)
    else:
        mod_map = lambda bi, i: (mod_row, 0, 0)
    const = lambda shape: pl.BlockSpec(shape, lambda bi, i: (0,) * len(shape))
    in_specs = [
        pl.BlockSpec((1, tm, D), lambda bi, i: (bi, i, 0)),
        pl.BlockSpec((1, 1, 3 * D), mod_map),
        const((1, D)), const(w.shape), const(wa1.shape), const(wa2.shape), const(ba.shape),
        const((2, tb, tb)),
    ]
    t_idx = np.arange(tb)
    same = (t_idx[:, None] // chunk) == (t_idx[None, :] // chunk)
    lower = same & (t_idx[None, :] <= t_idx[:, None])
    tri = jnp.asarray(np.stack([lower, lower.T]), dtype=BF16)
    rows = lambda w_: (jax.ShapeDtypeStruct((b, s, w_), BF16),
                       pl.BlockSpec((1, tm, w_), lambda bi, i: (bi, i, 0)))
    cols = (jax.ShapeDtypeStruct((b, B_K_WIDTH, s), BF16),
            pl.BlockSpec((1, B_K_WIDTH, tm), lambda bi, i: (bi, 0, i)))
    decs = (jax.ShapeDtypeStruct((b, nt, DEC_ROWS, B_K_WIDTH), F32),
            pl.BlockSpec((1, 1, DEC_ROWS, B_K_WIDTH), lambda bi, i: (bi, i, 0, 0)))
    if with_q:
        assert cpt <= DEC_ROWS and cpt % 2 == 0
        outs = [rows(B_K_WIDTH), cols, cols, decs] * 2 + [rows(B_V_WIDTH)] * 2
    else:
        outs = [rows(B_K_WIDTH)] * 2 + [rows(B_V_WIDTH)]
    return pl.pallas_call(
        functools.partial(_inproj_gla_kernel, chunk, with_q),
        grid=(b, nt),
        in_specs=in_specs,
        out_specs=tuple(o[1] for o in outs),
        out_shape=tuple(o[0] for o in outs),
        compiler_params=_params("parallel", "parallel"),
        name="inproj_gla" if with_q else "inproj_gla_ctx",
    )(x, mod3, norm_g.reshape(1, D), w, wa1, wa2, ba, tri)


def _gla_state_kernel(kf_ref, kb_ref, v_ref, sf_ref, sb_ref):
    for h in range(B_HEADS):
        vh = v_ref[0, :, h * B_VAL_DIM:(h + 1) * B_VAL_DIM]
        ks = slice(h * B_KEY_DIM, (h + 1) * B_KEY_DIM)
        sf_ref[0, h] = _dot_tn(kf_ref[0, :, ks], vh)
        sb_ref[0, h] = _dot_tn(kb_ref[0, :, ks], vh)


def _gla_state(ke_f, ke_b, v):
    b, n, _ = v.shape
    st = jax.ShapeDtypeStruct((b, B_HEADS, B_KEY_DIM, B_VAL_DIM), F32)
    st_spec = pl.BlockSpec((1, B_HEADS, B_KEY_DIM, B_VAL_DIM), lambda bi: (bi, 0, 0, 0))
    return pl.pallas_call(
        _gla_state_kernel,
        grid=(b,),
        in_specs=[pl.BlockSpec((1, n, B_K_WIDTH), lambda bi: (bi, 0, 0))] * 2
        + [pl.BlockSpec((1, n, B_V_WIDTH), lambda bi: (bi, 0, 0))],
        out_specs=(st_spec, st_spec),
        out_shape=(st, st),
        compiler_params=_params("parallel"),
        name="gla_ctx_state",
    )(ke_f, ke_b, v)


def _scan_tile(reverse, qd_ref, kit_ref, ket_ref, dec_ref, v_ref, st_ref):
    pair_rows = 2 * CHUNK
    ti = lax.broadcasted_iota(jnp.int32, (pair_rows, pair_rows), 0)
    si = lax.broadcasted_iota(jnp.int32, (pair_rows, pair_rows), 1)
    same = (ti // CHUNK) == (si // CHUNK)
    if reverse:
        use_inv = same & (ti <= si)
        use_end = (ti < CHUNK) & (si >= CHUNK)
        second_row = lax.broadcasted_iota(jnp.int32, (pair_rows, B_KEY_DIM), 0) < CHUNK
        first_col = lax.broadcasted_iota(jnp.int32, (B_KEY_DIM, pair_rows), 1) >= CHUNK
    else:
        use_inv = same & (ti >= si)
        use_end = (ti >= CHUNK) & (si < CHUNK)
        second_row = lax.broadcasted_iota(jnp.int32, (pair_rows, B_KEY_DIM), 0) >= CHUNK
        first_col = lax.broadcasted_iota(jnp.int32, (B_KEY_DIM, pair_rows), 1) < CHUNK
    n_pairs = SCAN_TILE // pair_rows
    order = range(n_pairs - 1, -1, -1) if reverse else range(n_pairs)
    dec = dec_ref[0, 0]
    dect = jnp.concatenate([dec, jnp.zeros((LANES - DEC_ROWS, B_K_WIDTH), F32)], axis=0).T
    heads = []
    for h in range(B_HEADS):
        ks = slice(h * B_KEY_DIM, (h + 1) * B_KEY_DIM)
        vs = slice(h * B_VAL_DIM, (h + 1) * B_VAL_DIM)
        st = st_ref[h]
        o_rows = [None] * n_pairs
        for p in order:
            rows = slice(p * pair_rows, (p + 1) * pair_rows)
            c_first, c_second = (2 * p + 1, 2 * p) if reverse else (2 * p, 2 * p + 1)
            qd = qd_ref[0, rows, ks]
            ket = ket_ref[0, ks, rows]
            vh = v_ref[0, rows, vs]
            a = jnp.where(use_inv, _dot(qd, kit_ref[0, ks, rows]),
                          jnp.where(use_end, _dot(qd, ket), 0.0)).astype(BF16)
            q_pair = jnp.where(second_row, qd.astype(F32) * dec[c_first:c_first + 1, ks], qd.astype(F32))
            k_pair = jnp.where(first_col, ket.astype(F32) * dect[ks, c_second:c_second + 1], ket.astype(F32))
            o_rows[p] = _dot(a, vh) + _dot(q_pair.astype(BF16), st.astype(BF16))
            st = (st * (dect[ks, c_first:c_first + 1] * dect[ks, c_second:c_second + 1])
                  + _dot(k_pair.astype(BF16), vh))
        st_ref[h] = st
        heads.append(jnp.concatenate(o_rows, axis=0))
    return jnp.concatenate(heads, axis=1)


def _gla_scan_kernel(nt, qdf_ref, kitf_ref, ketf_ref, decf_ref, qdb_ref, kitb_ref, ketb_ref, decb_ref,
                     v_ref, sf_ref, sb_ref, sg_ref, x_ref, m_ref, hn_ref, w_ref, fn_ref,
                     o_ref, st_ref, of_ref):
    j = pl.program_id(1)

    @pl.when(j == 0)
    def _():
        st_ref[...] = sf_ref[0]

    @pl.when(j == nt)
    def _():
        st_ref[...] = sb_ref[0]

    @pl.when(j < nt)
    def _():
        of_ref[j] = _scan_tile(False, qdf_ref, kitf_ref, ketf_ref, decf_ref, v_ref, st_ref)

    @pl.when(j >= nt)
    def _():
        o = _scan_tile(True, qdb_ref, kitb_ref, ketb_ref, decb_ref, v_ref, st_ref) + of_ref[2 * nt - 1 - j]
        normed = []
        for h in range(B_HEADS):
            oh = o[:, h * B_VAL_DIM:(h + 1) * B_VAL_DIM]
            normed.append(oh * lax.rsqrt(jnp.mean(oh * oh, axis=-1, keepdims=True) + EPS))
        of = jnp.concatenate(normed, axis=1) * hn_ref[...]
        y = _dot((of * sg_ref[0].astype(F32)).astype(BF16), w_ref[...])
        xn = x_ref[0] + m_ref[0][:, 2 * D:3 * D] * y
        ms = jnp.mean(xn * xn, axis=-1, keepdims=True)
        o_ref[0] = xn * lax.rsqrt(ms + EPS) * fn_ref[...]


def _gla_scan(fwd, bwd, v, s_f, s_b, sg, x, mod3, head_g, w_out, final_g):
    b, s, _ = v.shape
    assert SCAN_TILE == ROW_TILE
    nt = s // SCAN_TILE
    t_fwd = lambda j: jnp.minimum(j, nt - 1)
    t_bwd = lambda j: 2 * nt - 1 - jnp.maximum(j, nt)
    t_both = lambda j: jnp.where(j < nt, j, 2 * nt - 1 - j)

    def direction(t):
        rows = pl.BlockSpec((1, SCAN_TILE, B_K_WIDTH), lambda bi, j: (bi, t(j), 0))
        cols = pl.BlockSpec((1, B_K_WIDTH, SCAN_TILE), lambda bi, j: (bi, 0, t(j)))
        decs = pl.BlockSpec((1, 1, DEC_ROWS, B_K_WIDTH), lambda bi, j: (bi, t(j), 0, 0))
        return [rows, cols, cols, decs]

    const = lambda shape: pl.BlockSpec(shape, lambda bi, j: (0,) * len(shape))
    state = pl.BlockSpec((1, B_HEADS, B_KEY_DIM, B_VAL_DIM), lambda bi, j: (bi, 0, 0, 0))
    out_rows = lambda w_: pl.BlockSpec((1, SCAN_TILE, w_), lambda bi, j: (bi, t_bwd(j), 0))
    in_specs = direction(t_fwd) + direction(t_bwd) + [
        pl.BlockSpec((1, SCAN_TILE, B_V_WIDTH), lambda bi, j: (bi, t_both(j), 0)),
        state, state, out_rows(B_V_WIDTH), out_rows(D),
        pl.BlockSpec((1, 1, 3 * D), lambda bi, j: (bi, 0, 0)),
        const((1, B_V_WIDTH)), const((B_V_WIDTH, D)), const((1, D))]
    return pl.pallas_call(
        functools.partial(_gla_scan_kernel, nt),
        grid=(b, 2 * nt),
        in_specs=in_specs,
        out_specs=out_rows(D),
        out_shape=jax.ShapeDtypeStruct((b, s, D), F32),
        scratch_shapes=[pltpu.VMEM((B_HEADS, B_KEY_DIM, B_VAL_DIM), F32),
                        pltpu.VMEM((nt, SCAN_TILE, B_V_WIDTH), F32)],
        compiler_params=_params("parallel", "arbitrary"),
        name="gla_scan",
    )(*fwd, *bwd, v, s_f, s_b, sg, x, mod3, head_g.reshape(1, B_V_WIDTH), w_out, final_g.reshape(1, D))


def _rope_tables(n_tokens):
    rows_n = n_tokens // GRID_W
    row = np.repeat(np.arange(rows_n, dtype=np.float32), GRID_W)
    col = np.tile(np.arange(GRID_W, dtype=np.float32), rows_n)
    inv_freq = (np.float32(ROPE_BASE) ** (-np.arange(ROPE_FREQS, dtype=np.float32) / np.float32(ROPE_FREQS)))
    inv_freq = inv_freq.astype(np.float32)
    ang = np.stack([row[:, None] * inv_freq, col[:, None] * inv_freq], axis=1)
    cos, sin = np.cos(ang).astype(np.float32), np.sin(ang).astype(np.float32)
    zero = np.zeros_like(sin)
    tile = lambda t: np.tile(t.reshape(n_tokens, A_HEAD_DIM), (1, LANES // A_HEAD_DIM))
    return (tile(np.stack([cos, cos], axis=2)),
            tile(np.stack([-sin, zero], axis=2)),
            tile(np.stack([zero, sin], axis=2)))


def _pair_heads(t, axis):
    shape = t.shape
    t = t.reshape(shape[:axis] + (A_KV_HEADS, A_GROUP, -1) + shape[axis + 1:])
    return jnp.swapaxes(t, axis, axis + 1).reshape(shape)


def kernel(x, c, ctx, c_ctx, l0_norm_g, l0_w_ada, l0_b_ada, l0_w_in, l0_sink, l0_w_out, l1_norm_g, l1_w_ada, l1_b_ada, l1_w_in, l1_wa1_f, l1_wa2_f, l1_ba_f, l1_wa1_b, l1_wa2_b, l1_ba_b, l1_head_norm_g, l1_w_out, final_norm_g):
    b, s, _ = x.shape
    ctx_row = b

    cvec = jnp.concatenate([c, c_ctx[None, :], jnp.zeros((MOD_ROWS - b - 1, D), F32)], axis=0)
    mod0 = _modulation(cvec, l0_w_ada, l0_b_ada).reshape(MOD_ROWS, 1, 3 * D)
    mod1 = _modulation(cvec, l1_w_ada, l1_b_ada).reshape(MOD_ROWS, 1, 3 * D)

    q_end, kv_end = A_WIDTH, A_WIDTH + 2 * A_KV_WIDTH
    w0 = (_pair_heads(l0_w_in[:, :q_end].astype(BF16), 1), l0_w_in[:, q_end:kv_end].astype(BF16),
          _pair_heads(l0_w_in[:, kv_end:].astype(BF16), 1))
    w0_out = _pair_heads(l0_w_out.astype(BF16), 0)
    sink = _pair_heads(l0_sink.astype(F32), 0)
    qt, k, vt, sg = _inproj_attn(x, mod0, None, l0_norm_g, w0, _rope_tables(s))
    qct, kc, vct, sgc = _inproj_attn(ctx, mod0, ctx_row, l0_norm_g, w0, None)
    x1 = _attention(sink, qt, k, vt, kc, vct, sg, x, mod0, None, w0_out, True)
    xc1 = _attention(sink, qct, None, None, kc, vct, sgc, ctx, mod0, ctx_row, w0_out, False)

    w1 = l1_w_in.astype(BF16)
    wa1 = jnp.concatenate([l1_wa1_f, l1_wa1_b, jnp.zeros((D, LANES - 2 * GATE_RANK), F32)], axis=1).astype(BF16)
    wa2 = jnp.zeros((LANES, 2 * B_K_WIDTH), F32)
    wa2 = wa2.at[0:GATE_RANK, 0:B_K_WIDTH].set(l1_wa2_f)
    wa2 = wa2.at[GATE_RANK:2 * GATE_RANK, B_K_WIDTH:].set(l1_wa2_b).astype(BF16)
    ba = jnp.concatenate([l1_ba_f, l1_ba_b]).reshape(1, 2 * B_K_WIDTH)
    kec_f, kec_b, vcx = _inproj_gla(xc1, mod1, ctx_row, l1_norm_g, w1, wa1, wa2, ba, ctx.shape[1], False)
    s_f, s_b = _gla_state(kec_f, kec_b, vcx)
    outs = _inproj_gla(x1, mod1, None, l1_norm_g, w1, wa1, wa2, ba, CHUNK, True)
    v1, sg1 = outs[8:]
    return _gla_scan(outs[0:4], outs[4:8], v1, s_f, s_b, sg1, x1, mod1,
                     l1_head_norm_g, l1_w_out.astype(BF16), final_norm_g)
```

```python
import functools

import jax
import jax.numpy as jnp
import numpy as np
from jax import lax
from jax.experimental import pallas as pl
from jax.experimental.pallas import tpu as pltpu

F32 = jnp.float32
BF16 = jnp.bfloat16

D = 1024
GRID_W = 64
EPS = 1e-6
NEG_INF = -1e30

A_HEADS = 16
A_KV_HEADS = 2
A_GROUP = A_HEADS // A_KV_HEADS
A_HEAD_DIM = 64
A_WIDTH = A_HEADS * A_HEAD_DIM
A_KV_WIDTH = A_KV_HEADS * A_HEAD_DIM
BLOCK = 128
ROPE_BASE = 10000.0
ROPE_FREQS = A_HEAD_DIM // 4
Q_SCALE = A_HEAD_DIM ** -0.5
LOG2_E = 1.4426950408889634

B_HEADS = 4
B_K_WIDTH = D // 2
B_V_WIDTH = D
B_KEY_DIM = B_K_WIDTH // B_HEADS
B_VAL_DIM = B_V_WIDTH // B_HEADS
GATE_RANK = 16
GATE_TEMP = 16.0
CHUNK = 64
K_SCALE = B_KEY_DIM ** -0.5

LANES = 128
MXU_DIM = 256
MOD_ROWS = 16
ROW_TILE = 512
SCAN_TILE = 512
SUB_ROWS = 256
DEC_ROWS = 8
SUM_ROWS = 16
A_BLOCKS_PER_STEP = 4
VMEM_LIMIT = 48 * 1024 * 1024


def _params(*sem):
    return pltpu.CompilerParams(dimension_semantics=sem, vmem_limit_bytes=VMEM_LIMIT)


def _silu(x):
    return x / (1.0 + jnp.exp(-x))


def _dot(a, b):
    return jnp.dot(a, b, preferred_element_type=F32)


def _dot_nt(a, b):
    return lax.dot_general(a, b, (((1,), (1,)), ((), ())), preferred_element_type=F32)


def _dot_tn(a, b):
    return lax.dot_general(a, b, (((0,), (0,)), ((), ())), preferred_element_type=F32)


def _norm_mod(x, g, m):
    ms = jnp.mean(x * x, axis=-1, keepdims=True)
    y = x * lax.rsqrt(ms + EPS) * g
    return y * (1.0 + m[:, D:2 * D]) + m[:, 0:D]


def _mod_kernel(c_ref, w_ref, b_ref, o_ref):
    s = _silu(c_ref[...])
    o_ref[...] = _dot(s.astype(BF16), w_ref[...].astype(BF16)) + b_ref[...]


def _modulation(cvec, w_ada, b_ada):
    n = w_ada.shape[1] // D
    return pl.pallas_call(
        _mod_kernel,
        grid=(n,),
        in_specs=[
            pl.BlockSpec((MOD_ROWS, D), lambda j: (0, 0)),
            pl.BlockSpec((D, D), lambda j: (0, j)),
            pl.BlockSpec((1, D), lambda j: (0, j)),
        ],
        out_specs=pl.BlockSpec((MOD_ROWS, D), lambda j: (0, j)),
        out_shape=jax.ShapeDtypeStruct((MOD_ROWS, n * D), F32),
        compiler_params=_params("arbitrary"),
        name="modulation",
    )(cvec, w_ada, b_ada.reshape(1, n * D))


def _rope(t, cos, sin_hi, sin_lo):
    return t * cos + pltpu.roll(t, LANES - ROPE_FREQS, 1) * sin_hi + pltpu.roll(t, ROPE_FREQS, 1) * sin_lo


def _inproj_attn_kernel(rope, x_ref, m_ref, g_ref, wq_ref, wkv_ref, wg_ref, *rest):
    if rope:
        cos_ref, shi_ref, slo_ref, qt_ref, k_ref, vt_ref, sg_ref = rest
    else:
        qt_ref, k_ref, vt_ref, sg_ref = rest
    for r in range(x_ref.shape[1] // SUB_ROWS):
        rows = slice(r * SUB_ROWS, (r + 1) * SUB_ROWS)
        if rope:
            cos, shi, slo = cos_ref[rows], shi_ref[rows], slo_ref[rows]
        hb = _norm_mod(x_ref[0, rows], g_ref[...], m_ref[0]).astype(BF16)
        q = _dot(hb, wq_ref[...])
        for j in range(A_WIDTH // LANES):
            qj = q[:, j * LANES:(j + 1) * LANES]
            if rope:
                qj = _rope(qj, cos, shi, slo)
            qt_ref[0, j * LANES:(j + 1) * LANES, rows] = (qj * (Q_SCALE * LOG2_E)).T.astype(BF16)
        kv = _dot(hb, wkv_ref[...])
        k = kv[:, 0:A_KV_WIDTH]
        if rope:
            k = _rope(k, cos, shi, slo)
        k_ref[0, rows] = k.astype(BF16)
        vt_ref[0, :, rows] = kv[:, A_KV_WIDTH:].T.astype(BF16)
        g = _dot(hb, wg_ref[...])
        sg_ref[0, rows] = _silu(g).astype(BF16)


def _inproj_attn(x, mod3, mod_row, norm_g, weights, tables):
    b, s, _ = x.shape
    tm = min(ROW_TILE, s)
    rope = tables is not None
    if mod_row is None:
        mod_map = lambda bi, i: (bi, 0, 0)
    else:
        mod_map = lambda bi, i: (mod_row, 0, 0)
    in_specs = [
        pl.BlockSpec((1, tm, D), lambda bi, i: (bi, i, 0)),
        pl.BlockSpec((1, 1, 3 * D), mod_map),
        pl.BlockSpec((1, D), lambda bi, i: (0, 0)),
    ] + [pl.BlockSpec(w.shape, lambda bi, i: (0, 0)) for w in weights]
    args = [x, mod3, norm_g.reshape(1, D), *weights]
    if rope:
        in_specs += [pl.BlockSpec((tm, LANES), lambda bi, i: (i, 0))] * 3
        args += list(tables)
    out_shape = (
        jax.ShapeDtypeStruct((b, A_WIDTH, s), BF16),
        jax.ShapeDtypeStruct((b, s, A_KV_WIDTH), BF16),
        jax.ShapeDtypeStruct((b, A_KV_WIDTH, s), BF16),
        jax.ShapeDtypeStruct((b, s, A_WIDTH), BF16),
    )
    out_specs = (
        pl.BlockSpec((1, A_WIDTH, tm), lambda bi, i: (bi, 0, i)),
        pl.BlockSpec((1, tm, A_KV_WIDTH), lambda bi, i: (bi, i, 0)),
        pl.BlockSpec((1, A_KV_WIDTH, tm), lambda bi, i: (bi, 0, i)),
        pl.BlockSpec((1, tm, A_WIDTH), lambda bi, i: (bi, i, 0)),
    )
    return pl.pallas_call(
        functools.partial(_inproj_attn_kernel, rope),
        grid=(b, s // tm),
        in_specs=in_specs,
        out_specs=out_specs,
        out_shape=out_shape,
        compiler_params=_params("parallel", "parallel"),
        name="inproj_attn_rope" if rope else "inproj_attn_ctx",
    )(*args)


def _attn_block(kwin, vtw, qts, sinks, ok_prev, ok_next):
    n_keys, nq = kwin.shape[0], qts.shape[1]
    lane = lax.broadcasted_iota(jnp.int32, kwin.shape, 1)
    kbd = jnp.concatenate([jnp.where(lane < A_HEAD_DIM, kwin, jnp.zeros_like(kwin)),
                           jnp.where(lane >= A_HEAD_DIM, kwin, jnp.zeros_like(kwin))], axis=0)
    vrow = lax.broadcasted_iota(jnp.int32, vtw.shape, 0)
    vbd = jnp.concatenate([jnp.where(vrow < A_HEAD_DIM, vtw, jnp.zeros_like(vtw)),
                           jnp.where(vrow >= A_HEAD_DIM, vtw, jnp.zeros_like(vtw))], axis=1)
    srow = lax.broadcasted_iota(jnp.int32, (SUM_ROWS, 2 * n_keys), 0)
    scol = lax.broadcasted_iota(jnp.int32, (SUM_ROWS, 2 * n_keys), 1) // n_keys
    vbd = jnp.concatenate([vbd, jnp.where(srow == scol, 1.0, 0.0).astype(BF16)], axis=0)
    head_a = lax.broadcasted_iota(jnp.int32, (LANES, nq), 0) < A_HEAD_DIM

    st = _dot(kbd, qts)
    probs, esink = [], []
    for hh in range(2):
        sh = st[hh * n_keys:(hh + 1) * n_keys]
        if ok_prev is not None:
            parts = [jnp.where(ok_prev, sh[0:BLOCK], NEG_INF),
                     sh[BLOCK:2 * BLOCK],
                     jnp.where(ok_next, sh[2 * BLOCK:3 * BLOCK], NEG_INF),
                     sh[3 * BLOCK:]]
        else:
            parts = [sh]
        mx = functools.reduce(jnp.maximum, [jnp.max(t, axis=0, keepdims=True) for t in parts])
        mx = jnp.maximum(mx, sinks[hh])
        probs += [jnp.exp2(t - mx).astype(BF16) for t in parts]
        esink.append(jnp.exp2(sinks[hh] - mx))
    ot = _dot(vbd, jnp.concatenate(probs, axis=0))
    inv = [1.0 / (ot[LANES + hh:LANES + hh + 1] + esink[hh]) for hh in range(2)]
    return ot[0:LANES] * jnp.where(head_a, inv[0], inv[1])


def _attn_kernel(local, n_steps, sink_ref, qt_ref, *rest):
    n_blk = rest[-1].shape[1] // BLOCK
    if local:
        (kp_ref, kc_ref, kn_ref, vp_ref, vc_ref, vn_ref, kx_ref, vx_ref,
         sg_ref, x_ref, m_ref, w_ref, o_ref) = rest
        last = slice((n_blk - 1) * BLOCK, n_blk * BLOCK)
        k_blocks = ([kp_ref[0, last]] + [kc_ref[0, i * BLOCK:(i + 1) * BLOCK] for i in range(n_blk)]
                    + [kn_ref[0, 0:BLOCK]])
        v_blocks = ([vp_ref[0, :, last]] + [vc_ref[0, :, i * BLOCK:(i + 1) * BLOCK] for i in range(n_blk)]
                    + [vn_ref[0, :, 0:BLOCK]])
    else:
        kx_ref, vx_ref, sg_ref, x_ref, m_ref, w_ref, o_ref = rest
    step = pl.program_id(1)
    nq = A_GROUP * BLOCK
    chunk_of = lax.broadcasted_iota(jnp.int32, (1, nq), 1) // BLOCK
    sinks = []
    for hh in range(2):
        sk = jnp.full((1, nq), sink_ref[hh] * LOG2_E, F32)
        for c in range(1, A_GROUP):
            sk = jnp.where(chunk_of == c, sink_ref[2 * c + hh] * LOG2_E, sk)
        sinks.append(sk)
    if local:
        kj = lax.broadcasted_iota(jnp.int32, (BLOCK, nq), 0)
        qi = lax.broadcasted_iota(jnp.int32, (BLOCK, nq), 1) % BLOCK

    rows_out = []
    for blk in range(n_blk):
        qts = jnp.concatenate([qt_ref[0, c * LANES:(c + 1) * LANES, blk * BLOCK:(blk + 1) * BLOCK]
                               for c in range(A_GROUP)], axis=1)
        if local:
            kwin = jnp.concatenate(k_blocks[blk:blk + 3] + [kx_ref[0]], axis=0)
            vtw = jnp.concatenate(v_blocks[blk:blk + 3] + [vx_ref[0]], axis=1)
            ok_prev = (kj >= qi) & (step > 0) if blk == 0 else (kj >= qi)
            ok_next = (kj <= qi) & (step < n_steps - 1) if blk == n_blk - 1 else (kj <= qi)
        else:
            kwin, vtw, ok_prev, ok_next = kx_ref[0], vx_ref[0], None, None
        ot = _attn_block(kwin, vtw, qts, sinks, ok_prev, ok_next)
        rows = slice(blk * BLOCK, (blk + 1) * BLOCK)
        outs = []
        for c in range(A_GROUP):
            sg = sg_ref[0, rows, c * LANES:(c + 1) * LANES].astype(F32)
            outs.append((ot[:, c * BLOCK:(c + 1) * BLOCK].T * sg).astype(BF16))
        rows_out.append(jnp.concatenate(outs, axis=1))
    y = _dot(jnp.concatenate(rows_out, axis=0), w_ref[...])
    gate = m_ref[0][:, 2 * D:3 * D]
    o_ref[0] = x_ref[0] + gate * y


def _attention(sink, qt, k, vt, kx, vxt, sg, x, mod3, mod_row, w_out, local):
    b, s, _ = sg.shape
    rows = min(A_BLOCKS_PER_STEP * BLOCK, s)
    assert s % rows == 0
    ns = s // rows
    n_ctx = kx.shape[1]
    blk = lambda w: pl.BlockSpec((1, rows, w), lambda bi, i: (bi, i, 0))
    blk_t = lambda w: pl.BlockSpec((1, w, rows), lambda bi, i: (bi, 0, i))
    if mod_row is None:
        mod_map = lambda bi, i: (bi, 0, 0)
    else:
        mod_map = lambda bi, i: (mod_row, 0, 0)
    in_specs = [pl.BlockSpec(memory_space=pltpu.SMEM), blk_t(A_WIDTH)]
    args = [sink, qt]
    if local:
        lo = lambda i: jnp.maximum(i - 1, 0)
        hi = lambda i: jnp.minimum(i + 1, ns - 1)
        in_specs += [pl.BlockSpec((1, rows, A_KV_WIDTH), lambda bi, i: (bi, lo(i), 0)),
                     blk(A_KV_WIDTH),
                     pl.BlockSpec((1, rows, A_KV_WIDTH), lambda bi, i: (bi, hi(i), 0)),
                     pl.BlockSpec((1, A_KV_WIDTH, rows), lambda bi, i: (bi, 0, lo(i))),
                     blk_t(A_KV_WIDTH),
                     pl.BlockSpec((1, A_KV_WIDTH, rows), lambda bi, i: (bi, 0, hi(i)))]
        args += [k, k, k, vt, vt, vt]
    in_specs += [pl.BlockSpec((1, n_ctx, A_KV_WIDTH), lambda bi, i: (bi, 0, 0)),
                 pl.BlockSpec((1, A_KV_WIDTH, n_ctx), lambda bi, i: (bi, 0, 0)),
                 blk(A_WIDTH), blk(D),
                 pl.BlockSpec((1, 1, 3 * D), mod_map),
                 pl.BlockSpec((A_WIDTH, D), lambda bi, i: (0, 0))]
    args += [kx, vxt, sg, x, mod3, w_out]
    return pl.pallas_call(
        functools.partial(_attn_kernel, local, ns),
        grid=(b, ns),
        in_specs=in_specs,
        out_specs=blk(D),
        out_shape=jax.ShapeDtypeStruct((b, s, D), F32),
        compiler_params=_params("parallel", "parallel"),
        name="attn_local" if local else "attn_ctx",
    )(*args)


def _chunk_cumsum(x, tri):
    hi = x.astype(BF16)
    lo = (x - hi.astype(F32)).astype(BF16)
    return _dot(tri, hi) + _dot(tri, lo)


def _inproj_gla_kernel(chunk, with_q, x_ref, m_ref, g_ref, w_ref, wa1_ref, wa2_ref, ba_ref, tri_ref, *outs):
    if with_q:
        per_dir = (outs[0:4], outs[4:8])
        v_ref, sg_ref = outs[8:]
    else:
        per_dir = ((outs[0],), (outs[1],))
        v_ref = outs[2]
    tm = x_ref.shape[1]
    sub = tri_ref.shape[1]
    tots = ([], [])
    for r in range(tm // sub):
        base = r * sub
        hb = _norm_mod(x_ref[0, base:base + sub], g_ref[...], m_ref[0]).astype(BF16)
        z = _dot(_dot(hb, wa1_ref[...]).astype(BF16), wa2_ref[...]) + ba_ref[...]
        e = jnp.exp2(jnp.abs(z) * -LOG2_E)
        la = (jnp.minimum(z, 0.0) * LOG2_E - jnp.log2(1.0 + e)) * (1.0 / GATE_TEMP)
        k = _dot(hb, w_ref[:, B_K_WIDTH:2 * B_K_WIDTH])
        if with_q:
            q = _dot(hb, w_ref[:, 0:B_K_WIDTH]) * K_SCALE
        for reverse in (False, True):
            lad = la[:, B_K_WIDTH:] if reverse else la[:, 0:B_K_WIDTH]
            refs = per_dir[1] if reverse else per_dir[0]
            kis, kes = [], []
            cum_all = _chunk_cumsum(lad, tri_ref[1 if reverse else 0])
            for c in range(sub // chunk):
                rows = slice(c * chunk, (c + 1) * chunk)
                out_rows = slice(base + c * chunk, base + (c + 1) * chunk)
                cum = cum_all[rows]
                tot = cum[0:1] if reverse else cum[chunk - 1:chunk]
                ke = k[rows] * jnp.exp2(tot - cum)
                if not with_q:
                    refs[0][0, out_rows] = ke.astype(BF16)
                    continue
                refs[0][0, out_rows] = (q[rows] * jnp.exp2(cum)).astype(BF16)
                kis.append(k[rows] * jnp.exp2(-cum))
                kes.append(ke)
                tots[reverse].append(tot)
                if c % 2 == 1:
                    pair = slice(base + (c - 1) * chunk, base + (c + 1) * chunk)
                    refs[1][0, :, pair] = jnp.concatenate(kis[-2:], axis=0).T.astype(BF16)
                    refs[2][0, :, pair] = jnp.concatenate(kes[-2:], axis=0).T.astype(BF16)
        if with_q:
            sg_ref[0, base:base + sub] = _silu(_dot(hb, w_ref[:, 2 * B_K_WIDTH + B_V_WIDTH:])).astype(BF16)
        v_ref[0, base:base + sub] = _dot(hb, w_ref[:, 2 * B_K_WIDTH:2 * B_K_WIDTH + B_V_WIDTH]).astype(BF16)
    if with_q:
        n_chunks = tm // chunk
        pad = [jnp.zeros((DEC_ROWS - n_chunks, B_K_WIDTH), F32)] if n_chunks < DEC_ROWS else []
        for reverse in (False, True):
            dec = jnp.exp2(jnp.concatenate(tots[reverse] + pad, axis=0))
            per_dir[reverse][3][0, 0] = dec


def _inproj_gla(x, mod3, mod_row, norm_g, w, wa1, wa2, ba, chunk, with_q):
    b, s, _ = x.shape
    tm = min(ROW_TILE, s)
    nt = s // tm
    cpt = tm // chunk
    tb = max(chunk, MXU_DIM)
    assert tm % tb == 0 and tb % chunk == 0
    if mod_row is None:
        mod_map = lambda bi, i: (bi, 0, 0)
    else:
        mod_map = lambda bi, i: (mod_row, 0, 0)
    const = lambda shape: pl.BlockSpec(shape, lambda bi, i: (0,) * len(shape))
    in_specs = [
        pl.BlockSpec((1, tm, D), lambda bi, i: (bi, i, 0)),
        pl.BlockSpec((1, 1, 3 * D), mod_map),
        const((1, D)), const(w.shape), const(wa1.shape), const(wa2.shape), const(ba.shape),
        const((2, tb, tb)),
    ]
    t_idx = np.arange(tb)
    same = (t_idx[:, None] // chunk) == (t_idx[None, :] // chunk)
    lower = same & (t_idx[None, :] <= t_idx[:, None])
    tri = jnp.asarray(np.stack([lower, lower.T]), dtype=BF16)
    rows = lambda w_: (jax.ShapeDtypeStruct((b, s, w_), BF16),
                       pl.BlockSpec((1, tm, w_), lambda bi, i: (bi, i, 0)))
    cols = (jax.ShapeDtypeStruct((b, B_K_WIDTH, s), BF16),
            pl.BlockSpec((1, B_K_WIDTH, tm), lambda bi, i: (bi, 0, i)))
    decs = (jax.ShapeDtypeStruct((b, nt, DEC_ROWS, B_K_WIDTH), F32),
            pl.BlockSpec((1, 1, DEC_ROWS, B_K_WIDTH), lambda bi, i: (bi, i, 0, 0)))
    if with_q:
        assert cpt <= DEC_ROWS and cpt % 2 == 0
        outs = [rows(B_K_WIDTH), cols, cols, decs] * 2 + [rows(B_V_WIDTH)] * 2
    else:
        outs = [rows(B_K_WIDTH)] * 2 + [rows(B_V_WIDTH)]
    return pl.pallas_call(
        functools.partial(_inproj_gla_kernel, chunk, with_q),
        grid=(b, nt),
        in_specs=in_specs,
        out_specs=tuple(o[1] for o in outs),
        out_shape=tuple(o[0] for o in outs),
        compiler_params=_params("parallel", "parallel"),
        name="inproj_gla" if with_q else "inproj_gla_ctx",
    )(x, mod3, norm_g.reshape(1, D), w, wa1, wa2, ba, tri)


def _gla_state_kernel(kf_ref, kb_ref, v_ref, sf_ref, sb_ref):
    for h in range(B_HEADS):
        vh = v_ref[0, :, h * B_VAL_DIM:(h + 1) * B_VAL_DIM]
        ks = slice(h * B_KEY_DIM, (h + 1) * B_KEY_DIM)
        sf_ref[0, h] = _dot_tn(kf_ref[0, :, ks], vh)
        sb_ref[0, h] = _dot_tn(kb_ref[0, :, ks], vh)


def _gla_state(ke_f, ke_b, v):
    b, n, _ = v.shape
    st = jax.ShapeDtypeStruct((b, B_HEADS, B_KEY_DIM, B_VAL_DIM), F32)
    st_spec = pl.BlockSpec((1, B_HEADS, B_KEY_DIM, B_VAL_DIM), lambda bi: (bi, 0, 0, 0))
    return pl.pallas_call(
        _gla_state_kernel,
        grid=(b,),
        in_specs=[pl.BlockSpec((1, n, B_K_WIDTH), lambda bi: (bi, 0, 0))] * 2
        + [pl.BlockSpec((1, n, B_V_WIDTH), lambda bi: (bi, 0, 0))],
        out_specs=(st_spec, st_spec),
        out_shape=(st, st),
        compiler_params=_params("parallel"),
        name="gla_ctx_state",
    )(ke_f, ke_b, v)


def _scan_tile(reverse, qd_ref, kit_ref, ket_ref, dec_ref, v_ref, st_ref):
    pair_rows = 2 * CHUNK
    ti = lax.broadcasted_iota(jnp.int32, (pair_rows, pair_rows), 0)
    si = lax.broadcasted_iota(jnp.int32, (pair_rows, pair_rows), 1)
    same = (ti // CHUNK) == (si // CHUNK)
    if reverse:
        use_inv = same & (ti <= si)
        use_end = (ti < CHUNK) & (si >= CHUNK)
        second_row = lax.broadcasted_iota(jnp.int32, (pair_rows, B_KEY_DIM), 0) < CHUNK
        first_col = lax.broadcasted_iota(jnp.int32, (B_KEY_DIM, pair_rows), 1) >= CHUNK
    else:
        use_inv = same & (ti >= si)
        use_end = (ti >= CHUNK) & (si < CHUNK)
        second_row = lax.broadcasted_iota(jnp.int32, (pair_rows, B_KEY_DIM), 0) >= CHUNK
        first_col = lax.broadcasted_iota(jnp.int32, (B_KEY_DIM, pair_rows), 1) < CHUNK
    n_pairs = SCAN_TILE // pair_rows
    order = range(n_pairs - 1, -1, -1) if reverse else range(n_pairs)
    dec = dec_ref[0, 0]
    dect = jnp.concatenate([dec, jnp.zeros((LANES - DEC_ROWS, B_K_WIDTH), F32)], axis=0).T
    heads = []
    for h in range(B_HEADS):
        ks = slice(h * B_KEY_DIM, (h + 1) * B_KEY_DIM)
        vs = slice(h * B_VAL_DIM, (h + 1) * B_VAL_DIM)
        st = st_ref[h]
        o_rows = [None] * n_pairs
        for p in order:
            rows = slice(p * pair_rows, (p + 1) * pair_rows)
            c_first, c_second = (2 * p + 1, 2 * p) if reverse else (2 * p, 2 * p + 1)
            qd = qd_ref[0, rows, ks]
            ket = ket_ref[0, ks, rows]
            vh = v_ref[0, rows, vs]
            s2 = _dot(qd, jnp.concatenate([kit_ref[0, ks, rows], ket], axis=1))
            a = jnp.where(use_inv, s2[:, 0:pair_rows], jnp.where(use_end, s2[:, pair_rows:], 0.0)).astype(BF16)
            q_pair = jnp.where(second_row, qd.astype(F32) * dec[c_first:c_first + 1, ks], qd.astype(F32))
            k_pair = jnp.where(first_col, ket.astype(F32) * dect[ks, c_second:c_second + 1], ket.astype(F32))
            o_rows[p] = _dot(a, vh) + _dot(q_pair.astype(BF16), st.astype(BF16))
            st = (st * (dect[ks, c_first:c_first + 1] * dect[ks, c_second:c_second + 1])
                  + _dot(k_pair.astype(BF16), vh))
        st_ref[h] = st
        heads.append(jnp.concatenate(o_rows, axis=0))
    return jnp.concatenate(heads, axis=1)


def _gla_scan_kernel(nt, qdf_ref, kitf_ref, ketf_ref, decf_ref, qdb_ref, kitb_ref, ketb_ref, decb_ref,
                     v_ref, sf_ref, sb_ref, sg_ref, x_ref, m_ref, hn_ref, w_ref, fn_ref,
                     o_ref, st_ref, of_ref):
    j = pl.program_id(1)

    @pl.when(j == 0)
    def _():
        st_ref[...] = sf_ref[0]

    @pl.when(j == nt)
    def _():
        st_ref[...] = sb_ref[0]

    @pl.when(j < nt)
    def _():
        of_ref[j] = _scan_tile(False, qdf_ref, kitf_ref, ketf_ref, decf_ref, v_ref, st_ref)

    @pl.when(j >= nt)
    def _():
        o = _scan_tile(True, qdb_ref, kitb_ref, ketb_ref, decb_ref, v_ref, st_ref) + of_ref[2 * nt - 1 - j]
        normed = []
        for h in range(B_HEADS):
            oh = o[:, h * B_VAL_DIM:(h + 1) * B_VAL_DIM]
            normed.append(oh * lax.rsqrt(jnp.mean(oh * oh, axis=-1, keepdims=True) + EPS))
        of = jnp.concatenate(normed, axis=1) * hn_ref[...]
        y = _dot((of * sg_ref[0].astype(F32)).astype(BF16), w_ref[...])
        xn = x_ref[0] + m_ref[0][:, 2 * D:3 * D] * y
        ms = jnp.mean(xn * xn, axis=-1, keepdims=True)
        o_ref[0] = xn * lax.rsqrt(ms + EPS) * fn_ref[...]


def _gla_scan(fwd, bwd, v, s_f, s_b, sg, x, mod3, head_g, w_out, final_g):
    b, s, _ = v.shape
    assert SCAN_TILE == ROW_TILE
    nt = s // SCAN_TILE
    t_fwd = lambda j: jnp.minimum(j, nt - 1)
    t_bwd = lambda j: 2 * nt - 1 - jnp.maximum(j, nt)
    t_both = lambda j: jnp.where(j < nt, j, 2 * nt - 1 - j)

    def direction(t):
        rows = pl.BlockSpec((1, SCAN_TILE, B_K_WIDTH), lambda bi, j: (bi, t(j), 0))
        cols = pl.BlockSpec((1, B_K_WIDTH, SCAN_TILE), lambda bi, j: (bi, 0, t(j)))
        decs = pl.BlockSpec((1, 1, DEC_ROWS, B_K_WIDTH), lambda bi, j: (bi, t(j), 0, 0))
        return [rows, cols, cols, decs]

    const = lambda shape: pl.BlockSpec(shape, lambda bi, j: (0,) * len(shape))
    state = pl.BlockSpec((1, B_HEADS, B_KEY_DIM, B_VAL_DIM), lambda bi, j: (bi, 0, 0, 0))
    out_rows = lambda w_: pl.BlockSpec((1, SCAN_TILE, w_), lambda bi, j: (bi, t_bwd(j), 0))
    in_specs = direction(t_fwd) + direction(t_bwd) + [
        pl.BlockSpec((1, SCAN_TILE, B_V_WIDTH), lambda bi, j: (bi, t_both(j), 0)),
        state, state, out_rows(B_V_WIDTH), out_rows(D),
        pl.BlockSpec((1, 1, 3 * D), lambda bi, j: (bi, 0, 0)),
        const((1, B_V_WIDTH)), const((B_V_WIDTH, D)), const((1, D))]
    return pl.pallas_call(
        functools.partial(_gla_scan_kernel, nt),
        grid=(b, 2 * nt),
        in_specs=in_specs,
        out_specs=out_rows(D),
        out_shape=jax.ShapeDtypeStruct((b, s, D), F32),
        scratch_shapes=[pltpu.VMEM((B_HEADS, B_KEY_DIM, B_VAL_DIM), F32),
                        pltpu.VMEM((nt, SCAN_TILE, B_V_WIDTH), F32)],
        compiler_params=_params("parallel", "arbitrary"),
        name="gla_scan",
    )(*fwd, *bwd, v, s_f, s_b, sg, x, mod3, head_g.reshape(1, B_V_WIDTH), w_out, final_g.reshape(1, D))


def _rope_tables(n_tokens):
    rows_n = n_tokens // GRID_W
    row = np.repeat(np.arange(rows_n, dtype=np.float32), GRID_W)
    col = np.tile(np.arange(GRID_W, dtype=np.float32), rows_n)
    inv_freq = (np.float32(ROPE_BASE) ** (-np.arange(ROPE_FREQS, dtype=np.float32) / np.float32(ROPE_FREQS)))
    inv_freq = inv_freq.astype(np.float32)
    ang = np.stack([row[:, None] * inv_freq, col[:, None] * inv_freq], axis=1)
    cos, sin = np.cos(ang).astype(np.float32), np.sin(ang).astype(np.float32)
    zero = np.zeros_like(sin)
    tile = lambda t: np.tile(t.reshape(n_tokens, A_HEAD_DIM), (1, LANES // A_HEAD_DIM))
    return (tile(np.stack([cos, cos], axis=2)),
            tile(np.stack([-sin, zero], axis=2)),
            tile(np.stack([zero, sin], axis=2)))


def _pair_heads(t, axis):
    shape = t.shape
    t = t.reshape(shape[:axis] + (A_KV_HEADS, A_GROUP, -1) + shape[axis + 1:])
    return jnp.swapaxes(t, axis, axis + 1).reshape(shape)


def kernel(x, c, ctx, c_ctx, l0_norm_g, l0_w_ada, l0_b_ada, l0_w_in, l0_sink, l0_w_out, l1_norm_g, l1_w_ada, l1_b_ada, l1_w_in, l1_wa1_f, l1_wa2_f, l1_ba_f, l1_wa1_b, l1_wa2_b, l1_ba_b, l1_head_norm_g, l1_w_out, final_norm_g):
    b, s, _ = x.shape
    ctx_row = b

    cvec = jnp.concatenate([c, c_ctx[None, :], jnp.zeros((MOD_ROWS - b - 1, D), F32)], axis=0)
    mod0 = _modulation(cvec, l0_w_ada, l0_b_ada).reshape(MOD_ROWS, 1, 3 * D)
    mod1 = _modulation(cvec, l1_w_ada, l1_b_ada).reshape(MOD_ROWS, 1, 3 * D)

    q_end, kv_end = A_WIDTH, A_WIDTH + 2 * A_KV_WIDTH
    w0 = (_pair_heads(l0_w_in[:, :q_end].astype(BF16), 1), l0_w_in[:, q_end:kv_end].astype(BF16),
          _pair_heads(l0_w_in[:, kv_end:].astype(BF16), 1))
    w0_out = _pair_heads(l0_w_out.astype(BF16), 0)
    sink = _pair_heads(l0_sink.astype(F32), 0)
    qt, k, vt, sg = _inproj_attn(x, mod0, None, l0_norm_g, w0, _rope_tables(s))
    qct, kc, vct, sgc = _inproj_attn(ctx, mod0, ctx_row, l0_norm_g, w0, None)
    x1 = _attention(sink, qt, k, vt, kc, vct, sg, x, mod0, None, w0_out, True)
    xc1 = _attention(sink, qct, None, None, kc, vct, sgc, ctx, mod0, ctx_row, w0_out, False)

    w1 = l1_w_in.astype(BF16)
    wa1 = jnp.concatenate([l1_wa1_f, l1_wa1_b, jnp.zeros((D, LANES - 2 * GATE_RANK), F32)], axis=1).astype(BF16)
    wa2 = jnp.zeros((LANES, 2 * B_K_WIDTH), F32)
    wa2 = wa2.at[0:GATE_RANK, 0:B_K_WIDTH].set(l1_wa2_f)
    wa2 = wa2.at[GATE_RANK:2 * GATE_RANK, B_K_WIDTH:].set(l1_wa2_b).astype(BF16)
    ba = jnp.concatenate([l1_ba_f, l1_ba_b]).reshape(1, 2 * B_K_WIDTH)
    kec_f, kec_b, vcx = _inproj_gla(xc1, mod1, ctx_row, l1_norm_g, w1, wa1, wa2, ba, ctx.shape[1], False)
    s_f, s_b = _gla_state(kec_f, kec_b, vcx)
    outs = _inproj_gla(x1, mod1, None, l1_norm_g, w1, wa1, wa2, ba, CHUNK, True)
    v1, sg1 = outs[8:]
    return _gla_scan(outs[0:4], outs[4:8], v1, s_f, s_b, sg1, x1, mod1,
                     l1_head_norm_g, l1_w_out.astype(BF16), final_norm_g)
```

```python
import functools

import jax
import jax.numpy as jnp
import numpy as np
from jax import lax
from jax.experimental import pallas as pl
from jax.experimental.pallas import tpu as pltpu

F32 = jnp.float32
BF16 = jnp.bfloat16

D = 1024
GRID_W = 64
EPS = 1e-6
NEG_INF = -1e30

A_HEADS = 16
A_KV_HEADS = 2
A_GROUP = A_HEADS // A_KV_HEADS
A_HEAD_DIM = 64
A_WIDTH = A_HEADS * A_HEAD_DIM
A_KV_WIDTH = A_KV_HEADS * A_HEAD_DIM
BLOCK = 128
ROPE_BASE = 10000.0
ROPE_FREQS = A_HEAD_DIM // 4
Q_SCALE = A_HEAD_DIM ** -0.5
LOG2_E = 1.4426950408889634

B_HEADS = 4
B_K_WIDTH = D // 2
B_V_WIDTH = D
B_KEY_DIM = B_K_WIDTH // B_HEADS
B_VAL_DIM = B_V_WIDTH // B_HEADS
GATE_RANK = 16
GATE_TEMP = 16.0
CHUNK = 64
K_SCALE = B_KEY_DIM ** -0.5

LANES = 128
MXU_DIM = 256
MOD_ROWS = 16
ROW_TILE = 512
SCAN_TILE = 512
SUB_ROWS = 256
DEC_ROWS = 8
SUM_ROWS = 16
A_BLOCKS_PER_STEP = 4
VMEM_LIMIT = 48 * 1024 * 1024


def _params(*sem):
    return pltpu.CompilerParams(dimension_semantics=sem, vmem_limit_bytes=VMEM_LIMIT)


def _silu(x):
    return x / (1.0 + jnp.exp(-x))


def _dot(a, b):
    return jnp.dot(a, b, preferred_element_type=F32)


def _dot_nt(a, b):
    return lax.dot_general(a, b, (((1,), (1,)), ((), ())), preferred_element_type=F32)


def _dot_tn(a, b):
    return lax.dot_general(a, b, (((0,), (0,)), ((), ())), preferred_element_type=F32)


def _norm_mod(x, g, m):
    ms = jnp.mean(x * x, axis=-1, keepdims=True)
    y = x * lax.rsqrt(ms + EPS) * g
    return y * (1.0 + m[:, D:2 * D]) + m[:, 0:D]


def _mod_kernel(c_ref, w_ref, b_ref, o_ref):
    s = _silu(c_ref[...])
    o_ref[...] = _dot(s.astype(BF16), w_ref[...].astype(BF16)) + b_ref[...]


def _modulation(cvec, w_ada, b_ada):
    n = w_ada.shape[1] // D
    return pl.pallas_call(
        _mod_kernel,
        grid=(n,),
        in_specs=[
            pl.BlockSpec((MOD_ROWS, D), lambda j: (0, 0)),
            pl.BlockSpec((D, D), lambda j: (0, j)),
            pl.BlockSpec((1, D), lambda j: (0, j)),
        ],
        out_specs=pl.BlockSpec((MOD_ROWS, D), lambda j: (0, j)),
        out_shape=jax.ShapeDtypeStruct((MOD_ROWS, n * D), F32),
        compiler_params=_params("arbitrary"),
        name="modulation",
    )(cvec, w_ada, b_ada.reshape(1, n * D))


def _rope(t, cos, sin_hi, sin_lo):
    return t * cos + pltpu.roll(t, LANES - ROPE_FREQS, 1) * sin_hi + pltpu.roll(t, ROPE_FREQS, 1) * sin_lo


def _inproj_attn_kernel(rope, x_ref, m_ref, g_ref, wq_ref, wkv_ref, wg_ref, *rest):
    if rope:
        cos_ref, shi_ref, slo_ref, qt_ref, k_ref, vt_ref, sg_ref = rest
    else:
        qt_ref, k_ref, vt_ref, sg_ref = rest
    for r in range(x_ref.shape[1] // SUB_ROWS):
        rows = slice(r * SUB_ROWS, (r + 1) * SUB_ROWS)
        if rope:
            cos, shi, slo = cos_ref[rows], shi_ref[rows], slo_ref[rows]
        hb = _norm_mod(x_ref[0, rows], g_ref[...], m_ref[0]).astype(BF16)
        q = _dot(hb, wq_ref[...])
        for j in range(A_WIDTH // LANES):
            qj = q[:, j * LANES:(j + 1) * LANES]
            if rope:
                qj = _rope(qj, cos, shi, slo)
            qt_ref[0, j * LANES:(j + 1) * LANES, rows] = (qj * (Q_SCALE * LOG2_E)).T.astype(BF16)
        kv = _dot(hb, wkv_ref[...])
        k = kv[:, 0:A_KV_WIDTH]
        if rope:
            k = _rope(k, cos, shi, slo)
        k_ref[0, rows] = k.astype(BF16)
        vt_ref[0, :, rows] = kv[:, A_KV_WIDTH:].T.astype(BF16)
        g = _dot(hb, wg_ref[...])
        sg_ref[0, rows] = _silu(g).astype(BF16)


def _inproj_attn(x, mod3, mod_row, norm_g, weights, tables):
    b, s, _ = x.shape
    tm = min(ROW_TILE, s)
    rope = tables is not None
    if mod_row is None:
        mod_map = lambda bi, i: (bi, 0, 0)
    else:
        mod_map = lambda bi, i: (mod_row, 0, 0)
    in_specs = [
        pl.BlockSpec((1, tm, D), lambda bi, i: (bi, i, 0)),
        pl.BlockSpec((1, 1, 3 * D), mod_map),
        pl.BlockSpec((1, D), lambda bi, i: (0, 0)),
    ] + [pl.BlockSpec(w.shape, lambda bi, i: (0, 0)) for w in weights]
    args = [x, mod3, norm_g.reshape(1, D), *weights]
    if rope:
        in_specs += [pl.BlockSpec((tm, LANES), lambda bi, i: (i, 0))] * 3
        args += list(tables)
    out_shape = (
        jax.ShapeDtypeStruct((b, A_WIDTH, s), BF16),
        jax.ShapeDtypeStruct((b, s, A_KV_WIDTH), BF16),
        jax.ShapeDtypeStruct((b, A_KV_WIDTH, s), BF16),
        jax.ShapeDtypeStruct((b, s, A_WIDTH), BF16),
    )
    out_specs = (
        pl.BlockSpec((1, A_WIDTH, tm), lambda bi, i: (bi, 0, i)),
        pl.BlockSpec((1, tm, A_KV_WIDTH), lambda bi, i: (bi, i, 0)),
        pl.BlockSpec((1, A_KV_WIDTH, tm), lambda bi, i: (bi, 0, i)),
        pl.BlockSpec((1, tm, A_WIDTH), lambda bi, i: (bi, i, 0)),
    )
    return pl.pallas_call(
        functools.partial(_inproj_attn_kernel, rope),
        grid=(b, s // tm),
        in_specs=in_specs,
        out_specs=out_specs,
        out_shape=out_shape,
        compiler_params=_params("parallel", "parallel"),
        name="inproj_attn_rope" if rope else "inproj_attn_ctx",
    )(*args)


def _attn_block(kwin, vtw, qts, sinks, ok_prev, ok_next):
    n_keys, nq = kwin.shape[0], qts.shape[1]
    lane = lax.broadcasted_iota(jnp.int32, kwin.shape, 1)
    kbd = jnp.concatenate([jnp.where(lane < A_HEAD_DIM, kwin, jnp.zeros_like(kwin)),
                           jnp.where(lane >= A_HEAD_DIM, kwin, jnp.zeros_like(kwin))], axis=0)
    vrow = lax.broadcasted_iota(jnp.int32, vtw.shape, 0)
    vbd = jnp.concatenate([jnp.where(vrow < A_HEAD_DIM, vtw, jnp.zeros_like(vtw)),
                           jnp.where(vrow >= A_HEAD_DIM, vtw, jnp.zeros_like(vtw))], axis=1)
    srow = lax.broadcasted_iota(jnp.int32, (SUM_ROWS, 2 * n_keys), 0)
    scol = lax.broadcasted_iota(jnp.int32, (SUM_ROWS, 2 * n_keys), 1) // n_keys
    vbd = jnp.concatenate([vbd, jnp.where(srow == scol, 1.0, 0.0).astype(BF16)], axis=0)
    head_a = lax.broadcasted_iota(jnp.int32, (LANES, nq), 0) < A_HEAD_DIM

    st = _dot(kbd, qts)
    probs, esink = [], []
    for hh in range(2):
        sh = st[hh * n_keys:(hh + 1) * n_keys]
        if ok_prev is not None:
            parts = [jnp.where(ok_prev, sh[0:BLOCK], NEG_INF),
                     sh[BLOCK:2 * BLOCK],
                     jnp.where(ok_next, sh[2 * BLOCK:3 * BLOCK], NEG_INF),
                     sh[3 * BLOCK:]]
        else:
            parts = [sh]
        mx = functools.reduce(jnp.maximum, [jnp.max(t, axis=0, keepdims=True) for t in parts])
        mx = jnp.maximum(mx, sinks[hh])
        probs += [jnp.exp2(t - mx).astype(BF16) for t in parts]
        esink.append(jnp.exp2(sinks[hh] - mx))
    ot = _dot(vbd, jnp.concatenate(probs, axis=0))
    inv = [1.0 / (ot[LANES + hh:LANES + hh + 1] + esink[hh]) for hh in range(2)]
    return ot[0:LANES] * jnp.where(head_a, inv[0], inv[1])


def _attn_kernel(local, n_steps, sink_ref, qt_ref, *rest):
    n_blk = rest[-1].shape[1] // BLOCK
    if local:
        (kp_ref, kc_ref, kn_ref, vp_ref, vc_ref, vn_ref, kx_ref, vx_ref,
         sg_ref, x_ref, m_ref, w_ref, o_ref) = rest
        last = slice((n_blk - 1) * BLOCK, n_blk * BLOCK)
        k_blocks = ([kp_ref[0, last]] + [kc_ref[0, i * BLOCK:(i + 1) * BLOCK] for i in range(n_blk)]
                    + [kn_ref[0, 0:BLOCK]])
        v_blocks = ([vp_ref[0, :, last]] + [vc_ref[0, :, i * BLOCK:(i + 1) * BLOCK] for i in range(n_blk)]
                    + [vn_ref[0, :, 0:BLOCK]])
    else:
        kx_ref, vx_ref, sg_ref, x_ref, m_ref, w_ref, o_ref = rest
    step = pl.program_id(1)
    nq = A_GROUP * BLOCK
    chunk_of = lax.broadcasted_iota(jnp.int32, (1, nq), 1) // BLOCK
    sinks = []
    for hh in range(2):
        sk = jnp.full((1, nq), sink_ref[hh] * LOG2_E, F32)
        for c in range(1, A_GROUP):
            sk = jnp.where(chunk_of == c, sink_ref[2 * c + hh] * LOG2_E, sk)
        sinks.append(sk)
    if local:
        kj = lax.broadcasted_iota(jnp.int32, (BLOCK, nq), 0)
        qi = lax.broadcasted_iota(jnp.int32, (BLOCK, nq), 1) % BLOCK

    rows_out = []
    for blk in range(n_blk):
        qts = jnp.concatenate([qt_ref[0, c * LANES:(c + 1) * LANES, blk * BLOCK:(blk + 1) * BLOCK]
                               for c in range(A_GROUP)], axis=1)
        if local:
            kwin = jnp.concatenate(k_blocks[blk:blk + 3] + [kx_ref[0]], axis=0)
            vtw = jnp.concatenate(v_blocks[blk:blk + 3] + [vx_ref[0]], axis=1)
            ok_prev = (kj >= qi) & (step > 0) if blk == 0 else (kj >= qi)
            ok_next = (kj <= qi) & (step < n_steps - 1) if blk == n_blk - 1 else (kj <= qi)
        else:
            kwin, vtw, ok_prev, ok_next = kx_ref[0], vx_ref[0], None, None
        ot = _attn_block(kwin, vtw, qts, sinks, ok_prev, ok_next)
        rows = slice(blk * BLOCK, (blk + 1) * BLOCK)
        outs = []
        for c in range(A_GROUP):
            sg = sg_ref[0, rows, c * LANES:(c + 1) * LANES].astype(F32)
            outs.append((ot[:, c * BLOCK:(c + 1) * BLOCK].T * sg).astype(BF16))
        rows_out.append(jnp.concatenate(outs, axis=1))
    y = _dot(jnp.concatenate(rows_out, axis=0), w_ref[...])
    gate = m_ref[0][:, 2 * D:3 * D]
    o_ref[0] = x_ref[0] + gate * y


def _attention(sink, qt, k, vt, kx, vxt, sg, x, mod3, mod_row, w_out, local):
    b, s, _ = sg.shape
    rows = min(A_BLOCKS_PER_STEP * BLOCK, s)
    assert s % rows == 0
    ns = s // rows
    n_ctx = kx.shape[1]
    blk = lambda w: pl.BlockSpec((1, rows, w), lambda bi, i: (bi, i, 0))
    blk_t = lambda w: pl.BlockSpec((1, w, rows), lambda bi, i: (bi, 0, i))
    if mod_row is None:
        mod_map = lambda bi, i: (bi, 0, 0)
    else:
        mod_map = lambda bi, i: (mod_row, 0, 0)
    in_specs = [pl.BlockSpec(memory_space=pltpu.SMEM), blk_t(A_WIDTH)]
    args = [sink, qt]
    if local:
        lo = lambda i: jnp.maximum(i - 1, 0)
        hi = lambda i: jnp.minimum(i + 1, ns - 1)
        in_specs += [pl.BlockSpec((1, rows, A_KV_WIDTH), lambda bi, i: (bi, lo(i), 0)),
                     blk(A_KV_WIDTH),
                     pl.BlockSpec((1, rows, A_KV_WIDTH), lambda bi, i: (bi, hi(i), 0)),
                     pl.BlockSpec((1, A_KV_WIDTH, rows), lambda bi, i: (bi, 0, lo(i))),
                     blk_t(A_KV_WIDTH),
                     pl.BlockSpec((1, A_KV_WIDTH, rows), lambda bi, i: (bi, 0, hi(i)))]
        args += [k, k, k, vt, vt, vt]
    in_specs += [pl.BlockSpec((1, n_ctx, A_KV_WIDTH), lambda bi, i: (bi, 0, 0)),
                 pl.BlockSpec((1, A_KV_WIDTH, n_ctx), lambda bi, i: (bi, 0, 0)),
                 blk(A_WIDTH), blk(D),
                 pl.BlockSpec((1, 1, 3 * D), mod_map),
                 pl.BlockSpec((A_WIDTH, D), lambda bi, i: (0, 0))]
    args += [kx, vxt, sg, x, mod3, w_out]
    return pl.pallas_call(
        functools.partial(_attn_kernel, local, ns),
        grid=(b, ns),
        in_specs=in_specs,
        out_specs=blk(D),
        out_shape=jax.ShapeDtypeStruct((b, s, D), F32),
        compiler_params=_params("parallel", "parallel"),
        name="attn_local" if local else "attn_ctx",
    )(*args)


def _chunk_cumsum(x, tri):
    hi = x.astype(BF16)
    lo = (x - hi.astype(F32)).astype(BF16)
    return _dot(tri, hi) + _dot(tri, lo)


def _inproj_gla_kernel(chunk, with_q, x_ref, m_ref, g_ref, w_ref, wa1_ref, wa2_ref, ba_ref, tri_ref, *outs):
    if with_q:
        per_dir = (outs[0:4], outs[4:8])
        v_ref, sg_ref = outs[8:]
    else:
        per_dir = ((outs[0],), (outs[1],))
        v_ref = outs[2]
    tm = x_ref.shape[1]
    hb = _norm_mod(x_ref[0], g_ref[...], m_ref[0]).astype(BF16)
    r = _dot(hb, wa1_ref[...])
    z = _dot(r.astype(BF16), wa2_ref[...]) + ba_ref[...]
    e = jnp.exp2(jnp.abs(z) * -LOG2_E)
    la = (jnp.minimum(z, 0.0) * LOG2_E - jnp.log2(1.0 + e)) * (1.0 / GATE_TEMP)
    k = _dot(hb, w_ref[:, B_K_WIDTH:2 * B_K_WIDTH])
    if with_q:
        q = _dot(hb, w_ref[:, 0:B_K_WIDTH]) * K_SCALE
    n_chunks = tm // chunk
    for reverse in (False, True):
        lad = la[:, B_K_WIDTH:] if reverse else la[:, 0:B_K_WIDTH]
        refs = per_dir[1] if reverse else per_dir[0]
        tots, kis, kes = [], [], []
        tri = tri_ref[1 if reverse else 0]
        tb = tri.shape[0]
        cum_all = jnp.concatenate([_chunk_cumsum(lad[r * tb:(r + 1) * tb], tri) for r in range(tm // tb)], axis=0)
        for c in range(n_chunks):
            rows = slice(c * chunk, (c + 1) * chunk)
            cum = cum_all[rows]
            tot = cum[0:1] if reverse else cum[chunk - 1:chunk]
            ke = k[rows] * jnp.exp2(tot - cum)
            if not with_q:
                refs[0][0, rows] = ke.astype(BF16)
                continue
            refs[0][0, rows] = (q[rows] * jnp.exp2(cum)).astype(BF16)
            kis.append(k[rows] * jnp.exp2(-cum))
            kes.append(ke)
            tots.append(tot)
            if c % 2 == 1:
                pair = slice((c - 1) * chunk, (c + 1) * chunk)
                refs[1][0, :, pair] = jnp.concatenate(kis[-2:], axis=0).T.astype(BF16)
                refs[2][0, :, pair] = jnp.concatenate(kes[-2:], axis=0).T.astype(BF16)
        if with_q:
            pad = [jnp.zeros((DEC_ROWS - n_chunks, B_K_WIDTH), F32)] if n_chunks < DEC_ROWS else []
            refs[3][0, 0] = jnp.exp2(jnp.concatenate(tots + pad, axis=0))
    if with_q:
        sg_ref[0] = _silu(_dot(hb, w_ref[:, 2 * B_K_WIDTH + B_V_WIDTH:])).astype(BF16)
    v_ref[0] = _dot(hb, w_ref[:, 2 * B_K_WIDTH:2 * B_K_WIDTH + B_V_WIDTH]).astype(BF16)


def _inproj_gla(x, mod3, mod_row, norm_g, w, wa1, wa2, ba, chunk, with_q):
    b, s, _ = x.shape
    tm = min(ROW_TILE, s)
    nt = s // tm
    cpt = tm // chunk
    tb = max(chunk, MXU_DIM)
    assert tm % tb == 0 and tb % chunk == 0
    if mod_row is None:
        mod_map = lambda bi, i: (bi, 0, 0)
    else:
        mod_map = lambda bi, i: (mod_row, 0, 0)
    const = lambda shape: pl.BlockSpec(shape, lambda bi, i: (0,) * len(shape))
    in_specs = [
        pl.BlockSpec((1, tm, D), lambda bi, i: (bi, i, 0)),
        pl.BlockSpec((1, 1, 3 * D), mod_map),
        const((1, D)), const(w.shape), const(wa1.shape), const(wa2.shape), const(ba.shape),
        const((2, tb, tb)),
    ]
    t_idx = np.arange(tb)
    same = (t_idx[:, None] // chunk) == (t_idx[None, :] // chunk)
    lower = same & (t_idx[None, :] <= t_idx[:, None])
    tri = jnp.asarray(np.stack([lower, lower.T]), dtype=BF16)
    rows = lambda w_: (jax.ShapeDtypeStruct((b, s, w_), BF16),
                       pl.BlockSpec((1, tm, w_), lambda bi, i: (bi, i, 0)))
    cols = (jax.ShapeDtypeStruct((b, B_K_WIDTH, s), BF16),
            pl.BlockSpec((1, B_K_WIDTH, tm), lambda bi, i: (bi, 0, i)))
    decs = (jax.ShapeDtypeStruct((b, nt, DEC_ROWS, B_K_WIDTH), F32),
            pl.BlockSpec((1, 1, DEC_ROWS, B_K_WIDTH), lambda bi, i: (bi, i, 0, 0)))
    if with_q:
        assert cpt <= DEC_ROWS and cpt % 2 == 0
        outs = [rows(B_K_WIDTH), cols, cols, decs] * 2 + [rows(B_V_WIDTH)] * 2
    else:
        outs = [rows(B_K_WIDTH)] * 2 + [rows(B_V_WIDTH)]
    return pl.pallas_call(
        functools.partial(_inproj_gla_kernel, chunk, with_q),
        grid=(b, nt),
        in_specs=in_specs,
        out_specs=tuple(o[1] for o in outs),
        out_shape=tuple(o[0] for o in outs),
        compiler_params=_params("parallel", "parallel"),
        name="inproj_gla" if with_q else "inproj_gla_ctx",
    )(x, mod3, norm_g.reshape(1, D), w, wa1, wa2, ba, tri)


def _gla_state_kernel(kf_ref, kb_ref, v_ref, sf_ref, sb_ref):
    for h in range(B_HEADS):
        vh = v_ref[0, :, h * B_VAL_DIM:(h + 1) * B_VAL_DIM]
        ks = slice(h * B_KEY_DIM, (h + 1) * B_KEY_DIM)
        sf_ref[0, h] = _dot_tn(kf_ref[0, :, ks], vh)
        sb_ref[0, h] = _dot_tn(kb_ref[0, :, ks], vh)


def _gla_state(ke_f, ke_b, v):
    b, n, _ = v.shape
    st = jax.ShapeDtypeStruct((b, B_HEADS, B_KEY_DIM, B_VAL_DIM), F32)
    st_spec = pl.BlockSpec((1, B_HEADS, B_KEY_DIM, B_VAL_DIM), lambda bi: (bi, 0, 0, 0))
    return pl.pallas_call(
        _gla_state_kernel,
        grid=(b,),
        in_specs=[pl.BlockSpec((1, n, B_K_WIDTH), lambda bi: (bi, 0, 0))] * 2
        + [pl.BlockSpec((1, n, B_V_WIDTH), lambda bi: (bi, 0, 0))],
        out_specs=(st_spec, st_spec),
        out_shape=(st, st),
        compiler_params=_params("parallel"),
        name="gla_ctx_state",
    )(ke_f, ke_b, v)


def _scan_tile(reverse, qd_ref, kit_ref, ket_ref, dec_ref, v_ref, st_ref):
    pair_rows = 2 * CHUNK
    ti = lax.broadcasted_iota(jnp.int32, (pair_rows, pair_rows), 0)
    si = lax.broadcasted_iota(jnp.int32, (pair_rows, pair_rows), 1)
    same = (ti // CHUNK) == (si // CHUNK)
    if reverse:
        use_inv = same & (ti <= si)
        use_end = (ti < CHUNK) & (si >= CHUNK)
        second_row = lax.broadcasted_iota(jnp.int32, (pair_rows, B_KEY_DIM), 0) < CHUNK
        first_col = lax.broadcasted_iota(jnp.int32, (B_KEY_DIM, pair_rows), 1) >= CHUNK
    else:
        use_inv = same & (ti >= si)
        use_end = (ti >= CHUNK) & (si < CHUNK)
        second_row = lax.broadcasted_iota(jnp.int32, (pair_rows, B_KEY_DIM), 0) >= CHUNK
        first_col = lax.broadcasted_iota(jnp.int32, (B_KEY_DIM, pair_rows), 1) < CHUNK
    n_pairs = SCAN_TILE // pair_rows
    order = range(n_pairs - 1, -1, -1) if reverse else range(n_pairs)
    dec = dec_ref[0, 0]
    dect = jnp.concatenate([dec, jnp.zeros((LANES - DEC_ROWS, B_K_WIDTH), F32)], axis=0).T
    heads = []
    for h in range(B_HEADS):
        ks = slice(h * B_KEY_DIM, (h + 1) * B_KEY_DIM)
        vs = slice(h * B_VAL_DIM, (h + 1) * B_VAL_DIM)
        st = st_ref[h]
        o_rows = [None] * n_pairs
        for p in order:
            rows = slice(p * pair_rows, (p + 1) * pair_rows)
            c_first, c_second = (2 * p + 1, 2 * p) if reverse else (2 * p, 2 * p + 1)
            qd = qd_ref[0, rows, ks]
            ket = ket_ref[0, ks, rows]
            vh = v_ref[0, rows, vs]
            s2 = _dot(qd, jnp.concatenate([kit_ref[0, ks, rows], ket], axis=1))
            a = jnp.where(use_inv, s2[:, 0:pair_rows], jnp.where(use_end, s2[:, pair_rows:], 0.0)).astype(BF16)
            q_pair = jnp.where(second_row, qd.astype(F32) * dec[c_first:c_first + 1, ks], qd.astype(F32))
            k_pair = jnp.where(first_col, ket.astype(F32) * dect[ks, c_second:c_second + 1], ket.astype(F32))
            o_rows[p] = _dot(a, vh) + _dot(q_pair.astype(BF16), st.astype(BF16))
            st = (st * (dect[ks, c_first:c_first + 1] * dect[ks, c_second:c_second + 1])
                  + _dot(k_pair.astype(BF16), vh))
        st_ref[h] = st
        heads.append(jnp.concatenate(o_rows, axis=0))
    return jnp.concatenate(heads, axis=1)


def _gla_scan_kernel(nt, qdf_ref, kitf_ref, ketf_ref, decf_ref, qdb_ref, kitb_ref, ketb_ref, decb_ref,
                     v_ref, sf_ref, sb_ref, sg_ref, x_ref, m_ref, hn_ref, w_ref, fn_ref,
                     o_ref, st_ref, of_ref):
    j = pl.program_id(1)

    @pl.when(j == 0)
    def _():
        st_ref[...] = sf_ref[0]

    @pl.when(j == nt)
    def _():
        st_ref[...] = sb_ref[0]

    @pl.when(j < nt)
    def _():
        of_ref[j] = _scan_tile(False, qdf_ref, kitf_ref, ketf_ref, decf_ref, v_ref, st_ref)

    @pl.when(j >= nt)
    def _():
        o = _scan_tile(True, qdb_ref, kitb_ref, ketb_ref, decb_ref, v_ref, st_ref) + of_ref[2 * nt - 1 - j]
        normed = []
        for h in range(B_HEADS):
            oh = o[:, h * B_VAL_DIM:(h + 1) * B_VAL_DIM]
            normed.append(oh * lax.rsqrt(jnp.mean(oh * oh, axis=-1, keepdims=True) + EPS))
        of = jnp.concatenate(normed, axis=1) * hn_ref[...]
        y = _dot((of * sg_ref[0].astype(F32)).astype(BF16), w_ref[...])
        xn = x_ref[0] + m_ref[0][:, 2 * D:3 * D] * y
        ms = jnp.mean(xn * xn, axis=-1, keepdims=True)
        o_ref[0] = xn * lax.rsqrt(ms + EPS) * fn_ref[...]


def _gla_scan(fwd, bwd, v, s_f, s_b, sg, x, mod3, head_g, w_out, final_g):
    b, s, _ = v.shape
    assert SCAN_TILE == ROW_TILE
    nt = s // SCAN_TILE
    t_fwd = lambda j: jnp.minimum(j, nt - 1)
    t_bwd = lambda j: 2 * nt - 1 - jnp.maximum(j, nt)
    t_both = lambda j: jnp.where(j < nt, j, 2 * nt - 1 - j)

    def direction(t):
        rows = pl.BlockSpec((1, SCAN_TILE, B_K_WIDTH), lambda bi, j: (bi, t(j), 0))
        cols = pl.BlockSpec((1, B_K_WIDTH, SCAN_TILE), lambda bi, j: (bi, 0, t(j)))
        decs = pl.BlockSpec((1, 1, DEC_ROWS, B_K_WIDTH), lambda bi, j: (bi, t(j), 0, 0))
        return [rows, cols, cols, decs]

    const = lambda shape: pl.BlockSpec(shape, lambda bi, j: (0,) * len(shape))
    state = pl.BlockSpec((1, B_HEADS, B_KEY_DIM, B_VAL_DIM), lambda bi, j: (bi, 0, 0, 0))
    out_rows = lambda w_: pl.BlockSpec((1, SCAN_TILE, w_), lambda bi, j: (bi, t_bwd(j), 0))
    in_specs = direction(t_fwd) + direction(t_bwd) + [
        pl.BlockSpec((1, SCAN_TILE, B_V_WIDTH), lambda bi, j: (bi, t_both(j), 0)),
        state, state, out_rows(B_V_WIDTH), out_rows(D),
        pl.BlockSpec((1, 1, 3 * D), lambda bi, j: (bi, 0, 0)),
        const((1, B_V_WIDTH)), const((B_V_WIDTH, D)), const((1, D))]
    return pl.pallas_call(
        functools.partial(_gla_scan_kernel, nt),
        grid=(b, 2 * nt),
        in_specs=in_specs,
        out_specs=out_rows(D),
        out_shape=jax.ShapeDtypeStruct((b, s, D), F32),
        scratch_shapes=[pltpu.VMEM((B_HEADS, B_KEY_DIM, B_VAL_DIM), F32),
                        pltpu.VMEM((nt, SCAN_TILE, B_V_WIDTH), F32)],
        compiler_params=_params("parallel", "arbitrary"),
        name="gla_scan",
    )(*fwd, *bwd, v, s_f, s_b, sg, x, mod3, head_g.reshape(1, B_V_WIDTH), w_out, final_g.reshape(1, D))


def _rope_tables(n_tokens):
    rows_n = n_tokens // GRID_W
    row = np.repeat(np.arange(rows_n, dtype=np.float32), GRID_W)
    col = np.tile(np.arange(GRID_W, dtype=np.float32), rows_n)
    inv_freq = (np.float32(ROPE_BASE) ** (-np.arange(ROPE_FREQS, dtype=np.float32) / np.float32(ROPE_FREQS)))
    inv_freq = inv_freq.astype(np.float32)
    ang = np.stack([row[:, None] * inv_freq, col[:, None] * inv_freq], axis=1)
    cos, sin = np.cos(ang).astype(np.float32), np.sin(ang).astype(np.float32)
    zero = np.zeros_like(sin)
    tile = lambda t: np.tile(t.reshape(n_tokens, A_HEAD_DIM), (1, LANES // A_HEAD_DIM))
    return (tile(np.stack([cos, cos], axis=2)),
            tile(np.stack([-sin, zero], axis=2)),
            tile(np.stack([zero, sin], axis=2)))


def _pair_heads(t, axis):
    shape = t.shape
    t = t.reshape(shape[:axis] + (A_KV_HEADS, A_GROUP, -1) + shape[axis + 1:])
    return jnp.swapaxes(t, axis, axis + 1).reshape(shape)


def kernel(x, c, ctx, c_ctx, l0_norm_g, l0_w_ada, l0_b_ada, l0_w_in, l0_sink, l0_w_out, l1_norm_g, l1_w_ada, l1_b_ada, l1_w_in, l1_wa1_f, l1_wa2_f, l1_ba_f, l1_wa1_b, l1_wa2_b, l1_ba_b, l1_head_norm_g, l1_w_out, final_norm_g):
    b, s, _ = x.shape
    ctx_row = b

    cvec = jnp.concatenate([c, c_ctx[None, :], jnp.zeros((MOD_ROWS - b - 1, D), F32)], axis=0)
    mod0 = _modulation(cvec, l0_w_ada, l0_b_ada).reshape(MOD_ROWS, 1, 3 * D)
    mod1 = _modulation(cvec, l1_w_ada, l1_b_ada).reshape(MOD_ROWS, 1, 3 * D)

    q_end, kv_end = A_WIDTH, A_WIDTH + 2 * A_KV_WIDTH
    w0 = (_pair_heads(l0_w_in[:, :q_end].astype(BF16), 1), l0_w_in[:, q_end:kv_end].astype(BF16),
          _pair_heads(l0_w_in[:, kv_end:].astype(BF16), 1))
    w0_out = _pair_heads(l0_w_out.astype(BF16), 0)
    sink = _pair_heads(l0_sink.astype(F32), 0)
    qt, k, vt, sg = _inproj_attn(x, mod0, None, l0_norm_g, w0, _rope_tables(s))
    qct, kc, vct, sgc = _inproj_attn(ctx, mod0, ctx_row, l0_norm_g, w0, None)
    x1 = _attention(sink, qt, k, vt, kc, vct, sg, x, mod0, None, w0_out, True)
    xc1 = _attention(sink, qct, None, None, kc, vct, sgc, ctx, mod0, ctx_row, w0_out, False)

    w1 = l1_w_in.astype(BF16)
    wa1 = jnp.concatenate([l1_wa1_f, l1_wa1_b, jnp.zeros((D, LANES - 2 * GATE_RANK), F32)], axis=1).astype(BF16)
    wa2 = jnp.zeros((LANES, 2 * B_K_WIDTH), F32)
    wa2 = wa2.at[0:GATE_RANK, 0:B_K_WIDTH].set(l1_wa2_f)
    wa2 = wa2.at[GATE_RANK:2 * GATE_RANK, B_K_WIDTH:].set(l1_wa2_b).astype(BF16)
    ba = jnp.concatenate([l1_ba_f, l1_ba_b]).reshape(1, 2 * B_K_WIDTH)
    kec_f, kec_b, vcx = _inproj_gla(xc1, mod1, ctx_row, l1_norm_g, w1, wa1, wa2, ba, ctx.shape[1], False)
    s_f, s_b = _gla_state(kec_f, kec_b, vcx)
    outs = _inproj_gla(x1, mod1, None, l1_norm_g, w1, wa1, wa2, ba, CHUNK, True)
    v1, sg1 = outs[8:]
    return _gla_scan(outs[0:4], outs[4:8], v1, s_f, s_b, sg1, x1, mod1,
                     l1_head_norm_g, l1_w_out.astype(BF16), final_norm_g)
```

```python
import functools

import jax
import jax.numpy as jnp
import numpy as np
from jax import lax
from jax.experimental import pallas as pl
from jax.experimental.pallas import tpu as pltpu

F32 = jnp.float32
BF16 = jnp.bfloat16

D = 1024
GRID_W = 64
EPS = 1e-6
NEG_INF = -1e30

A_HEADS = 16
A_KV_HEADS = 2
A_GROUP = A_HEADS // A_KV_HEADS
A_HEAD_DIM = 64
A_WIDTH = A_HEADS * A_HEAD_DIM
A_KV_WIDTH = A_KV_HEADS * A_HEAD_DIM
BLOCK = 128
ROPE_BASE = 10000.0
ROPE_FREQS = A_HEAD_DIM // 4
Q_SCALE = A_HEAD_DIM ** -0.5
LOG2_E = 1.4426950408889634

B_HEADS = 4
B_K_WIDTH = D // 2
B_V_WIDTH = D
B_KEY_DIM = B_K_WIDTH // B_HEADS
B_VAL_DIM = B_V_WIDTH // B_HEADS
GATE_RANK = 16
GATE_TEMP = 16.0
CHUNK = 64
K_SCALE = B_KEY_DIM ** -0.5

LANES = 128
MXU_DIM = 256
MOD_ROWS = 16
ROW_TILE = 512
SCAN_TILE = 1024
SUB_ROWS = 256
DEC_ROWS = 8
SUM_ROWS = 16
A_BLOCKS_PER_STEP = 8
VMEM_LIMIT = 48 * 1024 * 1024
SCAN_VMEM_LIMIT = 56 * 1024 * 1024


def _params(*sem, vmem_limit=VMEM_LIMIT):
    return pltpu.CompilerParams(dimension_semantics=sem, vmem_limit_bytes=vmem_limit)


def _silu(x):
    return x / (1.0 + jnp.exp(-x))


def _dot(a, b):
    return jnp.dot(a, b, preferred_element_type=F32)


def _dot_nt(a, b):
    return lax.dot_general(a, b, (((1,), (1,)), ((), ())), preferred_element_type=F32)


def _dot_tn(a, b):
    return lax.dot_general(a, b, (((0,), (0,)), ((), ())), preferred_element_type=F32)


def _norm_mod(x, g, m):
    ms = jnp.mean(x * x, axis=-1, keepdims=True)
    y = x * lax.rsqrt(ms + EPS) * g
    return y * (1.0 + m[:, D:2 * D]) + m[:, 0:D]


def _mod_kernel(c_ref, w_ref, b_ref, o_ref):
    s = _silu(c_ref[...])
    o_ref[...] = _dot(s.astype(BF16), w_ref[...].astype(BF16)) + b_ref[...]


def _modulation(cvec, w_ada, b_ada):
    n = w_ada.shape[1] // D
    return pl.pallas_call(
        _mod_kernel,
        grid=(n,),
        in_specs=[
            pl.BlockSpec((MOD_ROWS, D), lambda j: (0, 0)),
            pl.BlockSpec((D, D), lambda j: (0, j)),
            pl.BlockSpec((1, D), lambda j: (0, j)),
        ],
        out_specs=pl.BlockSpec((MOD_ROWS, D), lambda j: (0, j)),
        out_shape=jax.ShapeDtypeStruct((MOD_ROWS, n * D), F32),
        compiler_params=_params("arbitrary"),
        name="modulation",
    )(cvec, w_ada, b_ada.reshape(1, n * D))


def _rope(t, cos, sin_hi, sin_lo):
    return t * cos + pltpu.roll(t, LANES - ROPE_FREQS, 1) * sin_hi + pltpu.roll(t, ROPE_FREQS, 1) * sin_lo


def _inproj_attn_kernel(rope, x_ref, m_ref, g_ref, wq_ref, wkv_ref, wg_ref, *rest):
    if rope:
        cos_ref, shi_ref, slo_ref, qt_ref, k_ref, vt_ref, sg_ref = rest
    else:
        qt_ref, k_ref, vt_ref, sg_ref = rest
    for r in range(x_ref.shape[1] // SUB_ROWS):
        rows = slice(r * SUB_ROWS, (r + 1) * SUB_ROWS)
        if rope:
            cos, shi, slo = cos_ref[rows], shi_ref[rows], slo_ref[rows]
        hb = _norm_mod(x_ref[0, rows], g_ref[...], m_ref[0]).astype(BF16)
        q = _dot(hb, wq_ref[...])
        for j in range(A_WIDTH // LANES):
            qj = q[:, j * LANES:(j + 1) * LANES]
            if rope:
                qj = _rope(qj, cos, shi, slo)
            qt_ref[0, j * LANES:(j + 1) * LANES, rows] = (qj * (Q_SCALE * LOG2_E)).T.astype(BF16)
        kv = _dot(hb, wkv_ref[...])
        k = kv[:, 0:A_KV_WIDTH]
        if rope:
            k = _rope(k, cos, shi, slo)
        k_ref[0, rows] = k.astype(BF16)
        vt_ref[0, :, rows] = kv[:, A_KV_WIDTH:].T.astype(BF16)
        g = _dot(hb, wg_ref[...])
        sg_ref[0, rows] = _silu(g).astype(BF16)


def _inproj_attn(x, mod3, mod_row, norm_g, weights, tables):
    b, s, _ = x.shape
    tm = min(ROW_TILE, s)
    rope = tables is not None
    if mod_row is None:
        mod_map = lambda bi, i: (bi, 0, 0)
    else:
        mod_map = lambda bi, i: (mod_row, 0, 0)
    in_specs = [
        pl.BlockSpec((1, tm, D), lambda bi, i: (bi, i, 0)),
        pl.BlockSpec((1, 1, 3 * D), mod_map),
        pl.BlockSpec((1, D), lambda bi, i: (0, 0)),
    ] + [pl.BlockSpec(w.shape, lambda bi, i: (0, 0)) for w in weights]
    args = [x, mod3, norm_g.reshape(1, D), *weights]
    if rope:
        in_specs += [pl.BlockSpec((tm, LANES), lambda bi, i: (i, 0))] * 3
        args += list(tables)
    out_shape = (
        jax.ShapeDtypeStruct((b, A_WIDTH, s), BF16),
        jax.ShapeDtypeStruct((b, s, A_KV_WIDTH), BF16),
        jax.ShapeDtypeStruct((b, A_KV_WIDTH, s), BF16),
        jax.ShapeDtypeStruct((b, s, A_WIDTH), BF16),
    )
    out_specs = (
        pl.BlockSpec((1, A_WIDTH, tm), lambda bi, i: (bi, 0, i)),
        pl.BlockSpec((1, tm, A_KV_WIDTH), lambda bi, i: (bi, i, 0)),
        pl.BlockSpec((1, A_KV_WIDTH, tm), lambda bi, i: (bi, 0, i)),
        pl.BlockSpec((1, tm, A_WIDTH), lambda bi, i: (bi, i, 0)),
    )
    return pl.pallas_call(
        functools.partial(_inproj_attn_kernel, rope),
        grid=(b, s // tm),
        in_specs=in_specs,
        out_specs=out_specs,
        out_shape=out_shape,
        compiler_params=_params("parallel", "parallel"),
        name="inproj_attn_rope" if rope else "inproj_attn_ctx",
    )(*args)


def _attn_block(kwin, vtw, qts, sinks, ok_prev, ok_next):
    n_keys, nq = kwin.shape[0], qts.shape[1]
    lane = lax.broadcasted_iota(jnp.int32, kwin.shape, 1)
    kbd = jnp.concatenate([jnp.where(lane < A_HEAD_DIM, kwin, jnp.zeros_like(kwin)),
                           jnp.where(lane >= A_HEAD_DIM, kwin, jnp.zeros_like(kwin))], axis=0)
    vrow = lax.broadcasted_iota(jnp.int32, vtw.shape, 0)
    vbd = jnp.concatenate([jnp.where(vrow < A_HEAD_DIM, vtw, jnp.zeros_like(vtw)),
                           jnp.where(vrow >= A_HEAD_DIM, vtw, jnp.zeros_like(vtw))], axis=1)
    srow = lax.broadcasted_iota(jnp.int32, (SUM_ROWS, 2 * n_keys), 0)
    scol = lax.broadcasted_iota(jnp.int32, (SUM_ROWS, 2 * n_keys), 1) // n_keys
    vbd = jnp.concatenate([vbd, jnp.where(srow == scol, 1.0, 0.0).astype(BF16)], axis=0)
    head_a = lax.broadcasted_iota(jnp.int32, (LANES, nq), 0) < A_HEAD_DIM

    st = _dot(kbd, qts)
    probs, esink = [], []
    for hh in range(2):
        sh = st[hh * n_keys:(hh + 1) * n_keys]
        if ok_prev is not None:
            parts = [jnp.where(ok_prev, sh[0:BLOCK], NEG_INF),
                     sh[BLOCK:2 * BLOCK],
                     jnp.where(ok_next, sh[2 * BLOCK:3 * BLOCK], NEG_INF),
                     sh[3 * BLOCK:]]
        else:
            parts = [sh]
        mx = functools.reduce(jnp.maximum, [jnp.max(t, axis=0, keepdims=True) for t in parts])
        mx = jnp.maximum(mx, sinks[hh])
        probs += [jnp.exp2(t - mx).astype(BF16) for t in parts]
        esink.append(jnp.exp2(sinks[hh] - mx))
    ot = _dot(vbd, jnp.concatenate(probs, axis=0))
    inv = [1.0 / (ot[LANES + hh:LANES + hh + 1] + esink[hh]) for hh in range(2)]
    return ot[0:LANES] * jnp.where(head_a, inv[0], inv[1])


def _attn_kernel(local, n_steps, sink_ref, qt_ref, *rest):
    n_blk = rest[-1].shape[1] // BLOCK
    if local:
        (kp_ref, kc_ref, kn_ref, vp_ref, vc_ref, vn_ref, kx_ref, vx_ref,
         sg_ref, x_ref, m_ref, w_ref, o_ref) = rest
        last = slice((n_blk - 1) * BLOCK, n_blk * BLOCK)
        k_blocks = ([kp_ref[0, last]] + [kc_ref[0, i * BLOCK:(i + 1) * BLOCK] for i in range(n_blk)]
                    + [kn_ref[0, 0:BLOCK]])
        v_blocks = ([vp_ref[0, :, last]] + [vc_ref[0, :, i * BLOCK:(i + 1) * BLOCK] for i in range(n_blk)]
                    + [vn_ref[0, :, 0:BLOCK]])
    else:
        kx_ref, vx_ref, sg_ref, x_ref, m_ref, w_ref, o_ref = rest
    step = pl.program_id(1)
    nq = A_GROUP * BLOCK
    chunk_of = lax.broadcasted_iota(jnp.int32, (1, nq), 1) // BLOCK
    sinks = []
    for hh in range(2):
        sk = jnp.full((1, nq), sink_ref[hh] * LOG2_E, F32)
        for c in range(1, A_GROUP):
            sk = jnp.where(chunk_of == c, sink_ref[2 * c + hh] * LOG2_E, sk)
        sinks.append(sk)
    if local:
        kj = lax.broadcasted_iota(jnp.int32, (BLOCK, nq), 0)
        qi = lax.broadcasted_iota(jnp.int32, (BLOCK, nq), 1) % BLOCK

    rows_out = []
    for blk in range(n_blk):
        qts = jnp.concatenate([qt_ref[0, c * LANES:(c + 1) * LANES, blk * BLOCK:(blk + 1) * BLOCK]
                               for c in range(A_GROUP)], axis=1)
        if local:
            kwin = jnp.concatenate(k_blocks[blk:blk + 3] + [kx_ref[0]], axis=0)
            vtw = jnp.concatenate(v_blocks[blk:blk + 3] + [vx_ref[0]], axis=1)
            ok_prev = (kj >= qi) & (step > 0) if blk == 0 else (kj >= qi)
            ok_next = (kj <= qi) & (step < n_steps - 1) if blk == n_blk - 1 else (kj <= qi)
        else:
            kwin, vtw, ok_prev, ok_next = kx_ref[0], vx_ref[0], None, None
        ot = _attn_block(kwin, vtw, qts, sinks, ok_prev, ok_next)
        rows = slice(blk * BLOCK, (blk + 1) * BLOCK)
        outs = []
        for c in range(A_GROUP):
            sg = sg_ref[0, rows, c * LANES:(c + 1) * LANES].astype(F32)
            outs.append((ot[:, c * BLOCK:(c + 1) * BLOCK].T * sg).astype(BF16))
        rows_out.append(jnp.concatenate(outs, axis=1))
    y = _dot(jnp.concatenate(rows_out, axis=0), w_ref[...])
    gate = m_ref[0][:, 2 * D:3 * D]
    o_ref[0] = x_ref[0] + gate * y


def _attention(sink, qt, k, vt, kx, vxt, sg, x, mod3, mod_row, w_out, local):
    b, s, _ = sg.shape
    rows = min(A_BLOCKS_PER_STEP * BLOCK, s)
    assert s % rows == 0
    ns = s // rows
    n_ctx = kx.shape[1]
    blk = lambda w: pl.BlockSpec((1, rows, w), lambda bi, i: (bi, i, 0))
    blk_t = lambda w: pl.BlockSpec((1, w, rows), lambda bi, i: (bi, 0, i))
    if mod_row is None:
        mod_map = lambda bi, i: (bi, 0, 0)
    else:
        mod_map = lambda bi, i: (mod_row, 0, 0)
    in_specs = [pl.BlockSpec(memory_space=pltpu.SMEM), blk_t(A_WIDTH)]
    args = [sink, qt]
    if local:
        lo = lambda i: jnp.maximum(i - 1, 0)
        hi = lambda i: jnp.minimum(i + 1, ns - 1)
        in_specs += [pl.BlockSpec((1, rows, A_KV_WIDTH), lambda bi, i: (bi, lo(i), 0)),
                     blk(A_KV_WIDTH),
                     pl.BlockSpec((1, rows, A_KV_WIDTH), lambda bi, i: (bi, hi(i), 0)),
                     pl.BlockSpec((1, A_KV_WIDTH, rows), lambda bi, i: (bi, 0, lo(i))),
                     blk_t(A_KV_WIDTH),
                     pl.BlockSpec((1, A_KV_WIDTH, rows), lambda bi, i: (bi, 0, hi(i)))]
        args += [k, k, k, vt, vt, vt]
    in_specs += [pl.BlockSpec((1, n_ctx, A_KV_WIDTH), lambda bi, i: (bi, 0, 0)),
                 pl.BlockSpec((1, A_KV_WIDTH, n_ctx), lambda bi, i: (bi, 0, 0)),
                 blk(A_WIDTH), blk(D),
                 pl.BlockSpec((1, 1, 3 * D), mod_map),
                 pl.BlockSpec((A_WIDTH, D), lambda bi, i: (0, 0))]
    args += [kx, vxt, sg, x, mod3, w_out]
    return pl.pallas_call(
        functools.partial(_attn_kernel, local, ns),
        grid=(b, ns),
        in_specs=in_specs,
        out_specs=blk(D),
        out_shape=jax.ShapeDtypeStruct((b, s, D), F32),
        compiler_params=_params("parallel", "parallel"),
        name="attn_local" if local else "attn_ctx",
    )(*args)


def _chunk_cumsum(x, tri):
    hi = x.astype(BF16)
    lo = (x - hi.astype(F32)).astype(BF16)
    return _dot(tri, hi) + _dot(tri, lo)


def _inproj_gla_kernel(chunk, with_q, x_ref, m_ref, g_ref, w_ref, wa1_ref, wa2_ref, ba_ref, tri_ref, *outs):
    if with_q:
        per_dir = (outs[0:4], outs[4:8])
        v_ref, sg_ref = outs[8:]
    else:
        per_dir = ((outs[0],), (outs[1],))
        v_ref = outs[2]
    tm = x_ref.shape[1]
    hb = _norm_mod(x_ref[0], g_ref[...], m_ref[0]).astype(BF16)
    r = _dot(hb, wa1_ref[...])
    z = _dot(r.astype(BF16), wa2_ref[...]) + ba_ref[...]
    e = jnp.exp2(jnp.abs(z) * -LOG2_E)
    la = (jnp.minimum(z, 0.0) * LOG2_E - jnp.log2(1.0 + e)) * (1.0 / GATE_TEMP)
    k = _dot(hb, w_ref[:, B_K_WIDTH:2 * B_K_WIDTH])
    if with_q:
        q = _dot(hb, w_ref[:, 0:B_K_WIDTH]) * K_SCALE
    n_chunks = tm // chunk
    for reverse in (False, True):
        lad = la[:, B_K_WIDTH:] if reverse else la[:, 0:B_K_WIDTH]
        refs = per_dir[1] if reverse else per_dir[0]
        tots, kis, kes = [], [], []
        tri = tri_ref[1 if reverse else 0]
        tb = tri.shape[0]
        cum_all = jnp.concatenate([_chunk_cumsum(lad[r * tb:(r + 1) * tb], tri) for r in range(tm // tb)], axis=0)
        for c in range(n_chunks):
            rows = slice(c * chunk, (c + 1) * chunk)
            cum = cum_all[rows]
            tot = cum[0:1] if reverse else cum[chunk - 1:chunk]
            ke = k[rows] * jnp.exp2(tot - cum)
            if not with_q:
                refs[0][0, rows] = ke.astype(BF16)
                continue
            refs[0][0, rows] = (q[rows] * jnp.exp2(cum)).astype(BF16)
            kis.append(k[rows] * jnp.exp2(-cum))
            kes.append(ke)
            tots.append(tot)
            if c % 2 == 1:
                pair = slice((c - 1) * chunk, (c + 1) * chunk)
                refs[1][0, :, pair] = jnp.concatenate(kis[-2:], axis=0).T.astype(BF16)
                refs[2][0, :, pair] = jnp.concatenate(kes[-2:], axis=0).T.astype(BF16)
        if with_q:
            pad = [jnp.zeros((DEC_ROWS - n_chunks, B_K_WIDTH), F32)] if n_chunks < DEC_ROWS else []
            refs[3][0, 0] = jnp.exp2(jnp.concatenate(tots + pad, axis=0))
    if with_q:
        sg_ref[0] = _silu(_dot(hb, w_ref[:, 2 * B_K_WIDTH + B_V_WIDTH:])).astype(BF16)
    v_ref[0] = _dot(hb, w_ref[:, 2 * B_K_WIDTH:2 * B_K_WIDTH + B_V_WIDTH]).astype(BF16)


def _inproj_gla(x, mod3, mod_row, norm_g, w, wa1, wa2, ba, chunk, with_q):
    b, s, _ = x.shape
    tm = min(ROW_TILE, s)
    nt = s // tm
    cpt = tm // chunk
    tb = max(chunk, MXU_DIM)
    assert tm % tb == 0 and tb % chunk == 0
    if mod_row is None:
        mod_map = lambda bi, i: (bi, 0, 0)
    else:
        mod_map = lambda bi, i: (mod_row, 0, 0)
    const = lambda shape: pl.BlockSpec(shape, lambda bi, i: (0,) * len(shape))
    in_specs = [
        pl.BlockSpec((1, tm, D), lambda bi, i: (bi, i, 0)),
        pl.BlockSpec((1, 1, 3 * D), mod_map),
        const((1, D)), const(w.shape), const(wa1.shape), const(wa2.shape), const(ba.shape),
        const((2, tb, tb)),
    ]
    t_idx = np.arange(tb)
    same = (t_idx[:, None] // chunk) == (t_idx[None, :] // chunk)
    lower = same & (t_idx[None, :] <= t_idx[:, None])
    tri = jnp.asarray(np.stack([lower, lower.T]), dtype=BF16)
    rows = lambda w_: (jax.ShapeDtypeStruct((b, s, w_), BF16),
                       pl.BlockSpec((1, tm, w_), lambda bi, i: (bi, i, 0)))
    cols = (jax.ShapeDtypeStruct((b, B_K_WIDTH, s), BF16),
            pl.BlockSpec((1, B_K_WIDTH, tm), lambda bi, i: (bi, 0, i)))
    decs = (jax.ShapeDtypeStruct((b, nt, DEC_ROWS, B_K_WIDTH), F32),
            pl.BlockSpec((1, 1, DEC_ROWS, B_K_WIDTH), lambda bi, i: (bi, i, 0, 0)))
    if with_q:
        assert cpt <= DEC_ROWS and cpt % 2 == 0
        outs = [rows(B_K_WIDTH), cols, cols, decs] * 2 + [rows(B_V_WIDTH)] * 2
    else:
        outs = [rows(B_K_WIDTH)] * 2 + [rows(B_V_WIDTH)]
    return pl.pallas_call(
        functools.partial(_inproj_gla_kernel, chunk, with_q),
        grid=(b, nt),
        in_specs=in_specs,
        out_specs=tuple(o[1] for o in outs),
        out_shape=tuple(o[0] for o in outs),
        compiler_params=_params("parallel", "parallel"),
        name="inproj_gla" if with_q else "inproj_gla_ctx",
    )(x, mod3, norm_g.reshape(1, D), w, wa1, wa2, ba, tri)


def _gla_state_kernel(kf_ref, kb_ref, v_ref, sf_ref, sb_ref):
    for h in range(B_HEADS):
        vh = v_ref[0, :, h * B_VAL_DIM:(h + 1) * B_VAL_DIM]
        ks = slice(h * B_KEY_DIM, (h + 1) * B_KEY_DIM)
        sf_ref[0, h] = _dot_tn(kf_ref[0, :, ks], vh)
        sb_ref[0, h] = _dot_tn(kb_ref[0, :, ks], vh)


def _gla_state(ke_f, ke_b, v):
    b, n, _ = v.shape
    st = jax.ShapeDtypeStruct((b, B_HEADS, B_KEY_DIM, B_VAL_DIM), F32)
    st_spec = pl.BlockSpec((1, B_HEADS, B_KEY_DIM, B_VAL_DIM), lambda bi: (bi, 0, 0, 0))
    return pl.pallas_call(
        _gla_state_kernel,
        grid=(b,),
        in_specs=[pl.BlockSpec((1, n, B_K_WIDTH), lambda bi: (bi, 0, 0))] * 2
        + [pl.BlockSpec((1, n, B_V_WIDTH), lambda bi: (bi, 0, 0))],
        out_specs=(st_spec, st_spec),
        out_shape=(st, st),
        compiler_params=_params("parallel"),
        name="gla_ctx_state",
    )(ke_f, ke_b, v)


def _scan_tile(reverse, qd_ref, kit_ref, ket_ref, dec_ref, v_ref, st_ref):
    pair_rows = 2 * CHUNK
    ti = lax.broadcasted_iota(jnp.int32, (pair_rows, pair_rows), 0)
    si = lax.broadcasted_iota(jnp.int32, (pair_rows, pair_rows), 1)
    same = (ti // CHUNK) == (si // CHUNK)
    if reverse:
        use_inv = same & (ti <= si)
        use_end = (ti < CHUNK) & (si >= CHUNK)
        second_row = lax.broadcasted_iota(jnp.int32, (pair_rows, B_KEY_DIM), 0) < CHUNK
        first_col = lax.broadcasted_iota(jnp.int32, (B_KEY_DIM, pair_rows), 1) >= CHUNK
    else:
        use_inv = same & (ti >= si)
        use_end = (ti >= CHUNK) & (si < CHUNK)
        second_row = lax.broadcasted_iota(jnp.int32, (pair_rows, B_KEY_DIM), 0) >= CHUNK
        first_col = lax.broadcasted_iota(jnp.int32, (B_KEY_DIM, pair_rows), 1) < CHUNK
    n_pairs = SCAN_TILE // pair_rows
    order = range(n_pairs - 1, -1, -1) if reverse else range(n_pairs)
    n_dec = dec_ref.shape[1] * DEC_ROWS
    dec = dec_ref[0].reshape(n_dec, B_K_WIDTH)
    dect = jnp.concatenate([dec, jnp.zeros((LANES - n_dec, B_K_WIDTH), F32)], axis=0).T
    heads = []
    for h in range(B_HEADS):
        ks = slice(h * B_KEY_DIM, (h + 1) * B_KEY_DIM)
        vs = slice(h * B_VAL_DIM, (h + 1) * B_VAL_DIM)
        st = st_ref[h]
        o_rows = [None] * n_pairs
        for p in order:
            rows = slice(p * pair_rows, (p + 1) * pair_rows)
            c_first, c_second = (2 * p + 1, 2 * p) if reverse else (2 * p, 2 * p + 1)
            qd = qd_ref[0, rows, ks]
            ket = ket_ref[0, ks, rows]
            vh = v_ref[0, rows, vs]
            s2 = _dot(qd, jnp.concatenate([kit_ref[0, ks, rows], ket], axis=1))
            a = jnp.where(use_inv, s2[:, 0:pair_rows], jnp.where(use_end, s2[:, pair_rows:], 0.0)).astype(BF16)
            q_pair = jnp.where(second_row, qd.astype(F32) * dec[c_first:c_first + 1, ks], qd.astype(F32))
            k_pair = jnp.where(first_col, ket.astype(F32) * dect[ks, c_second:c_second + 1], ket.astype(F32))
            o_rows[p] = _dot(a, vh) + _dot(q_pair.astype(BF16), st.astype(BF16))
            st = (st * (dect[ks, c_first:c_first + 1] * dect[ks, c_second:c_second + 1])
                  + _dot(k_pair.astype(BF16), vh))
        st_ref[h] = st
        heads.append(jnp.concatenate(o_rows, axis=0))
    return jnp.concatenate(heads, axis=1)


def _gla_scan_kernel(nt, qdf_ref, kitf_ref, ketf_ref, decf_ref, qdb_ref, kitb_ref, ketb_ref, decb_ref,
                     v_ref, sf_ref, sb_ref, sg_ref, x_ref, m_ref, hn_ref, w_ref, fn_ref,
                     o_ref, st_ref, of_ref):
    j = pl.program_id(1)

    @pl.when(j == 0)
    def _():
        st_ref[...] = sf_ref[0]

    @pl.when(j == nt)
    def _():
        st_ref[...] = sb_ref[0]

    @pl.when(j < nt)
    def _():
        of_ref[j] = _scan_tile(False, qdf_ref, kitf_ref, ketf_ref, decf_ref, v_ref, st_ref)

    @pl.when(j >= nt)
    def _():
        o = _scan_tile(True, qdb_ref, kitb_ref, ketb_ref, decb_ref, v_ref, st_ref) + of_ref[2 * nt - 1 - j]
        normed = []
        for h in range(B_HEADS):
            oh = o[:, h * B_VAL_DIM:(h + 1) * B_VAL_DIM]
            normed.append(oh * lax.rsqrt(jnp.mean(oh * oh, axis=-1, keepdims=True) + EPS))
        of = jnp.concatenate(normed, axis=1) * hn_ref[...]
        y = _dot((of * sg_ref[0].astype(F32)).astype(BF16), w_ref[...])
        xn = x_ref[0] + m_ref[0][:, 2 * D:3 * D] * y
        ms = jnp.mean(xn * xn, axis=-1, keepdims=True)
        o_ref[0] = xn * lax.rsqrt(ms + EPS) * fn_ref[...]


def _gla_scan(fwd, bwd, v, s_f, s_b, sg, x, mod3, head_g, w_out, final_g):
    b, s, _ = v.shape
    assert SCAN_TILE % ROW_TILE == 0 and ROW_TILE // CHUNK == DEC_ROWS
    dec_blocks = SCAN_TILE // ROW_TILE
    nt = s // SCAN_TILE
    t_fwd = lambda j: jnp.minimum(j, nt - 1)
    t_bwd = lambda j: 2 * nt - 1 - jnp.maximum(j, nt)
    t_both = lambda j: jnp.where(j < nt, j, 2 * nt - 1 - j)

    def direction(t):
        rows = pl.BlockSpec((1, SCAN_TILE, B_K_WIDTH), lambda bi, j: (bi, t(j), 0))
        cols = pl.BlockSpec((1, B_K_WIDTH, SCAN_TILE), lambda bi, j: (bi, 0, t(j)))
        decs = pl.BlockSpec((1, dec_blocks, DEC_ROWS, B_K_WIDTH), lambda bi, j: (bi, t(j), 0, 0))
        return [rows, cols, cols, decs]

    const = lambda shape: pl.BlockSpec(shape, lambda bi, j: (0,) * len(shape))
    state = pl.BlockSpec((1, B_HEADS, B_KEY_DIM, B_VAL_DIM), lambda bi, j: (bi, 0, 0, 0))
    out_rows = lambda w_: pl.BlockSpec((1, SCAN_TILE, w_), lambda bi, j: (bi, t_bwd(j), 0))
    in_specs = direction(t_fwd) + direction(t_bwd) + [
        pl.BlockSpec((1, SCAN_TILE, B_V_WIDTH), lambda bi, j: (bi, t_both(j), 0)),
        state, state, out_rows(B_V_WIDTH), out_rows(D),
        pl.BlockSpec((1, 1, 3 * D), lambda bi, j: (bi, 0, 0)),
        const((1, B_V_WIDTH)), const((B_V_WIDTH, D)), const((1, D))]
    return pl.pallas_call(
        functools.partial(_gla_scan_kernel, nt),
        grid=(b, 2 * nt),
        in_specs=in_specs,
        out_specs=out_rows(D),
        out_shape=jax.ShapeDtypeStruct((b, s, D), F32),
        scratch_shapes=[pltpu.VMEM((B_HEADS, B_KEY_DIM, B_VAL_DIM), F32),
                        pltpu.VMEM((nt, SCAN_TILE, B_V_WIDTH), F32)],
        compiler_params=_params("parallel", "arbitrary", vmem_limit=SCAN_VMEM_LIMIT),
        name="gla_scan",
    )(*fwd, *bwd, v, s_f, s_b, sg, x, mod3, head_g.reshape(1, B_V_WIDTH), w_out, final_g.reshape(1, D))


def _rope_tables(n_tokens):
    rows_n = n_tokens // GRID_W
    row = np.repeat(np.arange(rows_n, dtype=np.float32), GRID_W)
    col = np.tile(np.arange(GRID_W, dtype=np.float32), rows_n)
    inv_freq = (np.float32(ROPE_BASE) ** (-np.arange(ROPE_FREQS, dtype=np.float32) / np.float32(ROPE_FREQS)))
    inv_freq = inv_freq.astype(np.float32)
    ang = np.stack([row[:, None] * inv_freq, col[:, None] * inv_freq], axis=1)
    cos, sin = np.cos(ang).astype(np.float32), np.sin(ang).astype(np.float32)
    zero = np.zeros_like(sin)
    tile = lambda t: np.tile(t.reshape(n_tokens, A_HEAD_DIM), (1, LANES // A_HEAD_DIM))
    return (tile(np.stack([cos, cos], axis=2)),
            tile(np.stack([-sin, zero], axis=2)),
            tile(np.stack([zero, sin], axis=2)))


def _pair_heads(t, axis):
    shape = t.shape
    t = t.reshape(shape[:axis] + (A_KV_HEADS, A_GROUP, -1) + shape[axis + 1:])
    return jnp.swapaxes(t, axis, axis + 1).reshape(shape)


def kernel(x, c, ctx, c_ctx, l0_norm_g, l0_w_ada, l0_b_ada, l0_w_in, l0_sink, l0_w_out, l1_norm_g, l1_w_ada, l1_b_ada, l1_w_in, l1_wa1_f, l1_wa2_f, l1_ba_f, l1_wa1_b, l1_wa2_b, l1_ba_b, l1_head_norm_g, l1_w_out, final_norm_g):
    b, s, _ = x.shape
    ctx_row = b

    cvec = jnp.concatenate([c, c_ctx[None, :], jnp.zeros((MOD_ROWS - b - 1, D), F32)], axis=0)
    mod0 = _modulation(cvec, l0_w_ada, l0_b_ada).reshape(MOD_ROWS, 1, 3 * D)
    mod1 = _modulation(cvec, l1_w_ada, l1_b_ada).reshape(MOD_ROWS, 1, 3 * D)

    q_end, kv_end = A_WIDTH, A_WIDTH + 2 * A_KV_WIDTH
    w0 = (_pair_heads(l0_w_in[:, :q_end].astype(BF16), 1), l0_w_in[:, q_end:kv_end].astype(BF16),
          _pair_heads(l0_w_in[:, kv_end:].astype(BF16), 1))
    w0_out = _pair_heads(l0_w_out.astype(BF16), 0)
    sink = _pair_heads(l0_sink.astype(F32), 0)
    qt, k, vt, sg = _inproj_attn(x, mod0, None, l0_norm_g, w0, _rope_tables(s))
    qct, kc, vct, sgc = _inproj_attn(ctx, mod0, ctx_row, l0_norm_g, w0, None)
    x1 = _attention(sink, qt, k, vt, kc, vct, sg, x, mod0, None, w0_out, True)
    xc1 = _attention(sink, qct, None, None, kc, vct, sgc, ctx, mod0, ctx_row, w0_out, False)

    w1 = l1_w_in.astype(BF16)
    wa1 = jnp.concatenate([l1_wa1_f, l1_wa1_b, jnp.zeros((D, LANES - 2 * GATE_RANK), F32)], axis=1).astype(BF16)
    wa2 = jnp.zeros((LANES, 2 * B_K_WIDTH), F32)
    wa2 = wa2.at[0:GATE_RANK, 0:B_K_WIDTH].set(l1_wa2_f)
    wa2 = wa2.at[GATE_RANK:2 * GATE_RANK, B_K_WIDTH:].set(l1_wa2_b).astype(BF16)
    ba = jnp.concatenate([l1_ba_f, l1_ba_b]).reshape(1, 2 * B_K_WIDTH)
    kec_f, kec_b, vcx = _inproj_gla(xc1, mod1, ctx_row, l1_norm_g, w1, wa1, wa2, ba, ctx.shape[1], False)
    s_f, s_b = _gla_state(kec_f, kec_b, vcx)
    outs = _inproj_gla(x1, mod1, None, l1_norm_g, w1, wa1, wa2, ba, CHUNK, True)
    v1, sg1 = outs[8:]
    return _gla_scan(outs[0:4], outs[4:8], v1, s_f, s_b, sg1, x1, mod1,
                     l1_head_norm_g, l1_w_out.astype(BF16), final_norm_g)
```

```python
import functools

import jax
import jax.numpy as jnp
import numpy as np
from jax import lax
from jax.experimental import pallas as pl
from jax.experimental.pallas import tpu as pltpu

F32 = jnp.float32
BF16 = jnp.bfloat16

D = 1024
GRID_W = 64
EPS = 1e-6
NEG_INF = -1e30

A_HEADS = 16
A_KV_HEADS = 2
A_GROUP = A_HEADS // A_KV_HEADS
A_HEAD_DIM = 64
A_WIDTH = A_HEADS * A_HEAD_DIM
A_KV_WIDTH = A_KV_HEADS * A_HEAD_DIM
BLOCK = 128
ROPE_BASE = 10000.0
ROPE_FREQS = A_HEAD_DIM // 4
Q_SCALE = A_HEAD_DIM ** -0.5
LOG2_E = 1.4426950408889634

B_HEADS = 4
B_K_WIDTH = D // 2
B_V_WIDTH = D
B_KEY_DIM = B_K_WIDTH // B_HEADS
B_VAL_DIM = B_V_WIDTH // B_HEADS
GATE_RANK = 16
GATE_TEMP = 16.0
CHUNK = 64
K_SCALE = B_KEY_DIM ** -0.5

LANES = 128
MXU_DIM = 256
MOD_ROWS = 16
ROW_TILE = 512
SCAN_TILE = 1024
SUB_ROWS = 256
DEC_ROWS = 8
SUM_ROWS = 16
A_BLOCKS_PER_STEP = 4
VMEM_LIMIT = 48 * 1024 * 1024
SCAN_VMEM_LIMIT = 56 * 1024 * 1024


def _params(*sem, vmem_limit=VMEM_LIMIT):
    return pltpu.CompilerParams(dimension_semantics=sem, vmem_limit_bytes=vmem_limit)


def _silu(x):
    return x / (1.0 + jnp.exp(-x))


def _dot(a, b):
    return jnp.dot(a, b, preferred_element_type=F32)


def _dot_nt(a, b):
    return lax.dot_general(a, b, (((1,), (1,)), ((), ())), preferred_element_type=F32)


def _dot_tn(a, b):
    return lax.dot_general(a, b, (((0,), (0,)), ((), ())), preferred_element_type=F32)


def _norm_mod(x, g, m):
    ms = jnp.mean(x * x, axis=-1, keepdims=True)
    y = x * lax.rsqrt(ms + EPS) * g
    return y * (1.0 + m[:, D:2 * D]) + m[:, 0:D]


def _cast_once(src_ref, dst_ref):
    @pl.when((pl.program_id(0) == 0) & (pl.program_id(1) == 0))
    def _():
        def body(i, carry):
            rows = pl.ds(pl.multiple_of(i * LANES, LANES), LANES)
            dst_ref[rows, :] = src_ref[rows, :].astype(dst_ref.dtype)
            return carry
        lax.fori_loop(0, src_ref.shape[0] // LANES, body, 0)


def _mod_kernel(c_ref, w_ref, b_ref, o_ref):
    s = _silu(c_ref[...])
    o_ref[...] = _dot(s.astype(BF16), w_ref[...].astype(BF16)) + b_ref[...]


def _modulation(cvec, w_ada, b_ada):
    n = w_ada.shape[1] // D
    return pl.pallas_call(
        _mod_kernel,
        grid=(n,),
        in_specs=[
            pl.BlockSpec((MOD_ROWS, D), lambda j: (0, 0)),
            pl.BlockSpec((D, D), lambda j: (0, j)),
            pl.BlockSpec((1, D), lambda j: (0, j)),
        ],
        out_specs=pl.BlockSpec((MOD_ROWS, D), lambda j: (0, j)),
        out_shape=jax.ShapeDtypeStruct((MOD_ROWS, n * D), F32),
        compiler_params=_params("arbitrary"),
        name="modulation",
    )(cvec, w_ada, b_ada.reshape(1, n * D))


def _rope(t, cos, sin_hi, sin_lo):
    return t * cos + pltpu.roll(t, LANES - ROPE_FREQS, 1) * sin_hi + pltpu.roll(t, ROPE_FREQS, 1) * sin_lo


def _pair_cast_columns(src_ref, dst_ref):
    @pl.when((pl.program_id(0) == 0) & (pl.program_id(1) == 0))
    def _():
        kv_lo, kv_hi = A_WIDTH, A_WIDTH + 2 * A_KV_WIDTH

        def body(i, carry):
            rows = pl.ds(pl.multiple_of(i * LANES, LANES), LANES)
            src = src_ref[rows, :]
            dst_ref[rows, kv_lo:kv_hi] = src[:, kv_lo:kv_hi].astype(BF16)
            for base in (0, kv_hi):
                for p in range(A_GROUP):
                    a = base + p * A_HEAD_DIM
                    b = base + (p + A_GROUP) * A_HEAD_DIM
                    pair = jnp.concatenate([src[:, a:a + A_HEAD_DIM], src[:, b:b + A_HEAD_DIM]], axis=1)
                    dst_ref[rows, base + p * LANES:base + (p + 1) * LANES] = pair.astype(BF16)
            return carry
        lax.fori_loop(0, src_ref.shape[0] // LANES, body, 0)


def _inproj_attn_kernel(rope, x_ref, m_ref, g_ref, w32_ref, *rest):
    *rest, w_ref = rest
    _pair_cast_columns(w32_ref, w_ref)
    wq_ref = w_ref.at[:, 0:A_WIDTH]
    wkv_ref = w_ref.at[:, A_WIDTH:A_WIDTH + 2 * A_KV_WIDTH]
    wg_ref = w_ref.at[:, A_WIDTH + 2 * A_KV_WIDTH:]
    if rope:
        cos_ref, shi_ref, slo_ref, qt_ref, k_ref, vt_ref, sg_ref = rest
    else:
        qt_ref, k_ref, vt_ref, sg_ref = rest
    for r in range(x_ref.shape[1] // SUB_ROWS):
        rows = slice(r * SUB_ROWS, (r + 1) * SUB_ROWS)
        if rope:
            cos, shi, slo = cos_ref[rows], shi_ref[rows], slo_ref[rows]
        hb = _norm_mod(x_ref[0, rows], g_ref[...], m_ref[0]).astype(BF16)
        q = _dot(hb, wq_ref[...])
        for j in range(A_WIDTH // LANES):
            qj = q[:, j * LANES:(j + 1) * LANES]
            if rope:
                qj = _rope(qj, cos, shi, slo)
            qt_ref[0, j * LANES:(j + 1) * LANES, rows] = (qj * (Q_SCALE * LOG2_E)).T.astype(BF16)
        kv = _dot(hb, wkv_ref[...])
        k = kv[:, 0:A_KV_WIDTH]
        if rope:
            k = _rope(k, cos, shi, slo)
        k_ref[0, rows] = k.astype(BF16)
        vt_ref[0, :, rows] = kv[:, A_KV_WIDTH:].T.astype(BF16)
        g = _dot(hb, wg_ref[...])
        sg_ref[0, rows] = _silu(g).astype(BF16)


def _inproj_attn(x, mod3, mod_row, norm_g, w_in, tables):
    b, s, _ = x.shape
    tm = min(ROW_TILE, s)
    rope = tables is not None
    if mod_row is None:
        mod_map = lambda bi, i: (bi, 0, 0)
    else:
        mod_map = lambda bi, i: (mod_row, 0, 0)
    in_specs = [
        pl.BlockSpec((1, tm, D), lambda bi, i: (bi, i, 0)),
        pl.BlockSpec((1, 1, 3 * D), mod_map),
        pl.BlockSpec((1, D), lambda bi, i: (0, 0)),
        pl.BlockSpec(w_in.shape, lambda bi, i: (0, 0), pipeline_mode=pl.Buffered(1)),
    ]
    args = [x, mod3, norm_g.reshape(1, D), w_in]
    if rope:
        in_specs += [pl.BlockSpec((tm, LANES), lambda bi, i: (i, 0))] * 3
        args += list(tables)
    out_shape = (
        jax.ShapeDtypeStruct((b, A_WIDTH, s), BF16),
        jax.ShapeDtypeStruct((b, s, A_KV_WIDTH), BF16),
        jax.ShapeDtypeStruct((b, A_KV_WIDTH, s), BF16),
        jax.ShapeDtypeStruct((b, s, A_WIDTH), BF16),
    )
    out_specs = (
        pl.BlockSpec((1, A_WIDTH, tm), lambda bi, i: (bi, 0, i)),
        pl.BlockSpec((1, tm, A_KV_WIDTH), lambda bi, i: (bi, i, 0)),
        pl.BlockSpec((1, A_KV_WIDTH, tm), lambda bi, i: (bi, 0, i)),
        pl.BlockSpec((1, tm, A_WIDTH), lambda bi, i: (bi, i, 0)),
    )
    return pl.pallas_call(
        functools.partial(_inproj_attn_kernel, rope),
        grid=(b, s // tm),
        in_specs=in_specs,
        out_specs=out_specs,
        out_shape=out_shape,
        scratch_shapes=[pltpu.VMEM(w_in.shape, BF16)],
        compiler_params=_params("arbitrary", "arbitrary"),
        name="inproj_attn_rope" if rope else "inproj_attn_ctx",
    )(*args)


def _attn_block(kwin, vtw, qts, sinks, ok_prev, ok_next):
    n_keys, nq = kwin.shape[0], qts.shape[1]
    lane = lax.broadcasted_iota(jnp.int32, kwin.shape, 1)
    kbd = jnp.concatenate([jnp.where(lane < A_HEAD_DIM, kwin, jnp.zeros_like(kwin)),
                           jnp.where(lane >= A_HEAD_DIM, kwin, jnp.zeros_like(kwin))], axis=0)
    vrow = lax.broadcasted_iota(jnp.int32, vtw.shape, 0)
    vbd = jnp.concatenate([jnp.where(vrow < A_HEAD_DIM, vtw, jnp.zeros_like(vtw)),
                           jnp.where(vrow >= A_HEAD_DIM, vtw, jnp.zeros_like(vtw))], axis=1)
    srow = lax.broadcasted_iota(jnp.int32, (SUM_ROWS, 2 * n_keys), 0)
    scol = lax.broadcasted_iota(jnp.int32, (SUM_ROWS, 2 * n_keys), 1) // n_keys
    vbd = jnp.concatenate([vbd, jnp.where(srow == scol, 1.0, 0.0).astype(BF16)], axis=0)
    head_a = lax.broadcasted_iota(jnp.int32, (LANES, nq), 0) < A_HEAD_DIM

    st = _dot(kbd, qts)
    probs, esink = [], []
    for hh in range(2):
        sh = st[hh * n_keys:(hh + 1) * n_keys]
        if ok_prev is not None:
            parts = [jnp.where(ok_prev, sh[0:BLOCK], NEG_INF),
                     sh[BLOCK:2 * BLOCK],
                     jnp.where(ok_next, sh[2 * BLOCK:3 * BLOCK], NEG_INF),
                     sh[3 * BLOCK:]]
        else:
            parts = [sh]
        mx = functools.reduce(jnp.maximum, [jnp.max(t, axis=0, keepdims=True) for t in parts])
        mx = jnp.maximum(mx, sinks[hh])
        probs += [jnp.exp2(t - mx).astype(BF16) for t in parts]
        esink.append(jnp.exp2(sinks[hh] - mx))
    ot = _dot(vbd, jnp.concatenate(probs, axis=0))
    inv = [1.0 / (ot[LANES + hh:LANES + hh + 1] + esink[hh]) for hh in range(2)]
    return ot[0:LANES] * jnp.where(head_a, inv[0], inv[1])


def _pair_cast_rows(src_ref, dst_ref):
    @pl.when((pl.program_id(0) == 0) & (pl.program_id(1) == 0))
    def _():
        for p in range(A_GROUP):
            for half, h in enumerate((p, p + A_GROUP)):
                lo = p * LANES + half * A_HEAD_DIM
                dst_ref[lo:lo + A_HEAD_DIM, :] = src_ref[h * A_HEAD_DIM:(h + 1) * A_HEAD_DIM, :].astype(BF16)


def _attn_kernel(local, n_steps, sink_ref, qt_ref, *rest):
    *rest, w_ref = rest
    n_blk = rest[-1].shape[1] // BLOCK
    if local:
        (kp_ref, kc_ref, kn_ref, vp_ref, vc_ref, vn_ref, kx_ref, vx_ref,
         sg_ref, x_ref, m_ref, w32_ref, o_ref) = rest
        last = slice((n_blk - 1) * BLOCK, n_blk * BLOCK)
        k_blocks = ([kp_ref[0, last]] + [kc_ref[0, i * BLOCK:(i + 1) * BLOCK] for i in range(n_blk)]
                    + [kn_ref[0, 0:BLOCK]])
        v_blocks = ([vp_ref[0, :, last]] + [vc_ref[0, :, i * BLOCK:(i + 1) * BLOCK] for i in range(n_blk)]
                    + [vn_ref[0, :, 0:BLOCK]])
    else:
        kx_ref, vx_ref, sg_ref, x_ref, m_ref, w32_ref, o_ref = rest
    _pair_cast_rows(w32_ref, w_ref)
    step = pl.program_id(1)
    nq = A_GROUP * BLOCK
    chunk_of = lax.broadcasted_iota(jnp.int32, (1, nq), 1) // BLOCK
    sinks = []
    for hh in range(2):
        sk = jnp.full((1, nq), sink_ref[hh] * LOG2_E, F32)
        for c in range(1, A_GROUP):
            sk = jnp.where(chunk_of == c, sink_ref[2 * c + hh] * LOG2_E, sk)
        sinks.append(sk)
    if local:
        kj = lax.broadcasted_iota(jnp.int32, (BLOCK, nq), 0)
        qi = lax.broadcasted_iota(jnp.int32, (BLOCK, nq), 1) % BLOCK

    rows_out = []
    for blk in range(n_blk):
        qts = jnp.concatenate([qt_ref[0, c * LANES:(c + 1) * LANES, blk * BLOCK:(blk + 1) * BLOCK]
                               for c in range(A_GROUP)], axis=1)
        if local:
            kwin = jnp.concatenate(k_blocks[blk:blk + 3] + [kx_ref[0]], axis=0)
            vtw = jnp.concatenate(v_blocks[blk:blk + 3] + [vx_ref[0]], axis=1)
            ok_prev = (kj >= qi) & (step > 0) if blk == 0 else (kj >= qi)
            ok_next = (kj <= qi) & (step < n_steps - 1) if blk == n_blk - 1 else (kj <= qi)
        else:
            kwin, vtw, ok_prev, ok_next = kx_ref[0], vx_ref[0], None, None
        ot = _attn_block(kwin, vtw, qts, sinks, ok_prev, ok_next)
        rows = slice(blk * BLOCK, (blk + 1) * BLOCK)
        outs = []
        for c in range(A_GROUP):
            sg = sg_ref[0, rows, c * LANES:(c + 1) * LANES].astype(F32)
            outs.append((ot[:, c * BLOCK:(c + 1) * BLOCK].T * sg).astype(BF16))
        rows_out.append(jnp.concatenate(outs, axis=1))
    y = _dot(jnp.concatenate(rows_out, axis=0), w_ref[...])
    gate = m_ref[0][:, 2 * D:3 * D]
    o_ref[0] = x_ref[0] + gate * y


def _attention(sink, qt, k, vt, kx, vxt, sg, x, mod3, mod_row, w_out, local):
    b, s, _ = sg.shape
    rows = min(A_BLOCKS_PER_STEP * BLOCK, s)
    assert s % rows == 0
    ns = s // rows
    n_ctx = kx.shape[1]
    blk = lambda w: pl.BlockSpec((1, rows, w), lambda bi, i: (bi, i, 0))
    blk_t = lambda w: pl.BlockSpec((1, w, rows), lambda bi, i: (bi, 0, i))
    if mod_row is None:
        mod_map = lambda bi, i: (bi, 0, 0)
    else:
        mod_map = lambda bi, i: (mod_row, 0, 0)
    in_specs = [pl.BlockSpec(memory_space=pltpu.SMEM), blk_t(A_WIDTH)]
    args = [sink, qt]
    if local:
        lo = lambda i: jnp.maximum(i - 1, 0)
        hi = lambda i: jnp.minimum(i + 1, ns - 1)
        in_specs += [pl.BlockSpec((1, rows, A_KV_WIDTH), lambda bi, i: (bi, lo(i), 0)),
                     blk(A_KV_WIDTH),
                     pl.BlockSpec((1, rows, A_KV_WIDTH), lambda bi, i: (bi, hi(i), 0)),
                     pl.BlockSpec((1, A_KV_WIDTH, rows), lambda bi, i: (bi, 0, lo(i))),
                     blk_t(A_KV_WIDTH),
                     pl.BlockSpec((1, A_KV_WIDTH, rows), lambda bi, i: (bi, 0, hi(i)))]
        args += [k, k, k, vt, vt, vt]
    in_specs += [pl.BlockSpec((1, n_ctx, A_KV_WIDTH), lambda bi, i: (bi, 0, 0)),
                 pl.BlockSpec((1, A_KV_WIDTH, n_ctx), lambda bi, i: (bi, 0, 0)),
                 blk(A_WIDTH), blk(D),
                 pl.BlockSpec((1, 1, 3 * D), mod_map),
                 pl.BlockSpec((A_WIDTH, D), lambda bi, i: (0, 0), pipeline_mode=pl.Buffered(1))]
    args += [kx, vxt, sg, x, mod3, w_out]
    return pl.pallas_call(
        functools.partial(_attn_kernel, local, ns),
        grid=(b, ns),
        in_specs=in_specs,
        out_specs=blk(D),
        out_shape=jax.ShapeDtypeStruct((b, s, D), F32),
        scratch_shapes=[pltpu.VMEM((A_WIDTH, D), BF16)],
        compiler_params=_params("arbitrary", "arbitrary"),
        name="attn_local" if local else "attn_ctx",
    )(*args)


def _chunk_cumsum(x, tri):
    hi = x.astype(BF16)
    lo = (x - hi.astype(F32)).astype(BF16)
    return _dot(tri, hi) + _dot(tri, lo)


def _inproj_gla_kernel(chunk, with_q, x_ref, m_ref, g_ref, w32_ref, wa1_ref, wa2_ref, ba_ref, tri_ref, *outs):
    *outs, w_ref = outs
    _cast_once(w32_ref, w_ref)
    if with_q:
        per_dir = (outs[0:4], outs[4:8])
        v_ref, sg_ref = outs[8:]
    else:
        per_dir = ((outs[0],), (outs[1],))
        v_ref = outs[2]
    tm = x_ref.shape[1]
    hb = _norm_mod(x_ref[0], g_ref[...], m_ref[0]).astype(BF16)
    r = _dot(hb, wa1_ref[...])
    z = _dot(r.astype(BF16), wa2_ref[...]) + ba_ref[...]
    e = jnp.exp2(jnp.abs(z) * -LOG2_E)
    la = (jnp.minimum(z, 0.0) * LOG2_E - jnp.log2(1.0 + e)) * (1.0 / GATE_TEMP)
    k = _dot(hb, w_ref[:, B_K_WIDTH:2 * B_K_WIDTH])
    if with_q:
        q = _dot(hb, w_ref[:, 0:B_K_WIDTH]) * K_SCALE
    n_chunks = tm // chunk
    for reverse in (False, True):
        lad = la[:, B_K_WIDTH:] if reverse else la[:, 0:B_K_WIDTH]
        refs = per_dir[1] if reverse else per_dir[0]
        tots, kis, kes = [], [], []
        tri = tri_ref[1 if reverse else 0]
        tb = tri.shape[0]
        cum_all = jnp.concatenate([_chunk_cumsum(lad[r * tb:(r + 1) * tb], tri) for r in range(tm // tb)], axis=0)
        for c in range(n_chunks):
            rows = slice(c * chunk, (c + 1) * chunk)
            cum = cum_all[rows]
            tot = cum[0:1] if reverse else cum[chunk - 1:chunk]
            ke = k[rows] * jnp.exp2(tot - cum)
            if not with_q:
                refs[0][0, rows] = ke.astype(BF16)
                continue
            refs[0][0, rows] = (q[rows] * jnp.exp2(cum)).astype(BF16)
            kis.append(k[rows] * jnp.exp2(-cum))
            kes.append(ke)
            tots.append(tot)
            if c % 2 == 1:
                pair = slice((c - 1) * chunk, (c + 1) * chunk)
                refs[1][0, :, pair] = jnp.concatenate(kis[-2:], axis=0).T.astype(BF16)
                refs[2][0, :, pair] = jnp.concatenate(kes[-2:], axis=0).T.astype(BF16)
        if with_q:
            pad = [jnp.zeros((DEC_ROWS - n_chunks, B_K_WIDTH), F32)] if n_chunks < DEC_ROWS else []
            refs[3][0, 0] = jnp.exp2(jnp.concatenate(tots + pad, axis=0))
    if with_q:
        sg_ref[0] = _silu(_dot(hb, w_ref[:, 2 * B_K_WIDTH + B_V_WIDTH:])).astype(BF16)
    v_ref[0] = _dot(hb, w_ref[:, 2 * B_K_WIDTH:2 * B_K_WIDTH + B_V_WIDTH]).astype(BF16)


def _inproj_gla(x, mod3, mod_row, norm_g, w, wa1, wa2, ba, chunk, with_q):
    b, s, _ = x.shape
    tm = min(ROW_TILE, s)
    nt = s // tm
    cpt = tm // chunk
    tb = max(chunk, MXU_DIM)
    assert tm % tb == 0 and tb % chunk == 0
    if mod_row is None:
        mod_map = lambda bi, i: (bi, 0, 0)
    else:
        mod_map = lambda bi, i: (mod_row, 0, 0)
    const = lambda shape: pl.BlockSpec(shape, lambda bi, i: (0,) * len(shape))
    in_specs = [
        pl.BlockSpec((1, tm, D), lambda bi, i: (bi, i, 0)),
        pl.BlockSpec((1, 1, 3 * D), mod_map),
        const((1, D)), pl.BlockSpec(w.shape, lambda bi, i: (0, 0), pipeline_mode=pl.Buffered(1)),
        const(wa1.shape), const(wa2.shape), const(ba.shape),
        const((2, tb, tb)),
    ]
    t_idx = np.arange(tb)
    same = (t_idx[:, None] // chunk) == (t_idx[None, :] // chunk)
    lower = same & (t_idx[None, :] <= t_idx[:, None])
    tri = jnp.asarray(np.stack([lower, lower.T]), dtype=BF16)
    rows = lambda w_: (jax.ShapeDtypeStruct((b, s, w_), BF16),
                       pl.BlockSpec((1, tm, w_), lambda bi, i: (bi, i, 0)))
    cols = (jax.ShapeDtypeStruct((b, B_K_WIDTH, s), BF16),
            pl.BlockSpec((1, B_K_WIDTH, tm), lambda bi, i: (bi, 0, i)))
    decs = (jax.ShapeDtypeStruct((b, nt, DEC_ROWS, B_K_WIDTH), F32),
            pl.BlockSpec((1, 1, DEC_ROWS, B_K_WIDTH), lambda bi, i: (bi, i, 0, 0)))
    if with_q:
        assert cpt <= DEC_ROWS and cpt % 2 == 0
        outs = [rows(B_K_WIDTH), cols, cols, decs] * 2 + [rows(B_V_WIDTH)] * 2
    else:
        outs = [rows(B_K_WIDTH)] * 2 + [rows(B_V_WIDTH)]
    return pl.pallas_call(
        functools.partial(_inproj_gla_kernel, chunk, with_q),
        grid=(b, nt),
        in_specs=in_specs,
        out_specs=tuple(o[1] for o in outs),
        out_shape=tuple(o[0] for o in outs),
        scratch_shapes=[pltpu.VMEM(w.shape, BF16)],
        compiler_params=_params("arbitrary", "arbitrary"),
        name="inproj_gla" if with_q else "inproj_gla_ctx",
    )(x, mod3, norm_g.reshape(1, D), w, wa1, wa2, ba, tri)


def _gla_state_kernel(kf_ref, kb_ref, v_ref, sf_ref, sb_ref):
    for h in range(B_HEADS):
        vh = v_ref[0, :, h * B_VAL_DIM:(h + 1) * B_VAL_DIM]
        ks = slice(h * B_KEY_DIM, (h + 1) * B_KEY_DIM)
        sf_ref[0, h] = _dot_tn(kf_ref[0, :, ks], vh)
        sb_ref[0, h] = _dot_tn(kb_ref[0, :, ks], vh)


def _gla_state(ke_f, ke_b, v):
    b, n, _ = v.shape
    st = jax.ShapeDtypeStruct((b, B_HEADS, B_KEY_DIM, B_VAL_DIM), F32)
    st_spec = pl.BlockSpec((1, B_HEADS, B_KEY_DIM, B_VAL_DIM), lambda bi: (bi, 0, 0, 0))
    return pl.pallas_call(
        _gla_state_kernel,
        grid=(b,),
        in_specs=[pl.BlockSpec((1, n, B_K_WIDTH), lambda bi: (bi, 0, 0))] * 2
        + [pl.BlockSpec((1, n, B_V_WIDTH), lambda bi: (bi, 0, 0))],
        out_specs=(st_spec, st_spec),
        out_shape=(st, st),
        compiler_params=_params("parallel"),
        name="gla_ctx_state",
    )(ke_f, ke_b, v)


def _scan_tile(reverse, qd_ref, kit_ref, ket_ref, dec_ref, v_ref, st_ref):
    pair_rows = 2 * CHUNK
    ti = lax.broadcasted_iota(jnp.int32, (pair_rows, pair_rows), 0)
    si = lax.broadcasted_iota(jnp.int32, (pair_rows, pair_rows), 1)
    same = (ti // CHUNK) == (si // CHUNK)
    if reverse:
        use_inv = same & (ti <= si)
        use_end = (ti < CHUNK) & (si >= CHUNK)
        second_row = lax.broadcasted_iota(jnp.int32, (pair_rows, B_KEY_DIM), 0) < CHUNK
        first_col = lax.broadcasted_iota(jnp.int32, (B_KEY_DIM, pair_rows), 1) >= CHUNK
    else:
        use_inv = same & (ti >= si)
        use_end = (ti >= CHUNK) & (si < CHUNK)
        second_row = lax.broadcasted_iota(jnp.int32, (pair_rows, B_KEY_DIM), 0) >= CHUNK
        first_col = lax.broadcasted_iota(jnp.int32, (B_KEY_DIM, pair_rows), 1) < CHUNK
    n_pairs = SCAN_TILE // pair_rows
    order = range(n_pairs - 1, -1, -1) if reverse else range(n_pairs)
    n_dec = dec_ref.shape[1] * DEC_ROWS
    dec = dec_ref[0].reshape(n_dec, B_K_WIDTH)
    dect = jnp.concatenate([dec, jnp.zeros((LANES - n_dec, B_K_WIDTH), F32)], axis=0).T
    heads = []
    for h in range(B_HEADS):
        ks = slice(h * B_KEY_DIM, (h + 1) * B_KEY_DIM)
        vs = slice(h * B_VAL_DIM, (h + 1) * B_VAL_DIM)
        st = st_ref[h]
        o_rows = [None] * n_pairs
        for p in order:
            rows = slice(p * pair_rows, (p + 1) * pair_rows)
            c_first, c_second = (2 * p + 1, 2 * p) if reverse else (2 * p, 2 * p + 1)
            qd = qd_ref[0, rows, ks]
            ket = ket_ref[0, ks, rows]
            vh = v_ref[0, rows, vs]
            s2 = _dot(qd, jnp.concatenate([kit_ref[0, ks, rows], ket], axis=1))
            a = jnp.where(use_inv, s2[:, 0:pair_rows], jnp.where(use_end, s2[:, pair_rows:], 0.0)).astype(BF16)
            q_pair = jnp.where(second_row, qd.astype(F32) * dec[c_first:c_first + 1, ks], qd.astype(F32))
            k_pair = jnp.where(first_col, ket.astype(F32) * dect[ks, c_second:c_second + 1], ket.astype(F32))
            o_rows[p] = _dot(a, vh) + _dot(q_pair.astype(BF16), st.astype(BF16))
            st = (st * (dect[ks, c_first:c_first + 1] * dect[ks, c_second:c_second + 1])
                  + _dot(k_pair.astype(BF16), vh))
        st_ref[h] = st
        heads.append(jnp.concatenate(o_rows, axis=0))
    return jnp.concatenate(heads, axis=1)


def _gla_scan_kernel(nt, qdf_ref, kitf_ref, ketf_ref, decf_ref, qdb_ref, kitb_ref, ketb_ref, decb_ref,
                     v_ref, sf_ref, sb_ref, sg_ref, x_ref, m_ref, hn_ref, w_ref, fn_ref,
                     o_ref, st_ref, of_ref):
    j = pl.program_id(1)

    @pl.when(j == 0)
    def _():
        st_ref[...] = sf_ref[0]

    @pl.when(j == nt)
    def _():
        st_ref[...] = sb_ref[0]

    @pl.when(j < nt)
    def _():
        of_ref[j] = _scan_tile(False, qdf_ref, kitf_ref, ketf_ref, decf_ref, v_ref, st_ref)

    @pl.when(j >= nt)
    def _():
        o = _scan_tile(True, qdb_ref, kitb_ref, ketb_ref, decb_ref, v_ref, st_ref) + of_ref[2 * nt - 1 - j]
        normed = []
        for h in range(B_HEADS):
            oh = o[:, h * B_VAL_DIM:(h + 1) * B_VAL_DIM]
            normed.append(oh * lax.rsqrt(jnp.mean(oh * oh, axis=-1, keepdims=True) + EPS))
        of = jnp.concatenate(normed, axis=1) * hn_ref[...]
        y = _dot((of * sg_ref[0].astype(F32)).astype(BF16), w_ref[...])
        xn = x_ref[0] + m_ref[0][:, 2 * D:3 * D] * y
        ms = jnp.mean(xn * xn, axis=-1, keepdims=True)
        o_ref[0] = xn * lax.rsqrt(ms + EPS) * fn_ref[...]


def _gla_scan(fwd, bwd, v, s_f, s_b, sg, x, mod3, head_g, w_out, final_g):
    b, s, _ = v.shape
    assert SCAN_TILE % ROW_TILE == 0 and ROW_TILE // CHUNK == DEC_ROWS
    dec_blocks = SCAN_TILE // ROW_TILE
    nt = s // SCAN_TILE
    t_fwd = lambda j: jnp.minimum(j, nt - 1)
    t_bwd = lambda j: 2 * nt - 1 - jnp.maximum(j, nt)
    t_both = lambda j: jnp.where(j < nt, j, 2 * nt - 1 - j)

    def direction(t):
        rows = pl.BlockSpec((1, SCAN_TILE, B_K_WIDTH), lambda bi, j: (bi, t(j), 0))
        cols = pl.BlockSpec((1, B_K_WIDTH, SCAN_TILE), lambda bi, j: (bi, 0, t(j)))
        decs = pl.BlockSpec((1, dec_blocks, DEC_ROWS, B_K_WIDTH), lambda bi, j: (bi, t(j), 0, 0))
        return [rows, cols, cols, decs]

    const = lambda shape: pl.BlockSpec(shape, lambda bi, j: (0,) * len(shape))
    state = pl.BlockSpec((1, B_HEADS, B_KEY_DIM, B_VAL_DIM), lambda bi, j: (bi, 0, 0, 0))
    out_rows = lambda w_: pl.BlockSpec((1, SCAN_TILE, w_), lambda bi, j: (bi, t_bwd(j), 0))
    in_specs = direction(t_fwd) + direction(t_bwd) + [
        pl.BlockSpec((1, SCAN_TILE, B_V_WIDTH), lambda bi, j: (bi, t_both(j), 0)),
        state, state, out_rows(B_V_WIDTH), out_rows(D),
        pl.BlockSpec((1, 1, 3 * D), lambda bi, j: (bi, 0, 0)),
        const((1, B_V_WIDTH)), const((B_V_WIDTH, D)), const((1, D))]
    return pl.pallas_call(
        functools.partial(_gla_scan_kernel, nt),
        grid=(b, 2 * nt),
        in_specs=in_specs,
        out_specs=out_rows(D),
        out_shape=jax.ShapeDtypeStruct((b, s, D), F32),
        scratch_shapes=[pltpu.VMEM((B_HEADS, B_KEY_DIM, B_VAL_DIM), F32),
                        pltpu.VMEM((nt, SCAN_TILE, B_V_WIDTH), F32)],
        compiler_params=_params("parallel", "arbitrary", vmem_limit=SCAN_VMEM_LIMIT),
        name="gla_scan",
    )(*fwd, *bwd, v, s_f, s_b, sg, x, mod3, head_g.reshape(1, B_V_WIDTH), w_out, final_g.reshape(1, D))


def _rope_tables(n_tokens):
    rows_n = n_tokens // GRID_W
    row = np.repeat(np.arange(rows_n, dtype=np.float32), GRID_W)
    col = np.tile(np.arange(GRID_W, dtype=np.float32), rows_n)
    inv_freq = (np.float32(ROPE_BASE) ** (-np.arange(ROPE_FREQS, dtype=np.float32) / np.float32(ROPE_FREQS)))
    inv_freq = inv_freq.astype(np.float32)
    ang = np.stack([row[:, None] * inv_freq, col[:, None] * inv_freq], axis=1)
    cos, sin = np.cos(ang).astype(np.float32), np.sin(ang).astype(np.float32)
    zero = np.zeros_like(sin)
    tile = lambda t: np.tile(t.reshape(n_tokens, A_HEAD_DIM), (1, LANES // A_HEAD_DIM))
    return (tile(np.stack([cos, cos], axis=2)),
            tile(np.stack([-sin, zero], axis=2)),
            tile(np.stack([zero, sin], axis=2)))


def _pair_heads(t, axis):
    shape = t.shape
    t = t.reshape(shape[:axis] + (A_KV_HEADS, A_GROUP, -1) + shape[axis + 1:])
    return jnp.swapaxes(t, axis, axis + 1).reshape(shape)


def kernel(x, c, ctx, c_ctx, l0_norm_g, l0_w_ada, l0_b_ada, l0_w_in, l0_sink, l0_w_out, l1_norm_g, l1_w_ada, l1_b_ada, l1_w_in, l1_wa1_f, l1_wa2_f, l1_ba_f, l1_wa1_b, l1_wa2_b, l1_ba_b, l1_head_norm_g, l1_w_out, final_norm_g):
    b, s, _ = x.shape
    ctx_row = b

    cvec = jnp.concatenate([c, c_ctx[None, :], jnp.zeros((MOD_ROWS - b - 1, D), F32)], axis=0)
    mod0 = _modulation(cvec, l0_w_ada, l0_b_ada).reshape(MOD_ROWS, 1, 3 * D)
    mod1 = _modulation(cvec, l1_w_ada, l1_b_ada).reshape(MOD_ROWS, 1, 3 * D)

    sink = _pair_heads(l0_sink.astype(F32), 0)
    qt, k, vt, sg = _inproj_attn(x, mod0, None, l0_norm_g, l0_w_in, _rope_tables(s))
    qct, kc, vct, sgc = _inproj_attn(ctx, mod0, ctx_row, l0_norm_g, l0_w_in, None)
    x1 = _attention(sink, qt, k, vt, kc, vct, sg, x, mod0, None, l0_w_out, True)
    xc1 = _attention(sink, qct, None, None, kc, vct, sgc, ctx, mod0, ctx_row, l0_w_out, False)

    w1 = l1_w_in
    wa1 = jnp.concatenate([l1_wa1_f, l1_wa1_b, jnp.zeros((D, LANES - 2 * GATE_RANK), F32)], axis=1).astype(BF16)
    wa2 = jnp.zeros((LANES, 2 * B_K_WIDTH), F32)
    wa2 = wa2.at[0:GATE_RANK, 0:B_K_WIDTH].set(l1_wa2_f)
    wa2 = wa2.at[GATE_RANK:2 * GATE_RANK, B_K_WIDTH:].set(l1_wa2_b).astype(BF16)
    ba = jnp.concatenate([l1_ba_f, l1_ba_b]).reshape(1, 2 * B_K_WIDTH)
    kec_f, kec_b, vcx = _inproj_gla(xc1, mod1, ctx_row, l1_norm_g, w1, wa1, wa2, ba, ctx.shape[1], False)
    s_f, s_b = _gla_state(kec_f, kec_b, vcx)
    outs = _inproj_gla(x1, mod1, None, l1_norm_g, w1, wa1, wa2, ba, CHUNK, True)
    v1, sg1 = outs[8:]
    return _gla_scan(outs[0:4], outs[4:8], v1, s_f, s_b, sg1, x1, mod1,
                     l1_head_norm_g, l1_w_out.astype(BF16), final_norm_g)
```

```python
import functools

import jax
import jax.numpy as jnp
import numpy as np
from jax import lax
from jax.experimental import pallas as pl
from jax.experimental.pallas import tpu as pltpu

F32 = jnp.float32
BF16 = jnp.bfloat16

D = 1024
GRID_W = 64
EPS = 1e-6
NEG_INF = -1e30

A_HEADS = 16
A_KV_HEADS = 2
A_GROUP = A_HEADS // A_KV_HEADS
A_HEAD_DIM = 64
A_WIDTH = A_HEADS * A_HEAD_DIM
A_KV_WIDTH = A_KV_HEADS * A_HEAD_DIM
BLOCK = 128
ROPE_BASE = 10000.0
ROPE_FREQS = A_HEAD_DIM // 4
Q_SCALE = A_HEAD_DIM ** -0.5
LOG2_E = 1.4426950408889634

B_HEADS = 4
B_K_WIDTH = D // 2
B_V_WIDTH = D
B_KEY_DIM = B_K_WIDTH // B_HEADS
B_VAL_DIM = B_V_WIDTH // B_HEADS
GATE_RANK = 16
GATE_TEMP = 16.0
CHUNK = 64
K_SCALE = B_KEY_DIM ** -0.5

LANES = 128
MXU_DIM = 256
MOD_ROWS = 16
ROW_TILE = 512
SCAN_TILE = 1024
SUB_ROWS = 256
DEC_ROWS = 8
SUM_ROWS = 16
A_BLOCKS_PER_STEP = 4
VMEM_LIMIT = 48 * 1024 * 1024
SCAN_VMEM_LIMIT = 56 * 1024 * 1024


def _params(*sem, vmem_limit=VMEM_LIMIT):
    return pltpu.CompilerParams(dimension_semantics=sem, vmem_limit_bytes=vmem_limit)


def _silu(x):
    return x / (1.0 + jnp.exp(-x))


def _dot(a, b):
    return jnp.dot(a, b, preferred_element_type=F32)


def _dot_nt(a, b):
    return lax.dot_general(a, b, (((1,), (1,)), ((), ())), preferred_element_type=F32)


def _dot_tn(a, b):
    return lax.dot_general(a, b, (((0,), (0,)), ((), ())), preferred_element_type=F32)


def _norm_mod(x, g, m):
    ms = jnp.mean(x * x, axis=-1, keepdims=True)
    y = x * lax.rsqrt(ms + EPS) * g
    return y * (1.0 + m[:, D:2 * D]) + m[:, 0:D]


def _cast_once(src_ref, dst_ref):
    @pl.when((pl.program_id(0) == 0) & (pl.program_id(1) == 0))
    def _():
        def body(i, carry):
            rows = pl.ds(pl.multiple_of(i * LANES, LANES), LANES)
            dst_ref[rows, :] = src_ref[rows, :].astype(dst_ref.dtype)
            return carry
        lax.fori_loop(0, src_ref.shape[0] // LANES, body, 0)


def _mod_kernel(c_ref, w_ref, b_ref, o_ref):
    s = _silu(c_ref[...])
    o_ref[...] = _dot(s.astype(BF16), w_ref[...].astype(BF16)) + b_ref[...]


def _modulation(cvec, w_ada, b_ada):
    n = w_ada.shape[1] // D
    return pl.pallas_call(
        _mod_kernel,
        grid=(n,),
        in_specs=[
            pl.BlockSpec((MOD_ROWS, D), lambda j: (0, 0)),
            pl.BlockSpec((D, D), lambda j: (0, j)),
            pl.BlockSpec((1, D), lambda j: (0, j)),
        ],
        out_specs=pl.BlockSpec((MOD_ROWS, D), lambda j: (0, j)),
        out_shape=jax.ShapeDtypeStruct((MOD_ROWS, n * D), F32),
        compiler_params=_params("arbitrary"),
        name="modulation",
    )(cvec, w_ada, b_ada.reshape(1, n * D))


def _rope(t, cos, sin_hi, sin_lo):
    return t * cos + pltpu.roll(t, LANES - ROPE_FREQS, 1) * sin_hi + pltpu.roll(t, ROPE_FREQS, 1) * sin_lo


def _pair_cast_columns(src_ref, dst_ref):
    @pl.when((pl.program_id(0) == 0) & (pl.program_id(1) == 0))
    def _():
        kv_lo, kv_hi = A_WIDTH, A_WIDTH + 2 * A_KV_WIDTH

        def body(i, carry):
            rows = pl.ds(pl.multiple_of(i * LANES, LANES), LANES)
            src = src_ref[rows, :]
            dst_ref[rows, kv_lo:kv_hi] = src[:, kv_lo:kv_hi].astype(BF16)
            for base in (0, kv_hi):
                for p in range(A_GROUP):
                    a = base + p * A_HEAD_DIM
                    b = base + (p + A_GROUP) * A_HEAD_DIM
                    pair = jnp.concatenate([src[:, a:a + A_HEAD_DIM], src[:, b:b + A_HEAD_DIM]], axis=1)
                    dst_ref[rows, base + p * LANES:base + (p + 1) * LANES] = pair.astype(BF16)
            return carry
        lax.fori_loop(0, src_ref.shape[0] // LANES, body, 0)


def _inproj_attn_kernel(rope, x_ref, m_ref, g_ref, w32_ref, *rest):
    *rest, w_ref = rest
    _pair_cast_columns(w32_ref, w_ref)
    wq_ref = w_ref.at[:, 0:A_WIDTH]
    wkv_ref = w_ref.at[:, A_WIDTH:A_WIDTH + 2 * A_KV_WIDTH]
    wg_ref = w_ref.at[:, A_WIDTH + 2 * A_KV_WIDTH:]
    if rope:
        cos_ref, shi_ref, slo_ref, qt_ref, k_ref, vt_ref, sg_ref = rest
    else:
        qt_ref, k_ref, vt_ref, sg_ref = rest
    for r in range(x_ref.shape[1] // SUB_ROWS):
        rows = slice(r * SUB_ROWS, (r + 1) * SUB_ROWS)
        if rope:
            cos, shi, slo = cos_ref[rows], shi_ref[rows], slo_ref[rows]
        hb = _norm_mod(x_ref[0, rows], g_ref[...], m_ref[0]).astype(BF16)
        q = _dot(hb, wq_ref[...])
        for j in range(A_WIDTH // LANES):
            qj = q[:, j * LANES:(j + 1) * LANES]
            if rope:
                qj = _rope(qj, cos, shi, slo)
            qt_ref[0, j * LANES:(j + 1) * LANES, rows] = (qj * (Q_SCALE * LOG2_E)).T.astype(BF16)
        kv = _dot(hb, wkv_ref[...])
        k = kv[:, 0:A_KV_WIDTH]
        if rope:
            k = _rope(k, cos, shi, slo)
        k_ref[0, rows] = k.astype(BF16)
        vt_ref[0, :, rows] = kv[:, A_KV_WIDTH:].T.astype(BF16)
        g = _dot(hb, wg_ref[...])
        sg_ref[0, rows] = _silu(g).astype(BF16)


def _inproj_attn(x, mod3, mod_row, norm_g, w_in, tables):
    b, s, _ = x.shape
    tm = min(ROW_TILE, s)
    rope = tables is not None
    if mod_row is None:
        mod_map = lambda bi, i: (bi, 0, 0)
    else:
        mod_map = lambda bi, i: (mod_row, 0, 0)
    in_specs = [
        pl.BlockSpec((1, tm, D), lambda bi, i: (bi, i, 0)),
        pl.BlockSpec((1, 1, 3 * D), mod_map),
        pl.BlockSpec((1, D), lambda bi, i: (0, 0)),
        pl.BlockSpec(w_in.shape, lambda bi, i: (0, 0), pipeline_mode=pl.Buffered(1)),
    ]
    args = [x, mod3, norm_g.reshape(1, D), w_in]
    if rope:
        in_specs += [pl.BlockSpec((tm, LANES), lambda bi, i: (i, 0))] * 3
        args += list(tables)
    out_shape = (
        jax.ShapeDtypeStruct((b, A_WIDTH, s), BF16),
        jax.ShapeDtypeStruct((b, s, A_KV_WIDTH), BF16),
        jax.ShapeDtypeStruct((b, A_KV_WIDTH, s), BF16),
        jax.ShapeDtypeStruct((b, s, A_WIDTH), BF16),
    )
    out_specs = (
        pl.BlockSpec((1, A_WIDTH, tm), lambda bi, i: (bi, 0, i)),
        pl.BlockSpec((1, tm, A_KV_WIDTH), lambda bi, i: (bi, i, 0)),
        pl.BlockSpec((1, A_KV_WIDTH, tm), lambda bi, i: (bi, 0, i)),
        pl.BlockSpec((1, tm, A_WIDTH), lambda bi, i: (bi, i, 0)),
    )
    return pl.pallas_call(
        functools.partial(_inproj_attn_kernel, rope),
        grid=(b, s // tm),
        in_specs=in_specs,
        out_specs=out_specs,
        out_shape=out_shape,
        scratch_shapes=[pltpu.VMEM(w_in.shape, BF16)],
        compiler_params=_params("arbitrary", "arbitrary"),
        name="inproj_attn_rope" if rope else "inproj_attn_ctx",
    )(*args)


def _attn_block(kwin, vtw, qts, sinks, ok_prev, ok_next):
    n_keys, nq = kwin.shape[0], qts.shape[1]
    lane = lax.broadcasted_iota(jnp.int32, kwin.shape, 1)
    kbd = jnp.concatenate([jnp.where(lane < A_HEAD_DIM, kwin, jnp.zeros_like(kwin)),
                           jnp.where(lane >= A_HEAD_DIM, kwin, jnp.zeros_like(kwin))], axis=0)
    ones = jnp.where(lax.broadcasted_iota(jnp.int32, (SUM_ROWS, n_keys), 0) == 0, 1.0, 0.0).astype(BF16)

    st = _dot(kbd, qts)
    outs = []
    for hh in range(2):
        sh = st[hh * n_keys:(hh + 1) * n_keys]
        if ok_prev is not None:
            parts = [jnp.where(ok_prev, sh[0:BLOCK], NEG_INF),
                     sh[BLOCK:2 * BLOCK],
                     jnp.where(ok_next, sh[2 * BLOCK:3 * BLOCK], NEG_INF),
                     sh[3 * BLOCK:]]
        else:
            parts = [sh]
        mx = functools.reduce(jnp.maximum, [jnp.max(t, axis=0, keepdims=True) for t in parts])
        mx = jnp.maximum(mx, sinks[hh])
        probs = jnp.concatenate([jnp.exp2(t - mx).astype(BF16) for t in parts], axis=0)
        vt_h = jnp.concatenate([vtw[hh * A_HEAD_DIM:(hh + 1) * A_HEAD_DIM], ones], axis=0)
        ot = _dot(vt_h, probs)
        den = ot[A_HEAD_DIM:A_HEAD_DIM + 1] + jnp.exp2(sinks[hh] - mx)
        outs.append(ot[0:A_HEAD_DIM] * (1.0 / den))
    return jnp.concatenate(outs, axis=0)


def _pair_cast_rows(src_ref, dst_ref):
    @pl.when((pl.program_id(0) == 0) & (pl.program_id(1) == 0))
    def _():
        for p in range(A_GROUP):
            for half, h in enumerate((p, p + A_GROUP)):
                lo = p * LANES + half * A_HEAD_DIM
                dst_ref[lo:lo + A_HEAD_DIM, :] = src_ref[h * A_HEAD_DIM:(h + 1) * A_HEAD_DIM, :].astype(BF16)


def _attn_kernel(local, n_steps, sink_ref, qt_ref, *rest):
    *rest, w_ref = rest
    n_blk = rest[-1].shape[1] // BLOCK
    if local:
        (kp_ref, kc_ref, kn_ref, vp_ref, vc_ref, vn_ref, kx_ref, vx_ref,
         sg_ref, x_ref, m_ref, w32_ref, o_ref) = rest
        last = slice((n_blk - 1) * BLOCK, n_blk * BLOCK)
        k_blocks = ([kp_ref[0, last]] + [kc_ref[0, i * BLOCK:(i + 1) * BLOCK] for i in range(n_blk)]
                    + [kn_ref[0, 0:BLOCK]])
        v_blocks = ([vp_ref[0, :, last]] + [vc_ref[0, :, i * BLOCK:(i + 1) * BLOCK] for i in range(n_blk)]
                    + [vn_ref[0, :, 0:BLOCK]])
    else:
        kx_ref, vx_ref, sg_ref, x_ref, m_ref, w32_ref, o_ref = rest
    _pair_cast_rows(w32_ref, w_ref)
    step = pl.program_id(1)
    nq = A_GROUP * BLOCK
    chunk_of = lax.broadcasted_iota(jnp.int32, (1, nq), 1) // BLOCK
    sinks = []
    for hh in range(2):
        sk = jnp.full((1, nq), sink_ref[hh] * LOG2_E, F32)
        for c in range(1, A_GROUP):
            sk = jnp.where(chunk_of == c, sink_ref[2 * c + hh] * LOG2_E, sk)
        sinks.append(sk)
    if local:
        kj = lax.broadcasted_iota(jnp.int32, (BLOCK, nq), 0)
        qi = lax.broadcasted_iota(jnp.int32, (BLOCK, nq), 1) % BLOCK

    rows_out = []
    for blk in range(n_blk):
        qts = jnp.concatenate([qt_ref[0, c * LANES:(c + 1) * LANES, blk * BLOCK:(blk + 1) * BLOCK]
                               for c in range(A_GROUP)], axis=1)
        if local:
            kwin = jnp.concatenate(k_blocks[blk:blk + 3] + [kx_ref[0]], axis=0)
            vtw = jnp.concatenate(v_blocks[blk:blk + 3] + [vx_ref[0]], axis=1)
            ok_prev = (kj >= qi) & (step > 0) if blk == 0 else (kj >= qi)
            ok_next = (kj <= qi) & (step < n_steps - 1) if blk == n_blk - 1 else (kj <= qi)
        else:
            kwin, vtw, ok_prev, ok_next = kx_ref[0], vx_ref[0], None, None
        ot = _attn_block(kwin, vtw, qts, sinks, ok_prev, ok_next)
        rows = slice(blk * BLOCK, (blk + 1) * BLOCK)
        outs = []
        for c in range(A_GROUP):
            sg = sg_ref[0, rows, c * LANES:(c + 1) * LANES].astype(F32)
            outs.append((ot[:, c * BLOCK:(c + 1) * BLOCK].T * sg).astype(BF16))
        rows_out.append(jnp.concatenate(outs, axis=1))
    y = _dot(jnp.concatenate(rows_out, axis=0), w_ref[...])
    gate = m_ref[0][:, 2 * D:3 * D]
    o_ref[0] = x_ref[0] + gate * y


def _attention(sink, qt, k, vt, kx, vxt, sg, x, mod3, mod_row, w_out, local):
    b, s, _ = sg.shape
    rows = min(A_BLOCKS_PER_STEP * BLOCK, s)
    assert s % rows == 0
    ns = s // rows
    n_ctx = kx.shape[1]
    blk = lambda w: pl.BlockSpec((1, rows, w), lambda bi, i: (bi, i, 0))
    blk_t = lambda w: pl.BlockSpec((1, w, rows), lambda bi, i: (bi, 0, i))
    if mod_row is None:
        mod_map = lambda bi, i: (bi, 0, 0)
    else:
        mod_map = lambda bi, i: (mod_row, 0, 0)
    in_specs = [pl.BlockSpec(memory_space=pltpu.SMEM), blk_t(A_WIDTH)]
    args = [sink, qt]
    if local:
        lo = lambda i: jnp.maximum(i - 1, 0)
        hi = lambda i: jnp.minimum(i + 1, ns - 1)
        in_specs += [pl.BlockSpec((1, rows, A_KV_WIDTH), lambda bi, i: (bi, lo(i), 0)),
                     blk(A_KV_WIDTH),
                     pl.BlockSpec((1, rows, A_KV_WIDTH), lambda bi, i: (bi, hi(i), 0)),
                     pl.BlockSpec((1, A_KV_WIDTH, rows), lambda bi, i: (bi, 0, lo(i))),
                     blk_t(A_KV_WIDTH),
                     pl.BlockSpec((1, A_KV_WIDTH, rows), lambda bi, i: (bi, 0, hi(i)))]
        args += [k, k, k, vt, vt, vt]
    in_specs += [pl.BlockSpec((1, n_ctx, A_KV_WIDTH), lambda bi, i: (bi, 0, 0)),
                 pl.BlockSpec((1, A_KV_WIDTH, n_ctx), lambda bi, i: (bi, 0, 0)),
                 blk(A_WIDTH), blk(D),
                 pl.BlockSpec((1, 1, 3 * D), mod_map),
                 pl.BlockSpec((A_WIDTH, D), lambda bi, i: (0, 0), pipeline_mode=pl.Buffered(1))]
    args += [kx, vxt, sg, x, mod3, w_out]
    return pl.pallas_call(
        functools.partial(_attn_kernel, local, ns),
        grid=(b, ns),
        in_specs=in_specs,
        out_specs=blk(D),
        out_shape=jax.ShapeDtypeStruct((b, s, D), F32),
        scratch_shapes=[pltpu.VMEM((A_WIDTH, D), BF16)],
        compiler_params=_params("arbitrary", "arbitrary"),
        name="attn_local" if local else "attn_ctx",
    )(*args)


def _chunk_cumsum(x, tri):
    hi = x.astype(BF16)
    lo = (x - hi.astype(F32)).astype(BF16)
    return _dot(tri, hi) + _dot(tri, lo)


def _inproj_gla_kernel(chunk, with_q, x_ref, m_ref, g_ref, w32_ref, wa1_ref, wa2_ref, ba_ref, tri_ref, *outs):
    *outs, w_ref = outs
    _cast_once(w32_ref, w_ref)
    if with_q:
        per_dir = (outs[0:4], outs[4:8])
        v_ref, sg_ref = outs[8:]
    else:
        per_dir = ((outs[0],), (outs[1],))
        v_ref = outs[2]
    tm = x_ref.shape[1]
    hb = _norm_mod(x_ref[0], g_ref[...], m_ref[0]).astype(BF16)
    r = _dot(hb, wa1_ref[...])
    z = _dot(r.astype(BF16), wa2_ref[...]) + ba_ref[...]
    e = jnp.exp2(jnp.abs(z) * -LOG2_E)
    la = (jnp.minimum(z, 0.0) * LOG2_E - jnp.log2(1.0 + e)) * (1.0 / GATE_TEMP)
    k = _dot(hb, w_ref[:, B_K_WIDTH:2 * B_K_WIDTH])
    if with_q:
        q = _dot(hb, w_ref[:, 0:B_K_WIDTH]) * K_SCALE
    n_chunks = tm // chunk
    for reverse in (False, True):
        lad = la[:, B_K_WIDTH:] if reverse else la[:, 0:B_K_WIDTH]
        refs = per_dir[1] if reverse else per_dir[0]
        tots, kis, kes = [], [], []
        tri = tri_ref[1 if reverse else 0]
        tb = tri.shape[0]
        cum_all = jnp.concatenate([_chunk_cumsum(lad[r * tb:(r + 1) * tb], tri) for r in range(tm // tb)], axis=0)
        for c in range(n_chunks):
            rows = slice(c * chunk, (c + 1) * chunk)
            cum = cum_all[rows]
            tot = cum[0:1] if reverse else cum[chunk - 1:chunk]
            ke = k[rows] * jnp.exp2(tot - cum)
            if not with_q:
                refs[0][0, rows] = ke.astype(BF16)
                continue
            refs[0][0, rows] = (q[rows] * jnp.exp2(cum)).astype(BF16)
            kis.append(k[rows] * jnp.exp2(-cum))
            kes.append(ke)
            tots.append(tot)
            if c % 2 == 1:
                pair = slice((c - 1) * chunk, (c + 1) * chunk)
                refs[1][0, :, pair] = jnp.concatenate(kis[-2:], axis=0).T.astype(BF16)
                refs[2][0, :, pair] = jnp.concatenate(kes[-2:], axis=0).T.astype(BF16)
        if with_q:
            pad = [jnp.zeros((DEC_ROWS - n_chunks, B_K_WIDTH), F32)] if n_chunks < DEC_ROWS else []
            refs[3][0, 0] = jnp.exp2(jnp.concatenate(tots + pad, axis=0))
    if with_q:
        sg_ref[0] = _silu(_dot(hb, w_ref[:, 2 * B_K_WIDTH + B_V_WIDTH:])).astype(BF16)
    v_ref[0] = _dot(hb, w_ref[:, 2 * B_K_WIDTH:2 * B_K_WIDTH + B_V_WIDTH]).astype(BF16)


def _inproj_gla(x, mod3, mod_row, norm_g, w, wa1, wa2, ba, chunk, with_q):
    b, s, _ = x.shape
    tm = min(ROW_TILE, s)
    nt = s // tm
    cpt = tm // chunk
    tb = max(chunk, MXU_DIM)
    assert tm % tb == 0 and tb % chunk == 0
    if mod_row is None:
        mod_map = lambda bi, i: (bi, 0, 0)
    else:
        mod_map = lambda bi, i: (mod_row, 0, 0)
    const = lambda shape: pl.BlockSpec(shape, lambda bi, i: (0,) * len(shape))
    in_specs = [
        pl.BlockSpec((1, tm, D), lambda bi, i: (bi, i, 0)),
        pl.BlockSpec((1, 1, 3 * D), mod_map),
        const((1, D)), pl.BlockSpec(w.shape, lambda bi, i: (0, 0), pipeline_mode=pl.Buffered(1)),
        const(wa1.shape), const(wa2.shape), const(ba.shape),
        const((2, tb, tb)),
    ]
    t_idx = np.arange(tb)
    same = (t_idx[:, None] // chunk) == (t_idx[None, :] // chunk)
    lower = same & (t_idx[None, :] <= t_idx[:, None])
    tri = jnp.asarray(np.stack([lower, lower.T]), dtype=BF16)
    rows = lambda w_: (jax.ShapeDtypeStruct((b, s, w_), BF16),
                       pl.BlockSpec((1, tm, w_), lambda bi, i: (bi, i, 0)))
    cols = (jax.ShapeDtypeStruct((b, B_K_WIDTH, s), BF16),
            pl.BlockSpec((1, B_K_WIDTH, tm), lambda bi, i: (bi, 0, i)))
    decs = (jax.ShapeDtypeStruct((b, nt, DEC_ROWS, B_K_WIDTH), F32),
            pl.BlockSpec((1, 1, DEC_ROWS, B_K_WIDTH), lambda bi, i: (bi, i, 0, 0)))
    if with_q:
        assert cpt <= DEC_ROWS and cpt % 2 == 0
        outs = [rows(B_K_WIDTH), cols, cols, decs] * 2 + [rows(B_V_WIDTH)] * 2
    else:
        outs = [rows(B_K_WIDTH)] * 2 + [rows(B_V_WIDTH)]
    return pl.pallas_call(
        functools.partial(_inproj_gla_kernel, chunk, with_q),
        grid=(b, nt),
        in_specs=in_specs,
        out_specs=tuple(o[1] for o in outs),
        out_shape=tuple(o[0] for o in outs),
        scratch_shapes=[pltpu.VMEM(w.shape, BF16)],
        compiler_params=_params("arbitrary", "arbitrary"),
        name="inproj_gla" if with_q else "inproj_gla_ctx",
    )(x, mod3, norm_g.reshape(1, D), w, wa1, wa2, ba, tri)


def _gla_state_kernel(kf_ref, kb_ref, v_ref, sf_ref, sb_ref):
    for h in range(B_HEADS):
        vh = v_ref[0, :, h * B_VAL_DIM:(h + 1) * B_VAL_DIM]
        ks = slice(h * B_KEY_DIM, (h + 1) * B_KEY_DIM)
        sf_ref[0, h] = _dot_tn(kf_ref[0, :, ks], vh)
        sb_ref[0, h] = _dot_tn(kb_ref[0, :, ks], vh)


def _gla_state(ke_f, ke_b, v):
    b, n, _ = v.shape
    st = jax.ShapeDtypeStruct((b, B_HEADS, B_KEY_DIM, B_VAL_DIM), F32)
    st_spec = pl.BlockSpec((1, B_HEADS, B_KEY_DIM, B_VAL_DIM), lambda bi: (bi, 0, 0, 0))
    return pl.pallas_call(
        _gla_state_kernel,
        grid=(b,),
        in_specs=[pl.BlockSpec((1, n, B_K_WIDTH), lambda bi: (bi, 0, 0))] * 2
        + [pl.BlockSpec((1, n, B_V_WIDTH), lambda bi: (bi, 0, 0))],
        out_specs=(st_spec, st_spec),
        out_shape=(st, st),
        compiler_params=_params("parallel"),
        name="gla_ctx_state",
    )(ke_f, ke_b, v)


def _scan_tile(reverse, qd_ref, kit_ref, ket_ref, dec_ref, v_ref, st_ref):
    pair_rows = 2 * CHUNK
    ti = lax.broadcasted_iota(jnp.int32, (pair_rows, pair_rows), 0)
    si = lax.broadcasted_iota(jnp.int32, (pair_rows, pair_rows), 1)
    same = (ti // CHUNK) == (si // CHUNK)
    if reverse:
        use_inv = same & (ti <= si)
        use_end = (ti < CHUNK) & (si >= CHUNK)
        second_row = lax.broadcasted_iota(jnp.int32, (pair_rows, B_KEY_DIM), 0) < CHUNK
        first_col = lax.broadcasted_iota(jnp.int32, (B_KEY_DIM, pair_rows), 1) >= CHUNK
    else:
        use_inv = same & (ti >= si)
        use_end = (ti >= CHUNK) & (si < CHUNK)
        second_row = lax.broadcasted_iota(jnp.int32, (pair_rows, B_KEY_DIM), 0) >= CHUNK
        first_col = lax.broadcasted_iota(jnp.int32, (B_KEY_DIM, pair_rows), 1) < CHUNK
    n_pairs = SCAN_TILE // pair_rows
    order = range(n_pairs - 1, -1, -1) if reverse else range(n_pairs)
    n_dec = dec_ref.shape[1] * DEC_ROWS
    dec = dec_ref[0].reshape(n_dec, B_K_WIDTH)
    dect = jnp.concatenate([dec, jnp.zeros((LANES - n_dec, B_K_WIDTH), F32)], axis=0).T
    heads = []
    for h in range(B_HEADS):
        ks = slice(h * B_KEY_DIM, (h + 1) * B_KEY_DIM)
        vs = slice(h * B_VAL_DIM, (h + 1) * B_VAL_DIM)
        st = st_ref[h]
        o_rows = [None] * n_pairs
        for p in order:
            rows = slice(p * pair_rows, (p + 1) * pair_rows)
            c_first, c_second = (2 * p + 1, 2 * p) if reverse else (2 * p, 2 * p + 1)
            qd = qd_ref[0, rows, ks]
            ket = ket_ref[0, ks, rows]
            vh = v_ref[0, rows, vs]
            s2 = _dot(qd, jnp.concatenate([kit_ref[0, ks, rows], ket], axis=1))
            a = jnp.where(use_inv, s2[:, 0:pair_rows], jnp.where(use_end, s2[:, pair_rows:], 0.0)).astype(BF16)
            q_pair = jnp.where(second_row, qd.astype(F32) * dec[c_first:c_first + 1, ks], qd.astype(F32))
            k_pair = jnp.where(first_col, ket.astype(F32) * dect[ks, c_second:c_second + 1], ket.astype(F32))
            o_rows[p] = _dot(a, vh) + _dot(q_pair.astype(BF16), st.astype(BF16))
            st = (st * (dect[ks, c_first:c_first + 1] * dect[ks, c_second:c_second + 1])
                  + _dot(k_pair.astype(BF16), vh))
        st_ref[h] = st
        heads.append(jnp.concatenate(o_rows, axis=0))
    return jnp.concatenate(heads, axis=1)


def _gla_scan_kernel(nt, qdf_ref, kitf_ref, ketf_ref, decf_ref, qdb_ref, kitb_ref, ketb_ref, decb_ref,
                     v_ref, sf_ref, sb_ref, sg_ref, x_ref, m_ref, hn_ref, w_ref, fn_ref,
                     o_ref, st_ref, of_ref):
    j = pl.program_id(1)

    @pl.when(j == 0)
    def _():
        st_ref[...] = sf_ref[0]

    @pl.when(j == nt)
    def _():
        st_ref[...] = sb_ref[0]

    @pl.when(j < nt)
    def _():
        of_ref[j] = _scan_tile(False, qdf_ref, kitf_ref, ketf_ref, decf_ref, v_ref, st_ref)

    @pl.when(j >= nt)
    def _():
        o = _scan_tile(True, qdb_ref, kitb_ref, ketb_ref, decb_ref, v_ref, st_ref) + of_ref[2 * nt - 1 - j]
        normed = []
        for h in range(B_HEADS):
            oh = o[:, h * B_VAL_DIM:(h + 1) * B_VAL_DIM]
            normed.append(oh * lax.rsqrt(jnp.mean(oh * oh, axis=-1, keepdims=True) + EPS))
        of = jnp.concatenate(normed, axis=1) * hn_ref[...]
        y = _dot((of * sg_ref[0].astype(F32)).astype(BF16), w_ref[...])
        xn = x_ref[0] + m_ref[0][:, 2 * D:3 * D] * y
        ms = jnp.mean(xn * xn, axis=-1, keepdims=True)
        o_ref[0] = xn * lax.rsqrt(ms + EPS) * fn_ref[...]


def _gla_scan(fwd, bwd, v, s_f, s_b, sg, x, mod3, head_g, w_out, final_g):
    b, s, _ = v.shape
    assert SCAN_TILE % ROW_TILE == 0 and ROW_TILE // CHUNK == DEC_ROWS
    dec_blocks = SCAN_TILE // ROW_TILE
    nt = s // SCAN_TILE
    t_fwd = lambda j: jnp.minimum(j, nt - 1)
    t_bwd = lambda j: 2 * nt - 1 - jnp.maximum(j, nt)
    t_both = lambda j: jnp.where(j < nt, j, 2 * nt - 1 - j)

    def direction(t):
        rows = pl.BlockSpec((1, SCAN_TILE, B_K_WIDTH), lambda bi, j: (bi, t(j), 0))
        cols = pl.BlockSpec((1, B_K_WIDTH, SCAN_TILE), lambda bi, j: (bi, 0, t(j)))
        decs = pl.BlockSpec((1, dec_blocks, DEC_ROWS, B_K_WIDTH), lambda bi, j: (bi, t(j), 0, 0))
        return [rows, cols, cols, decs]

    const = lambda shape: pl.BlockSpec(shape, lambda bi, j: (0,) * len(shape))
    state = pl.BlockSpec((1, B_HEADS, B_KEY_DIM, B_VAL_DIM), lambda bi, j: (bi, 0, 0, 0))
    out_rows = lambda w_: pl.BlockSpec((1, SCAN_TILE, w_), lambda bi, j: (bi, t_bwd(j), 0))
    in_specs = direction(t_fwd) + direction(t_bwd) + [
        pl.BlockSpec((1, SCAN_TILE, B_V_WIDTH), lambda bi, j: (bi, t_both(j), 0)),
        state, state, out_rows(B_V_WIDTH), out_rows(D),
        pl.BlockSpec((1, 1, 3 * D), lambda bi, j: (bi, 0, 0)),
        const((1, B_V_WIDTH)), const((B_V_WIDTH, D)), const((1, D))]
    return pl.pallas_call(
        functools.partial(_gla_scan_kernel, nt),
        grid=(b, 2 * nt),
        in_specs=in_specs,
        out_specs=out_rows(D),
        out_shape=jax.ShapeDtypeStruct((b, s, D), F32),
        scratch_shapes=[pltpu.VMEM((B_HEADS, B_KEY_DIM, B_VAL_DIM), F32),
                        pltpu.VMEM((nt, SCAN_TILE, B_V_WIDTH), F32)],
        compiler_params=_params("parallel", "arbitrary", vmem_limit=SCAN_VMEM_LIMIT),
        name="gla_scan",
    )(*fwd, *bwd, v, s_f, s_b, sg, x, mod3, head_g.reshape(1, B_V_WIDTH), w_out, final_g.reshape(1, D))


def _rope_tables(n_tokens):
    rows_n = n_tokens // GRID_W
    row = np.repeat(np.arange(rows_n, dtype=np.float32), GRID_W)
    col = np.tile(np.arange(GRID_W, dtype=np.float32), rows_n)
    inv_freq = (np.float32(ROPE_BASE) ** (-np.arange(ROPE_FREQS, dtype=np.float32) / np.float32(ROPE_FREQS)))
    inv_freq = inv_freq.astype(np.float32)
    ang = np.stack([row[:, None] * inv_freq, col[:, None] * inv_freq], axis=1)
    cos, sin = np.cos(ang).astype(np.float32), np.sin(ang).astype(np.float32)
    zero = np.zeros_like(sin)
    tile = lambda t: np.tile(t.reshape(n_tokens, A_HEAD_DIM), (1, LANES // A_HEAD_DIM))
    return (tile(np.stack([cos, cos], axis=2)),
            tile(np.stack([-sin, zero], axis=2)),
            tile(np.stack([zero, sin], axis=2)))


def _pair_heads(t, axis):
    shape = t.shape
    t = t.reshape(shape[:axis] + (A_KV_HEADS, A_GROUP, -1) + shape[axis + 1:])
    return jnp.swapaxes(t, axis, axis + 1).reshape(shape)


def kernel(x, c, ctx, c_ctx, l0_norm_g, l0_w_ada, l0_b_ada, l0_w_in, l0_sink, l0_w_out, l1_norm_g, l1_w_ada, l1_b_ada, l1_w_in, l1_wa1_f, l1_wa2_f, l1_ba_f, l1_wa1_b, l1_wa2_b, l1_ba_b, l1_head_norm_g, l1_w_out, final_norm_g):
    b, s, _ = x.shape
    ctx_row = b

    cvec = jnp.concatenate([c, c_ctx[None, :], jnp.zeros((MOD_ROWS - b - 1, D), F32)], axis=0)
    mod0 = _modulation(cvec, l0_w_ada, l0_b_ada).reshape(MOD_ROWS, 1, 3 * D)
    mod1 = _modulation(cvec, l1_w_ada, l1_b_ada).reshape(MOD_ROWS, 1, 3 * D)

    sink = _pair_heads(l0_sink.astype(F32), 0)
    qt, k, vt, sg = _inproj_attn(x, mod0, None, l0_norm_g, l0_w_in, _rope_tables(s))
    qct, kc, vct, sgc = _inproj_attn(ctx, mod0, ctx_row, l0_norm_g, l0_w_in, None)
    x1 = _attention(sink, qt, k, vt, kc, vct, sg, x, mod0, None, l0_w_out, True)
    xc1 = _attention(sink, qct, None, None, kc, vct, sgc, ctx, mod0, ctx_row, l0_w_out, False)

    w1 = l1_w_in
    wa1 = jnp.concatenate([l1_wa1_f, l1_wa1_b, jnp.zeros((D, LANES - 2 * GATE_RANK), F32)], axis=1).astype(BF16)
    wa2 = jnp.zeros((LANES, 2 * B_K_WIDTH), F32)
    wa2 = wa2.at[0:GATE_RANK, 0:B_K_WIDTH].set(l1_wa2_f)
    wa2 = wa2.at[GATE_RANK:2 * GATE_RANK, B_K_WIDTH:].set(l1_wa2_b).astype(BF16)
    ba = jnp.concatenate([l1_ba_f, l1_ba_b]).reshape(1, 2 * B_K_WIDTH)
    kec_f, kec_b, vcx = _inproj_gla(xc1, mod1, ctx_row, l1_norm_g, w1, wa1, wa2, ba, ctx.shape[1], False)
    s_f, s_b = _gla_state(kec_f, kec_b, vcx)
    outs = _inproj_gla(x1, mod1, None, l1_norm_g, w1, wa1, wa2, ba, CHUNK, True)
    v1, sg1 = outs[8:]
    return _gla_scan(outs[0:4], outs[4:8], v1, s_f, s_b, sg1, x1, mod1,
                     l1_head_norm_g, l1_w_out.astype(BF16), final_norm_g)
```

```python
import functools

import jax
import jax.numpy as jnp
import numpy as np
from jax import lax
from jax.experimental import pallas as pl
from jax.experimental.pallas import tpu as pltpu

F32 = jnp.float32
BF16 = jnp.bfloat16

D = 1024
GRID_W = 64
EPS = 1e-6
NEG_INF = -1e30

A_HEADS = 16
A_KV_HEADS = 2
A_GROUP = A_HEADS // A_KV_HEADS
A_HEAD_DIM = 64
A_WIDTH = A_HEADS * A_HEAD_DIM
A_KV_WIDTH = A_KV_HEADS * A_HEAD_DIM
BLOCK = 128
ROPE_BASE = 10000.0
ROPE_FREQS = A_HEAD_DIM // 4
Q_SCALE = A_HEAD_DIM ** -0.5
LOG2_E = 1.4426950408889634

B_HEADS = 4
B_K_WIDTH = D // 2
B_V_WIDTH = D
B_KEY_DIM = B_K_WIDTH // B_HEADS
B_VAL_DIM = B_V_WIDTH // B_HEADS
GATE_RANK = 16
GATE_TEMP = 16.0
CHUNK = 64
K_SCALE = B_KEY_DIM ** -0.5

LANES = 128
MXU_DIM = 256
MOD_ROWS = 16
ROW_TILE = 512
SCAN_TILE = 1024
SUB_ROWS = 256
DEC_ROWS = 8
SUM_ROWS = 16
A_BLOCKS_PER_STEP = 4
VMEM_LIMIT = 48 * 1024 * 1024
SCAN_VMEM_LIMIT = 56 * 1024 * 1024


def _params(*sem, vmem_limit=VMEM_LIMIT):
    return pltpu.CompilerParams(dimension_semantics=sem, vmem_limit_bytes=vmem_limit)


def _silu(x):
    return x / (1.0 + jnp.exp(-x))


def _dot(a, b):
    return jnp.dot(a, b, preferred_element_type=F32)


def _dot_nt(a, b):
    return lax.dot_general(a, b, (((1,), (1,)), ((), ())), preferred_element_type=F32)


def _dot_tn(a, b):
    return lax.dot_general(a, b, (((0,), (0,)), ((), ())), preferred_element_type=F32)


def _norm_mod(x, g, m):
    ms = jnp.mean(x * x, axis=-1, keepdims=True)
    y = x * lax.rsqrt(ms + EPS) * g
    return y * (1.0 + m[:, D:2 * D]) + m[:, 0:D]


def _cast_once(src_ref, dst_ref):
    @pl.when((pl.program_id(0) == 0) & (pl.program_id(1) == 0))
    def _():
        def body(i, carry):
            rows = pl.ds(pl.multiple_of(i * LANES, LANES), LANES)
            dst_ref[rows, :] = src_ref[rows, :].astype(dst_ref.dtype)
            return carry
        lax.fori_loop(0, src_ref.shape[0] // LANES, body, 0)


def _mod_kernel(c_ref, w_ref, b_ref, o_ref):
    s = _silu(c_ref[...])
    o_ref[...] = _dot(s.astype(BF16), w_ref[...].astype(BF16)) + b_ref[...]


def _modulation(cvec, w_ada, b_ada):
    n = w_ada.shape[1] // D
    return pl.pallas_call(
        _mod_kernel,
        grid=(n,),
        in_specs=[
            pl.BlockSpec((MOD_ROWS, D), lambda j: (0, 0)),
            pl.BlockSpec((D, D), lambda j: (0, j)),
            pl.BlockSpec((1, D), lambda j: (0, j)),
        ],
        out_specs=pl.BlockSpec((MOD_ROWS, D), lambda j: (0, j)),
        out_shape=jax.ShapeDtypeStruct((MOD_ROWS, n * D), F32),
        compiler_params=_params("arbitrary"),
        name="modulation",
    )(cvec, w_ada, b_ada.reshape(1, n * D))


def _rope(t, cos, sin_hi, sin_lo):
    return t * cos + pltpu.roll(t, LANES - ROPE_FREQS, 1) * sin_hi + pltpu.roll(t, ROPE_FREQS, 1) * sin_lo


def _pair_cast_columns(src_ref, dst_ref):
    @pl.when((pl.program_id(0) == 0) & (pl.program_id(1) == 0))
    def _():
        kv_lo, kv_hi = A_WIDTH, A_WIDTH + 2 * A_KV_WIDTH

        def body(i, carry):
            rows = pl.ds(pl.multiple_of(i * LANES, LANES), LANES)
            src = src_ref[rows, :]
            dst_ref[rows, kv_lo:kv_hi] = src[:, kv_lo:kv_hi].astype(BF16)
            for base in (0, kv_hi):
                for p in range(A_GROUP):
                    a = base + p * A_HEAD_DIM
                    b = base + (p + A_GROUP) * A_HEAD_DIM
                    pair = jnp.concatenate([src[:, a:a + A_HEAD_DIM], src[:, b:b + A_HEAD_DIM]], axis=1)
                    dst_ref[rows, base + p * LANES:base + (p + 1) * LANES] = pair.astype(BF16)
            return carry
        lax.fori_loop(0, src_ref.shape[0] // LANES, body, 0)


def _inproj_attn_kernel(rope, x_ref, m_ref, g_ref, w32_ref, *rest):
    *rest, w_ref = rest
    _pair_cast_columns(w32_ref, w_ref)
    wq_ref = w_ref.at[:, 0:A_WIDTH]
    wkv_ref = w_ref.at[:, A_WIDTH:A_WIDTH + 2 * A_KV_WIDTH]
    wg_ref = w_ref.at[:, A_WIDTH + 2 * A_KV_WIDTH:]
    if rope:
        cos_ref, shi_ref, slo_ref, qt_ref, k_ref, vt_ref, sg_ref = rest
    else:
        qt_ref, k_ref, vt_ref, sg_ref = rest
    for r in range(x_ref.shape[1] // SUB_ROWS):
        rows = slice(r * SUB_ROWS, (r + 1) * SUB_ROWS)
        if rope:
            cos, shi, slo = cos_ref[rows], shi_ref[rows], slo_ref[rows]
        hb = _norm_mod(x_ref[0, rows], g_ref[...], m_ref[0]).astype(BF16)
        q = _dot(hb, wq_ref[...])
        for j in range(A_WIDTH // LANES):
            qj = q[:, j * LANES:(j + 1) * LANES]
            if rope:
                qj = _rope(qj, cos, shi, slo)
            qt_ref[0, j * LANES:(j + 1) * LANES, rows] = (qj * (Q_SCALE * LOG2_E)).T.astype(BF16)
        kv = _dot(hb, wkv_ref[...])
        k = kv[:, 0:A_KV_WIDTH]
        if rope:
            k = _rope(k, cos, shi, slo)
        k_ref[0, rows] = k.astype(BF16)
        vt_ref[0, :, rows] = kv[:, A_KV_WIDTH:].T.astype(BF16)
        g = _dot(hb, wg_ref[...])
        sg_ref[0, rows] = _silu(g).astype(BF16)


def _inproj_attn(x, mod3, mod_row, norm_g, w_in, tables):
    b, s, _ = x.shape
    tm = min(ROW_TILE, s)
    rope = tables is not None
    if mod_row is None:
        mod_map = lambda bi, i: (bi, 0, 0)
    else:
        mod_map = lambda bi, i: (mod_row, 0, 0)
    in_specs = [
        pl.BlockSpec((1, tm, D), lambda bi, i: (bi, i, 0)),
        pl.BlockSpec((1, 1, 3 * D), mod_map),
        pl.BlockSpec((1, D), lambda bi, i: (0, 0)),
        pl.BlockSpec(w_in.shape, lambda bi, i: (0, 0), pipeline_mode=pl.Buffered(1)),
    ]
    args = [x, mod3, norm_g.reshape(1, D), w_in]
    if rope:
        in_specs += [pl.BlockSpec((tm, LANES), lambda bi, i: (i, 0))] * 3
        args += list(tables)
    out_shape = (
        jax.ShapeDtypeStruct((b, A_WIDTH, s), BF16),
        jax.ShapeDtypeStruct((b, s, A_KV_WIDTH), BF16),
        jax.ShapeDtypeStruct((b, A_KV_WIDTH, s), BF16),
        jax.ShapeDtypeStruct((b, s, A_WIDTH), BF16),
    )
    out_specs = (
        pl.BlockSpec((1, A_WIDTH, tm), lambda bi, i: (bi, 0, i)),
        pl.BlockSpec((1, tm, A_KV_WIDTH), lambda bi, i: (bi, i, 0)),
        pl.BlockSpec((1, A_KV_WIDTH, tm), lambda bi, i: (bi, 0, i)),
        pl.BlockSpec((1, tm, A_WIDTH), lambda bi, i: (bi, i, 0)),
    )
    return pl.pallas_call(
        functools.partial(_inproj_attn_kernel, rope),
        grid=(b, s // tm),
        in_specs=in_specs,
        out_specs=out_specs,
        out_shape=out_shape,
        scratch_shapes=[pltpu.VMEM(w_in.shape, BF16)],
        compiler_params=_params("arbitrary", "arbitrary"),
        name="inproj_attn_rope" if rope else "inproj_attn_ctx",
    )(*args)


def _attn_block(kwin, vtw, qts, sinks, ok_prev, ok_next):
    n_keys, nq = kwin.shape[0], qts.shape[1]
    lane = lax.broadcasted_iota(jnp.int32, kwin.shape, 1)
    kbd = jnp.concatenate([jnp.where(lane < A_HEAD_DIM, kwin, jnp.zeros_like(kwin)),
                           jnp.where(lane >= A_HEAD_DIM, kwin, jnp.zeros_like(kwin))], axis=0)
    ones = jnp.where(lax.broadcasted_iota(jnp.int32, (SUM_ROWS, n_keys), 0) == 0, 1.0, 0.0).astype(BF16)

    st = _dot(kbd, qts)
    outs = []
    for hh in range(2):
        sh = st[hh * n_keys:(hh + 1) * n_keys]
        if ok_prev is not None:
            parts = [jnp.where(ok_prev, sh[0:BLOCK], NEG_INF),
                     sh[BLOCK:2 * BLOCK],
                     jnp.where(ok_next, sh[2 * BLOCK:3 * BLOCK], NEG_INF),
                     sh[3 * BLOCK:]]
        else:
            parts = [sh]
        mx = functools.reduce(jnp.maximum, [jnp.max(t, axis=0, keepdims=True) for t in parts])
        mx = jnp.maximum(mx, sinks[hh])
        probs = jnp.concatenate([jnp.exp2(t - mx).astype(BF16) for t in parts], axis=0)
        vt_h = jnp.concatenate([vtw[hh * A_HEAD_DIM:(hh + 1) * A_HEAD_DIM], ones], axis=0)
        ot = _dot(vt_h, probs)
        den = ot[A_HEAD_DIM:A_HEAD_DIM + 1] + jnp.exp2(sinks[hh] - mx)
        outs.append(ot[0:A_HEAD_DIM] * (1.0 / den))
    return jnp.concatenate(outs, axis=0)


def _pair_cast_rows(src_ref, dst_ref):
    @pl.when((pl.program_id(0) == 0) & (pl.program_id(1) == 0))
    def _():
        for p in range(A_GROUP):
            for half, h in enumerate((p, p + A_GROUP)):
                lo = p * LANES + half * A_HEAD_DIM
                dst_ref[lo:lo + A_HEAD_DIM, :] = src_ref[h * A_HEAD_DIM:(h + 1) * A_HEAD_DIM, :].astype(BF16)


def _attn_kernel(local, n_steps, sink_ref, qt_ref, *rest):
    *rest, w_ref = rest
    n_blk = rest[-1].shape[1] // BLOCK
    if local:
        (kp_ref, kc_ref, kn_ref, vp_ref, vc_ref, vn_ref, kx_ref, vx_ref,
         sg_ref, x_ref, m_ref, w32_ref, o_ref) = rest
        last = slice((n_blk - 1) * BLOCK, n_blk * BLOCK)
        k_blocks = ([kp_ref[0, last]] + [kc_ref[0, i * BLOCK:(i + 1) * BLOCK] for i in range(n_blk)]
                    + [kn_ref[0, 0:BLOCK]])
        v_blocks = ([vp_ref[0, :, last]] + [vc_ref[0, :, i * BLOCK:(i + 1) * BLOCK] for i in range(n_blk)]
                    + [vn_ref[0, :, 0:BLOCK]])
    else:
        kx_ref, vx_ref, sg_ref, x_ref, m_ref, w32_ref, o_ref = rest
    _pair_cast_rows(w32_ref, w_ref)
    step = pl.program_id(1)
    nq = A_GROUP * BLOCK
    chunk_of = lax.broadcasted_iota(jnp.int32, (1, nq), 1) // BLOCK
    sinks = []
    for hh in range(2):
        sk = jnp.full((1, nq), sink_ref[hh] * LOG2_E, F32)
        for c in range(1, A_GROUP):
            sk = jnp.where(chunk_of == c, sink_ref[2 * c + hh] * LOG2_E, sk)
        sinks.append(sk)
    if local:
        kj = lax.broadcasted_iota(jnp.int32, (BLOCK, nq), 0)
        qi = lax.broadcasted_iota(jnp.int32, (BLOCK, nq), 1) % BLOCK

    rows_out = []
    for blk in range(n_blk):
        qts = jnp.concatenate([qt_ref[0, c * LANES:(c + 1) * LANES, blk * BLOCK:(blk + 1) * BLOCK]
                               for c in range(A_GROUP)], axis=1)
        if local:
            kwin = jnp.concatenate(k_blocks[blk:blk + 3] + [kx_ref[0]], axis=0)
            vtw = jnp.concatenate(v_blocks[blk:blk + 3] + [vx_ref[0]], axis=1)
            ok_prev = (kj >= qi) & (step > 0) if blk == 0 else (kj >= qi)
            ok_next = (kj <= qi) & (step < n_steps - 1) if blk == n_blk - 1 else (kj <= qi)
        else:
            kwin, vtw, ok_prev, ok_next = kx_ref[0], vx_ref[0], None, None
        ot = _attn_block(kwin, vtw, qts, sinks, ok_prev, ok_next)
        rows = slice(blk * BLOCK, (blk + 1) * BLOCK)
        outs = []
        for c in range(A_GROUP):
            sg = sg_ref[0, rows, c * LANES:(c + 1) * LANES].astype(F32)
            outs.append((ot[:, c * BLOCK:(c + 1) * BLOCK].T * sg).astype(BF16))
        rows_out.append(jnp.concatenate(outs, axis=1))
    y = _dot(jnp.concatenate(rows_out, axis=0), w_ref[...])
    gate = m_ref[0][:, 2 * D:3 * D]
    o_ref[0] = x_ref[0] + gate * y


def _attention(sink, qt, k, vt, kx, vxt, sg, x, mod3, mod_row, w_out, local):
    b, s, _ = sg.shape
    rows = min(A_BLOCKS_PER_STEP * BLOCK, s)
    assert s % rows == 0
    ns = s // rows
    n_ctx = kx.shape[1]
    blk = lambda w: pl.BlockSpec((1, rows, w), lambda bi, i: (bi, i, 0))
    blk_t = lambda w: pl.BlockSpec((1, w, rows), lambda bi, i: (bi, 0, i))
    if mod_row is None:
        mod_map = lambda bi, i: (bi, 0, 0)
    else:
        mod_map = lambda bi, i: (mod_row, 0, 0)
    in_specs = [pl.BlockSpec(memory_space=pltpu.SMEM), blk_t(A_WIDTH)]
    args = [sink, qt]
    if local:
        lo = lambda i: jnp.maximum(i - 1, 0)
        hi = lambda i: jnp.minimum(i + 1, ns - 1)
        in_specs += [pl.BlockSpec((1, rows, A_KV_WIDTH), lambda bi, i: (bi, lo(i), 0)),
                     blk(A_KV_WIDTH),
                     pl.BlockSpec((1, rows, A_KV_WIDTH), lambda bi, i: (bi, hi(i), 0)),
                     pl.BlockSpec((1, A_KV_WIDTH, rows), lambda bi, i: (bi, 0, lo(i))),
                     blk_t(A_KV_WIDTH),
                     pl.BlockSpec((1, A_KV_WIDTH, rows), lambda bi, i: (bi, 0, hi(i)))]
        args += [k, k, k, vt, vt, vt]
    in_specs += [pl.BlockSpec((1, n_ctx, A_KV_WIDTH), lambda bi, i: (bi, 0, 0)),
                 pl.BlockSpec((1, A_KV_WIDTH, n_ctx), lambda bi, i: (bi, 0, 0)),
                 blk(A_WIDTH), blk(D),
                 pl.BlockSpec((1, 1, 3 * D), mod_map),
                 pl.BlockSpec((A_WIDTH, D), lambda bi, i: (0, 0), pipeline_mode=pl.Buffered(1))]
    args += [kx, vxt, sg, x, mod3, w_out]
    return pl.pallas_call(
        functools.partial(_attn_kernel, local, ns),
        grid=(b, ns),
        in_specs=in_specs,
        out_specs=blk(D),
        out_shape=jax.ShapeDtypeStruct((b, s, D), F32),
        scratch_shapes=[pltpu.VMEM((A_WIDTH, D), BF16)],
        compiler_params=_params("arbitrary", "arbitrary"),
        name="attn_local" if local else "attn_ctx",
    )(*args)


def _chunk_cumsum(x, tri):
    hi = x.astype(BF16)
    lo = (x - hi.astype(F32)).astype(BF16)
    return _dot(tri, hi) + _dot(tri, lo)


def _inproj_gla_kernel(chunk, with_q, x_ref, m_ref, g_ref, w32_ref, wa1_ref, wa2_ref, ba_ref, tri_ref, *outs):
    *outs, w_ref = outs
    _cast_once(w32_ref, w_ref)
    if with_q:
        per_dir = (outs[0:3], outs[3:6])
        v_ref, sg_ref = outs[6:]
    else:
        per_dir = ((outs[0],), (outs[1],))
        v_ref = outs[2]
    tm = x_ref.shape[1]
    hb = _norm_mod(x_ref[0], g_ref[...], m_ref[0]).astype(BF16)
    r = _dot(hb, wa1_ref[...])
    z = _dot(r.astype(BF16), wa2_ref[...]) + ba_ref[...]
    e = jnp.exp2(jnp.abs(z) * -LOG2_E)
    la = (jnp.minimum(z, 0.0) * LOG2_E - jnp.log2(1.0 + e)) * (1.0 / GATE_TEMP)
    k = _dot(hb, w_ref[:, B_K_WIDTH:2 * B_K_WIDTH])
    if with_q:
        q = _dot(hb, w_ref[:, 0:B_K_WIDTH]) * K_SCALE
    n_chunks = tm // chunk
    for reverse in (False, True):
        lad = la[:, B_K_WIDTH:] if reverse else la[:, 0:B_K_WIDTH]
        refs = per_dir[1] if reverse else per_dir[0]
        tots, kis = [], []
        tri = tri_ref[1 if reverse else 0]
        tb = tri.shape[0]
        cum_all = jnp.concatenate([_chunk_cumsum(lad[r * tb:(r + 1) * tb], tri) for r in range(tm // tb)], axis=0)
        for c in range(n_chunks):
            rows = slice(c * chunk, (c + 1) * chunk)
            cum = cum_all[rows]
            tot = cum[0:1] if reverse else cum[chunk - 1:chunk]
            if not with_q:
                refs[0][0, rows] = (k[rows] * jnp.exp2(tot - cum)).astype(BF16)
                continue
            refs[0][0, rows] = (q[rows] * jnp.exp2(cum)).astype(BF16)
            kis.append(k[rows] * jnp.exp2(-cum))
            tots.append(tot)
            if c % 2 == 1:
                pair = slice((c - 1) * chunk, (c + 1) * chunk)
                refs[1][0, :, pair] = jnp.concatenate(kis[-2:], axis=0).T.astype(BF16)
        if with_q:
            pad = [jnp.zeros((DEC_ROWS - n_chunks, B_K_WIDTH), F32)] if n_chunks < DEC_ROWS else []
            refs[2][0, 0] = jnp.exp2(jnp.concatenate(tots + pad, axis=0))
    if with_q:
        sg_ref[0] = _silu(_dot(hb, w_ref[:, 2 * B_K_WIDTH + B_V_WIDTH:])).astype(BF16)
    v_ref[0] = _dot(hb, w_ref[:, 2 * B_K_WIDTH:2 * B_K_WIDTH + B_V_WIDTH]).astype(BF16)


def _inproj_gla(x, mod3, mod_row, norm_g, w, wa1, wa2, ba, chunk, with_q):
    b, s, _ = x.shape
    tm = min(ROW_TILE, s)
    nt = s // tm
    cpt = tm // chunk
    tb = max(chunk, MXU_DIM)
    assert tm % tb == 0 and tb % chunk == 0
    if mod_row is None:
        mod_map = lambda bi, i: (bi, 0, 0)
    else:
        mod_map = lambda bi, i: (mod_row, 0, 0)
    const = lambda shape: pl.BlockSpec(shape, lambda bi, i: (0,) * len(shape))
    in_specs = [
        pl.BlockSpec((1, tm, D), lambda bi, i: (bi, i, 0)),
        pl.BlockSpec((1, 1, 3 * D), mod_map),
        const((1, D)), pl.BlockSpec(w.shape, lambda bi, i: (0, 0), pipeline_mode=pl.Buffered(1)),
        const(wa1.shape), const(wa2.shape), const(ba.shape),
        const((2, tb, tb)),
    ]
    t_idx = np.arange(tb)
    same = (t_idx[:, None] // chunk) == (t_idx[None, :] // chunk)
    lower = same & (t_idx[None, :] <= t_idx[:, None])
    tri = jnp.asarray(np.stack([lower, lower.T]), dtype=BF16)
    rows = lambda w_: (jax.ShapeDtypeStruct((b, s, w_), BF16),
                       pl.BlockSpec((1, tm, w_), lambda bi, i: (bi, i, 0)))
    cols = (jax.ShapeDtypeStruct((b, B_K_WIDTH, s), BF16),
            pl.BlockSpec((1, B_K_WIDTH, tm), lambda bi, i: (bi, 0, i)))
    decs = (jax.ShapeDtypeStruct((b, nt, DEC_ROWS, B_K_WIDTH), F32),
            pl.BlockSpec((1, 1, DEC_ROWS, B_K_WIDTH), lambda bi, i: (bi, i, 0, 0)))
    if with_q:
        assert cpt <= DEC_ROWS and cpt % 2 == 0
        outs = [rows(B_K_WIDTH), cols, decs] * 2 + [rows(B_V_WIDTH)] * 2
    else:
        outs = [rows(B_K_WIDTH)] * 2 + [rows(B_V_WIDTH)]
    return pl.pallas_call(
        functools.partial(_inproj_gla_kernel, chunk, with_q),
        grid=(b, nt),
        in_specs=in_specs,
        out_specs=tuple(o[1] for o in outs),
        out_shape=tuple(o[0] for o in outs),
        scratch_shapes=[pltpu.VMEM(w.shape, BF16)],
        compiler_params=_params("arbitrary", "arbitrary"),
        name="inproj_gla" if with_q else "inproj_gla_ctx",
    )(x, mod3, norm_g.reshape(1, D), w, wa1, wa2, ba, tri)


def _gla_state_kernel(kf_ref, kb_ref, v_ref, sf_ref, sb_ref):
    for h in range(B_HEADS):
        vh = v_ref[0, :, h * B_VAL_DIM:(h + 1) * B_VAL_DIM]
        ks = slice(h * B_KEY_DIM, (h + 1) * B_KEY_DIM)
        sf_ref[0, h] = _dot_tn(kf_ref[0, :, ks], vh)
        sb_ref[0, h] = _dot_tn(kb_ref[0, :, ks], vh)


def _gla_state(ke_f, ke_b, v):
    b, n, _ = v.shape
    st = jax.ShapeDtypeStruct((b, B_HEADS, B_KEY_DIM, B_VAL_DIM), F32)
    st_spec = pl.BlockSpec((1, B_HEADS, B_KEY_DIM, B_VAL_DIM), lambda bi: (bi, 0, 0, 0))
    return pl.pallas_call(
        _gla_state_kernel,
        grid=(b,),
        in_specs=[pl.BlockSpec((1, n, B_K_WIDTH), lambda bi: (bi, 0, 0))] * 2
        + [pl.BlockSpec((1, n, B_V_WIDTH), lambda bi: (bi, 0, 0))],
        out_specs=(st_spec, st_spec),
        out_shape=(st, st),
        compiler_params=_params("parallel"),
        name="gla_ctx_state",
    )(ke_f, ke_b, v)


def _scan_tile(reverse, qd_ref, kit_ref, dec_ref, v_ref, st_ref):
    pair_rows = 2 * CHUNK
    ti = lax.broadcasted_iota(jnp.int32, (pair_rows, pair_rows), 0)
    si = lax.broadcasted_iota(jnp.int32, (pair_rows, pair_rows), 1)
    same = (ti // CHUNK) == (si // CHUNK)
    if reverse:
        use_inv = same & (ti <= si)
        use_end = (ti < CHUNK) & (si >= CHUNK)
        second_row = lax.broadcasted_iota(jnp.int32, (pair_rows, B_KEY_DIM), 0) < CHUNK
        first_col = lax.broadcasted_iota(jnp.int32, (B_KEY_DIM, pair_rows), 1) >= CHUNK
    else:
        use_inv = same & (ti >= si)
        use_end = (ti >= CHUNK) & (si < CHUNK)
        second_row = lax.broadcasted_iota(jnp.int32, (pair_rows, B_KEY_DIM), 0) >= CHUNK
        first_col = lax.broadcasted_iota(jnp.int32, (B_KEY_DIM, pair_rows), 1) < CHUNK
    n_pairs = SCAN_TILE // pair_rows
    order = range(n_pairs - 1, -1, -1) if reverse else range(n_pairs)
    n_dec = dec_ref.shape[1] * DEC_ROWS
    dec = dec_ref[0].reshape(n_dec, B_K_WIDTH)
    dect = jnp.concatenate([dec, jnp.zeros((LANES - n_dec, B_K_WIDTH), F32)], axis=0).T
    heads = []
    for h in range(B_HEADS):
        ks = slice(h * B_KEY_DIM, (h + 1) * B_KEY_DIM)
        vs = slice(h * B_VAL_DIM, (h + 1) * B_VAL_DIM)
        st = st_ref[h]
        o_rows = [None] * n_pairs
        for p in order:
            rows = slice(p * pair_rows, (p + 1) * pair_rows)
            c_first, c_second = (2 * p + 1, 2 * p) if reverse else (2 * p, 2 * p + 1)
            qd = qd_ref[0, rows, ks]
            kit = kit_ref[0, ks, rows]
            vh = v_ref[0, rows, vs]
            d_first, d_second = dect[ks, c_first:c_first + 1], dect[ks, c_second:c_second + 1]
            ki32 = kit.astype(F32)
            s2 = _dot(qd, jnp.concatenate([kit, (ki32 * d_first).astype(BF16)], axis=1))
            a = jnp.where(use_inv, s2[:, 0:pair_rows], jnp.where(use_end, s2[:, pair_rows:], 0.0)).astype(BF16)
            q_pair = jnp.where(second_row, qd.astype(F32) * dec[c_first:c_first + 1, ks], qd.astype(F32))
            k_pair = ki32 * jnp.where(first_col, d_first * d_second, d_second)
            o_rows[p] = _dot(a, vh) + _dot(q_pair.astype(BF16), st.astype(BF16))
            st = st * (d_first * d_second) + _dot(k_pair.astype(BF16), vh)
        st_ref[h] = st
        heads.append(jnp.concatenate(o_rows, axis=0))
    return jnp.concatenate(heads, axis=1)


def _gla_scan_kernel(nt, qdf_ref, kitf_ref, decf_ref, qdb_ref, kitb_ref, decb_ref,
                     v_ref, sf_ref, sb_ref, sg_ref, x_ref, m_ref, hn_ref, w_ref, fn_ref,
                     o_ref, st_ref, of_ref):
    j = pl.program_id(1)

    @pl.when(j == 0)
    def _():
        st_ref[...] = sf_ref[0]

    @pl.when(j == nt)
    def _():
        st_ref[...] = sb_ref[0]

    @pl.when(j < nt)
    def _():
        of_ref[j] = _scan_tile(False, qdf_ref, kitf_ref, decf_ref, v_ref, st_ref)

    @pl.when(j >= nt)
    def _():
        o = _scan_tile(True, qdb_ref, kitb_ref, decb_ref, v_ref, st_ref) + of_ref[2 * nt - 1 - j]
        normed = []
        for h in range(B_HEADS):
            oh = o[:, h * B_VAL_DIM:(h + 1) * B_VAL_DIM]
            normed.append(oh * lax.rsqrt(jnp.mean(oh * oh, axis=-1, keepdims=True) + EPS))
        of = jnp.concatenate(normed, axis=1) * hn_ref[...]
        y = _dot((of * sg_ref[0].astype(F32)).astype(BF16), w_ref[...])
        xn = x_ref[0] + m_ref[0][:, 2 * D:3 * D] * y
        ms = jnp.mean(xn * xn, axis=-1, keepdims=True)
        o_ref[0] = xn * lax.rsqrt(ms + EPS) * fn_ref[...]


def _gla_scan(fwd, bwd, v, s_f, s_b, sg, x, mod3, head_g, w_out, final_g):
    b, s, _ = v.shape
    assert SCAN_TILE % ROW_TILE == 0 and ROW_TILE // CHUNK == DEC_ROWS
    dec_blocks = SCAN_TILE // ROW_TILE
    nt = s // SCAN_TILE
    t_fwd = lambda j: jnp.minimum(j, nt - 1)
    t_bwd = lambda j: 2 * nt - 1 - jnp.maximum(j, nt)
    t_both = lambda j: jnp.where(j < nt, j, 2 * nt - 1 - j)

    def direction(t):
        rows = pl.BlockSpec((1, SCAN_TILE, B_K_WIDTH), lambda bi, j: (bi, t(j), 0))
        cols = pl.BlockSpec((1, B_K_WIDTH, SCAN_TILE), lambda bi, j: (bi, 0, t(j)))
        decs = pl.BlockSpec((1, dec_blocks, DEC_ROWS, B_K_WIDTH), lambda bi, j: (bi, t(j), 0, 0))
        return [rows, cols, decs]

    const = lambda shape: pl.BlockSpec(shape, lambda bi, j: (0,) * len(shape))
    state = pl.BlockSpec((1, B_HEADS, B_KEY_DIM, B_VAL_DIM), lambda bi, j: (bi, 0, 0, 0))
    out_rows = lambda w_: pl.BlockSpec((1, SCAN_TILE, w_), lambda bi, j: (bi, t_bwd(j), 0))
    in_specs = direction(t_fwd) + direction(t_bwd) + [
        pl.BlockSpec((1, SCAN_TILE, B_V_WIDTH), lambda bi, j: (bi, t_both(j), 0)),
        state, state, out_rows(B_V_WIDTH), out_rows(D),
        pl.BlockSpec((1, 1, 3 * D), lambda bi, j: (bi, 0, 0)),
        const((1, B_V_WIDTH)), const((B_V_WIDTH, D)), const((1, D))]
    return pl.pallas_call(
        functools.partial(_gla_scan_kernel, nt),
        grid=(b, 2 * nt),
        in_specs=in_specs,
        out_specs=out_rows(D),
        out_shape=jax.ShapeDtypeStruct((b, s, D), F32),
        scratch_shapes=[pltpu.VMEM((B_HEADS, B_KEY_DIM, B_VAL_DIM), F32),
                        pltpu.VMEM((nt, SCAN_TILE, B_V_WIDTH), F32)],
        compiler_params=_params("parallel", "arbitrary", vmem_limit=SCAN_VMEM_LIMIT),
        name="gla_scan",
    )(*fwd, *bwd, v, s_f, s_b, sg, x, mod3, head_g.reshape(1, B_V_WIDTH), w_out, final_g.reshape(1, D))


def _rope_tables(n_tokens):
    rows_n = n_tokens // GRID_W
    row = np.repeat(np.arange(rows_n, dtype=np.float32), GRID_W)
    col = np.tile(np.arange(GRID_W, dtype=np.float32), rows_n)
    inv_freq = (np.float32(ROPE_BASE) ** (-np.arange(ROPE_FREQS, dtype=np.float32) / np.float32(ROPE_FREQS)))
    inv_freq = inv_freq.astype(np.float32)
    ang = np.stack([row[:, None] * inv_freq, col[:, None] * inv_freq], axis=1)
    cos, sin = np.cos(ang).astype(np.float32), np.sin(ang).astype(np.float32)
    zero = np.zeros_like(sin)
    tile = lambda t: np.tile(t.reshape(n_tokens, A_HEAD_DIM), (1, LANES // A_HEAD_DIM))
    return (tile(np.stack([cos, cos], axis=2)),
            tile(np.stack([-sin, zero], axis=2)),
            tile(np.stack([zero, sin], axis=2)))


def _pair_heads(t, axis):
    shape = t.shape
    t = t.reshape(shape[:axis] + (A_KV_HEADS, A_GROUP, -1) + shape[axis + 1:])
    return jnp.swapaxes(t, axis, axis + 1).reshape(shape)


def kernel(x, c, ctx, c_ctx, l0_norm_g, l0_w_ada, l0_b_ada, l0_w_in, l0_sink, l0_w_out, l1_norm_g, l1_w_ada, l1_b_ada, l1_w_in, l1_wa1_f, l1_wa2_f, l1_ba_f, l1_wa1_b, l1_wa2_b, l1_ba_b, l1_head_norm_g, l1_w_out, final_norm_g):
    b, s, _ = x.shape
    ctx_row = b

    cvec = jnp.concatenate([c, c_ctx[None, :], jnp.zeros((MOD_ROWS - b - 1, D), F32)], axis=0)
    mod0 = _modulation(cvec, l0_w_ada, l0_b_ada).reshape(MOD_ROWS, 1, 3 * D)
    mod1 = _modulation(cvec, l1_w_ada, l1_b_ada).reshape(MOD_ROWS, 1, 3 * D)

    sink = _pair_heads(l0_sink.astype(F32), 0)
    qt, k, vt, sg = _inproj_attn(x, mod0, None, l0_norm_g, l0_w_in, _rope_tables(s))
    qct, kc, vct, sgc = _inproj_attn(ctx, mod0, ctx_row, l0_norm_g, l0_w_in, None)
    x1 = _attention(sink, qt, k, vt, kc, vct, sg, x, mod0, None, l0_w_out, True)
    xc1 = _attention(sink, qct, None, None, kc, vct, sgc, ctx, mod0, ctx_row, l0_w_out, False)

    w1 = l1_w_in
    wa1 = jnp.concatenate([l1_wa1_f, l1_wa1_b, jnp.zeros((D, LANES - 2 * GATE_RANK), F32)], axis=1).astype(BF16)
    wa2 = jnp.zeros((LANES, 2 * B_K_WIDTH), F32)
    wa2 = wa2.at[0:GATE_RANK, 0:B_K_WIDTH].set(l1_wa2_f)
    wa2 = wa2.at[GATE_RANK:2 * GATE_RANK, B_K_WIDTH:].set(l1_wa2_b).astype(BF16)
    ba = jnp.concatenate([l1_ba_f, l1_ba_b]).reshape(1, 2 * B_K_WIDTH)
    kec_f, kec_b, vcx = _inproj_gla(xc1, mod1, ctx_row, l1_norm_g, w1, wa1, wa2, ba, ctx.shape[1], False)
    s_f, s_b = _gla_state(kec_f, kec_b, vcx)
    outs = _inproj_gla(x1, mod1, None, l1_norm_g, w1, wa1, wa2, ba, CHUNK, True)
    v1, sg1 = outs[6:]
    return _gla_scan(outs[0:3], outs[3:6], v1, s_f, s_b, sg1, x1, mod1,
                     l1_head_norm_g, l1_w_out.astype(BF16), final_norm_g)
```

```python
import functools

import jax
import jax.numpy as jnp
import numpy as np
from jax import lax
from jax.experimental import pallas as pl
from jax.experimental.pallas import tpu as pltpu

F32 = jnp.float32
BF16 = jnp.bfloat16

D = 1024
GRID_W = 64
EPS = 1e-6
NEG_INF = -1e30

A_HEADS = 16
A_KV_HEADS = 2
A_GROUP = A_HEADS // A_KV_HEADS
A_HEAD_DIM = 64
A_WIDTH = A_HEADS * A_HEAD_DIM
A_KV_WIDTH = A_KV_HEADS * A_HEAD_DIM
BLOCK = 128
ROPE_BASE = 10000.0
ROPE_FREQS = A_HEAD_DIM // 4
Q_SCALE = A_HEAD_DIM ** -0.5
LOG2_E = 1.4426950408889634

B_HEADS = 4
B_K_WIDTH = D // 2
B_V_WIDTH = D
B_KEY_DIM = B_K_WIDTH // B_HEADS
B_VAL_DIM = B_V_WIDTH // B_HEADS
GATE_RANK = 16
GATE_TEMP = 16.0
CHUNK = 64
K_SCALE = B_KEY_DIM ** -0.5

LANES = 128
MXU_DIM = 256
MOD_ROWS = 16
ROW_TILE = 512
SCAN_TILE = 1024
SUB_ROWS = 256
DEC_ROWS = 8
SUM_ROWS = 16
A_BLOCKS_PER_STEP = 4
VMEM_LIMIT = 48 * 1024 * 1024
SCAN_VMEM_LIMIT = 56 * 1024 * 1024


def _params(*sem, vmem_limit=VMEM_LIMIT):
    return pltpu.CompilerParams(dimension_semantics=sem, vmem_limit_bytes=vmem_limit)


def _silu(x):
    return x / (1.0 + jnp.exp(-x))


def _dot(a, b):
    return jnp.dot(a, b, preferred_element_type=F32)


def _dot_nt(a, b):
    return lax.dot_general(a, b, (((1,), (1,)), ((), ())), preferred_element_type=F32)


def _dot_tn(a, b):
    return lax.dot_general(a, b, (((0,), (0,)), ((), ())), preferred_element_type=F32)


def _norm_mod(x, g, m):
    ms = jnp.mean(x * x, axis=-1, keepdims=True)
    y = x * lax.rsqrt(ms + EPS) * g
    return y * (1.0 + m[:, D:2 * D]) + m[:, 0:D]


def _cast_once(src_ref, dst_ref):
    @pl.when((pl.program_id(0) == 0) & (pl.program_id(1) == 0))
    def _():
        def body(i, carry):
            rows = pl.ds(pl.multiple_of(i * LANES, LANES), LANES)
            dst_ref[rows, :] = src_ref[rows, :].astype(dst_ref.dtype)
            return carry
        lax.fori_loop(0, src_ref.shape[0] // LANES, body, 0)


def _mod_kernel(c_ref, w_ref, b_ref, o_ref):
    s = _silu(c_ref[...])
    o_ref[...] = _dot(s.astype(BF16), w_ref[...].astype(BF16)) + b_ref[...]


def _modulation(cvec, w_ada, b_ada):
    n = w_ada.shape[1] // D
    return pl.pallas_call(
        _mod_kernel,
        grid=(n,),
        in_specs=[
            pl.BlockSpec((MOD_ROWS, D), lambda j: (0, 0)),
            pl.BlockSpec((D, D), lambda j: (0, j)),
            pl.BlockSpec((1, D), lambda j: (0, j)),
        ],
        out_specs=pl.BlockSpec((MOD_ROWS, D), lambda j: (0, j)),
        out_shape=jax.ShapeDtypeStruct((MOD_ROWS, n * D), F32),
        compiler_params=_params("arbitrary"),
        name="modulation",
    )(cvec, w_ada, b_ada.reshape(1, n * D))


def _rope(t, cos, sin_hi, sin_lo):
    return t * cos + pltpu.roll(t, LANES - ROPE_FREQS, 1) * sin_hi + pltpu.roll(t, ROPE_FREQS, 1) * sin_lo


def _pair_cast_columns(src_ref, dst_ref):
    @pl.when((pl.program_id(0) == 0) & (pl.program_id(1) == 0))
    def _():
        kv_lo, kv_hi = A_WIDTH, A_WIDTH + 2 * A_KV_WIDTH

        def body(i, carry):
            rows = pl.ds(pl.multiple_of(i * LANES, LANES), LANES)
            src = src_ref[rows, :]
            dst_ref[rows, kv_lo:kv_hi] = src[:, kv_lo:kv_hi].astype(BF16)
            for base in (0, kv_hi):
                for p in range(A_GROUP):
                    a = base + p * A_HEAD_DIM
                    b = base + (p + A_GROUP) * A_HEAD_DIM
                    pair = jnp.concatenate([src[:, a:a + A_HEAD_DIM], src[:, b:b + A_HEAD_DIM]], axis=1)
                    dst_ref[rows, base + p * LANES:base + (p + 1) * LANES] = pair.astype(BF16)
            return carry
        lax.fori_loop(0, src_ref.shape[0] // LANES, body, 0)


def _inproj_attn_kernel(rope, x_ref, m_ref, g_ref, w32_ref, *rest):
    *rest, w_ref = rest
    _pair_cast_columns(w32_ref, w_ref)
    wq_ref = w_ref.at[:, 0:A_WIDTH]
    wkv_ref = w_ref.at[:, A_WIDTH:A_WIDTH + 2 * A_KV_WIDTH]
    wg_ref = w_ref.at[:, A_WIDTH + 2 * A_KV_WIDTH:]
    if rope:
        cos_ref, shi_ref, slo_ref, qt_ref, k_ref, vt_ref, sg_ref = rest
    else:
        qt_ref, k_ref, vt_ref, sg_ref = rest
    for r in range(x_ref.shape[1] // SUB_ROWS):
        rows = slice(r * SUB_ROWS, (r + 1) * SUB_ROWS)
        if rope:
            cos, shi, slo = cos_ref[rows], shi_ref[rows], slo_ref[rows]
        hb = _norm_mod(x_ref[0, rows], g_ref[...], m_ref[0]).astype(BF16)
        q = _dot(hb, wq_ref[...])
        for j in range(A_WIDTH // LANES):
            qj = q[:, j * LANES:(j + 1) * LANES]
            if rope:
                qj = _rope(qj, cos, shi, slo)
            qt_ref[0, j * LANES:(j + 1) * LANES, rows] = (qj * (Q_SCALE * LOG2_E)).T.astype(BF16)
        kv = _dot(hb, wkv_ref[...])
        k = kv[:, 0:A_KV_WIDTH]
        if rope:
            k = _rope(k, cos, shi, slo)
        k_ref[0, rows] = k.astype(BF16)
        vt_ref[0, :, rows] = kv[:, A_KV_WIDTH:].T.astype(BF16)
        g = _dot(hb, wg_ref[...])
        sg_ref[0, rows] = _silu(g).astype(BF16)


def _inproj_attn(x, mod3, mod_row, norm_g, w_in, tables):
    b, s, _ = x.shape
    tm = min(ROW_TILE, s)
    rope = tables is not None
    if mod_row is None:
        mod_map = lambda bi, i: (bi, 0, 0)
    else:
        mod_map = lambda bi, i: (mod_row, 0, 0)
    in_specs = [
        pl.BlockSpec((1, tm, D), lambda bi, i: (bi, i, 0)),
        pl.BlockSpec((1, 1, 3 * D), mod_map),
        pl.BlockSpec((1, D), lambda bi, i: (0, 0)),
        pl.BlockSpec(w_in.shape, lambda bi, i: (0, 0), pipeline_mode=pl.Buffered(1)),
    ]
    args = [x, mod3, norm_g.reshape(1, D), w_in]
    if rope:
        in_specs += [pl.BlockSpec((tm, LANES), lambda bi, i: (i, 0))] * 3
        args += list(tables)
    out_shape = (
        jax.ShapeDtypeStruct((b, A_WIDTH, s), BF16),
        jax.ShapeDtypeStruct((b, s, A_KV_WIDTH), BF16),
        jax.ShapeDtypeStruct((b, A_KV_WIDTH, s), BF16),
        jax.ShapeDtypeStruct((b, s, A_WIDTH), BF16),
    )
    out_specs = (
        pl.BlockSpec((1, A_WIDTH, tm), lambda bi, i: (bi, 0, i)),
        pl.BlockSpec((1, tm, A_KV_WIDTH), lambda bi, i: (bi, i, 0)),
        pl.BlockSpec((1, A_KV_WIDTH, tm), lambda bi, i: (bi, 0, i)),
        pl.BlockSpec((1, tm, A_WIDTH), lambda bi, i: (bi, i, 0)),
    )
    return pl.pallas_call(
        functools.partial(_inproj_attn_kernel, rope),
        grid=(b, s // tm),
        in_specs=in_specs,
        out_specs=out_specs,
        out_shape=out_shape,
        scratch_shapes=[pltpu.VMEM(w_in.shape, BF16)],
        compiler_params=_params("arbitrary", "arbitrary"),
        name="inproj_attn_rope" if rope else "inproj_attn_ctx",
    )(*args)


def _attn_block(kwin, vtw, qts, sinks, ok_prev, ok_next):
    n_keys, nq = kwin.shape[0], qts.shape[1]
    lane = lax.broadcasted_iota(jnp.int32, kwin.shape, 1)
    kbd = jnp.concatenate([jnp.where(lane < A_HEAD_DIM, kwin, jnp.zeros_like(kwin)),
                           jnp.where(lane >= A_HEAD_DIM, kwin, jnp.zeros_like(kwin))], axis=0)
    ones = jnp.where(lax.broadcasted_iota(jnp.int32, (SUM_ROWS, n_keys), 0) == 0, 1.0, 0.0).astype(BF16)

    st = _dot(kbd, qts)
    outs = []
    for hh in range(2):
        sh = st[hh * n_keys:(hh + 1) * n_keys]
        if ok_prev is not None:
            parts = [jnp.where(ok_prev, sh[0:BLOCK], NEG_INF),
                     sh[BLOCK:2 * BLOCK],
                     jnp.where(ok_next, sh[2 * BLOCK:3 * BLOCK], NEG_INF),
                     sh[3 * BLOCK:]]
        else:
            parts = [sh]
        mx = functools.reduce(jnp.maximum, [jnp.max(t, axis=0, keepdims=True) for t in parts])
        mx = jnp.maximum(mx, sinks[hh])
        probs = jnp.concatenate([jnp.exp2(t - mx).astype(BF16) for t in parts], axis=0)
        vt_h = jnp.concatenate([vtw[hh * A_HEAD_DIM:(hh + 1) * A_HEAD_DIM], ones], axis=0)
        ot = _dot(vt_h, probs)
        den = ot[A_HEAD_DIM:A_HEAD_DIM + 1] + jnp.exp2(sinks[hh] - mx)
        outs.append(ot[0:A_HEAD_DIM] * (1.0 / den))
    return jnp.concatenate(outs, axis=0)


def _pair_cast_rows(src_ref, dst_ref):
    @pl.when((pl.program_id(0) == 0) & (pl.program_id(1) == 0))
    def _():
        for p in range(A_GROUP):
            for half, h in enumerate((p, p + A_GROUP)):
                lo = p * LANES + half * A_HEAD_DIM
                dst_ref[lo:lo + A_HEAD_DIM, :] = src_ref[h * A_HEAD_DIM:(h + 1) * A_HEAD_DIM, :].astype(BF16)


def _attn_kernel(local, n_steps, sink_ref, qt_ref, *rest):
    *rest, w_ref = rest
    n_blk = rest[-1].shape[1] // BLOCK
    if local:
        (kp_ref, kc_ref, kn_ref, vp_ref, vc_ref, vn_ref, kx_ref, vx_ref,
         sg_ref, x_ref, m_ref, w32_ref, o_ref) = rest
        last = slice((n_blk - 1) * BLOCK, n_blk * BLOCK)
        k_blocks = ([kp_ref[0, last]] + [kc_ref[0, i * BLOCK:(i + 1) * BLOCK] for i in range(n_blk)]
                    + [kn_ref[0, 0:BLOCK]])
        v_blocks = ([vp_ref[0, :, last]] + [vc_ref[0, :, i * BLOCK:(i + 1) * BLOCK] for i in range(n_blk)]
                    + [vn_ref[0, :, 0:BLOCK]])
    else:
        kx_ref, vx_ref, sg_ref, x_ref, m_ref, w32_ref, o_ref = rest
    _pair_cast_rows(w32_ref, w_ref)
    step = pl.program_id(1)
    nq = A_GROUP * BLOCK
    chunk_of = lax.broadcasted_iota(jnp.int32, (1, nq), 1) // BLOCK
    sinks = []
    for hh in range(2):
        sk = jnp.full((1, nq), sink_ref[hh] * LOG2_E, F32)
        for c in range(1, A_GROUP):
            sk = jnp.where(chunk_of == c, sink_ref[2 * c + hh] * LOG2_E, sk)
        sinks.append(sk)
    if local:
        kj = lax.broadcasted_iota(jnp.int32, (BLOCK, nq), 0)
        qi = lax.broadcasted_iota(jnp.int32, (BLOCK, nq), 1) % BLOCK

    rows_out = []
    for blk in range(n_blk):
        qts = jnp.concatenate([qt_ref[0, c * LANES:(c + 1) * LANES, blk * BLOCK:(blk + 1) * BLOCK]
                               for c in range(A_GROUP)], axis=1)
        if local:
            kwin = jnp.concatenate(k_blocks[blk:blk + 3] + [kx_ref[0]], axis=0)
            vtw = jnp.concatenate(v_blocks[blk:blk + 3] + [vx_ref[0]], axis=1)
            ok_prev = (kj >= qi) & (step > 0) if blk == 0 else (kj >= qi)
            ok_next = (kj <= qi) & (step < n_steps - 1) if blk == n_blk - 1 else (kj <= qi)
        else:
            kwin, vtw, ok_prev, ok_next = kx_ref[0], vx_ref[0], None, None
        ot = _attn_block(kwin, vtw, qts, sinks, ok_prev, ok_next)
        rows = slice(blk * BLOCK, (blk + 1) * BLOCK)
        outs = []
        for c in range(A_GROUP):
            sg = sg_ref[0, rows, c * LANES:(c + 1) * LANES].astype(F32)
            outs.append((ot[:, c * BLOCK:(c + 1) * BLOCK].T * sg).astype(BF16))
        rows_out.append(jnp.concatenate(outs, axis=1))
    y = _dot(jnp.concatenate(rows_out, axis=0), w_ref[...])
    gate = m_ref[0][:, 2 * D:3 * D]
    o_ref[0] = x_ref[0] + gate * y


def _attention(sink, qt, k, vt, kx, vxt, sg, x, mod3, mod_row, w_out, local):
    b, s, _ = sg.shape
    rows = min(A_BLOCKS_PER_STEP * BLOCK, s)
    assert s % rows == 0
    ns = s // rows
    n_ctx = kx.shape[1]
    blk = lambda w: pl.BlockSpec((1, rows, w), lambda bi, i: (bi, i, 0))
    blk_t = lambda w: pl.BlockSpec((1, w, rows), lambda bi, i: (bi, 0, i))
    if mod_row is None:
        mod_map = lambda bi, i: (bi, 0, 0)
    else:
        mod_map = lambda bi, i: (mod_row, 0, 0)
    in_specs = [pl.BlockSpec(memory_space=pltpu.SMEM), blk_t(A_WIDTH)]
    args = [sink, qt]
    if local:
        lo = lambda i: jnp.maximum(i - 1, 0)
        hi = lambda i: jnp.minimum(i + 1, ns - 1)
        in_specs += [pl.BlockSpec((1, rows, A_KV_WIDTH), lambda bi, i: (bi, lo(i), 0)),
                     blk(A_KV_WIDTH),
                     pl.BlockSpec((1, rows, A_KV_WIDTH), lambda bi, i: (bi, hi(i), 0)),
                     pl.BlockSpec((1, A_KV_WIDTH, rows), lambda bi, i: (bi, 0, lo(i))),
                     blk_t(A_KV_WIDTH),
                     pl.BlockSpec((1, A_KV_WIDTH, rows), lambda bi, i: (bi, 0, hi(i)))]
        args += [k, k, k, vt, vt, vt]
    in_specs += [pl.BlockSpec((1, n_ctx, A_KV_WIDTH), lambda bi, i: (bi, 0, 0)),
                 pl.BlockSpec((1, A_KV_WIDTH, n_ctx), lambda bi, i: (bi, 0, 0)),
                 blk(A_WIDTH), blk(D),
                 pl.BlockSpec((1, 1, 3 * D), mod_map),
                 pl.BlockSpec((A_WIDTH, D), lambda bi, i: (0, 0), pipeline_mode=pl.Buffered(1))]
    args += [kx, vxt, sg, x, mod3, w_out]
    return pl.pallas_call(
        functools.partial(_attn_kernel, local, ns),
        grid=(b, ns),
        in_specs=in_specs,
        out_specs=blk(D),
        out_shape=jax.ShapeDtypeStruct((b, s, D), F32),
        scratch_shapes=[pltpu.VMEM((A_WIDTH, D), BF16)],
        compiler_params=_params("arbitrary", "arbitrary"),
        name="attn_local" if local else "attn_ctx",
    )(*args)


def _chunk_cumsum(x, tri):
    hi = x.astype(BF16)
    lo = (x - hi.astype(F32)).astype(BF16)
    return _dot(tri, hi) + _dot(tri, lo)


def _inproj_gla_kernel(chunk, with_q, x_ref, m_ref, g_ref, w32_ref, wa1_ref, wa2_ref, ba_ref, tri_ref, *outs):
    *outs, w_ref = outs
    _cast_once(w32_ref, w_ref)
    if with_q:
        per_dir = (outs[0:3], outs[3:6])
        v_ref, sg_ref = outs[6:]
    else:
        per_dir = ((outs[0],), (outs[1],))
    tm = x_ref.shape[1]
    hb = _norm_mod(x_ref[0], g_ref[...], m_ref[0]).astype(BF16)
    r = _dot(hb, wa1_ref[...])
    z = _dot(r.astype(BF16), wa2_ref[...]) + ba_ref[...]
    e = jnp.exp2(jnp.abs(z) * -LOG2_E)
    la = (jnp.minimum(z, 0.0) * LOG2_E - jnp.log2(1.0 + e)) * (1.0 / GATE_TEMP)
    k = _dot(hb, w_ref[:, B_K_WIDTH:2 * B_K_WIDTH])
    if with_q:
        q = _dot(hb, w_ref[:, 0:B_K_WIDTH]) * K_SCALE
    n_chunks = tm // chunk
    k_ends = []
    for reverse in (False, True):
        lad = la[:, B_K_WIDTH:] if reverse else la[:, 0:B_K_WIDTH]
        refs = per_dir[1] if reverse else per_dir[0]
        tots, kis, k_end = [], [], None
        tri = tri_ref[1 if reverse else 0]
        tb = tri.shape[0]
        cum_all = jnp.concatenate([_chunk_cumsum(lad[r * tb:(r + 1) * tb], tri) for r in range(tm // tb)], axis=0)
        for c in range(n_chunks):
            rows = slice(c * chunk, (c + 1) * chunk)
            cum = cum_all[rows]
            tot = cum[0:1] if reverse else cum[chunk - 1:chunk]
            if not with_q:
                k_end = (k[rows] * jnp.exp2(tot - cum)).astype(BF16)
                continue
            refs[0][0, rows] = (q[rows] * jnp.exp2(cum)).astype(BF16)
            kis.append(k[rows] * jnp.exp2(-cum))
            tots.append(tot)
            if c % 2 == 1:
                pair = slice((c - 1) * chunk, (c + 1) * chunk)
                refs[1][0, :, pair] = jnp.concatenate(kis[-2:], axis=0).T.astype(BF16)
        if with_q:
            pad = [jnp.zeros((DEC_ROWS - n_chunks, B_K_WIDTH), F32)] if n_chunks < DEC_ROWS else []
            refs[2][0, 0] = jnp.exp2(jnp.concatenate(tots + pad, axis=0))
        else:
            k_ends.append(k_end)
    if with_q:
        sg_ref[0] = _silu(_dot(hb, w_ref[:, 2 * B_K_WIDTH + B_V_WIDTH:])).astype(BF16)
    v = _dot(hb, w_ref[:, 2 * B_K_WIDTH:2 * B_K_WIDTH + B_V_WIDTH]).astype(BF16)
    if with_q:
        v_ref[0] = v
    else:
        for refs, k_end in zip(per_dir, k_ends):
            for h in range(B_HEADS):
                ks = slice(h * B_KEY_DIM, (h + 1) * B_KEY_DIM)
                refs[0][0, h] = _dot_tn(k_end[:, ks], v[:, h * B_VAL_DIM:(h + 1) * B_VAL_DIM])


def _inproj_gla(x, mod3, mod_row, norm_g, w, wa1, wa2, ba, chunk, with_q):
    b, s, _ = x.shape
    tm = min(ROW_TILE, s)
    nt = s // tm
    cpt = tm // chunk
    tb = max(chunk, MXU_DIM)
    assert tm % tb == 0 and tb % chunk == 0
    if mod_row is None:
        mod_map = lambda bi, i: (bi, 0, 0)
    else:
        mod_map = lambda bi, i: (mod_row, 0, 0)
    const = lambda shape: pl.BlockSpec(shape, lambda bi, i: (0,) * len(shape))
    in_specs = [
        pl.BlockSpec((1, tm, D), lambda bi, i: (bi, i, 0)),
        pl.BlockSpec((1, 1, 3 * D), mod_map),
        const((1, D)), pl.BlockSpec(w.shape, lambda bi, i: (0, 0), pipeline_mode=pl.Buffered(1)),
        const(wa1.shape), const(wa2.shape), const(ba.shape),
        const((2, tb, tb)),
    ]
    t_idx = np.arange(tb)
    same = (t_idx[:, None] // chunk) == (t_idx[None, :] // chunk)
    lower = same & (t_idx[None, :] <= t_idx[:, None])
    tri = jnp.asarray(np.stack([lower, lower.T]), dtype=BF16)
    rows = lambda w_: (jax.ShapeDtypeStruct((b, s, w_), BF16),
                       pl.BlockSpec((1, tm, w_), lambda bi, i: (bi, i, 0)))
    cols = (jax.ShapeDtypeStruct((b, B_K_WIDTH, s), BF16),
            pl.BlockSpec((1, B_K_WIDTH, tm), lambda bi, i: (bi, 0, i)))
    decs = (jax.ShapeDtypeStruct((b, nt, DEC_ROWS, B_K_WIDTH), F32),
            pl.BlockSpec((1, 1, DEC_ROWS, B_K_WIDTH), lambda bi, i: (bi, i, 0, 0)))
    if with_q:
        assert cpt <= DEC_ROWS and cpt % 2 == 0
        outs = [rows(B_K_WIDTH), cols, decs] * 2 + [rows(B_V_WIDTH)] * 2
    else:
        assert nt == 1 and cpt == 1
        state = (jax.ShapeDtypeStruct((b, B_HEADS, B_KEY_DIM, B_VAL_DIM), F32),
                 pl.BlockSpec((1, B_HEADS, B_KEY_DIM, B_VAL_DIM), lambda bi, i: (bi, 0, 0, 0)))
        outs = [state] * 2
    return pl.pallas_call(
        functools.partial(_inproj_gla_kernel, chunk, with_q),
        grid=(b, nt),
        in_specs=in_specs,
        out_specs=tuple(o[1] for o in outs),
        out_shape=tuple(o[0] for o in outs),
        scratch_shapes=[pltpu.VMEM(w.shape, BF16)],
        compiler_params=_params("arbitrary", "arbitrary"),
        name="inproj_gla" if with_q else "inproj_gla_ctx",
    )(x, mod3, norm_g.reshape(1, D), w, wa1, wa2, ba, tri)


def _scan_tile(reverse, qd_ref, kit_ref, dec_ref, v_ref, st_ref):
    pair_rows = 2 * CHUNK
    ti = lax.broadcasted_iota(jnp.int32, (pair_rows, pair_rows), 0)
    si = lax.broadcasted_iota(jnp.int32, (pair_rows, pair_rows), 1)
    same = (ti // CHUNK) == (si // CHUNK)
    if reverse:
        use_inv = same & (ti <= si)
        use_end = (ti < CHUNK) & (si >= CHUNK)
        second_row = lax.broadcasted_iota(jnp.int32, (pair_rows, B_KEY_DIM), 0) < CHUNK
        first_col = lax.broadcasted_iota(jnp.int32, (B_KEY_DIM, pair_rows), 1) >= CHUNK
    else:
        use_inv = same & (ti >= si)
        use_end = (ti >= CHUNK) & (si < CHUNK)
        second_row = lax.broadcasted_iota(jnp.int32, (pair_rows, B_KEY_DIM), 0) >= CHUNK
        first_col = lax.broadcasted_iota(jnp.int32, (B_KEY_DIM, pair_rows), 1) < CHUNK
    n_pairs = SCAN_TILE // pair_rows
    order = range(n_pairs - 1, -1, -1) if reverse else range(n_pairs)
    n_dec = dec_ref.shape[1] * DEC_ROWS
    dec = dec_ref[0].reshape(n_dec, B_K_WIDTH)
    dect = jnp.concatenate([dec, jnp.zeros((LANES - n_dec, B_K_WIDTH), F32)], axis=0).T
    heads = []
    for h in range(B_HEADS):
        ks = slice(h * B_KEY_DIM, (h + 1) * B_KEY_DIM)
        vs = slice(h * B_VAL_DIM, (h + 1) * B_VAL_DIM)
        st = st_ref[h]
        o_rows = [None] * n_pairs
        for p in order:
            rows = slice(p * pair_rows, (p + 1) * pair_rows)
            c_first, c_second = (2 * p + 1, 2 * p) if reverse else (2 * p, 2 * p + 1)
            qd = qd_ref[0, rows, ks]
            kit = kit_ref[0, ks, rows]
            vh = v_ref[0, rows, vs]
            d_first, d_second = dect[ks, c_first:c_first + 1], dect[ks, c_second:c_second + 1]
            ki32 = kit.astype(F32)
            s2 = _dot(qd, jnp.concatenate([kit, (ki32 * d_first).astype(BF16)], axis=1))
            a = jnp.where(use_inv, s2[:, 0:pair_rows], jnp.where(use_end, s2[:, pair_rows:], 0.0)).astype(BF16)
            q_pair = jnp.where(second_row, qd.astype(F32) * dec[c_first:c_first + 1, ks], qd.astype(F32))
            k_pair = ki32 * jnp.where(first_col, d_first * d_second, d_second)
            o_rows[p] = _dot(a, vh) + _dot(q_pair.astype(BF16), st.astype(BF16))
            st = st * (d_first * d_second) + _dot(k_pair.astype(BF16), vh)
        st_ref[h] = st
        heads.append(jnp.concatenate(o_rows, axis=0))
    return jnp.concatenate(heads, axis=1)


def _gla_scan_kernel(nt, qdf_ref, kitf_ref, decf_ref, qdb_ref, kitb_ref, decb_ref,
                     v_ref, sf_ref, sb_ref, sg_ref, x_ref, m_ref, hn_ref, w_ref, fn_ref,
                     o_ref, st_ref, of_ref):
    j = pl.program_id(1)

    @pl.when(j == 0)
    def _():
        st_ref[...] = sf_ref[0]

    @pl.when(j == nt)
    def _():
        st_ref[...] = sb_ref[0]

    @pl.when(j < nt)
    def _():
        of_ref[j] = _scan_tile(False, qdf_ref, kitf_ref, decf_ref, v_ref, st_ref)

    @pl.when(j >= nt)
    def _():
        o = _scan_tile(True, qdb_ref, kitb_ref, decb_ref, v_ref, st_ref) + of_ref[2 * nt - 1 - j]
        normed = []
        for h in range(B_HEADS):
            oh = o[:, h * B_VAL_DIM:(h + 1) * B_VAL_DIM]
            normed.append(oh * lax.rsqrt(jnp.mean(oh * oh, axis=-1, keepdims=True) + EPS))
        of = jnp.concatenate(normed, axis=1) * hn_ref[...]
        y = _dot((of * sg_ref[0].astype(F32)).astype(BF16), w_ref[...])
        xn = x_ref[0] + m_ref[0][:, 2 * D:3 * D] * y
        ms = jnp.mean(xn * xn, axis=-1, keepdims=True)
        o_ref[0] = xn * lax.rsqrt(ms + EPS) * fn_ref[...]


def _gla_scan(fwd, bwd, v, s_f, s_b, sg, x, mod3, head_g, w_out, final_g):
    b, s, _ = v.shape
    assert SCAN_TILE % ROW_TILE == 0 and ROW_TILE // CHUNK == DEC_ROWS
    dec_blocks = SCAN_TILE // ROW_TILE
    nt = s // SCAN_TILE
    t_fwd = lambda j: jnp.minimum(j, nt - 1)
    t_bwd = lambda j: 2 * nt - 1 - jnp.maximum(j, nt)
    t_both = lambda j: jnp.where(j < nt, j, 2 * nt - 1 - j)

    def direction(t):
        rows = pl.BlockSpec((1, SCAN_TILE, B_K_WIDTH), lambda bi, j: (bi, t(j), 0))
        cols = pl.BlockSpec((1, B_K_WIDTH, SCAN_TILE), lambda bi, j: (bi, 0, t(j)))
        decs = pl.BlockSpec((1, dec_blocks, DEC_ROWS, B_K_WIDTH), lambda bi, j: (bi, t(j), 0, 0))
        return [rows, cols, decs]

    const = lambda shape: pl.BlockSpec(shape, lambda bi, j: (0,) * len(shape))
    state = pl.BlockSpec((1, B_HEADS, B_KEY_DIM, B_VAL_DIM), lambda bi, j: (bi, 0, 0, 0))
    out_rows = lambda w_: pl.BlockSpec((1, SCAN_TILE, w_), lambda bi, j: (bi, t_bwd(j), 0))
    in_specs = direction(t_fwd) + direction(t_bwd) + [
        pl.BlockSpec((1, SCAN_TILE, B_V_WIDTH), lambda bi, j: (bi, t_both(j), 0)),
        state, state, out_rows(B_V_WIDTH), out_rows(D),
        pl.BlockSpec((1, 1, 3 * D), lambda bi, j: (bi, 0, 0)),
        const((1, B_V_WIDTH)), const((B_V_WIDTH, D)), const((1, D))]
    return pl.pallas_call(
        functools.partial(_gla_scan_kernel, nt),
        grid=(b, 2 * nt),
        in_specs=in_specs,
        out_specs=out_rows(D),
        out_shape=jax.ShapeDtypeStruct((b, s, D), F32),
        scratch_shapes=[pltpu.VMEM((B_HEADS, B_KEY_DIM, B_VAL_DIM), F32),
                        pltpu.VMEM((nt, SCAN_TILE, B_V_WIDTH), F32)],
        compiler_params=_params("parallel", "arbitrary", vmem_limit=SCAN_VMEM_LIMIT),
        name="gla_scan",
    )(*fwd, *bwd, v, s_f, s_b, sg, x, mod3, head_g.reshape(1, B_V_WIDTH), w_out, final_g.reshape(1, D))


def _rope_tables(n_tokens):
    rows_n = n_tokens // GRID_W
    row = np.repeat(np.arange(rows_n, dtype=np.float32), GRID_W)
    col = np.tile(np.arange(GRID_W, dtype=np.float32), rows_n)
    inv_freq = (np.float32(ROPE_BASE) ** (-np.arange(ROPE_FREQS, dtype=np.float32) / np.float32(ROPE_FREQS)))
    inv_freq = inv_freq.astype(np.float32)
    ang = np.stack([row[:, None] * inv_freq, col[:, None] * inv_freq], axis=1)
    cos, sin = np.cos(ang).astype(np.float32), np.sin(ang).astype(np.float32)
    zero = np.zeros_like(sin)
    tile = lambda t: np.tile(t.reshape(n_tokens, A_HEAD_DIM), (1, LANES // A_HEAD_DIM))
    return (tile(np.stack([cos, cos], axis=2)),
            tile(np.stack([-sin, zero], axis=2)),
            tile(np.stack([zero, sin], axis=2)))


def _pair_heads(t, axis):
    shape = t.shape
    t = t.reshape(shape[:axis] + (A_KV_HEADS, A_GROUP, -1) + shape[axis + 1:])
    return jnp.swapaxes(t, axis, axis + 1).reshape(shape)


def kernel(x, c, ctx, c_ctx, l0_norm_g, l0_w_ada, l0_b_ada, l0_w_in, l0_sink, l0_w_out, l1_norm_g, l1_w_ada, l1_b_ada, l1_w_in, l1_wa1_f, l1_wa2_f, l1_ba_f, l1_wa1_b, l1_wa2_b, l1_ba_b, l1_head_norm_g, l1_w_out, final_norm_g):
    b, s, _ = x.shape
    ctx_row = b

    cvec = jnp.concatenate([c, c_ctx[None, :], jnp.zeros((MOD_ROWS - b - 1, D), F32)], axis=0)
    mod0 = _modulation(cvec, l0_w_ada, l0_b_ada).reshape(MOD_ROWS, 1, 3 * D)
    mod1 = _modulation(cvec, l1_w_ada, l1_b_ada).reshape(MOD_ROWS, 1, 3 * D)

    sink = _pair_heads(l0_sink.astype(F32), 0)
    qt, k, vt, sg = _inproj_attn(x, mod0, None, l0_norm_g, l0_w_in, _rope_tables(s))
    qct, kc, vct, sgc = _inproj_attn(ctx, mod0, ctx_row, l0_norm_g, l0_w_in, None)
    x1 = _attention(sink, qt, k, vt, kc, vct, sg, x, mod0, None, l0_w_out, True)
    xc1 = _attention(sink, qct, None, None, kc, vct, sgc, ctx, mod0, ctx_row, l0_w_out, False)

    w1 = l1_w_in
    wa1 = jnp.concatenate([l1_wa1_f, l1_wa1_b, jnp.zeros((D, LANES - 2 * GATE_RANK), F32)], axis=1).astype(BF16)
    wa2 = jnp.zeros((LANES, 2 * B_K_WIDTH), F32)
    wa2 = wa2.at[0:GATE_RANK, 0:B_K_WIDTH].set(l1_wa2_f)
    wa2 = wa2.at[GATE_RANK:2 * GATE_RANK, B_K_WIDTH:].set(l1_wa2_b).astype(BF16)
    ba = jnp.concatenate([l1_ba_f, l1_ba_b]).reshape(1, 2 * B_K_WIDTH)
    s_f, s_b = _inproj_gla(xc1, mod1, ctx_row, l1_norm_g, w1, wa1, wa2, ba, ctx.shape[1], False)
    outs = _inproj_gla(x1, mod1, None, l1_norm_g, w1, wa1, wa2, ba, CHUNK, True)
    v1, sg1 = outs[6:]
    return _gla_scan(outs[0:3], outs[3:6], v1, s_f, s_b, sg1, x1, mod1,
                     l1_head_norm_g, l1_w_out.astype(BF16), final_norm_g)
```

```python
import functools

import jax
import jax.numpy as jnp
import numpy as np
from jax import lax
from jax.experimental import pallas as pl
from jax.experimental.pallas import tpu as pltpu

F32 = jnp.float32
BF16 = jnp.bfloat16

D = 1024
GRID_W = 64
EPS = 1e-6
NEG_INF = -1e30

A_HEADS = 16
A_KV_HEADS = 2
A_GROUP = A_HEADS // A_KV_HEADS
A_HEAD_DIM = 64
A_WIDTH = A_HEADS * A_HEAD_DIM
A_KV_WIDTH = A_KV_HEADS * A_HEAD_DIM
BLOCK = 128
ROPE_BASE = 10000.0
ROPE_FREQS = A_HEAD_DIM // 4
Q_SCALE = A_HEAD_DIM ** -0.5
LOG2_E = 1.4426950408889634

B_HEADS = 4
B_K_WIDTH = D // 2
B_V_WIDTH = D
B_KEY_DIM = B_K_WIDTH // B_HEADS
B_VAL_DIM = B_V_WIDTH // B_HEADS
GATE_RANK = 16
GATE_TEMP = 16.0
CHUNK = 64
K_SCALE = B_KEY_DIM ** -0.5

LANES = 128
MXU_DIM = 256
MOD_ROWS = 16
ROW_TILE = 512
SCAN_TILE = 1024
SUB_ROWS = 256
DEC_ROWS = 8
SUM_ROWS = 16
A_BLOCKS_PER_STEP = 4
VMEM_LIMIT = 48 * 1024 * 1024
SCAN_VMEM_LIMIT = 56 * 1024 * 1024


def _params(*sem, vmem_limit=VMEM_LIMIT):
    return pltpu.CompilerParams(dimension_semantics=sem, vmem_limit_bytes=vmem_limit)


def _silu(x):
    return x / (1.0 + jnp.exp(-x))


def _dot(a, b):
    return jnp.dot(a, b, preferred_element_type=F32)


def _dot_nt(a, b):
    return lax.dot_general(a, b, (((1,), (1,)), ((), ())), preferred_element_type=F32)


def _dot_tn(a, b):
    return lax.dot_general(a, b, (((0,), (0,)), ((), ())), preferred_element_type=F32)


def _norm_mod(x, g, m):
    ms = jnp.mean(x * x, axis=-1, keepdims=True)
    y = x * lax.rsqrt(ms + EPS) * g
    return y * (1.0 + m[:, D:2 * D]) + m[:, 0:D]


def _cast_once(src_ref, dst_ref):
    @pl.when((pl.program_id(0) == 0) & (pl.program_id(1) == 0))
    def _():
        def body(i, carry):
            rows = pl.ds(pl.multiple_of(i * LANES, LANES), LANES)
            dst_ref[rows, :] = src_ref[rows, :].astype(dst_ref.dtype)
            return carry
        lax.fori_loop(0, src_ref.shape[0] // LANES, body, 0)


def _mod_kernel(c_ref, w_ref, b_ref, o_ref):
    s = _silu(c_ref[...])
    o_ref[...] = _dot(s.astype(BF16), w_ref[...].astype(BF16)) + b_ref[...]


def _modulation(cvec, w_ada, b_ada):
    n = w_ada.shape[1] // D
    return pl.pallas_call(
        _mod_kernel,
        grid=(n,),
        in_specs=[
            pl.BlockSpec((MOD_ROWS, D), lambda j: (0, 0)),
            pl.BlockSpec((D, D), lambda j: (0, j)),
            pl.BlockSpec((1, D), lambda j: (0, j)),
        ],
        out_specs=pl.BlockSpec((MOD_ROWS, D), lambda j: (0, j)),
        out_shape=jax.ShapeDtypeStruct((MOD_ROWS, n * D), F32),
        compiler_params=_params("arbitrary"),
        name="modulation",
    )(cvec, w_ada, b_ada.reshape(1, n * D))


def _rope(t, cos, sin_hi, sin_lo):
    return t * cos + pltpu.roll(t, LANES - ROPE_FREQS, 1) * sin_hi + pltpu.roll(t, ROPE_FREQS, 1) * sin_lo


def _pair_cast_columns(src_ref, dst_ref):
    @pl.when((pl.program_id(0) == 0) & (pl.program_id(1) == 0))
    def _():
        kv_lo, kv_hi = A_WIDTH, A_WIDTH + 2 * A_KV_WIDTH

        def body(i, carry):
            rows = pl.ds(pl.multiple_of(i * LANES, LANES), LANES)
            src = src_ref[rows, :]
            dst_ref[rows, kv_lo:kv_hi] = src[:, kv_lo:kv_hi].astype(BF16)
            for base in (0, kv_hi):
                for p in range(A_GROUP):
                    a = base + p * A_HEAD_DIM
                    b = base + (p + A_GROUP) * A_HEAD_DIM
                    pair = jnp.concatenate([src[:, a:a + A_HEAD_DIM], src[:, b:b + A_HEAD_DIM]], axis=1)
                    dst_ref[rows, base + p * LANES:base + (p + 1) * LANES] = pair.astype(BF16)
            return carry
        lax.fori_loop(0, src_ref.shape[0] // LANES, body, 0)


def _inproj_attn_kernel(rope, x_ref, m_ref, g_ref, w32_ref, *rest):
    *rest, w_ref = rest
    _pair_cast_columns(w32_ref, w_ref)
    wq_ref = w_ref.at[:, 0:A_WIDTH]
    wkv_ref = w_ref.at[:, A_WIDTH:A_WIDTH + 2 * A_KV_WIDTH]
    wg_ref = w_ref.at[:, A_WIDTH + 2 * A_KV_WIDTH:]
    if rope:
        cos_ref, shi_ref, slo_ref, qt_ref, k_ref, vt_ref, sg_ref = rest
    else:
        qt_ref, k_ref, vt_ref, sg_ref = rest
    for r in range(x_ref.shape[1] // SUB_ROWS):
        rows = slice(r * SUB_ROWS, (r + 1) * SUB_ROWS)
        if rope:
            cos, shi, slo = cos_ref[rows], shi_ref[rows], slo_ref[rows]
        hb = _norm_mod(x_ref[0, rows], g_ref[...], m_ref[0]).astype(BF16)
        q = _dot(hb, wq_ref[...])
        for j in range(A_WIDTH // LANES):
            qj = q[:, j * LANES:(j + 1) * LANES]
            if rope:
                qj = _rope(qj, cos, shi, slo)
            qt_ref[0, j * LANES:(j + 1) * LANES, rows] = (qj * (Q_SCALE * LOG2_E)).T.astype(BF16)
        kv = _dot(hb, wkv_ref[...])
        k = kv[:, 0:A_KV_WIDTH]
        if rope:
            k = _rope(k, cos, shi, slo)
        k_ref[0, rows] = k.astype(BF16)
        vt_ref[0, :, rows] = kv[:, A_KV_WIDTH:].T.astype(BF16)
        g = _dot(hb, wg_ref[...])
        sg_ref[0, rows] = _silu(g).astype(BF16)


def _inproj_attn(x, mod3, norm_g, w_in, tables):
    b, s, _ = x.shape
    tm = min(ROW_TILE, s)
    in_specs = [
        pl.BlockSpec((1, tm, D), lambda bi, i: (bi, i, 0)),
        pl.BlockSpec((1, 1, 3 * D), lambda bi, i: (bi, 0, 0)),
        pl.BlockSpec((1, D), lambda bi, i: (0, 0)),
        pl.BlockSpec(w_in.shape, lambda bi, i: (0, 0), pipeline_mode=pl.Buffered(1)),
    ] + [pl.BlockSpec((tm, LANES), lambda bi, i: (i, 0))] * 3
    args = [x, mod3, norm_g.reshape(1, D), w_in, *tables]
    out_shape = (
        jax.ShapeDtypeStruct((b, A_WIDTH, s), BF16),
        jax.ShapeDtypeStruct((b, s, A_KV_WIDTH), BF16),
        jax.ShapeDtypeStruct((b, A_KV_WIDTH, s), BF16),
        jax.ShapeDtypeStruct((b, s, A_WIDTH), BF16),
    )
    out_specs = (
        pl.BlockSpec((1, A_WIDTH, tm), lambda bi, i: (bi, 0, i)),
        pl.BlockSpec((1, tm, A_KV_WIDTH), lambda bi, i: (bi, i, 0)),
        pl.BlockSpec((1, A_KV_WIDTH, tm), lambda bi, i: (bi, 0, i)),
        pl.BlockSpec((1, tm, A_WIDTH), lambda bi, i: (bi, i, 0)),
    )
    return pl.pallas_call(
        functools.partial(_inproj_attn_kernel, True),
        grid=(b, s // tm),
        in_specs=in_specs,
        out_specs=out_specs,
        out_shape=out_shape,
        scratch_shapes=[pltpu.VMEM(w_in.shape, BF16)],
        compiler_params=_params("arbitrary", "arbitrary"),
        name="inproj_attn",
    )(*args)


def _attn_block(kwin, vtw, qts, sinks, ok_prev, ok_next):
    n_keys, nq = kwin.shape[0], qts.shape[1]
    lane = lax.broadcasted_iota(jnp.int32, kwin.shape, 1)
    kbd = jnp.concatenate([jnp.where(lane < A_HEAD_DIM, kwin, jnp.zeros_like(kwin)),
                           jnp.where(lane >= A_HEAD_DIM, kwin, jnp.zeros_like(kwin))], axis=0)
    ones = jnp.where(lax.broadcasted_iota(jnp.int32, (SUM_ROWS, n_keys), 0) == 0, 1.0, 0.0).astype(BF16)

    st = _dot(kbd, qts)
    outs = []
    for hh in range(2):
        sh = st[hh * n_keys:(hh + 1) * n_keys]
        if ok_prev is not None:
            parts = [jnp.where(ok_prev, sh[0:BLOCK], NEG_INF),
                     sh[BLOCK:2 * BLOCK],
                     jnp.where(ok_next, sh[2 * BLOCK:3 * BLOCK], NEG_INF),
                     sh[3 * BLOCK:]]
        else:
            parts = [sh]
        mx = functools.reduce(jnp.maximum, [jnp.max(t, axis=0, keepdims=True) for t in parts])
        mx = jnp.maximum(mx, sinks[hh])
        probs = jnp.concatenate([jnp.exp2(t - mx).astype(BF16) for t in parts], axis=0)
        vt_h = jnp.concatenate([vtw[hh * A_HEAD_DIM:(hh + 1) * A_HEAD_DIM], ones], axis=0)
        ot = _dot(vt_h, probs)
        den = ot[A_HEAD_DIM:A_HEAD_DIM + 1] + jnp.exp2(sinks[hh] - mx)
        outs.append(ot[0:A_HEAD_DIM] * (1.0 / den))
    return jnp.concatenate(outs, axis=0)


def _pair_cast_rows(src_ref, dst_ref):
    @pl.when((pl.program_id(0) == 0) & (pl.program_id(1) == 0))
    def _():
        for p in range(A_GROUP):
            for half, h in enumerate((p, p + A_GROUP)):
                lo = p * LANES + half * A_HEAD_DIM
                dst_ref[lo:lo + A_HEAD_DIM, :] = src_ref[h * A_HEAD_DIM:(h + 1) * A_HEAD_DIM, :].astype(BF16)


def _attn_kernel(local, n_steps, sink_ref, qt_ref, *rest):
    *rest, w_ref = rest
    n_blk = rest[-1].shape[1] // BLOCK
    if local:
        (kp_ref, kc_ref, kn_ref, vp_ref, vc_ref, vn_ref, kx_ref, vx_ref,
         sg_ref, x_ref, m_ref, w32_ref, o_ref) = rest
        last = slice((n_blk - 1) * BLOCK, n_blk * BLOCK)
        k_blocks = ([kp_ref[0, last]] + [kc_ref[0, i * BLOCK:(i + 1) * BLOCK] for i in range(n_blk)]
                    + [kn_ref[0, 0:BLOCK]])
        v_blocks = ([vp_ref[0, :, last]] + [vc_ref[0, :, i * BLOCK:(i + 1) * BLOCK] for i in range(n_blk)]
                    + [vn_ref[0, :, 0:BLOCK]])
    else:
        kx_ref, vx_ref, sg_ref, x_ref, m_ref, w32_ref, o_ref = rest
    _pair_cast_rows(w32_ref, w_ref)
    step = pl.program_id(1)
    nq = A_GROUP * BLOCK
    chunk_of = lax.broadcasted_iota(jnp.int32, (1, nq), 1) // BLOCK
    sinks = []
    for hh in range(2):
        sk = jnp.full((1, nq), sink_ref[hh] * LOG2_E, F32)
        for c in range(1, A_GROUP):
            sk = jnp.where(chunk_of == c, sink_ref[2 * c + hh] * LOG2_E, sk)
        sinks.append(sk)
    if local:
        kj = lax.broadcasted_iota(jnp.int32, (BLOCK, nq), 0)
        qi = lax.broadcasted_iota(jnp.int32, (BLOCK, nq), 1) % BLOCK

    rows_out = []
    for blk in range(n_blk):
        qts = jnp.concatenate([qt_ref[0, c * LANES:(c + 1) * LANES, blk * BLOCK:(blk + 1) * BLOCK]
                               for c in range(A_GROUP)], axis=1)
        if local:
            kwin = jnp.concatenate(k_blocks[blk:blk + 3] + [kx_ref[0]], axis=0)
            vtw = jnp.concatenate(v_blocks[blk:blk + 3] + [vx_ref[0]], axis=1)
            ok_prev = (kj >= qi) & (step > 0) if blk == 0 else (kj >= qi)
            ok_next = (kj <= qi) & (step < n_steps - 1) if blk == n_blk - 1 else (kj <= qi)
        else:
            kwin, vtw, ok_prev, ok_next = kx_ref[0], vx_ref[0], None, None
        ot = _attn_block(kwin, vtw, qts, sinks, ok_prev, ok_next)
        rows = slice(blk * BLOCK, (blk + 1) * BLOCK)
        outs = []
        for c in range(A_GROUP):
            sg = sg_ref[0, rows, c * LANES:(c + 1) * LANES].astype(F32)
            outs.append((ot[:, c * BLOCK:(c + 1) * BLOCK].T * sg).astype(BF16))
        rows_out.append(jnp.concatenate(outs, axis=1))
    y = _dot(jnp.concatenate(rows_out, axis=0), w_ref[...])
    gate = m_ref[0][:, 2 * D:3 * D]
    o_ref[0] = x_ref[0] + gate * y


def _attention(sink, qt, k, vt, kx, vxt, sg, x, mod3, w_out):
    b, s, _ = sg.shape
    rows = A_BLOCKS_PER_STEP * BLOCK
    assert s % rows == 0
    ns = s // rows
    n_ctx = kx.shape[1]
    blk = lambda w: pl.BlockSpec((1, rows, w), lambda bi, i: (bi, i, 0))
    blk_t = lambda w: pl.BlockSpec((1, w, rows), lambda bi, i: (bi, 0, i))
    lo = lambda i: jnp.maximum(i - 1, 0)
    hi = lambda i: jnp.minimum(i + 1, ns - 1)
    in_specs = [pl.BlockSpec(memory_space=pltpu.SMEM), blk_t(A_WIDTH),
                pl.BlockSpec((1, rows, A_KV_WIDTH), lambda bi, i: (bi, lo(i), 0)),
                blk(A_KV_WIDTH),
                pl.BlockSpec((1, rows, A_KV_WIDTH), lambda bi, i: (bi, hi(i), 0)),
                pl.BlockSpec((1, A_KV_WIDTH, rows), lambda bi, i: (bi, 0, lo(i))),
                blk_t(A_KV_WIDTH),
                pl.BlockSpec((1, A_KV_WIDTH, rows), lambda bi, i: (bi, 0, hi(i))),
                pl.BlockSpec((1, n_ctx, A_KV_WIDTH), lambda bi, i: (bi, 0, 0)),
                pl.BlockSpec((1, A_KV_WIDTH, n_ctx), lambda bi, i: (bi, 0, 0)),
                blk(A_WIDTH), blk(D),
                pl.BlockSpec((1, 1, 3 * D), lambda bi, i: (bi, 0, 0)),
                pl.BlockSpec((A_WIDTH, D), lambda bi, i: (0, 0), pipeline_mode=pl.Buffered(1))]
    return pl.pallas_call(
        functools.partial(_attn_kernel, True, ns),
        grid=(b, ns),
        in_specs=in_specs,
        out_specs=blk(D),
        out_shape=jax.ShapeDtypeStruct((b, s, D), F32),
        scratch_shapes=[pltpu.VMEM((A_WIDTH, D), BF16)],
        compiler_params=_params("arbitrary", "arbitrary"),
        name="attn_local",
    )(sink, qt, k, k, k, vt, vt, vt, kx, vxt, sg, x, mod3, w_out)


def _ctx_attn_kernel(sink_ref, x_ref, m_ref, g_ref, win_ref, wout_ref, k_ref, vt_ref, o_ref,
                     win_scr, wout_scr, qt_scr, sg_scr):
    _inproj_attn_kernel(False, x_ref, m_ref, g_ref, win_ref, qt_scr, k_ref, vt_ref, sg_scr, win_scr)
    _attn_kernel(False, 1, sink_ref, qt_scr, k_ref, vt_ref, sg_scr, x_ref, m_ref, wout_ref, o_ref, wout_scr)


def _ctx_attention(sink, xc, mod3, mod_row, norm_g, w_in, w_out):
    b, n, _ = xc.shape
    const = lambda shape, **kw: pl.BlockSpec(shape, lambda bi, i: (0,) * len(shape), **kw)
    return pl.pallas_call(
        _ctx_attn_kernel,
        grid=(b, 1),
        in_specs=[pl.BlockSpec(memory_space=pltpu.SMEM),
                  pl.BlockSpec((1, n, D), lambda bi, i: (bi, 0, 0)),
                  pl.BlockSpec((1, 1, 3 * D), lambda bi, i: (mod_row, 0, 0)),
                  const((1, D)),
                  const(w_in.shape, pipeline_mode=pl.Buffered(1)),
                  const(w_out.shape, pipeline_mode=pl.Buffered(1))],
        out_specs=(pl.BlockSpec((1, n, A_KV_WIDTH), lambda bi, i: (bi, 0, 0)),
                   pl.BlockSpec((1, A_KV_WIDTH, n), lambda bi, i: (bi, 0, 0)),
                   pl.BlockSpec((1, n, D), lambda bi, i: (bi, 0, 0))),
        out_shape=(jax.ShapeDtypeStruct((b, n, A_KV_WIDTH), BF16),
                   jax.ShapeDtypeStruct((b, A_KV_WIDTH, n), BF16),
                   jax.ShapeDtypeStruct((b, n, D), F32)),
        scratch_shapes=[pltpu.VMEM(w_in.shape, BF16), pltpu.VMEM(w_out.shape, BF16),
                        pltpu.VMEM((1, A_WIDTH, n), BF16), pltpu.VMEM((1, n, A_WIDTH), BF16)],
        compiler_params=_params("arbitrary", "arbitrary"),
        name="ctx_attn",
    )(sink, xc, mod3, norm_g.reshape(1, D), w_in, w_out)


def _chunk_cumsum(x, tri):
    hi = x.astype(BF16)
    lo = (x - hi.astype(F32)).astype(BF16)
    return _dot(tri, hi) + _dot(tri, lo)


def _inproj_gla_kernel(chunk, with_q, x_ref, m_ref, g_ref, w32_ref, wa1_ref, wa2_ref, ba_ref, tri_ref, *outs):
    *outs, w_ref = outs
    _cast_once(w32_ref, w_ref)
    if with_q:
        per_dir = (outs[0:3], outs[3:6])
        v_ref, sg_ref = outs[6:]
    else:
        per_dir = ((outs[0],), (outs[1],))
    tm = x_ref.shape[1]
    hb = _norm_mod(x_ref[0], g_ref[...], m_ref[0]).astype(BF16)
    r = _dot(hb, wa1_ref[...])
    z = _dot(r.astype(BF16), wa2_ref[...]) + ba_ref[...]
    e = jnp.exp2(jnp.abs(z) * -LOG2_E)
    la = (jnp.minimum(z, 0.0) * LOG2_E - jnp.log2(1.0 + e)) * (1.0 / GATE_TEMP)
    k = _dot(hb, w_ref[:, B_K_WIDTH:2 * B_K_WIDTH])
    if with_q:
        q = _dot(hb, w_ref[:, 0:B_K_WIDTH]) * K_SCALE
    n_chunks = tm // chunk
    k_ends = []
    for reverse in (False, True):
        lad = la[:, B_K_WIDTH:] if reverse else la[:, 0:B_K_WIDTH]
        refs = per_dir[1] if reverse else per_dir[0]
        tots, kis, k_end = [], [], None
        tri = tri_ref[1 if reverse else 0]
        tb = tri.shape[0]
        cum_all = jnp.concatenate([_chunk_cumsum(lad[r * tb:(r + 1) * tb], tri) for r in range(tm // tb)], axis=0)
        for c in range(n_chunks):
            rows = slice(c * chunk, (c + 1) * chunk)
            cum = cum_all[rows]
            tot = cum[0:1] if reverse else cum[chunk - 1:chunk]
            if not with_q:
                k_end = (k[rows] * jnp.exp2(tot - cum)).astype(BF16)
                continue
            refs[0][0, rows] = (q[rows] * jnp.exp2(cum)).astype(BF16)
            kis.append(k[rows] * jnp.exp2(-cum))
            tots.append(tot)
            if c % 2 == 1:
                pair = slice((c - 1) * chunk, (c + 1) * chunk)
                refs[1][0, :, pair] = jnp.concatenate(kis[-2:], axis=0).T.astype(BF16)
        if with_q:
            pad = [jnp.zeros((DEC_ROWS - n_chunks, B_K_WIDTH), F32)] if n_chunks < DEC_ROWS else []
            refs[2][0, 0] = jnp.exp2(jnp.concatenate(tots + pad, axis=0))
        else:
            k_ends.append(k_end)
    if with_q:
        sg_ref[0] = _silu(_dot(hb, w_ref[:, 2 * B_K_WIDTH + B_V_WIDTH:])).astype(BF16)
    v = _dot(hb, w_ref[:, 2 * B_K_WIDTH:2 * B_K_WIDTH + B_V_WIDTH]).astype(BF16)
    if with_q:
        v_ref[0] = v
    else:
        for refs, k_end in zip(per_dir, k_ends):
            for h in range(B_HEADS):
                ks = slice(h * B_KEY_DIM, (h + 1) * B_KEY_DIM)
                refs[0][0, h] = _dot_tn(k_end[:, ks], v[:, h * B_VAL_DIM:(h + 1) * B_VAL_DIM])


def _inproj_gla(x, mod3, mod_row, norm_g, w, wa1, wa2, ba, chunk, with_q):
    b, s, _ = x.shape
    tm = min(ROW_TILE, s)
    nt = s // tm
    cpt = tm // chunk
    tb = max(chunk, MXU_DIM)
    assert tm % tb == 0 and tb % chunk == 0
    if mod_row is None:
        mod_map = lambda bi, i: (bi, 0, 0)
    else:
        mod_map = lambda bi, i: (mod_row, 0, 0)
    const = lambda shape: pl.BlockSpec(shape, lambda bi, i: (0,) * len(shape))
    in_specs = [
        pl.BlockSpec((1, tm, D), lambda bi, i: (bi, i, 0)),
        pl.BlockSpec((1, 1, 3 * D), mod_map),
        const((1, D)), pl.BlockSpec(w.shape, lambda bi, i: (0, 0), pipeline_mode=pl.Buffered(1)),
        const(wa1.shape), const(wa2.shape), const(ba.shape),
        const((2, tb, tb)),
    ]
    t_idx = np.arange(tb)
    same = (t_idx[:, None] // chunk) == (t_idx[None, :] // chunk)
    lower = same & (t_idx[None, :] <= t_idx[:, None])
    tri = jnp.asarray(np.stack([lower, lower.T]), dtype=BF16)
    rows = lambda w_: (jax.ShapeDtypeStruct((b, s, w_), BF16),
                       pl.BlockSpec((1, tm, w_), lambda bi, i: (bi, i, 0)))
    cols = (jax.ShapeDtypeStruct((b, B_K_WIDTH, s), BF16),
            pl.BlockSpec((1, B_K_WIDTH, tm), lambda bi, i: (bi, 0, i)))
    decs = (jax.ShapeDtypeStruct((b, nt, DEC_ROWS, B_K_WIDTH), F32),
            pl.BlockSpec((1, 1, DEC_ROWS, B_K_WIDTH), lambda bi, i: (bi, i, 0, 0)))
    if with_q:
        assert cpt <= DEC_ROWS and cpt % 2 == 0
        outs = [rows(B_K_WIDTH), cols, decs] * 2 + [rows(B_V_WIDTH)] * 2
    else:
        assert nt == 1 and cpt == 1
        state = (jax.ShapeDtypeStruct((b, B_HEADS, B_KEY_DIM, B_VAL_DIM), F32),
                 pl.BlockSpec((1, B_HEADS, B_KEY_DIM, B_VAL_DIM), lambda bi, i: (bi, 0, 0, 0)))
        outs = [state] * 2
    return pl.pallas_call(
        functools.partial(_inproj_gla_kernel, chunk, with_q),
        grid=(b, nt),
        in_specs=in_specs,
        out_specs=tuple(o[1] for o in outs),
        out_shape=tuple(o[0] for o in outs),
        scratch_shapes=[pltpu.VMEM(w.shape, BF16)],
        compiler_params=_params("arbitrary", "arbitrary"),
        name="inproj_gla" if with_q else "inproj_gla_ctx",
    )(x, mod3, norm_g.reshape(1, D), w, wa1, wa2, ba, tri)


def _scan_tile(reverse, qd_ref, kit_ref, dec_ref, v_ref, st_ref):
    pair_rows = 2 * CHUNK
    ti = lax.broadcasted_iota(jnp.int32, (pair_rows, pair_rows), 0)
    si = lax.broadcasted_iota(jnp.int32, (pair_rows, pair_rows), 1)
    same = (ti // CHUNK) == (si // CHUNK)
    if reverse:
        use_inv = same & (ti <= si)
        use_end = (ti < CHUNK) & (si >= CHUNK)
        second_row = lax.broadcasted_iota(jnp.int32, (pair_rows, B_KEY_DIM), 0) < CHUNK
        first_col = lax.broadcasted_iota(jnp.int32, (B_KEY_DIM, pair_rows), 1) >= CHUNK
    else:
        use_inv = same & (ti >= si)
        use_end = (ti >= CHUNK) & (si < CHUNK)
        second_row = lax.broadcasted_iota(jnp.int32, (pair_rows, B_KEY_DIM), 0) >= CHUNK
        first_col = lax.broadcasted_iota(jnp.int32, (B_KEY_DIM, pair_rows), 1) < CHUNK
    n_pairs = SCAN_TILE // pair_rows
    order = range(n_pairs - 1, -1, -1) if reverse else range(n_pairs)
    n_dec = dec_ref.shape[1] * DEC_ROWS
    dec = dec_ref[0].reshape(n_dec, B_K_WIDTH)
    dect = jnp.concatenate([dec, jnp.zeros((LANES - n_dec, B_K_WIDTH), F32)], axis=0).T
    heads = []
    for h in range(B_HEADS):
        ks = slice(h * B_KEY_DIM, (h + 1) * B_KEY_DIM)
        vs = slice(h * B_VAL_DIM, (h + 1) * B_VAL_DIM)
        st = st_ref[h]
        o_rows = [None] * n_pairs
        for p in order:
            rows = slice(p * pair_rows, (p + 1) * pair_rows)
            c_first, c_second = (2 * p + 1, 2 * p) if reverse else (2 * p, 2 * p + 1)
            qd = qd_ref[0, rows, ks]
            kit = kit_ref[0, ks, rows]
            vh = v_ref[0, rows, vs]
            d_first, d_second = dect[ks, c_first:c_first + 1], dect[ks, c_second:c_second + 1]
            ki32 = kit.astype(F32)
            s2 = _dot(qd, jnp.concatenate([kit, (ki32 * d_first).astype(BF16)], axis=1))
            a = jnp.where(use_inv, s2[:, 0:pair_rows], jnp.where(use_end, s2[:, pair_rows:], 0.0)).astype(BF16)
            q_pair = jnp.where(second_row, qd.astype(F32) * dec[c_first:c_first + 1, ks], qd.astype(F32))
            k_pair = ki32 * jnp.where(first_col, d_first * d_second, d_second)
            o_rows[p] = _dot(a, vh) + _dot(q_pair.astype(BF16), st.astype(BF16))
            st = st * (d_first * d_second) + _dot(k_pair.astype(BF16), vh)
        st_ref[h] = st
        heads.append(jnp.concatenate(o_rows, axis=0))
    return jnp.concatenate(heads, axis=1)


def _gla_scan_kernel(nt, qdf_ref, kitf_ref, decf_ref, qdb_ref, kitb_ref, decb_ref,
                     v_ref, sf_ref, sb_ref, sg_ref, x_ref, m_ref, hn_ref, w_ref, fn_ref,
                     o_ref, st_ref, of_ref):
    j = pl.program_id(1)

    @pl.when(j == 0)
    def _():
        st_ref[...] = sf_ref[0]

    @pl.when(j == nt)
    def _():
        st_ref[...] = sb_ref[0]

    @pl.when(j < nt)
    def _():
        of_ref[j] = _scan_tile(False, qdf_ref, kitf_ref, decf_ref, v_ref, st_ref)

    @pl.when(j >= nt)
    def _():
        o = _scan_tile(True, qdb_ref, kitb_ref, decb_ref, v_ref, st_ref) + of_ref[2 * nt - 1 - j]
        normed = []
        for h in range(B_HEADS):
            oh = o[:, h * B_VAL_DIM:(h + 1) * B_VAL_DIM]
            normed.append(oh * lax.rsqrt(jnp.mean(oh * oh, axis=-1, keepdims=True) + EPS))
        of = jnp.concatenate(normed, axis=1) * hn_ref[...]
        y = _dot((of * sg_ref[0].astype(F32)).astype(BF16), w_ref[...])
        xn = x_ref[0] + m_ref[0][:, 2 * D:3 * D] * y
        ms = jnp.mean(xn * xn, axis=-1, keepdims=True)
        o_ref[0] = xn * lax.rsqrt(ms + EPS) * fn_ref[...]


def _gla_scan(fwd, bwd, v, s_f, s_b, sg, x, mod3, head_g, w_out, final_g):
    b, s, _ = v.shape
    assert SCAN_TILE % ROW_TILE == 0 and ROW_TILE // CHUNK == DEC_ROWS
    dec_blocks = SCAN_TILE // ROW_TILE
    nt = s // SCAN_TILE
    t_fwd = lambda j: jnp.minimum(j, nt - 1)
    t_bwd = lambda j: 2 * nt - 1 - jnp.maximum(j, nt)
    t_both = lambda j: jnp.where(j < nt, j, 2 * nt - 1 - j)

    def direction(t):
        rows = pl.BlockSpec((1, SCAN_TILE, B_K_WIDTH), lambda bi, j: (bi, t(j), 0))
        cols = pl.BlockSpec((1, B_K_WIDTH, SCAN_TILE), lambda bi, j: (bi, 0, t(j)))
        decs = pl.BlockSpec((1, dec_blocks, DEC_ROWS, B_K_WIDTH), lambda bi, j: (bi, t(j), 0, 0))
        return [rows, cols, decs]

    const = lambda shape: pl.BlockSpec(shape, lambda bi, j: (0,) * len(shape))
    state = pl.BlockSpec((1, B_HEADS, B_KEY_DIM, B_VAL_DIM), lambda bi, j: (bi, 0, 0, 0))
    out_rows = lambda w_: pl.BlockSpec((1, SCAN_TILE, w_), lambda bi, j: (bi, t_bwd(j), 0))
    in_specs = direction(t_fwd) + direction(t_bwd) + [
        pl.BlockSpec((1, SCAN_TILE, B_V_WIDTH), lambda bi, j: (bi, t_both(j), 0)),
        state, state, out_rows(B_V_WIDTH), out_rows(D),
        pl.BlockSpec((1, 1, 3 * D), lambda bi, j: (bi, 0, 0)),
        const((1, B_V_WIDTH)), const((B_V_WIDTH, D)), const((1, D))]
    return pl.pallas_call(
        functools.partial(_gla_scan_kernel, nt),
        grid=(b, 2 * nt),
        in_specs=in_specs,
        out_specs=out_rows(D),
        out_shape=jax.ShapeDtypeStruct((b, s, D), F32),
        scratch_shapes=[pltpu.VMEM((B_HEADS, B_KEY_DIM, B_VAL_DIM), F32),
                        pltpu.VMEM((nt, SCAN_TILE, B_V_WIDTH), F32)],
        compiler_params=_params("parallel", "arbitrary", vmem_limit=SCAN_VMEM_LIMIT),
        name="gla_scan",
    )(*fwd, *bwd, v, s_f, s_b, sg, x, mod3, head_g.reshape(1, B_V_WIDTH), w_out, final_g.reshape(1, D))


def _rope_tables(n_tokens):
    rows_n = n_tokens // GRID_W
    row = np.repeat(np.arange(rows_n, dtype=np.float32), GRID_W)
    col = np.tile(np.arange(GRID_W, dtype=np.float32), rows_n)
    inv_freq = (np.float32(ROPE_BASE) ** (-np.arange(ROPE_FREQS, dtype=np.float32) / np.float32(ROPE_FREQS)))
    inv_freq = inv_freq.astype(np.float32)
    ang = np.stack([row[:, None] * inv_freq, col[:, None] * inv_freq], axis=1)
    cos, sin = np.cos(ang).astype(np.float32), np.sin(ang).astype(np.float32)
    zero = np.zeros_like(sin)
    tile = lambda t: np.tile(t.reshape(n_tokens, A_HEAD_DIM), (1, LANES // A_HEAD_DIM))
    return (tile(np.stack([cos, cos], axis=2)),
            tile(np.stack([-sin, zero], axis=2)),
            tile(np.stack([zero, sin], axis=2)))


def _pair_heads(t, axis):
    shape = t.shape
    t = t.reshape(shape[:axis] + (A_KV_HEADS, A_GROUP, -1) + shape[axis + 1:])
    return jnp.swapaxes(t, axis, axis + 1).reshape(shape)


def kernel(x, c, ctx, c_ctx, l0_norm_g, l0_w_ada, l0_b_ada, l0_w_in, l0_sink, l0_w_out, l1_norm_g, l1_w_ada, l1_b_ada, l1_w_in, l1_wa1_f, l1_wa2_f, l1_ba_f, l1_wa1_b, l1_wa2_b, l1_ba_b, l1_head_norm_g, l1_w_out, final_norm_g):
    b, s, _ = x.shape
    ctx_row = b

    cvec = jnp.concatenate([c, c_ctx[None, :], jnp.zeros((MOD_ROWS - b - 1, D), F32)], axis=0)
    mod0 = _modulation(cvec, l0_w_ada, l0_b_ada).reshape(MOD_ROWS, 1, 3 * D)
    mod1 = _modulation(cvec, l1_w_ada, l1_b_ada).reshape(MOD_ROWS, 1, 3 * D)

    sink = _pair_heads(l0_sink.astype(F32), 0)
    kc, vct, xc1 = _ctx_attention(sink, ctx, mod0, ctx_row, l0_norm_g, l0_w_in, l0_w_out)
    qt, k, vt, sg = _inproj_attn(x, mod0, l0_norm_g, l0_w_in, _rope_tables(s))
    x1 = _attention(sink, qt, k, vt, kc, vct, sg, x, mod0, l0_w_out)

    w1 = l1_w_in
    wa1 = jnp.concatenate([l1_wa1_f, l1_wa1_b, jnp.zeros((D, LANES - 2 * GATE_RANK), F32)], axis=1).astype(BF16)
    wa2 = jnp.zeros((LANES, 2 * B_K_WIDTH), F32)
    wa2 = wa2.at[0:GATE_RANK, 0:B_K_WIDTH].set(l1_wa2_f)
    wa2 = wa2.at[GATE_RANK:2 * GATE_RANK, B_K_WIDTH:].set(l1_wa2_b).astype(BF16)
    ba = jnp.concatenate([l1_ba_f, l1_ba_b]).reshape(1, 2 * B_K_WIDTH)
    s_f, s_b = _inproj_gla(xc1, mod1, ctx_row, l1_norm_g, w1, wa1, wa2, ba, ctx.shape[1], False)
    outs = _inproj_gla(x1, mod1, None, l1_norm_g, w1, wa1, wa2, ba, CHUNK, True)
    v1, sg1 = outs[6:]
    return _gla_scan(outs[0:3], outs[3:6], v1, s_f, s_b, sg1, x1, mod1,
                     l1_head_norm_g, l1_w_out.astype(BF16), final_norm_g)
```

```python
import functools

import jax
import jax.numpy as jnp
import numpy as np
from jax import lax
from jax.experimental import pallas as pl
from jax.experimental.pallas import tpu as pltpu

F32 = jnp.float32
BF16 = jnp.bfloat16

D = 1024
GRID_W = 64
EPS = 1e-6
NEG_INF = -1e30

A_HEADS = 16
A_KV_HEADS = 2
A_GROUP = A_HEADS // A_KV_HEADS
A_HEAD_DIM = 64
A_WIDTH = A_HEADS * A_HEAD_DIM
A_KV_WIDTH = A_KV_HEADS * A_HEAD_DIM
BLOCK = 128
ROPE_BASE = 10000.0
ROPE_FREQS = A_HEAD_DIM // 4
Q_SCALE = A_HEAD_DIM ** -0.5
LOG2_E = 1.4426950408889634

B_HEADS = 4
B_K_WIDTH = D // 2
B_V_WIDTH = D
B_KEY_DIM = B_K_WIDTH // B_HEADS
B_VAL_DIM = B_V_WIDTH // B_HEADS
GATE_RANK = 16
GATE_TEMP = 16.0
CHUNK = 64
K_SCALE = B_KEY_DIM ** -0.5

LANES = 128
MXU_DIM = 256
MOD_ROWS = 16
ROW_TILE = 512
SCAN_TILE = 1024
SUB_ROWS = 256
DEC_ROWS = 8
SUM_ROWS = 16
A_BLOCKS_PER_STEP = 4
VMEM_LIMIT = 48 * 1024 * 1024
SCAN_VMEM_LIMIT = 56 * 1024 * 1024


def _params(*sem, vmem_limit=VMEM_LIMIT):
    return pltpu.CompilerParams(dimension_semantics=sem, vmem_limit_bytes=vmem_limit)


def _silu(x):
    return x / (1.0 + jnp.exp(-x))


def _dot(a, b):
    return jnp.dot(a, b, preferred_element_type=F32)


def _dot_nt(a, b):
    return lax.dot_general(a, b, (((1,), (1,)), ((), ())), preferred_element_type=F32)


def _dot_tn(a, b):
    return lax.dot_general(a, b, (((0,), (0,)), ((), ())), preferred_element_type=F32)


def _norm_mod(x, g, m):
    ms = jnp.mean(x * x, axis=-1, keepdims=True)
    y = x * lax.rsqrt(ms + EPS) * g
    return y * (1.0 + m[:, D:2 * D]) + m[:, 0:D]


def _cast_once(src_ref, dst_ref):
    @pl.when((pl.program_id(0) == 0) & (pl.program_id(1) == 0))
    def _():
        def body(i, carry):
            rows = pl.ds(pl.multiple_of(i * LANES, LANES), LANES)
            dst_ref[rows, :] = src_ref[rows, :].astype(dst_ref.dtype)
            return carry
        lax.fori_loop(0, src_ref.shape[0] // LANES, body, 0)


def _mod_kernel(c_ref, w_ref, b_ref, o_ref):
    s = _silu(c_ref[...])
    o_ref[...] = _dot(s.astype(BF16), w_ref[...].astype(BF16)) + b_ref[...]


def _modulation(cvec, w_ada, b_ada):
    n = w_ada.shape[1] // D
    return pl.pallas_call(
        _mod_kernel,
        grid=(n,),
        in_specs=[
            pl.BlockSpec((MOD_ROWS, D), lambda j: (0, 0)),
            pl.BlockSpec((D, D), lambda j: (0, j)),
            pl.BlockSpec((1, D), lambda j: (0, j)),
        ],
        out_specs=pl.BlockSpec((MOD_ROWS, D), lambda j: (0, j)),
        out_shape=jax.ShapeDtypeStruct((MOD_ROWS, n * D), F32),
        compiler_params=_params("arbitrary"),
        name="modulation",
    )(cvec, w_ada, b_ada.reshape(1, n * D))


def _rope(t, cos, sin_hi, sin_lo):
    return t * cos + pltpu.roll(t, LANES - ROPE_FREQS, 1) * sin_hi + pltpu.roll(t, ROPE_FREQS, 1) * sin_lo


def _pair_cast_columns(src_ref, dst_ref):
    @pl.when((pl.program_id(0) == 0) & (pl.program_id(1) == 0))
    def _():
        kv_lo, kv_hi = A_WIDTH, A_WIDTH + 2 * A_KV_WIDTH

        def body(i, carry):
            rows = pl.ds(pl.multiple_of(i * LANES, LANES), LANES)
            src = src_ref[rows, :]
            dst_ref[rows, kv_lo:kv_hi] = src[:, kv_lo:kv_hi].astype(BF16)
            for base in (0, kv_hi):
                for p in range(A_GROUP):
                    a = base + p * A_HEAD_DIM
                    b = base + (p + A_GROUP) * A_HEAD_DIM
                    pair = jnp.concatenate([src[:, a:a + A_HEAD_DIM], src[:, b:b + A_HEAD_DIM]], axis=1)
                    dst_ref[rows, base + p * LANES:base + (p + 1) * LANES] = pair.astype(BF16)
            return carry
        lax.fori_loop(0, src_ref.shape[0] // LANES, body, 0)


def _inproj_attn_kernel(rope, x_ref, m_ref, g_ref, w32_ref, *rest):
    *rest, w_ref = rest
    _pair_cast_columns(w32_ref, w_ref)
    wq_ref = w_ref.at[:, 0:A_WIDTH]
    wkv_ref = w_ref.at[:, A_WIDTH:A_WIDTH + 2 * A_KV_WIDTH]
    wg_ref = w_ref.at[:, A_WIDTH + 2 * A_KV_WIDTH:]
    if rope:
        cos_ref, shi_ref, slo_ref, qt_ref, k_ref, vt_ref, sg_ref = rest
    else:
        qt_ref, k_ref, vt_ref, sg_ref = rest
    for r in range(x_ref.shape[1] // SUB_ROWS):
        rows = slice(r * SUB_ROWS, (r + 1) * SUB_ROWS)
        if rope:
            cos, shi, slo = cos_ref[rows], shi_ref[rows], slo_ref[rows]
        hb = _norm_mod(x_ref[0, rows], g_ref[...], m_ref[0]).astype(BF16)
        q = _dot(hb, wq_ref[...])
        for j in range(A_WIDTH // LANES):
            qj = q[:, j * LANES:(j + 1) * LANES]
            if rope:
                qj = _rope(qj, cos, shi, slo)
            qt_ref[0, j * LANES:(j + 1) * LANES, rows] = (qj * (Q_SCALE * LOG2_E)).T.astype(BF16)
        kv = _dot(hb, wkv_ref[...])
        k = kv[:, 0:A_KV_WIDTH]
        if rope:
            k = _rope(k, cos, shi, slo)
        k_ref[0, rows] = k.astype(BF16)
        vt_ref[0, :, rows] = kv[:, A_KV_WIDTH:].T.astype(BF16)
        g = _dot(hb, wg_ref[...])
        sg_ref[0, rows] = _silu(g).astype(BF16)


def _inproj_attn(x, mod3, norm_g, w_in, tables):
    b, s, _ = x.shape
    tm = min(ROW_TILE, s)
    in_specs = [
        pl.BlockSpec((1, tm, D), lambda bi, i: (bi, i, 0)),
        pl.BlockSpec((1, 1, 3 * D), lambda bi, i: (bi, 0, 0)),
        pl.BlockSpec((1, D), lambda bi, i: (0, 0)),
        pl.BlockSpec(w_in.shape, lambda bi, i: (0, 0), pipeline_mode=pl.Buffered(1)),
    ] + [pl.BlockSpec((tm, LANES), lambda bi, i: (i, 0))] * 3
    args = [x, mod3, norm_g.reshape(1, D), w_in, *tables]
    out_shape = (
        jax.ShapeDtypeStruct((b, A_WIDTH, s), BF16),
        jax.ShapeDtypeStruct((b, s, A_KV_WIDTH), BF16),
        jax.ShapeDtypeStruct((b, A_KV_WIDTH, s), BF16),
        jax.ShapeDtypeStruct((b, s, A_WIDTH), BF16),
    )
    out_specs = (
        pl.BlockSpec((1, A_WIDTH, tm), lambda bi, i: (bi, 0, i)),
        pl.BlockSpec((1, tm, A_KV_WIDTH), lambda bi, i: (bi, i, 0)),
        pl.BlockSpec((1, A_KV_WIDTH, tm), lambda bi, i: (bi, 0, i)),
        pl.BlockSpec((1, tm, A_WIDTH), lambda bi, i: (bi, i, 0)),
    )
    return pl.pallas_call(
        functools.partial(_inproj_attn_kernel, True),
        grid=(b, s // tm),
        in_specs=in_specs,
        out_specs=out_specs,
        out_shape=out_shape,
        scratch_shapes=[pltpu.VMEM(w_in.shape, BF16)],
        compiler_params=_params("arbitrary", "arbitrary"),
        name="inproj_attn",
    )(*args)


def _attn_block(kwin, vtw, qts, sinks, ok_prev, ok_next):
    n_keys, nq = kwin.shape[0], qts.shape[1]
    lane = lax.broadcasted_iota(jnp.int32, kwin.shape, 1)
    kbd = jnp.concatenate([jnp.where(lane < A_HEAD_DIM, kwin, jnp.zeros_like(kwin)),
                           jnp.where(lane >= A_HEAD_DIM, kwin, jnp.zeros_like(kwin))], axis=0)
    ones = jnp.where(lax.broadcasted_iota(jnp.int32, (SUM_ROWS, n_keys), 0) == 0, 1.0, 0.0).astype(BF16)

    st = _dot(kbd, qts)
    outs = []
    for hh in range(2):
        sh = st[hh * n_keys:(hh + 1) * n_keys]
        if ok_prev is not None:
            parts = [jnp.where(ok_prev, sh[0:BLOCK], NEG_INF),
                     sh[BLOCK:2 * BLOCK],
                     jnp.where(ok_next, sh[2 * BLOCK:3 * BLOCK], NEG_INF),
                     sh[3 * BLOCK:]]
        else:
            parts = [sh]
        mx = functools.reduce(jnp.maximum, [jnp.max(t, axis=0, keepdims=True) for t in parts])
        mx = jnp.maximum(mx, sinks[hh])
        probs = jnp.concatenate([jnp.exp2(t - mx).astype(BF16) for t in parts], axis=0)
        vt_h = jnp.concatenate([vtw[hh * A_HEAD_DIM:(hh + 1) * A_HEAD_DIM], ones], axis=0)
        ot = _dot(vt_h, probs)
        den = ot[A_HEAD_DIM:A_HEAD_DIM + 1] + jnp.exp2(sinks[hh] - mx)
        outs.append(ot[0:A_HEAD_DIM] * (1.0 / den))
    return jnp.concatenate(outs, axis=0)


def _pair_cast_rows(src_ref, dst_ref):
    @pl.when((pl.program_id(0) == 0) & (pl.program_id(1) == 0))
    def _():
        for p in range(A_GROUP):
            for half, h in enumerate((p, p + A_GROUP)):
                lo = p * LANES + half * A_HEAD_DIM
                dst_ref[lo:lo + A_HEAD_DIM, :] = src_ref[h * A_HEAD_DIM:(h + 1) * A_HEAD_DIM, :].astype(BF16)


def _attn_kernel(local, n_steps, sink_ref, qt_ref, *rest):
    *rest, w_ref = rest
    n_blk = rest[-1].shape[1] // BLOCK
    if local:
        (kp_ref, kc_ref, kn_ref, vp_ref, vc_ref, vn_ref, kx_ref, vx_ref,
         sg_ref, x_ref, m_ref, w32_ref, o_ref) = rest
        last = slice((n_blk - 1) * BLOCK, n_blk * BLOCK)
        k_blocks = ([kp_ref[0, last]] + [kc_ref[0, i * BLOCK:(i + 1) * BLOCK] for i in range(n_blk)]
                    + [kn_ref[0, 0:BLOCK]])
        v_blocks = ([vp_ref[0, :, last]] + [vc_ref[0, :, i * BLOCK:(i + 1) * BLOCK] for i in range(n_blk)]
                    + [vn_ref[0, :, 0:BLOCK]])
    else:
        kx_ref, vx_ref, sg_ref, x_ref, m_ref, w32_ref, o_ref = rest
    _pair_cast_rows(w32_ref, w_ref)
    step = pl.program_id(1)
    nq = A_GROUP * BLOCK
    chunk_of = lax.broadcasted_iota(jnp.int32, (1, nq), 1) // BLOCK
    sinks = []
    for hh in range(2):
        sk = jnp.full((1, nq), sink_ref[hh] * LOG2_E, F32)
        for c in range(1, A_GROUP):
            sk = jnp.where(chunk_of == c, sink_ref[2 * c + hh] * LOG2_E, sk)
        sinks.append(sk)
    if local:
        kj = lax.broadcasted_iota(jnp.int32, (BLOCK, nq), 0)
        qi = lax.broadcasted_iota(jnp.int32, (BLOCK, nq), 1) % BLOCK

    rows_out = []
    for blk in range(n_blk):
        qts = jnp.concatenate([qt_ref[0, c * LANES:(c + 1) * LANES, blk * BLOCK:(blk + 1) * BLOCK]
                               for c in range(A_GROUP)], axis=1)
        if local:
            kwin = jnp.concatenate(k_blocks[blk:blk + 3] + [kx_ref[0]], axis=0)
            vtw = jnp.concatenate(v_blocks[blk:blk + 3] + [vx_ref[0]], axis=1)
            ok_prev = (kj >= qi) & (step > 0) if blk == 0 else (kj >= qi)
            ok_next = (kj <= qi) & (step < n_steps - 1) if blk == n_blk - 1 else (kj <= qi)
        else:
            kwin, vtw, ok_prev, ok_next = kx_ref[0], vx_ref[0], None, None
        ot = _attn_block(kwin, vtw, qts, sinks, ok_prev, ok_next)
        rows = slice(blk * BLOCK, (blk + 1) * BLOCK)
        outs = []
        for c in range(A_GROUP):
            sg = sg_ref[0, rows, c * LANES:(c + 1) * LANES].astype(F32)
            outs.append((ot[:, c * BLOCK:(c + 1) * BLOCK].T * sg).astype(BF16))
        rows_out.append(jnp.concatenate(outs, axis=1))
    y = _dot(jnp.concatenate(rows_out, axis=0), w_ref[...])
    gate = m_ref[0][:, 2 * D:3 * D]
    o_ref[0] = x_ref[0] + gate * y


def _attention(sink, qt, k, vt, kx, vxt, sg, x, mod3, w_out):
    b, s, _ = sg.shape
    rows = A_BLOCKS_PER_STEP * BLOCK
    assert s % rows == 0
    ns = s // rows
    n_ctx = kx.shape[1]
    blk = lambda w: pl.BlockSpec((1, rows, w), lambda bi, i: (bi, i, 0))
    blk_t = lambda w: pl.BlockSpec((1, w, rows), lambda bi, i: (bi, 0, i))
    lo = lambda i: jnp.maximum(i - 1, 0)
    hi = lambda i: jnp.minimum(i + 1, ns - 1)
    in_specs = [pl.BlockSpec(memory_space=pltpu.SMEM), blk_t(A_WIDTH),
                pl.BlockSpec((1, rows, A_KV_WIDTH), lambda bi, i: (bi, lo(i), 0)),
                blk(A_KV_WIDTH),
                pl.BlockSpec((1, rows, A_KV_WIDTH), lambda bi, i: (bi, hi(i), 0)),
                pl.BlockSpec((1, A_KV_WIDTH, rows), lambda bi, i: (bi, 0, lo(i))),
                blk_t(A_KV_WIDTH),
                pl.BlockSpec((1, A_KV_WIDTH, rows), lambda bi, i: (bi, 0, hi(i))),
                pl.BlockSpec((1, n_ctx, A_KV_WIDTH), lambda bi, i: (bi, 0, 0)),
                pl.BlockSpec((1, A_KV_WIDTH, n_ctx), lambda bi, i: (bi, 0, 0)),
                blk(A_WIDTH), blk(D),
                pl.BlockSpec((1, 1, 3 * D), lambda bi, i: (bi, 0, 0)),
                pl.BlockSpec((A_WIDTH, D), lambda bi, i: (0, 0), pipeline_mode=pl.Buffered(1))]
    return pl.pallas_call(
        functools.partial(_attn_kernel, True, ns),
        grid=(b, ns),
        in_specs=in_specs,
        out_specs=blk(D),
        out_shape=jax.ShapeDtypeStruct((b, s, D), F32),
        scratch_shapes=[pltpu.VMEM((A_WIDTH, D), BF16)],
        compiler_params=_params("arbitrary", "arbitrary"),
        name="attn_local",
    )(sink, qt, k, k, k, vt, vt, vt, kx, vxt, sg, x, mod3, w_out)


def _ctx_attn_kernel(sink_ref, x_ref, m_ref, g_ref, win_ref, wout_ref, k_ref, vt_ref, o_ref,
                     win_scr, wout_scr, qt_scr, sg_scr):
    _inproj_attn_kernel(False, x_ref, m_ref, g_ref, win_ref, qt_scr, k_ref, vt_ref, sg_scr, win_scr)
    _attn_kernel(False, 1, sink_ref, qt_scr, k_ref, vt_ref, sg_scr, x_ref, m_ref, wout_ref, o_ref, wout_scr)


def _ctx_attention(sink, xc, mod3, mod_row, norm_g, w_in, w_out):
    b, n, _ = xc.shape
    const = lambda shape, **kw: pl.BlockSpec(shape, lambda bi, i: (0,) * len(shape), **kw)
    return pl.pallas_call(
        _ctx_attn_kernel,
        grid=(b, 1),
        in_specs=[pl.BlockSpec(memory_space=pltpu.SMEM),
                  pl.BlockSpec((1, n, D), lambda bi, i: (bi, 0, 0)),
                  pl.BlockSpec((1, 1, 3 * D), lambda bi, i: (mod_row, 0, 0)),
                  const((1, D)),
                  const(w_in.shape, pipeline_mode=pl.Buffered(1)),
                  const(w_out.shape, pipeline_mode=pl.Buffered(1))],
        out_specs=(pl.BlockSpec((1, n, A_KV_WIDTH), lambda bi, i: (bi, 0, 0)),
                   pl.BlockSpec((1, A_KV_WIDTH, n), lambda bi, i: (bi, 0, 0)),
                   pl.BlockSpec((1, n, D), lambda bi, i: (bi, 0, 0))),
        out_shape=(jax.ShapeDtypeStruct((b, n, A_KV_WIDTH), BF16),
                   jax.ShapeDtypeStruct((b, A_KV_WIDTH, n), BF16),
                   jax.ShapeDtypeStruct((b, n, D), F32)),
        scratch_shapes=[pltpu.VMEM(w_in.shape, BF16), pltpu.VMEM(w_out.shape, BF16),
                        pltpu.VMEM((1, A_WIDTH, n), BF16), pltpu.VMEM((1, n, A_WIDTH), BF16)],
        compiler_params=_params("arbitrary", "arbitrary"),
        name="ctx_attn",
    )(sink, xc, mod3, norm_g.reshape(1, D), w_in, w_out)


def _chunk_cumsum(x, tri):
    hi = x.astype(BF16)
    lo = (x - hi.astype(F32)).astype(BF16)
    return _dot(tri, hi) + _dot(tri, lo)


def _inproj_gla_kernel(chunk, with_q, x_ref, m_ref, g_ref, w32_ref, wa1_ref, wa2_ref, ba_ref, tri_ref, *outs):
    *outs, w_ref = outs
    _cast_once(w32_ref, w_ref)
    if with_q:
        per_dir = (outs[0:3], outs[3:6])
        v_ref, g_ref_out = outs[6:]
    else:
        per_dir = ((outs[0],), (outs[1],))
    tm = x_ref.shape[1]
    hb = _norm_mod(x_ref[0], g_ref[...], m_ref[0]).astype(BF16)
    r = _dot(hb, wa1_ref[...])
    z = _dot(r.astype(BF16), wa2_ref[...]) + ba_ref[...]
    e = jnp.exp2(jnp.abs(z) * -LOG2_E)
    la = (jnp.minimum(z, 0.0) * LOG2_E - jnp.log2(1.0 + e)) * (1.0 / GATE_TEMP)
    k = _dot(hb, w_ref[:, B_K_WIDTH:2 * B_K_WIDTH])
    if with_q:
        q = _dot(hb, w_ref[:, 0:B_K_WIDTH]) * K_SCALE
    n_chunks = tm // chunk
    k_ends = []
    for reverse in (False, True):
        lad = la[:, B_K_WIDTH:] if reverse else la[:, 0:B_K_WIDTH]
        refs = per_dir[1] if reverse else per_dir[0]
        tots, kis, k_end = [], [], None
        tri = tri_ref[1 if reverse else 0]
        tb = tri.shape[0]
        cum_all = jnp.concatenate([_chunk_cumsum(lad[r * tb:(r + 1) * tb], tri) for r in range(tm // tb)], axis=0)
        for c in range(n_chunks):
            rows = slice(c * chunk, (c + 1) * chunk)
            cum = cum_all[rows]
            tot = cum[0:1] if reverse else cum[chunk - 1:chunk]
            if not with_q:
                k_end = (k[rows] * jnp.exp2(tot - cum)).astype(BF16)
                continue
            refs[0][0, rows] = (q[rows] * jnp.exp2(cum)).astype(BF16)
            kis.append(k[rows] * jnp.exp2(-cum))
            tots.append(tot)
            if c % 2 == 1:
                pair = slice((c - 1) * chunk, (c + 1) * chunk)
                refs[1][0, :, pair] = jnp.concatenate(kis[-2:], axis=0).T.astype(BF16)
        if with_q:
            pad = [jnp.zeros((DEC_ROWS - n_chunks, B_K_WIDTH), F32)] if n_chunks < DEC_ROWS else []
            refs[2][0, 0] = jnp.exp2(jnp.concatenate(tots + pad, axis=0))
        else:
            k_ends.append(k_end)
    if with_q:
        g_ref_out[0] = _dot(hb, w_ref[:, 2 * B_K_WIDTH + B_V_WIDTH:]).astype(BF16)
    v = _dot(hb, w_ref[:, 2 * B_K_WIDTH:2 * B_K_WIDTH + B_V_WIDTH]).astype(BF16)
    if with_q:
        v_ref[0] = v
    else:
        for refs, k_end in zip(per_dir, k_ends):
            for h in range(B_HEADS):
                ks = slice(h * B_KEY_DIM, (h + 1) * B_KEY_DIM)
                refs[0][0, h] = _dot_tn(k_end[:, ks], v[:, h * B_VAL_DIM:(h + 1) * B_VAL_DIM])


def _inproj_gla(x, mod3, mod_row, norm_g, w, wa1, wa2, ba, chunk, with_q):
    b, s, _ = x.shape
    tm = min(ROW_TILE, s)
    nt = s // tm
    cpt = tm // chunk
    tb = max(chunk, MXU_DIM)
    assert tm % tb == 0 and tb % chunk == 0
    if mod_row is None:
        mod_map = lambda bi, i: (bi, 0, 0)
    else:
        mod_map = lambda bi, i: (mod_row, 0, 0)
    const = lambda shape: pl.BlockSpec(shape, lambda bi, i: (0,) * len(shape))
    in_specs = [
        pl.BlockSpec((1, tm, D), lambda bi, i: (bi, i, 0)),
        pl.BlockSpec((1, 1, 3 * D), mod_map),
        const((1, D)), pl.BlockSpec(w.shape, lambda bi, i: (0, 0), pipeline_mode=pl.Buffered(1)),
        const(wa1.shape), const(wa2.shape), const(ba.shape),
        const((2, tb, tb)),
    ]
    t_idx = np.arange(tb)
    same = (t_idx[:, None] // chunk) == (t_idx[None, :] // chunk)
    lower = same & (t_idx[None, :] <= t_idx[:, None])
    tri = jnp.asarray(np.stack([lower, lower.T]), dtype=BF16)
    rows = lambda w_: (jax.ShapeDtypeStruct((b, s, w_), BF16),
                       pl.BlockSpec((1, tm, w_), lambda bi, i: (bi, i, 0)))
    cols = (jax.ShapeDtypeStruct((b, B_K_WIDTH, s), BF16),
            pl.BlockSpec((1, B_K_WIDTH, tm), lambda bi, i: (bi, 0, i)))
    decs = (jax.ShapeDtypeStruct((b, nt, DEC_ROWS, B_K_WIDTH), F32),
            pl.BlockSpec((1, 1, DEC_ROWS, B_K_WIDTH), lambda bi, i: (bi, i, 0, 0)))
    if with_q:
        assert cpt <= DEC_ROWS and cpt % 2 == 0
        outs = [rows(B_K_WIDTH), cols, decs] * 2 + [rows(B_V_WIDTH)] * 2
    else:
        assert nt == 1 and cpt == 1
        state = (jax.ShapeDtypeStruct((b, B_HEADS, B_KEY_DIM, B_VAL_DIM), F32),
                 pl.BlockSpec((1, B_HEADS, B_KEY_DIM, B_VAL_DIM), lambda bi, i: (bi, 0, 0, 0)))
        outs = [state] * 2
    return pl.pallas_call(
        functools.partial(_inproj_gla_kernel, chunk, with_q),
        grid=(b, nt),
        in_specs=in_specs,
        out_specs=tuple(o[1] for o in outs),
        out_shape=tuple(o[0] for o in outs),
        scratch_shapes=[pltpu.VMEM(w.shape, BF16)],
        compiler_params=_params("arbitrary", "arbitrary"),
        name="inproj_gla" if with_q else "inproj_gla_ctx",
    )(x, mod3, norm_g.reshape(1, D), w, wa1, wa2, ba, tri)


def _scan_tile(reverse, qd_ref, kit_ref, dec_ref, v_ref, st_ref):
    pair_rows = 2 * CHUNK
    ti = lax.broadcasted_iota(jnp.int32, (pair_rows, pair_rows), 0)
    si = lax.broadcasted_iota(jnp.int32, (pair_rows, pair_rows), 1)
    same = (ti // CHUNK) == (si // CHUNK)
    if reverse:
        use_inv = same & (ti <= si)
        use_end = (ti < CHUNK) & (si >= CHUNK)
        second_row = lax.broadcasted_iota(jnp.int32, (pair_rows, B_KEY_DIM), 0) < CHUNK
        first_col = lax.broadcasted_iota(jnp.int32, (B_KEY_DIM, pair_rows), 1) >= CHUNK
    else:
        use_inv = same & (ti >= si)
        use_end = (ti >= CHUNK) & (si < CHUNK)
        second_row = lax.broadcasted_iota(jnp.int32, (pair_rows, B_KEY_DIM), 0) >= CHUNK
        first_col = lax.broadcasted_iota(jnp.int32, (B_KEY_DIM, pair_rows), 1) < CHUNK
    n_pairs = SCAN_TILE // pair_rows
    order = range(n_pairs - 1, -1, -1) if reverse else range(n_pairs)
    n_dec = dec_ref.shape[1] * DEC_ROWS
    dec = dec_ref[0].reshape(n_dec, B_K_WIDTH)
    dect = jnp.concatenate([dec, jnp.zeros((LANES - n_dec, B_K_WIDTH), F32)], axis=0).T
    heads = []
    for h in range(B_HEADS):
        ks = slice(h * B_KEY_DIM, (h + 1) * B_KEY_DIM)
        vs = slice(h * B_VAL_DIM, (h + 1) * B_VAL_DIM)
        st = st_ref[h]
        o_rows = [None] * n_pairs
        for p in order:
            rows = slice(p * pair_rows, (p + 1) * pair_rows)
            c_first, c_second = (2 * p + 1, 2 * p) if reverse else (2 * p, 2 * p + 1)
            qd = qd_ref[0, rows, ks]
            kit = kit_ref[0, ks, rows]
            vh = v_ref[0, rows, vs]
            d_first, d_second = dect[ks, c_first:c_first + 1], dect[ks, c_second:c_second + 1]
            ki32 = kit.astype(F32)
            s2 = _dot(qd, jnp.concatenate([kit, (ki32 * d_first).astype(BF16)], axis=1))
            a = jnp.where(use_inv, s2[:, 0:pair_rows], jnp.where(use_end, s2[:, pair_rows:], 0.0)).astype(BF16)
            q_pair = jnp.where(second_row, qd.astype(F32) * dec[c_first:c_first + 1, ks], qd.astype(F32))
            k_pair = ki32 * jnp.where(first_col, d_first * d_second, d_second)
            o_rows[p] = _dot(a, vh) + _dot(q_pair.astype(BF16), st.astype(BF16))
            st = st * (d_first * d_second) + _dot(k_pair.astype(BF16), vh)
        st_ref[h] = st
        heads.append(jnp.concatenate(o_rows, axis=0))
    return jnp.concatenate(heads, axis=1)


def _gla_scan_kernel(nt, qdf_ref, kitf_ref, decf_ref, qdb_ref, kitb_ref, decb_ref,
                     v_ref, sf_ref, sb_ref, gate_ref, x_ref, m_ref, hn_ref, w32_ref, fn_ref,
                     o_ref, st_ref, of_ref, w_ref):
    _cast_once(w32_ref, w_ref)
    j = pl.program_id(1)

    @pl.when(j == 0)
    def _():
        st_ref[...] = sf_ref[0]

    @pl.when(j == nt)
    def _():
        st_ref[...] = sb_ref[0]

    @pl.when(j < nt)
    def _():
        of_ref[j] = _scan_tile(False, qdf_ref, kitf_ref, decf_ref, v_ref, st_ref)

    @pl.when(j >= nt)
    def _():
        o = _scan_tile(True, qdb_ref, kitb_ref, decb_ref, v_ref, st_ref) + of_ref[2 * nt - 1 - j]
        normed = []
        for h in range(B_HEADS):
            oh = o[:, h * B_VAL_DIM:(h + 1) * B_VAL_DIM]
            normed.append(oh * lax.rsqrt(jnp.mean(oh * oh, axis=-1, keepdims=True) + EPS))
        of = jnp.concatenate(normed, axis=1) * hn_ref[...]
        y = _dot((of * _silu(gate_ref[0].astype(F32))).astype(BF16), w_ref[...])
        xn = x_ref[0] + m_ref[0][:, 2 * D:3 * D] * y
        ms = jnp.mean(xn * xn, axis=-1, keepdims=True)
        o_ref[0] = xn * lax.rsqrt(ms + EPS) * fn_ref[...]


def _gla_scan(fwd, bwd, v, s_f, s_b, sg, x, mod3, head_g, w_out, final_g):
    b, s, _ = v.shape
    assert SCAN_TILE % ROW_TILE == 0 and ROW_TILE // CHUNK == DEC_ROWS
    dec_blocks = SCAN_TILE // ROW_TILE
    nt = s // SCAN_TILE
    t_fwd = lambda j: jnp.minimum(j, nt - 1)
    t_bwd = lambda j: 2 * nt - 1 - jnp.maximum(j, nt)
    t_both = lambda j: jnp.where(j < nt, j, 2 * nt - 1 - j)

    def direction(t):
        rows = pl.BlockSpec((1, SCAN_TILE, B_K_WIDTH), lambda bi, j: (bi, t(j), 0))
        cols = pl.BlockSpec((1, B_K_WIDTH, SCAN_TILE), lambda bi, j: (bi, 0, t(j)))
        decs = pl.BlockSpec((1, dec_blocks, DEC_ROWS, B_K_WIDTH), lambda bi, j: (bi, t(j), 0, 0))
        return [rows, cols, decs]

    const = lambda shape: pl.BlockSpec(shape, lambda bi, j: (0,) * len(shape))
    state = pl.BlockSpec((1, B_HEADS, B_KEY_DIM, B_VAL_DIM), lambda bi, j: (bi, 0, 0, 0))
    out_rows = lambda w_: pl.BlockSpec((1, SCAN_TILE, w_), lambda bi, j: (bi, t_bwd(j), 0))
    in_specs = direction(t_fwd) + direction(t_bwd) + [
        pl.BlockSpec((1, SCAN_TILE, B_V_WIDTH), lambda bi, j: (bi, t_both(j), 0)),
        state, state, out_rows(B_V_WIDTH), out_rows(D),
        pl.BlockSpec((1, 1, 3 * D), lambda bi, j: (bi, 0, 0)),
        const((1, B_V_WIDTH)),
        pl.BlockSpec((B_V_WIDTH, D), lambda bi, j: (0, 0), pipeline_mode=pl.Buffered(1)),
        const((1, D))]
    return pl.pallas_call(
        functools.partial(_gla_scan_kernel, nt),
        grid=(b, 2 * nt),
        in_specs=in_specs,
        out_specs=out_rows(D),
        out_shape=jax.ShapeDtypeStruct((b, s, D), F32),
        scratch_shapes=[pltpu.VMEM((B_HEADS, B_KEY_DIM, B_VAL_DIM), F32),
                        pltpu.VMEM((nt, SCAN_TILE, B_V_WIDTH), F32),
                        pltpu.VMEM((B_V_WIDTH, D), BF16)],
        compiler_params=_params("arbitrary", "arbitrary", vmem_limit=SCAN_VMEM_LIMIT),
        name="gla_scan",
    )(*fwd, *bwd, v, s_f, s_b, sg, x, mod3, head_g.reshape(1, B_V_WIDTH), w_out, final_g.reshape(1, D))


def _rope_tables(n_tokens):
    rows_n = n_tokens // GRID_W
    row = np.repeat(np.arange(rows_n, dtype=np.float32), GRID_W)
    col = np.tile(np.arange(GRID_W, dtype=np.float32), rows_n)
    inv_freq = (np.float32(ROPE_BASE) ** (-np.arange(ROPE_FREQS, dtype=np.float32) / np.float32(ROPE_FREQS)))
    inv_freq = inv_freq.astype(np.float32)
    ang = np.stack([row[:, None] * inv_freq, col[:, None] * inv_freq], axis=1)
    cos, sin = np.cos(ang).astype(np.float32), np.sin(ang).astype(np.float32)
    zero = np.zeros_like(sin)
    tile = lambda t: np.tile(t.reshape(n_tokens, A_HEAD_DIM), (1, LANES // A_HEAD_DIM))
    return (tile(np.stack([cos, cos], axis=2)),
            tile(np.stack([-sin, zero], axis=2)),
            tile(np.stack([zero, sin], axis=2)))


def _pair_heads(t, axis):
    shape = t.shape
    t = t.reshape(shape[:axis] + (A_KV_HEADS, A_GROUP, -1) + shape[axis + 1:])
    return jnp.swapaxes(t, axis, axis + 1).reshape(shape)


def kernel(x, c, ctx, c_ctx, l0_norm_g, l0_w_ada, l0_b_ada, l0_w_in, l0_sink, l0_w_out, l1_norm_g, l1_w_ada, l1_b_ada, l1_w_in, l1_wa1_f, l1_wa2_f, l1_ba_f, l1_wa1_b, l1_wa2_b, l1_ba_b, l1_head_norm_g, l1_w_out, final_norm_g):
    b, s, _ = x.shape
    ctx_row = b

    cvec = jnp.concatenate([c, c_ctx[None, :], jnp.zeros((MOD_ROWS - b - 1, D), F32)], axis=0)
    mod0 = _modulation(cvec, l0_w_ada, l0_b_ada).reshape(MOD_ROWS, 1, 3 * D)
    mod1 = _modulation(cvec, l1_w_ada, l1_b_ada).reshape(MOD_ROWS, 1, 3 * D)

    sink = _pair_heads(l0_sink.astype(F32), 0)
    kc, vct, xc1 = _ctx_attention(sink, ctx, mod0, ctx_row, l0_norm_g, l0_w_in, l0_w_out)
    qt, k, vt, sg = _inproj_attn(x, mod0, l0_norm_g, l0_w_in, _rope_tables(s))
    x1 = _attention(sink, qt, k, vt, kc, vct, sg, x, mod0, l0_w_out)

    w1 = l1_w_in
    wa1 = jnp.concatenate([l1_wa1_f, l1_wa1_b, jnp.zeros((D, LANES - 2 * GATE_RANK), F32)], axis=1).astype(BF16)
    wa2 = jnp.zeros((LANES, 2 * B_K_WIDTH), F32)
    wa2 = wa2.at[0:GATE_RANK, 0:B_K_WIDTH].set(l1_wa2_f)
    wa2 = wa2.at[GATE_RANK:2 * GATE_RANK, B_K_WIDTH:].set(l1_wa2_b).astype(BF16)
    ba = jnp.concatenate([l1_ba_f, l1_ba_b]).reshape(1, 2 * B_K_WIDTH)
    s_f, s_b = _inproj_gla(xc1, mod1, ctx_row, l1_norm_g, w1, wa1, wa2, ba, ctx.shape[1], False)
    outs = _inproj_gla(x1, mod1, None, l1_norm_g, w1, wa1, wa2, ba, CHUNK, True)
    v1, sg1 = outs[6:]
    return _gla_scan(outs[0:3], outs[3:6], v1, s_f, s_b, sg1, x1, mod1,
                     l1_head_norm_g, l1_w_out, final_norm_g)
```

```python
import functools

import jax
import jax.numpy as jnp
import numpy as np
from jax import lax
from jax.experimental import pallas as pl
from jax.experimental.pallas import tpu as pltpu

F32 = jnp.float32
BF16 = jnp.bfloat16

D = 1024
GRID_W = 64
EPS = 1e-6
NEG_INF = -1e30

A_HEADS = 16
A_KV_HEADS = 2
A_GROUP = A_HEADS // A_KV_HEADS
A_HEAD_DIM = 64
A_WIDTH = A_HEADS * A_HEAD_DIM
A_KV_WIDTH = A_KV_HEADS * A_HEAD_DIM
BLOCK = 128
ROPE_BASE = 10000.0
ROPE_FREQS = A_HEAD_DIM // 4
Q_SCALE = A_HEAD_DIM ** -0.5
LOG2_E = 1.4426950408889634

B_HEADS = 4
B_K_WIDTH = D // 2
B_V_WIDTH = D
B_KEY_DIM = B_K_WIDTH // B_HEADS
B_VAL_DIM = B_V_WIDTH // B_HEADS
GATE_RANK = 16
GATE_TEMP = 16.0
CHUNK = 64
K_SCALE = B_KEY_DIM ** -0.5

LANES = 128
MXU_DIM = 256
MOD_ROWS = 16
ROW_TILE = 512
A_ROW_TILE = 1024
SCAN_TILE = 1024
SUB_ROWS = 256
DEC_ROWS = 8
SUM_ROWS = 16
A_BLOCKS_PER_STEP = 4
VMEM_LIMIT = 48 * 1024 * 1024
SCAN_VMEM_LIMIT = 56 * 1024 * 1024


def _params(*sem, vmem_limit=VMEM_LIMIT):
    return pltpu.CompilerParams(dimension_semantics=sem, vmem_limit_bytes=vmem_limit)


def _silu(x):
    return x / (1.0 + jnp.exp(-x))


def _dot(a, b):
    return jnp.dot(a, b, preferred_element_type=F32)


def _dot_nt(a, b):
    return lax.dot_general(a, b, (((1,), (1,)), ((), ())), preferred_element_type=F32)


def _dot_tn(a, b):
    return lax.dot_general(a, b, (((0,), (0,)), ((), ())), preferred_element_type=F32)


def _norm_mod(x, g, m):
    ms = jnp.mean(x * x, axis=-1, keepdims=True)
    y = x * lax.rsqrt(ms + EPS) * g
    return y * (1.0 + m[:, D:2 * D]) + m[:, 0:D]


def _cast_once(src_ref, dst_ref):
    @pl.when((pl.program_id(0) == 0) & (pl.program_id(1) == 0))
    def _():
        def body(i, carry):
            rows = pl.ds(pl.multiple_of(i * LANES, LANES), LANES)
            dst_ref[rows, :] = src_ref[rows, :].astype(dst_ref.dtype)
            return carry
        lax.fori_loop(0, src_ref.shape[0] // LANES, body, 0)


def _mod_kernel(c_ref, w_ref, b_ref, o_ref):
    s = _silu(c_ref[...])
    o_ref[...] = _dot(s.astype(BF16), w_ref[...].astype(BF16)) + b_ref[...]


def _modulation(cvec, w_ada, b_ada):
    n = w_ada.shape[1] // D
    return pl.pallas_call(
        _mod_kernel,
        grid=(n,),
        in_specs=[
            pl.BlockSpec((MOD_ROWS, D), lambda j: (0, 0)),
            pl.BlockSpec((D, D), lambda j: (0, j)),
            pl.BlockSpec((1, D), lambda j: (0, j)),
        ],
        out_specs=pl.BlockSpec((MOD_ROWS, D), lambda j: (0, j)),
        out_shape=jax.ShapeDtypeStruct((MOD_ROWS, n * D), F32),
        compiler_params=_params("arbitrary"),
        name="modulation",
    )(cvec, w_ada, b_ada.reshape(1, n * D))


def _rope(t, cos, sin_hi, sin_lo):
    return t * cos + pltpu.roll(t, LANES - ROPE_FREQS, 1) * sin_hi + pltpu.roll(t, ROPE_FREQS, 1) * sin_lo


def _pair_cast_columns(src_ref, dst_ref):
    @pl.when((pl.program_id(0) == 0) & (pl.program_id(1) == 0))
    def _():
        kv_lo, kv_hi = A_WIDTH, A_WIDTH + 2 * A_KV_WIDTH

        def body(i, carry):
            rows = pl.ds(pl.multiple_of(i * LANES, LANES), LANES)
            src = src_ref[rows, :]
            dst_ref[rows, kv_lo:kv_hi] = src[:, kv_lo:kv_hi].astype(BF16)
            for base in (0, kv_hi):
                for p in range(A_GROUP):
                    a = base + p * A_HEAD_DIM
                    b = base + (p + A_GROUP) * A_HEAD_DIM
                    pair = jnp.concatenate([src[:, a:a + A_HEAD_DIM], src[:, b:b + A_HEAD_DIM]], axis=1)
                    dst_ref[rows, base + p * LANES:base + (p + 1) * LANES] = pair.astype(BF16)
            return carry
        lax.fori_loop(0, src_ref.shape[0] // LANES, body, 0)


def _inproj_attn_kernel(rope, x_ref, m_ref, g_ref, w32_ref, *rest):
    *rest, w_ref = rest
    _pair_cast_columns(w32_ref, w_ref)
    wq_ref = w_ref.at[:, 0:A_WIDTH]
    wkv_ref = w_ref.at[:, A_WIDTH:A_WIDTH + 2 * A_KV_WIDTH]
    wg_ref = w_ref.at[:, A_WIDTH + 2 * A_KV_WIDTH:]
    if rope:
        cos_ref, shi_ref, slo_ref, qt_ref, k_ref, vt_ref, sg_ref = rest
    else:
        qt_ref, k_ref, vt_ref, sg_ref = rest
    for r in range(x_ref.shape[1] // SUB_ROWS):
        rows = slice(r * SUB_ROWS, (r + 1) * SUB_ROWS)
        if rope:
            cos, shi, slo = cos_ref[rows], shi_ref[rows], slo_ref[rows]
        hb = _norm_mod(x_ref[0, rows], g_ref[...], m_ref[0]).astype(BF16)
        q = _dot(hb, wq_ref[...])
        for j in range(A_WIDTH // LANES):
            qj = q[:, j * LANES:(j + 1) * LANES]
            if rope:
                qj = _rope(qj, cos, shi, slo)
            qt_ref[0, j * LANES:(j + 1) * LANES, rows] = (qj * (Q_SCALE * LOG2_E)).T.astype(BF16)
        kv = _dot(hb, wkv_ref[...])
        k = kv[:, 0:A_KV_WIDTH]
        if rope:
            k = _rope(k, cos, shi, slo)
        k_ref[0, rows] = k.astype(BF16)
        vt_ref[0, :, rows] = kv[:, A_KV_WIDTH:].T.astype(BF16)
        g = _dot(hb, wg_ref[...])
        sg_ref[0, rows] = _silu(g).astype(BF16)


def _inproj_attn(x, mod3, norm_g, w_in, tables):
    b, s, _ = x.shape
    tm = min(A_ROW_TILE, s)
    in_specs = [
        pl.BlockSpec((1, tm, D), lambda bi, i: (bi, i, 0)),
        pl.BlockSpec((1, 1, 3 * D), lambda bi, i: (bi, 0, 0)),
        pl.BlockSpec((1, D), lambda bi, i: (0, 0)),
        pl.BlockSpec(w_in.shape, lambda bi, i: (0, 0), pipeline_mode=pl.Buffered(1)),
    ] + [pl.BlockSpec((tm, LANES), lambda bi, i: (i, 0))] * 3
    args = [x, mod3, norm_g.reshape(1, D), w_in, *tables]
    out_shape = (
        jax.ShapeDtypeStruct((b, A_WIDTH, s), BF16),
        jax.ShapeDtypeStruct((b, s, A_KV_WIDTH), BF16),
        jax.ShapeDtypeStruct((b, A_KV_WIDTH, s), BF16),
        jax.ShapeDtypeStruct((b, s, A_WIDTH), BF16),
    )
    out_specs = (
        pl.BlockSpec((1, A_WIDTH, tm), lambda bi, i: (bi, 0, i)),
        pl.BlockSpec((1, tm, A_KV_WIDTH), lambda bi, i: (bi, i, 0)),
        pl.BlockSpec((1, A_KV_WIDTH, tm), lambda bi, i: (bi, 0, i)),
        pl.BlockSpec((1, tm, A_WIDTH), lambda bi, i: (bi, i, 0)),
    )
    return pl.pallas_call(
        functools.partial(_inproj_attn_kernel, True),
        grid=(b, s // tm),
        in_specs=in_specs,
        out_specs=out_specs,
        out_shape=out_shape,
        scratch_shapes=[pltpu.VMEM(w_in.shape, BF16)],
        compiler_params=_params("arbitrary", "arbitrary"),
        name="inproj_attn",
    )(*args)


def _attn_block(kwin, vtw, qts, sinks, ok_prev, ok_next):
    n_keys, nq = kwin.shape[0], qts.shape[1]
    lane = lax.broadcasted_iota(jnp.int32, kwin.shape, 1)
    kbd = jnp.concatenate([jnp.where(lane < A_HEAD_DIM, kwin, jnp.zeros_like(kwin)),
                           jnp.where(lane >= A_HEAD_DIM, kwin, jnp.zeros_like(kwin))], axis=0)
    ones = jnp.where(lax.broadcasted_iota(jnp.int32, (SUM_ROWS, n_keys), 0) == 0, 1.0, 0.0).astype(BF16)

    st = _dot(kbd, qts)
    outs = []
    for hh in range(2):
        sh = st[hh * n_keys:(hh + 1) * n_keys]
        if ok_prev is not None:
            parts = [jnp.where(ok_prev, sh[0:BLOCK], NEG_INF),
                     sh[BLOCK:2 * BLOCK],
                     jnp.where(ok_next, sh[2 * BLOCK:3 * BLOCK], NEG_INF),
                     sh[3 * BLOCK:]]
        else:
            parts = [sh]
        mx = functools.reduce(jnp.maximum, [jnp.max(t, axis=0, keepdims=True) for t in parts])
        mx = jnp.maximum(mx, sinks[hh])
        probs = jnp.concatenate([jnp.exp2(t - mx).astype(BF16) for t in parts], axis=0)
        vt_h = jnp.concatenate([vtw[hh * A_HEAD_DIM:(hh + 1) * A_HEAD_DIM], ones], axis=0)
        ot = _dot(vt_h, probs)
        den = ot[A_HEAD_DIM:A_HEAD_DIM + 1] + jnp.exp2(sinks[hh] - mx)
        outs.append(ot[0:A_HEAD_DIM] * (1.0 / den))
    return jnp.concatenate(outs, axis=0)


def _pair_cast_rows(src_ref, dst_ref):
    @pl.when((pl.program_id(0) == 0) & (pl.program_id(1) == 0))
    def _():
        for p in range(A_GROUP):
            for half, h in enumerate((p, p + A_GROUP)):
                lo = p * LANES + half * A_HEAD_DIM
                dst_ref[lo:lo + A_HEAD_DIM, :] = src_ref[h * A_HEAD_DIM:(h + 1) * A_HEAD_DIM, :].astype(BF16)


def _attn_kernel(local, n_steps, sink_ref, qt_ref, *rest):
    *rest, w_ref = rest
    n_blk = rest[-1].shape[1] // BLOCK
    if local:
        (kp_ref, kc_ref, kn_ref, vp_ref, vc_ref, vn_ref, kx_ref, vx_ref,
         sg_ref, x_ref, m_ref, w32_ref, o_ref) = rest
        last = slice((n_blk - 1) * BLOCK, n_blk * BLOCK)
        k_blocks = ([kp_ref[0, last]] + [kc_ref[0, i * BLOCK:(i + 1) * BLOCK] for i in range(n_blk)]
                    + [kn_ref[0, 0:BLOCK]])
        v_blocks = ([vp_ref[0, :, last]] + [vc_ref[0, :, i * BLOCK:(i + 1) * BLOCK] for i in range(n_blk)]
                    + [vn_ref[0, :, 0:BLOCK]])
    else:
        kx_ref, vx_ref, sg_ref, x_ref, m_ref, w32_ref, o_ref = rest
    _pair_cast_rows(w32_ref, w_ref)
    step = pl.program_id(1)
    nq = A_GROUP * BLOCK
    chunk_of = lax.broadcasted_iota(jnp.int32, (1, nq), 1) // BLOCK
    sinks = []
    for hh in range(2):
        sk = jnp.full((1, nq), sink_ref[hh] * LOG2_E, F32)
        for c in range(1, A_GROUP):
            sk = jnp.where(chunk_of == c, sink_ref[2 * c + hh] * LOG2_E, sk)
        sinks.append(sk)
    if local:
        kj = lax.broadcasted_iota(jnp.int32, (BLOCK, nq), 0)
        qi = lax.broadcasted_iota(jnp.int32, (BLOCK, nq), 1) % BLOCK

    rows_out = []
    for blk in range(n_blk):
        qts = jnp.concatenate([qt_ref[0, c * LANES:(c + 1) * LANES, blk * BLOCK:(blk + 1) * BLOCK]
                               for c in range(A_GROUP)], axis=1)
        if local:
            kwin = jnp.concatenate(k_blocks[blk:blk + 3] + [kx_ref[0]], axis=0)
            vtw = jnp.concatenate(v_blocks[blk:blk + 3] + [vx_ref[0]], axis=1)
            ok_prev = (kj >= qi) & (step > 0) if blk == 0 else (kj >= qi)
            ok_next = (kj <= qi) & (step < n_steps - 1) if blk == n_blk - 1 else (kj <= qi)
        else:
            kwin, vtw, ok_prev, ok_next = kx_ref[0], vx_ref[0], None, None
        ot = _attn_block(kwin, vtw, qts, sinks, ok_prev, ok_next)
        rows = slice(blk * BLOCK, (blk + 1) * BLOCK)
        outs = []
        for c in range(A_GROUP):
            sg = sg_ref[0, rows, c * LANES:(c + 1) * LANES].astype(F32)
            outs.append((ot[:, c * BLOCK:(c + 1) * BLOCK].T * sg).astype(BF16))
        rows_out.append(jnp.concatenate(outs, axis=1))
    y = _dot(jnp.concatenate(rows_out, axis=0), w_ref[...])
    gate = m_ref[0][:, 2 * D:3 * D]
    o_ref[0] = x_ref[0] + gate * y


def _attention(sink, qt, k, vt, kx, vxt, sg, x, mod3, w_out):
    b, s, _ = sg.shape
    rows = A_BLOCKS_PER_STEP * BLOCK
    assert s % rows == 0
    ns = s // rows
    n_ctx = kx.shape[1]
    blk = lambda w: pl.BlockSpec((1, rows, w), lambda bi, i: (bi, i, 0))
    blk_t = lambda w: pl.BlockSpec((1, w, rows), lambda bi, i: (bi, 0, i))
    lo = lambda i: jnp.maximum(i - 1, 0)
    hi = lambda i: jnp.minimum(i + 1, ns - 1)
    in_specs = [pl.BlockSpec(memory_space=pltpu.SMEM), blk_t(A_WIDTH),
                pl.BlockSpec((1, rows, A_KV_WIDTH), lambda bi, i: (bi, lo(i), 0)),
                blk(A_KV_WIDTH),
                pl.BlockSpec((1, rows, A_KV_WIDTH), lambda bi, i: (bi, hi(i), 0)),
                pl.BlockSpec((1, A_KV_WIDTH, rows), lambda bi, i: (bi, 0, lo(i))),
                blk_t(A_KV_WIDTH),
                pl.BlockSpec((1, A_KV_WIDTH, rows), lambda bi, i: (bi, 0, hi(i))),
                pl.BlockSpec((1, n_ctx, A_KV_WIDTH), lambda bi, i: (bi, 0, 0)),
                pl.BlockSpec((1, A_KV_WIDTH, n_ctx), lambda bi, i: (bi, 0, 0)),
                blk(A_WIDTH), blk(D),
                pl.BlockSpec((1, 1, 3 * D), lambda bi, i: (bi, 0, 0)),
                pl.BlockSpec((A_WIDTH, D), lambda bi, i: (0, 0), pipeline_mode=pl.Buffered(1))]
    return pl.pallas_call(
        functools.partial(_attn_kernel, True, ns),
        grid=(b, ns),
        in_specs=in_specs,
        out_specs=blk(D),
        out_shape=jax.ShapeDtypeStruct((b, s, D), F32),
        scratch_shapes=[pltpu.VMEM((A_WIDTH, D), BF16)],
        compiler_params=_params("arbitrary", "arbitrary"),
        name="attn_local",
    )(sink, qt, k, k, k, vt, vt, vt, kx, vxt, sg, x, mod3, w_out)


def _ctx_attn_kernel(sink_ref, x_ref, m_ref, g_ref, win_ref, wout_ref, k_ref, vt_ref, o_ref,
                     win_scr, wout_scr, qt_scr, sg_scr):
    _inproj_attn_kernel(False, x_ref, m_ref, g_ref, win_ref, qt_scr, k_ref, vt_ref, sg_scr, win_scr)
    _attn_kernel(False, 1, sink_ref, qt_scr, k_ref, vt_ref, sg_scr, x_ref, m_ref, wout_ref, o_ref, wout_scr)


def _ctx_attention(sink, xc, mod3, mod_row, norm_g, w_in, w_out):
    b, n, _ = xc.shape
    const = lambda shape, **kw: pl.BlockSpec(shape, lambda bi, i: (0,) * len(shape), **kw)
    return pl.pallas_call(
        _ctx_attn_kernel,
        grid=(b, 1),
        in_specs=[pl.BlockSpec(memory_space=pltpu.SMEM),
                  pl.BlockSpec((1, n, D), lambda bi, i: (bi, 0, 0)),
                  pl.BlockSpec((1, 1, 3 * D), lambda bi, i: (mod_row, 0, 0)),
                  const((1, D)),
                  const(w_in.shape, pipeline_mode=pl.Buffered(1)),
                  const(w_out.shape, pipeline_mode=pl.Buffered(1))],
        out_specs=(pl.BlockSpec((1, n, A_KV_WIDTH), lambda bi, i: (bi, 0, 0)),
                   pl.BlockSpec((1, A_KV_WIDTH, n), lambda bi, i: (bi, 0, 0)),
                   pl.BlockSpec((1, n, D), lambda bi, i: (bi, 0, 0))),
        out_shape=(jax.ShapeDtypeStruct((b, n, A_KV_WIDTH), BF16),
                   jax.ShapeDtypeStruct((b, A_KV_WIDTH, n), BF16),
                   jax.ShapeDtypeStruct((b, n, D), F32)),
        scratch_shapes=[pltpu.VMEM(w_in.shape, BF16), pltpu.VMEM(w_out.shape, BF16),
                        pltpu.VMEM((1, A_WIDTH, n), BF16), pltpu.VMEM((1, n, A_WIDTH), BF16)],
        compiler_params=_params("arbitrary", "arbitrary"),
        name="ctx_attn",
    )(sink, xc, mod3, norm_g.reshape(1, D), w_in, w_out)


def _chunk_cumsum(x, tri):
    hi = x.astype(BF16)
    lo = (x - hi.astype(F32)).astype(BF16)
    return _dot(tri, hi) + _dot(tri, lo)


def _inproj_gla_kernel(chunk, with_q, x_ref, m_ref, g_ref, w32_ref, wa1_ref, wa2_ref, ba_ref, tri_ref, *outs):
    *outs, w_ref = outs
    _cast_once(w32_ref, w_ref)
    if with_q:
        per_dir = (outs[0:3], outs[3:6])
        v_ref, sg_ref = outs[6:]
    else:
        per_dir = ((outs[0],), (outs[1],))
    tm = x_ref.shape[1]
    hb = _norm_mod(x_ref[0], g_ref[...], m_ref[0]).astype(BF16)
    r = _dot(hb, wa1_ref[...])
    z = _dot(r.astype(BF16), wa2_ref[...]) + ba_ref[...]
    e = jnp.exp2(jnp.abs(z) * -LOG2_E)
    la = (jnp.minimum(z, 0.0) * LOG2_E - jnp.log2(1.0 + e)) * (1.0 / GATE_TEMP)
    k = _dot(hb, w_ref[:, B_K_WIDTH:2 * B_K_WIDTH])
    if with_q:
        q = _dot(hb, w_ref[:, 0:B_K_WIDTH]) * K_SCALE
    n_chunks = tm // chunk
    k_ends = []
    for reverse in (False, True):
        lad = la[:, B_K_WIDTH:] if reverse else la[:, 0:B_K_WIDTH]
        refs = per_dir[1] if reverse else per_dir[0]
        tots, kis, k_end = [], [], None
        tri = tri_ref[1 if reverse else 0]
        tb = tri.shape[0]
        cum_all = jnp.concatenate([_chunk_cumsum(lad[r * tb:(r + 1) * tb], tri) for r in range(tm // tb)], axis=0)
        for c in range(n_chunks):
            rows = slice(c * chunk, (c + 1) * chunk)
            cum = cum_all[rows]
            tot = cum[0:1] if reverse else cum[chunk - 1:chunk]
            if not with_q:
                k_end = (k[rows] * jnp.exp2(tot - cum)).astype(BF16)
                continue
            refs[0][0, rows] = (q[rows] * jnp.exp2(cum)).astype(BF16)
            kis.append(k[rows] * jnp.exp2(-cum))
            tots.append(tot)
            if c % 2 == 1:
                pair = slice((c - 1) * chunk, (c + 1) * chunk)
                refs[1][0, :, pair] = jnp.concatenate(kis[-2:], axis=0).T.astype(BF16)
        if with_q:
            pad = [jnp.zeros((DEC_ROWS - n_chunks, B_K_WIDTH), F32)] if n_chunks < DEC_ROWS else []
            refs[2][0, 0] = jnp.exp2(jnp.concatenate(tots + pad, axis=0))
        else:
            k_ends.append(k_end)
    if with_q:
        sg_ref[0] = _silu(_dot(hb, w_ref[:, 2 * B_K_WIDTH + B_V_WIDTH:])).astype(BF16)
    v = _dot(hb, w_ref[:, 2 * B_K_WIDTH:2 * B_K_WIDTH + B_V_WIDTH]).astype(BF16)
    if with_q:
        v_ref[0] = v
    else:
        for refs, k_end in zip(per_dir, k_ends):
            for h in range(B_HEADS):
                ks = slice(h * B_KEY_DIM, (h + 1) * B_KEY_DIM)
                refs[0][0, h] = _dot_tn(k_end[:, ks], v[:, h * B_VAL_DIM:(h + 1) * B_VAL_DIM])


def _inproj_gla(x, mod3, mod_row, norm_g, w, wa1, wa2, ba, chunk, with_q):
    b, s, _ = x.shape
    tm = min(ROW_TILE, s)
    nt = s // tm
    cpt = tm // chunk
    tb = max(chunk, MXU_DIM)
    assert tm % tb == 0 and tb % chunk == 0
    if mod_row is None:
        mod_map = lambda bi, i: (bi, 0, 0)
    else:
        mod_map = lambda bi, i: (mod_row, 0, 0)
    const = lambda shape: pl.BlockSpec(shape, lambda bi, i: (0,) * len(shape))
    in_specs = [
        pl.BlockSpec((1, tm, D), lambda bi, i: (bi, i, 0)),
        pl.BlockSpec((1, 1, 3 * D), mod_map),
        const((1, D)), pl.BlockSpec(w.shape, lambda bi, i: (0, 0), pipeline_mode=pl.Buffered(1)),
        const(wa1.shape), const(wa2.shape), const(ba.shape),
        const((2, tb, tb)),
    ]
    t_idx = np.arange(tb)
    same = (t_idx[:, None] // chunk) == (t_idx[None, :] // chunk)
    lower = same & (t_idx[None, :] <= t_idx[:, None])
    tri = jnp.asarray(np.stack([lower, lower.T]), dtype=BF16)
    rows = lambda w_: (jax.ShapeDtypeStruct((b, s, w_), BF16),
                       pl.BlockSpec((1, tm, w_), lambda bi, i: (bi, i, 0)))
    cols = (jax.ShapeDtypeStruct((b, B_K_WIDTH, s), BF16),
            pl.BlockSpec((1, B_K_WIDTH, tm), lambda bi, i: (bi, 0, i)))
    decs = (jax.ShapeDtypeStruct((b, nt, DEC_ROWS, B_K_WIDTH), F32),
            pl.BlockSpec((1, 1, DEC_ROWS, B_K_WIDTH), lambda bi, i: (bi, i, 0, 0)))
    if with_q:
        assert cpt <= DEC_ROWS and cpt % 2 == 0
        outs = [rows(B_K_WIDTH), cols, decs] * 2 + [rows(B_V_WIDTH)] * 2
    else:
        assert nt == 1 and cpt == 1
        state = (jax.ShapeDtypeStruct((b, B_HEADS, B_KEY_DIM, B_VAL_DIM), F32),
                 pl.BlockSpec((1, B_HEADS, B_KEY_DIM, B_VAL_DIM), lambda bi, i: (bi, 0, 0, 0)))
        outs = [state] * 2
    return pl.pallas_call(
        functools.partial(_inproj_gla_kernel, chunk, with_q),
        grid=(b, nt),
        in_specs=in_specs,
        out_specs=tuple(o[1] for o in outs),
        out_shape=tuple(o[0] for o in outs),
        scratch_shapes=[pltpu.VMEM(w.shape, BF16)],
        compiler_params=_params("arbitrary", "arbitrary"),
        name="inproj_gla" if with_q else "inproj_gla_ctx",
    )(x, mod3, norm_g.reshape(1, D), w, wa1, wa2, ba, tri)


def _scan_tile(reverse, qd_ref, kit_ref, dec_ref, v_ref, st_ref):
    pair_rows = 2 * CHUNK
    ti = lax.broadcasted_iota(jnp.int32, (pair_rows, pair_rows), 0)
    si = lax.broadcasted_iota(jnp.int32, (pair_rows, pair_rows), 1)
    same = (ti // CHUNK) == (si // CHUNK)
    if reverse:
        use_inv = same & (ti <= si)
        use_end = (ti < CHUNK) & (si >= CHUNK)
        second_row = lax.broadcasted_iota(jnp.int32, (pair_rows, B_KEY_DIM), 0) < CHUNK
        first_col = lax.broadcasted_iota(jnp.int32, (B_KEY_DIM, pair_rows), 1) >= CHUNK
    else:
        use_inv = same & (ti >= si)
        use_end = (ti >= CHUNK) & (si < CHUNK)
        second_row = lax.broadcasted_iota(jnp.int32, (pair_rows, B_KEY_DIM), 0) >= CHUNK
        first_col = lax.broadcasted_iota(jnp.int32, (B_KEY_DIM, pair_rows), 1) < CHUNK
    n_pairs = SCAN_TILE // pair_rows
    order = range(n_pairs - 1, -1, -1) if reverse else range(n_pairs)
    n_dec = dec_ref.shape[1] * DEC_ROWS
    dec = dec_ref[0].reshape(n_dec, B_K_WIDTH)
    dect = jnp.concatenate([dec, jnp.zeros((LANES - n_dec, B_K_WIDTH), F32)], axis=0).T
    heads = []
    for h in range(B_HEADS):
        ks = slice(h * B_KEY_DIM, (h + 1) * B_KEY_DIM)
        vs = slice(h * B_VAL_DIM, (h + 1) * B_VAL_DIM)
        st = st_ref[h]
        o_rows = [None] * n_pairs
        for p in order:
            rows = slice(p * pair_rows, (p + 1) * pair_rows)
            c_first, c_second = (2 * p + 1, 2 * p) if reverse else (2 * p, 2 * p + 1)
            qd = qd_ref[0, rows, ks]
            kit = kit_ref[0, ks, rows]
            vh = v_ref[0, rows, vs]
            d_first, d_second = dect[ks, c_first:c_first + 1], dect[ks, c_second:c_second + 1]
            ki32 = kit.astype(F32)
            s2 = _dot(qd, jnp.concatenate([kit, (ki32 * d_first).astype(BF16)], axis=1))
            a = jnp.where(use_inv, s2[:, 0:pair_rows], jnp.where(use_end, s2[:, pair_rows:], 0.0)).astype(BF16)
            q_pair = jnp.where(second_row, qd.astype(F32) * dec[c_first:c_first + 1, ks], qd.astype(F32))
            k_pair = ki32 * jnp.where(first_col, d_first * d_second, d_second)
            o_rows[p] = _dot(a, vh) + _dot(q_pair.astype(BF16), st.astype(BF16))
            st = st * (d_first * d_second) + _dot(k_pair.astype(BF16), vh)
        st_ref[h] = st
        heads.append(jnp.concatenate(o_rows, axis=0))
    return jnp.concatenate(heads, axis=1)


def _gla_scan_kernel(nt, qdf_ref, kitf_ref, decf_ref, qdb_ref, kitb_ref, decb_ref,
                     v_ref, sf_ref, sb_ref, sg_ref, x_ref, m_ref, hn_ref, w32_ref, fn_ref,
                     o_ref, st_ref, of_ref, w_ref):
    _cast_once(w32_ref, w_ref)
    j = pl.program_id(1)

    @pl.when(j == 0)
    def _():
        st_ref[...] = sf_ref[0]

    @pl.when(j == nt)
    def _():
        st_ref[...] = sb_ref[0]

    @pl.when(j < nt)
    def _():
        of_ref[j] = _scan_tile(False, qdf_ref, kitf_ref, decf_ref, v_ref, st_ref)

    @pl.when(j >= nt)
    def _():
        o = _scan_tile(True, qdb_ref, kitb_ref, decb_ref, v_ref, st_ref) + of_ref[2 * nt - 1 - j]
        normed = []
        for h in range(B_HEADS):
            oh = o[:, h * B_VAL_DIM:(h + 1) * B_VAL_DIM]
            normed.append(oh * lax.rsqrt(jnp.mean(oh * oh, axis=-1, keepdims=True) + EPS))
        of = jnp.concatenate(normed, axis=1) * hn_ref[...]
        y = _dot((of * sg_ref[0].astype(F32)).astype(BF16), w_ref[...])
        xn = x_ref[0] + m_ref[0][:, 2 * D:3 * D] * y
        ms = jnp.mean(xn * xn, axis=-1, keepdims=True)
        o_ref[0] = xn * lax.rsqrt(ms + EPS) * fn_ref[...]


def _gla_scan(fwd, bwd, v, s_f, s_b, sg, x, mod3, head_g, w_out, final_g):
    b, s, _ = v.shape
    assert SCAN_TILE % ROW_TILE == 0 and ROW_TILE // CHUNK == DEC_ROWS
    dec_blocks = SCAN_TILE // ROW_TILE
    nt = s // SCAN_TILE
    t_fwd = lambda j: jnp.minimum(j, nt - 1)
    t_bwd = lambda j: 2 * nt - 1 - jnp.maximum(j, nt)
    t_both = lambda j: jnp.where(j < nt, j, 2 * nt - 1 - j)

    def direction(t):
        rows = pl.BlockSpec((1, SCAN_TILE, B_K_WIDTH), lambda bi, j: (bi, t(j), 0))
        cols = pl.BlockSpec((1, B_K_WIDTH, SCAN_TILE), lambda bi, j: (bi, 0, t(j)))
        decs = pl.BlockSpec((1, dec_blocks, DEC_ROWS, B_K_WIDTH), lambda bi, j: (bi, t(j), 0, 0))
        return [rows, cols, decs]

    const = lambda shape: pl.BlockSpec(shape, lambda bi, j: (0,) * len(shape))
    state = pl.BlockSpec((1, B_HEADS, B_KEY_DIM, B_VAL_DIM), lambda bi, j: (bi, 0, 0, 0))
    out_rows = lambda w_: pl.BlockSpec((1, SCAN_TILE, w_), lambda bi, j: (bi, t_bwd(j), 0))
    in_specs = direction(t_fwd) + direction(t_bwd) + [
        pl.BlockSpec((1, SCAN_TILE, B_V_WIDTH), lambda bi, j: (bi, t_both(j), 0)),
        state, state, out_rows(B_V_WIDTH), out_rows(D),
        pl.BlockSpec((1, 1, 3 * D), lambda bi, j: (bi, 0, 0)),
        const((1, B_V_WIDTH)),
        pl.BlockSpec((B_V_WIDTH, D), lambda bi, j: (0, 0), pipeline_mode=pl.Buffered(1)),
        const((1, D))]
    return pl.pallas_call(
        functools.partial(_gla_scan_kernel, nt),
        grid=(b, 2 * nt),
        in_specs=in_specs,
        out_specs=out_rows(D),
        out_shape=jax.ShapeDtypeStruct((b, s, D), F32),
        scratch_shapes=[pltpu.VMEM((B_HEADS, B_KEY_DIM, B_VAL_DIM), F32),
                        pltpu.VMEM((nt, SCAN_TILE, B_V_WIDTH), F32),
                        pltpu.VMEM((B_V_WIDTH, D), BF16)],
        compiler_params=_params("arbitrary", "arbitrary", vmem_limit=SCAN_VMEM_LIMIT),
        name="gla_scan",
    )(*fwd, *bwd, v, s_f, s_b, sg, x, mod3, head_g.reshape(1, B_V_WIDTH), w_out, final_g.reshape(1, D))


def _rope_tables(n_tokens):
    rows_n = n_tokens // GRID_W
    row = np.repeat(np.arange(rows_n, dtype=np.float32), GRID_W)
    col = np.tile(np.arange(GRID_W, dtype=np.float32), rows_n)
    inv_freq = (np.float32(ROPE_BASE) ** (-np.arange(ROPE_FREQS, dtype=np.float32) / np.float32(ROPE_FREQS)))
    inv_freq = inv_freq.astype(np.float32)
    ang = np.stack([row[:, None] * inv_freq, col[:, None] * inv_freq], axis=1)
    cos, sin = np.cos(ang).astype(np.float32), np.sin(ang).astype(np.float32)
    zero = np.zeros_like(sin)
    tile = lambda t: np.tile(t.reshape(n_tokens, A_HEAD_DIM), (1, LANES // A_HEAD_DIM))
    return (tile(np.stack([cos, cos], axis=2)),
            tile(np.stack([-sin, zero], axis=2)),
            tile(np.stack([zero, sin], axis=2)))


def _pair_heads(t, axis):
    shape = t.shape
    t = t.reshape(shape[:axis] + (A_KV_HEADS, A_GROUP, -1) + shape[axis + 1:])
    return jnp.swapaxes(t, axis, axis + 1).reshape(shape)


def kernel(x, c, ctx, c_ctx, l0_norm_g, l0_w_ada, l0_b_ada, l0_w_in, l0_sink, l0_w_out, l1_norm_g, l1_w_ada, l1_b_ada, l1_w_in, l1_wa1_f, l1_wa2_f, l1_ba_f, l1_wa1_b, l1_wa2_b, l1_ba_b, l1_head_norm_g, l1_w_out, final_norm_g):
    b, s, _ = x.shape
    ctx_row = b

    cvec = jnp.concatenate([c, c_ctx[None, :], jnp.zeros((MOD_ROWS - b - 1, D), F32)], axis=0)
    mod0 = _modulation(cvec, l0_w_ada, l0_b_ada).reshape(MOD_ROWS, 1, 3 * D)
    mod1 = _modulation(cvec, l1_w_ada, l1_b_ada).reshape(MOD_ROWS, 1, 3 * D)

    sink = _pair_heads(l0_sink.astype(F32), 0)
    kc, vct, xc1 = _ctx_attention(sink, ctx, mod0, ctx_row, l0_norm_g, l0_w_in, l0_w_out)
    qt, k, vt, sg = _inproj_attn(x, mod0, l0_norm_g, l0_w_in, _rope_tables(s))
    x1 = _attention(sink, qt, k, vt, kc, vct, sg, x, mod0, l0_w_out)

    w1 = l1_w_in
    wa1 = jnp.concatenate([l1_wa1_f, l1_wa1_b, jnp.zeros((D, LANES - 2 * GATE_RANK), F32)], axis=1).astype(BF16)
    wa2 = jnp.zeros((LANES, 2 * B_K_WIDTH), F32)
    wa2 = wa2.at[0:GATE_RANK, 0:B_K_WIDTH].set(l1_wa2_f)
    wa2 = wa2.at[GATE_RANK:2 * GATE_RANK, B_K_WIDTH:].set(l1_wa2_b).astype(BF16)
    ba = jnp.concatenate([l1_ba_f, l1_ba_b]).reshape(1, 2 * B_K_WIDTH)
    s_f, s_b = _inproj_gla(xc1, mod1, ctx_row, l1_norm_g, w1, wa1, wa2, ba, ctx.shape[1], False)
    outs = _inproj_gla(x1, mod1, None, l1_norm_g, w1, wa1, wa2, ba, CHUNK, True)
    v1, sg1 = outs[6:]
    return _gla_scan(outs[0:3], outs[3:6], v1, s_f, s_b, sg1, x1, mod1,
                     l1_head_norm_g, l1_w_out, final_norm_g)
```

```python
import functools

import jax
import jax.numpy as jnp
import numpy as np
from jax import lax
from jax.experimental import pallas as pl
from jax.experimental.pallas import tpu as pltpu

F32 = jnp.float32
BF16 = jnp.bfloat16

D = 1024
GRID_W = 64
EPS = 1e-6
NEG_INF = -1e30

A_HEADS = 16
A_KV_HEADS = 2
A_GROUP = A_HEADS // A_KV_HEADS
A_HEAD_DIM = 64
A_WIDTH = A_HEADS * A_HEAD_DIM
A_KV_WIDTH = A_KV_HEADS * A_HEAD_DIM
BLOCK = 128
ROPE_BASE = 10000.0
ROPE_FREQS = A_HEAD_DIM // 4
Q_SCALE = A_HEAD_DIM ** -0.5
LOG2_E = 1.4426950408889634

B_HEADS = 4
B_K_WIDTH = D // 2
B_V_WIDTH = D
B_KEY_DIM = B_K_WIDTH // B_HEADS
B_VAL_DIM = B_V_WIDTH // B_HEADS
GATE_RANK = 16
GATE_TEMP = 16.0
CHUNK = 64
K_SCALE = B_KEY_DIM ** -0.5

LANES = 128
MXU_DIM = 256
MOD_ROWS = 16
ROW_TILE = 512
A_ROW_TILE = 1024
SCAN_TILE = 1024
SUB_ROWS = 256
DEC_ROWS = 8
SUM_ROWS = 16
A_BLOCKS_PER_STEP = 4
VMEM_LIMIT = 48 * 1024 * 1024
SCAN_VMEM_LIMIT = 56 * 1024 * 1024


def _params(*sem, vmem_limit=VMEM_LIMIT):
    return pltpu.CompilerParams(dimension_semantics=sem, vmem_limit_bytes=vmem_limit)


def _silu(x):
    return x / (1.0 + jnp.exp(-x))


def _dot(a, b):
    return jnp.dot(a, b, preferred_element_type=F32)


def _dot_nt(a, b):
    return lax.dot_general(a, b, (((1,), (1,)), ((), ())), preferred_element_type=F32)


def _dot_tn(a, b):
    return lax.dot_general(a, b, (((0,), (0,)), ((), ())), preferred_element_type=F32)


def _norm_mod(x, g, m):
    ms = jnp.mean(x * x, axis=-1, keepdims=True)
    y = x * lax.rsqrt(ms + EPS) * g
    return y * (1.0 + m[:, D:2 * D]) + m[:, 0:D]


def _cast_once(src_ref, dst_ref):
    @pl.when((pl.program_id(0) == 0) & (pl.program_id(1) == 0))
    def _():
        def body(i, carry):
            rows = pl.ds(pl.multiple_of(i * LANES, LANES), LANES)
            dst_ref[rows, :] = src_ref[rows, :].astype(dst_ref.dtype)
            return carry
        lax.fori_loop(0, src_ref.shape[0] // LANES, body, 0)


def _mod_kernel(c_ref, w_ref, b_ref, o_ref):
    s = _silu(c_ref[...])
    o_ref[...] = _dot(s.astype(BF16), w_ref[...].astype(BF16)) + b_ref[...]


def _modulation(cvec, w_ada, b_ada):
    n = w_ada.shape[1] // D
    return pl.pallas_call(
        _mod_kernel,
        grid=(n,),
        in_specs=[
            pl.BlockSpec((MOD_ROWS, D), lambda j: (0, 0)),
            pl.BlockSpec((D, D), lambda j: (0, j)),
            pl.BlockSpec((1, D), lambda j: (0, j)),
        ],
        out_specs=pl.BlockSpec((MOD_ROWS, D), lambda j: (0, j)),
        out_shape=jax.ShapeDtypeStruct((MOD_ROWS, n * D), F32),
        compiler_params=_params("arbitrary"),
        name="modulation",
    )(cvec, w_ada, b_ada.reshape(1, n * D))


def _rope(t, cos, sin_hi, sin_lo):
    return t * cos + pltpu.roll(t, LANES - ROPE_FREQS, 1) * sin_hi + pltpu.roll(t, ROPE_FREQS, 1) * sin_lo


def _pair_cast_columns(src_ref, dst_ref):
    @pl.when((pl.program_id(0) == 0) & (pl.program_id(1) == 0))
    def _():
        kv_lo, kv_hi = A_WIDTH, A_WIDTH + 2 * A_KV_WIDTH

        def body(i, carry):
            rows = pl.ds(pl.multiple_of(i * LANES, LANES), LANES)
            src = src_ref[rows, :]
            dst_ref[rows, kv_lo:kv_hi] = src[:, kv_lo:kv_hi].astype(BF16)
            for base in (0, kv_hi):
                for p in range(A_GROUP):
                    a = base + p * A_HEAD_DIM
                    b = base + (p + A_GROUP) * A_HEAD_DIM
                    pair = jnp.concatenate([src[:, a:a + A_HEAD_DIM], src[:, b:b + A_HEAD_DIM]], axis=1)
                    dst_ref[rows, base + p * LANES:base + (p + 1) * LANES] = pair.astype(BF16)
            return carry
        lax.fori_loop(0, src_ref.shape[0] // LANES, body, 0)


def _inproj_attn_kernel(rope, x_ref, m_ref, g_ref, w32_ref, *rest):
    *rest, w_ref = rest
    _pair_cast_columns(w32_ref, w_ref)
    wq_ref = w_ref.at[:, 0:A_WIDTH]
    wkv_ref = w_ref.at[:, A_WIDTH:A_WIDTH + 2 * A_KV_WIDTH]
    wg_ref = w_ref.at[:, A_WIDTH + 2 * A_KV_WIDTH:]
    if rope:
        cos_ref, shi_ref, slo_ref, qt_ref, k_ref, vt_ref, sg_ref = rest
    else:
        qt_ref, k_ref, vt_ref, sg_ref = rest
    for r in range(x_ref.shape[1] // SUB_ROWS):
        rows = slice(r * SUB_ROWS, (r + 1) * SUB_ROWS)
        if rope:
            cos, shi, slo = cos_ref[rows], shi_ref[rows], slo_ref[rows]
        hb = _norm_mod(x_ref[0, rows], g_ref[...], m_ref[0]).astype(BF16)
        q = _dot(hb, wq_ref[...])
        for j in range(A_WIDTH // LANES):
            qj = q[:, j * LANES:(j + 1) * LANES]
            if rope:
                qj = _rope(qj, cos, shi, slo)
            qt_ref[0, j * LANES:(j + 1) * LANES, rows] = (qj * (Q_SCALE * LOG2_E)).T.astype(BF16)
        kv = _dot(hb, wkv_ref[...])
        k = kv[:, 0:A_KV_WIDTH]
        if rope:
            k = _rope(k, cos, shi, slo)
        k_ref[0, rows] = k.astype(BF16)
        vt_ref[0, :, rows] = kv[:, A_KV_WIDTH:].T.astype(BF16)
        g = _dot(hb, wg_ref[...])
        sg_ref[0, rows] = _silu(g).astype(BF16)


def _inproj_attn(x, mod3, norm_g, w_in, tables):
    b, s, _ = x.shape
    tm = min(A_ROW_TILE, s)
    in_specs = [
        pl.BlockSpec((1, tm, D), lambda bi, i: (bi, i, 0)),
        pl.BlockSpec((1, 1, 3 * D), lambda bi, i: (bi, 0, 0)),
        pl.BlockSpec((1, D), lambda bi, i: (0, 0)),
        pl.BlockSpec(w_in.shape, lambda bi, i: (0, 0), pipeline_mode=pl.Buffered(1)),
    ] + [pl.BlockSpec((tm, LANES), lambda bi, i: (i, 0))] * 3
    args = [x, mod3, norm_g.reshape(1, D), w_in, *tables]
    out_shape = (
        jax.ShapeDtypeStruct((b, A_WIDTH, s), BF16),
        jax.ShapeDtypeStruct((b, s, A_KV_WIDTH), BF16),
        jax.ShapeDtypeStruct((b, A_KV_WIDTH, s), BF16),
        jax.ShapeDtypeStruct((b, s, A_WIDTH), BF16),
    )
    out_specs = (
        pl.BlockSpec((1, A_WIDTH, tm), lambda bi, i: (bi, 0, i)),
        pl.BlockSpec((1, tm, A_KV_WIDTH), lambda bi, i: (bi, i, 0)),
        pl.BlockSpec((1, A_KV_WIDTH, tm), lambda bi, i: (bi, 0, i)),
        pl.BlockSpec((1, tm, A_WIDTH), lambda bi, i: (bi, i, 0)),
    )
    return pl.pallas_call(
        functools.partial(_inproj_attn_kernel, True),
        grid=(b, s // tm),
        in_specs=in_specs,
        out_specs=out_specs,
        out_shape=out_shape,
        scratch_shapes=[pltpu.VMEM(w_in.shape, BF16)],
        compiler_params=_params("arbitrary", "arbitrary"),
        name="inproj_attn",
    )(*args)


def _attn_block(kwin, vtw, qts, sinks, ok_prev, ok_next):
    n_keys, nq = kwin.shape[0], qts.shape[1]
    lane = lax.broadcasted_iota(jnp.int32, kwin.shape, 1)
    kbd = jnp.concatenate([jnp.where(lane < A_HEAD_DIM, kwin, jnp.zeros_like(kwin)),
                           jnp.where(lane >= A_HEAD_DIM, kwin, jnp.zeros_like(kwin))], axis=0)
    ones = jnp.where(lax.broadcasted_iota(jnp.int32, (SUM_ROWS, n_keys), 0) == 0, 1.0, 0.0).astype(BF16)

    st = _dot(kbd, qts).astype(BF16)
    neg = jnp.asarray(NEG_INF, BF16)
    outs = []
    for hh in range(2):
        sh = st[hh * n_keys:(hh + 1) * n_keys]
        if ok_prev is not None:
            parts = [jnp.where(ok_prev, sh[0:BLOCK], neg),
                     sh[BLOCK:2 * BLOCK],
                     jnp.where(ok_next, sh[2 * BLOCK:3 * BLOCK], neg),
                     sh[3 * BLOCK:]]
        else:
            parts = [sh]
        mx = functools.reduce(jnp.maximum, [jnp.max(t, axis=0, keepdims=True) for t in parts])
        mxb = jnp.maximum(mx.astype(F32), sinks[hh]).astype(BF16)
        mx = mxb.astype(F32)
        probs = jnp.concatenate([jnp.exp2(t - mxb) for t in parts], axis=0)
        vt_h = jnp.concatenate([vtw[hh * A_HEAD_DIM:(hh + 1) * A_HEAD_DIM], ones], axis=0)
        ot = _dot(vt_h, probs)
        den = ot[A_HEAD_DIM:A_HEAD_DIM + 1] + jnp.exp2(sinks[hh] - mx)
        outs.append(ot[0:A_HEAD_DIM] * (1.0 / den))
    return jnp.concatenate(outs, axis=0)


def _pair_cast_rows(src_ref, dst_ref):
    @pl.when((pl.program_id(0) == 0) & (pl.program_id(1) == 0))
    def _():
        for p in range(A_GROUP):
            for half, h in enumerate((p, p + A_GROUP)):
                lo = p * LANES + half * A_HEAD_DIM
                dst_ref[lo:lo + A_HEAD_DIM, :] = src_ref[h * A_HEAD_DIM:(h + 1) * A_HEAD_DIM, :].astype(BF16)


def _attn_kernel(local, n_steps, sink_ref, qt_ref, *rest):
    *rest, w_ref = rest
    n_blk = rest[-1].shape[1] // BLOCK
    if local:
        (kp_ref, kc_ref, kn_ref, vp_ref, vc_ref, vn_ref, kx_ref, vx_ref,
         sg_ref, x_ref, m_ref, w32_ref, o_ref) = rest
        last = slice((n_blk - 1) * BLOCK, n_blk * BLOCK)
        k_blocks = ([kp_ref[0, last]] + [kc_ref[0, i * BLOCK:(i + 1) * BLOCK] for i in range(n_blk)]
                    + [kn_ref[0, 0:BLOCK]])
        v_blocks = ([vp_ref[0, :, last]] + [vc_ref[0, :, i * BLOCK:(i + 1) * BLOCK] for i in range(n_blk)]
                    + [vn_ref[0, :, 0:BLOCK]])
    else:
        kx_ref, vx_ref, sg_ref, x_ref, m_ref, w32_ref, o_ref = rest
    _pair_cast_rows(w32_ref, w_ref)
    step = pl.program_id(1)
    nq = A_GROUP * BLOCK
    chunk_of = lax.broadcasted_iota(jnp.int32, (1, nq), 1) // BLOCK
    sinks = []
    for hh in range(2):
        sk = jnp.full((1, nq), sink_ref[hh] * LOG2_E, F32)
        for c in range(1, A_GROUP):
            sk = jnp.where(chunk_of == c, sink_ref[2 * c + hh] * LOG2_E, sk)
        sinks.append(sk)
    if local:
        kj = lax.broadcasted_iota(jnp.int32, (BLOCK, nq), 0)
        qi = lax.broadcasted_iota(jnp.int32, (BLOCK, nq), 1) % BLOCK

    rows_out = []
    for blk in range(n_blk):
        qts = jnp.concatenate([qt_ref[0, c * LANES:(c + 1) * LANES, blk * BLOCK:(blk + 1) * BLOCK]
                               for c in range(A_GROUP)], axis=1)
        if local:
            kwin = jnp.concatenate(k_blocks[blk:blk + 3] + [kx_ref[0]], axis=0)
            vtw = jnp.concatenate(v_blocks[blk:blk + 3] + [vx_ref[0]], axis=1)
            ok_prev = (kj >= qi) & (step > 0) if blk == 0 else (kj >= qi)
            ok_next = (kj <= qi) & (step < n_steps - 1) if blk == n_blk - 1 else (kj <= qi)
        else:
            kwin, vtw, ok_prev, ok_next = kx_ref[0], vx_ref[0], None, None
        ot = _attn_block(kwin, vtw, qts, sinks, ok_prev, ok_next)
        rows = slice(blk * BLOCK, (blk + 1) * BLOCK)
        outs = []
        for c in range(A_GROUP):
            sg = sg_ref[0, rows, c * LANES:(c + 1) * LANES].astype(F32)
            outs.append((ot[:, c * BLOCK:(c + 1) * BLOCK].T * sg).astype(BF16))
        rows_out.append(jnp.concatenate(outs, axis=1))
    y = _dot(jnp.concatenate(rows_out, axis=0), w_ref[...])
    gate = m_ref[0][:, 2 * D:3 * D]
    o_ref[0] = x_ref[0] + gate * y


def _attention(sink, qt, k, vt, kx, vxt, sg, x, mod3, w_out):
    b, s, _ = sg.shape
    rows = A_BLOCKS_PER_STEP * BLOCK
    assert s % rows == 0
    ns = s // rows
    n_ctx = kx.shape[1]
    blk = lambda w: pl.BlockSpec((1, rows, w), lambda bi, i: (bi, i, 0))
    blk_t = lambda w: pl.BlockSpec((1, w, rows), lambda bi, i: (bi, 0, i))
    lo = lambda i: jnp.maximum(i - 1, 0)
    hi = lambda i: jnp.minimum(i + 1, ns - 1)
    in_specs = [pl.BlockSpec(memory_space=pltpu.SMEM), blk_t(A_WIDTH),
                pl.BlockSpec((1, rows, A_KV_WIDTH), lambda bi, i: (bi, lo(i), 0)),
                blk(A_KV_WIDTH),
                pl.BlockSpec((1, rows, A_KV_WIDTH), lambda bi, i: (bi, hi(i), 0)),
                pl.BlockSpec((1, A_KV_WIDTH, rows), lambda bi, i: (bi, 0, lo(i))),
                blk_t(A_KV_WIDTH),
                pl.BlockSpec((1, A_KV_WIDTH, rows), lambda bi, i: (bi, 0, hi(i))),
                pl.BlockSpec((1, n_ctx, A_KV_WIDTH), lambda bi, i: (bi, 0, 0)),
                pl.BlockSpec((1, A_KV_WIDTH, n_ctx), lambda bi, i: (bi, 0, 0)),
                blk(A_WIDTH), blk(D),
                pl.BlockSpec((1, 1, 3 * D), lambda bi, i: (bi, 0, 0)),
                pl.BlockSpec((A_WIDTH, D), lambda bi, i: (0, 0), pipeline_mode=pl.Buffered(1))]
    return pl.pallas_call(
        functools.partial(_attn_kernel, True, ns),
        grid=(b, ns),
        in_specs=in_specs,
        out_specs=blk(D),
        out_shape=jax.ShapeDtypeStruct((b, s, D), F32),
        scratch_shapes=[pltpu.VMEM((A_WIDTH, D), BF16)],
        compiler_params=_params("arbitrary", "arbitrary"),
        name="attn_local",
    )(sink, qt, k, k, k, vt, vt, vt, kx, vxt, sg, x, mod3, w_out)


def _ctx_attn_kernel(sink_ref, x_ref, m_ref, g_ref, win_ref, wout_ref, k_ref, vt_ref, o_ref,
                     win_scr, wout_scr, qt_scr, sg_scr):
    _inproj_attn_kernel(False, x_ref, m_ref, g_ref, win_ref, qt_scr, k_ref, vt_ref, sg_scr, win_scr)
    _attn_kernel(False, 1, sink_ref, qt_scr, k_ref, vt_ref, sg_scr, x_ref, m_ref, wout_ref, o_ref, wout_scr)


def _ctx_attention(sink, xc, mod3, mod_row, norm_g, w_in, w_out):
    b, n, _ = xc.shape
    const = lambda shape, **kw: pl.BlockSpec(shape, lambda bi, i: (0,) * len(shape), **kw)
    return pl.pallas_call(
        _ctx_attn_kernel,
        grid=(b, 1),
        in_specs=[pl.BlockSpec(memory_space=pltpu.SMEM),
                  pl.BlockSpec((1, n, D), lambda bi, i: (bi, 0, 0)),
                  pl.BlockSpec((1, 1, 3 * D), lambda bi, i: (mod_row, 0, 0)),
                  const((1, D)),
                  const(w_in.shape, pipeline_mode=pl.Buffered(1)),
                  const(w_out.shape, pipeline_mode=pl.Buffered(1))],
        out_specs=(pl.BlockSpec((1, n, A_KV_WIDTH), lambda bi, i: (bi, 0, 0)),
                   pl.BlockSpec((1, A_KV_WIDTH, n), lambda bi, i: (bi, 0, 0)),
                   pl.BlockSpec((1, n, D), lambda bi, i: (bi, 0, 0))),
        out_shape=(jax.ShapeDtypeStruct((b, n, A_KV_WIDTH), BF16),
                   jax.ShapeDtypeStruct((b, A_KV_WIDTH, n), BF16),
                   jax.ShapeDtypeStruct((b, n, D), F32)),
        scratch_shapes=[pltpu.VMEM(w_in.shape, BF16), pltpu.VMEM(w_out.shape, BF16),
                        pltpu.VMEM((1, A_WIDTH, n), BF16), pltpu.VMEM((1, n, A_WIDTH), BF16)],
        compiler_params=_params("arbitrary", "arbitrary"),
        name="ctx_attn",
    )(sink, xc, mod3, norm_g.reshape(1, D), w_in, w_out)


def _chunk_cumsum(x, tri):
    hi = x.astype(BF16)
    lo = (x - hi.astype(F32)).astype(BF16)
    return _dot(tri, hi) + _dot(tri, lo)


def _inproj_gla_kernel(chunk, with_q, x_ref, m_ref, g_ref, w32_ref, wa1_ref, wa2_ref, ba_ref, tri_ref, *outs):
    *outs, w_ref = outs
    _cast_once(w32_ref, w_ref)
    if with_q:
        per_dir = (outs[0:3], outs[3:6])
        v_ref, sg_ref = outs[6:]
    else:
        per_dir = ((outs[0],), (outs[1],))
    tm = x_ref.shape[1]
    hb = _norm_mod(x_ref[0], g_ref[...], m_ref[0]).astype(BF16)
    r = _dot(hb, wa1_ref[...])
    z = _dot(r.astype(BF16), wa2_ref[...]) + ba_ref[...]
    e = jnp.exp2(jnp.abs(z) * -LOG2_E)
    la = (jnp.minimum(z, 0.0) * LOG2_E - jnp.log2(1.0 + e)) * (1.0 / GATE_TEMP)
    k = _dot(hb, w_ref[:, B_K_WIDTH:2 * B_K_WIDTH])
    if with_q:
        q = _dot(hb, w_ref[:, 0:B_K_WIDTH]) * K_SCALE
    n_chunks = tm // chunk
    k_ends = []
    for reverse in (False, True):
        lad = la[:, B_K_WIDTH:] if reverse else la[:, 0:B_K_WIDTH]
        refs = per_dir[1] if reverse else per_dir[0]
        tots, kis, k_end = [], [], None
        tri = tri_ref[1 if reverse else 0]
        tb = tri.shape[0]
        cum_all = jnp.concatenate([_chunk_cumsum(lad[r * tb:(r + 1) * tb], tri) for r in range(tm // tb)], axis=0)
        for c in range(n_chunks):
            rows = slice(c * chunk, (c + 1) * chunk)
            cum = cum_all[rows]
            tot = cum[0:1] if reverse else cum[chunk - 1:chunk]
            if not with_q:
                k_end = (k[rows] * jnp.exp2(tot - cum)).astype(BF16)
                continue
            refs[0][0, rows] = (q[rows] * jnp.exp2(cum)).astype(BF16)
            kis.append(k[rows] * jnp.exp2(-cum))
            tots.append(tot)
            if c % 2 == 1:
                pair = slice((c - 1) * chunk, (c + 1) * chunk)
                refs[1][0, :, pair] = jnp.concatenate(kis[-2:], axis=0).T.astype(BF16)
        if with_q:
            pad = [jnp.zeros((DEC_ROWS - n_chunks, B_K_WIDTH), F32)] if n_chunks < DEC_ROWS else []
            refs[2][0, 0] = jnp.exp2(jnp.concatenate(tots + pad, axis=0))
        else:
            k_ends.append(k_end)
    if with_q:
        sg_ref[0] = _silu(_dot(hb, w_ref[:, 2 * B_K_WIDTH + B_V_WIDTH:])).astype(BF16)
    v = _dot(hb, w_ref[:, 2 * B_K_WIDTH:2 * B_K_WIDTH + B_V_WIDTH]).astype(BF16)
    if with_q:
        v_ref[0] = v
    else:
        for refs, k_end in zip(per_dir, k_ends):
            for h in range(B_HEADS):
                ks = slice(h * B_KEY_DIM, (h + 1) * B_KEY_DIM)
                refs[0][0, h] = _dot_tn(k_end[:, ks], v[:, h * B_VAL_DIM:(h + 1) * B_VAL_DIM])


def _inproj_gla(x, mod3, mod_row, norm_g, w, wa1, wa2, ba, chunk, with_q):
    b, s, _ = x.shape
    tm = min(ROW_TILE, s)
    nt = s // tm
    cpt = tm // chunk
    tb = max(chunk, MXU_DIM)
    assert tm % tb == 0 and tb % chunk == 0
    if mod_row is None:
        mod_map = lambda bi, i: (bi, 0, 0)
    else:
        mod_map = lambda bi, i: (mod_row, 0, 0)
    const = lambda shape: pl.BlockSpec(shape, lambda bi, i: (0,) * len(shape))
    in_specs = [
        pl.BlockSpec((1, tm, D), lambda bi, i: (bi, i, 0)),
        pl.BlockSpec((1, 1, 3 * D), mod_map),
        const((1, D)), pl.BlockSpec(w.shape, lambda bi, i: (0, 0), pipeline_mode=pl.Buffered(1)),
        const(wa1.shape), const(wa2.shape), const(ba.shape),
        const((2, tb, tb)),
    ]
    t_idx = np.arange(tb)
    same = (t_idx[:, None] // chunk) == (t_idx[None, :] // chunk)
    lower = same & (t_idx[None, :] <= t_idx[:, None])
    tri = jnp.asarray(np.stack([lower, lower.T]), dtype=BF16)
    rows = lambda w_: (jax.ShapeDtypeStruct((b, s, w_), BF16),
                       pl.BlockSpec((1, tm, w_), lambda bi, i: (bi, i, 0)))
    cols = (jax.ShapeDtypeStruct((b, B_K_WIDTH, s), BF16),
            pl.BlockSpec((1, B_K_WIDTH, tm), lambda bi, i: (bi, 0, i)))
    decs = (jax.ShapeDtypeStruct((b, nt, DEC_ROWS, B_K_WIDTH), F32),
            pl.BlockSpec((1, 1, DEC_ROWS, B_K_WIDTH), lambda bi, i: (bi, i, 0, 0)))
    if with_q:
        assert cpt <= DEC_ROWS and cpt % 2 == 0
        outs = [rows(B_K_WIDTH), cols, decs] * 2 + [rows(B_V_WIDTH)] * 2
    else:
        assert nt == 1 and cpt == 1
        state = (jax.ShapeDtypeStruct((b, B_HEADS, B_KEY_DIM, B_VAL_DIM), F32),
                 pl.BlockSpec((1, B_HEADS, B_KEY_DIM, B_VAL_DIM), lambda bi, i: (bi, 0, 0, 0)))
        outs = [state] * 2
    return pl.pallas_call(
        functools.partial(_inproj_gla_kernel, chunk, with_q),
        grid=(b, nt),
        in_specs=in_specs,
        out_specs=tuple(o[1] for o in outs),
        out_shape=tuple(o[0] for o in outs),
        scratch_shapes=[pltpu.VMEM(w.shape, BF16)],
        compiler_params=_params("arbitrary", "arbitrary"),
        name="inproj_gla" if with_q else "inproj_gla_ctx",
    )(x, mod3, norm_g.reshape(1, D), w, wa1, wa2, ba, tri)


def _scan_tile(reverse, qd_ref, kit_ref, dec_ref, v_ref, st_ref):
    pair_rows = 2 * CHUNK
    ti = lax.broadcasted_iota(jnp.int32, (pair_rows, pair_rows), 0)
    si = lax.broadcasted_iota(jnp.int32, (pair_rows, pair_rows), 1)
    same = (ti // CHUNK) == (si // CHUNK)
    if reverse:
        use_inv = same & (ti <= si)
        use_end = (ti < CHUNK) & (si >= CHUNK)
        second_row = lax.broadcasted_iota(jnp.int32, (pair_rows, B_KEY_DIM), 0) < CHUNK
        first_col = lax.broadcasted_iota(jnp.int32, (B_KEY_DIM, pair_rows), 1) >= CHUNK
    else:
        use_inv = same & (ti >= si)
        use_end = (ti >= CHUNK) & (si < CHUNK)
        second_row = lax.broadcasted_iota(jnp.int32, (pair_rows, B_KEY_DIM), 0) >= CHUNK
        first_col = lax.broadcasted_iota(jnp.int32, (B_KEY_DIM, pair_rows), 1) < CHUNK
    n_pairs = SCAN_TILE // pair_rows
    order = range(n_pairs - 1, -1, -1) if reverse else range(n_pairs)
    n_dec = dec_ref.shape[1] * DEC_ROWS
    dec = dec_ref[0].reshape(n_dec, B_K_WIDTH)
    dect = jnp.concatenate([dec, jnp.zeros((LANES - n_dec, B_K_WIDTH), F32)], axis=0).T
    heads = []
    for h in range(B_HEADS):
        ks = slice(h * B_KEY_DIM, (h + 1) * B_KEY_DIM)
        vs = slice(h * B_VAL_DIM, (h + 1) * B_VAL_DIM)
        st = st_ref[h]
        o_rows = [None] * n_pairs
        for p in order:
            rows = slice(p * pair_rows, (p + 1) * pair_rows)
            c_first, c_second = (2 * p + 1, 2 * p) if reverse else (2 * p, 2 * p + 1)
            qd = qd_ref[0, rows, ks]
            kit = kit_ref[0, ks, rows]
            vh = v_ref[0, rows, vs]
            d_first, d_second = dect[ks, c_first:c_first + 1], dect[ks, c_second:c_second + 1]
            ki32 = kit.astype(F32)
            s2 = _dot(qd, jnp.concatenate([kit, (ki32 * d_first).astype(BF16)], axis=1))
            a = jnp.where(use_inv, s2[:, 0:pair_rows], jnp.where(use_end, s2[:, pair_rows:], 0.0)).astype(BF16)
            q_pair = jnp.where(second_row, qd.astype(F32) * dec[c_first:c_first + 1, ks], qd.astype(F32))
            k_pair = ki32 * jnp.where(first_col, d_first * d_second, d_second)
            o_rows[p] = _dot(a, vh) + _dot(q_pair.astype(BF16), st.astype(BF16))
            st = st * (d_first * d_second) + _dot(k_pair.astype(BF16), vh)
        st_ref[h] = st
        heads.append(jnp.concatenate(o_rows, axis=0))
    return jnp.concatenate(heads, axis=1)


def _gla_scan_kernel(nt, qdf_ref, kitf_ref, decf_ref, qdb_ref, kitb_ref, decb_ref,
                     v_ref, sf_ref, sb_ref, sg_ref, x_ref, m_ref, hn_ref, w32_ref, fn_ref,
                     o_ref, st_ref, of_ref, w_ref):
    _cast_once(w32_ref, w_ref)
    j = pl.program_id(1)

    @pl.when(j == 0)
    def _():
        st_ref[...] = sf_ref[0]

    @pl.when(j == nt)
    def _():
        st_ref[...] = sb_ref[0]

    @pl.when(j < nt)
    def _():
        of_ref[j] = _scan_tile(False, qdf_ref, kitf_ref, decf_ref, v_ref, st_ref)

    @pl.when(j >= nt)
    def _():
        o = _scan_tile(True, qdb_ref, kitb_ref, decb_ref, v_ref, st_ref) + of_ref[2 * nt - 1 - j]
        normed = []
        for h in range(B_HEADS):
            oh = o[:, h * B_VAL_DIM:(h + 1) * B_VAL_DIM]
            normed.append(oh * lax.rsqrt(jnp.mean(oh * oh, axis=-1, keepdims=True) + EPS))
        of = jnp.concatenate(normed, axis=1) * hn_ref[...]
        y = _dot((of * sg_ref[0].astype(F32)).astype(BF16), w_ref[...])
        xn = x_ref[0] + m_ref[0][:, 2 * D:3 * D] * y
        ms = jnp.mean(xn * xn, axis=-1, keepdims=True)
        o_ref[0] = xn * lax.rsqrt(ms + EPS) * fn_ref[...]


def _gla_scan(fwd, bwd, v, s_f, s_b, sg, x, mod3, head_g, w_out, final_g):
    b, s, _ = v.shape
    assert SCAN_TILE % ROW_TILE == 0 and ROW_TILE // CHUNK == DEC_ROWS
    dec_blocks = SCAN_TILE // ROW_TILE
    nt = s // SCAN_TILE
    t_fwd = lambda j: jnp.minimum(j, nt - 1)
    t_bwd = lambda j: 2 * nt - 1 - jnp.maximum(j, nt)
    t_both = lambda j: jnp.where(j < nt, j, 2 * nt - 1 - j)

    def direction(t):
        rows = pl.BlockSpec((1, SCAN_TILE, B_K_WIDTH), lambda bi, j: (bi, t(j), 0))
        cols = pl.BlockSpec((1, B_K_WIDTH, SCAN_TILE), lambda bi, j: (bi, 0, t(j)))
        decs = pl.BlockSpec((1, dec_blocks, DEC_ROWS, B_K_WIDTH), lambda bi, j: (bi, t(j), 0, 0))
        return [rows, cols, decs]

    const = lambda shape: pl.BlockSpec(shape, lambda bi, j: (0,) * len(shape))
    state = pl.BlockSpec((1, B_HEADS, B_KEY_DIM, B_VAL_DIM), lambda bi, j: (bi, 0, 0, 0))
    out_rows = lambda w_: pl.BlockSpec((1, SCAN_TILE, w_), lambda bi, j: (bi, t_bwd(j), 0))
    in_specs = direction(t_fwd) + direction(t_bwd) + [
        pl.BlockSpec((1, SCAN_TILE, B_V_WIDTH), lambda bi, j: (bi, t_both(j), 0)),
        state, state, out_rows(B_V_WIDTH), out_rows(D),
        pl.BlockSpec((1, 1, 3 * D), lambda bi, j: (bi, 0, 0)),
        const((1, B_V_WIDTH)),
        pl.BlockSpec((B_V_WIDTH, D), lambda bi, j: (0, 0), pipeline_mode=pl.Buffered(1)),
        const((1, D))]
    return pl.pallas_call(
        functools.partial(_gla_scan_kernel, nt),
        grid=(b, 2 * nt),
        in_specs=in_specs,
        out_specs=out_rows(D),
        out_shape=jax.ShapeDtypeStruct((b, s, D), F32),
        scratch_shapes=[pltpu.VMEM((B_HEADS, B_KEY_DIM, B_VAL_DIM), F32),
                        pltpu.VMEM((nt, SCAN_TILE, B_V_WIDTH), F32),
                        pltpu.VMEM((B_V_WIDTH, D), BF16)],
        compiler_params=_params("arbitrary", "arbitrary", vmem_limit=SCAN_VMEM_LIMIT),
        name="gla_scan",
    )(*fwd, *bwd, v, s_f, s_b, sg, x, mod3, head_g.reshape(1, B_V_WIDTH), w_out, final_g.reshape(1, D))


def _rope_tables(n_tokens):
    rows_n = n_tokens // GRID_W
    row = np.repeat(np.arange(rows_n, dtype=np.float32), GRID_W)
    col = np.tile(np.arange(GRID_W, dtype=np.float32), rows_n)
    inv_freq = (np.float32(ROPE_BASE) ** (-np.arange(ROPE_FREQS, dtype=np.float32) / np.float32(ROPE_FREQS)))
    inv_freq = inv_freq.astype(np.float32)
    ang = np.stack([row[:, None] * inv_freq, col[:, None] * inv_freq], axis=1)
    cos, sin = np.cos(ang).astype(np.float32), np.sin(ang).astype(np.float32)
    zero = np.zeros_like(sin)
    tile = lambda t: np.tile(t.reshape(n_tokens, A_HEAD_DIM), (1, LANES // A_HEAD_DIM))
    return (tile(np.stack([cos, cos], axis=2)),
            tile(np.stack([-sin, zero], axis=2)),
            tile(np.stack([zero, sin], axis=2)))


def _pair_heads(t, axis):
    shape = t.shape
    t = t.reshape(shape[:axis] + (A_KV_HEADS, A_GROUP, -1) + shape[axis + 1:])
    return jnp.swapaxes(t, axis, axis + 1).reshape(shape)


def kernel(x, c, ctx, c_ctx, l0_norm_g, l0_w_ada, l0_b_ada, l0_w_in, l0_sink, l0_w_out, l1_norm_g, l1_w_ada, l1_b_ada, l1_w_in, l1_wa1_f, l1_wa2_f, l1_ba_f, l1_wa1_b, l1_wa2_b, l1_ba_b, l1_head_norm_g, l1_w_out, final_norm_g):
    b, s, _ = x.shape
    ctx_row = b

    cvec = jnp.concatenate([c, c_ctx[None, :], jnp.zeros((MOD_ROWS - b - 1, D), F32)], axis=0)
    mod0 = _modulation(cvec, l0_w_ada, l0_b_ada).reshape(MOD_ROWS, 1, 3 * D)
    mod1 = _modulation(cvec, l1_w_ada, l1_b_ada).reshape(MOD_ROWS, 1, 3 * D)

    sink = _pair_heads(l0_sink.astype(F32), 0)
    kc, vct, xc1 = _ctx_attention(sink, ctx, mod0, ctx_row, l0_norm_g, l0_w_in, l0_w_out)
    qt, k, vt, sg = _inproj_attn(x, mod0, l0_norm_g, l0_w_in, _rope_tables(s))
    x1 = _attention(sink, qt, k, vt, kc, vct, sg, x, mod0, l0_w_out)

    w1 = l1_w_in
    wa1 = jnp.concatenate([l1_wa1_f, l1_wa1_b, jnp.zeros((D, LANES - 2 * GATE_RANK), F32)], axis=1).astype(BF16)
    wa2 = jnp.zeros((LANES, 2 * B_K_WIDTH), F32)
    wa2 = wa2.at[0:GATE_RANK, 0:B_K_WIDTH].set(l1_wa2_f)
    wa2 = wa2.at[GATE_RANK:2 * GATE_RANK, B_K_WIDTH:].set(l1_wa2_b).astype(BF16)
    ba = jnp.concatenate([l1_ba_f, l1_ba_b]).reshape(1, 2 * B_K_WIDTH)
    s_f, s_b = _inproj_gla(xc1, mod1, ctx_row, l1_norm_g, w1, wa1, wa2, ba, ctx.shape[1], False)
    outs = _inproj_gla(x1, mod1, None, l1_norm_g, w1, wa1, wa2, ba, CHUNK, True)
    v1, sg1 = outs[6:]
    return _gla_scan(outs[0:3], outs[3:6], v1, s_f, s_b, sg1, x1, mod1,
                     l1_head_norm_g, l1_w_out, final_norm_g)
```

```python
import functools

import jax
import jax.numpy as jnp
import numpy as np
from jax import lax
from jax.experimental import pallas as pl
from jax.experimental.pallas import tpu as pltpu

F32 = jnp.float32
BF16 = jnp.bfloat16

D = 1024
GRID_W = 64
EPS = 1e-6
NEG_INF = -1e30

A_HEADS = 16
A_KV_HEADS = 2
A_GROUP = A_HEADS // A_KV_HEADS
A_HEAD_DIM = 64
A_WIDTH = A_HEADS * A_HEAD_DIM
A_KV_WIDTH = A_KV_HEADS * A_HEAD_DIM
BLOCK = 128
ROPE_BASE = 10000.0
ROPE_FREQS = A_HEAD_DIM // 4
Q_SCALE = A_HEAD_DIM ** -0.5
LOG2_E = 1.4426950408889634

B_HEADS = 4
B_K_WIDTH = D // 2
B_V_WIDTH = D
B_KEY_DIM = B_K_WIDTH // B_HEADS
B_VAL_DIM = B_V_WIDTH // B_HEADS
GATE_RANK = 16
GATE_TEMP = 16.0
CHUNK = 64
K_SCALE = B_KEY_DIM ** -0.5

LANES = 128
MXU_DIM = 256
MOD_ROWS = 16
ROW_TILE = 512
A_ROW_TILE = 1024
SCAN_TILE = 1024
SUB_ROWS = 256
DEC_ROWS = 8
SUM_ROWS = 16
A_BLOCKS_PER_STEP = 8
VMEM_LIMIT = 48 * 1024 * 1024
SCAN_VMEM_LIMIT = 56 * 1024 * 1024


def _params(*sem, vmem_limit=VMEM_LIMIT):
    return pltpu.CompilerParams(dimension_semantics=sem, vmem_limit_bytes=vmem_limit)


def _silu(x):
    return x / (1.0 + jnp.exp(-x))


def _dot(a, b):
    return jnp.dot(a, b, preferred_element_type=F32)


def _dot_nt(a, b):
    return lax.dot_general(a, b, (((1,), (1,)), ((), ())), preferred_element_type=F32)


def _dot_tn(a, b):
    return lax.dot_general(a, b, (((0,), (0,)), ((), ())), preferred_element_type=F32)


def _norm_mod(x, g, m):
    ms = jnp.mean(x * x, axis=-1, keepdims=True)
    y = x * lax.rsqrt(ms + EPS) * g
    return y * (1.0 + m[:, D:2 * D]) + m[:, 0:D]


def _cast_once(src_ref, dst_ref):
    @pl.when((pl.program_id(0) == 0) & (pl.program_id(1) == 0))
    def _():
        def body(i, carry):
            rows = pl.ds(pl.multiple_of(i * LANES, LANES), LANES)
            dst_ref[rows, :] = src_ref[rows, :].astype(dst_ref.dtype)
            return carry
        lax.fori_loop(0, src_ref.shape[0] // LANES, body, 0)


def _mod_kernel(c_ref, w_ref, b_ref, o_ref):
    s = _silu(c_ref[...])
    o_ref[...] = _dot(s.astype(BF16), w_ref[...].astype(BF16)) + b_ref[...]


def _modulation(cvec, w_ada, b_ada):
    n = w_ada.shape[1] // D
    return pl.pallas_call(
        _mod_kernel,
        grid=(n,),
        in_specs=[
            pl.BlockSpec((MOD_ROWS, D), lambda j: (0, 0)),
            pl.BlockSpec((D, D), lambda j: (0, j)),
            pl.BlockSpec((1, D), lambda j: (0, j)),
        ],
        out_specs=pl.BlockSpec((MOD_ROWS, D), lambda j: (0, j)),
        out_shape=jax.ShapeDtypeStruct((MOD_ROWS, n * D), F32),
        compiler_params=_params("arbitrary"),
        name="modulation",
    )(cvec, w_ada, b_ada.reshape(1, n * D))


def _rope(t, cos, sin_hi, sin_lo):
    return t * cos + pltpu.roll(t, LANES - ROPE_FREQS, 1) * sin_hi + pltpu.roll(t, ROPE_FREQS, 1) * sin_lo


def _pair_cast_columns(src_ref, dst_ref):
    @pl.when((pl.program_id(0) == 0) & (pl.program_id(1) == 0))
    def _():
        kv_lo, kv_hi = A_WIDTH, A_WIDTH + 2 * A_KV_WIDTH

        def body(i, carry):
            rows = pl.ds(pl.multiple_of(i * LANES, LANES), LANES)
            src = src_ref[rows, :]
            dst_ref[rows, kv_lo:kv_hi] = src[:, kv_lo:kv_hi].astype(BF16)
            for base in (0, kv_hi):
                for p in range(A_GROUP):
                    a = base + p * A_HEAD_DIM
                    b = base + (p + A_GROUP) * A_HEAD_DIM
                    pair = jnp.concatenate([src[:, a:a + A_HEAD_DIM], src[:, b:b + A_HEAD_DIM]], axis=1)
                    dst_ref[rows, base + p * LANES:base + (p + 1) * LANES] = pair.astype(BF16)
            return carry
        lax.fori_loop(0, src_ref.shape[0] // LANES, body, 0)


def _inproj_attn_kernel(rope, x_ref, m_ref, g_ref, w32_ref, *rest):
    *rest, w_ref = rest
    _pair_cast_columns(w32_ref, w_ref)
    wq_ref = w_ref.at[:, 0:A_WIDTH]
    wkv_ref = w_ref.at[:, A_WIDTH:A_WIDTH + 2 * A_KV_WIDTH]
    wg_ref = w_ref.at[:, A_WIDTH + 2 * A_KV_WIDTH:]
    if rope:
        cos_ref, shi_ref, slo_ref, qt_ref, k_ref, vt_ref, sg_ref = rest
    else:
        qt_ref, k_ref, vt_ref, sg_ref = rest
    for r in range(x_ref.shape[1] // SUB_ROWS):
        rows = slice(r * SUB_ROWS, (r + 1) * SUB_ROWS)
        if rope:
            cos, shi, slo = cos_ref[rows], shi_ref[rows], slo_ref[rows]
        hb = _norm_mod(x_ref[0, rows], g_ref[...], m_ref[0]).astype(BF16)
        q = _dot(hb, wq_ref[...])
        for j in range(A_WIDTH // LANES):
            qj = q[:, j * LANES:(j + 1) * LANES]
            if rope:
                qj = _rope(qj, cos, shi, slo)
            qt_ref[0, j * LANES:(j + 1) * LANES, rows] = (qj * (Q_SCALE * LOG2_E)).T.astype(BF16)
        kv = _dot(hb, wkv_ref[...])
        k = kv[:, 0:A_KV_WIDTH]
        if rope:
            k = _rope(k, cos, shi, slo)
        k_ref[0, rows] = k.astype(BF16)
        vt_ref[0, :, rows] = kv[:, A_KV_WIDTH:].T.astype(BF16)
        g = _dot(hb, wg_ref[...])
        sg_ref[0, rows] = _silu(g).astype(BF16)


def _inproj_attn(x, mod3, norm_g, w_in, tables):
    b, s, _ = x.shape
    tm = min(A_ROW_TILE, s)
    in_specs = [
        pl.BlockSpec((1, tm, D), lambda bi, i: (bi, i, 0)),
        pl.BlockSpec((1, 1, 3 * D), lambda bi, i: (bi, 0, 0)),
        pl.BlockSpec((1, D), lambda bi, i: (0, 0)),
        pl.BlockSpec(w_in.shape, lambda bi, i: (0, 0), pipeline_mode=pl.Buffered(1)),
    ] + [pl.BlockSpec((tm, LANES), lambda bi, i: (i, 0))] * 3
    args = [x, mod3, norm_g.reshape(1, D), w_in, *tables]
    out_shape = (
        jax.ShapeDtypeStruct((b, A_WIDTH, s), BF16),
        jax.ShapeDtypeStruct((b, s, A_KV_WIDTH), BF16),
        jax.ShapeDtypeStruct((b, A_KV_WIDTH, s), BF16),
        jax.ShapeDtypeStruct((b, s, A_WIDTH), BF16),
    )
    out_specs = (
        pl.BlockSpec((1, A_WIDTH, tm), lambda bi, i: (bi, 0, i)),
        pl.BlockSpec((1, tm, A_KV_WIDTH), lambda bi, i: (bi, i, 0)),
        pl.BlockSpec((1, A_KV_WIDTH, tm), lambda bi, i: (bi, 0, i)),
        pl.BlockSpec((1, tm, A_WIDTH), lambda bi, i: (bi, i, 0)),
    )
    return pl.pallas_call(
        functools.partial(_inproj_attn_kernel, True),
        grid=(b, s // tm),
        in_specs=in_specs,
        out_specs=out_specs,
        out_shape=out_shape,
        scratch_shapes=[pltpu.VMEM(w_in.shape, BF16)],
        compiler_params=_params("arbitrary", "arbitrary"),
        name="inproj_attn",
    )(*args)


def _attn_block(kwin, vtw, qts, sinks, ok_prev, ok_next):
    n_keys, nq = kwin.shape[0], qts.shape[1]
    lane = lax.broadcasted_iota(jnp.int32, kwin.shape, 1)
    kbd = jnp.concatenate([jnp.where(lane < A_HEAD_DIM, kwin, jnp.zeros_like(kwin)),
                           jnp.where(lane >= A_HEAD_DIM, kwin, jnp.zeros_like(kwin))], axis=0)
    ones = jnp.where(lax.broadcasted_iota(jnp.int32, (SUM_ROWS, n_keys), 0) == 0, 1.0, 0.0).astype(BF16)

    st = _dot(kbd, qts).astype(BF16)
    neg = jnp.asarray(NEG_INF, BF16)
    outs = []
    for hh in range(2):
        sh = st[hh * n_keys:(hh + 1) * n_keys]
        if ok_prev is not None:
            parts = [jnp.where(ok_prev, sh[0:BLOCK], neg),
                     sh[BLOCK:2 * BLOCK],
                     jnp.where(ok_next, sh[2 * BLOCK:3 * BLOCK], neg),
                     sh[3 * BLOCK:]]
        else:
            parts = [sh]
        mx = functools.reduce(jnp.maximum, [jnp.max(t, axis=0, keepdims=True) for t in parts])
        mxb = jnp.maximum(mx.astype(F32), sinks[hh]).astype(BF16)
        mx = mxb.astype(F32)
        probs = jnp.concatenate([jnp.exp2(t - mxb) for t in parts], axis=0)
        vt_h = jnp.concatenate([vtw[hh * A_HEAD_DIM:(hh + 1) * A_HEAD_DIM], ones], axis=0)
        ot = _dot(vt_h, probs)
        den = ot[A_HEAD_DIM:A_HEAD_DIM + 1] + jnp.exp2(sinks[hh] - mx)
        outs.append(ot[0:A_HEAD_DIM] * (1.0 / den))
    return jnp.concatenate(outs, axis=0)


def _pair_cast_rows(src_ref, dst_ref):
    @pl.when((pl.program_id(0) == 0) & (pl.program_id(1) == 0))
    def _():
        for p in range(A_GROUP):
            for half, h in enumerate((p, p + A_GROUP)):
                lo = p * LANES + half * A_HEAD_DIM
                dst_ref[lo:lo + A_HEAD_DIM, :] = src_ref[h * A_HEAD_DIM:(h + 1) * A_HEAD_DIM, :].astype(BF16)


def _attn_kernel(local, n_steps, sink_ref, qt_ref, *rest):
    *rest, w_ref = rest
    n_blk = rest[-1].shape[1] // BLOCK
    if local:
        (kp_ref, kc_ref, kn_ref, vp_ref, vc_ref, vn_ref, kx_ref, vx_ref,
         sg_ref, x_ref, m_ref, w32_ref, o_ref) = rest
        last = slice((n_blk - 1) * BLOCK, n_blk * BLOCK)
        k_blocks = ([kp_ref[0, last]] + [kc_ref[0, i * BLOCK:(i + 1) * BLOCK] for i in range(n_blk)]
                    + [kn_ref[0, 0:BLOCK]])
        v_blocks = ([vp_ref[0, :, last]] + [vc_ref[0, :, i * BLOCK:(i + 1) * BLOCK] for i in range(n_blk)]
                    + [vn_ref[0, :, 0:BLOCK]])
    else:
        kx_ref, vx_ref, sg_ref, x_ref, m_ref, w32_ref, o_ref = rest
    _pair_cast_rows(w32_ref, w_ref)
    step = pl.program_id(1)
    nq = A_GROUP * BLOCK
    chunk_of = lax.broadcasted_iota(jnp.int32, (1, nq), 1) // BLOCK
    sinks = []
    for hh in range(2):
        sk = jnp.full((1, nq), sink_ref[hh] * LOG2_E, F32)
        for c in range(1, A_GROUP):
            sk = jnp.where(chunk_of == c, sink_ref[2 * c + hh] * LOG2_E, sk)
        sinks.append(sk)
    if local:
        kj = lax.broadcasted_iota(jnp.int32, (BLOCK, nq), 0)
        qi = lax.broadcasted_iota(jnp.int32, (BLOCK, nq), 1) % BLOCK

    rows_out = []
    for blk in range(n_blk):
        qts = jnp.concatenate([qt_ref[0, c * LANES:(c + 1) * LANES, blk * BLOCK:(blk + 1) * BLOCK]
                               for c in range(A_GROUP)], axis=1)
        if local:
            kwin = jnp.concatenate(k_blocks[blk:blk + 3] + [kx_ref[0]], axis=0)
            vtw = jnp.concatenate(v_blocks[blk:blk + 3] + [vx_ref[0]], axis=1)
            ok_prev = (kj >= qi) & (step > 0) if blk == 0 else (kj >= qi)
            ok_next = (kj <= qi) & (step < n_steps - 1) if blk == n_blk - 1 else (kj <= qi)
        else:
            kwin, vtw, ok_prev, ok_next = kx_ref[0], vx_ref[0], None, None
        ot = _attn_block(kwin, vtw, qts, sinks, ok_prev, ok_next)
        rows = slice(blk * BLOCK, (blk + 1) * BLOCK)
        outs = []
        for c in range(A_GROUP):
            sg = sg_ref[0, rows, c * LANES:(c + 1) * LANES].astype(F32)
            outs.append((ot[:, c * BLOCK:(c + 1) * BLOCK].T * sg).astype(BF16))
        rows_out.append(jnp.concatenate(outs, axis=1))
    y = _dot(jnp.concatenate(rows_out, axis=0), w_ref[...])
    gate = m_ref[0][:, 2 * D:3 * D]
    o_ref[0] = x_ref[0] + gate * y


def _attention(sink, qt, k, vt, kx, vxt, sg, x, mod3, w_out):
    b, s, _ = sg.shape
    rows = A_BLOCKS_PER_STEP * BLOCK
    assert s % rows == 0
    ns = s // rows
    n_ctx = kx.shape[1]
    blk = lambda w: pl.BlockSpec((1, rows, w), lambda bi, i: (bi, i, 0))
    blk_t = lambda w: pl.BlockSpec((1, w, rows), lambda bi, i: (bi, 0, i))
    lo = lambda i: jnp.maximum(i - 1, 0)
    hi = lambda i: jnp.minimum(i + 1, ns - 1)
    in_specs = [pl.BlockSpec(memory_space=pltpu.SMEM), blk_t(A_WIDTH),
                pl.BlockSpec((1, rows, A_KV_WIDTH), lambda bi, i: (bi, lo(i), 0)),
                blk(A_KV_WIDTH),
                pl.BlockSpec((1, rows, A_KV_WIDTH), lambda bi, i: (bi, hi(i), 0)),
                pl.BlockSpec((1, A_KV_WIDTH, rows), lambda bi, i: (bi, 0, lo(i))),
                blk_t(A_KV_WIDTH),
                pl.BlockSpec((1, A_KV_WIDTH, rows), lambda bi, i: (bi, 0, hi(i))),
                pl.BlockSpec((1, n_ctx, A_KV_WIDTH), lambda bi, i: (bi, 0, 0)),
                pl.BlockSpec((1, A_KV_WIDTH, n_ctx), lambda bi, i: (bi, 0, 0)),
                blk(A_WIDTH), blk(D),
                pl.BlockSpec((1, 1, 3 * D), lambda bi, i: (bi, 0, 0)),
                pl.BlockSpec((A_WIDTH, D), lambda bi, i: (0, 0), pipeline_mode=pl.Buffered(1))]
    return pl.pallas_call(
        functools.partial(_attn_kernel, True, ns),
        grid=(b, ns),
        in_specs=in_specs,
        out_specs=blk(D),
        out_shape=jax.ShapeDtypeStruct((b, s, D), F32),
        scratch_shapes=[pltpu.VMEM((A_WIDTH, D), BF16)],
        compiler_params=_params("arbitrary", "arbitrary"),
        name="attn_local",
    )(sink, qt, k, k, k, vt, vt, vt, kx, vxt, sg, x, mod3, w_out)


def _ctx_attn_kernel(sink_ref, x_ref, m_ref, g_ref, win_ref, wout_ref, k_ref, vt_ref, o_ref,
                     win_scr, wout_scr, qt_scr, sg_scr):
    _inproj_attn_kernel(False, x_ref, m_ref, g_ref, win_ref, qt_scr, k_ref, vt_ref, sg_scr, win_scr)
    _attn_kernel(False, 1, sink_ref, qt_scr, k_ref, vt_ref, sg_scr, x_ref, m_ref, wout_ref, o_ref, wout_scr)


def _ctx_attention(sink, xc, mod3, mod_row, norm_g, w_in, w_out):
    b, n, _ = xc.shape
    const = lambda shape, **kw: pl.BlockSpec(shape, lambda bi, i: (0,) * len(shape), **kw)
    return pl.pallas_call(
        _ctx_attn_kernel,
        grid=(b, 1),
        in_specs=[pl.BlockSpec(memory_space=pltpu.SMEM),
                  pl.BlockSpec((1, n, D), lambda bi, i: (bi, 0, 0)),
                  pl.BlockSpec((1, 1, 3 * D), lambda bi, i: (mod_row, 0, 0)),
                  const((1, D)),
                  const(w_in.shape, pipeline_mode=pl.Buffered(1)),
                  const(w_out.shape, pipeline_mode=pl.Buffered(1))],
        out_specs=(pl.BlockSpec((1, n, A_KV_WIDTH), lambda bi, i: (bi, 0, 0)),
                   pl.BlockSpec((1, A_KV_WIDTH, n), lambda bi, i: (bi, 0, 0)),
                   pl.BlockSpec((1, n, D), lambda bi, i: (bi, 0, 0))),
        out_shape=(jax.ShapeDtypeStruct((b, n, A_KV_WIDTH), BF16),
                   jax.ShapeDtypeStruct((b, A_KV_WIDTH, n), BF16),
                   jax.ShapeDtypeStruct((b, n, D), F32)),
        scratch_shapes=[pltpu.VMEM(w_in.shape, BF16), pltpu.VMEM(w_out.shape, BF16),
                        pltpu.VMEM((1, A_WIDTH, n), BF16), pltpu.VMEM((1, n, A_WIDTH), BF16)],
        compiler_params=_params("arbitrary", "arbitrary"),
        name="ctx_attn",
    )(sink, xc, mod3, norm_g.reshape(1, D), w_in, w_out)


def _chunk_cumsum(x, tri):
    hi = x.astype(BF16)
    lo = (x - hi.astype(F32)).astype(BF16)
    return _dot(tri, hi) + _dot(tri, lo)


def _inproj_gla_kernel(chunk, with_q, x_ref, m_ref, g_ref, w32_ref, wa1_ref, wa2_ref, ba_ref, tri_ref, *outs):
    *outs, w_ref = outs
    _cast_once(w32_ref, w_ref)
    if with_q:
        per_dir = (outs[0:3], outs[3:6])
        v_ref, sg_ref = outs[6:]
    else:
        per_dir = ((outs[0],), (outs[1],))
    tm = x_ref.shape[1]
    hb = _norm_mod(x_ref[0], g_ref[...], m_ref[0]).astype(BF16)
    r = _dot(hb, wa1_ref[...])
    z = _dot(r.astype(BF16), wa2_ref[...]) + ba_ref[...]
    e = jnp.exp2(jnp.abs(z) * -LOG2_E)
    la = (jnp.minimum(z, 0.0) * LOG2_E - jnp.log2(1.0 + e)) * (1.0 / GATE_TEMP)
    k = _dot(hb, w_ref[:, B_K_WIDTH:2 * B_K_WIDTH])
    if with_q:
        q = _dot(hb, w_ref[:, 0:B_K_WIDTH]) * K_SCALE
    n_chunks = tm // chunk
    k_ends = []
    for reverse in (False, True):
        lad = la[:, B_K_WIDTH:] if reverse else la[:, 0:B_K_WIDTH]
        refs = per_dir[1] if reverse else per_dir[0]
        tots, kis, k_end = [], [], None
        tri = tri_ref[1 if reverse else 0]
        tb = tri.shape[0]
        cum_all = jnp.concatenate([_chunk_cumsum(lad[r * tb:(r + 1) * tb], tri) for r in range(tm // tb)], axis=0)
        for c in range(n_chunks):
            rows = slice(c * chunk, (c + 1) * chunk)
            cum = cum_all[rows]
            tot = cum[0:1] if reverse else cum[chunk - 1:chunk]
            if not with_q:
                k_end = (k[rows] * jnp.exp2(tot - cum)).astype(BF16)
                continue
            refs[0][0, rows] = (q[rows] * jnp.exp2(cum)).astype(BF16)
            kis.append(k[rows] * jnp.exp2(-cum))
            tots.append(tot)
            if c % 2 == 1:
                pair = slice((c - 1) * chunk, (c + 1) * chunk)
                refs[1][0, :, pair] = jnp.concatenate(kis[-2:], axis=0).T.astype(BF16)
        if with_q:
            pad = [jnp.zeros((DEC_ROWS - n_chunks, B_K_WIDTH), F32)] if n_chunks < DEC_ROWS else []
            refs[2][0, 0] = jnp.exp2(jnp.concatenate(tots + pad, axis=0))
        else:
            k_ends.append(k_end)
    if with_q:
        sg_ref[0] = _silu(_dot(hb, w_ref[:, 2 * B_K_WIDTH + B_V_WIDTH:])).astype(BF16)
    v = _dot(hb, w_ref[:, 2 * B_K_WIDTH:2 * B_K_WIDTH + B_V_WIDTH]).astype(BF16)
    if with_q:
        v_ref[0] = v
    else:
        for refs, k_end in zip(per_dir, k_ends):
            for h in range(B_HEADS):
                ks = slice(h * B_KEY_DIM, (h + 1) * B_KEY_DIM)
                refs[0][0, h] = _dot_tn(k_end[:, ks], v[:, h * B_VAL_DIM:(h + 1) * B_VAL_DIM])


def _inproj_gla(x, mod3, mod_row, norm_g, w, wa1, wa2, ba, chunk, with_q):
    b, s, _ = x.shape
    tm = min(ROW_TILE, s)
    nt = s // tm
    cpt = tm // chunk
    tb = max(chunk, MXU_DIM)
    assert tm % tb == 0 and tb % chunk == 0
    if mod_row is None:
        mod_map = lambda bi, i: (bi, 0, 0)
    else:
        mod_map = lambda bi, i: (mod_row, 0, 0)
    const = lambda shape: pl.BlockSpec(shape, lambda bi, i: (0,) * len(shape))
    in_specs = [
        pl.BlockSpec((1, tm, D), lambda bi, i: (bi, i, 0)),
        pl.BlockSpec((1, 1, 3 * D), mod_map),
        const((1, D)), pl.BlockSpec(w.shape, lambda bi, i: (0, 0), pipeline_mode=pl.Buffered(1)),
        const(wa1.shape), const(wa2.shape), const(ba.shape),
        const((2, tb, tb)),
    ]
    t_idx = np.arange(tb)
    same = (t_idx[:, None] // chunk) == (t_idx[None, :] // chunk)
    lower = same & (t_idx[None, :] <= t_idx[:, None])
    tri = jnp.asarray(np.stack([lower, lower.T]), dtype=BF16)
    rows = lambda w_: (jax.ShapeDtypeStruct((b, s, w_), BF16),
                       pl.BlockSpec((1, tm, w_), lambda bi, i: (bi, i, 0)))
    cols = (jax.ShapeDtypeStruct((b, B_K_WIDTH, s), BF16),
            pl.BlockSpec((1, B_K_WIDTH, tm), lambda bi, i: (bi, 0, i)))
    decs = (jax.ShapeDtypeStruct((b, nt, DEC_ROWS, B_K_WIDTH), F32),
            pl.BlockSpec((1, 1, DEC_ROWS, B_K_WIDTH), lambda bi, i: (bi, i, 0, 0)))
    if with_q:
        assert cpt <= DEC_ROWS and cpt % 2 == 0
        outs = [rows(B_K_WIDTH), cols, decs] * 2 + [rows(B_V_WIDTH)] * 2
    else:
        assert nt == 1 and cpt == 1
        state = (jax.ShapeDtypeStruct((b, B_HEADS, B_KEY_DIM, B_VAL_DIM), F32),
                 pl.BlockSpec((1, B_HEADS, B_KEY_DIM, B_VAL_DIM), lambda bi, i: (bi, 0, 0, 0)))
        outs = [state] * 2
    return pl.pallas_call(
        functools.partial(_inproj_gla_kernel, chunk, with_q),
        grid=(b, nt),
        in_specs=in_specs,
        out_specs=tuple(o[1] for o in outs),
        out_shape=tuple(o[0] for o in outs),
        scratch_shapes=[pltpu.VMEM(w.shape, BF16)],
        compiler_params=_params("arbitrary", "arbitrary"),
        name="inproj_gla" if with_q else "inproj_gla_ctx",
    )(x, mod3, norm_g.reshape(1, D), w, wa1, wa2, ba, tri)


def _scan_tile(reverse, qd_ref, kit_ref, dec_ref, v_ref, st_ref):
    pair_rows = 2 * CHUNK
    ti = lax.broadcasted_iota(jnp.int32, (pair_rows, pair_rows), 0)
    si = lax.broadcasted_iota(jnp.int32, (pair_rows, pair_rows), 1)
    same = (ti // CHUNK) == (si // CHUNK)
    if reverse:
        use_inv = same & (ti <= si)
        use_end = (ti < CHUNK) & (si >= CHUNK)
        second_row = lax.broadcasted_iota(jnp.int32, (pair_rows, B_KEY_DIM), 0) < CHUNK
        first_col = lax.broadcasted_iota(jnp.int32, (B_KEY_DIM, pair_rows), 1) >= CHUNK
    else:
        use_inv = same & (ti >= si)
        use_end = (ti >= CHUNK) & (si < CHUNK)
        second_row = lax.broadcasted_iota(jnp.int32, (pair_rows, B_KEY_DIM), 0) >= CHUNK
        first_col = lax.broadcasted_iota(jnp.int32, (B_KEY_DIM, pair_rows), 1) < CHUNK
    n_pairs = SCAN_TILE // pair_rows
    order = range(n_pairs - 1, -1, -1) if reverse else range(n_pairs)
    n_dec = dec_ref.shape[1] * DEC_ROWS
    dec = dec_ref[0].reshape(n_dec, B_K_WIDTH)
    dect = jnp.concatenate([dec, jnp.zeros((LANES - n_dec, B_K_WIDTH), F32)], axis=0).T
    heads = []
    for h in range(B_HEADS):
        ks = slice(h * B_KEY_DIM, (h + 1) * B_KEY_DIM)
        vs = slice(h * B_VAL_DIM, (h + 1) * B_VAL_DIM)
        st = st_ref[h]
        o_rows = [None] * n_pairs
        for p in order:
            rows = slice(p * pair_rows, (p + 1) * pair_rows)
            c_first, c_second = (2 * p + 1, 2 * p) if reverse else (2 * p, 2 * p + 1)
            qd = qd_ref[0, rows, ks]
            kit = kit_ref[0, ks, rows]
            vh = v_ref[0, rows, vs]
            d_first, d_second = dect[ks, c_first:c_first + 1], dect[ks, c_second:c_second + 1]
            ki32 = kit.astype(F32)
            s2 = _dot(qd, jnp.concatenate([kit, (ki32 * d_first).astype(BF16)], axis=1))
            a = jnp.where(use_inv, s2[:, 0:pair_rows], jnp.where(use_end, s2[:, pair_rows:], 0.0)).astype(BF16)
            q_pair = jnp.where(second_row, qd.astype(F32) * dec[c_first:c_first + 1, ks], qd.astype(F32))
            k_pair = ki32 * jnp.where(first_col, d_first * d_second, d_second)
            o_rows[p] = _dot(a, vh) + _dot(q_pair.astype(BF16), st.astype(BF16))
            st = st * (d_first * d_second) + _dot(k_pair.astype(BF16), vh)
        st_ref[h] = st
        heads.append(jnp.concatenate(o_rows, axis=0))
    return jnp.concatenate(heads, axis=1)


def _gla_scan_kernel(nt, qdf_ref, kitf_ref, decf_ref, qdb_ref, kitb_ref, decb_ref,
                     v_ref, sf_ref, sb_ref, sg_ref, x_ref, m_ref, hn_ref, w32_ref, fn_ref,
                     o_ref, st_ref, of_ref, w_ref):
    _cast_once(w32_ref, w_ref)
    j = pl.program_id(1)

    @pl.when(j == 0)
    def _():
        st_ref[...] = sf_ref[0]

    @pl.when(j == nt)
    def _():
        st_ref[...] = sb_ref[0]

    @pl.when(j < nt)
    def _():
        of_ref[j] = _scan_tile(False, qdf_ref, kitf_ref, decf_ref, v_ref, st_ref)

    @pl.when(j >= nt)
    def _():
        o = _scan_tile(True, qdb_ref, kitb_ref, decb_ref, v_ref, st_ref) + of_ref[2 * nt - 1 - j]
        normed = []
        for h in range(B_HEADS):
            oh = o[:, h * B_VAL_DIM:(h + 1) * B_VAL_DIM]
            normed.append(oh * lax.rsqrt(jnp.mean(oh * oh, axis=-1, keepdims=True) + EPS))
        of = jnp.concatenate(normed, axis=1) * hn_ref[...]
        y = _dot((of * sg_ref[0].astype(F32)).astype(BF16), w_ref[...])
        xn = x_ref[0] + m_ref[0][:, 2 * D:3 * D] * y
        ms = jnp.mean(xn * xn, axis=-1, keepdims=True)
        o_ref[0] = xn * lax.rsqrt(ms + EPS) * fn_ref[...]


def _gla_scan(fwd, bwd, v, s_f, s_b, sg, x, mod3, head_g, w_out, final_g):
    b, s, _ = v.shape
    assert SCAN_TILE % ROW_TILE == 0 and ROW_TILE // CHUNK == DEC_ROWS
    dec_blocks = SCAN_TILE // ROW_TILE
    nt = s // SCAN_TILE
    t_fwd = lambda j: jnp.minimum(j, nt - 1)
    t_bwd = lambda j: 2 * nt - 1 - jnp.maximum(j, nt)
    t_both = lambda j: jnp.where(j < nt, j, 2 * nt - 1 - j)

    def direction(t):
        rows = pl.BlockSpec((1, SCAN_TILE, B_K_WIDTH), lambda bi, j: (bi, t(j), 0))
        cols = pl.BlockSpec((1, B_K_WIDTH, SCAN_TILE), lambda bi, j: (bi, 0, t(j)))
        decs = pl.BlockSpec((1, dec_blocks, DEC_ROWS, B_K_WIDTH), lambda bi, j: (bi, t(j), 0, 0))
        return [rows, cols, decs]

    const = lambda shape: pl.BlockSpec(shape, lambda bi, j: (0,) * len(shape))
    state = pl.BlockSpec((1, B_HEADS, B_KEY_DIM, B_VAL_DIM), lambda bi, j: (bi, 0, 0, 0))
    out_rows = lambda w_: pl.BlockSpec((1, SCAN_TILE, w_), lambda bi, j: (bi, t_bwd(j), 0))
    in_specs = direction(t_fwd) + direction(t_bwd) + [
        pl.BlockSpec((1, SCAN_TILE, B_V_WIDTH), lambda bi, j: (bi, t_both(j), 0)),
        state, state, out_rows(B_V_WIDTH), out_rows(D),
        pl.BlockSpec((1, 1, 3 * D), lambda bi, j: (bi, 0, 0)),
        const((1, B_V_WIDTH)),
        pl.BlockSpec((B_V_WIDTH, D), lambda bi, j: (0, 0), pipeline_mode=pl.Buffered(1)),
        const((1, D))]
    return pl.pallas_call(
        functools.partial(_gla_scan_kernel, nt),
        grid=(b, 2 * nt),
        in_specs=in_specs,
        out_specs=out_rows(D),
        out_shape=jax.ShapeDtypeStruct((b, s, D), F32),
        scratch_shapes=[pltpu.VMEM((B_HEADS, B_KEY_DIM, B_VAL_DIM), F32),
                        pltpu.VMEM((nt, SCAN_TILE, B_V_WIDTH), F32),
                        pltpu.VMEM((B_V_WIDTH, D), BF16)],
        compiler_params=_params("arbitrary", "arbitrary", vmem_limit=SCAN_VMEM_LIMIT),
        name="gla_scan",
    )(*fwd, *bwd, v, s_f, s_b, sg, x, mod3, head_g.reshape(1, B_V_WIDTH), w_out, final_g.reshape(1, D))


def _rope_tables(n_tokens):
    rows_n = n_tokens // GRID_W
    row = np.repeat(np.arange(rows_n, dtype=np.float32), GRID_W)
    col = np.tile(np.arange(GRID_W, dtype=np.float32), rows_n)
    inv_freq = (np.float32(ROPE_BASE) ** (-np.arange(ROPE_FREQS, dtype=np.float32) / np.float32(ROPE_FREQS)))
    inv_freq = inv_freq.astype(np.float32)
    ang = np.stack([row[:, None] * inv_freq, col[:, None] * inv_freq], axis=1)
    cos, sin = np.cos(ang).astype(np.float32), np.sin(ang).astype(np.float32)
    zero = np.zeros_like(sin)
    tile = lambda t: np.tile(t.reshape(n_tokens, A_HEAD_DIM), (1, LANES // A_HEAD_DIM))
    return (tile(np.stack([cos, cos], axis=2)),
            tile(np.stack([-sin, zero], axis=2)),
            tile(np.stack([zero, sin], axis=2)))


def _pair_heads(t, axis):
    shape = t.shape
    t = t.reshape(shape[:axis] + (A_KV_HEADS, A_GROUP, -1) + shape[axis + 1:])
    return jnp.swapaxes(t, axis, axis + 1).reshape(shape)


def kernel(x, c, ctx, c_ctx, l0_norm_g, l0_w_ada, l0_b_ada, l0_w_in, l0_sink, l0_w_out, l1_norm_g, l1_w_ada, l1_b_ada, l1_w_in, l1_wa1_f, l1_wa2_f, l1_ba_f, l1_wa1_b, l1_wa2_b, l1_ba_b, l1_head_norm_g, l1_w_out, final_norm_g):
    b, s, _ = x.shape
    ctx_row = b

    cvec = jnp.concatenate([c, c_ctx[None, :], jnp.zeros((MOD_ROWS - b - 1, D), F32)], axis=0)
    mod0 = _modulation(cvec, l0_w_ada, l0_b_ada).reshape(MOD_ROWS, 1, 3 * D)
    mod1 = _modulation(cvec, l1_w_ada, l1_b_ada).reshape(MOD_ROWS, 1, 3 * D)

    sink = _pair_heads(l0_sink.astype(F32), 0)
    kc, vct, xc1 = _ctx_attention(sink, ctx, mod0, ctx_row, l0_norm_g, l0_w_in, l0_w_out)
    qt, k, vt, sg = _inproj_attn(x, mod0, l0_norm_g, l0_w_in, _rope_tables(s))
    x1 = _attention(sink, qt, k, vt, kc, vct, sg, x, mod0, l0_w_out)

    w1 = l1_w_in
    wa1 = jnp.concatenate([l1_wa1_f, l1_wa1_b, jnp.zeros((D, LANES - 2 * GATE_RANK), F32)], axis=1).astype(BF16)
    wa2 = jnp.zeros((LANES, 2 * B_K_WIDTH), F32)
    wa2 = wa2.at[0:GATE_RANK, 0:B_K_WIDTH].set(l1_wa2_f)
    wa2 = wa2.at[GATE_RANK:2 * GATE_RANK, B_K_WIDTH:].set(l1_wa2_b).astype(BF16)
    ba = jnp.concatenate([l1_ba_f, l1_ba_b]).reshape(1, 2 * B_K_WIDTH)
    s_f, s_b = _inproj_gla(xc1, mod1, ctx_row, l1_norm_g, w1, wa1, wa2, ba, ctx.shape[1], False)
    outs = _inproj_gla(x1, mod1, None, l1_norm_g, w1, wa1, wa2, ba, CHUNK, True)
    v1, sg1 = outs[6:]
    return _gla_scan(outs[0:3], outs[3:6], v1, s_f, s_b, sg1, x1, mod1,
                     l1_head_norm_g, l1_w_out, final_norm_g)
```

```python
import functools

import jax
import jax.numpy as jnp
import numpy as np
from jax import lax
from jax.experimental import pallas as pl
from jax.experimental.pallas import tpu as pltpu

F32 = jnp.float32
BF16 = jnp.bfloat16

D = 1024
GRID_W = 64
EPS = 1e-6
NEG_INF = -1e30

A_HEADS = 16
A_KV_HEADS = 2
A_GROUP = A_HEADS // A_KV_HEADS
A_HEAD_DIM = 64
A_WIDTH = A_HEADS * A_HEAD_DIM
A_KV_WIDTH = A_KV_HEADS * A_HEAD_DIM
BLOCK = 128
ROPE_BASE = 10000.0
ROPE_FREQS = A_HEAD_DIM // 4
Q_SCALE = A_HEAD_DIM ** -0.5
LOG2_E = 1.4426950408889634

B_HEADS = 4
B_K_WIDTH = D // 2
B_V_WIDTH = D
B_KEY_DIM = B_K_WIDTH // B_HEADS
B_VAL_DIM = B_V_WIDTH // B_HEADS
GATE_RANK = 16
GATE_TEMP = 16.0
CHUNK = 64
K_SCALE = B_KEY_DIM ** -0.5

LANES = 128
MXU_DIM = 256
MOD_ROWS = 16
ROW_TILE = 512
A_ROW_TILE = 1024
SCAN_TILE = 1024
SUB_ROWS = 256
DEC_ROWS = 8
SUM_ROWS = 16
A_BLOCKS_PER_STEP = 8
VMEM_LIMIT = 48 * 1024 * 1024
SCAN_VMEM_LIMIT = 56 * 1024 * 1024


def _params(*sem, vmem_limit=VMEM_LIMIT):
    return pltpu.CompilerParams(dimension_semantics=sem, vmem_limit_bytes=vmem_limit)


def _silu(x):
    return x / (1.0 + jnp.exp(-x))


def _dot(a, b):
    return jnp.dot(a, b, preferred_element_type=F32)


def _dot_nt(a, b):
    return lax.dot_general(a, b, (((1,), (1,)), ((), ())), preferred_element_type=F32)


def _dot_tn(a, b):
    return lax.dot_general(a, b, (((0,), (0,)), ((), ())), preferred_element_type=F32)


def _norm_mod(x, g, m):
    ms = jnp.mean(x * x, axis=-1, keepdims=True)
    y = x * lax.rsqrt(ms + EPS) * g
    return y * (1.0 + m[:, D:2 * D]) + m[:, 0:D]


def _cast_once(src_ref, dst_ref):
    @pl.when((pl.program_id(0) == 0) & (pl.program_id(1) == 0))
    def _():
        def body(i, carry):
            rows = pl.ds(pl.multiple_of(i * LANES, LANES), LANES)
            dst_ref[rows, :] = src_ref[rows, :].astype(dst_ref.dtype)
            return carry
        lax.fori_loop(0, src_ref.shape[0] // LANES, body, 0)


def _mod_kernel(c_ref, w_ref, b_ref, o_ref):
    s = _silu(c_ref[...])
    o_ref[...] = _dot(s.astype(BF16), w_ref[...].astype(BF16)) + b_ref[...]


def _modulation(cvec, w_ada, b_ada):
    n = w_ada.shape[1] // D
    return pl.pallas_call(
        _mod_kernel,
        grid=(n,),
        in_specs=[
            pl.BlockSpec((MOD_ROWS, D), lambda j: (0, 0)),
            pl.BlockSpec((D, D), lambda j: (0, j)),
            pl.BlockSpec((1, D), lambda j: (0, j)),
        ],
        out_specs=pl.BlockSpec((MOD_ROWS, D), lambda j: (0, j)),
        out_shape=jax.ShapeDtypeStruct((MOD_ROWS, n * D), F32),
        compiler_params=_params("arbitrary"),
        name="modulation",
    )(cvec, w_ada, b_ada.reshape(1, n * D))


def _rope(t, cos, sin_hi, sin_lo):
    return t * cos + pltpu.roll(t, LANES - ROPE_FREQS, 1) * sin_hi + pltpu.roll(t, ROPE_FREQS, 1) * sin_lo


def _pair_cast_columns(src_ref, dst_ref):
    @pl.when((pl.program_id(0) == 0) & (pl.program_id(1) == 0))
    def _():
        kv_lo, kv_hi = A_WIDTH, A_WIDTH + 2 * A_KV_WIDTH

        def body(i, carry):
            rows = pl.ds(pl.multiple_of(i * LANES, LANES), LANES)
            src = src_ref[rows, :]
            dst_ref[rows, kv_lo:kv_hi] = src[:, kv_lo:kv_hi].astype(BF16)
            for base in (0, kv_hi):
                for p in range(A_GROUP):
                    a = base + p * A_HEAD_DIM
                    b = base + (p + A_GROUP) * A_HEAD_DIM
                    pair = jnp.concatenate([src[:, a:a + A_HEAD_DIM], src[:, b:b + A_HEAD_DIM]], axis=1)
                    dst_ref[rows, base + p * LANES:base + (p + 1) * LANES] = pair.astype(BF16)
            return carry
        lax.fori_loop(0, src_ref.shape[0] // LANES, body, 0)


def _inproj_attn_kernel(rope, x_ref, m_ref, g_ref, w32_ref, *rest):
    *rest, w_ref = rest
    _pair_cast_columns(w32_ref, w_ref)
    wq_ref = w_ref.at[:, 0:A_WIDTH]
    wkv_ref = w_ref.at[:, A_WIDTH:A_WIDTH + 2 * A_KV_WIDTH]
    wg_ref = w_ref.at[:, A_WIDTH + 2 * A_KV_WIDTH:]
    if rope:
        cos_ref, shi_ref, slo_ref, qt_ref, k_ref, vt_ref, sg_ref = rest
    else:
        qt_ref, k_ref, vt_ref, sg_ref = rest
    for r in range(x_ref.shape[1] // SUB_ROWS):
        rows = slice(r * SUB_ROWS, (r + 1) * SUB_ROWS)
        if rope:
            cos, shi, slo = cos_ref[rows], shi_ref[rows], slo_ref[rows]
        hb = _norm_mod(x_ref[0, rows], g_ref[...], m_ref[0]).astype(BF16)
        q = _dot(hb, wq_ref[...])
        for j in range(A_WIDTH // LANES):
            qj = q[:, j * LANES:(j + 1) * LANES]
            if rope:
                qj = _rope(qj, cos, shi, slo)
            qt_ref[0, j * LANES:(j + 1) * LANES, rows] = (qj * (Q_SCALE * LOG2_E)).T.astype(BF16)
        kv = _dot(hb, wkv_ref[...])
        k = kv[:, 0:A_KV_WIDTH]
        if rope:
            k = _rope(k, cos, shi, slo)
        k_ref[0, rows] = k.astype(BF16)
        vt_ref[0, :, rows] = kv[:, A_KV_WIDTH:].T.astype(BF16)
        g = _dot(hb, wg_ref[...])
        sg_ref[0, rows] = _silu(g).astype(BF16)


def _inproj_attn(x, mod3, norm_g, w_in, tables):
    b, s, _ = x.shape
    tm = min(A_ROW_TILE, s)
    in_specs = [
        pl.BlockSpec((1, tm, D), lambda bi, i: (bi, i, 0)),
        pl.BlockSpec((1, 1, 3 * D), lambda bi, i: (bi, 0, 0)),
        pl.BlockSpec((1, D), lambda bi, i: (0, 0)),
        pl.BlockSpec(w_in.shape, lambda bi, i: (0, 0), pipeline_mode=pl.Buffered(1)),
    ] + [pl.BlockSpec((tm, LANES), lambda bi, i: (i, 0))] * 3
    args = [x, mod3, norm_g.reshape(1, D), w_in, *tables]
    out_shape = (
        jax.ShapeDtypeStruct((b, A_WIDTH, s), BF16),
        jax.ShapeDtypeStruct((b, s, A_KV_WIDTH), BF16),
        jax.ShapeDtypeStruct((b, A_KV_WIDTH, s), BF16),
        jax.ShapeDtypeStruct((b, s, A_WIDTH), BF16),
    )
    out_specs = (
        pl.BlockSpec((1, A_WIDTH, tm), lambda bi, i: (bi, 0, i)),
        pl.BlockSpec((1, tm, A_KV_WIDTH), lambda bi, i: (bi, i, 0)),
        pl.BlockSpec((1, A_KV_WIDTH, tm), lambda bi, i: (bi, 0, i)),
        pl.BlockSpec((1, tm, A_WIDTH), lambda bi, i: (bi, i, 0)),
    )
    return pl.pallas_call(
        functools.partial(_inproj_attn_kernel, True),
        grid=(b, s // tm),
        in_specs=in_specs,
        out_specs=out_specs,
        out_shape=out_shape,
        scratch_shapes=[pltpu.VMEM(w_in.shape, BF16)],
        compiler_params=_params("arbitrary", "arbitrary"),
        name="inproj_attn",
    )(*args)


def _attn_block(kwin, vtw, qts, sinks, ok_prev, ok_next):
    n_keys, nq = kwin.shape[0], qts.shape[1]
    lane = lax.broadcasted_iota(jnp.int32, kwin.shape, 1)
    kbd = jnp.concatenate([jnp.where(lane < A_HEAD_DIM, kwin, jnp.zeros_like(kwin)),
                           jnp.where(lane >= A_HEAD_DIM, kwin, jnp.zeros_like(kwin))], axis=0)
    ones = jnp.where(lax.broadcasted_iota(jnp.int32, (SUM_ROWS, n_keys), 0) == 0, 1.0, 0.0).astype(BF16)

    st = _dot(kbd, qts).astype(BF16)
    neg = jnp.asarray(NEG_INF, BF16)
    outs = []
    for hh in range(2):
        sh = st[hh * n_keys:(hh + 1) * n_keys]
        if ok_prev is not None:
            parts = [jnp.where(ok_prev, sh[0:BLOCK], neg),
                     sh[BLOCK:2 * BLOCK],
                     jnp.where(ok_next, sh[2 * BLOCK:3 * BLOCK], neg),
                     sh[3 * BLOCK:]]
        else:
            parts = [sh]
        mx = functools.reduce(jnp.maximum, [jnp.max(t, axis=0, keepdims=True) for t in parts])
        mxb = jnp.maximum(mx.astype(F32), sinks[hh]).astype(BF16)
        mx = mxb.astype(F32)
        probs = jnp.concatenate([jnp.exp2(t - mxb) for t in parts], axis=0)
        vt_h = jnp.concatenate([vtw[hh * A_HEAD_DIM:(hh + 1) * A_HEAD_DIM], ones], axis=0)
        ot = _dot(vt_h, probs)
        den = ot[A_HEAD_DIM:A_HEAD_DIM + 1] + jnp.exp2(sinks[hh] - mx)
        outs.append(ot[0:A_HEAD_DIM] * (1.0 / den))
    return jnp.concatenate(outs, axis=0)


def _pair_cast_rows(src_ref, dst_ref):
    @pl.when((pl.program_id(0) == 0) & (pl.program_id(1) == 0))
    def _():
        for p in range(A_GROUP):
            for half, h in enumerate((p, p + A_GROUP)):
                lo = p * LANES + half * A_HEAD_DIM
                dst_ref[lo:lo + A_HEAD_DIM, :] = src_ref[h * A_HEAD_DIM:(h + 1) * A_HEAD_DIM, :].astype(BF16)


def _attn_kernel(local, n_steps, sink_ref, qt_ref, *rest):
    *rest, w_ref = rest
    n_blk = rest[-1].shape[1] // BLOCK
    if local:
        (kp_ref, kc_ref, kn_ref, vp_ref, vc_ref, vn_ref, kx_ref, vx_ref,
         sg_ref, x_ref, m_ref, w32_ref, o_ref) = rest
        last = slice((n_blk - 1) * BLOCK, n_blk * BLOCK)
        k_blocks = ([kp_ref[0, last]] + [kc_ref[0, i * BLOCK:(i + 1) * BLOCK] for i in range(n_blk)]
                    + [kn_ref[0, 0:BLOCK]])
        v_blocks = ([vp_ref[0, :, last]] + [vc_ref[0, :, i * BLOCK:(i + 1) * BLOCK] for i in range(n_blk)]
                    + [vn_ref[0, :, 0:BLOCK]])
    else:
        kx_ref, vx_ref, sg_ref, x_ref, m_ref, w32_ref, o_ref = rest
    _pair_cast_rows(w32_ref, w_ref)
    step = pl.program_id(1)
    nq = A_GROUP * BLOCK
    chunk_of = lax.broadcasted_iota(jnp.int32, (1, nq), 1) // BLOCK
    sinks = []
    for hh in range(2):
        sk = jnp.full((1, nq), sink_ref[hh] * LOG2_E, F32)
        for c in range(1, A_GROUP):
            sk = jnp.where(chunk_of == c, sink_ref[2 * c + hh] * LOG2_E, sk)
        sinks.append(sk)
    if local:
        kj = lax.broadcasted_iota(jnp.int32, (BLOCK, nq), 0)
        qi = lax.broadcasted_iota(jnp.int32, (BLOCK, nq), 1) % BLOCK

    rows_out = []
    for blk in range(n_blk):
        qts = jnp.concatenate([qt_ref[0, c * LANES:(c + 1) * LANES, blk * BLOCK:(blk + 1) * BLOCK]
                               for c in range(A_GROUP)], axis=1)
        if local:
            kwin = jnp.concatenate(k_blocks[blk:blk + 3] + [kx_ref[0]], axis=0)
            vtw = jnp.concatenate(v_blocks[blk:blk + 3] + [vx_ref[0]], axis=1)
            ok_prev = (kj >= qi) & (step > 0) if blk == 0 else (kj >= qi)
            ok_next = (kj <= qi) & (step < n_steps - 1) if blk == n_blk - 1 else (kj <= qi)
        else:
            kwin, vtw, ok_prev, ok_next = kx_ref[0], vx_ref[0], None, None
        ot = _attn_block(kwin, vtw, qts, sinks, ok_prev, ok_next)
        rows = slice(blk * BLOCK, (blk + 1) * BLOCK)
        outs = []
        for c in range(A_GROUP):
            sg = sg_ref[0, rows, c * LANES:(c + 1) * LANES].astype(F32)
            outs.append((ot[:, c * BLOCK:(c + 1) * BLOCK].T * sg).astype(BF16))
        rows_out.append(jnp.concatenate(outs, axis=1))
    y = _dot(jnp.concatenate(rows_out, axis=0), w_ref[...])
    gate = m_ref[0][:, 2 * D:3 * D]
    o_ref[0] = x_ref[0] + gate * y


def _attention(sink, qt, k, vt, kx, vxt, sg, x, mod3, w_out):
    b, s, _ = sg.shape
    rows = A_BLOCKS_PER_STEP * BLOCK
    assert s % rows == 0
    ns = s // rows
    n_ctx = kx.shape[1]
    blk = lambda w: pl.BlockSpec((1, rows, w), lambda bi, i: (bi, i, 0))
    blk_t = lambda w: pl.BlockSpec((1, w, rows), lambda bi, i: (bi, 0, i))
    lo = lambda i: jnp.maximum(i - 1, 0)
    hi = lambda i: jnp.minimum(i + 1, ns - 1)
    in_specs = [pl.BlockSpec(memory_space=pltpu.SMEM), blk_t(A_WIDTH),
                pl.BlockSpec((1, rows, A_KV_WIDTH), lambda bi, i: (bi, lo(i), 0)),
                blk(A_KV_WIDTH),
                pl.BlockSpec((1, rows, A_KV_WIDTH), lambda bi, i: (bi, hi(i), 0)),
                pl.BlockSpec((1, A_KV_WIDTH, rows), lambda bi, i: (bi, 0, lo(i))),
                blk_t(A_KV_WIDTH),
                pl.BlockSpec((1, A_KV_WIDTH, rows), lambda bi, i: (bi, 0, hi(i))),
                pl.BlockSpec((1, n_ctx, A_KV_WIDTH), lambda bi, i: (bi, 0, 0)),
                pl.BlockSpec((1, A_KV_WIDTH, n_ctx), lambda bi, i: (bi, 0, 0)),
                blk(A_WIDTH), blk(D),
                pl.BlockSpec((1, 1, 3 * D), lambda bi, i: (bi, 0, 0)),
                pl.BlockSpec((A_WIDTH, D), lambda bi, i: (0, 0), pipeline_mode=pl.Buffered(1))]
    return pl.pallas_call(
        functools.partial(_attn_kernel, True, ns),
        grid=(b, ns),
        in_specs=in_specs,
        out_specs=blk(D),
        out_shape=jax.ShapeDtypeStruct((b, s, D), F32),
        scratch_shapes=[pltpu.VMEM((A_WIDTH, D), BF16)],
        compiler_params=_params("arbitrary", "arbitrary"),
        name="attn_local",
    )(sink, qt, k, k, k, vt, vt, vt, kx, vxt, sg, x, mod3, w_out)


def _ctx_attn_kernel(sink_ref, x_ref, m_ref, g_ref, win_ref, wout_ref, k_ref, vt_ref, o_ref,
                     win_scr, wout_scr, qt_scr, sg_scr):
    _inproj_attn_kernel(False, x_ref, m_ref, g_ref, win_ref, qt_scr, k_ref, vt_ref, sg_scr, win_scr)
    _attn_kernel(False, 1, sink_ref, qt_scr, k_ref, vt_ref, sg_scr, x_ref, m_ref, wout_ref, o_ref, wout_scr)


def _ctx_attention(sink, xc, mod3, mod_row, norm_g, w_in, w_out):
    b, n, _ = xc.shape
    const = lambda shape, **kw: pl.BlockSpec(shape, lambda bi, i: (0,) * len(shape), **kw)
    return pl.pallas_call(
        _ctx_attn_kernel,
        grid=(b, 1),
        in_specs=[pl.BlockSpec(memory_space=pltpu.SMEM),
                  pl.BlockSpec((1, n, D), lambda bi, i: (bi, 0, 0)),
                  pl.BlockSpec((1, 1, 3 * D), lambda bi, i: (mod_row, 0, 0)),
                  const((1, D)),
                  const(w_in.shape, pipeline_mode=pl.Buffered(1)),
                  const(w_out.shape, pipeline_mode=pl.Buffered(1))],
        out_specs=(pl.BlockSpec((1, n, A_KV_WIDTH), lambda bi, i: (bi, 0, 0)),
                   pl.BlockSpec((1, A_KV_WIDTH, n), lambda bi, i: (bi, 0, 0)),
                   pl.BlockSpec((1, n, D), lambda bi, i: (bi, 0, 0))),
        out_shape=(jax.ShapeDtypeStruct((b, n, A_KV_WIDTH), BF16),
                   jax.ShapeDtypeStruct((b, A_KV_WIDTH, n), BF16),
                   jax.ShapeDtypeStruct((b, n, D), F32)),
        scratch_shapes=[pltpu.VMEM(w_in.shape, BF16), pltpu.VMEM(w_out.shape, BF16),
                        pltpu.VMEM((1, A_WIDTH, n), BF16), pltpu.VMEM((1, n, A_WIDTH), BF16)],
        compiler_params=_params("arbitrary", "arbitrary"),
        name="ctx_attn",
    )(sink, xc, mod3, norm_g.reshape(1, D), w_in, w_out)


def _chunk_cumsum(x, tri):
    return _dot(tri, x.astype(BF16))


def _inproj_gla_kernel(chunk, with_q, x_ref, m_ref, g_ref, w32_ref, wa1_ref, wa2_ref, ba_ref, tri_ref, *outs):
    *outs, w_ref = outs
    _cast_once(w32_ref, w_ref)
    if with_q:
        per_dir = (outs[0:3], outs[3:6])
        v_ref, sg_ref = outs[6:]
    else:
        per_dir = ((outs[0],), (outs[1],))
    tm = x_ref.shape[1]
    hb = _norm_mod(x_ref[0], g_ref[...], m_ref[0]).astype(BF16)
    r = _dot(hb, wa1_ref[...])
    z = _dot(r.astype(BF16), wa2_ref[...]) + ba_ref[...]
    e = jnp.exp2(jnp.abs(z) * -LOG2_E)
    la = (jnp.minimum(z, 0.0) * LOG2_E - jnp.log2(1.0 + e)) * (1.0 / GATE_TEMP)
    k = _dot(hb, w_ref[:, B_K_WIDTH:2 * B_K_WIDTH])
    if with_q:
        q = _dot(hb, w_ref[:, 0:B_K_WIDTH]) * K_SCALE
    n_chunks = tm // chunk
    k_ends = []
    for reverse in (False, True):
        lad = la[:, B_K_WIDTH:] if reverse else la[:, 0:B_K_WIDTH]
        refs = per_dir[1] if reverse else per_dir[0]
        tots, kis, k_end = [], [], None
        tri = tri_ref[1 if reverse else 0]
        tb = tri.shape[0]
        cum_all = jnp.concatenate([_chunk_cumsum(lad[r * tb:(r + 1) * tb], tri) for r in range(tm // tb)], axis=0)
        for c in range(n_chunks):
            rows = slice(c * chunk, (c + 1) * chunk)
            cum = cum_all[rows]
            tot = cum[0:1] if reverse else cum[chunk - 1:chunk]
            if not with_q:
                k_end = (k[rows] * jnp.exp2(tot - cum)).astype(BF16)
                continue
            refs[0][0, rows] = (q[rows] * jnp.exp2(cum)).astype(BF16)
            kis.append(k[rows] * jnp.exp2(-cum))
            tots.append(tot)
            if c % 2 == 1:
                pair = slice((c - 1) * chunk, (c + 1) * chunk)
                refs[1][0, :, pair] = jnp.concatenate(kis[-2:], axis=0).T.astype(BF16)
        if with_q:
            pad = [jnp.zeros((DEC_ROWS - n_chunks, B_K_WIDTH), F32)] if n_chunks < DEC_ROWS else []
            refs[2][0, 0] = jnp.exp2(jnp.concatenate(tots + pad, axis=0))
        else:
            k_ends.append(k_end)
    if with_q:
        sg_ref[0] = _silu(_dot(hb, w_ref[:, 2 * B_K_WIDTH + B_V_WIDTH:])).astype(BF16)
    v = _dot(hb, w_ref[:, 2 * B_K_WIDTH:2 * B_K_WIDTH + B_V_WIDTH]).astype(BF16)
    if with_q:
        v_ref[0] = v
    else:
        for refs, k_end in zip(per_dir, k_ends):
            for h in range(B_HEADS):
                ks = slice(h * B_KEY_DIM, (h + 1) * B_KEY_DIM)
                refs[0][0, h] = _dot_tn(k_end[:, ks], v[:, h * B_VAL_DIM:(h + 1) * B_VAL_DIM])


def _inproj_gla(x, mod3, mod_row, norm_g, w, wa1, wa2, ba, chunk, with_q):
    b, s, _ = x.shape
    tm = min(ROW_TILE, s)
    nt = s // tm
    cpt = tm // chunk
    tb = max(chunk, MXU_DIM)
    assert tm % tb == 0 and tb % chunk == 0
    if mod_row is None:
        mod_map = lambda bi, i: (bi, 0, 0)
    else:
        mod_map = lambda bi, i: (mod_row, 0, 0)
    const = lambda shape: pl.BlockSpec(shape, lambda bi, i: (0,) * len(shape))
    in_specs = [
        pl.BlockSpec((1, tm, D), lambda bi, i: (bi, i, 0)),
        pl.BlockSpec((1, 1, 3 * D), mod_map),
        const((1, D)), pl.BlockSpec(w.shape, lambda bi, i: (0, 0), pipeline_mode=pl.Buffered(1)),
        const(wa1.shape), const(wa2.shape), const(ba.shape),
        const((2, tb, tb)),
    ]
    t_idx = np.arange(tb)
    same = (t_idx[:, None] // chunk) == (t_idx[None, :] // chunk)
    lower = same & (t_idx[None, :] <= t_idx[:, None])
    tri = jnp.asarray(np.stack([lower, lower.T]), dtype=BF16)
    rows = lambda w_: (jax.ShapeDtypeStruct((b, s, w_), BF16),
                       pl.BlockSpec((1, tm, w_), lambda bi, i: (bi, i, 0)))
    cols = (jax.ShapeDtypeStruct((b, B_K_WIDTH, s), BF16),
            pl.BlockSpec((1, B_K_WIDTH, tm), lambda bi, i: (bi, 0, i)))
    decs = (jax.ShapeDtypeStruct((b, nt, DEC_ROWS, B_K_WIDTH), F32),
            pl.BlockSpec((1, 1, DEC_ROWS, B_K_WIDTH), lambda bi, i: (bi, i, 0, 0)))
    if with_q:
        assert cpt <= DEC_ROWS and cpt % 2 == 0
        outs = [rows(B_K_WIDTH), cols, decs] * 2 + [rows(B_V_WIDTH)] * 2
    else:
        assert nt == 1 and cpt == 1
        state = (jax.ShapeDtypeStruct((b, B_HEADS, B_KEY_DIM, B_VAL_DIM), F32),
                 pl.BlockSpec((1, B_HEADS, B_KEY_DIM, B_VAL_DIM), lambda bi, i: (bi, 0, 0, 0)))
        outs = [state] * 2
    return pl.pallas_call(
        functools.partial(_inproj_gla_kernel, chunk, with_q),
        grid=(b, nt),
        in_specs=in_specs,
        out_specs=tuple(o[1] for o in outs),
        out_shape=tuple(o[0] for o in outs),
        scratch_shapes=[pltpu.VMEM(w.shape, BF16)],
        compiler_params=_params("arbitrary", "arbitrary"),
        name="inproj_gla" if with_q else "inproj_gla_ctx",
    )(x, mod3, norm_g.reshape(1, D), w, wa1, wa2, ba, tri)


def _scan_tile(reverse, qd_ref, kit_ref, dec_ref, v_ref, st_ref):
    pair_rows = 2 * CHUNK
    ti = lax.broadcasted_iota(jnp.int32, (pair_rows, pair_rows), 0)
    si = lax.broadcasted_iota(jnp.int32, (pair_rows, pair_rows), 1)
    same = (ti // CHUNK) == (si // CHUNK)
    if reverse:
        use_inv = same & (ti <= si)
        use_end = (ti < CHUNK) & (si >= CHUNK)
        second_row = lax.broadcasted_iota(jnp.int32, (pair_rows, B_KEY_DIM), 0) < CHUNK
        first_col = lax.broadcasted_iota(jnp.int32, (B_KEY_DIM, pair_rows), 1) >= CHUNK
    else:
        use_inv = same & (ti >= si)
        use_end = (ti >= CHUNK) & (si < CHUNK)
        second_row = lax.broadcasted_iota(jnp.int32, (pair_rows, B_KEY_DIM), 0) >= CHUNK
        first_col = lax.broadcasted_iota(jnp.int32, (B_KEY_DIM, pair_rows), 1) < CHUNK
    n_pairs = SCAN_TILE // pair_rows
    order = range(n_pairs - 1, -1, -1) if reverse else range(n_pairs)
    n_dec = dec_ref.shape[1] * DEC_ROWS
    dec = dec_ref[0].reshape(n_dec, B_K_WIDTH)
    dect = jnp.concatenate([dec, jnp.zeros((LANES - n_dec, B_K_WIDTH), F32)], axis=0).T
    heads = []
    for h in range(B_HEADS):
        ks = slice(h * B_KEY_DIM, (h + 1) * B_KEY_DIM)
        vs = slice(h * B_VAL_DIM, (h + 1) * B_VAL_DIM)
        st = st_ref[h]
        o_rows = [None] * n_pairs
        for p in order:
            rows = slice(p * pair_rows, (p + 1) * pair_rows)
            c_first, c_second = (2 * p + 1, 2 * p) if reverse else (2 * p, 2 * p + 1)
            qd = qd_ref[0, rows, ks]
            kit = kit_ref[0, ks, rows]
            vh = v_ref[0, rows, vs]
            d_first, d_second = dect[ks, c_first:c_first + 1], dect[ks, c_second:c_second + 1]
            ki32 = kit.astype(F32)
            s2 = _dot(qd, jnp.concatenate([kit, (ki32 * d_first).astype(BF16)], axis=1))
            a = jnp.where(use_inv, s2[:, 0:pair_rows], jnp.where(use_end, s2[:, pair_rows:], 0.0)).astype(BF16)
            q_pair = jnp.where(second_row, qd.astype(F32) * dec[c_first:c_first + 1, ks], qd.astype(F32))
            k_pair = ki32 * jnp.where(first_col, d_first * d_second, d_second)
            o_rows[p] = _dot(a, vh) + _dot(q_pair.astype(BF16), st.astype(BF16))
            st = st * (d_first * d_second) + _dot(k_pair.astype(BF16), vh)
        st_ref[h] = st
        heads.append(jnp.concatenate(o_rows, axis=0))
    return jnp.concatenate(heads, axis=1)


def _gla_scan_kernel(nt, qdf_ref, kitf_ref, decf_ref, qdb_ref, kitb_ref, decb_ref,
                     v_ref, sf_ref, sb_ref, sg_ref, x_ref, m_ref, hn_ref, w32_ref, fn_ref,
                     o_ref, st_ref, of_ref, w_ref):
    _cast_once(w32_ref, w_ref)
    j = pl.program_id(1)

    @pl.when(j == 0)
    def _():
        st_ref[...] = sf_ref[0]

    @pl.when(j == nt)
    def _():
        st_ref[...] = sb_ref[0]

    @pl.when(j < nt)
    def _():
        of_ref[j] = _scan_tile(False, qdf_ref, kitf_ref, decf_ref, v_ref, st_ref)

    @pl.when(j >= nt)
    def _():
        o = _scan_tile(True, qdb_ref, kitb_ref, decb_ref, v_ref, st_ref) + of_ref[2 * nt - 1 - j]
        normed = []
        for h in range(B_HEADS):
            oh = o[:, h * B_VAL_DIM:(h + 1) * B_VAL_DIM]
            normed.append(oh * lax.rsqrt(jnp.mean(oh * oh, axis=-1, keepdims=True) + EPS))
        of = jnp.concatenate(normed, axis=1) * hn_ref[...]
        y = _dot((of * sg_ref[0].astype(F32)).astype(BF16), w_ref[...])
        xn = x_ref[0] + m_ref[0][:, 2 * D:3 * D] * y
        ms = jnp.mean(xn * xn, axis=-1, keepdims=True)
        o_ref[0] = xn * lax.rsqrt(ms + EPS) * fn_ref[...]


def _gla_scan(fwd, bwd, v, s_f, s_b, sg, x, mod3, head_g, w_out, final_g):
    b, s, _ = v.shape
    assert SCAN_TILE % ROW_TILE == 0 and ROW_TILE // CHUNK == DEC_ROWS
    dec_blocks = SCAN_TILE // ROW_TILE
    nt = s // SCAN_TILE
    t_fwd = lambda j: jnp.minimum(j, nt - 1)
    t_bwd = lambda j: 2 * nt - 1 - jnp.maximum(j, nt)
    t_both = lambda j: jnp.where(j < nt, j, 2 * nt - 1 - j)

    def direction(t):
        rows = pl.BlockSpec((1, SCAN_TILE, B_K_WIDTH), lambda bi, j: (bi, t(j), 0))
        cols = pl.BlockSpec((1, B_K_WIDTH, SCAN_TILE), lambda bi, j: (bi, 0, t(j)))
        decs = pl.BlockSpec((1, dec_blocks, DEC_ROWS, B_K_WIDTH), lambda bi, j: (bi, t(j), 0, 0))
        return [rows, cols, decs]

    const = lambda shape: pl.BlockSpec(shape, lambda bi, j: (0,) * len(shape))
    state = pl.BlockSpec((1, B_HEADS, B_KEY_DIM, B_VAL_DIM), lambda bi, j: (bi, 0, 0, 0))
    out_rows = lambda w_: pl.BlockSpec((1, SCAN_TILE, w_), lambda bi, j: (bi, t_bwd(j), 0))
    in_specs = direction(t_fwd) + direction(t_bwd) + [
        pl.BlockSpec((1, SCAN_TILE, B_V_WIDTH), lambda bi, j: (bi, t_both(j), 0)),
        state, state, out_rows(B_V_WIDTH), out_rows(D),
        pl.BlockSpec((1, 1, 3 * D), lambda bi, j: (bi, 0, 0)),
        const((1, B_V_WIDTH)),
        pl.BlockSpec((B_V_WIDTH, D), lambda bi, j: (0, 0), pipeline_mode=pl.Buffered(1)),
        const((1, D))]
    return pl.pallas_call(
        functools.partial(_gla_scan_kernel, nt),
        grid=(b, 2 * nt),
        in_specs=in_specs,
        out_specs=out_rows(D),
        out_shape=jax.ShapeDtypeStruct((b, s, D), F32),
        scratch_shapes=[pltpu.VMEM((B_HEADS, B_KEY_DIM, B_VAL_DIM), F32),
                        pltpu.VMEM((nt, SCAN_TILE, B_V_WIDTH), F32),
                        pltpu.VMEM((B_V_WIDTH, D), BF16)],
        compiler_params=_params("arbitrary", "arbitrary", vmem_limit=SCAN_VMEM_LIMIT),
        name="gla_scan",
    )(*fwd, *bwd, v, s_f, s_b, sg, x, mod3, head_g.reshape(1, B_V_WIDTH), w_out, final_g.reshape(1, D))


def _rope_tables(n_tokens):
    rows_n = n_tokens // GRID_W
    row = np.repeat(np.arange(rows_n, dtype=np.float32), GRID_W)
    col = np.tile(np.arange(GRID_W, dtype=np.float32), rows_n)
    inv_freq = (np.float32(ROPE_BASE) ** (-np.arange(ROPE_FREQS, dtype=np.float32) / np.float32(ROPE_FREQS)))
    inv_freq = inv_freq.astype(np.float32)
    ang = np.stack([row[:, None] * inv_freq, col[:, None] * inv_freq], axis=1)
    cos, sin = np.cos(ang).astype(np.float32), np.sin(ang).astype(np.float32)
    zero = np.zeros_like(sin)
    tile = lambda t: np.tile(t.reshape(n_tokens, A_HEAD_DIM), (1, LANES // A_HEAD_DIM))
    return (tile(np.stack([cos, cos], axis=2)),
            tile(np.stack([-sin, zero], axis=2)),
            tile(np.stack([zero, sin], axis=2)))


def _pair_heads(t, axis):
    shape = t.shape
    t = t.reshape(shape[:axis] + (A_KV_HEADS, A_GROUP, -1) + shape[axis + 1:])
    return jnp.swapaxes(t, axis, axis + 1).reshape(shape)


def kernel(x, c, ctx, c_ctx, l0_norm_g, l0_w_ada, l0_b_ada, l0_w_in, l0_sink, l0_w_out, l1_norm_g, l1_w_ada, l1_b_ada, l1_w_in, l1_wa1_f, l1_wa2_f, l1_ba_f, l1_wa1_b, l1_wa2_b, l1_ba_b, l1_head_norm_g, l1_w_out, final_norm_g):
    b, s, _ = x.shape
    ctx_row = b

    cvec = jnp.concatenate([c, c_ctx[None, :], jnp.zeros((MOD_ROWS - b - 1, D), F32)], axis=0)
    mod0 = _modulation(cvec, l0_w_ada, l0_b_ada).reshape(MOD_ROWS, 1, 3 * D)
    mod1 = _modulation(cvec, l1_w_ada, l1_b_ada).reshape(MOD_ROWS, 1, 3 * D)

    sink = _pair_heads(l0_sink.astype(F32), 0)
    kc, vct, xc1 = _ctx_attention(sink, ctx, mod0, ctx_row, l0_norm_g, l0_w_in, l0_w_out)
    qt, k, vt, sg = _inproj_attn(x, mod0, l0_norm_g, l0_w_in, _rope_tables(s))
    x1 = _attention(sink, qt, k, vt, kc, vct, sg, x, mod0, l0_w_out)

    w1 = l1_w_in
    wa1 = jnp.concatenate([l1_wa1_f, l1_wa1_b, jnp.zeros((D, LANES - 2 * GATE_RANK), F32)], axis=1).astype(BF16)
    wa2 = jnp.zeros((LANES, 2 * B_K_WIDTH), F32)
    wa2 = wa2.at[0:GATE_RANK, 0:B_K_WIDTH].set(l1_wa2_f)
    wa2 = wa2.at[GATE_RANK:2 * GATE_RANK, B_K_WIDTH:].set(l1_wa2_b).astype(BF16)
    ba = jnp.concatenate([l1_ba_f, l1_ba_b]).reshape(1, 2 * B_K_WIDTH)
    s_f, s_b = _inproj_gla(xc1, mod1, ctx_row, l1_norm_g, w1, wa1, wa2, ba, ctx.shape[1], False)
    outs = _inproj_gla(x1, mod1, None, l1_norm_g, w1, wa1, wa2, ba, CHUNK, True)
    v1, sg1 = outs[6:]
    return _gla_scan(outs[0:3], outs[3:6], v1, s_f, s_b, sg1, x1, mod1,
                     l1_head_norm_g, l1_w_out, final_norm_g)
```

```python
import functools

import jax
import jax.numpy as jnp
import numpy as np
from jax import lax
from jax.experimental import pallas as pl
from jax.experimental.pallas import tpu as pltpu

F32 = jnp.float32
BF16 = jnp.bfloat16

D = 1024
GRID_W = 64
EPS = 1e-6
NEG_INF = -1e30

A_HEADS = 16
A_KV_HEADS = 2
A_GROUP = A_HEADS // A_KV_HEADS
A_HEAD_DIM = 64
A_WIDTH = A_HEADS * A_HEAD_DIM
A_KV_WIDTH = A_KV_HEADS * A_HEAD_DIM
BLOCK = 128
ROPE_BASE = 10000.0
ROPE_FREQS = A_HEAD_DIM // 4
Q_SCALE = A_HEAD_DIM ** -0.5
LOG2_E = 1.4426950408889634

B_HEADS = 4
B_K_WIDTH = D // 2
B_V_WIDTH = D
B_KEY_DIM = B_K_WIDTH // B_HEADS
B_VAL_DIM = B_V_WIDTH // B_HEADS
GATE_RANK = 16
GATE_TEMP = 16.0
CHUNK = 64
K_SCALE = B_KEY_DIM ** -0.5

LANES = 128
MXU_DIM = 256
MOD_ROWS = 16
ROW_TILE = 512
A_ROW_TILE = 1024
SCAN_TILE = 1024
SUB_ROWS = 256
DEC_ROWS = 8
SUM_ROWS = 16
A_BLOCKS_PER_STEP = 8
VMEM_LIMIT = 48 * 1024 * 1024
SCAN_VMEM_LIMIT = 56 * 1024 * 1024


def _params(*sem, vmem_limit=VMEM_LIMIT):
    return pltpu.CompilerParams(dimension_semantics=sem, vmem_limit_bytes=vmem_limit)


def _silu(x):
    return x / (1.0 + jnp.exp(-x))


def _dot(a, b):
    return jnp.dot(a, b, preferred_element_type=F32)


def _dot_nt(a, b):
    return lax.dot_general(a, b, (((1,), (1,)), ((), ())), preferred_element_type=F32)


def _dot_tn(a, b):
    return lax.dot_general(a, b, (((0,), (0,)), ((), ())), preferred_element_type=F32)


def _norm_mod(x, g, m):
    ms = jnp.mean(x * x, axis=-1, keepdims=True)
    y = x * lax.rsqrt(ms + EPS) * g
    return y * (1.0 + m[:, D:2 * D]) + m[:, 0:D]


def _cast_once(src_ref, dst_ref):
    @pl.when((pl.program_id(0) == 0) & (pl.program_id(1) == 0))
    def _():
        def body(i, carry):
            rows = pl.ds(pl.multiple_of(i * LANES, LANES), LANES)
            dst_ref[rows, :] = src_ref[rows, :].astype(dst_ref.dtype)
            return carry
        lax.fori_loop(0, src_ref.shape[0] // LANES, body, 0)


def _mod_kernel(c_ref, w_ref, b_ref, o_ref):
    s = _silu(c_ref[...])
    o_ref[...] = _dot(s.astype(BF16), w_ref[...].astype(BF16)) + b_ref[...]


def _modulation(cvec, w_ada, b_ada):
    n = w_ada.shape[1] // D
    return pl.pallas_call(
        _mod_kernel,
        grid=(n,),
        in_specs=[
            pl.BlockSpec((MOD_ROWS, D), lambda j: (0, 0)),
            pl.BlockSpec((D, D), lambda j: (0, j)),
            pl.BlockSpec((1, D), lambda j: (0, j)),
        ],
        out_specs=pl.BlockSpec((MOD_ROWS, D), lambda j: (0, j)),
        out_shape=jax.ShapeDtypeStruct((MOD_ROWS, n * D), F32),
        compiler_params=_params("arbitrary"),
        name="modulation",
    )(cvec, w_ada, b_ada.reshape(1, n * D))


def _rope(t, cos, sin_hi, sin_lo):
    return t * cos + pltpu.roll(t, LANES - ROPE_FREQS, 1) * sin_hi + pltpu.roll(t, ROPE_FREQS, 1) * sin_lo


def _pair_cast_columns(src_ref, dst_ref):
    @pl.when((pl.program_id(0) == 0) & (pl.program_id(1) == 0))
    def _():
        kv_lo, kv_hi = A_WIDTH, A_WIDTH + 2 * A_KV_WIDTH

        def body(i, carry):
            rows = pl.ds(pl.multiple_of(i * LANES, LANES), LANES)
            src = src_ref[rows, :]
            dst_ref[rows, kv_lo:kv_hi] = src[:, kv_lo:kv_hi].astype(BF16)
            for base in (0, kv_hi):
                for p in range(A_GROUP):
                    a = base + p * A_HEAD_DIM
                    b = base + (p + A_GROUP) * A_HEAD_DIM
                    pair = jnp.concatenate([src[:, a:a + A_HEAD_DIM], src[:, b:b + A_HEAD_DIM]], axis=1)
                    dst_ref[rows, base + p * LANES:base + (p + 1) * LANES] = pair.astype(BF16)
            return carry
        lax.fori_loop(0, src_ref.shape[0] // LANES, body, 0)


def _inproj_attn_kernel(rope, x_ref, m_ref, g_ref, w_ref, *rest):
    wq_ref = w_ref.at[:, 0:A_WIDTH]
    wkv_ref = w_ref.at[:, A_WIDTH:A_WIDTH + 2 * A_KV_WIDTH]
    wg_ref = w_ref.at[:, A_WIDTH + 2 * A_KV_WIDTH:]
    if rope:
        cos_ref, shi_ref, slo_ref, qt_ref, k_ref, vt_ref, sg_ref = rest
    else:
        qt_ref, k_ref, vt_ref, sg_ref = rest
    for r in range(x_ref.shape[1] // SUB_ROWS):
        rows = slice(r * SUB_ROWS, (r + 1) * SUB_ROWS)
        if rope:
            cos, shi, slo = cos_ref[rows], shi_ref[rows], slo_ref[rows]
        hb = _norm_mod(x_ref[0, rows], g_ref[...], m_ref[0]).astype(BF16)
        q = _dot(hb, wq_ref[...])
        for j in range(A_WIDTH // LANES):
            qj = q[:, j * LANES:(j + 1) * LANES]
            if rope:
                qj = _rope(qj, cos, shi, slo)
            qt_ref[0, j * LANES:(j + 1) * LANES, rows] = (qj * (Q_SCALE * LOG2_E)).T.astype(BF16)
        kv = _dot(hb, wkv_ref[...])
        k = kv[:, 0:A_KV_WIDTH]
        if rope:
            k = _rope(k, cos, shi, slo)
        k_ref[0, rows] = k.astype(BF16)
        vt_ref[0, :, rows] = kv[:, A_KV_WIDTH:].T.astype(BF16)
        g = _dot(hb, wg_ref[...])
        sg_ref[0, rows] = _silu(g).astype(BF16)


def _inproj_attn(x, mod3, norm_g, w_in, tables):
    b, s, _ = x.shape
    tm = min(A_ROW_TILE, s)
    in_specs = [
        pl.BlockSpec((1, tm, D), lambda bi, i: (bi, i, 0)),
        pl.BlockSpec((1, 1, 3 * D), lambda bi, i: (bi, 0, 0)),
        pl.BlockSpec((1, D), lambda bi, i: (0, 0)),
        pl.BlockSpec(w_in.shape, lambda bi, i: (0, 0)),
    ] + [pl.BlockSpec((tm, LANES), lambda bi, i: (i, 0))] * 3
    args = [x, mod3, norm_g.reshape(1, D), w_in, *tables]
    out_shape = (
        jax.ShapeDtypeStruct((b, A_WIDTH, s), BF16),
        jax.ShapeDtypeStruct((b, s, A_KV_WIDTH), BF16),
        jax.ShapeDtypeStruct((b, A_KV_WIDTH, s), BF16),
        jax.ShapeDtypeStruct((b, s, A_WIDTH), BF16),
    )
    out_specs = (
        pl.BlockSpec((1, A_WIDTH, tm), lambda bi, i: (bi, 0, i)),
        pl.BlockSpec((1, tm, A_KV_WIDTH), lambda bi, i: (bi, i, 0)),
        pl.BlockSpec((1, A_KV_WIDTH, tm), lambda bi, i: (bi, 0, i)),
        pl.BlockSpec((1, tm, A_WIDTH), lambda bi, i: (bi, i, 0)),
    )
    return pl.pallas_call(
        functools.partial(_inproj_attn_kernel, True),
        grid=(b, s // tm),
        in_specs=in_specs,
        out_specs=out_specs,
        out_shape=out_shape,
        compiler_params=_params("parallel", "parallel"),
        name="inproj_attn",
    )(*args)


def _attn_block(kwin, vtw, qts, sinks, ok_prev, ok_next):
    n_keys, nq = kwin.shape[0], qts.shape[1]
    lane = lax.broadcasted_iota(jnp.int32, kwin.shape, 1)
    kbd = jnp.concatenate([jnp.where(lane < A_HEAD_DIM, kwin, jnp.zeros_like(kwin)),
                           jnp.where(lane >= A_HEAD_DIM, kwin, jnp.zeros_like(kwin))], axis=0)
    ones = jnp.where(lax.broadcasted_iota(jnp.int32, (SUM_ROWS, n_keys), 0) == 0, 1.0, 0.0).astype(BF16)

    st = _dot(kbd, qts).astype(BF16)
    neg = jnp.asarray(NEG_INF, BF16)
    outs = []
    for hh in range(2):
        sh = st[hh * n_keys:(hh + 1) * n_keys]
        if ok_prev is not None:
            parts = [jnp.where(ok_prev, sh[0:BLOCK], neg),
                     sh[BLOCK:2 * BLOCK],
                     jnp.where(ok_next, sh[2 * BLOCK:3 * BLOCK], neg),
                     sh[3 * BLOCK:]]
        else:
            parts = [sh]
        mx = functools.reduce(jnp.maximum, [jnp.max(t, axis=0, keepdims=True) for t in parts])
        mxb = jnp.maximum(mx.astype(F32), sinks[hh]).astype(BF16)
        mx = mxb.astype(F32)
        probs = jnp.concatenate([jnp.exp2(t - mxb) for t in parts], axis=0)
        vt_h = jnp.concatenate([vtw[hh * A_HEAD_DIM:(hh + 1) * A_HEAD_DIM], ones], axis=0)
        ot = _dot(vt_h, probs)
        den = ot[A_HEAD_DIM:A_HEAD_DIM + 1] + jnp.exp2(sinks[hh] - mx)
        outs.append(ot[0:A_HEAD_DIM] * (1.0 / den))
    return jnp.concatenate(outs, axis=0)


def _pair_cast_rows(src_ref, dst_ref):
    @pl.when((pl.program_id(0) == 0) & (pl.program_id(1) == 0))
    def _():
        for p in range(A_GROUP):
            for half, h in enumerate((p, p + A_GROUP)):
                lo = p * LANES + half * A_HEAD_DIM
                dst_ref[lo:lo + A_HEAD_DIM, :] = src_ref[h * A_HEAD_DIM:(h + 1) * A_HEAD_DIM, :].astype(BF16)


def _attn_kernel(local, n_steps, sink_ref, qt_ref, *rest):
    n_blk = rest[-1].shape[1] // BLOCK
    if local:
        (kp_ref, kc_ref, kn_ref, vp_ref, vc_ref, vn_ref, kx_ref, vx_ref,
         sg_ref, x_ref, m_ref, w_ref, o_ref) = rest
        last = slice((n_blk - 1) * BLOCK, n_blk * BLOCK)
        k_blocks = ([kp_ref[0, last]] + [kc_ref[0, i * BLOCK:(i + 1) * BLOCK] for i in range(n_blk)]
                    + [kn_ref[0, 0:BLOCK]])
        v_blocks = ([vp_ref[0, :, last]] + [vc_ref[0, :, i * BLOCK:(i + 1) * BLOCK] for i in range(n_blk)]
                    + [vn_ref[0, :, 0:BLOCK]])
    else:
        kx_ref, vx_ref, sg_ref, x_ref, m_ref, w_ref, o_ref = rest
    step = pl.program_id(1)
    nq = A_GROUP * BLOCK
    chunk_of = lax.broadcasted_iota(jnp.int32, (1, nq), 1) // BLOCK
    sinks = []
    for hh in range(2):
        sk = jnp.full((1, nq), sink_ref[hh] * LOG2_E, F32)
        for c in range(1, A_GROUP):
            sk = jnp.where(chunk_of == c, sink_ref[2 * c + hh] * LOG2_E, sk)
        sinks.append(sk)
    if local:
        kj = lax.broadcasted_iota(jnp.int32, (BLOCK, nq), 0)
        qi = lax.broadcasted_iota(jnp.int32, (BLOCK, nq), 1) % BLOCK

    rows_out = []
    for blk in range(n_blk):
        qts = jnp.concatenate([qt_ref[0, c * LANES:(c + 1) * LANES, blk * BLOCK:(blk + 1) * BLOCK]
                               for c in range(A_GROUP)], axis=1)
        if local:
            kwin = jnp.concatenate(k_blocks[blk:blk + 3] + [kx_ref[0]], axis=0)
            vtw = jnp.concatenate(v_blocks[blk:blk + 3] + [vx_ref[0]], axis=1)
            ok_prev = (kj >= qi) & (step > 0) if blk == 0 else (kj >= qi)
            ok_next = (kj <= qi) & (step < n_steps - 1) if blk == n_blk - 1 else (kj <= qi)
        else:
            kwin, vtw, ok_prev, ok_next = kx_ref[0], vx_ref[0], None, None
        ot = _attn_block(kwin, vtw, qts, sinks, ok_prev, ok_next)
        rows = slice(blk * BLOCK, (blk + 1) * BLOCK)
        outs = []
        for c in range(A_GROUP):
            sg = sg_ref[0, rows, c * LANES:(c + 1) * LANES].astype(F32)
            outs.append((ot[:, c * BLOCK:(c + 1) * BLOCK].T * sg).astype(BF16))
        rows_out.append(jnp.concatenate(outs, axis=1))
    y = _dot(jnp.concatenate(rows_out, axis=0), w_ref[...])
    gate = m_ref[0][:, 2 * D:3 * D]
    o_ref[0] = x_ref[0] + gate * y


def _attention(sink, qt, k, vt, kx, vxt, sg, x, mod3, w_out):
    b, s, _ = sg.shape
    rows = A_BLOCKS_PER_STEP * BLOCK
    assert s % rows == 0
    ns = s // rows
    n_ctx = kx.shape[1]
    blk = lambda w: pl.BlockSpec((1, rows, w), lambda bi, i: (bi, i, 0))
    blk_t = lambda w: pl.BlockSpec((1, w, rows), lambda bi, i: (bi, 0, i))
    lo = lambda i: jnp.maximum(i - 1, 0)
    hi = lambda i: jnp.minimum(i + 1, ns - 1)
    in_specs = [pl.BlockSpec(memory_space=pltpu.SMEM), blk_t(A_WIDTH),
                pl.BlockSpec((1, rows, A_KV_WIDTH), lambda bi, i: (bi, lo(i), 0)),
                blk(A_KV_WIDTH),
                pl.BlockSpec((1, rows, A_KV_WIDTH), lambda bi, i: (bi, hi(i), 0)),
                pl.BlockSpec((1, A_KV_WIDTH, rows), lambda bi, i: (bi, 0, lo(i))),
                blk_t(A_KV_WIDTH),
                pl.BlockSpec((1, A_KV_WIDTH, rows), lambda bi, i: (bi, 0, hi(i))),
                pl.BlockSpec((1, n_ctx, A_KV_WIDTH), lambda bi, i: (bi, 0, 0)),
                pl.BlockSpec((1, A_KV_WIDTH, n_ctx), lambda bi, i: (bi, 0, 0)),
                blk(A_WIDTH), blk(D),
                pl.BlockSpec((1, 1, 3 * D), lambda bi, i: (bi, 0, 0)),
                pl.BlockSpec((A_WIDTH, D), lambda bi, i: (0, 0))]
    return pl.pallas_call(
        functools.partial(_attn_kernel, True, ns),
        grid=(b, ns),
        in_specs=in_specs,
        out_specs=blk(D),
        out_shape=jax.ShapeDtypeStruct((b, s, D), F32),
        compiler_params=_params("parallel", "parallel"),
        name="attn_local",
    )(sink, qt, k, k, k, vt, vt, vt, kx, vxt, sg, x, mod3, w_out)


def _ctx_attn_kernel(sink_ref, x_ref, m_ref, g_ref, win_ref, wout_ref, k_ref, vt_ref, o_ref,
                     win_bf_ref, wout_bf_ref, qt_scr, sg_scr):
    _pair_cast_columns(win_ref, win_bf_ref)
    _pair_cast_rows(wout_ref, wout_bf_ref)
    _inproj_attn_kernel(False, x_ref, m_ref, g_ref, win_bf_ref, qt_scr, k_ref, vt_ref, sg_scr)
    _attn_kernel(False, 1, sink_ref, qt_scr, k_ref, vt_ref, sg_scr, x_ref, m_ref, wout_bf_ref, o_ref)


def _ctx_attention(sink, xc, mod3, mod_row, norm_g, w_in, w_out):
    b, n, _ = xc.shape
    const = lambda shape, **kw: pl.BlockSpec(shape, lambda bi, i: (0,) * len(shape), **kw)
    return pl.pallas_call(
        _ctx_attn_kernel,
        grid=(b, 1),
        in_specs=[pl.BlockSpec(memory_space=pltpu.SMEM),
                  pl.BlockSpec((1, n, D), lambda bi, i: (bi, 0, 0)),
                  pl.BlockSpec((1, 1, 3 * D), lambda bi, i: (mod_row, 0, 0)),
                  const((1, D)),
                  const(w_in.shape, pipeline_mode=pl.Buffered(1)),
                  const(w_out.shape, pipeline_mode=pl.Buffered(1))],
        out_specs=(pl.BlockSpec((1, n, A_KV_WIDTH), lambda bi, i: (bi, 0, 0)),
                   pl.BlockSpec((1, A_KV_WIDTH, n), lambda bi, i: (bi, 0, 0)),
                   pl.BlockSpec((1, n, D), lambda bi, i: (bi, 0, 0)),
                   const(w_in.shape), const(w_out.shape)),
        out_shape=(jax.ShapeDtypeStruct((b, n, A_KV_WIDTH), BF16),
                   jax.ShapeDtypeStruct((b, A_KV_WIDTH, n), BF16),
                   jax.ShapeDtypeStruct((b, n, D), F32),
                   jax.ShapeDtypeStruct(w_in.shape, BF16), jax.ShapeDtypeStruct(w_out.shape, BF16)),
        scratch_shapes=[pltpu.VMEM((1, A_WIDTH, n), BF16), pltpu.VMEM((1, n, A_WIDTH), BF16)],
        compiler_params=_params("arbitrary", "arbitrary"),
        name="ctx_attn",
    )(sink, xc, mod3, norm_g.reshape(1, D), w_in, w_out)


def _chunk_cumsum(x, tri):
    return _dot(tri, x.astype(BF16))


def _inproj_gla_kernel(chunk, with_q, x_ref, m_ref, g_ref, w_in_ref, wa1_ref, wa2_ref, ba_ref, tri_ref, *outs):
    if with_q:
        w_ref = w_in_ref
        per_dir = (outs[0:3], outs[3:6])
        v_ref, sg_ref = outs[6:]
    else:
        w_ref = outs[2]
        _cast_once(w_in_ref, w_ref)
        per_dir = ((outs[0],), (outs[1],))
    tm = x_ref.shape[1]
    hb = _norm_mod(x_ref[0], g_ref[...], m_ref[0]).astype(BF16)
    r = _dot(hb, wa1_ref[...])
    z = _dot(r.astype(BF16), wa2_ref[...]) + ba_ref[...]
    e = jnp.exp2(jnp.abs(z) * -LOG2_E)
    la = (jnp.minimum(z, 0.0) * LOG2_E - jnp.log2(1.0 + e)) * (1.0 / GATE_TEMP)
    k = _dot(hb, w_ref[:, B_K_WIDTH:2 * B_K_WIDTH])
    if with_q:
        q = _dot(hb, w_ref[:, 0:B_K_WIDTH]) * K_SCALE
    n_chunks = tm // chunk
    k_ends = []
    for reverse in (False, True):
        lad = la[:, B_K_WIDTH:] if reverse else la[:, 0:B_K_WIDTH]
        refs = per_dir[1] if reverse else per_dir[0]
        tots, kis, k_end = [], [], None
        tri = tri_ref[1 if reverse else 0]
        tb = tri.shape[0]
        cum_all = jnp.concatenate([_chunk_cumsum(lad[r * tb:(r + 1) * tb], tri) for r in range(tm // tb)], axis=0)
        for c in range(n_chunks):
            rows = slice(c * chunk, (c + 1) * chunk)
            cum = cum_all[rows]
            tot = cum[0:1] if reverse else cum[chunk - 1:chunk]
            if not with_q:
                k_end = (k[rows] * jnp.exp2(tot - cum)).astype(BF16)
                continue
            refs[0][0, rows] = (q[rows] * jnp.exp2(cum)).astype(BF16)
            kis.append(k[rows] * jnp.exp2(-cum))
            tots.append(tot)
            if c % 2 == 1:
                pair = slice((c - 1) * chunk, (c + 1) * chunk)
                refs[1][0, :, pair] = jnp.concatenate(kis[-2:], axis=0).T.astype(BF16)
        if with_q:
            pad = [jnp.zeros((DEC_ROWS - n_chunks, B_K_WIDTH), F32)] if n_chunks < DEC_ROWS else []
            refs[2][0, 0] = jnp.exp2(jnp.concatenate(tots + pad, axis=0))
        else:
            k_ends.append(k_end)
    if with_q:
        sg_ref[0] = _silu(_dot(hb, w_ref[:, 2 * B_K_WIDTH + B_V_WIDTH:])).astype(BF16)
    v = _dot(hb, w_ref[:, 2 * B_K_WIDTH:2 * B_K_WIDTH + B_V_WIDTH]).astype(BF16)
    if with_q:
        v_ref[0] = v
    else:
        for refs, k_end in zip(per_dir, k_ends):
            for h in range(B_HEADS):
                ks = slice(h * B_KEY_DIM, (h + 1) * B_KEY_DIM)
                refs[0][0, h] = _dot_tn(k_end[:, ks], v[:, h * B_VAL_DIM:(h + 1) * B_VAL_DIM])


def _inproj_gla(x, mod3, mod_row, norm_g, w, wa1, wa2, ba, chunk, with_q):
    b, s, _ = x.shape
    tm = min(ROW_TILE, s)
    nt = s // tm
    cpt = tm // chunk
    tb = max(chunk, MXU_DIM)
    assert tm % tb == 0 and tb % chunk == 0
    if mod_row is None:
        mod_map = lambda bi, i: (bi, 0, 0)
    else:
        mod_map = lambda bi, i: (mod_row, 0, 0)
    const = lambda shape, **kw: pl.BlockSpec(shape, lambda bi, i: (0,) * len(shape), **kw)
    in_specs = [
        pl.BlockSpec((1, tm, D), lambda bi, i: (bi, i, 0)),
        pl.BlockSpec((1, 1, 3 * D), mod_map),
        const((1, D)), const(w.shape, **({} if with_q else {"pipeline_mode": pl.Buffered(1)})),
        const(wa1.shape), const(wa2.shape), const(ba.shape),
        const((2, tb, tb)),
    ]
    t_idx = np.arange(tb)
    same = (t_idx[:, None] // chunk) == (t_idx[None, :] // chunk)
    lower = same & (t_idx[None, :] <= t_idx[:, None])
    tri = jnp.asarray(np.stack([lower, lower.T]), dtype=BF16)
    rows = lambda w_: (jax.ShapeDtypeStruct((b, s, w_), BF16),
                       pl.BlockSpec((1, tm, w_), lambda bi, i: (bi, i, 0)))
    cols = (jax.ShapeDtypeStruct((b, B_K_WIDTH, s), BF16),
            pl.BlockSpec((1, B_K_WIDTH, tm), lambda bi, i: (bi, 0, i)))
    decs = (jax.ShapeDtypeStruct((b, nt, DEC_ROWS, B_K_WIDTH), F32),
            pl.BlockSpec((1, 1, DEC_ROWS, B_K_WIDTH), lambda bi, i: (bi, i, 0, 0)))
    if with_q:
        assert cpt <= DEC_ROWS and cpt % 2 == 0
        outs = [rows(B_K_WIDTH), cols, decs] * 2 + [rows(B_V_WIDTH)] * 2
    else:
        assert nt == 1 and cpt == 1
        state = (jax.ShapeDtypeStruct((b, B_HEADS, B_KEY_DIM, B_VAL_DIM), F32),
                 pl.BlockSpec((1, B_HEADS, B_KEY_DIM, B_VAL_DIM), lambda bi, i: (bi, 0, 0, 0)))
        outs = [state] * 2 + [(jax.ShapeDtypeStruct(w.shape, BF16), const(w.shape))]
    return pl.pallas_call(
        functools.partial(_inproj_gla_kernel, chunk, with_q),
        grid=(b, nt),
        in_specs=in_specs,
        out_specs=tuple(o[1] for o in outs),
        out_shape=tuple(o[0] for o in outs),
        compiler_params=_params("parallel", "parallel") if with_q else _params("arbitrary", "arbitrary"),
        name="inproj_gla" if with_q else "inproj_gla_ctx",
    )(x, mod3, norm_g.reshape(1, D), w, wa1, wa2, ba, tri)


def _scan_tile(reverse, qd_ref, kit_ref, dec_ref, v_ref, st_ref):
    pair_rows = 2 * CHUNK
    ti = lax.broadcasted_iota(jnp.int32, (pair_rows, pair_rows), 0)
    si = lax.broadcasted_iota(jnp.int32, (pair_rows, pair_rows), 1)
    same = (ti // CHUNK) == (si // CHUNK)
    if reverse:
        use_inv = same & (ti <= si)
        use_end = (ti < CHUNK) & (si >= CHUNK)
        second_row = lax.broadcasted_iota(jnp.int32, (pair_rows, B_KEY_DIM), 0) < CHUNK
        first_col = lax.broadcasted_iota(jnp.int32, (B_KEY_DIM, pair_rows), 1) >= CHUNK
    else:
        use_inv = same & (ti >= si)
        use_end = (ti >= CHUNK) & (si < CHUNK)
        second_row = lax.broadcasted_iota(jnp.int32, (pair_rows, B_KEY_DIM), 0) >= CHUNK
        first_col = lax.broadcasted_iota(jnp.int32, (B_KEY_DIM, pair_rows), 1) < CHUNK
    n_pairs = SCAN_TILE // pair_rows
    order = range(n_pairs - 1, -1, -1) if reverse else range(n_pairs)
    n_dec = dec_ref.shape[1] * DEC_ROWS
    dec = dec_ref[0].reshape(n_dec, B_K_WIDTH)
    dect = jnp.concatenate([dec, jnp.zeros((LANES - n_dec, B_K_WIDTH), F32)], axis=0).T
    heads = []
    for h in range(B_HEADS):
        ks = slice(h * B_KEY_DIM, (h + 1) * B_KEY_DIM)
        vs = slice(h * B_VAL_DIM, (h + 1) * B_VAL_DIM)
        st = st_ref[h]
        o_rows = [None] * n_pairs
        for p in order:
            rows = slice(p * pair_rows, (p + 1) * pair_rows)
            c_first, c_second = (2 * p + 1, 2 * p) if reverse else (2 * p, 2 * p + 1)
            qd = qd_ref[0, rows, ks]
            kit = kit_ref[0, ks, rows]
            vh = v_ref[0, rows, vs]
            d_first, d_second = dect[ks, c_first:c_first + 1], dect[ks, c_second:c_second + 1]
            ki32 = kit.astype(F32)
            s2 = _dot(qd, jnp.concatenate([kit, (ki32 * d_first).astype(BF16)], axis=1))
            a = jnp.where(use_inv, s2[:, 0:pair_rows], jnp.where(use_end, s2[:, pair_rows:], 0.0)).astype(BF16)
            q_pair = jnp.where(second_row, qd.astype(F32) * dec[c_first:c_first + 1, ks], qd.astype(F32))
            k_pair = ki32 * jnp.where(first_col, d_first * d_second, d_second)
            o_rows[p] = _dot(a, vh) + _dot(q_pair.astype(BF16), st.astype(BF16))
            st = st * (d_first * d_second) + _dot(k_pair.astype(BF16), vh)
        st_ref[h] = st
        heads.append(jnp.concatenate(o_rows, axis=0))
    return jnp.concatenate(heads, axis=1)


def _gla_scan_kernel(nt, qdf_ref, kitf_ref, decf_ref, qdb_ref, kitb_ref, decb_ref,
                     v_ref, sf_ref, sb_ref, sg_ref, x_ref, m_ref, hn_ref, w32_ref, fn_ref,
                     o_ref, st_ref, of_ref, w_ref):
    _cast_once(w32_ref, w_ref)
    j = pl.program_id(1)

    @pl.when(j == 0)
    def _():
        st_ref[...] = sf_ref[0]

    @pl.when(j == nt)
    def _():
        st_ref[...] = sb_ref[0]

    @pl.when(j < nt)
    def _():
        of_ref[j] = _scan_tile(False, qdf_ref, kitf_ref, decf_ref, v_ref, st_ref)

    @pl.when(j >= nt)
    def _():
        o = _scan_tile(True, qdb_ref, kitb_ref, decb_ref, v_ref, st_ref) + of_ref[2 * nt - 1 - j]
        normed = []
        for h in range(B_HEADS):
            oh = o[:, h * B_VAL_DIM:(h + 1) * B_VAL_DIM]
            normed.append(oh * lax.rsqrt(jnp.mean(oh * oh, axis=-1, keepdims=True) + EPS))
        of = jnp.concatenate(normed, axis=1) * hn_ref[...]
        y = _dot((of * sg_ref[0].astype(F32)).astype(BF16), w_ref[...])
        xn = x_ref[0] + m_ref[0][:, 2 * D:3 * D] * y
        ms = jnp.mean(xn * xn, axis=-1, keepdims=True)
        o_ref[0] = xn * lax.rsqrt(ms + EPS) * fn_ref[...]


def _gla_scan(fwd, bwd, v, s_f, s_b, sg, x, mod3, head_g, w_out, final_g):
    b, s, _ = v.shape
    assert SCAN_TILE % ROW_TILE == 0 and ROW_TILE // CHUNK == DEC_ROWS
    dec_blocks = SCAN_TILE // ROW_TILE
    nt = s // SCAN_TILE
    t_fwd = lambda j: jnp.minimum(j, nt - 1)
    t_bwd = lambda j: 2 * nt - 1 - jnp.maximum(j, nt)
    t_both = lambda j: jnp.where(j < nt, j, 2 * nt - 1 - j)

    def direction(t):
        rows = pl.BlockSpec((1, SCAN_TILE, B_K_WIDTH), lambda bi, j: (bi, t(j), 0))
        cols = pl.BlockSpec((1, B_K_WIDTH, SCAN_TILE), lambda bi, j: (bi, 0, t(j)))
        decs = pl.BlockSpec((1, dec_blocks, DEC_ROWS, B_K_WIDTH), lambda bi, j: (bi, t(j), 0, 0))
        return [rows, cols, decs]

    const = lambda shape: pl.BlockSpec(shape, lambda bi, j: (0,) * len(shape))
    state = pl.BlockSpec((1, B_HEADS, B_KEY_DIM, B_VAL_DIM), lambda bi, j: (bi, 0, 0, 0))
    out_rows = lambda w_: pl.BlockSpec((1, SCAN_TILE, w_), lambda bi, j: (bi, t_bwd(j), 0))
    in_specs = direction(t_fwd) + direction(t_bwd) + [
        pl.BlockSpec((1, SCAN_TILE, B_V_WIDTH), lambda bi, j: (bi, t_both(j), 0)),
        state, state, out_rows(B_V_WIDTH), out_rows(D),
        pl.BlockSpec((1, 1, 3 * D), lambda bi, j: (bi, 0, 0)),
        const((1, B_V_WIDTH)),
        pl.BlockSpec((B_V_WIDTH, D), lambda bi, j: (0, 0), pipeline_mode=pl.Buffered(1)),
        const((1, D))]
    return pl.pallas_call(
        functools.partial(_gla_scan_kernel, nt),
        grid=(b, 2 * nt),
        in_specs=in_specs,
        out_specs=out_rows(D),
        out_shape=jax.ShapeDtypeStruct((b, s, D), F32),
        scratch_shapes=[pltpu.VMEM((B_HEADS, B_KEY_DIM, B_VAL_DIM), F32),
                        pltpu.VMEM((nt, SCAN_TILE, B_V_WIDTH), F32),
                        pltpu.VMEM((B_V_WIDTH, D), BF16)],
        compiler_params=_params("arbitrary", "arbitrary", vmem_limit=SCAN_VMEM_LIMIT),
        name="gla_scan",
    )(*fwd, *bwd, v, s_f, s_b, sg, x, mod3, head_g.reshape(1, B_V_WIDTH), w_out, final_g.reshape(1, D))


def _rope_tables(n_tokens):
    rows_n = n_tokens // GRID_W
    row = np.repeat(np.arange(rows_n, dtype=np.float32), GRID_W)
    col = np.tile(np.arange(GRID_W, dtype=np.float32), rows_n)
    inv_freq = (np.float32(ROPE_BASE) ** (-np.arange(ROPE_FREQS, dtype=np.float32) / np.float32(ROPE_FREQS)))
    inv_freq = inv_freq.astype(np.float32)
    ang = np.stack([row[:, None] * inv_freq, col[:, None] * inv_freq], axis=1)
    cos, sin = np.cos(ang).astype(np.float32), np.sin(ang).astype(np.float32)
    zero = np.zeros_like(sin)
    tile = lambda t: np.tile(t.reshape(n_tokens, A_HEAD_DIM), (1, LANES // A_HEAD_DIM))
    return (tile(np.stack([cos, cos], axis=2)),
            tile(np.stack([-sin, zero], axis=2)),
            tile(np.stack([zero, sin], axis=2)))


def _pair_heads(t, axis):
    shape = t.shape
    t = t.reshape(shape[:axis] + (A_KV_HEADS, A_GROUP, -1) + shape[axis + 1:])
    return jnp.swapaxes(t, axis, axis + 1).reshape(shape)


def kernel(x, c, ctx, c_ctx, l0_norm_g, l0_w_ada, l0_b_ada, l0_w_in, l0_sink, l0_w_out, l1_norm_g, l1_w_ada, l1_b_ada, l1_w_in, l1_wa1_f, l1_wa2_f, l1_ba_f, l1_wa1_b, l1_wa2_b, l1_ba_b, l1_head_norm_g, l1_w_out, final_norm_g):
    b, s, _ = x.shape
    ctx_row = b

    cvec = jnp.concatenate([c, c_ctx[None, :], jnp.zeros((MOD_ROWS - b - 1, D), F32)], axis=0)
    mod0 = _modulation(cvec, l0_w_ada, l0_b_ada).reshape(MOD_ROWS, 1, 3 * D)
    mod1 = _modulation(cvec, l1_w_ada, l1_b_ada).reshape(MOD_ROWS, 1, 3 * D)

    sink = _pair_heads(l0_sink.astype(F32), 0)
    kc, vct, xc1, w0_in, w0_out = _ctx_attention(sink, ctx, mod0, ctx_row, l0_norm_g, l0_w_in, l0_w_out)
    qt, k, vt, sg = _inproj_attn(x, mod0, l0_norm_g, w0_in, _rope_tables(s))
    x1 = _attention(sink, qt, k, vt, kc, vct, sg, x, mod0, w0_out)

    wa1 = jnp.concatenate([l1_wa1_f, l1_wa1_b, jnp.zeros((D, LANES - 2 * GATE_RANK), F32)], axis=1).astype(BF16)
    wa2 = jnp.zeros((LANES, 2 * B_K_WIDTH), F32)
    wa2 = wa2.at[0:GATE_RANK, 0:B_K_WIDTH].set(l1_wa2_f)
    wa2 = wa2.at[GATE_RANK:2 * GATE_RANK, B_K_WIDTH:].set(l1_wa2_b).astype(BF16)
    ba = jnp.concatenate([l1_ba_f, l1_ba_b]).reshape(1, 2 * B_K_WIDTH)
    s_f, s_b, w1 = _inproj_gla(xc1, mod1, ctx_row, l1_norm_g, l1_w_in, wa1, wa2, ba, ctx.shape[1], False)
    outs = _inproj_gla(x1, mod1, None, l1_norm_g, w1, wa1, wa2, ba, CHUNK, True)
    v1, sg1 = outs[6:]
    return _gla_scan(outs[0:3], outs[3:6], v1, s_f, s_b, sg1, x1, mod1,
                     l1_head_norm_g, l1_w_out, final_norm_g)
```

```python
import functools

import jax
import jax.numpy as jnp
import numpy as np
from jax import lax
from jax.experimental import pallas as pl
from jax.experimental.pallas import tpu as pltpu

F32 = jnp.float32
BF16 = jnp.bfloat16

D = 1024
GRID_W = 64
EPS = 1e-6
NEG_INF = -1e30

A_HEADS = 16
A_KV_HEADS = 2
A_GROUP = A_HEADS // A_KV_HEADS
A_HEAD_DIM = 64
A_WIDTH = A_HEADS * A_HEAD_DIM
A_KV_WIDTH = A_KV_HEADS * A_HEAD_DIM
BLOCK = 128
ROPE_BASE = 10000.0
ROPE_FREQS = A_HEAD_DIM // 4
Q_SCALE = A_HEAD_DIM ** -0.5
LOG2_E = 1.4426950408889634

B_HEADS = 4
B_K_WIDTH = D // 2
B_V_WIDTH = D
B_KEY_DIM = B_K_WIDTH // B_HEADS
B_VAL_DIM = B_V_WIDTH // B_HEADS
GATE_RANK = 16
GATE_TEMP = 16.0
CHUNK = 64
K_SCALE = B_KEY_DIM ** -0.5

LANES = 128
MXU_DIM = 256
MOD_ROWS = 16
ROW_TILE = 1024
A_ROW_TILE = 1024
SCAN_TILE = 1024
SUB_ROWS = 256
DEC_ROWS = ROW_TILE // CHUNK
SUM_ROWS = 16
A_BLOCKS_PER_STEP = 8
VMEM_LIMIT = 48 * 1024 * 1024
SCAN_VMEM_LIMIT = 56 * 1024 * 1024


def _params(*sem, vmem_limit=VMEM_LIMIT):
    return pltpu.CompilerParams(dimension_semantics=sem, vmem_limit_bytes=vmem_limit)


def _silu(x):
    return x / (1.0 + jnp.exp(-x))


def _dot(a, b):
    return jnp.dot(a, b, preferred_element_type=F32)


def _dot_nt(a, b):
    return lax.dot_general(a, b, (((1,), (1,)), ((), ())), preferred_element_type=F32)


def _dot_tn(a, b):
    return lax.dot_general(a, b, (((0,), (0,)), ((), ())), preferred_element_type=F32)


def _norm_mod(x, g, m):
    ms = jnp.mean(x * x, axis=-1, keepdims=True)
    y = x * lax.rsqrt(ms + EPS) * g
    return y * (1.0 + m[:, D:2 * D]) + m[:, 0:D]


def _cast_once(src_ref, dst_ref):
    @pl.when((pl.program_id(0) == 0) & (pl.program_id(1) == 0))
    def _():
        def body(i, carry):
            rows = pl.ds(pl.multiple_of(i * LANES, LANES), LANES)
            dst_ref[rows, :] = src_ref[rows, :].astype(dst_ref.dtype)
            return carry
        lax.fori_loop(0, src_ref.shape[0] // LANES, body, 0)


def _mod_kernel(c_ref, w_ref, b_ref, o_ref):
    s = _silu(c_ref[...])
    o_ref[...] = _dot(s.astype(BF16), w_ref[...].astype(BF16)) + b_ref[...]


def _modulation(cvec, w_ada, b_ada):
    n = w_ada.shape[1] // D
    return pl.pallas_call(
        _mod_kernel,
        grid=(n,),
        in_specs=[
            pl.BlockSpec((MOD_ROWS, D), lambda j: (0, 0)),
            pl.BlockSpec((D, D), lambda j: (0, j)),
            pl.BlockSpec((1, D), lambda j: (0, j)),
        ],
        out_specs=pl.BlockSpec((MOD_ROWS, D), lambda j: (0, j)),
        out_shape=jax.ShapeDtypeStruct((MOD_ROWS, n * D), F32),
        compiler_params=_params("arbitrary"),
        name="modulation",
    )(cvec, w_ada, b_ada.reshape(1, n * D))


def _rope(t, cos, sin_hi, sin_lo):
    return t * cos + pltpu.roll(t, LANES - ROPE_FREQS, 1) * sin_hi + pltpu.roll(t, ROPE_FREQS, 1) * sin_lo


def _pair_cast_columns(src_ref, dst_ref):
    @pl.when((pl.program_id(0) == 0) & (pl.program_id(1) == 0))
    def _():
        kv_lo, kv_hi = A_WIDTH, A_WIDTH + 2 * A_KV_WIDTH

        def body(i, carry):
            rows = pl.ds(pl.multiple_of(i * LANES, LANES), LANES)
            src = src_ref[rows, :]
            dst_ref[rows, kv_lo:kv_hi] = src[:, kv_lo:kv_hi].astype(BF16)
            for base in (0, kv_hi):
                for p in range(A_GROUP):
                    a = base + p * A_HEAD_DIM
                    b = base + (p + A_GROUP) * A_HEAD_DIM
                    pair = jnp.concatenate([src[:, a:a + A_HEAD_DIM], src[:, b:b + A_HEAD_DIM]], axis=1)
                    dst_ref[rows, base + p * LANES:base + (p + 1) * LANES] = pair.astype(BF16)
            return carry
        lax.fori_loop(0, src_ref.shape[0] // LANES, body, 0)


def _inproj_attn_kernel(rope, x_ref, m_ref, g_ref, w_ref, *rest):
    wq_ref = w_ref.at[:, 0:A_WIDTH]
    wkv_ref = w_ref.at[:, A_WIDTH:A_WIDTH + 2 * A_KV_WIDTH]
    wg_ref = w_ref.at[:, A_WIDTH + 2 * A_KV_WIDTH:]
    if rope:
        cos_ref, shi_ref, slo_ref, qt_ref, k_ref, vt_ref, sg_ref = rest
    else:
        qt_ref, k_ref, vt_ref, sg_ref = rest
    for r in range(x_ref.shape[1] // SUB_ROWS):
        rows = slice(r * SUB_ROWS, (r + 1) * SUB_ROWS)
        if rope:
            cos, shi, slo = cos_ref[rows], shi_ref[rows], slo_ref[rows]
        hb = _norm_mod(x_ref[0, rows], g_ref[...], m_ref[0]).astype(BF16)
        q = _dot(hb, wq_ref[...])
        for j in range(A_WIDTH // LANES):
            qj = q[:, j * LANES:(j + 1) * LANES]
            if rope:
                qj = _rope(qj, cos, shi, slo)
            qt_ref[0, j * LANES:(j + 1) * LANES, rows] = (qj * (Q_SCALE * LOG2_E)).T.astype(BF16)
        kv = _dot(hb, wkv_ref[...])
        k = kv[:, 0:A_KV_WIDTH]
        if rope:
            k = _rope(k, cos, shi, slo)
        k_ref[0, rows] = k.astype(BF16)
        vt_ref[0, :, rows] = kv[:, A_KV_WIDTH:].T.astype(BF16)
        g = _dot(hb, wg_ref[...])
        sg_ref[0, rows] = _silu(g).astype(BF16)


def _inproj_attn(x, mod3, norm_g, w_in, tables):
    b, s, _ = x.shape
    tm = min(A_ROW_TILE, s)
    in_specs = [
        pl.BlockSpec((1, tm, D), lambda bi, i: (bi, i, 0)),
        pl.BlockSpec((1, 1, 3 * D), lambda bi, i: (bi, 0, 0)),
        pl.BlockSpec((1, D), lambda bi, i: (0, 0)),
        pl.BlockSpec(w_in.shape, lambda bi, i: (0, 0)),
    ] + [pl.BlockSpec((tm, LANES), lambda bi, i: (i, 0))] * 3
    args = [x, mod3, norm_g.reshape(1, D), w_in, *tables]
    out_shape = (
        jax.ShapeDtypeStruct((b, A_WIDTH, s), BF16),
        jax.ShapeDtypeStruct((b, s, A_KV_WIDTH), BF16),
        jax.ShapeDtypeStruct((b, A_KV_WIDTH, s), BF16),
        jax.ShapeDtypeStruct((b, s, A_WIDTH), BF16),
    )
    out_specs = (
        pl.BlockSpec((1, A_WIDTH, tm), lambda bi, i: (bi, 0, i)),
        pl.BlockSpec((1, tm, A_KV_WIDTH), lambda bi, i: (bi, i, 0)),
        pl.BlockSpec((1, A_KV_WIDTH, tm), lambda bi, i: (bi, 0, i)),
        pl.BlockSpec((1, tm, A_WIDTH), lambda bi, i: (bi, i, 0)),
    )
    return pl.pallas_call(
        functools.partial(_inproj_attn_kernel, True),
        grid=(b, s // tm),
        in_specs=in_specs,
        out_specs=out_specs,
        out_shape=out_shape,
        compiler_params=_params("parallel", "parallel"),
        name="inproj_attn",
    )(*args)


def _attn_block(kwin, vtw, qts, sinks, ok_prev, ok_next):
    n_keys, nq = kwin.shape[0], qts.shape[1]
    lane = lax.broadcasted_iota(jnp.int32, kwin.shape, 1)
    kbd = jnp.concatenate([jnp.where(lane < A_HEAD_DIM, kwin, jnp.zeros_like(kwin)),
                           jnp.where(lane >= A_HEAD_DIM, kwin, jnp.zeros_like(kwin))], axis=0)
    ones = jnp.where(lax.broadcasted_iota(jnp.int32, (SUM_ROWS, n_keys), 0) == 0, 1.0, 0.0).astype(BF16)

    st = _dot(kbd, qts).astype(BF16)
    neg = jnp.asarray(NEG_INF, BF16)
    outs = []
    for hh in range(2):
        sh = st[hh * n_keys:(hh + 1) * n_keys]
        if ok_prev is not None:
            parts = [jnp.where(ok_prev, sh[0:BLOCK], neg),
                     sh[BLOCK:2 * BLOCK],
                     jnp.where(ok_next, sh[2 * BLOCK:3 * BLOCK], neg),
                     sh[3 * BLOCK:]]
        else:
            parts = [sh]
        mx = functools.reduce(jnp.maximum, [jnp.max(t, axis=0, keepdims=True) for t in parts])
        mxb = jnp.maximum(mx.astype(F32), sinks[hh]).astype(BF16)
        mx = mxb.astype(F32)
        probs = jnp.concatenate([jnp.exp2(t - mxb) for t in parts], axis=0)
        vt_h = jnp.concatenate([vtw[hh * A_HEAD_DIM:(hh + 1) * A_HEAD_DIM], ones], axis=0)
        ot = _dot(vt_h, probs)
        den = ot[A_HEAD_DIM:A_HEAD_DIM + 1] + jnp.exp2(sinks[hh] - mx)
        outs.append(ot[0:A_HEAD_DIM] * (1.0 / den))
    return jnp.concatenate(outs, axis=0)


def _pair_cast_rows(src_ref, dst_ref):
    @pl.when((pl.program_id(0) == 0) & (pl.program_id(1) == 0))
    def _():
        for p in range(A_GROUP):
            for half, h in enumerate((p, p + A_GROUP)):
                lo = p * LANES + half * A_HEAD_DIM
                dst_ref[lo:lo + A_HEAD_DIM, :] = src_ref[h * A_HEAD_DIM:(h + 1) * A_HEAD_DIM, :].astype(BF16)


def _attn_kernel(local, n_steps, sink_ref, qt_ref, *rest):
    n_blk = rest[-1].shape[1] // BLOCK
    if local:
        (kp_ref, kc_ref, kn_ref, vp_ref, vc_ref, vn_ref, kx_ref, vx_ref,
         sg_ref, x_ref, m_ref, w_ref, o_ref) = rest
        last = slice((n_blk - 1) * BLOCK, n_blk * BLOCK)
        k_blocks = ([kp_ref[0, last]] + [kc_ref[0, i * BLOCK:(i + 1) * BLOCK] for i in range(n_blk)]
                    + [kn_ref[0, 0:BLOCK]])
        v_blocks = ([vp_ref[0, :, last]] + [vc_ref[0, :, i * BLOCK:(i + 1) * BLOCK] for i in range(n_blk)]
                    + [vn_ref[0, :, 0:BLOCK]])
    else:
        kx_ref, vx_ref, sg_ref, x_ref, m_ref, w_ref, o_ref = rest
    step = pl.program_id(1)
    nq = A_GROUP * BLOCK
    chunk_of = lax.broadcasted_iota(jnp.int32, (1, nq), 1) // BLOCK
    sinks = []
    for hh in range(2):
        sk = jnp.full((1, nq), sink_ref[hh] * LOG2_E, F32)
        for c in range(1, A_GROUP):
            sk = jnp.where(chunk_of == c, sink_ref[2 * c + hh] * LOG2_E, sk)
        sinks.append(sk)
    if local:
        kj = lax.broadcasted_iota(jnp.int32, (BLOCK, nq), 0)
        qi = lax.broadcasted_iota(jnp.int32, (BLOCK, nq), 1) % BLOCK

    rows_out = []
    for blk in range(n_blk):
        qts = jnp.concatenate([qt_ref[0, c * LANES:(c + 1) * LANES, blk * BLOCK:(blk + 1) * BLOCK]
                               for c in range(A_GROUP)], axis=1)
        if local:
            kwin = jnp.concatenate(k_blocks[blk:blk + 3] + [kx_ref[0]], axis=0)
            vtw = jnp.concatenate(v_blocks[blk:blk + 3] + [vx_ref[0]], axis=1)
            ok_prev = (kj >= qi) & (step > 0) if blk == 0 else (kj >= qi)
            ok_next = (kj <= qi) & (step < n_steps - 1) if blk == n_blk - 1 else (kj <= qi)
        else:
            kwin, vtw, ok_prev, ok_next = kx_ref[0], vx_ref[0], None, None
        ot = _attn_block(kwin, vtw, qts, sinks, ok_prev, ok_next)
        rows = slice(blk * BLOCK, (blk + 1) * BLOCK)
        outs = []
        for c in range(A_GROUP):
            sg = sg_ref[0, rows, c * LANES:(c + 1) * LANES].astype(F32)
            outs.append((ot[:, c * BLOCK:(c + 1) * BLOCK].T * sg).astype(BF16))
        rows_out.append(jnp.concatenate(outs, axis=1))
    y = _dot(jnp.concatenate(rows_out, axis=0), w_ref[...])
    gate = m_ref[0][:, 2 * D:3 * D]
    o_ref[0] = x_ref[0] + gate * y


def _attention(sink, qt, k, vt, kx, vxt, sg, x, mod3, w_out):
    b, s, _ = sg.shape
    rows = A_BLOCKS_PER_STEP * BLOCK
    assert s % rows == 0
    ns = s // rows
    n_ctx = kx.shape[1]
    blk = lambda w: pl.BlockSpec((1, rows, w), lambda bi, i: (bi, i, 0))
    blk_t = lambda w: pl.BlockSpec((1, w, rows), lambda bi, i: (bi, 0, i))
    lo = lambda i: jnp.maximum(i - 1, 0)
    hi = lambda i: jnp.minimum(i + 1, ns - 1)
    in_specs = [pl.BlockSpec(memory_space=pltpu.SMEM), blk_t(A_WIDTH),
                pl.BlockSpec((1, rows, A_KV_WIDTH), lambda bi, i: (bi, lo(i), 0)),
                blk(A_KV_WIDTH),
                pl.BlockSpec((1, rows, A_KV_WIDTH), lambda bi, i: (bi, hi(i), 0)),
                pl.BlockSpec((1, A_KV_WIDTH, rows), lambda bi, i: (bi, 0, lo(i))),
                blk_t(A_KV_WIDTH),
                pl.BlockSpec((1, A_KV_WIDTH, rows), lambda bi, i: (bi, 0, hi(i))),
                pl.BlockSpec((1, n_ctx, A_KV_WIDTH), lambda bi, i: (bi, 0, 0)),
                pl.BlockSpec((1, A_KV_WIDTH, n_ctx), lambda bi, i: (bi, 0, 0)),
                blk(A_WIDTH), blk(D),
                pl.BlockSpec((1, 1, 3 * D), lambda bi, i: (bi, 0, 0)),
                pl.BlockSpec((A_WIDTH, D), lambda bi, i: (0, 0))]
    return pl.pallas_call(
        functools.partial(_attn_kernel, True, ns),
        grid=(b, ns),
        in_specs=in_specs,
        out_specs=blk(D),
        out_shape=jax.ShapeDtypeStruct((b, s, D), F32),
        compiler_params=_params("parallel", "parallel"),
        name="attn_local",
    )(sink, qt, k, k, k, vt, vt, vt, kx, vxt, sg, x, mod3, w_out)


def _ctx_attn_kernel(sink_ref, x_ref, m_ref, g_ref, win_ref, wout_ref, k_ref, vt_ref, o_ref,
                     win_bf_ref, wout_bf_ref, qt_scr, sg_scr):
    _pair_cast_columns(win_ref, win_bf_ref)
    _pair_cast_rows(wout_ref, wout_bf_ref)
    _inproj_attn_kernel(False, x_ref, m_ref, g_ref, win_bf_ref, qt_scr, k_ref, vt_ref, sg_scr)
    _attn_kernel(False, 1, sink_ref, qt_scr, k_ref, vt_ref, sg_scr, x_ref, m_ref, wout_bf_ref, o_ref)


def _ctx_attention(sink, xc, mod3, mod_row, norm_g, w_in, w_out):
    b, n, _ = xc.shape
    const = lambda shape, **kw: pl.BlockSpec(shape, lambda bi, i: (0,) * len(shape), **kw)
    return pl.pallas_call(
        _ctx_attn_kernel,
        grid=(b, 1),
        in_specs=[pl.BlockSpec(memory_space=pltpu.SMEM),
                  pl.BlockSpec((1, n, D), lambda bi, i: (bi, 0, 0)),
                  pl.BlockSpec((1, 1, 3 * D), lambda bi, i: (mod_row, 0, 0)),
                  const((1, D)),
                  const(w_in.shape, pipeline_mode=pl.Buffered(1)),
                  const(w_out.shape, pipeline_mode=pl.Buffered(1))],
        out_specs=(pl.BlockSpec((1, n, A_KV_WIDTH), lambda bi, i: (bi, 0, 0)),
                   pl.BlockSpec((1, A_KV_WIDTH, n), lambda bi, i: (bi, 0, 0)),
                   pl.BlockSpec((1, n, D), lambda bi, i: (bi, 0, 0)),
                   const(w_in.shape), const(w_out.shape)),
        out_shape=(jax.ShapeDtypeStruct((b, n, A_KV_WIDTH), BF16),
                   jax.ShapeDtypeStruct((b, A_KV_WIDTH, n), BF16),
                   jax.ShapeDtypeStruct((b, n, D), F32),
                   jax.ShapeDtypeStruct(w_in.shape, BF16), jax.ShapeDtypeStruct(w_out.shape, BF16)),
        scratch_shapes=[pltpu.VMEM((1, A_WIDTH, n), BF16), pltpu.VMEM((1, n, A_WIDTH), BF16)],
        compiler_params=_params("arbitrary", "arbitrary"),
        name="ctx_attn",
    )(sink, xc, mod3, norm_g.reshape(1, D), w_in, w_out)


def _chunk_cumsum(x, tri):
    return _dot(tri, x.astype(BF16))


def _inproj_gla_kernel(chunk, with_q, x_ref, m_ref, g_ref, w_in_ref, wa1_ref, wa2_ref, ba_ref, tri_ref, *outs):
    if with_q:
        w_ref = w_in_ref
        per_dir = (outs[0:3], outs[3:6])
        v_ref, sg_ref = outs[6:]
    else:
        w_ref = outs[2]
        _cast_once(w_in_ref, w_ref)
        per_dir = ((outs[0],), (outs[1],))
    tm = x_ref.shape[1]
    hb = _norm_mod(x_ref[0], g_ref[...], m_ref[0]).astype(BF16)
    r = _dot(hb, wa1_ref[...])
    z = _dot(r.astype(BF16), wa2_ref[...]) + ba_ref[...]
    e = jnp.exp2(jnp.abs(z) * -LOG2_E)
    la = (jnp.minimum(z, 0.0) * LOG2_E - jnp.log2(1.0 + e)) * (1.0 / GATE_TEMP)
    k = _dot(hb, w_ref[:, B_K_WIDTH:2 * B_K_WIDTH])
    if with_q:
        q = _dot(hb, w_ref[:, 0:B_K_WIDTH]) * K_SCALE
    n_chunks = tm // chunk
    k_ends = []
    for reverse in (False, True):
        lad = la[:, B_K_WIDTH:] if reverse else la[:, 0:B_K_WIDTH]
        refs = per_dir[1] if reverse else per_dir[0]
        tots, kis, k_end = [], [], None
        tri = tri_ref[1 if reverse else 0]
        tb = tri.shape[0]
        cum_all = jnp.concatenate([_chunk_cumsum(lad[r * tb:(r + 1) * tb], tri) for r in range(tm // tb)], axis=0)
        for c in range(n_chunks):
            rows = slice(c * chunk, (c + 1) * chunk)
            cum = cum_all[rows]
            tot = cum[0:1] if reverse else cum[chunk - 1:chunk]
            if not with_q:
                k_end = (k[rows] * jnp.exp2(tot - cum)).astype(BF16)
                continue
            refs[0][0, rows] = (q[rows] * jnp.exp2(cum)).astype(BF16)
            kis.append(k[rows] * jnp.exp2(-cum))
            tots.append(tot)
            if c % 2 == 1:
                pair = slice((c - 1) * chunk, (c + 1) * chunk)
                refs[1][0, :, pair] = jnp.concatenate(kis[-2:], axis=0).T.astype(BF16)
        if with_q:
            pad = [jnp.zeros((DEC_ROWS - n_chunks, B_K_WIDTH), F32)] if n_chunks < DEC_ROWS else []
            refs[2][0, 0] = jnp.exp2(jnp.concatenate(tots + pad, axis=0))
        else:
            k_ends.append(k_end)
    if with_q:
        sg_ref[0] = _silu(_dot(hb, w_ref[:, 2 * B_K_WIDTH + B_V_WIDTH:])).astype(BF16)
    v = _dot(hb, w_ref[:, 2 * B_K_WIDTH:2 * B_K_WIDTH + B_V_WIDTH]).astype(BF16)
    if with_q:
        v_ref[0] = v
    else:
        for refs, k_end in zip(per_dir, k_ends):
            for h in range(B_HEADS):
                ks = slice(h * B_KEY_DIM, (h + 1) * B_KEY_DIM)
                refs[0][0, h] = _dot_tn(k_end[:, ks], v[:, h * B_VAL_DIM:(h + 1) * B_VAL_DIM])


def _inproj_gla(x, mod3, mod_row, norm_g, w, wa1, wa2, ba, chunk, with_q):
    b, s, _ = x.shape
    tm = min(ROW_TILE, s)
    nt = s // tm
    cpt = tm // chunk
    tb = max(chunk, MXU_DIM)
    assert tm % tb == 0 and tb % chunk == 0
    if mod_row is None:
        mod_map = lambda bi, i: (bi, 0, 0)
    else:
        mod_map = lambda bi, i: (mod_row, 0, 0)
    const = lambda shape, **kw: pl.BlockSpec(shape, lambda bi, i: (0,) * len(shape), **kw)
    in_specs = [
        pl.BlockSpec((1, tm, D), lambda bi, i: (bi, i, 0)),
        pl.BlockSpec((1, 1, 3 * D), mod_map),
        const((1, D)), const(w.shape, **({} if with_q else {"pipeline_mode": pl.Buffered(1)})),
        const(wa1.shape), const(wa2.shape), const(ba.shape),
        const((2, tb, tb)),
    ]
    t_idx = np.arange(tb)
    same = (t_idx[:, None] // chunk) == (t_idx[None, :] // chunk)
    lower = same & (t_idx[None, :] <= t_idx[:, None])
    tri = jnp.asarray(np.stack([lower, lower.T]), dtype=BF16)
    rows = lambda w_: (jax.ShapeDtypeStruct((b, s, w_), BF16),
                       pl.BlockSpec((1, tm, w_), lambda bi, i: (bi, i, 0)))
    cols = (jax.ShapeDtypeStruct((b, B_K_WIDTH, s), BF16),
            pl.BlockSpec((1, B_K_WIDTH, tm), lambda bi, i: (bi, 0, i)))
    decs = (jax.ShapeDtypeStruct((b, nt, DEC_ROWS, B_K_WIDTH), F32),
            pl.BlockSpec((1, 1, DEC_ROWS, B_K_WIDTH), lambda bi, i: (bi, i, 0, 0)))
    if with_q:
        assert cpt <= DEC_ROWS and cpt % 2 == 0
        outs = [rows(B_K_WIDTH), cols, decs] * 2 + [rows(B_V_WIDTH)] * 2
    else:
        assert nt == 1 and cpt == 1
        state = (jax.ShapeDtypeStruct((b, B_HEADS, B_KEY_DIM, B_VAL_DIM), F32),
                 pl.BlockSpec((1, B_HEADS, B_KEY_DIM, B_VAL_DIM), lambda bi, i: (bi, 0, 0, 0)))
        outs = [state] * 2 + [(jax.ShapeDtypeStruct(w.shape, BF16), const(w.shape))]
    return pl.pallas_call(
        functools.partial(_inproj_gla_kernel, chunk, with_q),
        grid=(b, nt),
        in_specs=in_specs,
        out_specs=tuple(o[1] for o in outs),
        out_shape=tuple(o[0] for o in outs),
        compiler_params=_params("parallel", "parallel") if with_q else _params("arbitrary", "arbitrary"),
        name="inproj_gla" if with_q else "inproj_gla_ctx",
    )(x, mod3, norm_g.reshape(1, D), w, wa1, wa2, ba, tri)


def _scan_tile(reverse, qd_ref, kit_ref, dec_ref, v_ref, st_ref):
    pair_rows = 2 * CHUNK
    ti = lax.broadcasted_iota(jnp.int32, (pair_rows, pair_rows), 0)
    si = lax.broadcasted_iota(jnp.int32, (pair_rows, pair_rows), 1)
    same = (ti // CHUNK) == (si // CHUNK)
    if reverse:
        use_inv = same & (ti <= si)
        use_end = (ti < CHUNK) & (si >= CHUNK)
        second_row = lax.broadcasted_iota(jnp.int32, (pair_rows, B_KEY_DIM), 0) < CHUNK
        first_col = lax.broadcasted_iota(jnp.int32, (B_KEY_DIM, pair_rows), 1) >= CHUNK
    else:
        use_inv = same & (ti >= si)
        use_end = (ti >= CHUNK) & (si < CHUNK)
        second_row = lax.broadcasted_iota(jnp.int32, (pair_rows, B_KEY_DIM), 0) >= CHUNK
        first_col = lax.broadcasted_iota(jnp.int32, (B_KEY_DIM, pair_rows), 1) < CHUNK
    n_pairs = SCAN_TILE // pair_rows
    order = range(n_pairs - 1, -1, -1) if reverse else range(n_pairs)
    n_dec = dec_ref.shape[1] * DEC_ROWS
    dec = dec_ref[0].reshape(n_dec, B_K_WIDTH)
    dect = jnp.concatenate([dec, jnp.zeros((LANES - n_dec, B_K_WIDTH), F32)], axis=0).T
    heads = []
    for h in range(B_HEADS):
        ks = slice(h * B_KEY_DIM, (h + 1) * B_KEY_DIM)
        vs = slice(h * B_VAL_DIM, (h + 1) * B_VAL_DIM)
        st = st_ref[h]
        o_rows = [None] * n_pairs
        for p in order:
            rows = slice(p * pair_rows, (p + 1) * pair_rows)
            c_first, c_second = (2 * p + 1, 2 * p) if reverse else (2 * p, 2 * p + 1)
            qd = qd_ref[0, rows, ks]
            kit = kit_ref[0, ks, rows]
            vh = v_ref[0, rows, vs]
            d_first, d_second = dect[ks, c_first:c_first + 1], dect[ks, c_second:c_second + 1]
            ki32 = kit.astype(F32)
            s2 = _dot(qd, jnp.concatenate([kit, (ki32 * d_first).astype(BF16)], axis=1))
            a = jnp.where(use_inv, s2[:, 0:pair_rows], jnp.where(use_end, s2[:, pair_rows:], 0.0)).astype(BF16)
            q_pair = jnp.where(second_row, qd.astype(F32) * dec[c_first:c_first + 1, ks], qd.astype(F32))
            k_pair = ki32 * jnp.where(first_col, d_first * d_second, d_second)
            o_rows[p] = _dot(a, vh) + _dot(q_pair.astype(BF16), st.astype(BF16))
            st = st * (d_first * d_second) + _dot(k_pair.astype(BF16), vh)
        st_ref[h] = st
        heads.append(jnp.concatenate(o_rows, axis=0))
    return jnp.concatenate(heads, axis=1)


def _gla_scan_kernel(nt, qdf_ref, kitf_ref, decf_ref, qdb_ref, kitb_ref, decb_ref,
                     v_ref, sf_ref, sb_ref, sg_ref, x_ref, m_ref, hn_ref, w32_ref, fn_ref,
                     o_ref, st_ref, of_ref, w_ref):
    _cast_once(w32_ref, w_ref)
    j = pl.program_id(1)

    @pl.when(j == 0)
    def _():
        st_ref[...] = sf_ref[0]

    @pl.when(j == nt)
    def _():
        st_ref[...] = sb_ref[0]

    @pl.when(j < nt)
    def _():
        of_ref[j] = _scan_tile(False, qdf_ref, kitf_ref, decf_ref, v_ref, st_ref)

    @pl.when(j >= nt)
    def _():
        o = _scan_tile(True, qdb_ref, kitb_ref, decb_ref, v_ref, st_ref) + of_ref[2 * nt - 1 - j]
        normed = []
        for h in range(B_HEADS):
            oh = o[:, h * B_VAL_DIM:(h + 1) * B_VAL_DIM]
            normed.append(oh * lax.rsqrt(jnp.mean(oh * oh, axis=-1, keepdims=True) + EPS))
        of = jnp.concatenate(normed, axis=1) * hn_ref[...]
        y = _dot((of * sg_ref[0].astype(F32)).astype(BF16), w_ref[...])
        xn = x_ref[0] + m_ref[0][:, 2 * D:3 * D] * y
        ms = jnp.mean(xn * xn, axis=-1, keepdims=True)
        o_ref[0] = xn * lax.rsqrt(ms + EPS) * fn_ref[...]


def _gla_scan(fwd, bwd, v, s_f, s_b, sg, x, mod3, head_g, w_out, final_g):
    b, s, _ = v.shape
    assert SCAN_TILE % ROW_TILE == 0 and ROW_TILE // CHUNK == DEC_ROWS
    dec_blocks = SCAN_TILE // ROW_TILE
    nt = s // SCAN_TILE
    t_fwd = lambda j: jnp.minimum(j, nt - 1)
    t_bwd = lambda j: 2 * nt - 1 - jnp.maximum(j, nt)
    t_both = lambda j: jnp.where(j < nt, j, 2 * nt - 1 - j)

    def direction(t):
        rows = pl.BlockSpec((1, SCAN_TILE, B_K_WIDTH), lambda bi, j: (bi, t(j), 0))
        cols = pl.BlockSpec((1, B_K_WIDTH, SCAN_TILE), lambda bi, j: (bi, 0, t(j)))
        decs = pl.BlockSpec((1, dec_blocks, DEC_ROWS, B_K_WIDTH), lambda bi, j: (bi, t(j), 0, 0))
        return [rows, cols, decs]

    const = lambda shape: pl.BlockSpec(shape, lambda bi, j: (0,) * len(shape))
    state = pl.BlockSpec((1, B_HEADS, B_KEY_DIM, B_VAL_DIM), lambda bi, j: (bi, 0, 0, 0))
    out_rows = lambda w_: pl.BlockSpec((1, SCAN_TILE, w_), lambda bi, j: (bi, t_bwd(j), 0))
    in_specs = direction(t_fwd) + direction(t_bwd) + [
        pl.BlockSpec((1, SCAN_TILE, B_V_WIDTH), lambda bi, j: (bi, t_both(j), 0)),
        state, state, out_rows(B_V_WIDTH), out_rows(D),
        pl.BlockSpec((1, 1, 3 * D), lambda bi, j: (bi, 0, 0)),
        const((1, B_V_WIDTH)),
        pl.BlockSpec((B_V_WIDTH, D), lambda bi, j: (0, 0), pipeline_mode=pl.Buffered(1)),
        const((1, D))]
    return pl.pallas_call(
        functools.partial(_gla_scan_kernel, nt),
        grid=(b, 2 * nt),
        in_specs=in_specs,
        out_specs=out_rows(D),
        out_shape=jax.ShapeDtypeStruct((b, s, D), F32),
        scratch_shapes=[pltpu.VMEM((B_HEADS, B_KEY_DIM, B_VAL_DIM), F32),
                        pltpu.VMEM((nt, SCAN_TILE, B_V_WIDTH), F32),
                        pltpu.VMEM((B_V_WIDTH, D), BF16)],
        compiler_params=_params("arbitrary", "arbitrary", vmem_limit=SCAN_VMEM_LIMIT),
        name="gla_scan",
    )(*fwd, *bwd, v, s_f, s_b, sg, x, mod3, head_g.reshape(1, B_V_WIDTH), w_out, final_g.reshape(1, D))


def _rope_tables(n_tokens):
    rows_n = n_tokens // GRID_W
    row = np.repeat(np.arange(rows_n, dtype=np.float32), GRID_W)
    col = np.tile(np.arange(GRID_W, dtype=np.float32), rows_n)
    inv_freq = (np.float32(ROPE_BASE) ** (-np.arange(ROPE_FREQS, dtype=np.float32) / np.float32(ROPE_FREQS)))
    inv_freq = inv_freq.astype(np.float32)
    ang = np.stack([row[:, None] * inv_freq, col[:, None] * inv_freq], axis=1)
    cos, sin = np.cos(ang).astype(np.float32), np.sin(ang).astype(np.float32)
    zero = np.zeros_like(sin)
    tile = lambda t: np.tile(t.reshape(n_tokens, A_HEAD_DIM), (1, LANES // A_HEAD_DIM))
    return (tile(np.stack([cos, cos], axis=2)),
            tile(np.stack([-sin, zero], axis=2)),
            tile(np.stack([zero, sin], axis=2)))


def _pair_heads(t, axis):
    shape = t.shape
    t = t.reshape(shape[:axis] + (A_KV_HEADS, A_GROUP, -1) + shape[axis + 1:])
    return jnp.swapaxes(t, axis, axis + 1).reshape(shape)


def kernel(x, c, ctx, c_ctx, l0_norm_g, l0_w_ada, l0_b_ada, l0_w_in, l0_sink, l0_w_out, l1_norm_g, l1_w_ada, l1_b_ada, l1_w_in, l1_wa1_f, l1_wa2_f, l1_ba_f, l1_wa1_b, l1_wa2_b, l1_ba_b, l1_head_norm_g, l1_w_out, final_norm_g):
    b, s, _ = x.shape
    ctx_row = b

    cvec = jnp.concatenate([c, c_ctx[None, :], jnp.zeros((MOD_ROWS - b - 1, D), F32)], axis=0)
    mod0 = _modulation(cvec, l0_w_ada, l0_b_ada).reshape(MOD_ROWS, 1, 3 * D)
    mod1 = _modulation(cvec, l1_w_ada, l1_b_ada).reshape(MOD_ROWS, 1, 3 * D)

    sink = _pair_heads(l0_sink.astype(F32), 0)
    kc, vct, xc1, w0_in, w0_out = _ctx_attention(sink, ctx, mod0, ctx_row, l0_norm_g, l0_w_in, l0_w_out)
    qt, k, vt, sg = _inproj_attn(x, mod0, l0_norm_g, w0_in, _rope_tables(s))
    x1 = _attention(sink, qt, k, vt, kc, vct, sg, x, mod0, w0_out)

    wa1 = jnp.concatenate([l1_wa1_f, l1_wa1_b, jnp.zeros((D, LANES - 2 * GATE_RANK), F32)], axis=1).astype(BF16)
    wa2 = jnp.zeros((LANES, 2 * B_K_WIDTH), F32)
    wa2 = wa2.at[0:GATE_RANK, 0:B_K_WIDTH].set(l1_wa2_f)
    wa2 = wa2.at[GATE_RANK:2 * GATE_RANK, B_K_WIDTH:].set(l1_wa2_b).astype(BF16)
    ba = jnp.concatenate([l1_ba_f, l1_ba_b]).reshape(1, 2 * B_K_WIDTH)
    s_f, s_b, w1 = _inproj_gla(xc1, mod1, ctx_row, l1_norm_g, l1_w_in, wa1, wa2, ba, ctx.shape[1], False)
    outs = _inproj_gla(x1, mod1, None, l1_norm_g, w1, wa1, wa2, ba, CHUNK, True)
    v1, sg1 = outs[6:]
    return _gla_scan(outs[0:3], outs[3:6], v1, s_f, s_b, sg1, x1, mod1,
                     l1_head_norm_g, l1_w_out, final_norm_g)
```

```python
import functools

import jax
import jax.numpy as jnp
import numpy as np
from jax import lax
from jax.experimental import pallas as pl
from jax.experimental.pallas import tpu as pltpu

F32 = jnp.float32
BF16 = jnp.bfloat16

D = 1024
GRID_W = 64
EPS = 1e-6
NEG_INF = -1e30

A_HEADS = 16
A_KV_HEADS = 2
A_GROUP = A_HEADS // A_KV_HEADS
A_HEAD_DIM = 64
A_WIDTH = A_HEADS * A_HEAD_DIM
A_KV_WIDTH = A_KV_HEADS * A_HEAD_DIM
BLOCK = 128
ROPE_BASE = 10000.0
ROPE_FREQS = A_HEAD_DIM // 4
Q_SCALE = A_HEAD_DIM ** -0.5
LOG2_E = 1.4426950408889634

B_HEADS = 4
B_K_WIDTH = D // 2
B_V_WIDTH = D
B_KEY_DIM = B_K_WIDTH // B_HEADS
B_VAL_DIM = B_V_WIDTH // B_HEADS
GATE_RANK = 16
GATE_TEMP = 16.0
CHUNK = 64
K_SCALE = B_KEY_DIM ** -0.5

LANES = 128
MXU_DIM = 256
MOD_ROWS = 16
ROW_TILE = 1024
A_ROW_TILE = 1024
SCAN_TILE = 1024
SUB_ROWS = 256
DEC_ROWS = ROW_TILE // CHUNK
SUM_ROWS = 16
A_BLOCKS_PER_STEP = 8
VMEM_LIMIT = 48 * 1024 * 1024
SCAN_VMEM_LIMIT = 56 * 1024 * 1024


def _params(*sem, vmem_limit=VMEM_LIMIT):
    return pltpu.CompilerParams(dimension_semantics=sem, vmem_limit_bytes=vmem_limit)


def _silu(x):
    return x / (1.0 + jnp.exp(-x))


def _dot(a, b):
    return jnp.dot(a, b, preferred_element_type=F32)


def _dot_nt(a, b):
    return lax.dot_general(a, b, (((1,), (1,)), ((), ())), preferred_element_type=F32)


def _dot_tn(a, b):
    return lax.dot_general(a, b, (((0,), (0,)), ((), ())), preferred_element_type=F32)


def _norm_mod(x, g, m):
    ms = jnp.mean(x * x, axis=-1, keepdims=True)
    y = x * lax.rsqrt(ms + EPS) * g
    return y * (1.0 + m[:, D:2 * D]) + m[:, 0:D]


def _cast_once(src_ref, dst_ref):
    @pl.when((pl.program_id(0) == 0) & (pl.program_id(1) == 0))
    def _():
        def body(i, carry):
            rows = pl.ds(pl.multiple_of(i * LANES, LANES), LANES)
            dst_ref[rows, :] = src_ref[rows, :].astype(dst_ref.dtype)
            return carry
        lax.fori_loop(0, src_ref.shape[0] // LANES, body, 0)


def _mod_kernel(c_ref, w_ref, b_ref, o_ref):
    s = _silu(c_ref[...])
    o_ref[...] = _dot(s.astype(BF16), w_ref[...].astype(BF16)) + b_ref[...]


def _modulation(cvec, w_ada, b_ada):
    n = w_ada.shape[1] // D
    return pl.pallas_call(
        _mod_kernel,
        grid=(n,),
        in_specs=[
            pl.BlockSpec((MOD_ROWS, D), lambda j: (0, 0)),
            pl.BlockSpec((D, D), lambda j: (0, j)),
            pl.BlockSpec((1, D), lambda j: (0, j)),
        ],
        out_specs=pl.BlockSpec((MOD_ROWS, D), lambda j: (0, j)),
        out_shape=jax.ShapeDtypeStruct((MOD_ROWS, n * D), F32),
        compiler_params=_params("arbitrary"),
        name="modulation",
    )(cvec, w_ada, b_ada.reshape(1, n * D))


def _rope(t, cos, sin_hi, sin_lo):
    return t * cos + pltpu.roll(t, LANES - ROPE_FREQS, 1) * sin_hi + pltpu.roll(t, ROPE_FREQS, 1) * sin_lo


def _pair_cast_columns(src_ref, dst_ref):
    @pl.when((pl.program_id(0) == 0) & (pl.program_id(1) == 0))
    def _():
        kv_lo, kv_hi = A_WIDTH, A_WIDTH + 2 * A_KV_WIDTH

        def body(i, carry):
            rows = pl.ds(pl.multiple_of(i * LANES, LANES), LANES)
            src = src_ref[rows, :]
            dst_ref[rows, kv_lo:kv_hi] = src[:, kv_lo:kv_hi].astype(BF16)
            for base in (0, kv_hi):
                for p in range(A_GROUP):
                    a = base + p * A_HEAD_DIM
                    b = base + (p + A_GROUP) * A_HEAD_DIM
                    pair = jnp.concatenate([src[:, a:a + A_HEAD_DIM], src[:, b:b + A_HEAD_DIM]], axis=1)
                    dst_ref[rows, base + p * LANES:base + (p + 1) * LANES] = pair.astype(BF16)
            return carry
        lax.fori_loop(0, src_ref.shape[0] // LANES, body, 0)


def _inproj_attn_kernel(rope, x_ref, m_ref, g_ref, w_ref, *rest):
    wq_ref = w_ref.at[:, 0:A_WIDTH]
    wkv_ref = w_ref.at[:, A_WIDTH:A_WIDTH + 2 * A_KV_WIDTH]
    wg_ref = w_ref.at[:, A_WIDTH + 2 * A_KV_WIDTH:]
    if rope:
        cos_ref, shi_ref, slo_ref, qt_ref, k_ref, vt_ref, sg_ref = rest
    else:
        qt_ref, k_ref, vt_ref, sg_ref = rest
    assert x_ref.shape[1] % SUB_ROWS == 0
    for r in range(x_ref.shape[1] // SUB_ROWS):
        rows = slice(r * SUB_ROWS, (r + 1) * SUB_ROWS)
        if rope:
            cos, shi, slo = cos_ref[rows], shi_ref[rows], slo_ref[rows]
        hb = _norm_mod(x_ref[0, rows], g_ref[...], m_ref[0]).astype(BF16)
        q = _dot(hb, wq_ref[...])
        for j in range(A_WIDTH // LANES):
            qj = q[:, j * LANES:(j + 1) * LANES]
            if rope:
                qj = _rope(qj, cos, shi, slo)
            qt_ref[0, j * LANES:(j + 1) * LANES, rows] = (qj * (Q_SCALE * LOG2_E)).T.astype(BF16)
        kv = _dot(hb, wkv_ref[...])
        k = kv[:, 0:A_KV_WIDTH]
        if rope:
            k = _rope(k, cos, shi, slo)
        k_ref[0, rows] = k.astype(BF16)
        vt_ref[0, :, rows] = kv[:, A_KV_WIDTH:].T.astype(BF16)
        g = _dot(hb, wg_ref[...])
        sg_ref[0, rows] = _silu(g).astype(BF16)


def _inproj_attn(x, mod3, norm_g, w_in, tables):
    b, s, _ = x.shape
    tm = min(A_ROW_TILE, s)
    in_specs = [
        pl.BlockSpec((1, tm, D), lambda bi, i: (bi, i, 0)),
        pl.BlockSpec((1, 1, 3 * D), lambda bi, i: (bi, 0, 0)),
        pl.BlockSpec((1, D), lambda bi, i: (0, 0)),
        pl.BlockSpec(w_in.shape, lambda bi, i: (0, 0)),
    ] + [pl.BlockSpec((tm, LANES), lambda bi, i: (i, 0))] * 3
    args = [x, mod3, norm_g.reshape(1, D), w_in, *tables]
    out_shape = (
        jax.ShapeDtypeStruct((b, A_WIDTH, s), BF16),
        jax.ShapeDtypeStruct((b, s, A_KV_WIDTH), BF16),
        jax.ShapeDtypeStruct((b, A_KV_WIDTH, s), BF16),
        jax.ShapeDtypeStruct((b, s, A_WIDTH), BF16),
    )
    out_specs = (
        pl.BlockSpec((1, A_WIDTH, tm), lambda bi, i: (bi, 0, i)),
        pl.BlockSpec((1, tm, A_KV_WIDTH), lambda bi, i: (bi, i, 0)),
        pl.BlockSpec((1, A_KV_WIDTH, tm), lambda bi, i: (bi, 0, i)),
        pl.BlockSpec((1, tm, A_WIDTH), lambda bi, i: (bi, i, 0)),
    )
    return pl.pallas_call(
        functools.partial(_inproj_attn_kernel, True),
        grid=(b, s // tm),
        in_specs=in_specs,
        out_specs=out_specs,
        out_shape=out_shape,
        compiler_params=_params("parallel", "parallel"),
        name="inproj_attn",
    )(*args)


def _attn_block(kwin, vtw, qts, sinks, ok_prev, ok_next):
    n_keys, nq = kwin.shape[0], qts.shape[1]
    lane = lax.broadcasted_iota(jnp.int32, kwin.shape, 1)
    kbd = jnp.concatenate([jnp.where(lane < A_HEAD_DIM, kwin, jnp.zeros_like(kwin)),
                           jnp.where(lane >= A_HEAD_DIM, kwin, jnp.zeros_like(kwin))], axis=0)
    ones = jnp.where(lax.broadcasted_iota(jnp.int32, (SUM_ROWS, n_keys), 0) == 0, 1.0, 0.0).astype(BF16)

    st = _dot(kbd, qts).astype(BF16)
    neg = jnp.asarray(NEG_INF, BF16)
    outs = []
    for hh in range(2):
        sh = st[hh * n_keys:(hh + 1) * n_keys]
        if ok_prev is not None:
            parts = [jnp.where(ok_prev, sh[0:BLOCK], neg),
                     sh[BLOCK:2 * BLOCK],
                     jnp.where(ok_next, sh[2 * BLOCK:3 * BLOCK], neg),
                     sh[3 * BLOCK:]]
        else:
            parts = [sh]
        mx = functools.reduce(jnp.maximum, [jnp.max(t, axis=0, keepdims=True) for t in parts])
        mxb = jnp.maximum(mx.astype(F32), sinks[hh]).astype(BF16)
        mx = mxb.astype(F32)
        probs = jnp.concatenate([jnp.exp2(t - mxb) for t in parts], axis=0)
        vt_h = jnp.concatenate([vtw[hh * A_HEAD_DIM:(hh + 1) * A_HEAD_DIM], ones], axis=0)
        ot = _dot(vt_h, probs)
        den = ot[A_HEAD_DIM:A_HEAD_DIM + 1] + jnp.exp2(sinks[hh] - mx)
        outs.append(ot[0:A_HEAD_DIM] * (1.0 / den))
    return jnp.concatenate(outs, axis=0)


def _pair_cast_rows(src_ref, dst_ref):
    @pl.when((pl.program_id(0) == 0) & (pl.program_id(1) == 0))
    def _():
        for p in range(A_GROUP):
            for half, h in enumerate((p, p + A_GROUP)):
                lo = p * LANES + half * A_HEAD_DIM
                dst_ref[lo:lo + A_HEAD_DIM, :] = src_ref[h * A_HEAD_DIM:(h + 1) * A_HEAD_DIM, :].astype(BF16)


def _attn_kernel(local, n_steps, sink_ref, qt_ref, *rest):
    *rest, hb_ref = rest
    n_blk = hb_ref.shape[1] // BLOCK
    if local:
        (kp_ref, kc_ref, kn_ref, vp_ref, vc_ref, vn_ref, kx_ref, vx_ref,
         sg_ref, x_ref, m_ref, w_ref, m_next_ref, g_next_ref, o_ref) = rest
        last = slice((n_blk - 1) * BLOCK, n_blk * BLOCK)
        k_blocks = ([kp_ref[0, last]] + [kc_ref[0, i * BLOCK:(i + 1) * BLOCK] for i in range(n_blk)]
                    + [kn_ref[0, 0:BLOCK]])
        v_blocks = ([vp_ref[0, :, last]] + [vc_ref[0, :, i * BLOCK:(i + 1) * BLOCK] for i in range(n_blk)]
                    + [vn_ref[0, :, 0:BLOCK]])
    else:
        kx_ref, vx_ref, sg_ref, x_ref, m_ref, w_ref, m_next_ref, g_next_ref, o_ref = rest
    step = pl.program_id(1)
    nq = A_GROUP * BLOCK
    chunk_of = lax.broadcasted_iota(jnp.int32, (1, nq), 1) // BLOCK
    sinks = []
    for hh in range(2):
        sk = jnp.full((1, nq), sink_ref[hh] * LOG2_E, F32)
        for c in range(1, A_GROUP):
            sk = jnp.where(chunk_of == c, sink_ref[2 * c + hh] * LOG2_E, sk)
        sinks.append(sk)
    if local:
        kj = lax.broadcasted_iota(jnp.int32, (BLOCK, nq), 0)
        qi = lax.broadcasted_iota(jnp.int32, (BLOCK, nq), 1) % BLOCK

    rows_out = []
    for blk in range(n_blk):
        qts = jnp.concatenate([qt_ref[0, c * LANES:(c + 1) * LANES, blk * BLOCK:(blk + 1) * BLOCK]
                               for c in range(A_GROUP)], axis=1)
        if local:
            kwin = jnp.concatenate(k_blocks[blk:blk + 3] + [kx_ref[0]], axis=0)
            vtw = jnp.concatenate(v_blocks[blk:blk + 3] + [vx_ref[0]], axis=1)
            ok_prev = (kj >= qi) & (step > 0) if blk == 0 else (kj >= qi)
            ok_next = (kj <= qi) & (step < n_steps - 1) if blk == n_blk - 1 else (kj <= qi)
        else:
            kwin, vtw, ok_prev, ok_next = kx_ref[0], vx_ref[0], None, None
        ot = _attn_block(kwin, vtw, qts, sinks, ok_prev, ok_next)
        rows = slice(blk * BLOCK, (blk + 1) * BLOCK)
        outs = []
        for c in range(A_GROUP):
            sg = sg_ref[0, rows, c * LANES:(c + 1) * LANES].astype(F32)
            outs.append((ot[:, c * BLOCK:(c + 1) * BLOCK].T * sg).astype(BF16))
        rows_out.append(jnp.concatenate(outs, axis=1))
    y = _dot(jnp.concatenate(rows_out, axis=0), w_ref[...])
    gate = m_ref[0][:, 2 * D:3 * D]
    x_new = x_ref[0] + gate * y
    if o_ref is not None:
        o_ref[0] = x_new
    hb_ref[0] = _norm_mod(x_new, g_next_ref[...], m_next_ref[0]).astype(BF16)


def _attention(sink, qt, k, vt, kx, vxt, sg, x, mod3, w_out, mod3_next, norm_g_next):
    b, s, _ = sg.shape
    rows = A_BLOCKS_PER_STEP * BLOCK
    assert s % rows == 0
    ns = s // rows
    n_ctx = kx.shape[1]
    blk = lambda w: pl.BlockSpec((1, rows, w), lambda bi, i: (bi, i, 0))
    blk_t = lambda w: pl.BlockSpec((1, w, rows), lambda bi, i: (bi, 0, i))
    lo = lambda i: jnp.maximum(i - 1, 0)
    hi = lambda i: jnp.minimum(i + 1, ns - 1)
    in_specs = [pl.BlockSpec(memory_space=pltpu.SMEM), blk_t(A_WIDTH),
                pl.BlockSpec((1, rows, A_KV_WIDTH), lambda bi, i: (bi, lo(i), 0)),
                blk(A_KV_WIDTH),
                pl.BlockSpec((1, rows, A_KV_WIDTH), lambda bi, i: (bi, hi(i), 0)),
                pl.BlockSpec((1, A_KV_WIDTH, rows), lambda bi, i: (bi, 0, lo(i))),
                blk_t(A_KV_WIDTH),
                pl.BlockSpec((1, A_KV_WIDTH, rows), lambda bi, i: (bi, 0, hi(i))),
                pl.BlockSpec((1, n_ctx, A_KV_WIDTH), lambda bi, i: (bi, 0, 0)),
                pl.BlockSpec((1, A_KV_WIDTH, n_ctx), lambda bi, i: (bi, 0, 0)),
                blk(A_WIDTH), blk(D),
                pl.BlockSpec((1, 1, 3 * D), lambda bi, i: (bi, 0, 0)),
                pl.BlockSpec((A_WIDTH, D), lambda bi, i: (0, 0)),
                pl.BlockSpec((1, 1, 3 * D), lambda bi, i: (bi, 0, 0)),
                pl.BlockSpec((1, D), lambda bi, i: (0, 0))]
    return pl.pallas_call(
        functools.partial(_attn_kernel, True, ns),
        grid=(b, ns),
        in_specs=in_specs,
        out_specs=(blk(D), blk(D)),
        out_shape=(jax.ShapeDtypeStruct((b, s, D), F32), jax.ShapeDtypeStruct((b, s, D), BF16)),
        compiler_params=_params("parallel", "parallel"),
        name="attn_local",
    )(sink, qt, k, k, k, vt, vt, vt, kx, vxt, sg, x, mod3, w_out, mod3_next, norm_g_next.reshape(1, D))


def _ctx_attn_kernel(sink_ref, x_ref, m_ref, g_ref, win_ref, wout_ref, m_next_ref, g_next_ref,
                     k_ref, vt_ref, hb_ref, win_bf_ref, wout_bf_ref, qt_scr, sg_scr):
    _pair_cast_columns(win_ref, win_bf_ref)
    _pair_cast_rows(wout_ref, wout_bf_ref)
    _inproj_attn_kernel(False, x_ref, m_ref, g_ref, win_bf_ref, qt_scr, k_ref, vt_ref, sg_scr)
    _attn_kernel(False, 1, sink_ref, qt_scr, k_ref, vt_ref, sg_scr, x_ref, m_ref, wout_bf_ref,
                 m_next_ref, g_next_ref, None, hb_ref)


def _ctx_attention(sink, xc, mod3, mod_row, norm_g, w_in, w_out, mod3_next, norm_g_next):
    b, n, _ = xc.shape
    const = lambda shape, **kw: pl.BlockSpec(shape, lambda bi, i: (0,) * len(shape), **kw)
    return pl.pallas_call(
        _ctx_attn_kernel,
        grid=(b, 1),
        in_specs=[pl.BlockSpec(memory_space=pltpu.SMEM),
                  pl.BlockSpec((1, n, D), lambda bi, i: (bi, 0, 0)),
                  pl.BlockSpec((1, 1, 3 * D), lambda bi, i: (mod_row, 0, 0)),
                  const((1, D)),
                  const(w_in.shape, pipeline_mode=pl.Buffered(1)),
                  const(w_out.shape, pipeline_mode=pl.Buffered(1)),
                  pl.BlockSpec((1, 1, 3 * D), lambda bi, i: (mod_row, 0, 0)),
                  const((1, D))],
        out_specs=(pl.BlockSpec((1, n, A_KV_WIDTH), lambda bi, i: (bi, 0, 0)),
                   pl.BlockSpec((1, A_KV_WIDTH, n), lambda bi, i: (bi, 0, 0)),
                   pl.BlockSpec((1, n, D), lambda bi, i: (bi, 0, 0)),
                   const(w_in.shape), const(w_out.shape)),
        out_shape=(jax.ShapeDtypeStruct((b, n, A_KV_WIDTH), BF16),
                   jax.ShapeDtypeStruct((b, A_KV_WIDTH, n), BF16),
                   jax.ShapeDtypeStruct((b, n, D), BF16),
                   jax.ShapeDtypeStruct(w_in.shape, BF16), jax.ShapeDtypeStruct(w_out.shape, BF16)),
        scratch_shapes=[pltpu.VMEM((1, A_WIDTH, n), BF16), pltpu.VMEM((1, n, A_WIDTH), BF16)],
        compiler_params=_params("arbitrary", "arbitrary"),
        name="ctx_attn",
    )(sink, xc, mod3, norm_g.reshape(1, D), w_in, w_out, mod3_next, norm_g_next.reshape(1, D))


def _chunk_cumsum(x, tri):
    return _dot(tri, x.astype(BF16))


def _inproj_gla_kernel(chunk, with_q, hb_ref, w_in_ref, wa1_ref, wa2_ref, ba_ref, tri_ref, *outs):
    if with_q:
        w_ref = w_in_ref
        per_dir = (outs[0:3], outs[3:6])
        v_ref, sg_ref = outs[6:]
    else:
        w_ref = outs[2]
        _cast_once(w_in_ref, w_ref)
        per_dir = ((outs[0],), (outs[1],))
    tm = hb_ref.shape[1]
    hb = hb_ref[0]
    r = _dot(hb, wa1_ref[...])
    z = _dot(r.astype(BF16), wa2_ref[...]) + ba_ref[...]
    e = jnp.exp2(jnp.abs(z) * -LOG2_E)
    la = (jnp.minimum(z, 0.0) * LOG2_E - jnp.log2(1.0 + e)) * (1.0 / GATE_TEMP)
    k = _dot(hb, w_ref[:, B_K_WIDTH:2 * B_K_WIDTH])
    if with_q:
        q = _dot(hb, w_ref[:, 0:B_K_WIDTH]) * K_SCALE
    n_chunks = tm // chunk
    k_ends = []
    for reverse in (False, True):
        lad = la[:, B_K_WIDTH:] if reverse else la[:, 0:B_K_WIDTH]
        refs = per_dir[1] if reverse else per_dir[0]
        tots, kis, k_end = [], [], None
        tri = tri_ref[1 if reverse else 0]
        tb = tri.shape[0]
        cum_all = jnp.concatenate([_chunk_cumsum(lad[r * tb:(r + 1) * tb], tri) for r in range(tm // tb)], axis=0)
        for c in range(n_chunks):
            rows = slice(c * chunk, (c + 1) * chunk)
            cum = cum_all[rows]
            tot = cum[0:1] if reverse else cum[chunk - 1:chunk]
            if not with_q:
                k_end = (k[rows] * jnp.exp2(tot - cum)).astype(BF16)
                continue
            refs[0][0, rows] = (q[rows] * jnp.exp2(cum)).astype(BF16)
            kis.append(k[rows] * jnp.exp2(-cum))
            tots.append(tot)
            if c % 2 == 1:
                pair = slice((c - 1) * chunk, (c + 1) * chunk)
                refs[1][0, :, pair] = jnp.concatenate(kis[-2:], axis=0).T.astype(BF16)
        if with_q:
            pad = [jnp.zeros((DEC_ROWS - n_chunks, B_K_WIDTH), F32)] if n_chunks < DEC_ROWS else []
            refs[2][0, 0] = jnp.exp2(jnp.concatenate(tots + pad, axis=0))
        else:
            k_ends.append(k_end)
    if with_q:
        sg_ref[0] = _silu(_dot(hb, w_ref[:, 2 * B_K_WIDTH + B_V_WIDTH:])).astype(BF16)
    v = _dot(hb, w_ref[:, 2 * B_K_WIDTH:2 * B_K_WIDTH + B_V_WIDTH]).astype(BF16)
    if with_q:
        v_ref[0] = v
    else:
        for refs, k_end in zip(per_dir, k_ends):
            for h in range(B_HEADS):
                ks = slice(h * B_KEY_DIM, (h + 1) * B_KEY_DIM)
                refs[0][0, h] = _dot_tn(k_end[:, ks], v[:, h * B_VAL_DIM:(h + 1) * B_VAL_DIM])


def _inproj_gla(hb, w, wa1, wa2, ba, chunk, with_q):
    b, s, _ = hb.shape
    tm = min(ROW_TILE, s)
    nt = s // tm
    cpt = tm // chunk
    tb = max(chunk, MXU_DIM)
    assert tm % tb == 0 and tb % chunk == 0
    const = lambda shape, **kw: pl.BlockSpec(shape, lambda bi, i: (0,) * len(shape), **kw)
    in_specs = [
        pl.BlockSpec((1, tm, D), lambda bi, i: (bi, i, 0)),
        const(w.shape, **({} if with_q else {"pipeline_mode": pl.Buffered(1)})),
        const(wa1.shape), const(wa2.shape), const(ba.shape),
        const((2, tb, tb)),
    ]
    t_idx = np.arange(tb)
    same = (t_idx[:, None] // chunk) == (t_idx[None, :] // chunk)
    lower = same & (t_idx[None, :] <= t_idx[:, None])
    tri = jnp.asarray(np.stack([lower, lower.T]), dtype=BF16)
    rows = lambda w_: (jax.ShapeDtypeStruct((b, s, w_), BF16),
                       pl.BlockSpec((1, tm, w_), lambda bi, i: (bi, i, 0)))
    cols = (jax.ShapeDtypeStruct((b, B_K_WIDTH, s), BF16),
            pl.BlockSpec((1, B_K_WIDTH, tm), lambda bi, i: (bi, 0, i)))
    decs = (jax.ShapeDtypeStruct((b, nt, DEC_ROWS, B_K_WIDTH), F32),
            pl.BlockSpec((1, 1, DEC_ROWS, B_K_WIDTH), lambda bi, i: (bi, i, 0, 0)))
    if with_q:
        assert cpt <= DEC_ROWS and cpt % 2 == 0
        outs = [rows(B_K_WIDTH), cols, decs] * 2 + [rows(B_V_WIDTH)] * 2
    else:
        assert nt == 1 and cpt == 1
        state = (jax.ShapeDtypeStruct((b, B_HEADS, B_KEY_DIM, B_VAL_DIM), F32),
                 pl.BlockSpec((1, B_HEADS, B_KEY_DIM, B_VAL_DIM), lambda bi, i: (bi, 0, 0, 0)))
        outs = [state] * 2 + [(jax.ShapeDtypeStruct(w.shape, BF16), const(w.shape))]
    return pl.pallas_call(
        functools.partial(_inproj_gla_kernel, chunk, with_q),
        grid=(b, nt),
        in_specs=in_specs,
        out_specs=tuple(o[1] for o in outs),
        out_shape=tuple(o[0] for o in outs),
        compiler_params=_params("parallel", "parallel") if with_q else _params("arbitrary", "arbitrary"),
        name="inproj_gla" if with_q else "inproj_gla_ctx",
    )(hb, w, wa1, wa2, ba, tri)


def _scan_tile(reverse, qd_ref, kit_ref, dec_ref, v_ref, st_ref):
    pair_rows = 2 * CHUNK
    ti = lax.broadcasted_iota(jnp.int32, (pair_rows, pair_rows), 0)
    si = lax.broadcasted_iota(jnp.int32, (pair_rows, pair_rows), 1)
    same = (ti // CHUNK) == (si // CHUNK)
    if reverse:
        use_inv = same & (ti <= si)
        use_end = (ti < CHUNK) & (si >= CHUNK)
        second_row = lax.broadcasted_iota(jnp.int32, (pair_rows, B_KEY_DIM), 0) < CHUNK
        first_col = lax.broadcasted_iota(jnp.int32, (B_KEY_DIM, pair_rows), 1) >= CHUNK
    else:
        use_inv = same & (ti >= si)
        use_end = (ti >= CHUNK) & (si < CHUNK)
        second_row = lax.broadcasted_iota(jnp.int32, (pair_rows, B_KEY_DIM), 0) >= CHUNK
        first_col = lax.broadcasted_iota(jnp.int32, (B_KEY_DIM, pair_rows), 1) < CHUNK
    n_pairs = SCAN_TILE // pair_rows
    order = range(n_pairs - 1, -1, -1) if reverse else range(n_pairs)
    n_dec = dec_ref.shape[1] * DEC_ROWS
    dec = dec_ref[0].reshape(n_dec, B_K_WIDTH)
    dect = jnp.concatenate([dec, jnp.zeros((LANES - n_dec, B_K_WIDTH), F32)], axis=0).T
    heads = []
    for h in range(B_HEADS):
        ks = slice(h * B_KEY_DIM, (h + 1) * B_KEY_DIM)
        vs = slice(h * B_VAL_DIM, (h + 1) * B_VAL_DIM)
        st = st_ref[h]
        o_rows = [None] * n_pairs
        for p in order:
            rows = slice(p * pair_rows, (p + 1) * pair_rows)
            c_first, c_second = (2 * p + 1, 2 * p) if reverse else (2 * p, 2 * p + 1)
            qd = qd_ref[0, rows, ks]
            kit = kit_ref[0, ks, rows]
            vh = v_ref[0, rows, vs]
            d_first, d_second = dect[ks, c_first:c_first + 1], dect[ks, c_second:c_second + 1]
            ki32 = kit.astype(F32)
            s2 = _dot(qd, jnp.concatenate([kit, (ki32 * d_first).astype(BF16)], axis=1))
            a = jnp.where(use_inv, s2[:, 0:pair_rows], jnp.where(use_end, s2[:, pair_rows:], 0.0)).astype(BF16)
            q_pair = jnp.where(second_row, qd.astype(F32) * dec[c_first:c_first + 1, ks], qd.astype(F32))
            k_pair = ki32 * jnp.where(first_col, d_first * d_second, d_second)
            o_rows[p] = _dot(a, vh) + _dot(q_pair.astype(BF16), st.astype(BF16))
            st = st * (d_first * d_second) + _dot(k_pair.astype(BF16), vh)
        st_ref[h] = st
        heads.append(jnp.concatenate(o_rows, axis=0))
    return jnp.concatenate(heads, axis=1)


def _gla_scan_kernel(nt, qdf_ref, kitf_ref, decf_ref, qdb_ref, kitb_ref, decb_ref,
                     v_ref, sf_ref, sb_ref, sg_ref, x_ref, m_ref, hn_ref, w32_ref, fn_ref,
                     o_ref, st_ref, of_ref, w_ref):
    _cast_once(w32_ref, w_ref)
    j = pl.program_id(1)

    @pl.when(j == 0)
    def _():
        st_ref[...] = sf_ref[0]

    @pl.when(j == nt)
    def _():
        st_ref[...] = sb_ref[0]

    @pl.when(j < nt)
    def _():
        of_ref[j] = _scan_tile(False, qdf_ref, kitf_ref, decf_ref, v_ref, st_ref)

    @pl.when(j >= nt)
    def _():
        o = _scan_tile(True, qdb_ref, kitb_ref, decb_ref, v_ref, st_ref) + of_ref[2 * nt - 1 - j]
        normed = []
        for h in range(B_HEADS):
            oh = o[:, h * B_VAL_DIM:(h + 1) * B_VAL_DIM]
            normed.append(oh * lax.rsqrt(jnp.mean(oh * oh, axis=-1, keepdims=True) + EPS))
        of = jnp.concatenate(normed, axis=1) * hn_ref[...]
        y = _dot((of * sg_ref[0].astype(F32)).astype(BF16), w_ref[...])
        xn = x_ref[0] + m_ref[0][:, 2 * D:3 * D] * y
        ms = jnp.mean(xn * xn, axis=-1, keepdims=True)
        o_ref[0] = xn * lax.rsqrt(ms + EPS) * fn_ref[...]


def _gla_scan(fwd, bwd, v, s_f, s_b, sg, x, mod3, head_g, w_out, final_g):
    b, s, _ = v.shape
    assert SCAN_TILE % ROW_TILE == 0 and ROW_TILE // CHUNK == DEC_ROWS
    dec_blocks = SCAN_TILE // ROW_TILE
    nt = s // SCAN_TILE
    t_fwd = lambda j: jnp.minimum(j, nt - 1)
    t_bwd = lambda j: 2 * nt - 1 - jnp.maximum(j, nt)
    t_both = lambda j: jnp.where(j < nt, j, 2 * nt - 1 - j)

    def direction(t):
        rows = pl.BlockSpec((1, SCAN_TILE, B_K_WIDTH), lambda bi, j: (bi, t(j), 0))
        cols = pl.BlockSpec((1, B_K_WIDTH, SCAN_TILE), lambda bi, j: (bi, 0, t(j)))
        decs = pl.BlockSpec((1, dec_blocks, DEC_ROWS, B_K_WIDTH), lambda bi, j: (bi, t(j), 0, 0))
        return [rows, cols, decs]

    const = lambda shape: pl.BlockSpec(shape, lambda bi, j: (0,) * len(shape))
    state = pl.BlockSpec((1, B_HEADS, B_KEY_DIM, B_VAL_DIM), lambda bi, j: (bi, 0, 0, 0))
    out_rows = lambda w_: pl.BlockSpec((1, SCAN_TILE, w_), lambda bi, j: (bi, t_bwd(j), 0))
    in_specs = direction(t_fwd) + direction(t_bwd) + [
        pl.BlockSpec((1, SCAN_TILE, B_V_WIDTH), lambda bi, j: (bi, t_both(j), 0)),
        state, state, out_rows(B_V_WIDTH), out_rows(D),
        pl.BlockSpec((1, 1, 3 * D), lambda bi, j: (bi, 0, 0)),
        const((1, B_V_WIDTH)),
        pl.BlockSpec((B_V_WIDTH, D), lambda bi, j: (0, 0), pipeline_mode=pl.Buffered(1)),
        const((1, D))]
    return pl.pallas_call(
        functools.partial(_gla_scan_kernel, nt),
        grid=(b, 2 * nt),
        in_specs=in_specs,
        out_specs=out_rows(D),
        out_shape=jax.ShapeDtypeStruct((b, s, D), F32),
        scratch_shapes=[pltpu.VMEM((B_HEADS, B_KEY_DIM, B_VAL_DIM), F32),
                        pltpu.VMEM((nt, SCAN_TILE, B_V_WIDTH), F32),
                        pltpu.VMEM((B_V_WIDTH, D), BF16)],
        compiler_params=_params("arbitrary", "arbitrary", vmem_limit=SCAN_VMEM_LIMIT),
        name="gla_scan",
    )(*fwd, *bwd, v, s_f, s_b, sg, x, mod3, head_g.reshape(1, B_V_WIDTH), w_out, final_g.reshape(1, D))


def _rope_tables(n_tokens):
    rows_n = n_tokens // GRID_W
    row = np.repeat(np.arange(rows_n, dtype=np.float32), GRID_W)
    col = np.tile(np.arange(GRID_W, dtype=np.float32), rows_n)
    inv_freq = (np.float32(ROPE_BASE) ** (-np.arange(ROPE_FREQS, dtype=np.float32) / np.float32(ROPE_FREQS)))
    inv_freq = inv_freq.astype(np.float32)
    ang = np.stack([row[:, None] * inv_freq, col[:, None] * inv_freq], axis=1)
    cos, sin = np.cos(ang).astype(np.float32), np.sin(ang).astype(np.float32)
    zero = np.zeros_like(sin)
    tile = lambda t: np.tile(t.reshape(n_tokens, A_HEAD_DIM), (1, LANES // A_HEAD_DIM))
    return (tile(np.stack([cos, cos], axis=2)),
            tile(np.stack([-sin, zero], axis=2)),
            tile(np.stack([zero, sin], axis=2)))


def _pair_heads(t, axis):
    shape = t.shape
    t = t.reshape(shape[:axis] + (A_KV_HEADS, A_GROUP, -1) + shape[axis + 1:])
    return jnp.swapaxes(t, axis, axis + 1).reshape(shape)


def kernel(x, c, ctx, c_ctx, l0_norm_g, l0_w_ada, l0_b_ada, l0_w_in, l0_sink, l0_w_out, l1_norm_g, l1_w_ada, l1_b_ada, l1_w_in, l1_wa1_f, l1_wa2_f, l1_ba_f, l1_wa1_b, l1_wa2_b, l1_ba_b, l1_head_norm_g, l1_w_out, final_norm_g):
    b, s, _ = x.shape
    ctx_row = b

    cvec = jnp.concatenate([c, c_ctx[None, :], jnp.zeros((MOD_ROWS - b - 1, D), F32)], axis=0)
    mod0 = _modulation(cvec, l0_w_ada, l0_b_ada).reshape(MOD_ROWS, 1, 3 * D)
    mod1 = _modulation(cvec, l1_w_ada, l1_b_ada).reshape(MOD_ROWS, 1, 3 * D)

    sink = _pair_heads(l0_sink.astype(F32), 0)
    kc, vct, hbc, w0_in, w0_out = _ctx_attention(sink, ctx, mod0, ctx_row, l0_norm_g, l0_w_in, l0_w_out,
                                                 mod1, l1_norm_g)
    qt, k, vt, sg = _inproj_attn(x, mod0, l0_norm_g, w0_in, _rope_tables(s))
    x1, hb1 = _attention(sink, qt, k, vt, kc, vct, sg, x, mod0, w0_out, mod1, l1_norm_g)

    wa1 = jnp.concatenate([l1_wa1_f, l1_wa1_b, jnp.zeros((D, LANES - 2 * GATE_RANK), F32)], axis=1).astype(BF16)
    wa2 = jnp.zeros((LANES, 2 * B_K_WIDTH), F32)
    wa2 = wa2.at[0:GATE_RANK, 0:B_K_WIDTH].set(l1_wa2_f)
    wa2 = wa2.at[GATE_RANK:2 * GATE_RANK, B_K_WIDTH:].set(l1_wa2_b).astype(BF16)
    ba = jnp.concatenate([l1_ba_f, l1_ba_b]).reshape(1, 2 * B_K_WIDTH)
    s_f, s_b, w1 = _inproj_gla(hbc, l1_w_in, wa1, wa2, ba, ctx.shape[1], False)
    outs = _inproj_gla(hb1, w1, wa1, wa2, ba, CHUNK, True)
    v1, sg1 = outs[6:]
    return _gla_scan(outs[0:3], outs[3:6], v1, s_f, s_b, sg1, x1, mod1,
                     l1_head_norm_g, l1_w_out, final_norm_g)
```

```python
import functools

import jax
import jax.numpy as jnp
import numpy as np
from jax import lax
from jax.experimental import pallas as pl
from jax.experimental.pallas import tpu as pltpu

F32 = jnp.float32
BF16 = jnp.bfloat16

D = 1024
GRID_W = 64
EPS = 1e-6
NEG_INF = -1e30

A_HEADS = 16
A_KV_HEADS = 2
A_GROUP = A_HEADS // A_KV_HEADS
A_HEAD_DIM = 64
A_WIDTH = A_HEADS * A_HEAD_DIM
A_KV_WIDTH = A_KV_HEADS * A_HEAD_DIM
BLOCK = 128
ROPE_BASE = 10000.0
ROPE_FREQS = A_HEAD_DIM // 4
Q_SCALE = A_HEAD_DIM ** -0.5
LOG2_E = 1.4426950408889634

B_HEADS = 4
B_K_WIDTH = D // 2
B_V_WIDTH = D
B_KEY_DIM = B_K_WIDTH // B_HEADS
B_VAL_DIM = B_V_WIDTH // B_HEADS
GATE_RANK = 16
GATE_TEMP = 16.0
CHUNK = 64
K_SCALE = B_KEY_DIM ** -0.5

LANES = 128
MXU_DIM = 256
MOD_ROWS = 16
ROW_TILE = 1024
A_ROW_TILE = 1024
SCAN_TILE = 1024
SUB_ROWS = 256
DEC_ROWS = ROW_TILE // CHUNK
SUM_ROWS = 16
A_BLOCKS_PER_STEP = 8
VMEM_LIMIT = 48 * 1024 * 1024
SCAN_VMEM_LIMIT = 56 * 1024 * 1024


def _params(*sem, vmem_limit=VMEM_LIMIT):
    return pltpu.CompilerParams(dimension_semantics=sem, vmem_limit_bytes=vmem_limit)


def _silu(x):
    return x / (1.0 + jnp.exp(-x))


def _dot(a, b):
    return jnp.dot(a, b, preferred_element_type=F32)


def _dot_nt(a, b):
    return lax.dot_general(a, b, (((1,), (1,)), ((), ())), preferred_element_type=F32)


def _dot_tn(a, b):
    return lax.dot_general(a, b, (((0,), (0,)), ((), ())), preferred_element_type=F32)


def _norm_mod(x, g, m):
    ms = jnp.mean(x * x, axis=-1, keepdims=True)
    y = x * lax.rsqrt(ms + EPS) * g
    return y * (1.0 + m[:, D:2 * D]) + m[:, 0:D]


def _cast_once(src_ref, dst_ref):
    @pl.when((pl.program_id(0) == 0) & (pl.program_id(1) == 0))
    def _():
        def body(i, carry):
            rows = pl.ds(pl.multiple_of(i * LANES, LANES), LANES)
            dst_ref[rows, :] = src_ref[rows, :].astype(dst_ref.dtype)
            return carry
        lax.fori_loop(0, src_ref.shape[0] // LANES, body, 0)


def _mod_kernel(n_tiles, c_ref, w0_ref, b0_ref, w1_ref, b1_ref, o0_ref, o1_ref):
    s = _silu(c_ref[...]).astype(BF16)
    j = pl.program_id(0)

    @pl.when(j < n_tiles)
    def _():
        o0_ref[...] = _dot(s, w0_ref[...].astype(BF16)) + b0_ref[...]

    @pl.when(j >= n_tiles)
    def _():
        o1_ref[...] = _dot(s, w1_ref[...].astype(BF16)) + b1_ref[...]


def _modulation(cvec, w_ada0, b_ada0, w_ada1, b_ada1):
    n = w_ada0.shape[1] // D
    first = lambda j: (0, jnp.minimum(j, n - 1))
    second = lambda j: (0, jnp.maximum(j - n, 0))
    out = jax.ShapeDtypeStruct((MOD_ROWS, n * D), F32)
    return pl.pallas_call(
        functools.partial(_mod_kernel, n),
        grid=(2 * n,),
        in_specs=[
            pl.BlockSpec((MOD_ROWS, D), lambda j: (0, 0)),
            pl.BlockSpec((D, D), first), pl.BlockSpec((1, D), first),
            pl.BlockSpec((D, D), second), pl.BlockSpec((1, D), second),
        ],
        out_specs=(pl.BlockSpec((MOD_ROWS, D), first), pl.BlockSpec((MOD_ROWS, D), second)),
        out_shape=(out, out),
        compiler_params=_params("arbitrary"),
        name="modulation",
    )(cvec, w_ada0, b_ada0.reshape(1, n * D), w_ada1, b_ada1.reshape(1, n * D))


def _rope(t, cos, sin_hi, sin_lo):
    return t * cos + pltpu.roll(t, LANES - ROPE_FREQS, 1) * sin_hi + pltpu.roll(t, ROPE_FREQS, 1) * sin_lo


def _pair_cast_columns(src_ref, dst_ref):
    @pl.when((pl.program_id(0) == 0) & (pl.program_id(1) == 0))
    def _():
        kv_lo, kv_hi = A_WIDTH, A_WIDTH + 2 * A_KV_WIDTH

        def body(i, carry):
            rows = pl.ds(pl.multiple_of(i * LANES, LANES), LANES)
            src = src_ref[rows, :]
            dst_ref[rows, kv_lo:kv_hi] = src[:, kv_lo:kv_hi].astype(BF16)
            for base in (0, kv_hi):
                for p in range(A_GROUP):
                    a = base + p * A_HEAD_DIM
                    b = base + (p + A_GROUP) * A_HEAD_DIM
                    pair = jnp.concatenate([src[:, a:a + A_HEAD_DIM], src[:, b:b + A_HEAD_DIM]], axis=1)
                    dst_ref[rows, base + p * LANES:base + (p + 1) * LANES] = pair.astype(BF16)
            return carry
        lax.fori_loop(0, src_ref.shape[0] // LANES, body, 0)


def _inproj_attn_kernel(rope, x_ref, m_ref, g_ref, w_ref, *rest):
    wq_ref = w_ref.at[:, 0:A_WIDTH]
    wkv_ref = w_ref.at[:, A_WIDTH:A_WIDTH + 2 * A_KV_WIDTH]
    wg_ref = w_ref.at[:, A_WIDTH + 2 * A_KV_WIDTH:]
    if rope:
        cos_ref, shi_ref, slo_ref, qt_ref, k_ref, vt_ref, sg_ref = rest
    else:
        qt_ref, k_ref, vt_ref, sg_ref = rest
    assert x_ref.shape[1] % SUB_ROWS == 0
    for r in range(x_ref.shape[1] // SUB_ROWS):
        rows = slice(r * SUB_ROWS, (r + 1) * SUB_ROWS)
        if rope:
            cos, shi, slo = cos_ref[rows], shi_ref[rows], slo_ref[rows]
        hb = _norm_mod(x_ref[0, rows], g_ref[...], m_ref[0]).astype(BF16)
        q = _dot(hb, wq_ref[...])
        for j in range(A_WIDTH // LANES):
            qj = q[:, j * LANES:(j + 1) * LANES]
            if rope:
                qj = _rope(qj, cos, shi, slo)
            qt_ref[0, j * LANES:(j + 1) * LANES, rows] = (qj * (Q_SCALE * LOG2_E)).T.astype(BF16)
        kv = _dot(hb, wkv_ref[...])
        k = kv[:, 0:A_KV_WIDTH]
        if rope:
            k = _rope(k, cos, shi, slo)
        k_ref[0, rows] = k.astype(BF16)
        vt_ref[0, :, rows] = kv[:, A_KV_WIDTH:].T.astype(BF16)
        g = _dot(hb, wg_ref[...])
        sg_ref[0, rows] = _silu(g).astype(BF16)


def _inproj_attn(x, mod3, norm_g, w_in, tables):
    b, s, _ = x.shape
    tm = min(A_ROW_TILE, s)
    in_specs = [
        pl.BlockSpec((1, tm, D), lambda bi, i: (bi, i, 0)),
        pl.BlockSpec((1, 1, 3 * D), lambda bi, i: (bi, 0, 0)),
        pl.BlockSpec((1, D), lambda bi, i: (0, 0)),
        pl.BlockSpec(w_in.shape, lambda bi, i: (0, 0)),
    ] + [pl.BlockSpec((tm, LANES), lambda bi, i: (i, 0))] * 3
    args = [x, mod3, norm_g.reshape(1, D), w_in, *tables]
    out_shape = (
        jax.ShapeDtypeStruct((b, A_WIDTH, s), BF16),
        jax.ShapeDtypeStruct((b, s, A_KV_WIDTH), BF16),
        jax.ShapeDtypeStruct((b, A_KV_WIDTH, s), BF16),
        jax.ShapeDtypeStruct((b, s, A_WIDTH), BF16),
    )
    out_specs = (
        pl.BlockSpec((1, A_WIDTH, tm), lambda bi, i: (bi, 0, i)),
        pl.BlockSpec((1, tm, A_KV_WIDTH), lambda bi, i: (bi, i, 0)),
        pl.BlockSpec((1, A_KV_WIDTH, tm), lambda bi, i: (bi, 0, i)),
        pl.BlockSpec((1, tm, A_WIDTH), lambda bi, i: (bi, i, 0)),
    )
    return pl.pallas_call(
        functools.partial(_inproj_attn_kernel, True),
        grid=(b, s // tm),
        in_specs=in_specs,
        out_specs=out_specs,
        out_shape=out_shape,
        compiler_params=_params("parallel", "parallel"),
        name="inproj_attn",
    )(*args)


def _attn_block(kwin, vtw, qts, sinks, ok_prev, ok_next):
    n_keys, nq = kwin.shape[0], qts.shape[1]
    lane = lax.broadcasted_iota(jnp.int32, kwin.shape, 1)
    kbd = jnp.concatenate([jnp.where(lane < A_HEAD_DIM, kwin, jnp.zeros_like(kwin)),
                           jnp.where(lane >= A_HEAD_DIM, kwin, jnp.zeros_like(kwin))], axis=0)
    ones = jnp.where(lax.broadcasted_iota(jnp.int32, (SUM_ROWS, n_keys), 0) == 0, 1.0, 0.0).astype(BF16)

    st = _dot(kbd, qts).astype(BF16)
    neg = jnp.asarray(NEG_INF, BF16)
    outs = []
    for hh in range(2):
        sh = st[hh * n_keys:(hh + 1) * n_keys]
        if ok_prev is not None:
            parts = [jnp.where(ok_prev, sh[0:BLOCK], neg),
                     sh[BLOCK:2 * BLOCK],
                     jnp.where(ok_next, sh[2 * BLOCK:3 * BLOCK], neg),
                     sh[3 * BLOCK:]]
        else:
            parts = [sh]
        mx = functools.reduce(jnp.maximum, [jnp.max(t, axis=0, keepdims=True) for t in parts])
        mxb = jnp.maximum(mx.astype(F32), sinks[hh]).astype(BF16)
        mx = mxb.astype(F32)
        probs = jnp.concatenate([jnp.exp2(t - mxb) for t in parts], axis=0)
        vt_h = jnp.concatenate([vtw[hh * A_HEAD_DIM:(hh + 1) * A_HEAD_DIM], ones], axis=0)
        ot = _dot(vt_h, probs)
        den = ot[A_HEAD_DIM:A_HEAD_DIM + 1] + jnp.exp2(sinks[hh] - mx)
        outs.append(ot[0:A_HEAD_DIM] * (1.0 / den))
    return jnp.concatenate(outs, axis=0)


def _pair_cast_rows(src_ref, dst_ref):
    @pl.when((pl.program_id(0) == 0) & (pl.program_id(1) == 0))
    def _():
        for p in range(A_GROUP):
            for half, h in enumerate((p, p + A_GROUP)):
                lo = p * LANES + half * A_HEAD_DIM
                dst_ref[lo:lo + A_HEAD_DIM, :] = src_ref[h * A_HEAD_DIM:(h + 1) * A_HEAD_DIM, :].astype(BF16)


def _attn_kernel(local, n_steps, sink_ref, qt_ref, *rest):
    n_blk = rest[-1].shape[1] // BLOCK
    if local:
        (kp_ref, kc_ref, kn_ref, vp_ref, vc_ref, vn_ref, kx_ref, vx_ref,
         sg_ref, x_ref, m_ref, w_ref, o_ref) = rest
        last = slice((n_blk - 1) * BLOCK, n_blk * BLOCK)
        k_blocks = ([kp_ref[0, last]] + [kc_ref[0, i * BLOCK:(i + 1) * BLOCK] for i in range(n_blk)]
                    + [kn_ref[0, 0:BLOCK]])
        v_blocks = ([vp_ref[0, :, last]] + [vc_ref[0, :, i * BLOCK:(i + 1) * BLOCK] for i in range(n_blk)]
                    + [vn_ref[0, :, 0:BLOCK]])
    else:
        kx_ref, vx_ref, sg_ref, x_ref, m_ref, w_ref, o_ref = rest
    step = pl.program_id(1)
    nq = A_GROUP * BLOCK
    chunk_of = lax.broadcasted_iota(jnp.int32, (1, nq), 1) // BLOCK
    sinks = []
    for hh in range(2):
        sk = jnp.full((1, nq), sink_ref[hh] * LOG2_E, F32)
        for c in range(1, A_GROUP):
            sk = jnp.where(chunk_of == c, sink_ref[2 * c + hh] * LOG2_E, sk)
        sinks.append(sk)
    if local:
        kj = lax.broadcasted_iota(jnp.int32, (BLOCK, nq), 0)
        qi = lax.broadcasted_iota(jnp.int32, (BLOCK, nq), 1) % BLOCK

    rows_out = []
    for blk in range(n_blk):
        qts = jnp.concatenate([qt_ref[0, c * LANES:(c + 1) * LANES, blk * BLOCK:(blk + 1) * BLOCK]
                               for c in range(A_GROUP)], axis=1)
        if local:
            kwin = jnp.concatenate(k_blocks[blk:blk + 3] + [kx_ref[0]], axis=0)
            vtw = jnp.concatenate(v_blocks[blk:blk + 3] + [vx_ref[0]], axis=1)
            ok_prev = (kj >= qi) & (step > 0) if blk == 0 else (kj >= qi)
            ok_next = (kj <= qi) & (step < n_steps - 1) if blk == n_blk - 1 else (kj <= qi)
        else:
            kwin, vtw, ok_prev, ok_next = kx_ref[0], vx_ref[0], None, None
        ot = _attn_block(kwin, vtw, qts, sinks, ok_prev, ok_next)
        rows = slice(blk * BLOCK, (blk + 1) * BLOCK)
        outs = []
        for c in range(A_GROUP):
            sg = sg_ref[0, rows, c * LANES:(c + 1) * LANES].astype(F32)
            outs.append((ot[:, c * BLOCK:(c + 1) * BLOCK].T * sg).astype(BF16))
        rows_out.append(jnp.concatenate(outs, axis=1))
    y = _dot(jnp.concatenate(rows_out, axis=0), w_ref[...])
    gate = m_ref[0][:, 2 * D:3 * D]
    o_ref[0] = x_ref[0] + gate * y


def _attention(sink, qt, k, vt, kx, vxt, sg, x, mod3, w_out):
    b, s, _ = sg.shape
    rows = A_BLOCKS_PER_STEP * BLOCK
    assert s % rows == 0
    ns = s // rows
    n_ctx = kx.shape[1]
    blk = lambda w: pl.BlockSpec((1, rows, w), lambda bi, i: (bi, i, 0))
    blk_t = lambda w: pl.BlockSpec((1, w, rows), lambda bi, i: (bi, 0, i))
    lo = lambda i: jnp.maximum(i - 1, 0)
    hi = lambda i: jnp.minimum(i + 1, ns - 1)
    in_specs = [pl.BlockSpec(memory_space=pltpu.SMEM), blk_t(A_WIDTH),
                pl.BlockSpec((1, rows, A_KV_WIDTH), lambda bi, i: (bi, lo(i), 0)),
                blk(A_KV_WIDTH),
                pl.BlockSpec((1, rows, A_KV_WIDTH), lambda bi, i: (bi, hi(i), 0)),
                pl.BlockSpec((1, A_KV_WIDTH, rows), lambda bi, i: (bi, 0, lo(i))),
                blk_t(A_KV_WIDTH),
                pl.BlockSpec((1, A_KV_WIDTH, rows), lambda bi, i: (bi, 0, hi(i))),
                pl.BlockSpec((1, n_ctx, A_KV_WIDTH), lambda bi, i: (bi, 0, 0)),
                pl.BlockSpec((1, A_KV_WIDTH, n_ctx), lambda bi, i: (bi, 0, 0)),
                blk(A_WIDTH), blk(D),
                pl.BlockSpec((1, 1, 3 * D), lambda bi, i: (bi, 0, 0)),
                pl.BlockSpec((A_WIDTH, D), lambda bi, i: (0, 0))]
    return pl.pallas_call(
        functools.partial(_attn_kernel, True, ns),
        grid=(b, ns),
        in_specs=in_specs,
        out_specs=blk(D),
        out_shape=jax.ShapeDtypeStruct((b, s, D), F32),
        compiler_params=_params("parallel", "parallel"),
        name="attn_local",
    )(sink, qt, k, k, k, vt, vt, vt, kx, vxt, sg, x, mod3, w_out)


def _ctx_attn_kernel(sink_ref, x_ref, m_ref, g_ref, win_ref, wout_ref, k_ref, vt_ref, o_ref,
                     win_bf_ref, wout_bf_ref, qt_scr, sg_scr):
    _pair_cast_columns(win_ref, win_bf_ref)
    _pair_cast_rows(wout_ref, wout_bf_ref)
    _inproj_attn_kernel(False, x_ref, m_ref, g_ref, win_bf_ref, qt_scr, k_ref, vt_ref, sg_scr)
    _attn_kernel(False, 1, sink_ref, qt_scr, k_ref, vt_ref, sg_scr, x_ref, m_ref, wout_bf_ref, o_ref)


def _ctx_attention(sink, xc, mod3, mod_row, norm_g, w_in, w_out):
    b, n, _ = xc.shape
    const = lambda shape, **kw: pl.BlockSpec(shape, lambda bi, i: (0,) * len(shape), **kw)
    return pl.pallas_call(
        _ctx_attn_kernel,
        grid=(b, 1),
        in_specs=[pl.BlockSpec(memory_space=pltpu.SMEM),
                  pl.BlockSpec((1, n, D), lambda bi, i: (bi, 0, 0)),
                  pl.BlockSpec((1, 1, 3 * D), lambda bi, i: (mod_row, 0, 0)),
                  const((1, D)),
                  const(w_in.shape, pipeline_mode=pl.Buffered(1)),
                  const(w_out.shape, pipeline_mode=pl.Buffered(1))],
        out_specs=(pl.BlockSpec((1, n, A_KV_WIDTH), lambda bi, i: (bi, 0, 0)),
                   pl.BlockSpec((1, A_KV_WIDTH, n), lambda bi, i: (bi, 0, 0)),
                   pl.BlockSpec((1, n, D), lambda bi, i: (bi, 0, 0)),
                   const(w_in.shape), const(w_out.shape)),
        out_shape=(jax.ShapeDtypeStruct((b, n, A_KV_WIDTH), BF16),
                   jax.ShapeDtypeStruct((b, A_KV_WIDTH, n), BF16),
                   jax.ShapeDtypeStruct((b, n, D), F32),
                   jax.ShapeDtypeStruct(w_in.shape, BF16), jax.ShapeDtypeStruct(w_out.shape, BF16)),
        scratch_shapes=[pltpu.VMEM((1, A_WIDTH, n), BF16), pltpu.VMEM((1, n, A_WIDTH), BF16)],
        compiler_params=_params("arbitrary", "arbitrary"),
        name="ctx_attn",
    )(sink, xc, mod3, norm_g.reshape(1, D), w_in, w_out)


def _chunk_cumsum(x, tri):
    return _dot(tri, x.astype(BF16))


def _inproj_gla_kernel(chunk, with_q, x_ref, m_ref, g_ref, w_in_ref, wa1_ref, wa2_ref, ba_ref, tri_ref, *outs):
    if with_q:
        w_ref = w_in_ref
        per_dir = (outs[0:3], outs[3:6])
        v_ref, sg_ref = outs[6:]
    else:
        w_ref = outs[2]
        _cast_once(w_in_ref, w_ref)
        per_dir = ((outs[0],), (outs[1],))
    tm = x_ref.shape[1]
    hb = _norm_mod(x_ref[0], g_ref[...], m_ref[0]).astype(BF16)
    r = _dot(hb, wa1_ref[...])
    z = _dot(r.astype(BF16), wa2_ref[...]) + ba_ref[...]
    e = jnp.exp2(jnp.abs(z) * -LOG2_E)
    la = (jnp.minimum(z, 0.0) * LOG2_E - jnp.log2(1.0 + e)) * (1.0 / GATE_TEMP)
    k = _dot(hb, w_ref[:, B_K_WIDTH:2 * B_K_WIDTH])
    if with_q:
        q = _dot(hb, w_ref[:, 0:B_K_WIDTH]) * K_SCALE
    n_chunks = tm // chunk
    k_ends = []
    for reverse in (False, True):
        lad = la[:, B_K_WIDTH:] if reverse else la[:, 0:B_K_WIDTH]
        refs = per_dir[1] if reverse else per_dir[0]
        tots, kis, k_end = [], [], None
        tri = tri_ref[1 if reverse else 0]
        tb = tri.shape[0]
        cum_all = jnp.concatenate([_chunk_cumsum(lad[r * tb:(r + 1) * tb], tri) for r in range(tm // tb)], axis=0)
        for c in range(n_chunks):
            rows = slice(c * chunk, (c + 1) * chunk)
            cum = cum_all[rows]
            tot = cum[0:1] if reverse else cum[chunk - 1:chunk]
            if not with_q:
                k_end = (k[rows] * jnp.exp2(tot - cum)).astype(BF16)
                continue
            refs[0][0, rows] = (q[rows] * jnp.exp2(cum)).astype(BF16)
            kis.append(k[rows] * jnp.exp2(-cum))
            tots.append(tot)
            if c % 2 == 1:
                pair = slice((c - 1) * chunk, (c + 1) * chunk)
                refs[1][0, :, pair] = jnp.concatenate(kis[-2:], axis=0).T.astype(BF16)
        if with_q:
            pad = [jnp.zeros((DEC_ROWS - n_chunks, B_K_WIDTH), F32)] if n_chunks < DEC_ROWS else []
            refs[2][0, 0] = jnp.exp2(jnp.concatenate(tots + pad, axis=0))
        else:
            k_ends.append(k_end)
    if with_q:
        sg_ref[0] = _silu(_dot(hb, w_ref[:, 2 * B_K_WIDTH + B_V_WIDTH:])).astype(BF16)
    v = _dot(hb, w_ref[:, 2 * B_K_WIDTH:2 * B_K_WIDTH + B_V_WIDTH]).astype(BF16)
    if with_q:
        v_ref[0] = v
    else:
        for refs, k_end in zip(per_dir, k_ends):
            for h in range(B_HEADS):
                ks = slice(h * B_KEY_DIM, (h + 1) * B_KEY_DIM)
                refs[0][0, h] = _dot_tn(k_end[:, ks], v[:, h * B_VAL_DIM:(h + 1) * B_VAL_DIM])


def _inproj_gla(x, mod3, mod_row, norm_g, w, wa1, wa2, ba, chunk, with_q):
    b, s, _ = x.shape
    tm = min(ROW_TILE, s)
    nt = s // tm
    cpt = tm // chunk
    tb = max(chunk, MXU_DIM)
    assert tm % tb == 0 and tb % chunk == 0
    if mod_row is None:
        mod_map = lambda bi, i: (bi, 0, 0)
    else:
        mod_map = lambda bi, i: (mod_row, 0, 0)
    const = lambda shape, **kw: pl.BlockSpec(shape, lambda bi, i: (0,) * len(shape), **kw)
    in_specs = [
        pl.BlockSpec((1, tm, D), lambda bi, i: (bi, i, 0)),
        pl.BlockSpec((1, 1, 3 * D), mod_map),
        const((1, D)), const(w.shape, **({} if with_q else {"pipeline_mode": pl.Buffered(1)})),
        const(wa1.shape), const(wa2.shape), const(ba.shape),
        const((2, tb, tb)),
    ]
    t_idx = np.arange(tb)
    same = (t_idx[:, None] // chunk) == (t_idx[None, :] // chunk)
    lower = same & (t_idx[None, :] <= t_idx[:, None])
    tri = jnp.asarray(np.stack([lower, lower.T]), dtype=BF16)
    rows = lambda w_: (jax.ShapeDtypeStruct((b, s, w_), BF16),
                       pl.BlockSpec((1, tm, w_), lambda bi, i: (bi, i, 0)))
    cols = (jax.ShapeDtypeStruct((b, B_K_WIDTH, s), BF16),
            pl.BlockSpec((1, B_K_WIDTH, tm), lambda bi, i: (bi, 0, i)))
    decs = (jax.ShapeDtypeStruct((b, nt, DEC_ROWS, B_K_WIDTH), F32),
            pl.BlockSpec((1, 1, DEC_ROWS, B_K_WIDTH), lambda bi, i: (bi, i, 0, 0)))
    if with_q:
        assert cpt <= DEC_ROWS and cpt % 2 == 0
        outs = [rows(B_K_WIDTH), cols, decs] * 2 + [rows(B_V_WIDTH)] * 2
    else:
        assert nt == 1 and cpt == 1
        state = (jax.ShapeDtypeStruct((b, B_HEADS, B_KEY_DIM, B_VAL_DIM), F32),
                 pl.BlockSpec((1, B_HEADS, B_KEY_DIM, B_VAL_DIM), lambda bi, i: (bi, 0, 0, 0)))
        outs = [state] * 2 + [(jax.ShapeDtypeStruct(w.shape, BF16), const(w.shape))]
    return pl.pallas_call(
        functools.partial(_inproj_gla_kernel, chunk, with_q),
        grid=(b, nt),
        in_specs=in_specs,
        out_specs=tuple(o[1] for o in outs),
        out_shape=tuple(o[0] for o in outs),
        compiler_params=_params("parallel", "parallel") if with_q else _params("arbitrary", "arbitrary"),
        name="inproj_gla" if with_q else "inproj_gla_ctx",
    )(x, mod3, norm_g.reshape(1, D), w, wa1, wa2, ba, tri)


def _scan_tile(reverse, qd_ref, kit_ref, dec_ref, v_ref, st_ref):
    pair_rows = 2 * CHUNK
    ti = lax.broadcasted_iota(jnp.int32, (pair_rows, pair_rows), 0)
    si = lax.broadcasted_iota(jnp.int32, (pair_rows, pair_rows), 1)
    same = (ti // CHUNK) == (si // CHUNK)
    if reverse:
        use_inv = same & (ti <= si)
        use_end = (ti < CHUNK) & (si >= CHUNK)
        second_row = lax.broadcasted_iota(jnp.int32, (pair_rows, B_KEY_DIM), 0) < CHUNK
        first_col = lax.broadcasted_iota(jnp.int32, (B_KEY_DIM, pair_rows), 1) >= CHUNK
    else:
        use_inv = same & (ti >= si)
        use_end = (ti >= CHUNK) & (si < CHUNK)
        second_row = lax.broadcasted_iota(jnp.int32, (pair_rows, B_KEY_DIM), 0) >= CHUNK
        first_col = lax.broadcasted_iota(jnp.int32, (B_KEY_DIM, pair_rows), 1) < CHUNK
    n_pairs = SCAN_TILE // pair_rows
    order = range(n_pairs - 1, -1, -1) if reverse else range(n_pairs)
    n_dec = dec_ref.shape[1] * DEC_ROWS
    dec = dec_ref[0].reshape(n_dec, B_K_WIDTH)
    dect = jnp.concatenate([dec, jnp.zeros((LANES - n_dec, B_K_WIDTH), F32)], axis=0).T
    heads = []
    for h in range(B_HEADS):
        ks = slice(h * B_KEY_DIM, (h + 1) * B_KEY_DIM)
        vs = slice(h * B_VAL_DIM, (h + 1) * B_VAL_DIM)
        st = st_ref[h]
        o_rows = [None] * n_pairs
        for p in order:
            rows = slice(p * pair_rows, (p + 1) * pair_rows)
            c_first, c_second = (2 * p + 1, 2 * p) if reverse else (2 * p, 2 * p + 1)
            qd = qd_ref[0, rows, ks]
            kit = kit_ref[0, ks, rows]
            vh = v_ref[0, rows, vs]
            d_first, d_second = dect[ks, c_first:c_first + 1], dect[ks, c_second:c_second + 1]
            ki32 = kit.astype(F32)
            s2 = _dot(qd, jnp.concatenate([kit, (ki32 * d_first).astype(BF16)], axis=1))
            a = jnp.where(use_inv, s2[:, 0:pair_rows], jnp.where(use_end, s2[:, pair_rows:], 0.0)).astype(BF16)
            q_pair = jnp.where(second_row, qd.astype(F32) * dec[c_first:c_first + 1, ks], qd.astype(F32))
            k_pair = ki32 * jnp.where(first_col, d_first * d_second, d_second)
            o_rows[p] = _dot(a, vh) + _dot(q_pair.astype(BF16), st.astype(BF16))
            st = st * (d_first * d_second) + _dot(k_pair.astype(BF16), vh)
        st_ref[h] = st
        heads.append(jnp.concatenate(o_rows, axis=0))
    return jnp.concatenate(heads, axis=1)


def _gla_scan_kernel(nt, qdf_ref, kitf_ref, decf_ref, qdb_ref, kitb_ref, decb_ref,
                     v_ref, sf_ref, sb_ref, sg_ref, x_ref, m_ref, hn_ref, w32_ref, fn_ref,
                     o_ref, st_ref, of_ref, w_ref):
    _cast_once(w32_ref, w_ref)
    j = pl.program_id(1)

    @pl.when(j == 0)
    def _():
        st_ref[...] = sf_ref[0]

    @pl.when(j == nt)
    def _():
        st_ref[...] = sb_ref[0]

    @pl.when(j < nt)
    def _():
        of_ref[j] = _scan_tile(False, qdf_ref, kitf_ref, decf_ref, v_ref, st_ref)

    @pl.when(j >= nt)
    def _():
        o = _scan_tile(True, qdb_ref, kitb_ref, decb_ref, v_ref, st_ref) + of_ref[2 * nt - 1 - j]
        normed = []
        for h in range(B_HEADS):
            oh = o[:, h * B_VAL_DIM:(h + 1) * B_VAL_DIM]
            normed.append(oh * lax.rsqrt(jnp.mean(oh * oh, axis=-1, keepdims=True) + EPS))
        of = jnp.concatenate(normed, axis=1) * hn_ref[...]
        y = _dot((of * sg_ref[0].astype(F32)).astype(BF16), w_ref[...])
        xn = x_ref[0] + m_ref[0][:, 2 * D:3 * D] * y
        ms = jnp.mean(xn * xn, axis=-1, keepdims=True)
        o_ref[0] = xn * lax.rsqrt(ms + EPS) * fn_ref[...]


def _gla_scan(fwd, bwd, v, s_f, s_b, sg, x, mod3, head_g, w_out, final_g):
    b, s, _ = v.shape
    assert SCAN_TILE % ROW_TILE == 0 and ROW_TILE // CHUNK == DEC_ROWS
    dec_blocks = SCAN_TILE // ROW_TILE
    nt = s // SCAN_TILE
    t_fwd = lambda j: jnp.minimum(j, nt - 1)
    t_bwd = lambda j: 2 * nt - 1 - jnp.maximum(j, nt)
    t_both = lambda j: jnp.where(j < nt, j, 2 * nt - 1 - j)

    def direction(t):
        rows = pl.BlockSpec((1, SCAN_TILE, B_K_WIDTH), lambda bi, j: (bi, t(j), 0))
        cols = pl.BlockSpec((1, B_K_WIDTH, SCAN_TILE), lambda bi, j: (bi, 0, t(j)))
        decs = pl.BlockSpec((1, dec_blocks, DEC_ROWS, B_K_WIDTH), lambda bi, j: (bi, t(j), 0, 0))
        return [rows, cols, decs]

    const = lambda shape: pl.BlockSpec(shape, lambda bi, j: (0,) * len(shape))
    state = pl.BlockSpec((1, B_HEADS, B_KEY_DIM, B_VAL_DIM), lambda bi, j: (bi, 0, 0, 0))
    out_rows = lambda w_: pl.BlockSpec((1, SCAN_TILE, w_), lambda bi, j: (bi, t_bwd(j), 0))
    in_specs = direction(t_fwd) + direction(t_bwd) + [
        pl.BlockSpec((1, SCAN_TILE, B_V_WIDTH), lambda bi, j: (bi, t_both(j), 0)),
        state, state, out_rows(B_V_WIDTH), out_rows(D),
        pl.BlockSpec((1, 1, 3 * D), lambda bi, j: (bi, 0, 0)),
        const((1, B_V_WIDTH)),
        pl.BlockSpec((B_V_WIDTH, D), lambda bi, j: (0, 0), pipeline_mode=pl.Buffered(1)),
        const((1, D))]
    return pl.pallas_call(
        functools.partial(_gla_scan_kernel, nt),
        grid=(b, 2 * nt),
        in_specs=in_specs,
        out_specs=out_rows(D),
        out_shape=jax.ShapeDtypeStruct((b, s, D), F32),
        scratch_shapes=[pltpu.VMEM((B_HEADS, B_KEY_DIM, B_VAL_DIM), F32),
                        pltpu.VMEM((nt, SCAN_TILE, B_V_WIDTH), F32),
                        pltpu.VMEM((B_V_WIDTH, D), BF16)],
        compiler_params=_params("arbitrary", "arbitrary", vmem_limit=SCAN_VMEM_LIMIT),
        name="gla_scan",
    )(*fwd, *bwd, v, s_f, s_b, sg, x, mod3, head_g.reshape(1, B_V_WIDTH), w_out, final_g.reshape(1, D))


def _rope_tables(n_tokens):
    rows_n = n_tokens // GRID_W
    row = np.repeat(np.arange(rows_n, dtype=np.float32), GRID_W)
    col = np.tile(np.arange(GRID_W, dtype=np.float32), rows_n)
    inv_freq = (np.float32(ROPE_BASE) ** (-np.arange(ROPE_FREQS, dtype=np.float32) / np.float32(ROPE_FREQS)))
    inv_freq = inv_freq.astype(np.float32)
    ang = np.stack([row[:, None] * inv_freq, col[:, None] * inv_freq], axis=1)
    cos, sin = np.cos(ang).astype(np.float32), np.sin(ang).astype(np.float32)
    zero = np.zeros_like(sin)
    tile = lambda t: np.tile(t.reshape(n_tokens, A_HEAD_DIM), (1, LANES // A_HEAD_DIM))
    return (tile(np.stack([cos, cos], axis=2)),
            tile(np.stack([-sin, zero], axis=2)),
            tile(np.stack([zero, sin], axis=2)))


def _pair_heads(t, axis):
    shape = t.shape
    t = t.reshape(shape[:axis] + (A_KV_HEADS, A_GROUP, -1) + shape[axis + 1:])
    return jnp.swapaxes(t, axis, axis + 1).reshape(shape)


def kernel(x, c, ctx, c_ctx, l0_norm_g, l0_w_ada, l0_b_ada, l0_w_in, l0_sink, l0_w_out, l1_norm_g, l1_w_ada, l1_b_ada, l1_w_in, l1_wa1_f, l1_wa2_f, l1_ba_f, l1_wa1_b, l1_wa2_b, l1_ba_b, l1_head_norm_g, l1_w_out, final_norm_g):
    b, s, _ = x.shape
    ctx_row = b

    cvec = jnp.concatenate([c, c_ctx[None, :], jnp.zeros((MOD_ROWS - b - 1, D), F32)], axis=0)
    mod0, mod1 = (m.reshape(MOD_ROWS, 1, 3 * D) for m in _modulation(cvec, l0_w_ada, l0_b_ada, l1_w_ada, l1_b_ada))

    sink = _pair_heads(l0_sink.astype(F32), 0)
    kc, vct, xc1, w0_in, w0_out = _ctx_attention(sink, ctx, mod0, ctx_row, l0_norm_g, l0_w_in, l0_w_out)
    qt, k, vt, sg = _inproj_attn(x, mod0, l0_norm_g, w0_in, _rope_tables(s))
    x1 = _attention(sink, qt, k, vt, kc, vct, sg, x, mod0, w0_out)

    wa1 = jnp.concatenate([l1_wa1_f, l1_wa1_b, jnp.zeros((D, LANES - 2 * GATE_RANK), F32)], axis=1).astype(BF16)
    wa2 = jnp.zeros((LANES, 2 * B_K_WIDTH), F32)
    wa2 = wa2.at[0:GATE_RANK, 0:B_K_WIDTH].set(l1_wa2_f)
    wa2 = wa2.at[GATE_RANK:2 * GATE_RANK, B_K_WIDTH:].set(l1_wa2_b).astype(BF16)
    ba = jnp.concatenate([l1_ba_f, l1_ba_b]).reshape(1, 2 * B_K_WIDTH)
    s_f, s_b, w1 = _inproj_gla(xc1, mod1, ctx_row, l1_norm_g, l1_w_in, wa1, wa2, ba, ctx.shape[1], False)
    outs = _inproj_gla(x1, mod1, None, l1_norm_g, w1, wa1, wa2, ba, CHUNK, True)
    v1, sg1 = outs[6:]
    return _gla_scan(outs[0:3], outs[3:6], v1, s_f, s_b, sg1, x1, mod1,
                     l1_head_norm_g, l1_w_out, final_norm_g)
```

```python
import functools

import jax
import jax.numpy as jnp
import numpy as np
from jax import lax
from jax.experimental import pallas as pl
from jax.experimental.pallas import tpu as pltpu

F32 = jnp.float32
BF16 = jnp.bfloat16

D = 1024
GRID_W = 64
EPS = 1e-6
NEG_INF = -1e30

A_HEADS = 16
A_KV_HEADS = 2
A_GROUP = A_HEADS // A_KV_HEADS
A_HEAD_DIM = 64
A_WIDTH = A_HEADS * A_HEAD_DIM
A_KV_WIDTH = A_KV_HEADS * A_HEAD_DIM
BLOCK = 128
ROPE_BASE = 10000.0
ROPE_FREQS = A_HEAD_DIM // 4
Q_SCALE = A_HEAD_DIM ** -0.5
LOG2_E = 1.4426950408889634

B_HEADS = 4
B_K_WIDTH = D // 2
B_V_WIDTH = D
B_KEY_DIM = B_K_WIDTH // B_HEADS
B_VAL_DIM = B_V_WIDTH // B_HEADS
GATE_RANK = 16
GATE_TEMP = 16.0
CHUNK = 64
K_SCALE = B_KEY_DIM ** -0.5

LANES = 128
MXU_DIM = 256
MOD_ROWS = 16
ROW_TILE = 1024
A_ROW_TILE = 1024
SCAN_TILE = 1024
SUB_ROWS = 256
DEC_ROWS = ROW_TILE // CHUNK
SUM_ROWS = 16
A_BLOCKS_PER_STEP = 8
VMEM_LIMIT = 48 * 1024 * 1024
SCAN_VMEM_LIMIT = 56 * 1024 * 1024


def _params(*sem, vmem_limit=VMEM_LIMIT):
    return pltpu.CompilerParams(dimension_semantics=sem, vmem_limit_bytes=vmem_limit)


def _silu(x):
    return x / (1.0 + jnp.exp(-x))


def _dot(a, b):
    return jnp.dot(a, b, preferred_element_type=F32)


def _dot_nt(a, b):
    return lax.dot_general(a, b, (((1,), (1,)), ((), ())), preferred_element_type=F32)


def _dot_tn(a, b):
    return lax.dot_general(a, b, (((0,), (0,)), ((), ())), preferred_element_type=F32)


def _norm_mod(x, g, m):
    ms = jnp.mean(x * x, axis=-1, keepdims=True)
    y = x * lax.rsqrt(ms + EPS) * g
    return y * (1.0 + m[:, D:2 * D]) + m[:, 0:D]


def _cast_once(src_ref, dst_ref):
    @pl.when((pl.program_id(0) == 0) & (pl.program_id(1) == 0))
    def _():
        def body(i, carry):
            rows = pl.ds(pl.multiple_of(i * LANES, LANES), LANES)
            dst_ref[rows, :] = src_ref[rows, :].astype(dst_ref.dtype)
            return carry
        lax.fori_loop(0, src_ref.shape[0] // LANES, body, 0)


def _mod_kernel(n_tiles, c_ref, w0_ref, b0_ref, w1_ref, b1_ref, o0_ref, o1_ref):
    s = _silu(c_ref[...]).astype(BF16)
    j = pl.program_id(0)

    @pl.when(j < n_tiles)
    def _():
        o0_ref[...] = _dot(s, w0_ref[...].astype(BF16)) + b0_ref[...]

    @pl.when(j >= n_tiles)
    def _():
        o1_ref[...] = _dot(s, w1_ref[...].astype(BF16)) + b1_ref[...]


def _modulation(cvec, w_ada0, b_ada0, w_ada1, b_ada1):
    n = w_ada0.shape[1] // D
    first = lambda j: (0, jnp.minimum(j, n - 1))
    second = lambda j: (0, jnp.maximum(j - n, 0))
    out = jax.ShapeDtypeStruct((MOD_ROWS, n * D), F32)
    return pl.pallas_call(
        functools.partial(_mod_kernel, n),
        grid=(2 * n,),
        in_specs=[
            pl.BlockSpec((MOD_ROWS, D), lambda j: (0, 0)),
            pl.BlockSpec((D, D), first), pl.BlockSpec((1, D), first),
            pl.BlockSpec((D, D), second), pl.BlockSpec((1, D), second),
        ],
        out_specs=(pl.BlockSpec((MOD_ROWS, D), first), pl.BlockSpec((MOD_ROWS, D), second)),
        out_shape=(out, out),
        compiler_params=_params("arbitrary"),
        name="modulation",
    )(cvec, w_ada0, b_ada0.reshape(1, n * D), w_ada1, b_ada1.reshape(1, n * D))


def _rope(t, cos, sin_hi, sin_lo):
    return t * cos + pltpu.roll(t, LANES - ROPE_FREQS, 1) * sin_hi + pltpu.roll(t, ROPE_FREQS, 1) * sin_lo


def _pair_cast_columns(src_ref, dst_ref):
    @pl.when((pl.program_id(0) == 0) & (pl.program_id(1) == 0))
    def _():
        kv_lo, kv_hi = A_WIDTH, A_WIDTH + 2 * A_KV_WIDTH

        def body(i, carry):
            rows = pl.ds(pl.multiple_of(i * LANES, LANES), LANES)
            src = src_ref[rows, :]
            dst_ref[rows, kv_lo:kv_hi] = src[:, kv_lo:kv_hi].astype(BF16)
            for base in (0, kv_hi):
                for p in range(A_GROUP):
                    a = base + p * A_HEAD_DIM
                    b = base + (p + A_GROUP) * A_HEAD_DIM
                    pair = jnp.concatenate([src[:, a:a + A_HEAD_DIM], src[:, b:b + A_HEAD_DIM]], axis=1)
                    dst_ref[rows, base + p * LANES:base + (p + 1) * LANES] = pair.astype(BF16)
            return carry
        lax.fori_loop(0, src_ref.shape[0] // LANES, body, 0)


def _inproj_attn_kernel(rope, x_ref, m_ref, g_ref, w_ref, *rest):
    wq_ref = w_ref.at[:, 0:A_WIDTH]
    wkv_ref = w_ref.at[:, A_WIDTH:A_WIDTH + 2 * A_KV_WIDTH]
    wg_ref = w_ref.at[:, A_WIDTH + 2 * A_KV_WIDTH:]
    if rope:
        cos_ref, shi_ref, slo_ref, qt_ref, k_ref, vt_ref, sg_ref = rest
    else:
        qt_ref, k_ref, vt_ref, sg_ref = rest
    assert x_ref.shape[1] % SUB_ROWS == 0
    for r in range(x_ref.shape[1] // SUB_ROWS):
        rows = slice(r * SUB_ROWS, (r + 1) * SUB_ROWS)
        if rope:
            cos, shi, slo = cos_ref[rows], shi_ref[rows], slo_ref[rows]
        hb = _norm_mod(x_ref[0, rows], g_ref[...], m_ref[0]).astype(BF16)
        q = _dot(hb, wq_ref[...])
        for j in range(A_WIDTH // LANES):
            qj = q[:, j * LANES:(j + 1) * LANES]
            if rope:
                qj = _rope(qj, cos, shi, slo)
            qt_ref[0, j * LANES:(j + 1) * LANES, rows] = (qj * (Q_SCALE * LOG2_E)).T.astype(BF16)
        kv = _dot(hb, wkv_ref[...])
        k = kv[:, 0:A_KV_WIDTH]
        if rope:
            k = _rope(k, cos, shi, slo)
        k_ref[0, rows] = k.astype(BF16)
        vt_ref[0, :, rows] = kv[:, A_KV_WIDTH:].T.astype(BF16)
        g = _dot(hb, wg_ref[...])
        sg_ref[0, rows] = _silu(g).astype(BF16)


def _inproj_attn(x, mod3, norm_g, w_in, tables):
    b, s, _ = x.shape
    tm = min(A_ROW_TILE, s)
    in_specs = [
        pl.BlockSpec((1, tm, D), lambda bi, i: (bi, i, 0)),
        pl.BlockSpec((1, 1, 3 * D), lambda bi, i: (bi, 0, 0)),
        pl.BlockSpec((1, D), lambda bi, i: (0, 0)),
        pl.BlockSpec(w_in.shape, lambda bi, i: (0, 0)),
    ] + [pl.BlockSpec((tm, LANES), lambda bi, i: (i, 0))] * 3
    args = [x, mod3, norm_g.reshape(1, D), w_in, *tables]
    out_shape = (
        jax.ShapeDtypeStruct((b, A_WIDTH, s), BF16),
        jax.ShapeDtypeStruct((b, s, A_KV_WIDTH), BF16),
        jax.ShapeDtypeStruct((b, A_KV_WIDTH, s), BF16),
        jax.ShapeDtypeStruct((b, s, A_WIDTH), BF16),
    )
    out_specs = (
        pl.BlockSpec((1, A_WIDTH, tm), lambda bi, i: (bi, 0, i)),
        pl.BlockSpec((1, tm, A_KV_WIDTH), lambda bi, i: (bi, i, 0)),
        pl.BlockSpec((1, A_KV_WIDTH, tm), lambda bi, i: (bi, 0, i)),
        pl.BlockSpec((1, tm, A_WIDTH), lambda bi, i: (bi, i, 0)),
    )
    return pl.pallas_call(
        functools.partial(_inproj_attn_kernel, True),
        grid=(b, s // tm),
        in_specs=in_specs,
        out_specs=out_specs,
        out_shape=out_shape,
        compiler_params=_params("parallel", "parallel"),
        name="inproj_attn",
    )(*args)


def _attn_block(kwin, vtw, qts, sinks, ok_prev, ok_next):
    n_keys, nq = kwin.shape[0], qts.shape[1]
    lane = lax.broadcasted_iota(jnp.int32, kwin.shape, 1)
    kbd = jnp.concatenate([jnp.where(lane < A_HEAD_DIM, kwin, jnp.zeros_like(kwin)),
                           jnp.where(lane >= A_HEAD_DIM, kwin, jnp.zeros_like(kwin))], axis=0)
    ones = jnp.where(lax.broadcasted_iota(jnp.int32, (SUM_ROWS, n_keys), 0) == 0, 1.0, 0.0).astype(BF16)

    st = _dot(kbd, qts).astype(BF16)
    neg = jnp.asarray(NEG_INF, BF16)
    outs = []
    for hh in range(2):
        sh = st[hh * n_keys:(hh + 1) * n_keys]
        if ok_prev is not None:
            parts = [jnp.where(ok_prev, sh[0:BLOCK], neg),
                     sh[BLOCK:2 * BLOCK],
                     jnp.where(ok_next, sh[2 * BLOCK:3 * BLOCK], neg),
                     sh[3 * BLOCK:]]
        else:
            parts = [sh]
        mx = functools.reduce(jnp.maximum, [jnp.max(t, axis=0, keepdims=True) for t in parts])
        mxb = jnp.maximum(mx.astype(F32), sinks[hh]).astype(BF16)
        mx = mxb.astype(F32)
        probs = jnp.concatenate([jnp.exp2(t - mxb) for t in parts], axis=0)
        vt_h = jnp.concatenate([vtw[hh * A_HEAD_DIM:(hh + 1) * A_HEAD_DIM], ones], axis=0)
        ot = _dot(vt_h, probs)
        den = ot[A_HEAD_DIM:A_HEAD_DIM + 1] + jnp.exp2(sinks[hh] - mx)
        outs.append(ot[0:A_HEAD_DIM] * (1.0 / den))
    return jnp.concatenate(outs, axis=0)


def _pair_cast_rows(src_ref, dst_ref):
    @pl.when((pl.program_id(0) == 0) & (pl.program_id(1) == 0))
    def _():
        for p in range(A_GROUP):
            for half, h in enumerate((p, p + A_GROUP)):
                lo = p * LANES + half * A_HEAD_DIM
                dst_ref[lo:lo + A_HEAD_DIM, :] = src_ref[h * A_HEAD_DIM:(h + 1) * A_HEAD_DIM, :].astype(BF16)


def _attn_kernel(local, n_steps, sink_ref, qt_ref, *rest):
    n_blk = rest[-1].shape[1] // BLOCK
    if local:
        (kp_ref, kc_ref, kn_ref, vp_ref, vc_ref, vn_ref, kx_ref, vx_ref,
         sg_ref, x_ref, m_ref, w_ref, o_ref) = rest
        last = slice((n_blk - 1) * BLOCK, n_blk * BLOCK)
        k_blocks = ([kp_ref[0, last]] + [kc_ref[0, i * BLOCK:(i + 1) * BLOCK] for i in range(n_blk)]
                    + [kn_ref[0, 0:BLOCK]])
        v_blocks = ([vp_ref[0, :, last]] + [vc_ref[0, :, i * BLOCK:(i + 1) * BLOCK] for i in range(n_blk)]
                    + [vn_ref[0, :, 0:BLOCK]])
    else:
        kx_ref, vx_ref, sg_ref, x_ref, m_ref, w_ref, o_ref = rest
    step = pl.program_id(1)
    nq = A_GROUP * BLOCK
    chunk_of = lax.broadcasted_iota(jnp.int32, (1, nq), 1) // BLOCK
    sinks = []
    for hh in range(2):
        sk = jnp.full((1, nq), sink_ref[hh] * LOG2_E, F32)
        for c in range(1, A_GROUP):
            sk = jnp.where(chunk_of == c, sink_ref[2 * c + hh] * LOG2_E, sk)
        sinks.append(sk)
    if local:
        kj = lax.broadcasted_iota(jnp.int32, (BLOCK, nq), 0)
        qi = lax.broadcasted_iota(jnp.int32, (BLOCK, nq), 1) % BLOCK

    rows_out = []
    for blk in range(n_blk):
        qts = jnp.concatenate([qt_ref[0, c * LANES:(c + 1) * LANES, blk * BLOCK:(blk + 1) * BLOCK]
                               for c in range(A_GROUP)], axis=1)
        if local:
            kwin = jnp.concatenate(k_blocks[blk:blk + 3] + [kx_ref[0]], axis=0)
            vtw = jnp.concatenate(v_blocks[blk:blk + 3] + [vx_ref[0]], axis=1)
            ok_prev = (kj >= qi) & (step > 0) if blk == 0 else (kj >= qi)
            ok_next = (kj <= qi) & (step < n_steps - 1) if blk == n_blk - 1 else (kj <= qi)
        else:
            kwin, vtw, ok_prev, ok_next = kx_ref[0], vx_ref[0], None, None
        ot = _attn_block(kwin, vtw, qts, sinks, ok_prev, ok_next)
        rows = slice(blk * BLOCK, (blk + 1) * BLOCK)
        outs = []
        for c in range(A_GROUP):
            sg = sg_ref[0, rows, c * LANES:(c + 1) * LANES].astype(F32)
            outs.append((ot[:, c * BLOCK:(c + 1) * BLOCK].T * sg).astype(BF16))
        rows_out.append(jnp.concatenate(outs, axis=1))
    y = _dot(jnp.concatenate(rows_out, axis=0), w_ref[...])
    gate = m_ref[0][:, 2 * D:3 * D]
    o_ref[0] = x_ref[0] + gate * y


def _attention(sink, qt, k, vt, kx, vxt, sg, x, mod3, w_out):
    b, s, _ = sg.shape
    rows = A_BLOCKS_PER_STEP * BLOCK
    assert s % rows == 0
    ns = s // rows
    n_ctx = kx.shape[1]
    blk = lambda w: pl.BlockSpec((1, rows, w), lambda bi, i: (bi, i, 0))
    blk_t = lambda w: pl.BlockSpec((1, w, rows), lambda bi, i: (bi, 0, i))
    lo = lambda i: jnp.maximum(i - 1, 0)
    hi = lambda i: jnp.minimum(i + 1, ns - 1)
    in_specs = [pl.BlockSpec(memory_space=pltpu.SMEM), blk_t(A_WIDTH),
                pl.BlockSpec((1, rows, A_KV_WIDTH), lambda bi, i: (bi, lo(i), 0)),
                blk(A_KV_WIDTH),
                pl.BlockSpec((1, rows, A_KV_WIDTH), lambda bi, i: (bi, hi(i), 0)),
                pl.BlockSpec((1, A_KV_WIDTH, rows), lambda bi, i: (bi, 0, lo(i))),
                blk_t(A_KV_WIDTH),
                pl.BlockSpec((1, A_KV_WIDTH, rows), lambda bi, i: (bi, 0, hi(i))),
                pl.BlockSpec((1, n_ctx, A_KV_WIDTH), lambda bi, i: (bi, 0, 0)),
                pl.BlockSpec((1, A_KV_WIDTH, n_ctx), lambda bi, i: (bi, 0, 0)),
                blk(A_WIDTH), blk(D),
                pl.BlockSpec((1, 1, 3 * D), lambda bi, i: (bi, 0, 0)),
                pl.BlockSpec((A_WIDTH, D), lambda bi, i: (0, 0))]
    return pl.pallas_call(
        functools.partial(_attn_kernel, True, ns),
        grid=(b, ns),
        in_specs=in_specs,
        out_specs=blk(D),
        out_shape=jax.ShapeDtypeStruct((b, s, D), F32),
        compiler_params=_params("parallel", "parallel"),
        name="attn_local",
    )(sink, qt, k, k, k, vt, vt, vt, kx, vxt, sg, x, mod3, w_out)


def _ctx_attn_kernel(sink_ref, x_ref, m_ref, g_ref, win_ref, wout_ref, k_ref, vt_ref, o_ref,
                     win_bf_ref, wout_bf_ref, qt_scr, sg_scr):
    _pair_cast_columns(win_ref, win_bf_ref)
    _pair_cast_rows(wout_ref, wout_bf_ref)
    _inproj_attn_kernel(False, x_ref, m_ref, g_ref, win_bf_ref, qt_scr, k_ref, vt_ref, sg_scr)
    _attn_kernel(False, 1, sink_ref, qt_scr, k_ref, vt_ref, sg_scr, x_ref, m_ref, wout_bf_ref, o_ref)


def _ctx_attention(sink, xc, mod3, mod_row, norm_g, w_in, w_out):
    b, n, _ = xc.shape
    const = lambda shape, **kw: pl.BlockSpec(shape, lambda bi, i: (0,) * len(shape), **kw)
    return pl.pallas_call(
        _ctx_attn_kernel,
        grid=(b, 1),
        in_specs=[pl.BlockSpec(memory_space=pltpu.SMEM),
                  pl.BlockSpec((1, n, D), lambda bi, i: (bi, 0, 0)),
                  pl.BlockSpec((1, 1, 3 * D), lambda bi, i: (mod_row, 0, 0)),
                  const((1, D)),
                  const(w_in.shape, pipeline_mode=pl.Buffered(1)),
                  const(w_out.shape, pipeline_mode=pl.Buffered(1))],
        out_specs=(pl.BlockSpec((1, n, A_KV_WIDTH), lambda bi, i: (bi, 0, 0)),
                   pl.BlockSpec((1, A_KV_WIDTH, n), lambda bi, i: (bi, 0, 0)),
                   pl.BlockSpec((1, n, D), lambda bi, i: (bi, 0, 0)),
                   const(w_in.shape), const(w_out.shape)),
        out_shape=(jax.ShapeDtypeStruct((b, n, A_KV_WIDTH), BF16),
                   jax.ShapeDtypeStruct((b, A_KV_WIDTH, n), BF16),
                   jax.ShapeDtypeStruct((b, n, D), F32),
                   jax.ShapeDtypeStruct(w_in.shape, BF16), jax.ShapeDtypeStruct(w_out.shape, BF16)),
        scratch_shapes=[pltpu.VMEM((1, A_WIDTH, n), BF16), pltpu.VMEM((1, n, A_WIDTH), BF16)],
        compiler_params=_params("arbitrary", "arbitrary"),
        name="ctx_attn",
    )(sink, xc, mod3, norm_g.reshape(1, D), w_in, w_out)


def _chunk_cumsum(x, tri):
    return _dot(tri, x.astype(BF16))


def _inproj_gla_kernel(chunk, with_q, x_ref, m_ref, g_ref, w_in_ref, wa1_ref, wa2_ref, ba_ref, tri_ref, *outs):
    if with_q:
        w_ref = w_in_ref
        per_dir = (outs[0:3], outs[3:6])
        v_ref, sg_ref = outs[6:]
    else:
        w_ref = outs[2]
        _cast_once(w_in_ref, w_ref)
        per_dir = ((outs[0],), (outs[1],))
    tm = x_ref.shape[1]
    hb = _norm_mod(x_ref[0], g_ref[...], m_ref[0]).astype(BF16)
    r = _dot(hb, wa1_ref[...])
    z = _dot(r.astype(BF16), wa2_ref[...]) + ba_ref[...]
    e = jnp.exp2(jnp.abs(z) * -LOG2_E)
    la = (jnp.minimum(z, 0.0) * LOG2_E - jnp.log2(1.0 + e)) * (1.0 / GATE_TEMP)
    k = _dot(hb, w_ref[:, B_K_WIDTH:2 * B_K_WIDTH])
    if with_q:
        q = _dot(hb, w_ref[:, 0:B_K_WIDTH]) * K_SCALE
    n_chunks = tm // chunk
    k_ends = ([], [])
    for reverse in (False, True):
        lad = la[:, B_K_WIDTH:] if reverse else la[:, 0:B_K_WIDTH]
        refs = per_dir[1] if reverse else per_dir[0]
        tots, kis = [], []
        tri = tri_ref[1 if reverse else 0]
        tb = tri.shape[0]
        cum_all = jnp.concatenate([_chunk_cumsum(lad[r * tb:(r + 1) * tb], tri) for r in range(tm // tb)], axis=0)
        for c in range(n_chunks):
            rows = slice(c * chunk, (c + 1) * chunk)
            cum = cum_all[rows]
            tot = cum[0:1] if reverse else cum[chunk - 1:chunk]
            if not with_q:
                k_ends[reverse].append((k[rows] * jnp.exp2(tot - cum)).astype(BF16))
                continue
            refs[0][0, rows] = (q[rows] * jnp.exp2(cum)).astype(BF16)
            kis.append(k[rows] * jnp.exp2(-cum))
            tots.append(tot)
            if c % 2 == 1:
                pair = slice((c - 1) * chunk, (c + 1) * chunk)
                refs[1][0, :, pair] = jnp.concatenate(kis[-2:], axis=0).T.astype(BF16)
        if with_q:
            pad = [jnp.zeros((DEC_ROWS - n_chunks, B_K_WIDTH), F32)] if n_chunks < DEC_ROWS else []
            refs[2][0, 0] = jnp.exp2(jnp.concatenate(tots + pad, axis=0))
    if with_q:
        sg_ref[0] = _silu(_dot(hb, w_ref[:, 2 * B_K_WIDTH + B_V_WIDTH:])).astype(BF16)
    v = _dot(hb, w_ref[:, 2 * B_K_WIDTH:2 * B_K_WIDTH + B_V_WIDTH]).astype(BF16)
    if with_q:
        v_ref[0] = v
    else:
        for refs, per_chunk in zip(per_dir, k_ends):
            for c, k_end in enumerate(per_chunk):
                vc = v[c * chunk:(c + 1) * chunk]
                for h in range(B_HEADS):
                    ks = slice(h * B_KEY_DIM, (h + 1) * B_KEY_DIM)
                    refs[0][c, h] = _dot_tn(k_end[:, ks], vc[:, h * B_VAL_DIM:(h + 1) * B_VAL_DIM])


def _inproj_gla(x, mod3, mod_row, norm_g, w, wa1, wa2, ba, chunk, with_q):
    b, s, _ = x.shape
    tm = min(ROW_TILE, s)
    nt = s // tm
    cpt = tm // chunk
    tb = max(chunk, MXU_DIM)
    assert tm % tb == 0 and tb % chunk == 0
    if mod_row is None:
        mod_map = lambda bi, i: (bi, 0, 0)
    else:
        mod_map = lambda bi, i: (mod_row, 0, 0)
    const = lambda shape, **kw: pl.BlockSpec(shape, lambda bi, i: (0,) * len(shape), **kw)
    in_specs = [
        pl.BlockSpec((1, tm, D), lambda bi, i: (bi, i, 0)),
        pl.BlockSpec((1, 1, 3 * D), mod_map),
        const((1, D)), const(w.shape, **({} if with_q else {"pipeline_mode": pl.Buffered(1)})),
        const(wa1.shape), const(wa2.shape), const(ba.shape),
        const((2, tb, tb)),
    ]
    t_idx = np.arange(tb)
    same = (t_idx[:, None] // chunk) == (t_idx[None, :] // chunk)
    lower = same & (t_idx[None, :] <= t_idx[:, None])
    tri = jnp.asarray(np.stack([lower, lower.T]), dtype=BF16)
    rows = lambda w_: (jax.ShapeDtypeStruct((b, s, w_), BF16),
                       pl.BlockSpec((1, tm, w_), lambda bi, i: (bi, i, 0)))
    cols = (jax.ShapeDtypeStruct((b, B_K_WIDTH, s), BF16),
            pl.BlockSpec((1, B_K_WIDTH, tm), lambda bi, i: (bi, 0, i)))
    decs = (jax.ShapeDtypeStruct((b, nt, DEC_ROWS, B_K_WIDTH), F32),
            pl.BlockSpec((1, 1, DEC_ROWS, B_K_WIDTH), lambda bi, i: (bi, i, 0, 0)))
    if with_q:
        assert cpt <= DEC_ROWS and cpt % 2 == 0
        outs = [rows(B_K_WIDTH), cols, decs] * 2 + [rows(B_V_WIDTH)] * 2
    else:
        assert b == 1
        state = (jax.ShapeDtypeStruct((nt * cpt, B_HEADS, B_KEY_DIM, B_VAL_DIM), F32),
                 pl.BlockSpec((cpt, B_HEADS, B_KEY_DIM, B_VAL_DIM), lambda bi, i: (i, 0, 0, 0)))
        outs = [state] * 2 + [(jax.ShapeDtypeStruct(w.shape, BF16), const(w.shape))]
    return pl.pallas_call(
        functools.partial(_inproj_gla_kernel, chunk, with_q),
        grid=(b, nt),
        in_specs=in_specs,
        out_specs=tuple(o[1] for o in outs),
        out_shape=tuple(o[0] for o in outs),
        compiler_params=_params("parallel", "parallel") if with_q else _params("arbitrary", "arbitrary"),
        name="inproj_gla" if with_q else "inproj_gla_ctx",
    )(x, mod3, norm_g.reshape(1, D), w, wa1, wa2, ba, tri)


def _scan_tile(reverse, qd_ref, kit_ref, dec_ref, v_ref, st_ref):
    pair_rows = 2 * CHUNK
    ti = lax.broadcasted_iota(jnp.int32, (pair_rows, pair_rows), 0)
    si = lax.broadcasted_iota(jnp.int32, (pair_rows, pair_rows), 1)
    same = (ti // CHUNK) == (si // CHUNK)
    if reverse:
        use_inv = same & (ti <= si)
        use_end = (ti < CHUNK) & (si >= CHUNK)
        second_row = lax.broadcasted_iota(jnp.int32, (pair_rows, B_KEY_DIM), 0) < CHUNK
        first_col = lax.broadcasted_iota(jnp.int32, (B_KEY_DIM, pair_rows), 1) >= CHUNK
    else:
        use_inv = same & (ti >= si)
        use_end = (ti >= CHUNK) & (si < CHUNK)
        second_row = lax.broadcasted_iota(jnp.int32, (pair_rows, B_KEY_DIM), 0) >= CHUNK
        first_col = lax.broadcasted_iota(jnp.int32, (B_KEY_DIM, pair_rows), 1) < CHUNK
    n_pairs = SCAN_TILE // pair_rows
    order = range(n_pairs - 1, -1, -1) if reverse else range(n_pairs)
    n_dec = dec_ref.shape[1] * DEC_ROWS
    dec = dec_ref[0].reshape(n_dec, B_K_WIDTH)
    dect = jnp.concatenate([dec, jnp.zeros((LANES - n_dec, B_K_WIDTH), F32)], axis=0).T
    heads = []
    for h in range(B_HEADS):
        ks = slice(h * B_KEY_DIM, (h + 1) * B_KEY_DIM)
        vs = slice(h * B_VAL_DIM, (h + 1) * B_VAL_DIM)
        st = st_ref[h]
        o_rows = [None] * n_pairs
        for p in order:
            rows = slice(p * pair_rows, (p + 1) * pair_rows)
            c_first, c_second = (2 * p + 1, 2 * p) if reverse else (2 * p, 2 * p + 1)
            qd = qd_ref[0, rows, ks]
            kit = kit_ref[0, ks, rows]
            vh = v_ref[0, rows, vs]
            d_first, d_second = dect[ks, c_first:c_first + 1], dect[ks, c_second:c_second + 1]
            ki32 = kit.astype(F32)
            s2 = _dot(qd, jnp.concatenate([kit, (ki32 * d_first).astype(BF16)], axis=1))
            a = jnp.where(use_inv, s2[:, 0:pair_rows], jnp.where(use_end, s2[:, pair_rows:], 0.0)).astype(BF16)
            q_pair = jnp.where(second_row, qd.astype(F32) * dec[c_first:c_first + 1, ks], qd.astype(F32))
            k_pair = ki32 * jnp.where(first_col, d_first * d_second, d_second)
            o_rows[p] = _dot(a, vh) + _dot(q_pair.astype(BF16), st.astype(BF16))
            st = st * (d_first * d_second) + _dot(k_pair.astype(BF16), vh)
        st_ref[h] = st
        heads.append(jnp.concatenate(o_rows, axis=0))
    return jnp.concatenate(heads, axis=1)


def _gla_scan_kernel(nt, qdf_ref, kitf_ref, decf_ref, qdb_ref, kitb_ref, decb_ref,
                     v_ref, sf_ref, sb_ref, sg_ref, x_ref, m_ref, hn_ref, w32_ref, fn_ref,
                     o_ref, st_ref, of_ref, w_ref):
    _cast_once(w32_ref, w_ref)
    j = pl.program_id(1)

    @pl.when(j == 0)
    def _():
        st_ref[...] = sf_ref[0]

    @pl.when(j == nt)
    def _():
        st_ref[...] = sb_ref[0]

    @pl.when(j < nt)
    def _():
        of_ref[j] = _scan_tile(False, qdf_ref, kitf_ref, decf_ref, v_ref, st_ref)

    @pl.when(j >= nt)
    def _():
        o = _scan_tile(True, qdb_ref, kitb_ref, decb_ref, v_ref, st_ref) + of_ref[2 * nt - 1 - j]
        normed = []
        for h in range(B_HEADS):
            oh = o[:, h * B_VAL_DIM:(h + 1) * B_VAL_DIM]
            normed.append(oh * lax.rsqrt(jnp.mean(oh * oh, axis=-1, keepdims=True) + EPS))
        of = jnp.concatenate(normed, axis=1) * hn_ref[...]
        y = _dot((of * sg_ref[0].astype(F32)).astype(BF16), w_ref[...])
        xn = x_ref[0] + m_ref[0][:, 2 * D:3 * D] * y
        ms = jnp.mean(xn * xn, axis=-1, keepdims=True)
        o_ref[0] = xn * lax.rsqrt(ms + EPS) * fn_ref[...]


def _gla_scan(fwd, bwd, v, s_f, s_b, sg, x, mod3, head_g, w_out, final_g):
    b, s, _ = v.shape
    assert SCAN_TILE % ROW_TILE == 0 and ROW_TILE // CHUNK == DEC_ROWS
    dec_blocks = SCAN_TILE // ROW_TILE
    nt = s // SCAN_TILE
    t_fwd = lambda j: jnp.minimum(j, nt - 1)
    t_bwd = lambda j: 2 * nt - 1 - jnp.maximum(j, nt)
    t_both = lambda j: jnp.where(j < nt, j, 2 * nt - 1 - j)

    def direction(t):
        rows = pl.BlockSpec((1, SCAN_TILE, B_K_WIDTH), lambda bi, j: (bi, t(j), 0))
        cols = pl.BlockSpec((1, B_K_WIDTH, SCAN_TILE), lambda bi, j: (bi, 0, t(j)))
        decs = pl.BlockSpec((1, dec_blocks, DEC_ROWS, B_K_WIDTH), lambda bi, j: (bi, t(j), 0, 0))
        return [rows, cols, decs]

    const = lambda shape: pl.BlockSpec(shape, lambda bi, j: (0,) * len(shape))
    state = pl.BlockSpec((1, B_HEADS, B_KEY_DIM, B_VAL_DIM), lambda bi, j: (bi, 0, 0, 0))
    out_rows = lambda w_: pl.BlockSpec((1, SCAN_TILE, w_), lambda bi, j: (bi, t_bwd(j), 0))
    in_specs = direction(t_fwd) + direction(t_bwd) + [
        pl.BlockSpec((1, SCAN_TILE, B_V_WIDTH), lambda bi, j: (bi, t_both(j), 0)),
        state, state, out_rows(B_V_WIDTH), out_rows(D),
        pl.BlockSpec((1, 1, 3 * D), lambda bi, j: (bi, 0, 0)),
        const((1, B_V_WIDTH)),
        pl.BlockSpec((B_V_WIDTH, D), lambda bi, j: (0, 0), pipeline_mode=pl.Buffered(1)),
        const((1, D))]
    return pl.pallas_call(
        functools.partial(_gla_scan_kernel, nt),
        grid=(b, 2 * nt),
        in_specs=in_specs,
        out_specs=out_rows(D),
        out_shape=jax.ShapeDtypeStruct((b, s, D), F32),
        scratch_shapes=[pltpu.VMEM((B_HEADS, B_KEY_DIM, B_VAL_DIM), F32),
                        pltpu.VMEM((nt, SCAN_TILE, B_V_WIDTH), F32),
                        pltpu.VMEM((B_V_WIDTH, D), BF16)],
        compiler_params=_params("arbitrary", "arbitrary", vmem_limit=SCAN_VMEM_LIMIT),
        name="gla_scan",
    )(*fwd, *bwd, v, s_f, s_b, sg, x, mod3, head_g.reshape(1, B_V_WIDTH), w_out, final_g.reshape(1, D))


def _rope_tables(n_tokens):
    rows_n = n_tokens // GRID_W
    row = np.repeat(np.arange(rows_n, dtype=np.float32), GRID_W)
    col = np.tile(np.arange(GRID_W, dtype=np.float32), rows_n)
    inv_freq = (np.float32(ROPE_BASE) ** (-np.arange(ROPE_FREQS, dtype=np.float32) / np.float32(ROPE_FREQS)))
    inv_freq = inv_freq.astype(np.float32)
    ang = np.stack([row[:, None] * inv_freq, col[:, None] * inv_freq], axis=1)
    cos, sin = np.cos(ang).astype(np.float32), np.sin(ang).astype(np.float32)
    zero = np.zeros_like(sin)
    tile = lambda t: np.tile(t.reshape(n_tokens, A_HEAD_DIM), (1, LANES // A_HEAD_DIM))
    return (tile(np.stack([cos, cos], axis=2)),
            tile(np.stack([-sin, zero], axis=2)),
            tile(np.stack([zero, sin], axis=2)))


def _pair_heads(t, axis):
    shape = t.shape
    t = t.reshape(shape[:axis] + (A_KV_HEADS, A_GROUP, -1) + shape[axis + 1:])
    return jnp.swapaxes(t, axis, axis + 1).reshape(shape)


def kernel(x, c, ctx, c_ctx, l0_norm_g, l0_w_ada, l0_b_ada, l0_w_in, l0_sink, l0_w_out, l1_norm_g, l1_w_ada, l1_b_ada, l1_w_in, l1_wa1_f, l1_wa2_f, l1_ba_f, l1_wa1_b, l1_wa2_b, l1_ba_b, l1_head_norm_g, l1_w_out, final_norm_g):
    b, s, _ = x.shape
    ctx_row = b

    cvec = jnp.concatenate([c, c_ctx[None, :], jnp.zeros((MOD_ROWS - b - 1, D), F32)], axis=0)
    mod0, mod1 = (m.reshape(MOD_ROWS, 1, 3 * D) for m in _modulation(cvec, l0_w_ada, l0_b_ada, l1_w_ada, l1_b_ada))

    sink = _pair_heads(l0_sink.astype(F32), 0)
    kc, vct, xc1, w0_in, w0_out = _ctx_attention(sink, ctx, mod0, ctx_row, l0_norm_g, l0_w_in, l0_w_out)
    qt, k, vt, sg = _inproj_attn(x, mod0, l0_norm_g, w0_in, _rope_tables(s))
    x1 = _attention(sink, qt, k, vt, kc, vct, sg, x, mod0, w0_out)

    wa1 = jnp.concatenate([l1_wa1_f, l1_wa1_b, jnp.zeros((D, LANES - 2 * GATE_RANK), F32)], axis=1).astype(BF16)
    wa2 = jnp.zeros((LANES, 2 * B_K_WIDTH), F32)
    wa2 = wa2.at[0:GATE_RANK, 0:B_K_WIDTH].set(l1_wa2_f)
    wa2 = wa2.at[GATE_RANK:2 * GATE_RANK, B_K_WIDTH:].set(l1_wa2_b).astype(BF16)
    ba = jnp.concatenate([l1_ba_f, l1_ba_b]).reshape(1, 2 * B_K_WIDTH)
    n_ctx = ctx.shape[1]
    s_f, s_b, w1 = _inproj_gla(xc1.reshape(1, b * n_ctx, D), mod1, ctx_row, l1_norm_g, l1_w_in, wa1, wa2, ba,
                               n_ctx, False)
    outs = _inproj_gla(x1, mod1, None, l1_norm_g, w1, wa1, wa2, ba, CHUNK, True)
    v1, sg1 = outs[6:]
    return _gla_scan(outs[0:3], outs[3:6], v1, s_f, s_b, sg1, x1, mod1,
                     l1_head_norm_g, l1_w_out, final_norm_g)
```

```python
import functools

import jax
import jax.numpy as jnp
import numpy as np
from jax import lax
from jax.experimental import pallas as pl
from jax.experimental.pallas import tpu as pltpu

F32 = jnp.float32
BF16 = jnp.bfloat16

D = 1024
GRID_W = 64
EPS = 1e-6
NEG_INF = -1e30

A_HEADS = 16
A_KV_HEADS = 2
A_GROUP = A_HEADS // A_KV_HEADS
A_HEAD_DIM = 64
A_WIDTH = A_HEADS * A_HEAD_DIM
A_KV_WIDTH = A_KV_HEADS * A_HEAD_DIM
BLOCK = 128
ROPE_BASE = 10000.0
ROPE_FREQS = A_HEAD_DIM // 4
Q_SCALE = A_HEAD_DIM ** -0.5
LOG2_E = 1.4426950408889634

B_HEADS = 4
B_K_WIDTH = D // 2
B_V_WIDTH = D
B_KEY_DIM = B_K_WIDTH // B_HEADS
B_VAL_DIM = B_V_WIDTH // B_HEADS
GATE_RANK = 16
GATE_TEMP = 16.0
CHUNK = 64
K_SCALE = B_KEY_DIM ** -0.5

LANES = 128
MXU_DIM = 256
MOD_ROWS = 16
ROW_TILE = 1024
A_ROW_TILE = 2048
SCAN_TILE = 1024
SUB_ROWS = 256
DEC_ROWS = ROW_TILE // CHUNK
SUM_ROWS = 16
A_BLOCKS_PER_STEP = 8
VMEM_LIMIT = 48 * 1024 * 1024
SCAN_VMEM_LIMIT = 56 * 1024 * 1024


def _params(*sem, vmem_limit=VMEM_LIMIT):
    return pltpu.CompilerParams(dimension_semantics=sem, vmem_limit_bytes=vmem_limit)


def _silu(x):
    return x / (1.0 + jnp.exp(-x))


def _dot(a, b):
    return jnp.dot(a, b, preferred_element_type=F32)


def _dot_nt(a, b):
    return lax.dot_general(a, b, (((1,), (1,)), ((), ())), preferred_element_type=F32)


def _dot_tn(a, b):
    return lax.dot_general(a, b, (((0,), (0,)), ((), ())), preferred_element_type=F32)


def _norm_mod(x, g, m):
    ms = jnp.mean(x * x, axis=-1, keepdims=True)
    y = x * lax.rsqrt(ms + EPS) * g
    return y * (1.0 + m[:, D:2 * D]) + m[:, 0:D]


def _cast_once(src_ref, dst_ref):
    @pl.when((pl.program_id(0) == 0) & (pl.program_id(1) == 0))
    def _():
        def body(i, carry):
            rows = pl.ds(pl.multiple_of(i * LANES, LANES), LANES)
            dst_ref[rows, :] = src_ref[rows, :].astype(dst_ref.dtype)
            return carry
        lax.fori_loop(0, src_ref.shape[0] // LANES, body, 0)


def _mod_kernel(n_tiles, c_ref, w0_ref, b0_ref, w1_ref, b1_ref, o0_ref, o1_ref):
    s = _silu(c_ref[...]).astype(BF16)
    j = pl.program_id(0)

    @pl.when(j < n_tiles)
    def _():
        o0_ref[...] = _dot(s, w0_ref[...].astype(BF16)) + b0_ref[...]

    @pl.when(j >= n_tiles)
    def _():
        o1_ref[...] = _dot(s, w1_ref[...].astype(BF16)) + b1_ref[...]


def _modulation(cvec, w_ada0, b_ada0, w_ada1, b_ada1):
    n = w_ada0.shape[1] // D
    first = lambda j: (0, jnp.minimum(j, n - 1))
    second = lambda j: (0, jnp.maximum(j - n, 0))
    out = jax.ShapeDtypeStruct((MOD_ROWS, n * D), F32)
    return pl.pallas_call(
        functools.partial(_mod_kernel, n),
        grid=(2 * n,),
        in_specs=[
            pl.BlockSpec((MOD_ROWS, D), lambda j: (0, 0)),
            pl.BlockSpec((D, D), first), pl.BlockSpec((1, D), first),
            pl.BlockSpec((D, D), second), pl.BlockSpec((1, D), second),
        ],
        out_specs=(pl.BlockSpec((MOD_ROWS, D), first), pl.BlockSpec((MOD_ROWS, D), second)),
        out_shape=(out, out),
        compiler_params=_params("arbitrary"),
        name="modulation",
    )(cvec, w_ada0, b_ada0.reshape(1, n * D), w_ada1, b_ada1.reshape(1, n * D))


def _rope(t, cos, sin_hi, sin_lo):
    return t * cos + pltpu.roll(t, LANES - ROPE_FREQS, 1) * sin_hi + pltpu.roll(t, ROPE_FREQS, 1) * sin_lo


def _pair_cast_columns(src_ref, dst_ref):
    @pl.when((pl.program_id(0) == 0) & (pl.program_id(1) == 0))
    def _():
        kv_lo, kv_hi = A_WIDTH, A_WIDTH + 2 * A_KV_WIDTH

        def body(i, carry):
            rows = pl.ds(pl.multiple_of(i * LANES, LANES), LANES)
            src = src_ref[rows, :]
            dst_ref[rows, kv_lo:kv_hi] = src[:, kv_lo:kv_hi].astype(BF16)
            for base in (0, kv_hi):
                for p in range(A_GROUP):
                    a = base + p * A_HEAD_DIM
                    b = base + (p + A_GROUP) * A_HEAD_DIM
                    pair = jnp.concatenate([src[:, a:a + A_HEAD_DIM], src[:, b:b + A_HEAD_DIM]], axis=1)
                    dst_ref[rows, base + p * LANES:base + (p + 1) * LANES] = pair.astype(BF16)
            return carry
        lax.fori_loop(0, src_ref.shape[0] // LANES, body, 0)


def _inproj_attn_kernel(rope, x_ref, m_ref, g_ref, w_ref, *rest):
    wq_ref = w_ref.at[:, 0:A_WIDTH]
    wkv_ref = w_ref.at[:, A_WIDTH:A_WIDTH + 2 * A_KV_WIDTH]
    wg_ref = w_ref.at[:, A_WIDTH + 2 * A_KV_WIDTH:]
    if rope:
        cos_ref, shi_ref, slo_ref, qt_ref, k_ref, vt_ref, sg_ref = rest
    else:
        qt_ref, k_ref, vt_ref, sg_ref = rest
    assert x_ref.shape[1] % SUB_ROWS == 0
    for r in range(x_ref.shape[1] // SUB_ROWS):
        rows = slice(r * SUB_ROWS, (r + 1) * SUB_ROWS)
        if rope:
            cos, shi, slo = cos_ref[rows], shi_ref[rows], slo_ref[rows]
        hb = _norm_mod(x_ref[0, rows], g_ref[...], m_ref[0]).astype(BF16)
        q = _dot(hb, wq_ref[...])
        for j in range(A_WIDTH // LANES):
            qj = q[:, j * LANES:(j + 1) * LANES]
            if rope:
                qj = _rope(qj, cos, shi, slo)
            qt_ref[0, j * LANES:(j + 1) * LANES, rows] = (qj * (Q_SCALE * LOG2_E)).T.astype(BF16)
        kv = _dot(hb, wkv_ref[...])
        k = kv[:, 0:A_KV_WIDTH]
        if rope:
            k = _rope(k, cos, shi, slo)
        k_ref[0, rows] = k.astype(BF16)
        vt_ref[0, :, rows] = kv[:, A_KV_WIDTH:].T.astype(BF16)
        g = _dot(hb, wg_ref[...])
        sg_ref[0, rows] = _silu(g).astype(BF16)


def _inproj_attn(x, mod3, norm_g, w_in, tables):
    b, s, _ = x.shape
    tm = min(A_ROW_TILE, s)
    in_specs = [
        pl.BlockSpec((1, tm, D), lambda bi, i: (bi, i, 0)),
        pl.BlockSpec((1, 1, 3 * D), lambda bi, i: (bi, 0, 0)),
        pl.BlockSpec((1, D), lambda bi, i: (0, 0)),
        pl.BlockSpec(w_in.shape, lambda bi, i: (0, 0)),
    ] + [pl.BlockSpec((tm, LANES), lambda bi, i: (i, 0))] * 3
    args = [x, mod3, norm_g.reshape(1, D), w_in, *tables]
    out_shape = (
        jax.ShapeDtypeStruct((b, A_WIDTH, s), BF16),
        jax.ShapeDtypeStruct((b, s, A_KV_WIDTH), BF16),
        jax.ShapeDtypeStruct((b, A_KV_WIDTH, s), BF16),
        jax.ShapeDtypeStruct((b, s, A_WIDTH), BF16),
    )
    out_specs = (
        pl.BlockSpec((1, A_WIDTH, tm), lambda bi, i: (bi, 0, i)),
        pl.BlockSpec((1, tm, A_KV_WIDTH), lambda bi, i: (bi, i, 0)),
        pl.BlockSpec((1, A_KV_WIDTH, tm), lambda bi, i: (bi, 0, i)),
        pl.BlockSpec((1, tm, A_WIDTH), lambda bi, i: (bi, i, 0)),
    )
    return pl.pallas_call(
        functools.partial(_inproj_attn_kernel, True),
        grid=(b, s // tm),
        in_specs=in_specs,
        out_specs=out_specs,
        out_shape=out_shape,
        compiler_params=_params("parallel", "parallel"),
        name="inproj_attn",
    )(*args)


def _attn_block(kwin, vtw, qts, sinks, ok_prev, ok_next):
    n_keys, nq = kwin.shape[0], qts.shape[1]
    lane = lax.broadcasted_iota(jnp.int32, kwin.shape, 1)
    kbd = jnp.concatenate([jnp.where(lane < A_HEAD_DIM, kwin, jnp.zeros_like(kwin)),
                           jnp.where(lane >= A_HEAD_DIM, kwin, jnp.zeros_like(kwin))], axis=0)
    ones = jnp.where(lax.broadcasted_iota(jnp.int32, (SUM_ROWS, n_keys), 0) == 0, 1.0, 0.0).astype(BF16)

    st = _dot(kbd, qts).astype(BF16)
    neg = jnp.asarray(NEG_INF, BF16)
    outs = []
    for hh in range(2):
        sh = st[hh * n_keys:(hh + 1) * n_keys]
        if ok_prev is not None:
            parts = [jnp.where(ok_prev, sh[0:BLOCK], neg),
                     sh[BLOCK:2 * BLOCK],
                     jnp.where(ok_next, sh[2 * BLOCK:3 * BLOCK], neg),
                     sh[3 * BLOCK:]]
        else:
            parts = [sh]
        mx = functools.reduce(jnp.maximum, [jnp.max(t, axis=0, keepdims=True) for t in parts])
        mxb = jnp.maximum(mx.astype(F32), sinks[hh]).astype(BF16)
        mx = mxb.astype(F32)
        probs = jnp.concatenate([jnp.exp2(t - mxb) for t in parts], axis=0)
        vt_h = jnp.concatenate([vtw[hh * A_HEAD_DIM:(hh + 1) * A_HEAD_DIM], ones], axis=0)
        ot = _dot(vt_h, probs)
        den = ot[A_HEAD_DIM:A_HEAD_DIM + 1] + jnp.exp2(sinks[hh] - mx)
        outs.append(ot[0:A_HEAD_DIM] * (1.0 / den))
    return jnp.concatenate(outs, axis=0)


def _pair_cast_rows(src_ref, dst_ref):
    @pl.when((pl.program_id(0) == 0) & (pl.program_id(1) == 0))
    def _():
        for p in range(A_GROUP):
            for half, h in enumerate((p, p + A_GROUP)):
                lo = p * LANES + half * A_HEAD_DIM
                dst_ref[lo:lo + A_HEAD_DIM, :] = src_ref[h * A_HEAD_DIM:(h + 1) * A_HEAD_DIM, :].astype(BF16)


def _attn_kernel(local, n_steps, sink_ref, qt_ref, *rest):
    n_blk = rest[-1].shape[1] // BLOCK
    if local:
        (kp_ref, kc_ref, kn_ref, vp_ref, vc_ref, vn_ref, kx_ref, vx_ref,
         sg_ref, x_ref, m_ref, w_ref, o_ref) = rest
        last = slice((n_blk - 1) * BLOCK, n_blk * BLOCK)
        k_blocks = ([kp_ref[0, last]] + [kc_ref[0, i * BLOCK:(i + 1) * BLOCK] for i in range(n_blk)]
                    + [kn_ref[0, 0:BLOCK]])
        v_blocks = ([vp_ref[0, :, last]] + [vc_ref[0, :, i * BLOCK:(i + 1) * BLOCK] for i in range(n_blk)]
                    + [vn_ref[0, :, 0:BLOCK]])
    else:
        kx_ref, vx_ref, sg_ref, x_ref, m_ref, w_ref, o_ref = rest
    step = pl.program_id(1)
    nq = A_GROUP * BLOCK
    chunk_of = lax.broadcasted_iota(jnp.int32, (1, nq), 1) // BLOCK
    sinks = []
    for hh in range(2):
        sk = jnp.full((1, nq), sink_ref[hh] * LOG2_E, F32)
        for c in range(1, A_GROUP):
            sk = jnp.where(chunk_of == c, sink_ref[2 * c + hh] * LOG2_E, sk)
        sinks.append(sk)
    if local:
        kj = lax.broadcasted_iota(jnp.int32, (BLOCK, nq), 0)
        qi = lax.broadcasted_iota(jnp.int32, (BLOCK, nq), 1) % BLOCK

    rows_out = []
    for blk in range(n_blk):
        qts = jnp.concatenate([qt_ref[0, c * LANES:(c + 1) * LANES, blk * BLOCK:(blk + 1) * BLOCK]
                               for c in range(A_GROUP)], axis=1)
        if local:
            kwin = jnp.concatenate(k_blocks[blk:blk + 3] + [kx_ref[0]], axis=0)
            vtw = jnp.concatenate(v_blocks[blk:blk + 3] + [vx_ref[0]], axis=1)
            ok_prev = (kj >= qi) & (step > 0) if blk == 0 else (kj >= qi)
            ok_next = (kj <= qi) & (step < n_steps - 1) if blk == n_blk - 1 else (kj <= qi)
        else:
            kwin, vtw, ok_prev, ok_next = kx_ref[0], vx_ref[0], None, None
        ot = _attn_block(kwin, vtw, qts, sinks, ok_prev, ok_next)
        rows = slice(blk * BLOCK, (blk + 1) * BLOCK)
        outs = []
        for c in range(A_GROUP):
            sg = sg_ref[0, rows, c * LANES:(c + 1) * LANES].astype(F32)
            outs.append((ot[:, c * BLOCK:(c + 1) * BLOCK].T * sg).astype(BF16))
        rows_out.append(jnp.concatenate(outs, axis=1))
    y = _dot(jnp.concatenate(rows_out, axis=0), w_ref[...])
    gate = m_ref[0][:, 2 * D:3 * D]
    o_ref[0] = x_ref[0] + gate * y


def _attention(sink, qt, k, vt, kx, vxt, sg, x, mod3, w_out):
    b, s, _ = sg.shape
    rows = A_BLOCKS_PER_STEP * BLOCK
    assert s % rows == 0
    ns = s // rows
    n_ctx = kx.shape[1]
    blk = lambda w: pl.BlockSpec((1, rows, w), lambda bi, i: (bi, i, 0))
    blk_t = lambda w: pl.BlockSpec((1, w, rows), lambda bi, i: (bi, 0, i))
    lo = lambda i: jnp.maximum(i - 1, 0)
    hi = lambda i: jnp.minimum(i + 1, ns - 1)
    in_specs = [pl.BlockSpec(memory_space=pltpu.SMEM), blk_t(A_WIDTH),
                pl.BlockSpec((1, rows, A_KV_WIDTH), lambda bi, i: (bi, lo(i), 0)),
                blk(A_KV_WIDTH),
                pl.BlockSpec((1, rows, A_KV_WIDTH), lambda bi, i: (bi, hi(i), 0)),
                pl.BlockSpec((1, A_KV_WIDTH, rows), lambda bi, i: (bi, 0, lo(i))),
                blk_t(A_KV_WIDTH),
                pl.BlockSpec((1, A_KV_WIDTH, rows), lambda bi, i: (bi, 0, hi(i))),
                pl.BlockSpec((1, n_ctx, A_KV_WIDTH), lambda bi, i: (bi, 0, 0)),
                pl.BlockSpec((1, A_KV_WIDTH, n_ctx), lambda bi, i: (bi, 0, 0)),
                blk(A_WIDTH), blk(D),
                pl.BlockSpec((1, 1, 3 * D), lambda bi, i: (bi, 0, 0)),
                pl.BlockSpec((A_WIDTH, D), lambda bi, i: (0, 0))]
    return pl.pallas_call(
        functools.partial(_attn_kernel, True, ns),
        grid=(b, ns),
        in_specs=in_specs,
        out_specs=blk(D),
        out_shape=jax.ShapeDtypeStruct((b, s, D), F32),
        compiler_params=_params("parallel", "parallel"),
        name="attn_local",
    )(sink, qt, k, k, k, vt, vt, vt, kx, vxt, sg, x, mod3, w_out)


def _ctx_attn_kernel(sink_ref, x_ref, m_ref, g_ref, win_ref, wout_ref, k_ref, vt_ref, o_ref,
                     win_bf_ref, wout_bf_ref, qt_scr, sg_scr):
    _pair_cast_columns(win_ref, win_bf_ref)
    _pair_cast_rows(wout_ref, wout_bf_ref)
    _inproj_attn_kernel(False, x_ref, m_ref, g_ref, win_bf_ref, qt_scr, k_ref, vt_ref, sg_scr)
    _attn_kernel(False, 1, sink_ref, qt_scr, k_ref, vt_ref, sg_scr, x_ref, m_ref, wout_bf_ref, o_ref)


def _ctx_attention(sink, xc, mod3, mod_row, norm_g, w_in, w_out):
    b, n, _ = xc.shape
    const = lambda shape, **kw: pl.BlockSpec(shape, lambda bi, i: (0,) * len(shape), **kw)
    return pl.pallas_call(
        _ctx_attn_kernel,
        grid=(b, 1),
        in_specs=[pl.BlockSpec(memory_space=pltpu.SMEM),
                  pl.BlockSpec((1, n, D), lambda bi, i: (bi, 0, 0)),
                  pl.BlockSpec((1, 1, 3 * D), lambda bi, i: (mod_row, 0, 0)),
                  const((1, D)),
                  const(w_in.shape, pipeline_mode=pl.Buffered(1)),
                  const(w_out.shape, pipeline_mode=pl.Buffered(1))],
        out_specs=(pl.BlockSpec((1, n, A_KV_WIDTH), lambda bi, i: (bi, 0, 0)),
                   pl.BlockSpec((1, A_KV_WIDTH, n), lambda bi, i: (bi, 0, 0)),
                   pl.BlockSpec((1, n, D), lambda bi, i: (bi, 0, 0)),
                   const(w_in.shape), const(w_out.shape)),
        out_shape=(jax.ShapeDtypeStruct((b, n, A_KV_WIDTH), BF16),
                   jax.ShapeDtypeStruct((b, A_KV_WIDTH, n), BF16),
                   jax.ShapeDtypeStruct((b, n, D), F32),
                   jax.ShapeDtypeStruct(w_in.shape, BF16), jax.ShapeDtypeStruct(w_out.shape, BF16)),
        scratch_shapes=[pltpu.VMEM((1, A_WIDTH, n), BF16), pltpu.VMEM((1, n, A_WIDTH), BF16)],
        compiler_params=_params("arbitrary", "arbitrary"),
        name="ctx_attn",
    )(sink, xc, mod3, norm_g.reshape(1, D), w_in, w_out)


def _chunk_cumsum(x, tri):
    return _dot(tri, x.astype(BF16))


def _inproj_gla_kernel(chunk, with_q, x_ref, m_ref, g_ref, w_in_ref, wa1_ref, wa2_ref, ba_ref, tri_ref, *outs):
    if with_q:
        w_ref = w_in_ref
        per_dir = (outs[0:3], outs[3:6])
        v_ref, sg_ref = outs[6:]
    else:
        w_ref = outs[2]
        _cast_once(w_in_ref, w_ref)
        per_dir = ((outs[0],), (outs[1],))
    tm = x_ref.shape[1]
    hb = _norm_mod(x_ref[0], g_ref[...], m_ref[0]).astype(BF16)
    r = _dot(hb, wa1_ref[...])
    z = _dot(r.astype(BF16), wa2_ref[...]) + ba_ref[...]
    e = jnp.exp2(jnp.abs(z) * -LOG2_E)
    la = (jnp.minimum(z, 0.0) * LOG2_E - jnp.log2(1.0 + e)) * (1.0 / GATE_TEMP)
    k = _dot(hb, w_ref[:, B_K_WIDTH:2 * B_K_WIDTH])
    if with_q:
        q = _dot(hb, w_ref[:, 0:B_K_WIDTH]) * K_SCALE
    n_chunks = tm // chunk
    k_ends = ([], [])
    for reverse in (False, True):
        lad = la[:, B_K_WIDTH:] if reverse else la[:, 0:B_K_WIDTH]
        refs = per_dir[1] if reverse else per_dir[0]
        tots, kis = [], []
        tri = tri_ref[1 if reverse else 0]
        tb = tri.shape[0]
        cum_all = jnp.concatenate([_chunk_cumsum(lad[r * tb:(r + 1) * tb], tri) for r in range(tm // tb)], axis=0)
        for c in range(n_chunks):
            rows = slice(c * chunk, (c + 1) * chunk)
            cum = cum_all[rows]
            tot = cum[0:1] if reverse else cum[chunk - 1:chunk]
            if not with_q:
                k_ends[reverse].append((k[rows] * jnp.exp2(tot - cum)).astype(BF16))
                continue
            refs[0][0, rows] = (q[rows] * jnp.exp2(cum)).astype(BF16)
            kis.append(k[rows] * jnp.exp2(-cum))
            tots.append(tot)
            if c % 2 == 1:
                pair = slice((c - 1) * chunk, (c + 1) * chunk)
                refs[1][0, :, pair] = jnp.concatenate(kis[-2:], axis=0).T.astype(BF16)
        if with_q:
            pad = [jnp.zeros((DEC_ROWS - n_chunks, B_K_WIDTH), F32)] if n_chunks < DEC_ROWS else []
            refs[2][0, 0] = jnp.exp2(jnp.concatenate(tots + pad, axis=0))
    if with_q:
        sg_ref[0] = _silu(_dot(hb, w_ref[:, 2 * B_K_WIDTH + B_V_WIDTH:])).astype(BF16)
    v = _dot(hb, w_ref[:, 2 * B_K_WIDTH:2 * B_K_WIDTH + B_V_WIDTH]).astype(BF16)
    if with_q:
        v_ref[0] = v
    else:
        for refs, per_chunk in zip(per_dir, k_ends):
            for c, k_end in enumerate(per_chunk):
                vc = v[c * chunk:(c + 1) * chunk]
                for h in range(B_HEADS):
                    ks = slice(h * B_KEY_DIM, (h + 1) * B_KEY_DIM)
                    refs[0][c, h] = _dot_tn(k_end[:, ks], vc[:, h * B_VAL_DIM:(h + 1) * B_VAL_DIM])


def _inproj_gla(x, mod3, mod_row, norm_g, w, wa1, wa2, ba, chunk, with_q):
    b, s, _ = x.shape
    tm = min(ROW_TILE, s)
    nt = s // tm
    cpt = tm // chunk
    tb = max(chunk, MXU_DIM)
    assert tm % tb == 0 and tb % chunk == 0
    if mod_row is None:
        mod_map = lambda bi, i: (bi, 0, 0)
    else:
        mod_map = lambda bi, i: (mod_row, 0, 0)
    const = lambda shape, **kw: pl.BlockSpec(shape, lambda bi, i: (0,) * len(shape), **kw)
    in_specs = [
        pl.BlockSpec((1, tm, D), lambda bi, i: (bi, i, 0)),
        pl.BlockSpec((1, 1, 3 * D), mod_map),
        const((1, D)), const(w.shape, **({} if with_q else {"pipeline_mode": pl.Buffered(1)})),
        const(wa1.shape), const(wa2.shape), const(ba.shape),
        const((2, tb, tb)),
    ]
    t_idx = np.arange(tb)
    same = (t_idx[:, None] // chunk) == (t_idx[None, :] // chunk)
    lower = same & (t_idx[None, :] <= t_idx[:, None])
    tri = jnp.asarray(np.stack([lower, lower.T]), dtype=BF16)
    rows = lambda w_: (jax.ShapeDtypeStruct((b, s, w_), BF16),
                       pl.BlockSpec((1, tm, w_), lambda bi, i: (bi, i, 0)))
    cols = (jax.ShapeDtypeStruct((b, B_K_WIDTH, s), BF16),
            pl.BlockSpec((1, B_K_WIDTH, tm), lambda bi, i: (bi, 0, i)))
    decs = (jax.ShapeDtypeStruct((b, nt, DEC_ROWS, B_K_WIDTH), F32),
            pl.BlockSpec((1, 1, DEC_ROWS, B_K_WIDTH), lambda bi, i: (bi, i, 0, 0)))
    if with_q:
        assert cpt <= DEC_ROWS and cpt % 2 == 0
        outs = [rows(B_K_WIDTH), cols, decs] * 2 + [rows(B_V_WIDTH)] * 2
    else:
        assert b == 1
        state = (jax.ShapeDtypeStruct((nt * cpt, B_HEADS, B_KEY_DIM, B_VAL_DIM), F32),
                 pl.BlockSpec((cpt, B_HEADS, B_KEY_DIM, B_VAL_DIM), lambda bi, i: (i, 0, 0, 0)))
        outs = [state] * 2 + [(jax.ShapeDtypeStruct(w.shape, BF16), const(w.shape))]
    return pl.pallas_call(
        functools.partial(_inproj_gla_kernel, chunk, with_q),
        grid=(b, nt),
        in_specs=in_specs,
        out_specs=tuple(o[1] for o in outs),
        out_shape=tuple(o[0] for o in outs),
        compiler_params=_params("parallel", "parallel") if with_q else _params("arbitrary", "arbitrary"),
        name="inproj_gla" if with_q else "inproj_gla_ctx",
    )(x, mod3, norm_g.reshape(1, D), w, wa1, wa2, ba, tri)


def _scan_tile(reverse, qd_ref, kit_ref, dec_ref, v_ref, st_ref):
    pair_rows = 2 * CHUNK
    ti = lax.broadcasted_iota(jnp.int32, (pair_rows, pair_rows), 0)
    si = lax.broadcasted_iota(jnp.int32, (pair_rows, pair_rows), 1)
    same = (ti // CHUNK) == (si // CHUNK)
    if reverse:
        use_inv = same & (ti <= si)
        use_end = (ti < CHUNK) & (si >= CHUNK)
        second_row = lax.broadcasted_iota(jnp.int32, (pair_rows, B_KEY_DIM), 0) < CHUNK
        first_col = lax.broadcasted_iota(jnp.int32, (B_KEY_DIM, pair_rows), 1) >= CHUNK
    else:
        use_inv = same & (ti >= si)
        use_end = (ti >= CHUNK) & (si < CHUNK)
        second_row = lax.broadcasted_iota(jnp.int32, (pair_rows, B_KEY_DIM), 0) >= CHUNK
        first_col = lax.broadcasted_iota(jnp.int32, (B_KEY_DIM, pair_rows), 1) < CHUNK
    n_pairs = SCAN_TILE // pair_rows
    order = range(n_pairs - 1, -1, -1) if reverse else range(n_pairs)
    n_dec = dec_ref.shape[1] * DEC_ROWS
    dec = dec_ref[0].reshape(n_dec, B_K_WIDTH)
    dect = jnp.concatenate([dec, jnp.zeros((LANES - n_dec, B_K_WIDTH), F32)], axis=0).T
    heads = []
    for h in range(B_HEADS):
        ks = slice(h * B_KEY_DIM, (h + 1) * B_KEY_DIM)
        vs = slice(h * B_VAL_DIM, (h + 1) * B_VAL_DIM)
        st = st_ref[h]
        o_rows = [None] * n_pairs
        for p in order:
            rows = slice(p * pair_rows, (p + 1) * pair_rows)
            c_first, c_second = (2 * p + 1, 2 * p) if reverse else (2 * p, 2 * p + 1)
            qd = qd_ref[0, rows, ks]
            kit = kit_ref[0, ks, rows]
            vh = v_ref[0, rows, vs]
            d_first, d_second = dect[ks, c_first:c_first + 1], dect[ks, c_second:c_second + 1]
            ki32 = kit.astype(F32)
            s2 = _dot(qd, jnp.concatenate([kit, (ki32 * d_first).astype(BF16)], axis=1))
            a = jnp.where(use_inv, s2[:, 0:pair_rows], jnp.where(use_end, s2[:, pair_rows:], 0.0)).astype(BF16)
            q_pair = jnp.where(second_row, qd.astype(F32) * dec[c_first:c_first + 1, ks], qd.astype(F32))
            k_pair = ki32 * jnp.where(first_col, d_first * d_second, d_second)
            o_rows[p] = _dot(a, vh) + _dot(q_pair.astype(BF16), st.astype(BF16))
            st = st * (d_first * d_second) + _dot(k_pair.astype(BF16), vh)
        st_ref[h] = st
        heads.append(jnp.concatenate(o_rows, axis=0))
    return jnp.concatenate(heads, axis=1)


def _gla_scan_kernel(nt, qdf_ref, kitf_ref, decf_ref, qdb_ref, kitb_ref, decb_ref,
                     v_ref, sf_ref, sb_ref, sg_ref, x_ref, m_ref, hn_ref, w32_ref, fn_ref,
                     o_ref, st_ref, of_ref, w_ref):
    _cast_once(w32_ref, w_ref)
    j = pl.program_id(1)

    @pl.when(j == 0)
    def _():
        st_ref[...] = sf_ref[0]

    @pl.when(j == nt)
    def _():
        st_ref[...] = sb_ref[0]

    @pl.when(j < nt)
    def _():
        of_ref[j] = _scan_tile(False, qdf_ref, kitf_ref, decf_ref, v_ref, st_ref)

    @pl.when(j >= nt)
    def _():
        o = _scan_tile(True, qdb_ref, kitb_ref, decb_ref, v_ref, st_ref) + of_ref[2 * nt - 1 - j]
        normed = []
        for h in range(B_HEADS):
            oh = o[:, h * B_VAL_DIM:(h + 1) * B_VAL_DIM]
            normed.append(oh * lax.rsqrt(jnp.mean(oh * oh, axis=-1, keepdims=True) + EPS))
        of = jnp.concatenate(normed, axis=1) * hn_ref[...]
        y = _dot((of * sg_ref[0].astype(F32)).astype(BF16), w_ref[...])
        xn = x_ref[0] + m_ref[0][:, 2 * D:3 * D] * y
        ms = jnp.mean(xn * xn, axis=-1, keepdims=True)
        o_ref[0] = xn * lax.rsqrt(ms + EPS) * fn_ref[...]


def _gla_scan(fwd, bwd, v, s_f, s_b, sg, x, mod3, head_g, w_out, final_g):
    b, s, _ = v.shape
    assert SCAN_TILE % ROW_TILE == 0 and ROW_TILE // CHUNK == DEC_ROWS
    dec_blocks = SCAN_TILE // ROW_TILE
    nt = s // SCAN_TILE
    t_fwd = lambda j: jnp.minimum(j, nt - 1)
    t_bwd = lambda j: 2 * nt - 1 - jnp.maximum(j, nt)
    t_both = lambda j: jnp.where(j < nt, j, 2 * nt - 1 - j)

    def direction(t):
        rows = pl.BlockSpec((1, SCAN_TILE, B_K_WIDTH), lambda bi, j: (bi, t(j), 0))
        cols = pl.BlockSpec((1, B_K_WIDTH, SCAN_TILE), lambda bi, j: (bi, 0, t(j)))
        decs = pl.BlockSpec((1, dec_blocks, DEC_ROWS, B_K_WIDTH), lambda bi, j: (bi, t(j), 0, 0))
        return [rows, cols, decs]

    const = lambda shape: pl.BlockSpec(shape, lambda bi, j: (0,) * len(shape))
    state = pl.BlockSpec((1, B_HEADS, B_KEY_DIM, B_VAL_DIM), lambda bi, j: (bi, 0, 0, 0))
    out_rows = lambda w_: pl.BlockSpec((1, SCAN_TILE, w_), lambda bi, j: (bi, t_bwd(j), 0))
    in_specs = direction(t_fwd) + direction(t_bwd) + [
        pl.BlockSpec((1, SCAN_TILE, B_V_WIDTH), lambda bi, j: (bi, t_both(j), 0)),
        state, state, out_rows(B_V_WIDTH), out_rows(D),
        pl.BlockSpec((1, 1, 3 * D), lambda bi, j: (bi, 0, 0)),
        const((1, B_V_WIDTH)),
        pl.BlockSpec((B_V_WIDTH, D), lambda bi, j: (0, 0), pipeline_mode=pl.Buffered(1)),
        const((1, D))]
    return pl.pallas_call(
        functools.partial(_gla_scan_kernel, nt),
        grid=(b, 2 * nt),
        in_specs=in_specs,
        out_specs=out_rows(D),
        out_shape=jax.ShapeDtypeStruct((b, s, D), F32),
        scratch_shapes=[pltpu.VMEM((B_HEADS, B_KEY_DIM, B_VAL_DIM), F32),
                        pltpu.VMEM((nt, SCAN_TILE, B_V_WIDTH), F32),
                        pltpu.VMEM((B_V_WIDTH, D), BF16)],
        compiler_params=_params("arbitrary", "arbitrary", vmem_limit=SCAN_VMEM_LIMIT),
        name="gla_scan",
    )(*fwd, *bwd, v, s_f, s_b, sg, x, mod3, head_g.reshape(1, B_V_WIDTH), w_out, final_g.reshape(1, D))


def _rope_tables(n_tokens):
    rows_n = n_tokens // GRID_W
    row = np.repeat(np.arange(rows_n, dtype=np.float32), GRID_W)
    col = np.tile(np.arange(GRID_W, dtype=np.float32), rows_n)
    inv_freq = (np.float32(ROPE_BASE) ** (-np.arange(ROPE_FREQS, dtype=np.float32) / np.float32(ROPE_FREQS)))
    inv_freq = inv_freq.astype(np.float32)
    ang = np.stack([row[:, None] * inv_freq, col[:, None] * inv_freq], axis=1)
    cos, sin = np.cos(ang).astype(np.float32), np.sin(ang).astype(np.float32)
    zero = np.zeros_like(sin)
    tile = lambda t: np.tile(t.reshape(n_tokens, A_HEAD_DIM), (1, LANES // A_HEAD_DIM))
    return (tile(np.stack([cos, cos], axis=2)),
            tile(np.stack([-sin, zero], axis=2)),
            tile(np.stack([zero, sin], axis=2)))


def _pair_heads(t, axis):
    shape = t.shape
    t = t.reshape(shape[:axis] + (A_KV_HEADS, A_GROUP, -1) + shape[axis + 1:])
    return jnp.swapaxes(t, axis, axis + 1).reshape(shape)


def kernel(x, c, ctx, c_ctx, l0_norm_g, l0_w_ada, l0_b_ada, l0_w_in, l0_sink, l0_w_out, l1_norm_g, l1_w_ada, l1_b_ada, l1_w_in, l1_wa1_f, l1_wa2_f, l1_ba_f, l1_wa1_b, l1_wa2_b, l1_ba_b, l1_head_norm_g, l1_w_out, final_norm_g):
    b, s, _ = x.shape
    ctx_row = b

    cvec = jnp.concatenate([c, c_ctx[None, :], jnp.zeros((MOD_ROWS - b - 1, D), F32)], axis=0)
    mod0, mod1 = (m.reshape(MOD_ROWS, 1, 3 * D) for m in _modulation(cvec, l0_w_ada, l0_b_ada, l1_w_ada, l1_b_ada))

    sink = _pair_heads(l0_sink.astype(F32), 0)
    kc, vct, xc1, w0_in, w0_out = _ctx_attention(sink, ctx, mod0, ctx_row, l0_norm_g, l0_w_in, l0_w_out)
    qt, k, vt, sg = _inproj_attn(x, mod0, l0_norm_g, w0_in, _rope_tables(s))
    x1 = _attention(sink, qt, k, vt, kc, vct, sg, x, mod0, w0_out)

    wa1 = jnp.concatenate([l1_wa1_f, l1_wa1_b, jnp.zeros((D, LANES - 2 * GATE_RANK), F32)], axis=1).astype(BF16)
    wa2 = jnp.zeros((LANES, 2 * B_K_WIDTH), F32)
    wa2 = wa2.at[0:GATE_RANK, 0:B_K_WIDTH].set(l1_wa2_f)
    wa2 = wa2.at[GATE_RANK:2 * GATE_RANK, B_K_WIDTH:].set(l1_wa2_b).astype(BF16)
    ba = jnp.concatenate([l1_ba_f, l1_ba_b]).reshape(1, 2 * B_K_WIDTH)
    n_ctx = ctx.shape[1]
    s_f, s_b, w1 = _inproj_gla(xc1.reshape(1, b * n_ctx, D), mod1, ctx_row, l1_norm_g, l1_w_in, wa1, wa2, ba,
                               n_ctx, False)
    outs = _inproj_gla(x1, mod1, None, l1_norm_g, w1, wa1, wa2, ba, CHUNK, True)
    v1, sg1 = outs[6:]
    return _gla_scan(outs[0:3], outs[3:6], v1, s_f, s_b, sg1, x1, mod1,
                     l1_head_norm_g, l1_w_out, final_norm_g)
```

```python
import functools

import jax
import jax.numpy as jnp
import numpy as np
from jax import lax
from jax.experimental import pallas as pl
from jax.experimental.pallas import tpu as pltpu

F32 = jnp.float32
BF16 = jnp.bfloat16

D = 1024
GRID_W = 64
EPS = 1e-6
NEG_INF = -1e30

A_HEADS = 16
A_KV_HEADS = 2
A_GROUP = A_HEADS // A_KV_HEADS
A_HEAD_DIM = 64
A_WIDTH = A_HEADS * A_HEAD_DIM
A_KV_WIDTH = A_KV_HEADS * A_HEAD_DIM
BLOCK = 128
ROPE_BASE = 10000.0
ROPE_FREQS = A_HEAD_DIM // 4
Q_SCALE = A_HEAD_DIM ** -0.5
LOG2_E = 1.4426950408889634

B_HEADS = 4
B_K_WIDTH = D // 2
B_V_WIDTH = D
B_KEY_DIM = B_K_WIDTH // B_HEADS
B_VAL_DIM = B_V_WIDTH // B_HEADS
GATE_RANK = 16
GATE_TEMP = 16.0
CHUNK = 64
K_SCALE = B_KEY_DIM ** -0.5

LANES = 128
MXU_DIM = 256
MOD_ROWS = 16
ROW_TILE = 1024
A_ROW_TILE = 2048
SCAN_TILE = 1024
SUB_ROWS = 256
OUT_ROWS = 512
DEC_ROWS = ROW_TILE // CHUNK
SUM_ROWS = 16
A_BLOCKS_PER_STEP = 8
VMEM_LIMIT = 48 * 1024 * 1024
SCAN_VMEM_LIMIT = 56 * 1024 * 1024


def _params(*sem, vmem_limit=VMEM_LIMIT):
    return pltpu.CompilerParams(dimension_semantics=sem, vmem_limit_bytes=vmem_limit)


def _silu(x):
    return x / (1.0 + jnp.exp(-x))


def _dot(a, b):
    return jnp.dot(a, b, preferred_element_type=F32)


def _dot_nt(a, b):
    return lax.dot_general(a, b, (((1,), (1,)), ((), ())), preferred_element_type=F32)


def _dot_tn(a, b):
    return lax.dot_general(a, b, (((0,), (0,)), ((), ())), preferred_element_type=F32)


def _norm_mod(x, g, m):
    ms = jnp.mean(x * x, axis=-1, keepdims=True)
    y = x * lax.rsqrt(ms + EPS) * g
    return y * (1.0 + m[:, D:2 * D]) + m[:, 0:D]


def _cast_once(src_ref, dst_ref, row_gain_ref=None):
    @pl.when((pl.program_id(0) == 0) & (pl.program_id(1) == 0))
    def _():
        def body(i, carry):
            rows = pl.ds(pl.multiple_of(i * LANES, LANES), LANES)
            w = src_ref[rows, :]
            if row_gain_ref is not None:
                w = w * row_gain_ref[rows, :]
            dst_ref[rows, :] = w.astype(dst_ref.dtype)
            return carry
        lax.fori_loop(0, src_ref.shape[0] // LANES, body, 0)


def _mod_kernel(n_tiles, c_ref, w0_ref, b0_ref, w1_ref, b1_ref, o0_ref, o1_ref):
    s = _silu(c_ref[...]).astype(BF16)
    j = pl.program_id(0)

    @pl.when(j < n_tiles)
    def _():
        o0_ref[...] = _dot(s, w0_ref[...].astype(BF16)) + b0_ref[...]

    @pl.when(j >= n_tiles)
    def _():
        o1_ref[...] = _dot(s, w1_ref[...].astype(BF16)) + b1_ref[...]


def _modulation(cvec, w_ada0, b_ada0, w_ada1, b_ada1):
    n = w_ada0.shape[1] // D
    first = lambda j: (0, jnp.minimum(j, n - 1))
    second = lambda j: (0, jnp.maximum(j - n, 0))
    out = jax.ShapeDtypeStruct((MOD_ROWS, n * D), F32)
    return pl.pallas_call(
        functools.partial(_mod_kernel, n),
        grid=(2 * n,),
        in_specs=[
            pl.BlockSpec((MOD_ROWS, D), lambda j: (0, 0)),
            pl.BlockSpec((D, D), first), pl.BlockSpec((1, D), first),
            pl.BlockSpec((D, D), second), pl.BlockSpec((1, D), second),
        ],
        out_specs=(pl.BlockSpec((MOD_ROWS, D), first), pl.BlockSpec((MOD_ROWS, D), second)),
        out_shape=(out, out),
        compiler_params=_params("arbitrary"),
        name="modulation",
    )(cvec, w_ada0, b_ada0.reshape(1, n * D), w_ada1, b_ada1.reshape(1, n * D))


def _rope(t, cos, sin_hi, sin_lo):
    return t * cos + pltpu.roll(t, LANES - ROPE_FREQS, 1) * sin_hi + pltpu.roll(t, ROPE_FREQS, 1) * sin_lo


def _pair_cast_columns(src_ref, dst_ref):
    @pl.when((pl.program_id(0) == 0) & (pl.program_id(1) == 0))
    def _():
        kv_lo, kv_hi = A_WIDTH, A_WIDTH + 2 * A_KV_WIDTH

        def body(i, carry):
            rows = pl.ds(pl.multiple_of(i * LANES, LANES), LANES)
            src = src_ref[rows, :]
            dst_ref[rows, kv_lo:kv_hi] = src[:, kv_lo:kv_hi].astype(BF16)
            for base in (0, kv_hi):
                for p in range(A_GROUP):
                    a = base + p * A_HEAD_DIM
                    b = base + (p + A_GROUP) * A_HEAD_DIM
                    pair = jnp.concatenate([src[:, a:a + A_HEAD_DIM], src[:, b:b + A_HEAD_DIM]], axis=1)
                    dst_ref[rows, base + p * LANES:base + (p + 1) * LANES] = pair.astype(BF16)
            return carry
        lax.fori_loop(0, src_ref.shape[0] // LANES, body, 0)


def _inproj_attn_kernel(rope, x_ref, m_ref, g_ref, w_ref, *rest):
    wq_ref = w_ref.at[:, 0:A_WIDTH]
    wkv_ref = w_ref.at[:, A_WIDTH:A_WIDTH + 2 * A_KV_WIDTH]
    wg_ref = w_ref.at[:, A_WIDTH + 2 * A_KV_WIDTH:]
    if rope:
        cos_ref, shi_ref, slo_ref, qt_ref, k_ref, vt_ref, sg_ref = rest
    else:
        qt_ref, k_ref, vt_ref, sg_ref = rest
    assert x_ref.shape[1] % SUB_ROWS == 0
    for r in range(x_ref.shape[1] // SUB_ROWS):
        rows = slice(r * SUB_ROWS, (r + 1) * SUB_ROWS)
        if rope:
            cos, shi, slo = cos_ref[rows], shi_ref[rows], slo_ref[rows]
        hb = _norm_mod(x_ref[0, rows], g_ref[...], m_ref[0]).astype(BF16)
        q = _dot(hb, wq_ref[...])
        for j in range(A_WIDTH // LANES):
            qj = q[:, j * LANES:(j + 1) * LANES]
            if rope:
                qj = _rope(qj, cos, shi, slo)
            qt_ref[0, j * LANES:(j + 1) * LANES, rows] = (qj * (Q_SCALE * LOG2_E)).T.astype(BF16)
        kv = _dot(hb, wkv_ref[...])
        k = kv[:, 0:A_KV_WIDTH]
        if rope:
            k = _rope(k, cos, shi, slo)
        k_ref[0, rows] = k.astype(BF16)
        vt_ref[0, :, rows] = kv[:, A_KV_WIDTH:].T.astype(BF16)
        g = _dot(hb, wg_ref[...])
        sg_ref[0, rows] = _silu(g).astype(BF16)


def _inproj_attn(x, mod3, norm_g, w_in, tables):
    b, s, _ = x.shape
    tm = min(A_ROW_TILE, s)
    in_specs = [
        pl.BlockSpec((1, tm, D), lambda bi, i: (bi, i, 0)),
        pl.BlockSpec((1, 1, 3 * D), lambda bi, i: (bi, 0, 0)),
        pl.BlockSpec((1, D), lambda bi, i: (0, 0)),
        pl.BlockSpec(w_in.shape, lambda bi, i: (0, 0)),
    ] + [pl.BlockSpec((tm, LANES), lambda bi, i: (i, 0))] * 3
    args = [x, mod3, norm_g.reshape(1, D), w_in, *tables]
    out_shape = (
        jax.ShapeDtypeStruct((b, A_WIDTH, s), BF16),
        jax.ShapeDtypeStruct((b, s, A_KV_WIDTH), BF16),
        jax.ShapeDtypeStruct((b, A_KV_WIDTH, s), BF16),
        jax.ShapeDtypeStruct((b, s, A_WIDTH), BF16),
    )
    out_specs = (
        pl.BlockSpec((1, A_WIDTH, tm), lambda bi, i: (bi, 0, i)),
        pl.BlockSpec((1, tm, A_KV_WIDTH), lambda bi, i: (bi, i, 0)),
        pl.BlockSpec((1, A_KV_WIDTH, tm), lambda bi, i: (bi, 0, i)),
        pl.BlockSpec((1, tm, A_WIDTH), lambda bi, i: (bi, i, 0)),
    )
    return pl.pallas_call(
        functools.partial(_inproj_attn_kernel, True),
        grid=(b, s // tm),
        in_specs=in_specs,
        out_specs=out_specs,
        out_shape=out_shape,
        compiler_params=_params("parallel", "parallel"),
        name="inproj_attn",
    )(*args)


def _attn_block(kwin, vtw, qts, sinks, ok_prev, ok_next):
    n_keys, nq = kwin.shape[0], qts.shape[1]
    lane = lax.broadcasted_iota(jnp.int32, kwin.shape, 1)
    kbd = jnp.concatenate([jnp.where(lane < A_HEAD_DIM, kwin, jnp.zeros_like(kwin)),
                           jnp.where(lane >= A_HEAD_DIM, kwin, jnp.zeros_like(kwin))], axis=0)
    ones = jnp.where(lax.broadcasted_iota(jnp.int32, (SUM_ROWS, n_keys), 0) == 0, 1.0, 0.0).astype(BF16)

    st = _dot(kbd, qts).astype(BF16)
    neg = jnp.asarray(NEG_INF, BF16)
    outs = []
    for hh in range(2):
        sh = st[hh * n_keys:(hh + 1) * n_keys]
        if ok_prev is not None:
            parts = [jnp.where(ok_prev, sh[0:BLOCK], neg),
                     sh[BLOCK:2 * BLOCK],
                     jnp.where(ok_next, sh[2 * BLOCK:3 * BLOCK], neg),
                     sh[3 * BLOCK:]]
        else:
            parts = [sh]
        mx = functools.reduce(jnp.maximum, [jnp.max(t, axis=0, keepdims=True) for t in parts])
        mxb = jnp.maximum(mx.astype(F32), sinks[hh]).astype(BF16)
        mx = mxb.astype(F32)
        probs = jnp.concatenate([jnp.exp2(t - mxb) for t in parts], axis=0)
        vt_h = jnp.concatenate([vtw[hh * A_HEAD_DIM:(hh + 1) * A_HEAD_DIM], ones], axis=0)
        ot = _dot(vt_h, probs)
        den = ot[A_HEAD_DIM:A_HEAD_DIM + 1] + jnp.exp2(sinks[hh] - mx)
        outs.append(ot[0:A_HEAD_DIM] * (1.0 / den))
    return jnp.concatenate(outs, axis=0)


def _pair_cast_rows(src_ref, dst_ref):
    @pl.when((pl.program_id(0) == 0) & (pl.program_id(1) == 0))
    def _():
        for p in range(A_GROUP):
            for half, h in enumerate((p, p + A_GROUP)):
                lo = p * LANES + half * A_HEAD_DIM
                dst_ref[lo:lo + A_HEAD_DIM, :] = src_ref[h * A_HEAD_DIM:(h + 1) * A_HEAD_DIM, :].astype(BF16)


def _attn_kernel(local, n_steps, sink_ref, qt_ref, *rest):
    n_blk = rest[-1].shape[1] // BLOCK
    if local:
        (kp_ref, kc_ref, kn_ref, vp_ref, vc_ref, vn_ref, kx_ref, vx_ref,
         sg_ref, x_ref, m_ref, w_ref, o_ref) = rest
        last = slice((n_blk - 1) * BLOCK, n_blk * BLOCK)
        k_blocks = ([kp_ref[0, last]] + [kc_ref[0, i * BLOCK:(i + 1) * BLOCK] for i in range(n_blk)]
                    + [kn_ref[0, 0:BLOCK]])
        v_blocks = ([vp_ref[0, :, last]] + [vc_ref[0, :, i * BLOCK:(i + 1) * BLOCK] for i in range(n_blk)]
                    + [vn_ref[0, :, 0:BLOCK]])
    else:
        kx_ref, vx_ref, sg_ref, x_ref, m_ref, w_ref, o_ref = rest
    step = pl.program_id(1)
    nq = A_GROUP * BLOCK
    chunk_of = lax.broadcasted_iota(jnp.int32, (1, nq), 1) // BLOCK
    sinks = []
    for hh in range(2):
        sk = jnp.full((1, nq), sink_ref[hh] * LOG2_E, F32)
        for c in range(1, A_GROUP):
            sk = jnp.where(chunk_of == c, sink_ref[2 * c + hh] * LOG2_E, sk)
        sinks.append(sk)
    if local:
        kj = lax.broadcasted_iota(jnp.int32, (BLOCK, nq), 0)
        qi = lax.broadcasted_iota(jnp.int32, (BLOCK, nq), 1) % BLOCK

    rows_out = []
    for blk in range(n_blk):
        qts = jnp.concatenate([qt_ref[0, c * LANES:(c + 1) * LANES, blk * BLOCK:(blk + 1) * BLOCK]
                               for c in range(A_GROUP)], axis=1)
        if local:
            kwin = jnp.concatenate(k_blocks[blk:blk + 3] + [kx_ref[0]], axis=0)
            vtw = jnp.concatenate(v_blocks[blk:blk + 3] + [vx_ref[0]], axis=1)
            ok_prev = (kj >= qi) & (step > 0) if blk == 0 else (kj >= qi)
            ok_next = (kj <= qi) & (step < n_steps - 1) if blk == n_blk - 1 else (kj <= qi)
        else:
            kwin, vtw, ok_prev, ok_next = kx_ref[0], vx_ref[0], None, None
        ot = _attn_block(kwin, vtw, qts, sinks, ok_prev, ok_next)
        rows = slice(blk * BLOCK, (blk + 1) * BLOCK)
        outs = []
        for c in range(A_GROUP):
            sg = sg_ref[0, rows, c * LANES:(c + 1) * LANES].astype(F32)
            outs.append((ot[:, c * BLOCK:(c + 1) * BLOCK].T * sg).astype(BF16))
        rows_out.append(jnp.concatenate(outs, axis=1))
    y = _dot(jnp.concatenate(rows_out, axis=0), w_ref[...])
    gate = m_ref[0][:, 2 * D:3 * D]
    o_ref[0] = x_ref[0] + gate * y


def _attention(sink, qt, k, vt, kx, vxt, sg, x, mod3, w_out):
    b, s, _ = sg.shape
    rows = A_BLOCKS_PER_STEP * BLOCK
    assert s % rows == 0
    ns = s // rows
    n_ctx = kx.shape[1]
    blk = lambda w: pl.BlockSpec((1, rows, w), lambda bi, i: (bi, i, 0))
    blk_t = lambda w: pl.BlockSpec((1, w, rows), lambda bi, i: (bi, 0, i))
    lo = lambda i: jnp.maximum(i - 1, 0)
    hi = lambda i: jnp.minimum(i + 1, ns - 1)
    in_specs = [pl.BlockSpec(memory_space=pltpu.SMEM), blk_t(A_WIDTH),
                pl.BlockSpec((1, rows, A_KV_WIDTH), lambda bi, i: (bi, lo(i), 0)),
                blk(A_KV_WIDTH),
                pl.BlockSpec((1, rows, A_KV_WIDTH), lambda bi, i: (bi, hi(i), 0)),
                pl.BlockSpec((1, A_KV_WIDTH, rows), lambda bi, i: (bi, 0, lo(i))),
                blk_t(A_KV_WIDTH),
                pl.BlockSpec((1, A_KV_WIDTH, rows), lambda bi, i: (bi, 0, hi(i))),
                pl.BlockSpec((1, n_ctx, A_KV_WIDTH), lambda bi, i: (bi, 0, 0)),
                pl.BlockSpec((1, A_KV_WIDTH, n_ctx), lambda bi, i: (bi, 0, 0)),
                blk(A_WIDTH), blk(D),
                pl.BlockSpec((1, 1, 3 * D), lambda bi, i: (bi, 0, 0)),
                pl.BlockSpec((A_WIDTH, D), lambda bi, i: (0, 0))]
    return pl.pallas_call(
        functools.partial(_attn_kernel, True, ns),
        grid=(b, ns),
        in_specs=in_specs,
        out_specs=blk(D),
        out_shape=jax.ShapeDtypeStruct((b, s, D), F32),
        compiler_params=_params("parallel", "parallel"),
        name="attn_local",
    )(sink, qt, k, k, k, vt, vt, vt, kx, vxt, sg, x, mod3, w_out)


def _ctx_attn_kernel(sink_ref, x_ref, m_ref, g_ref, win_ref, wout_ref, k_ref, vt_ref, o_ref,
                     win_bf_ref, wout_bf_ref, qt_scr, sg_scr):
    _pair_cast_columns(win_ref, win_bf_ref)
    _pair_cast_rows(wout_ref, wout_bf_ref)
    _inproj_attn_kernel(False, x_ref, m_ref, g_ref, win_bf_ref, qt_scr, k_ref, vt_ref, sg_scr)
    _attn_kernel(False, 1, sink_ref, qt_scr, k_ref, vt_ref, sg_scr, x_ref, m_ref, wout_bf_ref, o_ref)


def _ctx_attention(sink, xc, mod3, mod_row, norm_g, w_in, w_out):
    b, n, _ = xc.shape
    const = lambda shape, **kw: pl.BlockSpec(shape, lambda bi, i: (0,) * len(shape), **kw)
    return pl.pallas_call(
        _ctx_attn_kernel,
        grid=(b, 1),
        in_specs=[pl.BlockSpec(memory_space=pltpu.SMEM),
                  pl.BlockSpec((1, n, D), lambda bi, i: (bi, 0, 0)),
                  pl.BlockSpec((1, 1, 3 * D), lambda bi, i: (mod_row, 0, 0)),
                  const((1, D)),
                  const(w_in.shape, pipeline_mode=pl.Buffered(1)),
                  const(w_out.shape, pipeline_mode=pl.Buffered(1))],
        out_specs=(pl.BlockSpec((1, n, A_KV_WIDTH), lambda bi, i: (bi, 0, 0)),
                   pl.BlockSpec((1, A_KV_WIDTH, n), lambda bi, i: (bi, 0, 0)),
                   pl.BlockSpec((1, n, D), lambda bi, i: (bi, 0, 0)),
                   const(w_in.shape), const(w_out.shape)),
        out_shape=(jax.ShapeDtypeStruct((b, n, A_KV_WIDTH), BF16),
                   jax.ShapeDtypeStruct((b, A_KV_WIDTH, n), BF16),
                   jax.ShapeDtypeStruct((b, n, D), F32),
                   jax.ShapeDtypeStruct(w_in.shape, BF16), jax.ShapeDtypeStruct(w_out.shape, BF16)),
        scratch_shapes=[pltpu.VMEM((1, A_WIDTH, n), BF16), pltpu.VMEM((1, n, A_WIDTH), BF16)],
        compiler_params=_params("arbitrary", "arbitrary"),
        name="ctx_attn",
    )(sink, xc, mod3, norm_g.reshape(1, D), w_in, w_out)


def _chunk_cumsum(x, tri):
    return _dot(tri, x.astype(BF16))


def _inproj_gla_kernel(chunk, with_q, x_ref, m_ref, g_ref, w_in_ref, wa1_ref, wa2_ref, ba_ref, tri_ref, *outs):
    if with_q:
        w_ref = w_in_ref
        per_dir = (outs[0:3], outs[3:6])
        v_ref, sg_ref = outs[6:]
    else:
        w_ref = outs[2]
        _cast_once(w_in_ref, w_ref)
        per_dir = ((outs[0],), (outs[1],))
    tm = x_ref.shape[1]
    hb = _norm_mod(x_ref[0], g_ref[...], m_ref[0]).astype(BF16)
    r = _dot(hb, wa1_ref[...])
    z = _dot(r.astype(BF16), wa2_ref[...]) + ba_ref[...]
    e = jnp.exp2(jnp.abs(z) * -LOG2_E)
    la = (jnp.minimum(z, 0.0) * LOG2_E - jnp.log2(1.0 + e)) * (1.0 / GATE_TEMP)
    k = _dot(hb, w_ref[:, B_K_WIDTH:2 * B_K_WIDTH])
    if with_q:
        q = _dot(hb, w_ref[:, 0:B_K_WIDTH]) * K_SCALE
    n_chunks = tm // chunk
    k_ends = ([], [])
    for reverse in (False, True):
        lad = la[:, B_K_WIDTH:] if reverse else la[:, 0:B_K_WIDTH]
        refs = per_dir[1] if reverse else per_dir[0]
        tots, kis = [], []
        tri = tri_ref[1 if reverse else 0]
        tb = tri.shape[0]
        cum_all = jnp.concatenate([_chunk_cumsum(lad[r * tb:(r + 1) * tb], tri) for r in range(tm // tb)], axis=0)
        for c in range(n_chunks):
            rows = slice(c * chunk, (c + 1) * chunk)
            cum = cum_all[rows]
            tot = cum[0:1] if reverse else cum[chunk - 1:chunk]
            if not with_q:
                k_ends[reverse].append((k[rows] * jnp.exp2(tot - cum)).astype(BF16))
                continue
            refs[0][0, rows] = (q[rows] * jnp.exp2(cum)).astype(BF16)
            kis.append(k[rows] * jnp.exp2(-cum))
            tots.append(tot)
            if c % 2 == 1:
                pair = slice((c - 1) * chunk, (c + 1) * chunk)
                refs[1][0, :, pair] = jnp.concatenate(kis[-2:], axis=0).T.astype(BF16)
        if with_q:
            pad = [jnp.zeros((DEC_ROWS - n_chunks, B_K_WIDTH), F32)] if n_chunks < DEC_ROWS else []
            refs[2][0, 0] = jnp.exp2(jnp.concatenate(tots + pad, axis=0))
    if with_q:
        sg_ref[0] = _silu(_dot(hb, w_ref[:, 2 * B_K_WIDTH + B_V_WIDTH:])).astype(BF16)
    v = _dot(hb, w_ref[:, 2 * B_K_WIDTH:2 * B_K_WIDTH + B_V_WIDTH]).astype(BF16)
    if with_q:
        v_ref[0] = v
    else:
        for refs, per_chunk in zip(per_dir, k_ends):
            for c, k_end in enumerate(per_chunk):
                vc = v[c * chunk:(c + 1) * chunk]
                for h in range(B_HEADS):
                    ks = slice(h * B_KEY_DIM, (h + 1) * B_KEY_DIM)
                    refs[0][c, h] = _dot_tn(k_end[:, ks], vc[:, h * B_VAL_DIM:(h + 1) * B_VAL_DIM])


def _inproj_gla(x, mod3, mod_row, norm_g, w, wa1, wa2, ba, chunk, with_q):
    b, s, _ = x.shape
    tm = min(ROW_TILE, s)
    nt = s // tm
    cpt = tm // chunk
    tb = max(chunk, MXU_DIM)
    assert tm % tb == 0 and tb % chunk == 0
    if mod_row is None:
        mod_map = lambda bi, i: (bi, 0, 0)
    else:
        mod_map = lambda bi, i: (mod_row, 0, 0)
    const = lambda shape, **kw: pl.BlockSpec(shape, lambda bi, i: (0,) * len(shape), **kw)
    in_specs = [
        pl.BlockSpec((1, tm, D), lambda bi, i: (bi, i, 0)),
        pl.BlockSpec((1, 1, 3 * D), mod_map),
        const((1, D)), const(w.shape, **({} if with_q else {"pipeline_mode": pl.Buffered(1)})),
        const(wa1.shape), const(wa2.shape), const(ba.shape),
        const((2, tb, tb)),
    ]
    t_idx = np.arange(tb)
    same = (t_idx[:, None] // chunk) == (t_idx[None, :] // chunk)
    lower = same & (t_idx[None, :] <= t_idx[:, None])
    tri = jnp.asarray(np.stack([lower, lower.T]), dtype=BF16)
    rows = lambda w_: (jax.ShapeDtypeStruct((b, s, w_), BF16),
                       pl.BlockSpec((1, tm, w_), lambda bi, i: (bi, i, 0)))
    cols = (jax.ShapeDtypeStruct((b, B_K_WIDTH, s), BF16),
            pl.BlockSpec((1, B_K_WIDTH, tm), lambda bi, i: (bi, 0, i)))
    decs = (jax.ShapeDtypeStruct((b, nt, DEC_ROWS, B_K_WIDTH), F32),
            pl.BlockSpec((1, 1, DEC_ROWS, B_K_WIDTH), lambda bi, i: (bi, i, 0, 0)))
    if with_q:
        assert cpt <= DEC_ROWS and cpt % 2 == 0
        outs = [rows(B_K_WIDTH), cols, decs] * 2 + [rows(B_V_WIDTH)] * 2
    else:
        assert b == 1
        state = (jax.ShapeDtypeStruct((nt * cpt, B_HEADS, B_KEY_DIM, B_VAL_DIM), F32),
                 pl.BlockSpec((cpt, B_HEADS, B_KEY_DIM, B_VAL_DIM), lambda bi, i: (i, 0, 0, 0)))
        outs = [state] * 2 + [(jax.ShapeDtypeStruct(w.shape, BF16), const(w.shape))]
    return pl.pallas_call(
        functools.partial(_inproj_gla_kernel, chunk, with_q),
        grid=(b, nt),
        in_specs=in_specs,
        out_specs=tuple(o[1] for o in outs),
        out_shape=tuple(o[0] for o in outs),
        compiler_params=_params("parallel", "parallel") if with_q else _params("arbitrary", "arbitrary"),
        name="inproj_gla" if with_q else "inproj_gla_ctx",
    )(x, mod3, norm_g.reshape(1, D), w, wa1, wa2, ba, tri)


def _scan_tile(reverse, qd_ref, kit_ref, dec_ref, v_ref, st_ref, head_fn):
    pair_rows = 2 * CHUNK
    ti = lax.broadcasted_iota(jnp.int32, (pair_rows, pair_rows), 0)
    si = lax.broadcasted_iota(jnp.int32, (pair_rows, pair_rows), 1)
    same = (ti // CHUNK) == (si // CHUNK)
    if reverse:
        use_inv = same & (ti <= si)
        use_end = (ti < CHUNK) & (si >= CHUNK)
        second_row = lax.broadcasted_iota(jnp.int32, (pair_rows, B_KEY_DIM), 0) < CHUNK
        first_col = lax.broadcasted_iota(jnp.int32, (B_KEY_DIM, pair_rows), 1) >= CHUNK
    else:
        use_inv = same & (ti >= si)
        use_end = (ti >= CHUNK) & (si < CHUNK)
        second_row = lax.broadcasted_iota(jnp.int32, (pair_rows, B_KEY_DIM), 0) >= CHUNK
        first_col = lax.broadcasted_iota(jnp.int32, (B_KEY_DIM, pair_rows), 1) < CHUNK
    n_pairs = SCAN_TILE // pair_rows
    order = range(n_pairs - 1, -1, -1) if reverse else range(n_pairs)
    n_dec = dec_ref.shape[1] * DEC_ROWS
    dec = dec_ref[0].reshape(n_dec, B_K_WIDTH)
    dect = jnp.concatenate([dec, jnp.zeros((LANES - n_dec, B_K_WIDTH), F32)], axis=0).T
    heads = []
    for h in range(B_HEADS):
        ks = slice(h * B_KEY_DIM, (h + 1) * B_KEY_DIM)
        vs = slice(h * B_VAL_DIM, (h + 1) * B_VAL_DIM)
        st = st_ref[h]
        o_rows = [None] * n_pairs
        for p in order:
            rows = slice(p * pair_rows, (p + 1) * pair_rows)
            c_first, c_second = (2 * p + 1, 2 * p) if reverse else (2 * p, 2 * p + 1)
            qd = qd_ref[0, rows, ks]
            kit = kit_ref[0, ks, rows]
            vh = v_ref[0, rows, vs]
            d_first, d_second = dect[ks, c_first:c_first + 1], dect[ks, c_second:c_second + 1]
            ki32 = kit.astype(F32)
            s2 = _dot(qd, jnp.concatenate([kit, (ki32 * d_first).astype(BF16)], axis=1))
            a = jnp.where(use_inv, s2[:, 0:pair_rows], jnp.where(use_end, s2[:, pair_rows:], 0.0)).astype(BF16)
            q_pair = jnp.where(second_row, qd.astype(F32) * dec[c_first:c_first + 1, ks], qd.astype(F32))
            k_pair = ki32 * jnp.where(first_col, d_first * d_second, d_second)
            o_rows[p] = _dot(a, vh) + _dot(q_pair.astype(BF16), st.astype(BF16))
            st = st * (d_first * d_second) + _dot(k_pair.astype(BF16), vh)
        st_ref[h] = st
        heads.append(head_fn(vs, jnp.concatenate(o_rows, axis=0)))
    return jnp.concatenate(heads, axis=1)


def _gla_scan_kernel(nt, qdf_ref, kitf_ref, decf_ref, qdb_ref, kitb_ref, decb_ref,
                     v_ref, sf_ref, sb_ref, sg_ref, x_ref, m_ref, hn_ref, w32_ref, fn_ref,
                     o_ref, st_ref, of_ref, w_ref):
    _cast_once(w32_ref, w_ref, hn_ref)
    j = pl.program_id(1)

    @pl.when(j == 0)
    def _():
        st_ref[...] = sf_ref[0]

    @pl.when(j == nt)
    def _():
        st_ref[...] = sb_ref[0]

    @pl.when(j < nt)
    def _():
        of_ref[j] = _scan_tile(False, qdf_ref, kitf_ref, decf_ref, v_ref, st_ref, lambda vs, o: o)

    @pl.when(j >= nt)
    def _():
        def gated_head(vs, o_bwd):
            oh = o_bwd + of_ref[2 * nt - 1 - j, :, vs]
            normed = oh * lax.rsqrt(jnp.mean(oh * oh, axis=-1, keepdims=True) + EPS)
            return normed.astype(BF16) * sg_ref[0, :, vs]

        a = _scan_tile(True, qdb_ref, kitb_ref, decb_ref, v_ref, st_ref, gated_head)
        gate = m_ref[0][:, 2 * D:3 * D]
        for r in range(SCAN_TILE // OUT_ROWS):
            rows = slice(r * OUT_ROWS, (r + 1) * OUT_ROWS)
            xn = x_ref[0, rows] + gate * _dot(a[rows], w_ref[...])
            ms = jnp.mean(xn * xn, axis=-1, keepdims=True)
            o_ref[0, rows] = xn * lax.rsqrt(ms + EPS) * fn_ref[...]


def _gla_scan(fwd, bwd, v, s_f, s_b, sg, x, mod3, head_g, w_out, final_g):
    b, s, _ = v.shape
    assert SCAN_TILE % ROW_TILE == 0 and ROW_TILE // CHUNK == DEC_ROWS
    dec_blocks = SCAN_TILE // ROW_TILE
    nt = s // SCAN_TILE
    t_fwd = lambda j: jnp.minimum(j, nt - 1)
    t_bwd = lambda j: 2 * nt - 1 - jnp.maximum(j, nt)
    t_both = lambda j: jnp.where(j < nt, j, 2 * nt - 1 - j)

    def direction(t):
        rows = pl.BlockSpec((1, SCAN_TILE, B_K_WIDTH), lambda bi, j: (bi, t(j), 0))
        cols = pl.BlockSpec((1, B_K_WIDTH, SCAN_TILE), lambda bi, j: (bi, 0, t(j)))
        decs = pl.BlockSpec((1, dec_blocks, DEC_ROWS, B_K_WIDTH), lambda bi, j: (bi, t(j), 0, 0))
        return [rows, cols, decs]

    const = lambda shape: pl.BlockSpec(shape, lambda bi, j: (0,) * len(shape))
    state = pl.BlockSpec((1, B_HEADS, B_KEY_DIM, B_VAL_DIM), lambda bi, j: (bi, 0, 0, 0))
    out_rows = lambda w_: pl.BlockSpec((1, SCAN_TILE, w_), lambda bi, j: (bi, t_bwd(j), 0))
    in_specs = direction(t_fwd) + direction(t_bwd) + [
        pl.BlockSpec((1, SCAN_TILE, B_V_WIDTH), lambda bi, j: (bi, t_both(j), 0)),
        state, state, out_rows(B_V_WIDTH), out_rows(D),
        pl.BlockSpec((1, 1, 3 * D), lambda bi, j: (bi, 0, 0)),
        pl.BlockSpec((B_V_WIDTH, 1), lambda bi, j: (0, 0), pipeline_mode=pl.Buffered(1)),
        pl.BlockSpec((B_V_WIDTH, D), lambda bi, j: (0, 0), pipeline_mode=pl.Buffered(1)),
        const((1, D))]
    return pl.pallas_call(
        functools.partial(_gla_scan_kernel, nt),
        grid=(b, 2 * nt),
        in_specs=in_specs,
        out_specs=out_rows(D),
        out_shape=jax.ShapeDtypeStruct((b, s, D), F32),
        scratch_shapes=[pltpu.VMEM((B_HEADS, B_KEY_DIM, B_VAL_DIM), F32),
                        pltpu.VMEM((nt, SCAN_TILE, B_V_WIDTH), F32),
                        pltpu.VMEM((B_V_WIDTH, D), BF16)],
        compiler_params=_params("arbitrary", "arbitrary", vmem_limit=SCAN_VMEM_LIMIT),
        name="gla_scan",
    )(*fwd, *bwd, v, s_f, s_b, sg, x, mod3, head_g.reshape(B_V_WIDTH, 1), w_out, final_g.reshape(1, D))


def _rope_tables(n_tokens):
    rows_n = n_tokens // GRID_W
    row = np.repeat(np.arange(rows_n, dtype=np.float32), GRID_W)
    col = np.tile(np.arange(GRID_W, dtype=np.float32), rows_n)
    inv_freq = (np.float32(ROPE_BASE) ** (-np.arange(ROPE_FREQS, dtype=np.float32) / np.float32(ROPE_FREQS)))
    inv_freq = inv_freq.astype(np.float32)
    ang = np.stack([row[:, None] * inv_freq, col[:, None] * inv_freq], axis=1)
    cos, sin = np.cos(ang).astype(np.float32), np.sin(ang).astype(np.float32)
    zero = np.zeros_like(sin)
    tile = lambda t: np.tile(t.reshape(n_tokens, A_HEAD_DIM), (1, LANES // A_HEAD_DIM))
    return (tile(np.stack([cos, cos], axis=2)),
            tile(np.stack([-sin, zero], axis=2)),
            tile(np.stack([zero, sin], axis=2)))


def _pair_heads(t, axis):
    shape = t.shape
    t = t.reshape(shape[:axis] + (A_KV_HEADS, A_GROUP, -1) + shape[axis + 1:])
    return jnp.swapaxes(t, axis, axis + 1).reshape(shape)


def kernel(x, c, ctx, c_ctx, l0_norm_g, l0_w_ada, l0_b_ada, l0_w_in, l0_sink, l0_w_out, l1_norm_g, l1_w_ada, l1_b_ada, l1_w_in, l1_wa1_f, l1_wa2_f, l1_ba_f, l1_wa1_b, l1_wa2_b, l1_ba_b, l1_head_norm_g, l1_w_out, final_norm_g):
    b, s, _ = x.shape
    ctx_row = b

    cvec = jnp.concatenate([c, c_ctx[None, :], jnp.zeros((MOD_ROWS - b - 1, D), F32)], axis=0)
    mod0, mod1 = (m.reshape(MOD_ROWS, 1, 3 * D) for m in _modulation(cvec, l0_w_ada, l0_b_ada, l1_w_ada, l1_b_ada))

    sink = _pair_heads(l0_sink.astype(F32), 0)
    kc, vct, xc1, w0_in, w0_out = _ctx_attention(sink, ctx, mod0, ctx_row, l0_norm_g, l0_w_in, l0_w_out)
    qt, k, vt, sg = _inproj_attn(x, mod0, l0_norm_g, w0_in, _rope_tables(s))
    x1 = _attention(sink, qt, k, vt, kc, vct, sg, x, mod0, w0_out)

    wa1 = jnp.concatenate([l1_wa1_f, l1_wa1_b, jnp.zeros((D, LANES - 2 * GATE_RANK), F32)], axis=1).astype(BF16)
    wa2 = jnp.zeros((LANES, 2 * B_K_WIDTH), F32)
    wa2 = wa2.at[0:GATE_RANK, 0:B_K_WIDTH].set(l1_wa2_f)
    wa2 = wa2.at[GATE_RANK:2 * GATE_RANK, B_K_WIDTH:].set(l1_wa2_b).astype(BF16)
    ba = jnp.concatenate([l1_ba_f, l1_ba_b]).reshape(1, 2 * B_K_WIDTH)
    n_ctx = ctx.shape[1]
    s_f, s_b, w1 = _inproj_gla(xc1.reshape(1, b * n_ctx, D), mod1, ctx_row, l1_norm_g, l1_w_in, wa1, wa2, ba,
                               n_ctx, False)
    outs = _inproj_gla(x1, mod1, None, l1_norm_g, w1, wa1, wa2, ba, CHUNK, True)
    v1, sg1 = outs[6:]
    return _gla_scan(outs[0:3], outs[3:6], v1, s_f, s_b, sg1, x1, mod1,
                     l1_head_norm_g, l1_w_out, final_norm_g)
```

```python
import functools

import jax
import jax.numpy as jnp
import numpy as np
from jax import lax
from jax.experimental import pallas as pl
from jax.experimental.pallas import tpu as pltpu

F32 = jnp.float32
BF16 = jnp.bfloat16

D = 1024
GRID_W = 64
EPS = 1e-6
NEG_INF = -1e30

A_HEADS = 16
A_KV_HEADS = 2
A_GROUP = A_HEADS // A_KV_HEADS
A_HEAD_DIM = 64
A_WIDTH = A_HEADS * A_HEAD_DIM
A_KV_WIDTH = A_KV_HEADS * A_HEAD_DIM
BLOCK = 128
ROPE_BASE = 10000.0
ROPE_FREQS = A_HEAD_DIM // 4
Q_SCALE = A_HEAD_DIM ** -0.5
LOG2_E = 1.4426950408889634

B_HEADS = 4
B_K_WIDTH = D // 2
B_V_WIDTH = D
B_KEY_DIM = B_K_WIDTH // B_HEADS
B_VAL_DIM = B_V_WIDTH // B_HEADS
GATE_RANK = 16
GATE_TEMP = 16.0
CHUNK = 64
K_SCALE = B_KEY_DIM ** -0.5

LANES = 128
MXU_DIM = 256
MOD_ROWS = 16
ROW_TILE = 1024
A_ROW_TILE = 1024
SCAN_TILE = 1024
SUB_ROWS = 256
DEC_ROWS = ROW_TILE // CHUNK
SUM_ROWS = 16
A_BLOCKS_PER_STEP = 8
VMEM_LIMIT = 48 * 1024 * 1024
SCAN_VMEM_LIMIT = 60 * 1024 * 1024


def _params(*sem, vmem_limit=VMEM_LIMIT):
    return pltpu.CompilerParams(dimension_semantics=sem, vmem_limit_bytes=vmem_limit)


def _silu(x):
    return x / (1.0 + jnp.exp(-x))


def _dot(a, b):
    return jnp.dot(a, b, preferred_element_type=F32)


def _dot_nt(a, b):
    return lax.dot_general(a, b, (((1,), (1,)), ((), ())), preferred_element_type=F32)


def _dot_tn(a, b):
    return lax.dot_general(a, b, (((0,), (0,)), ((), ())), preferred_element_type=F32)


def _norm_mod(x, g, m):
    ms = jnp.mean(x * x, axis=-1, keepdims=True)
    y = x * lax.rsqrt(ms + EPS) * g
    return y * (1.0 + m[:, D:2 * D]) + m[:, 0:D]


def _cast_once(src_ref, dst_ref):
    @pl.when((pl.program_id(0) == 0) & (pl.program_id(1) == 0))
    def _():
        def body(i, carry):
            rows = pl.ds(pl.multiple_of(i * LANES, LANES), LANES)
            dst_ref[rows, :] = src_ref[rows, :].astype(dst_ref.dtype)
            return carry
        lax.fori_loop(0, src_ref.shape[0] // LANES, body, 0)


def _mod_kernel(n_tiles, c_ref, w0_ref, b0_ref, w1_ref, b1_ref, o0_ref, o1_ref):
    s = _silu(c_ref[...]).astype(BF16)
    j = pl.program_id(0)

    @pl.when(j < n_tiles)
    def _():
        o0_ref[...] = _dot(s, w0_ref[...].astype(BF16)) + b0_ref[...]

    @pl.when(j >= n_tiles)
    def _():
        o1_ref[...] = _dot(s, w1_ref[...].astype(BF16)) + b1_ref[...]


def _modulation(cvec, w_ada0, b_ada0, w_ada1, b_ada1):
    n = w_ada0.shape[1] // D
    first = lambda j: (0, jnp.minimum(j, n - 1))
    second = lambda j: (0, jnp.maximum(j - n, 0))
    out = jax.ShapeDtypeStruct((MOD_ROWS, n * D), F32)
    return pl.pallas_call(
        functools.partial(_mod_kernel, n),
        grid=(2 * n,),
        in_specs=[
            pl.BlockSpec((MOD_ROWS, D), lambda j: (0, 0)),
            pl.BlockSpec((D, D), first), pl.BlockSpec((1, D), first),
            pl.BlockSpec((D, D), second), pl.BlockSpec((1, D), second),
        ],
        out_specs=(pl.BlockSpec((MOD_ROWS, D), first), pl.BlockSpec((MOD_ROWS, D), second)),
        out_shape=(out, out),
        compiler_params=_params("arbitrary"),
        name="modulation",
    )(cvec, w_ada0, b_ada0.reshape(1, n * D), w_ada1, b_ada1.reshape(1, n * D))


def _rope(t, cos, sin_hi, sin_lo):
    return t * cos + pltpu.roll(t, LANES - ROPE_FREQS, 1) * sin_hi + pltpu.roll(t, ROPE_FREQS, 1) * sin_lo


def _pair_cast_columns(src_ref, dst_ref):
    @pl.when((pl.program_id(0) == 0) & (pl.program_id(1) == 0))
    def _():
        kv_lo, kv_hi = A_WIDTH, A_WIDTH + 2 * A_KV_WIDTH

        def body(i, carry):
            rows = pl.ds(pl.multiple_of(i * LANES, LANES), LANES)
            src = src_ref[rows, :]
            dst_ref[rows, kv_lo:kv_hi] = src[:, kv_lo:kv_hi].astype(BF16)
            for base in (0, kv_hi):
                for p in range(A_GROUP):
                    a = base + p * A_HEAD_DIM
                    b = base + (p + A_GROUP) * A_HEAD_DIM
                    pair = jnp.concatenate([src[:, a:a + A_HEAD_DIM], src[:, b:b + A_HEAD_DIM]], axis=1)
                    dst_ref[rows, base + p * LANES:base + (p + 1) * LANES] = pair.astype(BF16)
            return carry
        lax.fori_loop(0, src_ref.shape[0] // LANES, body, 0)


def _inproj_attn_kernel(rope, x_ref, m_ref, g_ref, w_ref, *rest):
    wq_ref = w_ref.at[:, 0:A_WIDTH]
    wkv_ref = w_ref.at[:, A_WIDTH:A_WIDTH + 2 * A_KV_WIDTH]
    wg_ref = w_ref.at[:, A_WIDTH + 2 * A_KV_WIDTH:]
    if rope:
        cos_ref, shi_ref, slo_ref, qt_ref, k_ref, vt_ref, sg_ref = rest
    else:
        qt_ref, k_ref, vt_ref, sg_ref = rest
    assert x_ref.shape[1] % SUB_ROWS == 0
    for r in range(x_ref.shape[1] // SUB_ROWS):
        rows = slice(r * SUB_ROWS, (r + 1) * SUB_ROWS)
        if rope:
            cos, shi, slo = cos_ref[rows], shi_ref[rows], slo_ref[rows]
        hb = _norm_mod(x_ref[0, rows], g_ref[...], m_ref[0]).astype(BF16)
        q = _dot(hb, wq_ref[...])
        for j in range(A_WIDTH // LANES):
            qj = q[:, j * LANES:(j + 1) * LANES]
            if rope:
                qj = _rope(qj, cos, shi, slo)
            qt_ref[0, j * LANES:(j + 1) * LANES, rows] = (qj * (Q_SCALE * LOG2_E)).T.astype(BF16)
        kv = _dot(hb, wkv_ref[...])
        k = kv[:, 0:A_KV_WIDTH]
        if rope:
            k = _rope(k, cos, shi, slo)
        k_ref[0, rows] = k.astype(BF16)
        vt_ref[0, :, rows] = kv[:, A_KV_WIDTH:].T.astype(BF16)
        g = _dot(hb, wg_ref[...])
        sg_ref[0, rows] = _silu(g).astype(BF16)


def _inproj_attn(x, mod3, norm_g, w_in, tables):
    b, s, _ = x.shape
    tm = min(A_ROW_TILE, s)
    in_specs = [
        pl.BlockSpec((1, tm, D), lambda bi, i: (bi, i, 0)),
        pl.BlockSpec((1, 1, 3 * D), lambda bi, i: (bi, 0, 0)),
        pl.BlockSpec((1, D), lambda bi, i: (0, 0)),
        pl.BlockSpec(w_in.shape, lambda bi, i: (0, 0)),
    ] + [pl.BlockSpec((tm, LANES), lambda bi, i: (i, 0))] * 3
    args = [x, mod3, norm_g.reshape(1, D), w_in, *tables]
    out_shape = (
        jax.ShapeDtypeStruct((b, A_WIDTH, s), BF16),
        jax.ShapeDtypeStruct((b, s, A_KV_WIDTH), BF16),
        jax.ShapeDtypeStruct((b, A_KV_WIDTH, s), BF16),
        jax.ShapeDtypeStruct((b, s, A_WIDTH), BF16),
    )
    out_specs = (
        pl.BlockSpec((1, A_WIDTH, tm), lambda bi, i: (bi, 0, i)),
        pl.BlockSpec((1, tm, A_KV_WIDTH), lambda bi, i: (bi, i, 0)),
        pl.BlockSpec((1, A_KV_WIDTH, tm), lambda bi, i: (bi, 0, i)),
        pl.BlockSpec((1, tm, A_WIDTH), lambda bi, i: (bi, i, 0)),
    )
    return pl.pallas_call(
        functools.partial(_inproj_attn_kernel, True),
        grid=(b, s // tm),
        in_specs=in_specs,
        out_specs=out_specs,
        out_shape=out_shape,
        compiler_params=_params("parallel", "parallel"),
        name="inproj_attn",
    )(*args)


def _attn_block(kwin, vtw, qts, sinks, ok_prev, ok_next):
    n_keys, nq = kwin.shape[0], qts.shape[1]
    lane = lax.broadcasted_iota(jnp.int32, kwin.shape, 1)
    kbd = jnp.concatenate([jnp.where(lane < A_HEAD_DIM, kwin, jnp.zeros_like(kwin)),
                           jnp.where(lane >= A_HEAD_DIM, kwin, jnp.zeros_like(kwin))], axis=0)
    ones = jnp.where(lax.broadcasted_iota(jnp.int32, (SUM_ROWS, n_keys), 0) == 0, 1.0, 0.0).astype(BF16)

    st = _dot(kbd, qts).astype(BF16)
    neg = jnp.asarray(NEG_INF, BF16)
    outs = []
    for hh in range(2):
        sh = st[hh * n_keys:(hh + 1) * n_keys]
        if ok_prev is not None:
            parts = [jnp.where(ok_prev, sh[0:BLOCK], neg),
                     sh[BLOCK:2 * BLOCK],
                     jnp.where(ok_next, sh[2 * BLOCK:3 * BLOCK], neg),
                     sh[3 * BLOCK:]]
        else:
            parts = [sh]
        mx = functools.reduce(jnp.maximum, [jnp.max(t, axis=0, keepdims=True) for t in parts])
        mxb = jnp.maximum(mx.astype(F32), sinks[hh]).astype(BF16)
        mx = mxb.astype(F32)
        probs = jnp.concatenate([jnp.exp2(t - mxb) for t in parts], axis=0)
        vt_h = jnp.concatenate([vtw[hh * A_HEAD_DIM:(hh + 1) * A_HEAD_DIM], ones], axis=0)
        ot = _dot(vt_h, probs)
        den = ot[A_HEAD_DIM:A_HEAD_DIM + 1] + jnp.exp2(sinks[hh] - mx)
        outs.append(ot[0:A_HEAD_DIM] * (1.0 / den))
    return jnp.concatenate(outs, axis=0)


def _pair_cast_rows(src_ref, dst_ref):
    @pl.when((pl.program_id(0) == 0) & (pl.program_id(1) == 0))
    def _():
        for p in range(A_GROUP):
            for half, h in enumerate((p, p + A_GROUP)):
                lo = p * LANES + half * A_HEAD_DIM
                dst_ref[lo:lo + A_HEAD_DIM, :] = src_ref[h * A_HEAD_DIM:(h + 1) * A_HEAD_DIM, :].astype(BF16)


def _attn_kernel(local, n_steps, sink_ref, qt_ref, *rest):
    n_blk = rest[-1].shape[1] // BLOCK
    if local:
        (kp_ref, kc_ref, kn_ref, vp_ref, vc_ref, vn_ref, kx_ref, vx_ref,
         sg_ref, x_ref, m_ref, w_ref, o_ref) = rest
        last = slice((n_blk - 1) * BLOCK, n_blk * BLOCK)
        k_blocks = ([kp_ref[0, last]] + [kc_ref[0, i * BLOCK:(i + 1) * BLOCK] for i in range(n_blk)]
                    + [kn_ref[0, 0:BLOCK]])
        v_blocks = ([vp_ref[0, :, last]] + [vc_ref[0, :, i * BLOCK:(i + 1) * BLOCK] for i in range(n_blk)]
                    + [vn_ref[0, :, 0:BLOCK]])
    else:
        kx_ref, vx_ref, sg_ref, x_ref, m_ref, w_ref, o_ref = rest
    step = pl.program_id(1)
    nq = A_GROUP * BLOCK
    chunk_of = lax.broadcasted_iota(jnp.int32, (1, nq), 1) // BLOCK
    sinks = []
    for hh in range(2):
        sk = jnp.full((1, nq), sink_ref[hh] * LOG2_E, F32)
        for c in range(1, A_GROUP):
            sk = jnp.where(chunk_of == c, sink_ref[2 * c + hh] * LOG2_E, sk)
        sinks.append(sk)
    if local:
        kj = lax.broadcasted_iota(jnp.int32, (BLOCK, nq), 0)
        qi = lax.broadcasted_iota(jnp.int32, (BLOCK, nq), 1) % BLOCK

    rows_out = []
    for blk in range(n_blk):
        qts = jnp.concatenate([qt_ref[0, c * LANES:(c + 1) * LANES, blk * BLOCK:(blk + 1) * BLOCK]
                               for c in range(A_GROUP)], axis=1)
        if local:
            kwin = jnp.concatenate(k_blocks[blk:blk + 3] + [kx_ref[0]], axis=0)
            vtw = jnp.concatenate(v_blocks[blk:blk + 3] + [vx_ref[0]], axis=1)
            ok_prev = (kj >= qi) & (step > 0) if blk == 0 else (kj >= qi)
            ok_next = (kj <= qi) & (step < n_steps - 1) if blk == n_blk - 1 else (kj <= qi)
        else:
            kwin, vtw, ok_prev, ok_next = kx_ref[0], vx_ref[0], None, None
        ot = _attn_block(kwin, vtw, qts, sinks, ok_prev, ok_next)
        rows = slice(blk * BLOCK, (blk + 1) * BLOCK)
        outs = []
        for c in range(A_GROUP):
            sg = sg_ref[0, rows, c * LANES:(c + 1) * LANES].astype(F32)
            outs.append((ot[:, c * BLOCK:(c + 1) * BLOCK].T * sg).astype(BF16))
        rows_out.append(jnp.concatenate(outs, axis=1))
    y = _dot(jnp.concatenate(rows_out, axis=0), w_ref[...])
    gate = m_ref[0][:, 2 * D:3 * D]
    o_ref[0] = x_ref[0] + gate * y


def _attention(sink, qt, k, vt, kx, vxt, sg, x, mod3, w_out):
    b, s, _ = sg.shape
    rows = A_BLOCKS_PER_STEP * BLOCK
    assert s % rows == 0
    ns = s // rows
    n_ctx = kx.shape[1]
    blk = lambda w: pl.BlockSpec((1, rows, w), lambda bi, i: (bi, i, 0))
    blk_t = lambda w: pl.BlockSpec((1, w, rows), lambda bi, i: (bi, 0, i))
    lo = lambda i: jnp.maximum(i - 1, 0)
    hi = lambda i: jnp.minimum(i + 1, ns - 1)
    in_specs = [pl.BlockSpec(memory_space=pltpu.SMEM), blk_t(A_WIDTH),
                pl.BlockSpec((1, rows, A_KV_WIDTH), lambda bi, i: (bi, lo(i), 0)),
                blk(A_KV_WIDTH),
                pl.BlockSpec((1, rows, A_KV_WIDTH), lambda bi, i: (bi, hi(i), 0)),
                pl.BlockSpec((1, A_KV_WIDTH, rows), lambda bi, i: (bi, 0, lo(i))),
                blk_t(A_KV_WIDTH),
                pl.BlockSpec((1, A_KV_WIDTH, rows), lambda bi, i: (bi, 0, hi(i))),
                pl.BlockSpec((1, n_ctx, A_KV_WIDTH), lambda bi, i: (bi, 0, 0)),
                pl.BlockSpec((1, A_KV_WIDTH, n_ctx), lambda bi, i: (bi, 0, 0)),
                blk(A_WIDTH), blk(D),
                pl.BlockSpec((1, 1, 3 * D), lambda bi, i: (bi, 0, 0)),
                pl.BlockSpec((A_WIDTH, D), lambda bi, i: (0, 0))]
    return pl.pallas_call(
        functools.partial(_attn_kernel, True, ns),
        grid=(b, ns),
        in_specs=in_specs,
        out_specs=blk(D),
        out_shape=jax.ShapeDtypeStruct((b, s, D), F32),
        compiler_params=_params("parallel", "parallel"),
        name="attn_local",
    )(sink, qt, k, k, k, vt, vt, vt, kx, vxt, sg, x, mod3, w_out)


def _ctx_attn_kernel(sink_ref, x_ref, m_ref, g_ref, win_ref, wout_ref, k_ref, vt_ref, o_ref,
                     win_bf_ref, wout_bf_ref, qt_scr, sg_scr):
    _pair_cast_columns(win_ref, win_bf_ref)
    _pair_cast_rows(wout_ref, wout_bf_ref)
    _inproj_attn_kernel(False, x_ref, m_ref, g_ref, win_bf_ref, qt_scr, k_ref, vt_ref, sg_scr)
    _attn_kernel(False, 1, sink_ref, qt_scr, k_ref, vt_ref, sg_scr, x_ref, m_ref, wout_bf_ref, o_ref)


def _ctx_attention(sink, xc, mod3, mod_row, norm_g, w_in, w_out):
    b, n, _ = xc.shape
    const = lambda shape, **kw: pl.BlockSpec(shape, lambda bi, i: (0,) * len(shape), **kw)
    return pl.pallas_call(
        _ctx_attn_kernel,
        grid=(b, 1),
        in_specs=[pl.BlockSpec(memory_space=pltpu.SMEM),
                  pl.BlockSpec((1, n, D), lambda bi, i: (bi, 0, 0)),
                  pl.BlockSpec((1, 1, 3 * D), lambda bi, i: (mod_row, 0, 0)),
                  const((1, D)),
                  const(w_in.shape, pipeline_mode=pl.Buffered(1)),
                  const(w_out.shape, pipeline_mode=pl.Buffered(1))],
        out_specs=(pl.BlockSpec((1, n, A_KV_WIDTH), lambda bi, i: (bi, 0, 0)),
                   pl.BlockSpec((1, A_KV_WIDTH, n), lambda bi, i: (bi, 0, 0)),
                   pl.BlockSpec((1, n, D), lambda bi, i: (bi, 0, 0)),
                   const(w_in.shape), const(w_out.shape)),
        out_shape=(jax.ShapeDtypeStruct((b, n, A_KV_WIDTH), BF16),
                   jax.ShapeDtypeStruct((b, A_KV_WIDTH, n), BF16),
                   jax.ShapeDtypeStruct((b, n, D), F32),
                   jax.ShapeDtypeStruct(w_in.shape, BF16), jax.ShapeDtypeStruct(w_out.shape, BF16)),
        scratch_shapes=[pltpu.VMEM((1, A_WIDTH, n), BF16), pltpu.VMEM((1, n, A_WIDTH), BF16)],
        compiler_params=_params("arbitrary", "arbitrary"),
        name="ctx_attn",
    )(sink, xc, mod3, norm_g.reshape(1, D), w_in, w_out)


def _chunk_cumsum(x, tri):
    return _dot(tri, x.astype(BF16))


def _inproj_gla_kernel(chunk, with_q, x_ref, m_ref, g_ref, w_in_ref, wa1_ref, wa2_ref, ba_ref, tri_ref, *outs):
    if with_q:
        w_ref = w_in_ref
        per_dir = (outs[0:3], outs[3:6])
        v_ref, sg_ref = outs[6:]
    else:
        w_ref = outs[2]
        _cast_once(w_in_ref, w_ref)
        per_dir = ((outs[0],), (outs[1],))
    tm = x_ref.shape[1]
    hb = _norm_mod(x_ref[0], g_ref[...], m_ref[0]).astype(BF16)
    r = _dot(hb, wa1_ref[...])
    z = _dot(r.astype(BF16), wa2_ref[...]) + ba_ref[...]
    e = jnp.exp2(jnp.abs(z) * -LOG2_E)
    la = (jnp.minimum(z, 0.0) * LOG2_E - jnp.log2(1.0 + e)) * (1.0 / GATE_TEMP)
    k = _dot(hb, w_ref[:, B_K_WIDTH:2 * B_K_WIDTH])
    if with_q:
        q = _dot(hb, w_ref[:, 0:B_K_WIDTH]) * K_SCALE
    n_chunks = tm // chunk
    k_ends = []
    for reverse in (False, True):
        lad = la[:, B_K_WIDTH:] if reverse else la[:, 0:B_K_WIDTH]
        refs = per_dir[1] if reverse else per_dir[0]
        tots, kis, k_end = [], [], None
        tri = tri_ref[1 if reverse else 0]
        tb = tri.shape[0]
        cum_all = jnp.concatenate([_chunk_cumsum(lad[r * tb:(r + 1) * tb], tri) for r in range(tm // tb)], axis=0)
        for c in range(n_chunks):
            rows = slice(c * chunk, (c + 1) * chunk)
            cum = cum_all[rows]
            tot = cum[0:1] if reverse else cum[chunk - 1:chunk]
            if not with_q:
                k_end = (k[rows] * jnp.exp2(tot - cum)).astype(BF16)
                continue
            refs[0][0, rows] = (q[rows] * jnp.exp2(cum)).astype(BF16)
            kis.append(k[rows] * jnp.exp2(-cum))
            tots.append(tot)
            if c % 2 == 1:
                pair = slice((c - 1) * chunk, (c + 1) * chunk)
                refs[1][0, :, pair] = jnp.concatenate(kis[-2:], axis=0).T.astype(BF16)
        if with_q:
            pad = [jnp.zeros((DEC_ROWS - n_chunks, B_K_WIDTH), F32)] if n_chunks < DEC_ROWS else []
            refs[2][0, 0] = jnp.exp2(jnp.concatenate(tots + pad, axis=0))
        else:
            k_ends.append(k_end)
    if with_q:
        sg_ref[0] = _silu(_dot(hb, w_ref[:, 2 * B_K_WIDTH + B_V_WIDTH:])).astype(BF16)
    v = _dot(hb, w_ref[:, 2 * B_K_WIDTH:2 * B_K_WIDTH + B_V_WIDTH]).astype(BF16)
    if with_q:
        v_ref[0] = v
    else:
        for refs, k_end in zip(per_dir, k_ends):
            for h in range(B_HEADS):
                ks = slice(h * B_KEY_DIM, (h + 1) * B_KEY_DIM)
                refs[0][0, h] = _dot_tn(k_end[:, ks], v[:, h * B_VAL_DIM:(h + 1) * B_VAL_DIM])


def _inproj_gla(x, mod3, mod_row, norm_g, w, wa1, wa2, ba, chunk, with_q):
    b, s, _ = x.shape
    tm = min(ROW_TILE, s)
    nt = s // tm
    cpt = tm // chunk
    tb = max(chunk, MXU_DIM)
    assert tm % tb == 0 and tb % chunk == 0
    if mod_row is None:
        mod_map = lambda bi, i: (bi, 0, 0)
    else:
        mod_map = lambda bi, i: (mod_row, 0, 0)
    const = lambda shape, **kw: pl.BlockSpec(shape, lambda bi, i: (0,) * len(shape), **kw)
    in_specs = [
        pl.BlockSpec((1, tm, D), lambda bi, i: (bi, i, 0)),
        pl.BlockSpec((1, 1, 3 * D), mod_map),
        const((1, D)), const(w.shape, **({} if with_q else {"pipeline_mode": pl.Buffered(1)})),
        const(wa1.shape), const(wa2.shape), const(ba.shape),
        const((2, tb, tb)),
    ]
    t_idx = np.arange(tb)
    same = (t_idx[:, None] // chunk) == (t_idx[None, :] // chunk)
    lower = same & (t_idx[None, :] <= t_idx[:, None])
    tri = jnp.asarray(np.stack([lower, lower.T]), dtype=BF16)
    rows = lambda w_: (jax.ShapeDtypeStruct((b, s, w_), BF16),
                       pl.BlockSpec((1, tm, w_), lambda bi, i: (bi, i, 0)))
    cols = (jax.ShapeDtypeStruct((b, B_K_WIDTH, s), BF16),
            pl.BlockSpec((1, B_K_WIDTH, tm), lambda bi, i: (bi, 0, i)))
    decs = (jax.ShapeDtypeStruct((b, nt, DEC_ROWS, B_K_WIDTH), F32),
            pl.BlockSpec((1, 1, DEC_ROWS, B_K_WIDTH), lambda bi, i: (bi, i, 0, 0)))
    if with_q:
        assert cpt <= DEC_ROWS and cpt % 2 == 0
        outs = [rows(B_K_WIDTH), cols, decs] * 2 + [rows(B_V_WIDTH)] * 2
    else:
        assert nt == 1 and cpt == 1
        state = (jax.ShapeDtypeStruct((b, B_HEADS, B_KEY_DIM, B_VAL_DIM), F32),
                 pl.BlockSpec((1, B_HEADS, B_KEY_DIM, B_VAL_DIM), lambda bi, i: (bi, 0, 0, 0)))
        outs = [state] * 2 + [(jax.ShapeDtypeStruct(w.shape, BF16), const(w.shape))]
    return pl.pallas_call(
        functools.partial(_inproj_gla_kernel, chunk, with_q),
        grid=(b, nt),
        in_specs=in_specs,
        out_specs=tuple(o[1] for o in outs),
        out_shape=tuple(o[0] for o in outs),
        compiler_params=_params("parallel", "parallel") if with_q else _params("arbitrary", "arbitrary"),
        name="inproj_gla" if with_q else "inproj_gla_ctx",
    )(x, mod3, norm_g.reshape(1, D), w, wa1, wa2, ba, tri)


def _scan_tile(reverse, tile, qd_ref, kit_ref, dec_ref, v_ref, st_ref):
    pair_rows = 2 * CHUNK
    ti = lax.broadcasted_iota(jnp.int32, (pair_rows, pair_rows), 0)
    si = lax.broadcasted_iota(jnp.int32, (pair_rows, pair_rows), 1)
    same = (ti // CHUNK) == (si // CHUNK)
    if reverse:
        use_inv = same & (ti <= si)
        use_end = (ti < CHUNK) & (si >= CHUNK)
        second_row = lax.broadcasted_iota(jnp.int32, (pair_rows, B_KEY_DIM), 0) < CHUNK
        first_col = lax.broadcasted_iota(jnp.int32, (B_KEY_DIM, pair_rows), 1) >= CHUNK
    else:
        use_inv = same & (ti >= si)
        use_end = (ti >= CHUNK) & (si < CHUNK)
        second_row = lax.broadcasted_iota(jnp.int32, (pair_rows, B_KEY_DIM), 0) >= CHUNK
        first_col = lax.broadcasted_iota(jnp.int32, (B_KEY_DIM, pair_rows), 1) < CHUNK
    n_pairs = SCAN_TILE // pair_rows
    order = range(n_pairs - 1, -1, -1) if reverse else range(n_pairs)
    n_dec = dec_ref.shape[1] * DEC_ROWS
    dec = dec_ref[0].reshape(n_dec, B_K_WIDTH)
    dect = jnp.concatenate([dec, jnp.zeros((LANES - n_dec, B_K_WIDTH), F32)], axis=0).T
    heads = []
    for h in range(B_HEADS):
        ks = slice(h * B_KEY_DIM, (h + 1) * B_KEY_DIM)
        vs = slice(h * B_VAL_DIM, (h + 1) * B_VAL_DIM)
        st = st_ref[h]
        o_rows = [None] * n_pairs
        for p in order:
            rows = slice(p * pair_rows, (p + 1) * pair_rows)
            c_first, c_second = (2 * p + 1, 2 * p) if reverse else (2 * p, 2 * p + 1)
            qd = qd_ref[0, rows, ks]
            kit = kit_ref[0, ks, rows]
            vh = v_ref[0, pl.ds(pl.multiple_of(tile * SCAN_TILE + p * pair_rows, pair_rows), pair_rows), vs]
            d_first, d_second = dect[ks, c_first:c_first + 1], dect[ks, c_second:c_second + 1]
            ki32 = kit.astype(F32)
            s2 = _dot(qd, jnp.concatenate([kit, (ki32 * d_first).astype(BF16)], axis=1))
            a = jnp.where(use_inv, s2[:, 0:pair_rows], jnp.where(use_end, s2[:, pair_rows:], 0.0)).astype(BF16)
            q_pair = jnp.where(second_row, qd.astype(F32) * dec[c_first:c_first + 1, ks], qd.astype(F32))
            k_pair = ki32 * jnp.where(first_col, d_first * d_second, d_second)
            o_rows[p] = _dot(a, vh) + _dot(q_pair.astype(BF16), st.astype(BF16))
            st = st * (d_first * d_second) + _dot(k_pair.astype(BF16), vh)
        st_ref[h] = st
        heads.append(jnp.concatenate(o_rows, axis=0))
    return jnp.concatenate(heads, axis=1)


def _gla_scan_kernel(nt, qdf_ref, kitf_ref, decf_ref, qdb_ref, kitb_ref, decb_ref,
                     v_ref, sf_ref, sb_ref, sg_ref, x_ref, m_ref, hn_ref, w32_ref, fn_ref,
                     o_ref, st_ref, of_ref, w_ref):
    _cast_once(w32_ref, w_ref)
    j = pl.program_id(1)

    @pl.when(j == 0)
    def _():
        st_ref[...] = sf_ref[0]

    @pl.when(j == nt)
    def _():
        st_ref[...] = sb_ref[0]

    @pl.when(j < nt)
    def _():
        of_ref[j] = _scan_tile(False, j, qdf_ref, kitf_ref, decf_ref, v_ref, st_ref)

    @pl.when(j >= nt)
    def _():
        tile = 2 * nt - 1 - j
        o = _scan_tile(True, tile, qdb_ref, kitb_ref, decb_ref, v_ref, st_ref) + of_ref[tile]
        normed = []
        for h in range(B_HEADS):
            oh = o[:, h * B_VAL_DIM:(h + 1) * B_VAL_DIM]
            normed.append(oh * lax.rsqrt(jnp.mean(oh * oh, axis=-1, keepdims=True) + EPS))
        of = jnp.concatenate(normed, axis=1) * hn_ref[...]
        y = _dot((of * sg_ref[0].astype(F32)).astype(BF16), w_ref[...])
        xn = x_ref[0] + m_ref[0][:, 2 * D:3 * D] * y
        ms = jnp.mean(xn * xn, axis=-1, keepdims=True)
        o_ref[0] = xn * lax.rsqrt(ms + EPS) * fn_ref[...]


def _gla_scan(fwd, bwd, v, s_f, s_b, sg, x, mod3, head_g, w_out, final_g):
    b, s, _ = v.shape
    assert SCAN_TILE % ROW_TILE == 0 and ROW_TILE // CHUNK == DEC_ROWS
    dec_blocks = SCAN_TILE // ROW_TILE
    nt = s // SCAN_TILE
    t_fwd = lambda j: jnp.minimum(j, nt - 1)
    t_bwd = lambda j: 2 * nt - 1 - jnp.maximum(j, nt)

    def direction(t):
        rows = pl.BlockSpec((1, SCAN_TILE, B_K_WIDTH), lambda bi, j: (bi, t(j), 0))
        cols = pl.BlockSpec((1, B_K_WIDTH, SCAN_TILE), lambda bi, j: (bi, 0, t(j)))
        decs = pl.BlockSpec((1, dec_blocks, DEC_ROWS, B_K_WIDTH), lambda bi, j: (bi, t(j), 0, 0))
        return [rows, cols, decs]

    const = lambda shape: pl.BlockSpec(shape, lambda bi, j: (0,) * len(shape))
    state = pl.BlockSpec((1, B_HEADS, B_KEY_DIM, B_VAL_DIM), lambda bi, j: (bi, 0, 0, 0))
    out_rows = lambda w_: pl.BlockSpec((1, SCAN_TILE, w_), lambda bi, j: (bi, t_bwd(j), 0))
    in_specs = direction(t_fwd) + direction(t_bwd) + [
        pl.BlockSpec((1, s, B_V_WIDTH), lambda bi, j: (bi, 0, 0)),
        state, state, out_rows(B_V_WIDTH), out_rows(D),
        pl.BlockSpec((1, 1, 3 * D), lambda bi, j: (bi, 0, 0)),
        const((1, B_V_WIDTH)),
        pl.BlockSpec((B_V_WIDTH, D), lambda bi, j: (0, 0), pipeline_mode=pl.Buffered(1)),
        const((1, D))]
    return pl.pallas_call(
        functools.partial(_gla_scan_kernel, nt),
        grid=(b, 2 * nt),
        in_specs=in_specs,
        out_specs=out_rows(D),
        out_shape=jax.ShapeDtypeStruct((b, s, D), F32),
        scratch_shapes=[pltpu.VMEM((B_HEADS, B_KEY_DIM, B_VAL_DIM), F32),
                        pltpu.VMEM((nt, SCAN_TILE, B_V_WIDTH), F32),
                        pltpu.VMEM((B_V_WIDTH, D), BF16)],
        compiler_params=_params("arbitrary", "arbitrary", vmem_limit=SCAN_VMEM_LIMIT),
        name="gla_scan",
    )(*fwd, *bwd, v, s_f, s_b, sg, x, mod3, head_g.reshape(1, B_V_WIDTH), w_out, final_g.reshape(1, D))


def _rope_tables(n_tokens):
    rows_n = n_tokens // GRID_W
    row = np.repeat(np.arange(rows_n, dtype=np.float32), GRID_W)
    col = np.tile(np.arange(GRID_W, dtype=np.float32), rows_n)
    inv_freq = (np.float32(ROPE_BASE) ** (-np.arange(ROPE_FREQS, dtype=np.float32) / np.float32(ROPE_FREQS)))
    inv_freq = inv_freq.astype(np.float32)
    ang = np.stack([row[:, None] * inv_freq, col[:, None] * inv_freq], axis=1)
    cos, sin = np.cos(ang).astype(np.float32), np.sin(ang).astype(np.float32)
    zero = np.zeros_like(sin)
    tile = lambda t: np.tile(t.reshape(n_tokens, A_HEAD_DIM), (1, LANES // A_HEAD_DIM))
    return (tile(np.stack([cos, cos], axis=2)),
            tile(np.stack([-sin, zero], axis=2)),
            tile(np.stack([zero, sin], axis=2)))


def _pair_heads(t, axis):
    shape = t.shape
    t = t.reshape(shape[:axis] + (A_KV_HEADS, A_GROUP, -1) + shape[axis + 1:])
    return jnp.swapaxes(t, axis, axis + 1).reshape(shape)


def kernel(x, c, ctx, c_ctx, l0_norm_g, l0_w_ada, l0_b_ada, l0_w_in, l0_sink, l0_w_out, l1_norm_g, l1_w_ada, l1_b_ada, l1_w_in, l1_wa1_f, l1_wa2_f, l1_ba_f, l1_wa1_b, l1_wa2_b, l1_ba_b, l1_head_norm_g, l1_w_out, final_norm_g):
    b, s, _ = x.shape
    ctx_row = b

    cvec = jnp.concatenate([c, c_ctx[None, :], jnp.zeros((MOD_ROWS - b - 1, D), F32)], axis=0)
    mod0, mod1 = (m.reshape(MOD_ROWS, 1, 3 * D) for m in _modulation(cvec, l0_w_ada, l0_b_ada, l1_w_ada, l1_b_ada))

    sink = _pair_heads(l0_sink.astype(F32), 0)
    kc, vct, xc1, w0_in, w0_out = _ctx_attention(sink, ctx, mod0, ctx_row, l0_norm_g, l0_w_in, l0_w_out)
    qt, k, vt, sg = _inproj_attn(x, mod0, l0_norm_g, w0_in, _rope_tables(s))
    x1 = _attention(sink, qt, k, vt, kc, vct, sg, x, mod0, w0_out)

    wa1 = jnp.concatenate([l1_wa1_f, l1_wa1_b, jnp.zeros((D, LANES - 2 * GATE_RANK), F32)], axis=1).astype(BF16)
    wa2 = jnp.zeros((LANES, 2 * B_K_WIDTH), F32)
    wa2 = wa2.at[0:GATE_RANK, 0:B_K_WIDTH].set(l1_wa2_f)
    wa2 = wa2.at[GATE_RANK:2 * GATE_RANK, B_K_WIDTH:].set(l1_wa2_b).astype(BF16)
    ba = jnp.concatenate([l1_ba_f, l1_ba_b]).reshape(1, 2 * B_K_WIDTH)
    s_f, s_b, w1 = _inproj_gla(xc1, mod1, ctx_row, l1_norm_g, l1_w_in, wa1, wa2, ba, ctx.shape[1], False)
    outs = _inproj_gla(x1, mod1, None, l1_norm_g, w1, wa1, wa2, ba, CHUNK, True)
    v1, sg1 = outs[6:]
    return _gla_scan(outs[0:3], outs[3:6], v1, s_f, s_b, sg1, x1, mod1,
                     l1_head_norm_g, l1_w_out, final_norm_g)
```

```python
import functools

import jax
import jax.numpy as jnp
import numpy as np
from jax import lax
from jax.experimental import pallas as pl
from jax.experimental.pallas import tpu as pltpu

F32 = jnp.float32
BF16 = jnp.bfloat16

D = 1024
GRID_W = 64
EPS = 1e-6
NEG_INF = -1e30

A_HEADS = 16
A_KV_HEADS = 2
A_GROUP = A_HEADS // A_KV_HEADS
A_HEAD_DIM = 64
A_WIDTH = A_HEADS * A_HEAD_DIM
A_KV_WIDTH = A_KV_HEADS * A_HEAD_DIM
BLOCK = 128
ROPE_BASE = 10000.0
ROPE_FREQS = A_HEAD_DIM // 4
Q_SCALE = A_HEAD_DIM ** -0.5
LOG2_E = 1.4426950408889634

B_HEADS = 4
B_K_WIDTH = D // 2
B_V_WIDTH = D
B_KEY_DIM = B_K_WIDTH // B_HEADS
B_VAL_DIM = B_V_WIDTH // B_HEADS
GATE_RANK = 16
GATE_TEMP = 16.0
CHUNK = 64
K_SCALE = B_KEY_DIM ** -0.5

LANES = 128
MXU_DIM = 256
MOD_ROWS = 16
ROW_TILE = 1024
A_ROW_TILE = 1024
SCAN_TILE = 1024
SUB_ROWS = 256
DEC_ROWS = ROW_TILE // CHUNK
SUM_ROWS = 16
A_BLOCKS_PER_STEP = 8
VMEM_LIMIT = 48 * 1024 * 1024
SCAN_VMEM_LIMIT = 60 * 1024 * 1024


def _params(*sem, vmem_limit=VMEM_LIMIT):
    return pltpu.CompilerParams(dimension_semantics=sem, vmem_limit_bytes=vmem_limit)


def _silu(x):
    return x / (1.0 + jnp.exp(-x))


def _dot(a, b):
    return jnp.dot(a, b, preferred_element_type=F32)


def _dot_nt(a, b):
    return lax.dot_general(a, b, (((1,), (1,)), ((), ())), preferred_element_type=F32)


def _dot_tn(a, b):
    return lax.dot_general(a, b, (((0,), (0,)), ((), ())), preferred_element_type=F32)


def _norm_mod(x, g, m):
    ms = jnp.mean(x * x, axis=-1, keepdims=True)
    y = x * lax.rsqrt(ms + EPS) * g
    return y * (1.0 + m[:, D:2 * D]) + m[:, 0:D]


def _cast_once(src_ref, dst_ref):
    @pl.when((pl.program_id(0) == 0) & (pl.program_id(1) == 0))
    def _():
        def body(i, carry):
            rows = pl.ds(pl.multiple_of(i * LANES, LANES), LANES)
            dst_ref[rows, :] = src_ref[rows, :].astype(dst_ref.dtype)
            return carry
        lax.fori_loop(0, src_ref.shape[0] // LANES, body, 0)


def _mod_kernel(n_tiles, c_ref, w0_ref, b0_ref, w1_ref, b1_ref, o0_ref, o1_ref):
    s = _silu(c_ref[...]).astype(BF16)
    j = pl.program_id(0)

    @pl.when(j < n_tiles)
    def _():
        o0_ref[...] = _dot(s, w0_ref[...].astype(BF16)) + b0_ref[...]

    @pl.when(j >= n_tiles)
    def _():
        o1_ref[...] = _dot(s, w1_ref[...].astype(BF16)) + b1_ref[...]


def _modulation(cvec, w_ada0, b_ada0, w_ada1, b_ada1):
    n = w_ada0.shape[1] // D
    first = lambda j: (0, jnp.minimum(j, n - 1))
    second = lambda j: (0, jnp.maximum(j - n, 0))
    out = jax.ShapeDtypeStruct((MOD_ROWS, n * D), F32)
    return pl.pallas_call(
        functools.partial(_mod_kernel, n),
        grid=(2 * n,),
        in_specs=[
            pl.BlockSpec((MOD_ROWS, D), lambda j: (0, 0)),
            pl.BlockSpec((D, D), first), pl.BlockSpec((1, D), first),
            pl.BlockSpec((D, D), second), pl.BlockSpec((1, D), second),
        ],
        out_specs=(pl.BlockSpec((MOD_ROWS, D), first), pl.BlockSpec((MOD_ROWS, D), second)),
        out_shape=(out, out),
        compiler_params=_params("arbitrary"),
        name="modulation",
    )(cvec, w_ada0, b_ada0.reshape(1, n * D), w_ada1, b_ada1.reshape(1, n * D))


def _rope(t, cos, sin_hi, sin_lo):
    return t * cos + pltpu.roll(t, LANES - ROPE_FREQS, 1) * sin_hi + pltpu.roll(t, ROPE_FREQS, 1) * sin_lo


def _pair_cast_columns(src_ref, dst_ref):
    @pl.when((pl.program_id(0) == 0) & (pl.program_id(1) == 0))
    def _():
        kv_lo, kv_hi = A_WIDTH, A_WIDTH + 2 * A_KV_WIDTH

        def body(i, carry):
            rows = pl.ds(pl.multiple_of(i * LANES, LANES), LANES)
            src = src_ref[rows, :]
            dst_ref[rows, kv_lo:kv_hi] = src[:, kv_lo:kv_hi].astype(BF16)
            for base in (0, kv_hi):
                for p in range(A_GROUP):
                    a = base + p * A_HEAD_DIM
                    b = base + (p + A_GROUP) * A_HEAD_DIM
                    pair = jnp.concatenate([src[:, a:a + A_HEAD_DIM], src[:, b:b + A_HEAD_DIM]], axis=1)
                    dst_ref[rows, base + p * LANES:base + (p + 1) * LANES] = pair.astype(BF16)
            return carry
        lax.fori_loop(0, src_ref.shape[0] // LANES, body, 0)


def _inproj_attn_kernel(rope, x_ref, m_ref, g_ref, w_ref, *rest):
    wq_ref = w_ref.at[:, 0:A_WIDTH]
    wkv_ref = w_ref.at[:, A_WIDTH:A_WIDTH + 2 * A_KV_WIDTH]
    wg_ref = w_ref.at[:, A_WIDTH + 2 * A_KV_WIDTH:]
    if rope:
        cos_ref, shi_ref, slo_ref, qt_ref, k_ref, vt_ref, sg_ref = rest
    else:
        qt_ref, k_ref, vt_ref, sg_ref = rest
    assert x_ref.shape[1] % SUB_ROWS == 0
    for r in range(x_ref.shape[1] // SUB_ROWS):
        rows = slice(r * SUB_ROWS, (r + 1) * SUB_ROWS)
        if rope:
            cos, shi, slo = cos_ref[rows], shi_ref[rows], slo_ref[rows]
        hb = _norm_mod(x_ref[0, rows], g_ref[...], m_ref[0]).astype(BF16)
        q = _dot(hb, wq_ref[...])
        for j in range(A_WIDTH // LANES):
            qj = q[:, j * LANES:(j + 1) * LANES]
            if rope:
                qj = _rope(qj, cos, shi, slo)
            qt_ref[0, j * LANES:(j + 1) * LANES, rows] = (qj * (Q_SCALE * LOG2_E)).T.astype(BF16)
        kv = _dot(hb, wkv_ref[...])
        k = kv[:, 0:A_KV_WIDTH]
        if rope:
            k = _rope(k, cos, shi, slo)
        k_ref[0, rows] = k.astype(BF16)
        vt_ref[0, :, rows] = kv[:, A_KV_WIDTH:].T.astype(BF16)
        g = _dot(hb, wg_ref[...])
        sg_ref[0, rows] = _silu(g).astype(BF16)


def _inproj_attn(x, mod3, norm_g, w_in, tables):
    b, s, _ = x.shape
    tm = min(A_ROW_TILE, s)
    in_specs = [
        pl.BlockSpec((1, tm, D), lambda bi, i: (bi, i, 0)),
        pl.BlockSpec((1, 1, 3 * D), lambda bi, i: (bi, 0, 0)),
        pl.BlockSpec((1, D), lambda bi, i: (0, 0)),
        pl.BlockSpec(w_in.shape, lambda bi, i: (0, 0)),
    ] + [pl.BlockSpec((tm, LANES), lambda bi, i: (i, 0))] * 3
    args = [x, mod3, norm_g.reshape(1, D), w_in, *tables]
    out_shape = (
        jax.ShapeDtypeStruct((b, A_WIDTH, s), BF16),
        jax.ShapeDtypeStruct((b, s, A_KV_WIDTH), BF16),
        jax.ShapeDtypeStruct((b, A_KV_WIDTH, s), BF16),
        jax.ShapeDtypeStruct((b, s, A_WIDTH), BF16),
    )
    out_specs = (
        pl.BlockSpec((1, A_WIDTH, tm), lambda bi, i: (bi, 0, i)),
        pl.BlockSpec((1, tm, A_KV_WIDTH), lambda bi, i: (bi, i, 0)),
        pl.BlockSpec((1, A_KV_WIDTH, tm), lambda bi, i: (bi, 0, i)),
        pl.BlockSpec((1, tm, A_WIDTH), lambda bi, i: (bi, i, 0)),
    )
    return pl.pallas_call(
        functools.partial(_inproj_attn_kernel, True),
        grid=(b, s // tm),
        in_specs=in_specs,
        out_specs=out_specs,
        out_shape=out_shape,
        compiler_params=_params("parallel", "parallel"),
        name="inproj_attn",
    )(*args)


def _attn_block(kwin, vtw, qts, sinks, ok_prev, ok_next):
    n_keys, nq = kwin.shape[0], qts.shape[1]
    lane = lax.broadcasted_iota(jnp.int32, kwin.shape, 1)
    kbd = jnp.concatenate([jnp.where(lane < A_HEAD_DIM, kwin, jnp.zeros_like(kwin)),
                           jnp.where(lane >= A_HEAD_DIM, kwin, jnp.zeros_like(kwin))], axis=0)
    ones = jnp.where(lax.broadcasted_iota(jnp.int32, (SUM_ROWS, n_keys), 0) == 0, 1.0, 0.0).astype(BF16)

    st = _dot(kbd, qts).astype(BF16)
    neg = jnp.asarray(NEG_INF, BF16)
    outs = []
    for hh in range(2):
        sh = st[hh * n_keys:(hh + 1) * n_keys]
        if ok_prev is not None:
            parts = [jnp.where(ok_prev, sh[0:BLOCK], neg),
                     sh[BLOCK:2 * BLOCK],
                     jnp.where(ok_next, sh[2 * BLOCK:3 * BLOCK], neg),
                     sh[3 * BLOCK:]]
        else:
            parts = [sh]
        mx = functools.reduce(jnp.maximum, [jnp.max(t, axis=0, keepdims=True) for t in parts])
        mxb = jnp.maximum(mx.astype(F32), sinks[hh]).astype(BF16)
        mx = mxb.astype(F32)
        probs = jnp.concatenate([jnp.exp2(t - mxb) for t in parts], axis=0)
        vt_h = jnp.concatenate([vtw[hh * A_HEAD_DIM:(hh + 1) * A_HEAD_DIM], ones], axis=0)
        ot = _dot(vt_h, probs)
        den = ot[A_HEAD_DIM:A_HEAD_DIM + 1] + jnp.exp2(sinks[hh] - mx)
        outs.append(ot[0:A_HEAD_DIM] * (1.0 / den))
    return jnp.concatenate(outs, axis=0)


def _pair_cast_rows(src_ref, dst_ref):
    @pl.when((pl.program_id(0) == 0) & (pl.program_id(1) == 0))
    def _():
        for p in range(A_GROUP):
            for half, h in enumerate((p, p + A_GROUP)):
                lo = p * LANES + half * A_HEAD_DIM
                dst_ref[lo:lo + A_HEAD_DIM, :] = src_ref[h * A_HEAD_DIM:(h + 1) * A_HEAD_DIM, :].astype(BF16)


def _attn_kernel(local, n_steps, sink_ref, qt_ref, *rest):
    n_blk = rest[-1].shape[1] // BLOCK
    step = pl.program_id(1)
    if local:
        k_ref, vt_ref, kx_ref, vx_ref, sg_ref, x_ref, m_ref, w_ref, o_ref = rest
        n_all = k_ref.shape[1] // BLOCK
        offs = [pl.multiple_of(jnp.clip(step * n_blk + i, 0, n_all - 1) * BLOCK, BLOCK)
                for i in range(-1, n_blk + 1)]
        k_blocks = [k_ref[0, pl.ds(off, BLOCK)] for off in offs]
        v_blocks = [vt_ref[0, :, pl.ds(off, BLOCK)] for off in offs]
    else:
        kx_ref, vx_ref, sg_ref, x_ref, m_ref, w_ref, o_ref = rest
    nq = A_GROUP * BLOCK
    chunk_of = lax.broadcasted_iota(jnp.int32, (1, nq), 1) // BLOCK
    sinks = []
    for hh in range(2):
        sk = jnp.full((1, nq), sink_ref[hh] * LOG2_E, F32)
        for c in range(1, A_GROUP):
            sk = jnp.where(chunk_of == c, sink_ref[2 * c + hh] * LOG2_E, sk)
        sinks.append(sk)
    if local:
        kj = lax.broadcasted_iota(jnp.int32, (BLOCK, nq), 0)
        qi = lax.broadcasted_iota(jnp.int32, (BLOCK, nq), 1) % BLOCK

    rows_out = []
    for blk in range(n_blk):
        qts = jnp.concatenate([qt_ref[0, c * LANES:(c + 1) * LANES, blk * BLOCK:(blk + 1) * BLOCK]
                               for c in range(A_GROUP)], axis=1)
        if local:
            kwin = jnp.concatenate(k_blocks[blk:blk + 3] + [kx_ref[0]], axis=0)
            vtw = jnp.concatenate(v_blocks[blk:blk + 3] + [vx_ref[0]], axis=1)
            ok_prev = (kj >= qi) & (step > 0) if blk == 0 else (kj >= qi)
            ok_next = (kj <= qi) & (step < n_steps - 1) if blk == n_blk - 1 else (kj <= qi)
        else:
            kwin, vtw, ok_prev, ok_next = kx_ref[0], vx_ref[0], None, None
        ot = _attn_block(kwin, vtw, qts, sinks, ok_prev, ok_next)
        rows = slice(blk * BLOCK, (blk + 1) * BLOCK)
        outs = []
        for c in range(A_GROUP):
            sg = sg_ref[0, rows, c * LANES:(c + 1) * LANES].astype(F32)
            outs.append((ot[:, c * BLOCK:(c + 1) * BLOCK].T * sg).astype(BF16))
        rows_out.append(jnp.concatenate(outs, axis=1))
    y = _dot(jnp.concatenate(rows_out, axis=0), w_ref[...])
    gate = m_ref[0][:, 2 * D:3 * D]
    o_ref[0] = x_ref[0] + gate * y


def _attention(sink, qt, k, vt, kx, vxt, sg, x, mod3, w_out):
    b, s, _ = sg.shape
    rows = A_BLOCKS_PER_STEP * BLOCK
    assert s % rows == 0
    ns = s // rows
    n_ctx = kx.shape[1]
    blk = lambda w: pl.BlockSpec((1, rows, w), lambda bi, i: (bi, i, 0))
    blk_t = lambda w: pl.BlockSpec((1, w, rows), lambda bi, i: (bi, 0, i))
    in_specs = [pl.BlockSpec(memory_space=pltpu.SMEM), blk_t(A_WIDTH),
                pl.BlockSpec((1, s, A_KV_WIDTH), lambda bi, i: (bi, 0, 0)),
                pl.BlockSpec((1, A_KV_WIDTH, s), lambda bi, i: (bi, 0, 0)),
                pl.BlockSpec((1, n_ctx, A_KV_WIDTH), lambda bi, i: (bi, 0, 0)),
                pl.BlockSpec((1, A_KV_WIDTH, n_ctx), lambda bi, i: (bi, 0, 0)),
                blk(A_WIDTH), blk(D),
                pl.BlockSpec((1, 1, 3 * D), lambda bi, i: (bi, 0, 0)),
                pl.BlockSpec((A_WIDTH, D), lambda bi, i: (0, 0))]
    return pl.pallas_call(
        functools.partial(_attn_kernel, True, ns),
        grid=(b, ns),
        in_specs=in_specs,
        out_specs=blk(D),
        out_shape=jax.ShapeDtypeStruct((b, s, D), F32),
        compiler_params=_params("parallel", "parallel"),
        name="attn_local",
    )(sink, qt, k, vt, kx, vxt, sg, x, mod3, w_out)


def _ctx_attn_kernel(sink_ref, x_ref, m_ref, g_ref, win_ref, wout_ref, k_ref, vt_ref, o_ref,
                     win_bf_ref, wout_bf_ref, qt_scr, sg_scr):
    _pair_cast_columns(win_ref, win_bf_ref)
    _pair_cast_rows(wout_ref, wout_bf_ref)
    _inproj_attn_kernel(False, x_ref, m_ref, g_ref, win_bf_ref, qt_scr, k_ref, vt_ref, sg_scr)
    _attn_kernel(False, 1, sink_ref, qt_scr, k_ref, vt_ref, sg_scr, x_ref, m_ref, wout_bf_ref, o_ref)


def _ctx_attention(sink, xc, mod3, mod_row, norm_g, w_in, w_out):
    b, n, _ = xc.shape
    const = lambda shape, **kw: pl.BlockSpec(shape, lambda bi, i: (0,) * len(shape), **kw)
    return pl.pallas_call(
        _ctx_attn_kernel,
        grid=(b, 1),
        in_specs=[pl.BlockSpec(memory_space=pltpu.SMEM),
                  pl.BlockSpec((1, n, D), lambda bi, i: (bi, 0, 0)),
                  pl.BlockSpec((1, 1, 3 * D), lambda bi, i: (mod_row, 0, 0)),
                  const((1, D)),
                  const(w_in.shape, pipeline_mode=pl.Buffered(1)),
                  const(w_out.shape, pipeline_mode=pl.Buffered(1))],
        out_specs=(pl.BlockSpec((1, n, A_KV_WIDTH), lambda bi, i: (bi, 0, 0)),
                   pl.BlockSpec((1, A_KV_WIDTH, n), lambda bi, i: (bi, 0, 0)),
                   pl.BlockSpec((1, n, D), lambda bi, i: (bi, 0, 0)),
                   const(w_in.shape), const(w_out.shape)),
        out_shape=(jax.ShapeDtypeStruct((b, n, A_KV_WIDTH), BF16),
                   jax.ShapeDtypeStruct((b, A_KV_WIDTH, n), BF16),
                   jax.ShapeDtypeStruct((b, n, D), F32),
                   jax.ShapeDtypeStruct(w_in.shape, BF16), jax.ShapeDtypeStruct(w_out.shape, BF16)),
        scratch_shapes=[pltpu.VMEM((1, A_WIDTH, n), BF16), pltpu.VMEM((1, n, A_WIDTH), BF16)],
        compiler_params=_params("arbitrary", "arbitrary"),
        name="ctx_attn",
    )(sink, xc, mod3, norm_g.reshape(1, D), w_in, w_out)


def _chunk_cumsum(x, tri):
    return _dot(tri, x.astype(BF16))


def _inproj_gla_kernel(chunk, with_q, x_ref, m_ref, g_ref, w_in_ref, wa1_ref, wa2_ref, ba_ref, tri_ref, *outs):
    if with_q:
        w_ref = w_in_ref
        per_dir = (outs[0:3], outs[3:6])
        v_ref, sg_ref = outs[6:]
    else:
        w_ref = outs[2]
        _cast_once(w_in_ref, w_ref)
        per_dir = ((outs[0],), (outs[1],))
    tm = x_ref.shape[1]
    hb = _norm_mod(x_ref[0], g_ref[...], m_ref[0]).astype(BF16)
    r = _dot(hb, wa1_ref[...])
    z = _dot(r.astype(BF16), wa2_ref[...]) + ba_ref[...]
    e = jnp.exp2(jnp.abs(z) * -LOG2_E)
    la = (jnp.minimum(z, 0.0) * LOG2_E - jnp.log2(1.0 + e)) * (1.0 / GATE_TEMP)
    k = _dot(hb, w_ref[:, B_K_WIDTH:2 * B_K_WIDTH])
    if with_q:
        q = _dot(hb, w_ref[:, 0:B_K_WIDTH]) * K_SCALE
    n_chunks = tm // chunk
    k_ends = []
    for reverse in (False, True):
        lad = la[:, B_K_WIDTH:] if reverse else la[:, 0:B_K_WIDTH]
        refs = per_dir[1] if reverse else per_dir[0]
        tots, kis, k_end = [], [], None
        tri = tri_ref[1 if reverse else 0]
        tb = tri.shape[0]
        cum_all = jnp.concatenate([_chunk_cumsum(lad[r * tb:(r + 1) * tb], tri) for r in range(tm // tb)], axis=0)
        for c in range(n_chunks):
            rows = slice(c * chunk, (c + 1) * chunk)
            cum = cum_all[rows]
            tot = cum[0:1] if reverse else cum[chunk - 1:chunk]
            if not with_q:
                k_end = (k[rows] * jnp.exp2(tot - cum)).astype(BF16)
                continue
            refs[0][0, rows] = (q[rows] * jnp.exp2(cum)).astype(BF16)
            kis.append(k[rows] * jnp.exp2(-cum))
            tots.append(tot)
            if c % 2 == 1:
                pair = slice((c - 1) * chunk, (c + 1) * chunk)
                refs[1][0, :, pair] = jnp.concatenate(kis[-2:], axis=0).T.astype(BF16)
        if with_q:
            pad = [jnp.zeros((DEC_ROWS - n_chunks, B_K_WIDTH), F32)] if n_chunks < DEC_ROWS else []
            refs[2][0, 0] = jnp.exp2(jnp.concatenate(tots + pad, axis=0))
        else:
            k_ends.append(k_end)
    if with_q:
        sg_ref[0] = _silu(_dot(hb, w_ref[:, 2 * B_K_WIDTH + B_V_WIDTH:])).astype(BF16)
    v = _dot(hb, w_ref[:, 2 * B_K_WIDTH:2 * B_K_WIDTH + B_V_WIDTH]).astype(BF16)
    if with_q:
        v_ref[0] = v
    else:
        for refs, k_end in zip(per_dir, k_ends):
            for h in range(B_HEADS):
                ks = slice(h * B_KEY_DIM, (h + 1) * B_KEY_DIM)
                refs[0][0, h] = _dot_tn(k_end[:, ks], v[:, h * B_VAL_DIM:(h + 1) * B_VAL_DIM])


def _inproj_gla(x, mod3, mod_row, norm_g, w, wa1, wa2, ba, chunk, with_q):
    b, s, _ = x.shape
    tm = min(ROW_TILE, s)
    nt = s // tm
    cpt = tm // chunk
    tb = max(chunk, MXU_DIM)
    assert tm % tb == 0 and tb % chunk == 0
    if mod_row is None:
        mod_map = lambda bi, i: (bi, 0, 0)
    else:
        mod_map = lambda bi, i: (mod_row, 0, 0)
    const = lambda shape, **kw: pl.BlockSpec(shape, lambda bi, i: (0,) * len(shape), **kw)
    in_specs = [
        pl.BlockSpec((1, tm, D), lambda bi, i: (bi, i, 0)),
        pl.BlockSpec((1, 1, 3 * D), mod_map),
        const((1, D)), const(w.shape, **({} if with_q else {"pipeline_mode": pl.Buffered(1)})),
        const(wa1.shape), const(wa2.shape), const(ba.shape),
        const((2, tb, tb)),
    ]
    t_idx = np.arange(tb)
    same = (t_idx[:, None] // chunk) == (t_idx[None, :] // chunk)
    lower = same & (t_idx[None, :] <= t_idx[:, None])
    tri = jnp.asarray(np.stack([lower, lower.T]), dtype=BF16)
    rows = lambda w_: (jax.ShapeDtypeStruct((b, s, w_), BF16),
                       pl.BlockSpec((1, tm, w_), lambda bi, i: (bi, i, 0)))
    cols = (jax.ShapeDtypeStruct((b, B_K_WIDTH, s), BF16),
            pl.BlockSpec((1, B_K_WIDTH, tm), lambda bi, i: (bi, 0, i)))
    decs = (jax.ShapeDtypeStruct((b, nt, DEC_ROWS, B_K_WIDTH), F32),
            pl.BlockSpec((1, 1, DEC_ROWS, B_K_WIDTH), lambda bi, i: (bi, i, 0, 0)))
    if with_q:
        assert cpt <= DEC_ROWS and cpt % 2 == 0
        outs = [rows(B_K_WIDTH), cols, decs] * 2 + [rows(B_V_WIDTH)] * 2
    else:
        assert nt == 1 and cpt == 1
        state = (jax.ShapeDtypeStruct((b, B_HEADS, B_KEY_DIM, B_VAL_DIM), F32),
                 pl.BlockSpec((1, B_HEADS, B_KEY_DIM, B_VAL_DIM), lambda bi, i: (bi, 0, 0, 0)))
        outs = [state] * 2 + [(jax.ShapeDtypeStruct(w.shape, BF16), const(w.shape))]
    return pl.pallas_call(
        functools.partial(_inproj_gla_kernel, chunk, with_q),
        grid=(b, nt),
        in_specs=in_specs,
        out_specs=tuple(o[1] for o in outs),
        out_shape=tuple(o[0] for o in outs),
        compiler_params=_params("parallel", "parallel") if with_q else _params("arbitrary", "arbitrary"),
        name="inproj_gla" if with_q else "inproj_gla_ctx",
    )(x, mod3, norm_g.reshape(1, D), w, wa1, wa2, ba, tri)


def _scan_tile(reverse, tile, qd_ref, kit_ref, dec_ref, v_ref, st_ref):
    pair_rows = 2 * CHUNK
    ti = lax.broadcasted_iota(jnp.int32, (pair_rows, pair_rows), 0)
    si = lax.broadcasted_iota(jnp.int32, (pair_rows, pair_rows), 1)
    same = (ti // CHUNK) == (si // CHUNK)
    if reverse:
        use_inv = same & (ti <= si)
        use_end = (ti < CHUNK) & (si >= CHUNK)
        second_row = lax.broadcasted_iota(jnp.int32, (pair_rows, B_KEY_DIM), 0) < CHUNK
        first_col = lax.broadcasted_iota(jnp.int32, (B_KEY_DIM, pair_rows), 1) >= CHUNK
    else:
        use_inv = same & (ti >= si)
        use_end = (ti >= CHUNK) & (si < CHUNK)
        second_row = lax.broadcasted_iota(jnp.int32, (pair_rows, B_KEY_DIM), 0) >= CHUNK
        first_col = lax.broadcasted_iota(jnp.int32, (B_KEY_DIM, pair_rows), 1) < CHUNK
    n_pairs = SCAN_TILE // pair_rows
    order = range(n_pairs - 1, -1, -1) if reverse else range(n_pairs)
    n_dec = dec_ref.shape[1] * DEC_ROWS
    dec = dec_ref[0].reshape(n_dec, B_K_WIDTH)
    dect = jnp.concatenate([dec, jnp.zeros((LANES - n_dec, B_K_WIDTH), F32)], axis=0).T
    heads = []
    for h in range(B_HEADS):
        ks = slice(h * B_KEY_DIM, (h + 1) * B_KEY_DIM)
        vs = slice(h * B_VAL_DIM, (h + 1) * B_VAL_DIM)
        st = st_ref[h]
        o_rows = [None] * n_pairs
        for p in order:
            rows = slice(p * pair_rows, (p + 1) * pair_rows)
            c_first, c_second = (2 * p + 1, 2 * p) if reverse else (2 * p, 2 * p + 1)
            qd = qd_ref[0, rows, ks]
            kit = kit_ref[0, ks, rows]
            vh = v_ref[0, pl.ds(pl.multiple_of(tile * SCAN_TILE + p * pair_rows, pair_rows), pair_rows), vs]
            d_first, d_second = dect[ks, c_first:c_first + 1], dect[ks, c_second:c_second + 1]
            ki32 = kit.astype(F32)
            s2 = _dot(qd, jnp.concatenate([kit, (ki32 * d_first).astype(BF16)], axis=1))
            a = jnp.where(use_inv, s2[:, 0:pair_rows], jnp.where(use_end, s2[:, pair_rows:], 0.0)).astype(BF16)
            q_pair = jnp.where(second_row, qd.astype(F32) * dec[c_first:c_first + 1, ks], qd.astype(F32))
            k_pair = ki32 * jnp.where(first_col, d_first * d_second, d_second)
            o_rows[p] = _dot(a, vh) + _dot(q_pair.astype(BF16), st.astype(BF16))
            st = st * (d_first * d_second) + _dot(k_pair.astype(BF16), vh)
        st_ref[h] = st
        heads.append(jnp.concatenate(o_rows, axis=0))
    return jnp.concatenate(heads, axis=1)


def _gla_scan_kernel(nt, qdf_ref, kitf_ref, decf_ref, qdb_ref, kitb_ref, decb_ref,
                     v_ref, sf_ref, sb_ref, sg_ref, x_ref, m_ref, hn_ref, w32_ref, fn_ref,
                     o_ref, st_ref, of_ref, w_ref):
    _cast_once(w32_ref, w_ref)
    j = pl.program_id(1)

    @pl.when(j == 0)
    def _():
        st_ref[...] = sf_ref[0]

    @pl.when(j == nt)
    def _():
        st_ref[...] = sb_ref[0]

    @pl.when(j < nt)
    def _():
        of_ref[j] = _scan_tile(False, j, qdf_ref, kitf_ref, decf_ref, v_ref, st_ref)

    @pl.when(j >= nt)
    def _():
        tile = 2 * nt - 1 - j
        o = _scan_tile(True, tile, qdb_ref, kitb_ref, decb_ref, v_ref, st_ref) + of_ref[tile]
        normed = []
        for h in range(B_HEADS):
            oh = o[:, h * B_VAL_DIM:(h + 1) * B_VAL_DIM]
            normed.append(oh * lax.rsqrt(jnp.mean(oh * oh, axis=-1, keepdims=True) + EPS))
        of = jnp.concatenate(normed, axis=1) * hn_ref[...]
        y = _dot((of * sg_ref[0].astype(F32)).astype(BF16), w_ref[...])
        xn = x_ref[0] + m_ref[0][:, 2 * D:3 * D] * y
        ms = jnp.mean(xn * xn, axis=-1, keepdims=True)
        o_ref[0] = xn * lax.rsqrt(ms + EPS) * fn_ref[...]


def _gla_scan(fwd, bwd, v, s_f, s_b, sg, x, mod3, head_g, w_out, final_g):
    b, s, _ = v.shape
    assert SCAN_TILE % ROW_TILE == 0 and ROW_TILE // CHUNK == DEC_ROWS
    dec_blocks = SCAN_TILE // ROW_TILE
    nt = s // SCAN_TILE
    t_fwd = lambda j: jnp.minimum(j, nt - 1)
    t_bwd = lambda j: 2 * nt - 1 - jnp.maximum(j, nt)

    def direction(t):
        rows = pl.BlockSpec((1, SCAN_TILE, B_K_WIDTH), lambda bi, j: (bi, t(j), 0))
        cols = pl.BlockSpec((1, B_K_WIDTH, SCAN_TILE), lambda bi, j: (bi, 0, t(j)))
        decs = pl.BlockSpec((1, dec_blocks, DEC_ROWS, B_K_WIDTH), lambda bi, j: (bi, t(j), 0, 0))
        return [rows, cols, decs]

    const = lambda shape: pl.BlockSpec(shape, lambda bi, j: (0,) * len(shape))
    state = pl.BlockSpec((1, B_HEADS, B_KEY_DIM, B_VAL_DIM), lambda bi, j: (bi, 0, 0, 0))
    out_rows = lambda w_: pl.BlockSpec((1, SCAN_TILE, w_), lambda bi, j: (bi, t_bwd(j), 0))
    in_specs = direction(t_fwd) + direction(t_bwd) + [
        pl.BlockSpec((1, s, B_V_WIDTH), lambda bi, j: (bi, 0, 0)),
        state, state, out_rows(B_V_WIDTH), out_rows(D),
        pl.BlockSpec((1, 1, 3 * D), lambda bi, j: (bi, 0, 0)),
        const((1, B_V_WIDTH)),
        pl.BlockSpec((B_V_WIDTH, D), lambda bi, j: (0, 0), pipeline_mode=pl.Buffered(1)),
        const((1, D))]
    return pl.pallas_call(
        functools.partial(_gla_scan_kernel, nt),
        grid=(b, 2 * nt),
        in_specs=in_specs,
        out_specs=out_rows(D),
        out_shape=jax.ShapeDtypeStruct((b, s, D), F32),
        scratch_shapes=[pltpu.VMEM((B_HEADS, B_KEY_DIM, B_VAL_DIM), F32),
                        pltpu.VMEM((nt, SCAN_TILE, B_V_WIDTH), F32),
                        pltpu.VMEM((B_V_WIDTH, D), BF16)],
        compiler_params=_params("arbitrary", "arbitrary", vmem_limit=SCAN_VMEM_LIMIT),
        name="gla_scan",
    )(*fwd, *bwd, v, s_f, s_b, sg, x, mod3, head_g.reshape(1, B_V_WIDTH), w_out, final_g.reshape(1, D))


def _rope_tables(n_tokens):
    rows_n = n_tokens // GRID_W
    row = np.repeat(np.arange(rows_n, dtype=np.float32), GRID_W)
    col = np.tile(np.arange(GRID_W, dtype=np.float32), rows_n)
    inv_freq = (np.float32(ROPE_BASE) ** (-np.arange(ROPE_FREQS, dtype=np.float32) / np.float32(ROPE_FREQS)))
    inv_freq = inv_freq.astype(np.float32)
    ang = np.stack([row[:, None] * inv_freq, col[:, None] * inv_freq], axis=1)
    cos, sin = np.cos(ang).astype(np.float32), np.sin(ang).astype(np.float32)
    zero = np.zeros_like(sin)
    tile = lambda t: np.tile(t.reshape(n_tokens, A_HEAD_DIM), (1, LANES // A_HEAD_DIM))
    return (tile(np.stack([cos, cos], axis=2)),
            tile(np.stack([-sin, zero], axis=2)),
            tile(np.stack([zero, sin], axis=2)))


def _pair_heads(t, axis):
    shape = t.shape
    t = t.reshape(shape[:axis] + (A_KV_HEADS, A_GROUP, -1) + shape[axis + 1:])
    return jnp.swapaxes(t, axis, axis + 1).reshape(shape)


def kernel(x, c, ctx, c_ctx, l0_norm_g, l0_w_ada, l0_b_ada, l0_w_in, l0_sink, l0_w_out, l1_norm_g, l1_w_ada, l1_b_ada, l1_w_in, l1_wa1_f, l1_wa2_f, l1_ba_f, l1_wa1_b, l1_wa2_b, l1_ba_b, l1_head_norm_g, l1_w_out, final_norm_g):
    b, s, _ = x.shape
    ctx_row = b

    cvec = jnp.concatenate([c, c_ctx[None, :], jnp.zeros((MOD_ROWS - b - 1, D), F32)], axis=0)
    mod0, mod1 = (m.reshape(MOD_ROWS, 1, 3 * D) for m in _modulation(cvec, l0_w_ada, l0_b_ada, l1_w_ada, l1_b_ada))

    sink = _pair_heads(l0_sink.astype(F32), 0)
    kc, vct, xc1, w0_in, w0_out = _ctx_attention(sink, ctx, mod0, ctx_row, l0_norm_g, l0_w_in, l0_w_out)
    qt, k, vt, sg = _inproj_attn(x, mod0, l0_norm_g, w0_in, _rope_tables(s))
    x1 = _attention(sink, qt, k, vt, kc, vct, sg, x, mod0, w0_out)

    wa1 = jnp.concatenate([l1_wa1_f, l1_wa1_b, jnp.zeros((D, LANES - 2 * GATE_RANK), F32)], axis=1).astype(BF16)
    wa2 = jnp.zeros((LANES, 2 * B_K_WIDTH), F32)
    wa2 = wa2.at[0:GATE_RANK, 0:B_K_WIDTH].set(l1_wa2_f)
    wa2 = wa2.at[GATE_RANK:2 * GATE_RANK, B_K_WIDTH:].set(l1_wa2_b).astype(BF16)
    ba = jnp.concatenate([l1_ba_f, l1_ba_b]).reshape(1, 2 * B_K_WIDTH)
    s_f, s_b, w1 = _inproj_gla(xc1, mod1, ctx_row, l1_norm_g, l1_w_in, wa1, wa2, ba, ctx.shape[1], False)
    outs = _inproj_gla(x1, mod1, None, l1_norm_g, w1, wa1, wa2, ba, CHUNK, True)
    v1, sg1 = outs[6:]
    return _gla_scan(outs[0:3], outs[3:6], v1, s_f, s_b, sg1, x1, mod1,
                     l1_head_norm_g, l1_w_out, final_norm_g)
```

```python
import functools

import jax
import jax.numpy as jnp
import numpy as np
from jax import lax
from jax.experimental import pallas as pl
from jax.experimental.pallas import tpu as pltpu

F32 = jnp.float32
BF16 = jnp.bfloat16

D = 1024
GRID_W = 64
EPS = 1e-6
NEG_INF = -1e30

A_HEADS = 16
A_KV_HEADS = 2
A_GROUP = A_HEADS // A_KV_HEADS
A_HEAD_DIM = 64
A_WIDTH = A_HEADS * A_HEAD_DIM
A_KV_WIDTH = A_KV_HEADS * A_HEAD_DIM
BLOCK = 128
ROPE_BASE = 10000.0
ROPE_FREQS = A_HEAD_DIM // 4
Q_SCALE = A_HEAD_DIM ** -0.5
LOG2_E = 1.4426950408889634

B_HEADS = 4
B_K_WIDTH = D // 2
B_V_WIDTH = D
B_KEY_DIM = B_K_WIDTH // B_HEADS
B_VAL_DIM = B_V_WIDTH // B_HEADS
GATE_RANK = 16
GATE_TEMP = 16.0
CHUNK = 64
K_SCALE = B_KEY_DIM ** -0.5

LANES = 128
MXU_DIM = 256
MOD_ROWS = 16
ROW_TILE = 1024
A_ROW_TILE = 1024
SCAN_TILE = 1024
SUB_ROWS = 256
DEC_ROWS = ROW_TILE // CHUNK
SUM_ROWS = 16
A_BLOCKS_PER_STEP = 8
VMEM_LIMIT = 48 * 1024 * 1024
SCAN_VMEM_LIMIT = 60 * 1024 * 1024


def _params(*sem, vmem_limit=VMEM_LIMIT):
    return pltpu.CompilerParams(dimension_semantics=sem, vmem_limit_bytes=vmem_limit)


def _silu(x):
    return x / (1.0 + jnp.exp(-x))


def _dot(a, b):
    return jnp.dot(a, b, preferred_element_type=F32)


def _dot_nt(a, b):
    return lax.dot_general(a, b, (((1,), (1,)), ((), ())), preferred_element_type=F32)


def _dot_tn(a, b):
    return lax.dot_general(a, b, (((0,), (0,)), ((), ())), preferred_element_type=F32)


def _norm_mod(x, g, m):
    ms = jnp.mean(x * x, axis=-1, keepdims=True)
    y = x * lax.rsqrt(ms + EPS) * g
    return y * (1.0 + m[:, D:2 * D]) + m[:, 0:D]


def _cast_once(src_ref, dst_ref):
    @pl.when((pl.program_id(0) == 0) & (pl.program_id(1) == 0))
    def _():
        def body(i, carry):
            rows = pl.ds(pl.multiple_of(i * LANES, LANES), LANES)
            dst_ref[rows, :] = src_ref[rows, :].astype(dst_ref.dtype)
            return carry
        lax.fori_loop(0, src_ref.shape[0] // LANES, body, 0)


def _mod_kernel(n_tiles, c_ref, w0_ref, b0_ref, w1_ref, b1_ref, o0_ref, o1_ref):
    s = _silu(c_ref[...]).astype(BF16)
    j = pl.program_id(0)

    @pl.when(j < n_tiles)
    def _():
        o0_ref[...] = _dot(s, w0_ref[...].astype(BF16)) + b0_ref[...]

    @pl.when(j >= n_tiles)
    def _():
        o1_ref[...] = _dot(s, w1_ref[...].astype(BF16)) + b1_ref[...]


def _modulation(cvec, w_ada0, b_ada0, w_ada1, b_ada1):
    n = w_ada0.shape[1] // D
    first = lambda j: (0, jnp.minimum(j, n - 1))
    second = lambda j: (0, jnp.maximum(j - n, 0))
    out = jax.ShapeDtypeStruct((MOD_ROWS, n * D), F32)
    return pl.pallas_call(
        functools.partial(_mod_kernel, n),
        grid=(2 * n,),
        in_specs=[
            pl.BlockSpec((MOD_ROWS, D), lambda j: (0, 0)),
            pl.BlockSpec((D, D), first), pl.BlockSpec((1, D), first),
            pl.BlockSpec((D, D), second), pl.BlockSpec((1, D), second),
        ],
        out_specs=(pl.BlockSpec((MOD_ROWS, D), first), pl.BlockSpec((MOD_ROWS, D), second)),
        out_shape=(out, out),
        compiler_params=_params("arbitrary"),
        name="modulation",
    )(cvec, w_ada0, b_ada0.reshape(1, n * D), w_ada1, b_ada1.reshape(1, n * D))


def _rope(t, cos, sin_hi, sin_lo):
    return t * cos + pltpu.roll(t, LANES - ROPE_FREQS, 1) * sin_hi + pltpu.roll(t, ROPE_FREQS, 1) * sin_lo


def _pair_cast_columns(src_ref, dst_ref):
    @pl.when((pl.program_id(0) == 0) & (pl.program_id(1) == 0))
    def _():
        kv_lo, kv_hi = A_WIDTH, A_WIDTH + 2 * A_KV_WIDTH

        def body(i, carry):
            rows = pl.ds(pl.multiple_of(i * LANES, LANES), LANES)
            src = src_ref[rows, :]
            dst_ref[rows, kv_lo:kv_hi] = src[:, kv_lo:kv_hi].astype(BF16)
            for base in (0, kv_hi):
                for p in range(A_GROUP):
                    a = base + p * A_HEAD_DIM
                    b = base + (p + A_GROUP) * A_HEAD_DIM
                    pair = jnp.concatenate([src[:, a:a + A_HEAD_DIM], src[:, b:b + A_HEAD_DIM]], axis=1)
                    dst_ref[rows, base + p * LANES:base + (p + 1) * LANES] = pair.astype(BF16)
            return carry
        lax.fori_loop(0, src_ref.shape[0] // LANES, body, 0)


def _inproj_attn_kernel(rope, x_ref, m_ref, g_ref, w_ref, *rest):
    wq_ref = w_ref.at[:, 0:A_WIDTH]
    wkv_ref = w_ref.at[:, A_WIDTH:A_WIDTH + 2 * A_KV_WIDTH]
    wg_ref = w_ref.at[:, A_WIDTH + 2 * A_KV_WIDTH:]
    if rope:
        cos_ref, shi_ref, slo_ref, qt_ref, k_ref, vt_ref, sg_ref = rest
    else:
        qt_ref, k_ref, vt_ref, sg_ref = rest
    assert x_ref.shape[1] % SUB_ROWS == 0
    for r in range(x_ref.shape[1] // SUB_ROWS):
        rows = slice(r * SUB_ROWS, (r + 1) * SUB_ROWS)
        if rope:
            pos = pl.ds(pl.multiple_of(pl.program_id(1) * x_ref.shape[1] + r * SUB_ROWS, SUB_ROWS), SUB_ROWS)
            cos, shi, slo = cos_ref[pos], shi_ref[pos], slo_ref[pos]
        hb = _norm_mod(x_ref[0, rows], g_ref[...], m_ref[0]).astype(BF16)
        q = _dot(hb, wq_ref[...])
        for j in range(A_WIDTH // LANES):
            qj = q[:, j * LANES:(j + 1) * LANES]
            if rope:
                qj = _rope(qj, cos, shi, slo)
            qt_ref[0, j * LANES:(j + 1) * LANES, rows] = (qj * (Q_SCALE * LOG2_E)).T.astype(BF16)
        kv = _dot(hb, wkv_ref[...])
        k = kv[:, 0:A_KV_WIDTH]
        if rope:
            k = _rope(k, cos, shi, slo)
        k_ref[0, rows] = k.astype(BF16)
        vt_ref[0, :, rows] = kv[:, A_KV_WIDTH:].T.astype(BF16)
        g = _dot(hb, wg_ref[...])
        sg_ref[0, rows] = _silu(g).astype(BF16)


def _inproj_attn(x, mod3, norm_g, w_in, tables):
    b, s, _ = x.shape
    tm = min(A_ROW_TILE, s)
    in_specs = [
        pl.BlockSpec((1, tm, D), lambda bi, i: (bi, i, 0)),
        pl.BlockSpec((1, 1, 3 * D), lambda bi, i: (bi, 0, 0)),
        pl.BlockSpec((1, D), lambda bi, i: (0, 0)),
        pl.BlockSpec(w_in.shape, lambda bi, i: (0, 0)),
    ] + [pl.BlockSpec((s, LANES), lambda bi, i: (0, 0))] * 3
    args = [x, mod3, norm_g.reshape(1, D), w_in, *tables]
    out_shape = (
        jax.ShapeDtypeStruct((b, A_WIDTH, s), BF16),
        jax.ShapeDtypeStruct((b, s, A_KV_WIDTH), BF16),
        jax.ShapeDtypeStruct((b, A_KV_WIDTH, s), BF16),
        jax.ShapeDtypeStruct((b, s, A_WIDTH), BF16),
    )
    out_specs = (
        pl.BlockSpec((1, A_WIDTH, tm), lambda bi, i: (bi, 0, i)),
        pl.BlockSpec((1, tm, A_KV_WIDTH), lambda bi, i: (bi, i, 0)),
        pl.BlockSpec((1, A_KV_WIDTH, tm), lambda bi, i: (bi, 0, i)),
        pl.BlockSpec((1, tm, A_WIDTH), lambda bi, i: (bi, i, 0)),
    )
    return pl.pallas_call(
        functools.partial(_inproj_attn_kernel, True),
        grid=(b, s // tm),
        in_specs=in_specs,
        out_specs=out_specs,
        out_shape=out_shape,
        compiler_params=_params("parallel", "parallel"),
        name="inproj_attn",
    )(*args)


def _attn_block(kwin, vtw, qts, sinks, ok_prev, ok_next):
    n_keys, nq = kwin.shape[0], qts.shape[1]
    lane = lax.broadcasted_iota(jnp.int32, kwin.shape, 1)
    kbd = jnp.concatenate([jnp.where(lane < A_HEAD_DIM, kwin, jnp.zeros_like(kwin)),
                           jnp.where(lane >= A_HEAD_DIM, kwin, jnp.zeros_like(kwin))], axis=0)
    ones = jnp.where(lax.broadcasted_iota(jnp.int32, (SUM_ROWS, n_keys), 0) == 0, 1.0, 0.0).astype(BF16)

    st = _dot(kbd, qts).astype(BF16)
    neg = jnp.asarray(NEG_INF, BF16)
    outs = []
    for hh in range(2):
        sh = st[hh * n_keys:(hh + 1) * n_keys]
        if ok_prev is not None:
            parts = [jnp.where(ok_prev, sh[0:BLOCK], neg),
                     sh[BLOCK:2 * BLOCK],
                     jnp.where(ok_next, sh[2 * BLOCK:3 * BLOCK], neg),
                     sh[3 * BLOCK:]]
        else:
            parts = [sh]
        mx = functools.reduce(jnp.maximum, [jnp.max(t, axis=0, keepdims=True) for t in parts])
        mxb = jnp.maximum(mx.astype(F32), sinks[hh]).astype(BF16)
        mx = mxb.astype(F32)
        probs = jnp.concatenate([jnp.exp2(t - mxb) for t in parts], axis=0)
        vt_h = jnp.concatenate([vtw[hh * A_HEAD_DIM:(hh + 1) * A_HEAD_DIM], ones], axis=0)
        ot = _dot(vt_h, probs)
        den = ot[A_HEAD_DIM:A_HEAD_DIM + 1] + jnp.exp2(sinks[hh] - mx)
        outs.append(ot[0:A_HEAD_DIM] * (1.0 / den))
    return jnp.concatenate(outs, axis=0)


def _pair_cast_rows(src_ref, dst_ref):
    @pl.when((pl.program_id(0) == 0) & (pl.program_id(1) == 0))
    def _():
        for p in range(A_GROUP):
            for half, h in enumerate((p, p + A_GROUP)):
                lo = p * LANES + half * A_HEAD_DIM
                dst_ref[lo:lo + A_HEAD_DIM, :] = src_ref[h * A_HEAD_DIM:(h + 1) * A_HEAD_DIM, :].astype(BF16)


def _attn_kernel(local, n_steps, sink_ref, qt_ref, *rest):
    n_blk = rest[-1].shape[1] // BLOCK
    if local:
        (kp_ref, kc_ref, kn_ref, vp_ref, vc_ref, vn_ref, kx_ref, vx_ref,
         sg_ref, x_ref, m_ref, w_ref, o_ref) = rest
        last = slice((n_blk - 1) * BLOCK, n_blk * BLOCK)
        k_blocks = ([kp_ref[0, last]] + [kc_ref[0, i * BLOCK:(i + 1) * BLOCK] for i in range(n_blk)]
                    + [kn_ref[0, 0:BLOCK]])
        v_blocks = ([vp_ref[0, :, last]] + [vc_ref[0, :, i * BLOCK:(i + 1) * BLOCK] for i in range(n_blk)]
                    + [vn_ref[0, :, 0:BLOCK]])
    else:
        kx_ref, vx_ref, sg_ref, x_ref, m_ref, w_ref, o_ref = rest
    step = pl.program_id(1)
    nq = A_GROUP * BLOCK
    chunk_of = lax.broadcasted_iota(jnp.int32, (1, nq), 1) // BLOCK
    sinks = []
    for hh in range(2):
        sk = jnp.full((1, nq), sink_ref[hh] * LOG2_E, F32)
        for c in range(1, A_GROUP):
            sk = jnp.where(chunk_of == c, sink_ref[2 * c + hh] * LOG2_E, sk)
        sinks.append(sk)
    if local:
        kj = lax.broadcasted_iota(jnp.int32, (BLOCK, nq), 0)
        qi = lax.broadcasted_iota(jnp.int32, (BLOCK, nq), 1) % BLOCK

    rows_out = []
    for blk in range(n_blk):
        qts = jnp.concatenate([qt_ref[0, c * LANES:(c + 1) * LANES, blk * BLOCK:(blk + 1) * BLOCK]
                               for c in range(A_GROUP)], axis=1)
        if local:
            kwin = jnp.concatenate(k_blocks[blk:blk + 3] + [kx_ref[0]], axis=0)
            vtw = jnp.concatenate(v_blocks[blk:blk + 3] + [vx_ref[0]], axis=1)
            ok_prev = (kj >= qi) & (step > 0) if blk == 0 else (kj >= qi)
            ok_next = (kj <= qi) & (step < n_steps - 1) if blk == n_blk - 1 else (kj <= qi)
        else:
            kwin, vtw, ok_prev, ok_next = kx_ref[0], vx_ref[0], None, None
        ot = _attn_block(kwin, vtw, qts, sinks, ok_prev, ok_next)
        rows = slice(blk * BLOCK, (blk + 1) * BLOCK)
        outs = []
        for c in range(A_GROUP):
            sg = sg_ref[0, rows, c * LANES:(c + 1) * LANES].astype(F32)
            outs.append((ot[:, c * BLOCK:(c + 1) * BLOCK].T * sg).astype(BF16))
        rows_out.append(jnp.concatenate(outs, axis=1))
    y = _dot(jnp.concatenate(rows_out, axis=0), w_ref[...])
    gate = m_ref[0][:, 2 * D:3 * D]
    o_ref[0] = x_ref[0] + gate * y


def _attention(sink, qt, k, vt, kx, vxt, sg, x, mod3, w_out):
    b, s, _ = sg.shape
    rows = A_BLOCKS_PER_STEP * BLOCK
    assert s % rows == 0
    ns = s // rows
    n_ctx = kx.shape[1]
    blk = lambda w: pl.BlockSpec((1, rows, w), lambda bi, i: (bi, i, 0))
    blk_t = lambda w: pl.BlockSpec((1, w, rows), lambda bi, i: (bi, 0, i))
    lo = lambda i: jnp.maximum(i - 1, 0)
    hi = lambda i: jnp.minimum(i + 1, ns - 1)
    in_specs = [pl.BlockSpec(memory_space=pltpu.SMEM), blk_t(A_WIDTH),
                pl.BlockSpec((1, rows, A_KV_WIDTH), lambda bi, i: (bi, lo(i), 0)),
                blk(A_KV_WIDTH),
                pl.BlockSpec((1, rows, A_KV_WIDTH), lambda bi, i: (bi, hi(i), 0)),
                pl.BlockSpec((1, A_KV_WIDTH, rows), lambda bi, i: (bi, 0, lo(i))),
                blk_t(A_KV_WIDTH),
                pl.BlockSpec((1, A_KV_WIDTH, rows), lambda bi, i: (bi, 0, hi(i))),
                pl.BlockSpec((1, n_ctx, A_KV_WIDTH), lambda bi, i: (bi, 0, 0)),
                pl.BlockSpec((1, A_KV_WIDTH, n_ctx), lambda bi, i: (bi, 0, 0)),
                blk(A_WIDTH), blk(D),
                pl.BlockSpec((1, 1, 3 * D), lambda bi, i: (bi, 0, 0)),
                pl.BlockSpec((A_WIDTH, D), lambda bi, i: (0, 0))]
    return pl.pallas_call(
        functools.partial(_attn_kernel, True, ns),
        grid=(b, ns),
        in_specs=in_specs,
        out_specs=blk(D),
        out_shape=jax.ShapeDtypeStruct((b, s, D), F32),
        compiler_params=_params("parallel", "parallel"),
        name="attn_local",
    )(sink, qt, k, k, k, vt, vt, vt, kx, vxt, sg, x, mod3, w_out)


def _ctx_attn_kernel(sink_ref, x_ref, m_ref, g_ref, win_ref, wout_ref, k_ref, vt_ref, o_ref,
                     win_bf_ref, wout_bf_ref, qt_scr, sg_scr):
    _pair_cast_columns(win_ref, win_bf_ref)
    _pair_cast_rows(wout_ref, wout_bf_ref)
    _inproj_attn_kernel(False, x_ref, m_ref, g_ref, win_bf_ref, qt_scr, k_ref, vt_ref, sg_scr)
    _attn_kernel(False, 1, sink_ref, qt_scr, k_ref, vt_ref, sg_scr, x_ref, m_ref, wout_bf_ref, o_ref)


def _ctx_attention(sink, xc, mod3, mod_row, norm_g, w_in, w_out):
    b, n, _ = xc.shape
    const = lambda shape, **kw: pl.BlockSpec(shape, lambda bi, i: (0,) * len(shape), **kw)
    return pl.pallas_call(
        _ctx_attn_kernel,
        grid=(b, 1),
        in_specs=[pl.BlockSpec(memory_space=pltpu.SMEM),
                  pl.BlockSpec((1, n, D), lambda bi, i: (bi, 0, 0)),
                  pl.BlockSpec((1, 1, 3 * D), lambda bi, i: (mod_row, 0, 0)),
                  const((1, D)),
                  const(w_in.shape, pipeline_mode=pl.Buffered(1)),
                  const(w_out.shape, pipeline_mode=pl.Buffered(1))],
        out_specs=(pl.BlockSpec((1, n, A_KV_WIDTH), lambda bi, i: (bi, 0, 0)),
                   pl.BlockSpec((1, A_KV_WIDTH, n), lambda bi, i: (bi, 0, 0)),
                   pl.BlockSpec((1, n, D), lambda bi, i: (bi, 0, 0)),
                   const(w_in.shape), const(w_out.shape)),
        out_shape=(jax.ShapeDtypeStruct((b, n, A_KV_WIDTH), BF16),
                   jax.ShapeDtypeStruct((b, A_KV_WIDTH, n), BF16),
                   jax.ShapeDtypeStruct((b, n, D), F32),
                   jax.ShapeDtypeStruct(w_in.shape, BF16), jax.ShapeDtypeStruct(w_out.shape, BF16)),
        scratch_shapes=[pltpu.VMEM((1, A_WIDTH, n), BF16), pltpu.VMEM((1, n, A_WIDTH), BF16)],
        compiler_params=_params("arbitrary", "arbitrary"),
        name="ctx_attn",
    )(sink, xc, mod3, norm_g.reshape(1, D), w_in, w_out)


def _chunk_cumsum(x, tri):
    return _dot(tri, x.astype(BF16))


def _inproj_gla_kernel(chunk, with_q, x_ref, m_ref, g_ref, w_in_ref, wa1_ref, wa2_ref, ba_ref, tri_ref, *outs):
    if with_q:
        w_ref = w_in_ref
        per_dir = (outs[0:3], outs[3:6])
        v_ref, sg_ref = outs[6:]
    else:
        w_ref = outs[2]
        _cast_once(w_in_ref, w_ref)
        per_dir = ((outs[0],), (outs[1],))
    tm = x_ref.shape[1]
    hb = _norm_mod(x_ref[0], g_ref[...], m_ref[0]).astype(BF16)
    r = _dot(hb, wa1_ref[...])
    z = _dot(r.astype(BF16), wa2_ref[...]) + ba_ref[...]
    e = jnp.exp2(jnp.abs(z) * -LOG2_E)
    la = (jnp.minimum(z, 0.0) * LOG2_E - jnp.log2(1.0 + e)) * (1.0 / GATE_TEMP)
    k = _dot(hb, w_ref[:, B_K_WIDTH:2 * B_K_WIDTH])
    if with_q:
        q = _dot(hb, w_ref[:, 0:B_K_WIDTH]) * K_SCALE
    n_chunks = tm // chunk
    k_ends = []
    for reverse in (False, True):
        lad = la[:, B_K_WIDTH:] if reverse else la[:, 0:B_K_WIDTH]
        refs = per_dir[1] if reverse else per_dir[0]
        tots, kis, k_end = [], [], None
        tri = tri_ref[1 if reverse else 0]
        tb = tri.shape[0]
        cum_all = jnp.concatenate([_chunk_cumsum(lad[r * tb:(r + 1) * tb], tri) for r in range(tm // tb)], axis=0)
        for c in range(n_chunks):
            rows = slice(c * chunk, (c + 1) * chunk)
            cum = cum_all[rows]
            tot = cum[0:1] if reverse else cum[chunk - 1:chunk]
            if not with_q:
                k_end = (k[rows] * jnp.exp2(tot - cum)).astype(BF16)
                continue
            refs[0][0, rows] = (q[rows] * jnp.exp2(cum)).astype(BF16)
            kis.append(k[rows] * jnp.exp2(-cum))
            tots.append(tot)
            if c % 2 == 1:
                pair = slice((c - 1) * chunk, (c + 1) * chunk)
                refs[1][0, :, pair] = jnp.concatenate(kis[-2:], axis=0).T.astype(BF16)
        if with_q:
            pad = [jnp.zeros((DEC_ROWS - n_chunks, B_K_WIDTH), F32)] if n_chunks < DEC_ROWS else []
            refs[2][0, 0] = jnp.exp2(jnp.concatenate(tots + pad, axis=0))
        else:
            k_ends.append(k_end)
    if with_q:
        sg_ref[0] = _silu(_dot(hb, w_ref[:, 2 * B_K_WIDTH + B_V_WIDTH:])).astype(BF16)
    v = _dot(hb, w_ref[:, 2 * B_K_WIDTH:2 * B_K_WIDTH + B_V_WIDTH]).astype(BF16)
    if with_q:
        v_ref[0] = v
    else:
        for refs, k_end in zip(per_dir, k_ends):
            for h in range(B_HEADS):
                ks = slice(h * B_KEY_DIM, (h + 1) * B_KEY_DIM)
                refs[0][0, h] = _dot_tn(k_end[:, ks], v[:, h * B_VAL_DIM:(h + 1) * B_VAL_DIM])


def _inproj_gla(x, mod3, mod_row, norm_g, w, wa1, wa2, ba, chunk, with_q):
    b, s, _ = x.shape
    tm = min(ROW_TILE, s)
    nt = s // tm
    cpt = tm // chunk
    tb = max(chunk, MXU_DIM)
    assert tm % tb == 0 and tb % chunk == 0
    if mod_row is None:
        mod_map = lambda bi, i: (bi, 0, 0)
    else:
        mod_map = lambda bi, i: (mod_row, 0, 0)
    const = lambda shape, **kw: pl.BlockSpec(shape, lambda bi, i: (0,) * len(shape), **kw)
    in_specs = [
        pl.BlockSpec((1, tm, D), lambda bi, i: (bi, i, 0)),
        pl.BlockSpec((1, 1, 3 * D), mod_map),
        const((1, D)), const(w.shape, **({} if with_q else {"pipeline_mode": pl.Buffered(1)})),
        const(wa1.shape), const(wa2.shape), const(ba.shape),
        const((2, tb, tb)),
    ]
    t_idx = np.arange(tb)
    same = (t_idx[:, None] // chunk) == (t_idx[None, :] // chunk)
    lower = same & (t_idx[None, :] <= t_idx[:, None])
    tri = jnp.asarray(np.stack([lower, lower.T]), dtype=BF16)
    rows = lambda w_: (jax.ShapeDtypeStruct((b, s, w_), BF16),
                       pl.BlockSpec((1, tm, w_), lambda bi, i: (bi, i, 0)))
    cols = (jax.ShapeDtypeStruct((b, B_K_WIDTH, s), BF16),
            pl.BlockSpec((1, B_K_WIDTH, tm), lambda bi, i: (bi, 0, i)))
    decs = (jax.ShapeDtypeStruct((b, nt, DEC_ROWS, B_K_WIDTH), F32),
            pl.BlockSpec((1, 1, DEC_ROWS, B_K_WIDTH), lambda bi, i: (bi, i, 0, 0)))
    if with_q:
        assert cpt <= DEC_ROWS and cpt % 2 == 0
        outs = [rows(B_K_WIDTH), cols, decs] * 2 + [rows(B_V_WIDTH)] * 2
    else:
        assert nt == 1 and cpt == 1
        state = (jax.ShapeDtypeStruct((b, B_HEADS, B_KEY_DIM, B_VAL_DIM), F32),
                 pl.BlockSpec((1, B_HEADS, B_KEY_DIM, B_VAL_DIM), lambda bi, i: (bi, 0, 0, 0)))
        outs = [state] * 2 + [(jax.ShapeDtypeStruct(w.shape, BF16), const(w.shape))]
    return pl.pallas_call(
        functools.partial(_inproj_gla_kernel, chunk, with_q),
        grid=(b, nt),
        in_specs=in_specs,
        out_specs=tuple(o[1] for o in outs),
        out_shape=tuple(o[0] for o in outs),
        compiler_params=_params("parallel", "parallel") if with_q else _params("arbitrary", "arbitrary"),
        name="inproj_gla" if with_q else "inproj_gla_ctx",
    )(x, mod3, norm_g.reshape(1, D), w, wa1, wa2, ba, tri)


def _scan_tile(reverse, tile, qd_ref, kit_ref, dec_ref, v_ref, st_ref):
    pair_rows = 2 * CHUNK
    ti = lax.broadcasted_iota(jnp.int32, (pair_rows, pair_rows), 0)
    si = lax.broadcasted_iota(jnp.int32, (pair_rows, pair_rows), 1)
    same = (ti // CHUNK) == (si // CHUNK)
    if reverse:
        use_inv = same & (ti <= si)
        use_end = (ti < CHUNK) & (si >= CHUNK)
        second_row = lax.broadcasted_iota(jnp.int32, (pair_rows, B_KEY_DIM), 0) < CHUNK
        first_col = lax.broadcasted_iota(jnp.int32, (B_KEY_DIM, pair_rows), 1) >= CHUNK
    else:
        use_inv = same & (ti >= si)
        use_end = (ti >= CHUNK) & (si < CHUNK)
        second_row = lax.broadcasted_iota(jnp.int32, (pair_rows, B_KEY_DIM), 0) >= CHUNK
        first_col = lax.broadcasted_iota(jnp.int32, (B_KEY_DIM, pair_rows), 1) < CHUNK
    n_pairs = SCAN_TILE // pair_rows
    order = range(n_pairs - 1, -1, -1) if reverse else range(n_pairs)
    n_dec = dec_ref.shape[1] * DEC_ROWS
    dec = dec_ref[0].reshape(n_dec, B_K_WIDTH)
    dect = jnp.concatenate([dec, jnp.zeros((LANES - n_dec, B_K_WIDTH), F32)], axis=0).T
    heads = []
    for h in range(B_HEADS):
        ks = slice(h * B_KEY_DIM, (h + 1) * B_KEY_DIM)
        vs = slice(h * B_VAL_DIM, (h + 1) * B_VAL_DIM)
        st = st_ref[h]
        o_rows = [None] * n_pairs
        for p in order:
            rows = slice(p * pair_rows, (p + 1) * pair_rows)
            c_first, c_second = (2 * p + 1, 2 * p) if reverse else (2 * p, 2 * p + 1)
            qd = qd_ref[0, rows, ks]
            kit = kit_ref[0, ks, rows]
            vh = v_ref[0, pl.ds(pl.multiple_of(tile * SCAN_TILE + p * pair_rows, pair_rows), pair_rows), vs]
            d_first, d_second = dect[ks, c_first:c_first + 1], dect[ks, c_second:c_second + 1]
            ki32 = kit.astype(F32)
            s2 = _dot(qd, jnp.concatenate([kit, (ki32 * d_first).astype(BF16)], axis=1))
            a = jnp.where(use_inv, s2[:, 0:pair_rows], jnp.where(use_end, s2[:, pair_rows:], 0.0)).astype(BF16)
            q_pair = jnp.where(second_row, qd.astype(F32) * dec[c_first:c_first + 1, ks], qd.astype(F32))
            k_pair = ki32 * jnp.where(first_col, d_first * d_second, d_second)
            o_rows[p] = _dot(a, vh) + _dot(q_pair.astype(BF16), st.astype(BF16))
            st = st * (d_first * d_second) + _dot(k_pair.astype(BF16), vh)
        st_ref[h] = st
        heads.append(jnp.concatenate(o_rows, axis=0))
    return jnp.concatenate(heads, axis=1)


def _gla_scan_kernel(nt, qdf_ref, kitf_ref, decf_ref, qdb_ref, kitb_ref, decb_ref,
                     v_ref, sf_ref, sb_ref, sg_ref, x_ref, m_ref, hn_ref, w32_ref, fn_ref,
                     o_ref, st_ref, of_ref, w_ref):
    _cast_once(w32_ref, w_ref)
    j = pl.program_id(1)

    @pl.when(j == 0)
    def _():
        st_ref[...] = sf_ref[0]

    @pl.when(j == nt)
    def _():
        st_ref[...] = sb_ref[0]

    @pl.when(j < nt)
    def _():
        of_ref[j] = _scan_tile(False, j, qdf_ref, kitf_ref, decf_ref, v_ref, st_ref)

    @pl.when(j >= nt)
    def _():
        tile = 2 * nt - 1 - j
        o = _scan_tile(True, tile, qdb_ref, kitb_ref, decb_ref, v_ref, st_ref) + of_ref[tile]
        normed = []
        for h in range(B_HEADS):
            oh = o[:, h * B_VAL_DIM:(h + 1) * B_VAL_DIM]
            normed.append(oh * lax.rsqrt(jnp.mean(oh * oh, axis=-1, keepdims=True) + EPS))
        of = jnp.concatenate(normed, axis=1) * hn_ref[...]
        y = _dot((of * sg_ref[0].astype(F32)).astype(BF16), w_ref[...])
        xn = x_ref[0] + m_ref[0][:, 2 * D:3 * D] * y
        ms = jnp.mean(xn * xn, axis=-1, keepdims=True)
        o_ref[0] = xn * lax.rsqrt(ms + EPS) * fn_ref[...]


def _gla_scan(fwd, bwd, v, s_f, s_b, sg, x, mod3, head_g, w_out, final_g):
    b, s, _ = v.shape
    assert SCAN_TILE % ROW_TILE == 0 and ROW_TILE // CHUNK == DEC_ROWS
    dec_blocks = SCAN_TILE // ROW_TILE
    nt = s // SCAN_TILE
    t_fwd = lambda j: jnp.minimum(j, nt - 1)
    t_bwd = lambda j: 2 * nt - 1 - jnp.maximum(j, nt)

    def direction(t):
        rows = pl.BlockSpec((1, SCAN_TILE, B_K_WIDTH), lambda bi, j: (bi, t(j), 0))
        cols = pl.BlockSpec((1, B_K_WIDTH, SCAN_TILE), lambda bi, j: (bi, 0, t(j)))
        decs = pl.BlockSpec((1, dec_blocks, DEC_ROWS, B_K_WIDTH), lambda bi, j: (bi, t(j), 0, 0))
        return [rows, cols, decs]

    const = lambda shape: pl.BlockSpec(shape, lambda bi, j: (0,) * len(shape))
    state = pl.BlockSpec((1, B_HEADS, B_KEY_DIM, B_VAL_DIM), lambda bi, j: (bi, 0, 0, 0))
    out_rows = lambda w_: pl.BlockSpec((1, SCAN_TILE, w_), lambda bi, j: (bi, t_bwd(j), 0))
    in_specs = direction(t_fwd) + direction(t_bwd) + [
        pl.BlockSpec((1, s, B_V_WIDTH), lambda bi, j: (bi, 0, 0)),
        state, state, out_rows(B_V_WIDTH), out_rows(D),
        pl.BlockSpec((1, 1, 3 * D), lambda bi, j: (bi, 0, 0)),
        const((1, B_V_WIDTH)),
        pl.BlockSpec((B_V_WIDTH, D), lambda bi, j: (0, 0), pipeline_mode=pl.Buffered(1)),
        const((1, D))]
    return pl.pallas_call(
        functools.partial(_gla_scan_kernel, nt),
        grid=(b, 2 * nt),
        in_specs=in_specs,
        out_specs=out_rows(D),
        out_shape=jax.ShapeDtypeStruct((b, s, D), F32),
        scratch_shapes=[pltpu.VMEM((B_HEADS, B_KEY_DIM, B_VAL_DIM), F32),
                        pltpu.VMEM((nt, SCAN_TILE, B_V_WIDTH), F32),
                        pltpu.VMEM((B_V_WIDTH, D), BF16)],
        compiler_params=_params("arbitrary", "arbitrary", vmem_limit=SCAN_VMEM_LIMIT),
        name="gla_scan",
    )(*fwd, *bwd, v, s_f, s_b, sg, x, mod3, head_g.reshape(1, B_V_WIDTH), w_out, final_g.reshape(1, D))


def _rope_tables(n_tokens):
    rows_n = n_tokens // GRID_W
    row = np.repeat(np.arange(rows_n, dtype=np.float32), GRID_W)
    col = np.tile(np.arange(GRID_W, dtype=np.float32), rows_n)
    inv_freq = (np.float32(ROPE_BASE) ** (-np.arange(ROPE_FREQS, dtype=np.float32) / np.float32(ROPE_FREQS)))
    inv_freq = inv_freq.astype(np.float32)
    ang = np.stack([row[:, None] * inv_freq, col[:, None] * inv_freq], axis=1)
    cos, sin = np.cos(ang).astype(np.float32), np.sin(ang).astype(np.float32)
    zero = np.zeros_like(sin)
    tile = lambda t: np.tile(t.reshape(n_tokens, A_HEAD_DIM), (1, LANES // A_HEAD_DIM))
    return (tile(np.stack([cos, cos], axis=2)),
            tile(np.stack([-sin, zero], axis=2)),
            tile(np.stack([zero, sin], axis=2)))


def _pair_heads(t, axis):
    shape = t.shape
    t = t.reshape(shape[:axis] + (A_KV_HEADS, A_GROUP, -1) + shape[axis + 1:])
    return jnp.swapaxes(t, axis, axis + 1).reshape(shape)


def kernel(x, c, ctx, c_ctx, l0_norm_g, l0_w_ada, l0_b_ada, l0_w_in, l0_sink, l0_w_out, l1_norm_g, l1_w_ada, l1_b_ada, l1_w_in, l1_wa1_f, l1_wa2_f, l1_ba_f, l1_wa1_b, l1_wa2_b, l1_ba_b, l1_head_norm_g, l1_w_out, final_norm_g):
    b, s, _ = x.shape
    ctx_row = b

    cvec = jnp.concatenate([c, c_ctx[None, :], jnp.zeros((MOD_ROWS - b - 1, D), F32)], axis=0)
    mod0, mod1 = (m.reshape(MOD_ROWS, 1, 3 * D) for m in _modulation(cvec, l0_w_ada, l0_b_ada, l1_w_ada, l1_b_ada))

    sink = _pair_heads(l0_sink.astype(F32), 0)
    kc, vct, xc1, w0_in, w0_out = _ctx_attention(sink, ctx, mod0, ctx_row, l0_norm_g, l0_w_in, l0_w_out)
    qt, k, vt, sg = _inproj_attn(x, mod0, l0_norm_g, w0_in, _rope_tables(s))
    x1 = _attention(sink, qt, k, vt, kc, vct, sg, x, mod0, w0_out)

    wa1 = jnp.concatenate([l1_wa1_f, l1_wa1_b, jnp.zeros((D, LANES - 2 * GATE_RANK), F32)], axis=1).astype(BF16)
    wa2 = jnp.zeros((LANES, 2 * B_K_WIDTH), F32)
    wa2 = wa2.at[0:GATE_RANK, 0:B_K_WIDTH].set(l1_wa2_f)
    wa2 = wa2.at[GATE_RANK:2 * GATE_RANK, B_K_WIDTH:].set(l1_wa2_b).astype(BF16)
    ba = jnp.concatenate([l1_ba_f, l1_ba_b]).reshape(1, 2 * B_K_WIDTH)
    s_f, s_b, w1 = _inproj_gla(xc1, mod1, ctx_row, l1_norm_g, l1_w_in, wa1, wa2, ba, ctx.shape[1], False)
    outs = _inproj_gla(x1, mod1, None, l1_norm_g, w1, wa1, wa2, ba, CHUNK, True)
    v1, sg1 = outs[6:]
    return _gla_scan(outs[0:3], outs[3:6], v1, s_f, s_b, sg1, x1, mod1,
                     l1_head_norm_g, l1_w_out, final_norm_g)
```

```python
import functools

import jax
import jax.numpy as jnp
import numpy as np
from jax import lax
from jax.experimental import pallas as pl
from jax.experimental.pallas import tpu as pltpu

F32 = jnp.float32
BF16 = jnp.bfloat16

D = 1024
GRID_W = 64
EPS = 1e-6
NEG_INF = -1e30

A_HEADS = 16
A_KV_HEADS = 2
A_GROUP = A_HEADS // A_KV_HEADS
A_HEAD_DIM = 64
A_WIDTH = A_HEADS * A_HEAD_DIM
A_KV_WIDTH = A_KV_HEADS * A_HEAD_DIM
BLOCK = 128
ROPE_BASE = 10000.0
ROPE_FREQS = A_HEAD_DIM // 4
Q_SCALE = A_HEAD_DIM ** -0.5
LOG2_E = 1.4426950408889634

B_HEADS = 4
B_K_WIDTH = D // 2
B_V_WIDTH = D
B_KEY_DIM = B_K_WIDTH // B_HEADS
B_VAL_DIM = B_V_WIDTH // B_HEADS
GATE_RANK = 16
GATE_TEMP = 16.0
CHUNK = 64
K_SCALE = B_KEY_DIM ** -0.5

LANES = 128
MXU_DIM = 256
MOD_ROWS = 16
ROW_TILE = 1024
A_ROW_TILE = 1024
SCAN_TILE = 1024
SUB_ROWS = 256
DEC_ROWS = ROW_TILE // CHUNK
SUM_ROWS = 16
A_BLOCKS_PER_STEP = 8
VMEM_LIMIT = 48 * 1024 * 1024
SCAN_VMEM_LIMIT = 56 * 1024 * 1024


def _params(*sem, vmem_limit=VMEM_LIMIT):
    return pltpu.CompilerParams(dimension_semantics=sem, vmem_limit_bytes=vmem_limit)


def _silu(x):
    return x / (1.0 + jnp.exp(-x))


def _dot(a, b):
    return jnp.dot(a, b, preferred_element_type=F32)


def _dot_nt(a, b):
    return lax.dot_general(a, b, (((1,), (1,)), ((), ())), preferred_element_type=F32)


def _dot_tn(a, b):
    return lax.dot_general(a, b, (((0,), (0,)), ((), ())), preferred_element_type=F32)


def _norm_mod(x, g, m):
    ms = jnp.mean(x * x, axis=-1, keepdims=True)
    y = x * lax.rsqrt(ms + EPS) * g
    return y * (1.0 + m[:, D:2 * D]) + m[:, 0:D]


def _cast_once(src_ref, dst_ref):
    @pl.when((pl.program_id(0) == 0) & (pl.program_id(1) == 0))
    def _():
        def body(i, carry):
            rows = pl.ds(pl.multiple_of(i * LANES, LANES), LANES)
            dst_ref[rows, :] = src_ref[rows, :].astype(dst_ref.dtype)
            return carry
        lax.fori_loop(0, src_ref.shape[0] // LANES, body, 0)


def _mod_kernel(n_tiles, c_ref, w0_ref, b0_ref, w1_ref, b1_ref, o0_ref, o1_ref):
    s = _silu(c_ref[...]).astype(BF16)
    j = pl.program_id(0)

    @pl.when(j < n_tiles)
    def _():
        o0_ref[...] = _dot(s, w0_ref[...].astype(BF16)) + b0_ref[...]

    @pl.when(j >= n_tiles)
    def _():
        o1_ref[...] = _dot(s, w1_ref[...].astype(BF16)) + b1_ref[...]


def _modulation(cvec, w_ada0, b_ada0, w_ada1, b_ada1):
    n = w_ada0.shape[1] // D
    first = lambda j: (0, jnp.minimum(j, n - 1))
    second = lambda j: (0, jnp.maximum(j - n, 0))
    out = jax.ShapeDtypeStruct((MOD_ROWS, n * D), F32)
    return pl.pallas_call(
        functools.partial(_mod_kernel, n),
        grid=(2 * n,),
        in_specs=[
            pl.BlockSpec((MOD_ROWS, D), lambda j: (0, 0)),
            pl.BlockSpec((D, D), first), pl.BlockSpec((1, D), first),
            pl.BlockSpec((D, D), second), pl.BlockSpec((1, D), second),
        ],
        out_specs=(pl.BlockSpec((MOD_ROWS, D), first), pl.BlockSpec((MOD_ROWS, D), second)),
        out_shape=(out, out),
        compiler_params=_params("arbitrary"),
        name="modulation",
    )(cvec, w_ada0, b_ada0.reshape(1, n * D), w_ada1, b_ada1.reshape(1, n * D))


def _rope(t, cos, sin_hi, sin_lo):
    return t * cos + pltpu.roll(t, LANES - ROPE_FREQS, 1) * sin_hi + pltpu.roll(t, ROPE_FREQS, 1) * sin_lo


def _pair_cast_columns(src_ref, dst_ref):
    @pl.when((pl.program_id(0) == 0) & (pl.program_id(1) == 0))
    def _():
        kv_lo, kv_hi = A_WIDTH, A_WIDTH + 2 * A_KV_WIDTH

        def body(i, carry):
            rows = pl.ds(pl.multiple_of(i * LANES, LANES), LANES)
            src = src_ref[rows, :]
            dst_ref[rows, kv_lo:kv_hi] = src[:, kv_lo:kv_hi].astype(BF16)
            for base in (0, kv_hi):
                for p in range(A_GROUP):
                    a = base + p * A_HEAD_DIM
                    b = base + (p + A_GROUP) * A_HEAD_DIM
                    pair = jnp.concatenate([src[:, a:a + A_HEAD_DIM], src[:, b:b + A_HEAD_DIM]], axis=1)
                    dst_ref[rows, base + p * LANES:base + (p + 1) * LANES] = pair.astype(BF16)
            return carry
        lax.fori_loop(0, src_ref.shape[0] // LANES, body, 0)


def _inproj_attn_kernel(rope, x_ref, m_ref, g_ref, w_ref, *rest):
    wq_ref = w_ref.at[:, 0:A_WIDTH]
    wkv_ref = w_ref.at[:, A_WIDTH:A_WIDTH + 2 * A_KV_WIDTH]
    wg_ref = w_ref.at[:, A_WIDTH + 2 * A_KV_WIDTH:]
    if rope:
        cos_ref, shi_ref, slo_ref, qt_ref, k_ref, vt_ref, sg_ref = rest
    else:
        qt_ref, k_ref, vt_ref, sg_ref = rest
    assert x_ref.shape[1] % SUB_ROWS == 0
    for r in range(x_ref.shape[1] // SUB_ROWS):
        rows = slice(r * SUB_ROWS, (r + 1) * SUB_ROWS)
        if rope:
            pos = pl.ds(pl.multiple_of(pl.program_id(1) * x_ref.shape[1] + r * SUB_ROWS, SUB_ROWS), SUB_ROWS)
            cos, shi, slo = cos_ref[pos], shi_ref[pos], slo_ref[pos]
        hb = _norm_mod(x_ref[0, rows], g_ref[...], m_ref[0]).astype(BF16)
        q = _dot(hb, wq_ref[...])
        for j in range(A_WIDTH // LANES):
            qj = q[:, j * LANES:(j + 1) * LANES]
            if rope:
                qj = _rope(qj, cos, shi, slo)
            qt_ref[0, j * LANES:(j + 1) * LANES, rows] = (qj * (Q_SCALE * LOG2_E)).T.astype(BF16)
        kv = _dot(hb, wkv_ref[...])
        k = kv[:, 0:A_KV_WIDTH]
        if rope:
            k = _rope(k, cos, shi, slo)
        k_ref[0, rows] = k.astype(BF16)
        vt_ref[0, :, rows] = kv[:, A_KV_WIDTH:].T.astype(BF16)
        g = _dot(hb, wg_ref[...])
        sg_ref[0, rows] = _silu(g).astype(BF16)


def _inproj_attn(x, mod3, norm_g, w_in, tables):
    b, s, _ = x.shape
    tm = min(A_ROW_TILE, s)
    in_specs = [
        pl.BlockSpec((1, tm, D), lambda bi, i: (bi, i, 0)),
        pl.BlockSpec((1, 1, 3 * D), lambda bi, i: (bi, 0, 0)),
        pl.BlockSpec((1, D), lambda bi, i: (0, 0)),
        pl.BlockSpec(w_in.shape, lambda bi, i: (0, 0)),
    ] + [pl.BlockSpec((s, LANES), lambda bi, i: (0, 0))] * 3
    args = [x, mod3, norm_g.reshape(1, D), w_in, *tables]
    out_shape = (
        jax.ShapeDtypeStruct((b, A_WIDTH, s), BF16),
        jax.ShapeDtypeStruct((b, s, A_KV_WIDTH), BF16),
        jax.ShapeDtypeStruct((b, A_KV_WIDTH, s), BF16),
        jax.ShapeDtypeStruct((b, s, A_WIDTH), BF16),
    )
    out_specs = (
        pl.BlockSpec((1, A_WIDTH, tm), lambda bi, i: (bi, 0, i)),
        pl.BlockSpec((1, tm, A_KV_WIDTH), lambda bi, i: (bi, i, 0)),
        pl.BlockSpec((1, A_KV_WIDTH, tm), lambda bi, i: (bi, 0, i)),
        pl.BlockSpec((1, tm, A_WIDTH), lambda bi, i: (bi, i, 0)),
    )
    return pl.pallas_call(
        functools.partial(_inproj_attn_kernel, True),
        grid=(b, s // tm),
        in_specs=in_specs,
        out_specs=out_specs,
        out_shape=out_shape,
        compiler_params=_params("parallel", "parallel"),
        name="inproj_attn",
    )(*args)


def _attn_block(kwin, vtw, qts, sinks, ok_prev, ok_next):
    n_keys, nq = kwin.shape[0], qts.shape[1]
    lane = lax.broadcasted_iota(jnp.int32, kwin.shape, 1)
    kbd = jnp.concatenate([jnp.where(lane < A_HEAD_DIM, kwin, jnp.zeros_like(kwin)),
                           jnp.where(lane >= A_HEAD_DIM, kwin, jnp.zeros_like(kwin))], axis=0)
    ones = jnp.where(lax.broadcasted_iota(jnp.int32, (SUM_ROWS, n_keys), 0) == 0, 1.0, 0.0).astype(BF16)

    st = _dot(kbd, qts).astype(BF16)
    neg = jnp.asarray(NEG_INF, BF16)
    outs = []
    for hh in range(2):
        sh = st[hh * n_keys:(hh + 1) * n_keys]
        if ok_prev is not None:
            parts = [jnp.where(ok_prev, sh[0:BLOCK], neg),
                     sh[BLOCK:2 * BLOCK],
                     jnp.where(ok_next, sh[2 * BLOCK:3 * BLOCK], neg),
                     sh[3 * BLOCK:]]
        else:
            parts = [sh]
        mx = functools.reduce(jnp.maximum, [jnp.max(t, axis=0, keepdims=True) for t in parts])
        mxb = jnp.maximum(mx.astype(F32), sinks[hh]).astype(BF16)
        mx = mxb.astype(F32)
        probs = jnp.concatenate([jnp.exp2(t - mxb) for t in parts], axis=0)
        vt_h = jnp.concatenate([vtw[hh * A_HEAD_DIM:(hh + 1) * A_HEAD_DIM], ones], axis=0)
        ot = _dot(vt_h, probs)
        den = ot[A_HEAD_DIM:A_HEAD_DIM + 1] + jnp.exp2(sinks[hh] - mx)
        outs.append(ot[0:A_HEAD_DIM] * (1.0 / den))
    return jnp.concatenate(outs, axis=0)


def _pair_cast_rows(src_ref, dst_ref):
    @pl.when((pl.program_id(0) == 0) & (pl.program_id(1) == 0))
    def _():
        for p in range(A_GROUP):
            for half, h in enumerate((p, p + A_GROUP)):
                lo = p * LANES + half * A_HEAD_DIM
                dst_ref[lo:lo + A_HEAD_DIM, :] = src_ref[h * A_HEAD_DIM:(h + 1) * A_HEAD_DIM, :].astype(BF16)


def _attn_kernel(local, n_steps, sink_ref, qt_ref, *rest):
    n_blk = rest[-1].shape[1] // BLOCK
    if local:
        (kp_ref, kc_ref, kn_ref, vp_ref, vc_ref, vn_ref, kx_ref, vx_ref,
         sg_ref, x_ref, m_ref, w_ref, o_ref) = rest
        last = slice((n_blk - 1) * BLOCK, n_blk * BLOCK)
        k_blocks = ([kp_ref[0, last]] + [kc_ref[0, i * BLOCK:(i + 1) * BLOCK] for i in range(n_blk)]
                    + [kn_ref[0, 0:BLOCK]])
        v_blocks = ([vp_ref[0, :, last]] + [vc_ref[0, :, i * BLOCK:(i + 1) * BLOCK] for i in range(n_blk)]
                    + [vn_ref[0, :, 0:BLOCK]])
    else:
        kx_ref, vx_ref, sg_ref, x_ref, m_ref, w_ref, o_ref = rest
    step = pl.program_id(1)
    nq = A_GROUP * BLOCK
    chunk_of = lax.broadcasted_iota(jnp.int32, (1, nq), 1) // BLOCK
    sinks = []
    for hh in range(2):
        sk = jnp.full((1, nq), sink_ref[hh] * LOG2_E, F32)
        for c in range(1, A_GROUP):
            sk = jnp.where(chunk_of == c, sink_ref[2 * c + hh] * LOG2_E, sk)
        sinks.append(sk)
    if local:
        kj = lax.broadcasted_iota(jnp.int32, (BLOCK, nq), 0)
        qi = lax.broadcasted_iota(jnp.int32, (BLOCK, nq), 1) % BLOCK

    rows_out = []
    for blk in range(n_blk):
        qts = jnp.concatenate([qt_ref[0, c * LANES:(c + 1) * LANES, blk * BLOCK:(blk + 1) * BLOCK]
                               for c in range(A_GROUP)], axis=1)
        if local:
            kwin = jnp.concatenate(k_blocks[blk:blk + 3] + [kx_ref[0]], axis=0)
            vtw = jnp.concatenate(v_blocks[blk:blk + 3] + [vx_ref[0]], axis=1)
            ok_prev = (kj >= qi) & (step > 0) if blk == 0 else (kj >= qi)
            ok_next = (kj <= qi) & (step < n_steps - 1) if blk == n_blk - 1 else (kj <= qi)
        else:
            kwin, vtw, ok_prev, ok_next = kx_ref[0], vx_ref[0], None, None
        ot = _attn_block(kwin, vtw, qts, sinks, ok_prev, ok_next)
        rows = slice(blk * BLOCK, (blk + 1) * BLOCK)
        outs = []
        for c in range(A_GROUP):
            sg = sg_ref[0, rows, c * LANES:(c + 1) * LANES].astype(F32)
            outs.append((ot[:, c * BLOCK:(c + 1) * BLOCK].T * sg).astype(BF16))
        rows_out.append(jnp.concatenate(outs, axis=1))
    y = _dot(jnp.concatenate(rows_out, axis=0), w_ref[...])
    gate = m_ref[0][:, 2 * D:3 * D]
    o_ref[0] = x_ref[0] + gate * y


def _attention(sink, qt, k, vt, kx, vxt, sg, x, mod3, w_out):
    b, s, _ = sg.shape
    rows = A_BLOCKS_PER_STEP * BLOCK
    assert s % rows == 0
    ns = s // rows
    n_ctx = kx.shape[1]
    blk = lambda w: pl.BlockSpec((1, rows, w), lambda bi, i: (bi, i, 0))
    blk_t = lambda w: pl.BlockSpec((1, w, rows), lambda bi, i: (bi, 0, i))
    lo = lambda i: jnp.maximum(i - 1, 0)
    hi = lambda i: jnp.minimum(i + 1, ns - 1)
    in_specs = [pl.BlockSpec(memory_space=pltpu.SMEM), blk_t(A_WIDTH),
                pl.BlockSpec((1, rows, A_KV_WIDTH), lambda bi, i: (bi, lo(i), 0)),
                blk(A_KV_WIDTH),
                pl.BlockSpec((1, rows, A_KV_WIDTH), lambda bi, i: (bi, hi(i), 0)),
                pl.BlockSpec((1, A_KV_WIDTH, rows), lambda bi, i: (bi, 0, lo(i))),
                blk_t(A_KV_WIDTH),
                pl.BlockSpec((1, A_KV_WIDTH, rows), lambda bi, i: (bi, 0, hi(i))),
                pl.BlockSpec((1, n_ctx, A_KV_WIDTH), lambda bi, i: (bi, 0, 0)),
                pl.BlockSpec((1, A_KV_WIDTH, n_ctx), lambda bi, i: (bi, 0, 0)),
                blk(A_WIDTH), blk(D),
                pl.BlockSpec((1, 1, 3 * D), lambda bi, i: (bi, 0, 0)),
                pl.BlockSpec((A_WIDTH, D), lambda bi, i: (0, 0))]
    return pl.pallas_call(
        functools.partial(_attn_kernel, True, ns),
        grid=(b, ns),
        in_specs=in_specs,
        out_specs=blk(D),
        out_shape=jax.ShapeDtypeStruct((b, s, D), F32),
        compiler_params=_params("parallel", "parallel"),
        name="attn_local",
    )(sink, qt, k, k, k, vt, vt, vt, kx, vxt, sg, x, mod3, w_out)


def _ctx_attn_kernel(sink_ref, x_ref, m_ref, g_ref, win_ref, wout_ref, k_ref, vt_ref, o_ref,
                     win_bf_ref, wout_bf_ref, qt_scr, sg_scr):
    _pair_cast_columns(win_ref, win_bf_ref)
    _pair_cast_rows(wout_ref, wout_bf_ref)
    _inproj_attn_kernel(False, x_ref, m_ref, g_ref, win_bf_ref, qt_scr, k_ref, vt_ref, sg_scr)
    _attn_kernel(False, 1, sink_ref, qt_scr, k_ref, vt_ref, sg_scr, x_ref, m_ref, wout_bf_ref, o_ref)


def _ctx_attention(sink, xc, mod3, mod_row, norm_g, w_in, w_out):
    b, n, _ = xc.shape
    const = lambda shape, **kw: pl.BlockSpec(shape, lambda bi, i: (0,) * len(shape), **kw)
    return pl.pallas_call(
        _ctx_attn_kernel,
        grid=(b, 1),
        in_specs=[pl.BlockSpec(memory_space=pltpu.SMEM),
                  pl.BlockSpec((1, n, D), lambda bi, i: (bi, 0, 0)),
                  pl.BlockSpec((1, 1, 3 * D), lambda bi, i: (mod_row, 0, 0)),
                  const((1, D)),
                  const(w_in.shape, pipeline_mode=pl.Buffered(1)),
                  const(w_out.shape, pipeline_mode=pl.Buffered(1))],
        out_specs=(pl.BlockSpec((1, n, A_KV_WIDTH), lambda bi, i: (bi, 0, 0)),
                   pl.BlockSpec((1, A_KV_WIDTH, n), lambda bi, i: (bi, 0, 0)),
                   pl.BlockSpec((1, n, D), lambda bi, i: (bi, 0, 0)),
                   const(w_in.shape), const(w_out.shape)),
        out_shape=(jax.ShapeDtypeStruct((b, n, A_KV_WIDTH), BF16),
                   jax.ShapeDtypeStruct((b, A_KV_WIDTH, n), BF16),
                   jax.ShapeDtypeStruct((b, n, D), F32),
                   jax.ShapeDtypeStruct(w_in.shape, BF16), jax.ShapeDtypeStruct(w_out.shape, BF16)),
        scratch_shapes=[pltpu.VMEM((1, A_WIDTH, n), BF16), pltpu.VMEM((1, n, A_WIDTH), BF16)],
        compiler_params=_params("arbitrary", "arbitrary"),
        name="ctx_attn",
    )(sink, xc, mod3, norm_g.reshape(1, D), w_in, w_out)


def _chunk_cumsum(x, tri):
    return _dot(tri, x.astype(BF16))


def _inproj_gla_kernel(chunk, with_q, x_ref, m_ref, g_ref, w_in_ref, wa1_ref, wa2_ref, ba_ref, tri_ref, *outs):
    if with_q:
        w_ref = w_in_ref
        per_dir = (outs[0:3], outs[3:6])
        v_ref, sg_ref = outs[6:]
    else:
        w_ref = outs[2]
        _cast_once(w_in_ref, w_ref)
        per_dir = ((outs[0],), (outs[1],))
    tm = x_ref.shape[1]
    hb = _norm_mod(x_ref[0], g_ref[...], m_ref[0]).astype(BF16)
    r = _dot(hb, wa1_ref[...])
    z = _dot(r.astype(BF16), wa2_ref[...]) + ba_ref[...]
    e = jnp.exp2(jnp.abs(z) * -LOG2_E)
    la = (jnp.minimum(z, 0.0) * LOG2_E - jnp.log2(1.0 + e)) * (1.0 / GATE_TEMP)
    k = _dot(hb, w_ref[:, B_K_WIDTH:2 * B_K_WIDTH])
    if with_q:
        q = _dot(hb, w_ref[:, 0:B_K_WIDTH]) * K_SCALE
    n_chunks = tm // chunk
    k_ends = []
    for reverse in (False, True):
        lad = la[:, B_K_WIDTH:] if reverse else la[:, 0:B_K_WIDTH]
        refs = per_dir[1] if reverse else per_dir[0]
        tots, kis, k_end = [], [], None
        tri = tri_ref[1 if reverse else 0]
        tb = tri.shape[0]
        cum_all = jnp.concatenate([_chunk_cumsum(lad[r * tb:(r + 1) * tb], tri) for r in range(tm // tb)], axis=0)
        for c in range(n_chunks):
            rows = slice(c * chunk, (c + 1) * chunk)
            cum = cum_all[rows]
            tot = cum[0:1] if reverse else cum[chunk - 1:chunk]
            if not with_q:
                k_end = (k[rows] * jnp.exp2(tot - cum)).astype(BF16)
                continue
            refs[0][0, rows] = (q[rows] * jnp.exp2(cum)).astype(BF16)
            kis.append(k[rows] * jnp.exp2(-cum))
            tots.append(tot)
            if c % 2 == 1:
                pair = slice((c - 1) * chunk, (c + 1) * chunk)
                refs[1][0, :, pair] = jnp.concatenate(kis[-2:], axis=0).T.astype(BF16)
        if with_q:
            pad = [jnp.zeros((DEC_ROWS - n_chunks, B_K_WIDTH), F32)] if n_chunks < DEC_ROWS else []
            refs[2][0, 0] = jnp.exp2(jnp.concatenate(tots + pad, axis=0))
        else:
            k_ends.append(k_end)
    if with_q:
        sg_ref[0] = _silu(_dot(hb, w_ref[:, 2 * B_K_WIDTH + B_V_WIDTH:])).astype(BF16)
    v = _dot(hb, w_ref[:, 2 * B_K_WIDTH:2 * B_K_WIDTH + B_V_WIDTH]).astype(BF16)
    if with_q:
        v_ref[0] = v
    else:
        for refs, k_end in zip(per_dir, k_ends):
            for h in range(B_HEADS):
                ks = slice(h * B_KEY_DIM, (h + 1) * B_KEY_DIM)
                refs[0][0, h] = _dot_tn(k_end[:, ks], v[:, h * B_VAL_DIM:(h + 1) * B_VAL_DIM])


def _inproj_gla(x, mod3, mod_row, norm_g, w, wa1, wa2, ba, chunk, with_q):
    b, s, _ = x.shape
    tm = min(ROW_TILE, s)
    nt = s // tm
    cpt = tm // chunk
    tb = max(chunk, MXU_DIM)
    assert tm % tb == 0 and tb % chunk == 0
    if mod_row is None:
        mod_map = lambda bi, i: (bi, 0, 0)
    else:
        mod_map = lambda bi, i: (mod_row, 0, 0)
    const = lambda shape, **kw: pl.BlockSpec(shape, lambda bi, i: (0,) * len(shape), **kw)
    in_specs = [
        pl.BlockSpec((1, tm, D), lambda bi, i: (bi, i, 0)),
        pl.BlockSpec((1, 1, 3 * D), mod_map),
        const((1, D)), const(w.shape, **({} if with_q else {"pipeline_mode": pl.Buffered(1)})),
        const(wa1.shape), const(wa2.shape), const(ba.shape),
        const((2, tb, tb)),
    ]
    t_idx = np.arange(tb)
    same = (t_idx[:, None] // chunk) == (t_idx[None, :] // chunk)
    lower = same & (t_idx[None, :] <= t_idx[:, None])
    tri = jnp.asarray(np.stack([lower, lower.T]), dtype=BF16)
    rows = lambda w_: (jax.ShapeDtypeStruct((b, s, w_), BF16),
                       pl.BlockSpec((1, tm, w_), lambda bi, i: (bi, i, 0)))
    cols = (jax.ShapeDtypeStruct((b, B_K_WIDTH, s), BF16),
            pl.BlockSpec((1, B_K_WIDTH, tm), lambda bi, i: (bi, 0, i)))
    decs = (jax.ShapeDtypeStruct((b, nt, DEC_ROWS, B_K_WIDTH), F32),
            pl.BlockSpec((1, 1, DEC_ROWS, B_K_WIDTH), lambda bi, i: (bi, i, 0, 0)))
    if with_q:
        assert cpt <= DEC_ROWS and cpt % 2 == 0
        outs = [rows(B_K_WIDTH), cols, decs] * 2 + [rows(B_V_WIDTH)] * 2
    else:
        assert nt == 1 and cpt == 1
        state = (jax.ShapeDtypeStruct((b, B_HEADS, B_KEY_DIM, B_VAL_DIM), F32),
                 pl.BlockSpec((1, B_HEADS, B_KEY_DIM, B_VAL_DIM), lambda bi, i: (bi, 0, 0, 0)))
        outs = [state] * 2 + [(jax.ShapeDtypeStruct(w.shape, BF16), const(w.shape))]
    return pl.pallas_call(
        functools.partial(_inproj_gla_kernel, chunk, with_q),
        grid=(b, nt),
        in_specs=in_specs,
        out_specs=tuple(o[1] for o in outs),
        out_shape=tuple(o[0] for o in outs),
        compiler_params=_params("parallel", "parallel") if with_q else _params("arbitrary", "arbitrary"),
        name="inproj_gla" if with_q else "inproj_gla_ctx",
    )(x, mod3, norm_g.reshape(1, D), w, wa1, wa2, ba, tri)


def _scan_tile(reverse, qd_ref, kit_ref, dec_ref, v_ref, st_ref):
    pair_rows = 2 * CHUNK
    ti = lax.broadcasted_iota(jnp.int32, (pair_rows, pair_rows), 0)
    si = lax.broadcasted_iota(jnp.int32, (pair_rows, pair_rows), 1)
    same = (ti // CHUNK) == (si // CHUNK)
    if reverse:
        use_inv = same & (ti <= si)
        use_end = (ti < CHUNK) & (si >= CHUNK)
        second_row = lax.broadcasted_iota(jnp.int32, (pair_rows, B_KEY_DIM), 0) < CHUNK
        first_col = lax.broadcasted_iota(jnp.int32, (B_KEY_DIM, pair_rows), 1) >= CHUNK
    else:
        use_inv = same & (ti >= si)
        use_end = (ti >= CHUNK) & (si < CHUNK)
        second_row = lax.broadcasted_iota(jnp.int32, (pair_rows, B_KEY_DIM), 0) >= CHUNK
        first_col = lax.broadcasted_iota(jnp.int32, (B_KEY_DIM, pair_rows), 1) < CHUNK
    n_pairs = SCAN_TILE // pair_rows
    order = range(n_pairs - 1, -1, -1) if reverse else range(n_pairs)
    n_dec = dec_ref.shape[1] * DEC_ROWS
    dec = dec_ref[0].reshape(n_dec, B_K_WIDTH)
    dect = jnp.concatenate([dec, jnp.zeros((LANES - n_dec, B_K_WIDTH), F32)], axis=0).T
    heads = []
    for h in range(B_HEADS):
        ks = slice(h * B_KEY_DIM, (h + 1) * B_KEY_DIM)
        vs = slice(h * B_VAL_DIM, (h + 1) * B_VAL_DIM)
        st = st_ref[h]
        o_rows = [None] * n_pairs
        for p in order:
            rows = slice(p * pair_rows, (p + 1) * pair_rows)
            c_first, c_second = (2 * p + 1, 2 * p) if reverse else (2 * p, 2 * p + 1)
            qd = qd_ref[0, rows, ks]
            kit = kit_ref[0, ks, rows]
            vh = v_ref[0, rows, vs]
            d_first, d_second = dect[ks, c_first:c_first + 1], dect[ks, c_second:c_second + 1]
            ki32 = kit.astype(F32)
            s2 = _dot(qd, jnp.concatenate([kit, (ki32 * d_first).astype(BF16)], axis=1))
            a = jnp.where(use_inv, s2[:, 0:pair_rows], jnp.where(use_end, s2[:, pair_rows:], 0.0)).astype(BF16)
            q_pair = jnp.where(second_row, qd.astype(F32) * dec[c_first:c_first + 1, ks], qd.astype(F32))
            k_pair = ki32 * jnp.where(first_col, d_first * d_second, d_second)
            o_rows[p] = _dot(a, vh) + _dot(q_pair.astype(BF16), st.astype(BF16))
            st = st * (d_first * d_second) + _dot(k_pair.astype(BF16), vh)
        st_ref[h] = st
        heads.append(jnp.concatenate(o_rows, axis=0))
    return jnp.concatenate(heads, axis=1)


def _gla_scan_kernel(nt, qdf_ref, kitf_ref, decf_ref, qdb_ref, kitb_ref, decb_ref,
                     v_ref, sf_ref, sb_ref, sg_ref, x_ref, m_ref, hn_ref, w32_ref, fn_ref,
                     o_ref, st_ref, of_ref, w_ref):
    _cast_once(w32_ref, w_ref)
    j = pl.program_id(1)

    @pl.when(j == 0)
    def _():
        st_ref[...] = sf_ref[0]

    @pl.when(j == nt)
    def _():
        st_ref[...] = sb_ref[0]

    @pl.when(j < nt)
    def _():
        of_ref[j] = _scan_tile(False, qdf_ref, kitf_ref, decf_ref, v_ref, st_ref)

    @pl.when(j >= nt)
    def _():
        o = _scan_tile(True, qdb_ref, kitb_ref, decb_ref, v_ref, st_ref) + of_ref[2 * nt - 1 - j]
        normed = []
        for h in range(B_HEADS):
            oh = o[:, h * B_VAL_DIM:(h + 1) * B_VAL_DIM]
            normed.append(oh * lax.rsqrt(jnp.mean(oh * oh, axis=-1, keepdims=True) + EPS))
        of = jnp.concatenate(normed, axis=1) * hn_ref[...]
        y = _dot((of * sg_ref[0].astype(F32)).astype(BF16), w_ref[...])
        xn = x_ref[0] + m_ref[0][:, 2 * D:3 * D] * y
        ms = jnp.mean(xn * xn, axis=-1, keepdims=True)
        o_ref[0] = xn * lax.rsqrt(ms + EPS) * fn_ref[...]


def _gla_scan(fwd, bwd, v, s_f, s_b, sg, x, mod3, head_g, w_out, final_g):
    b, s, _ = v.shape
    assert SCAN_TILE % ROW_TILE == 0 and ROW_TILE // CHUNK == DEC_ROWS
    dec_blocks = SCAN_TILE // ROW_TILE
    nt = s // SCAN_TILE
    t_fwd = lambda j: jnp.minimum(j, nt - 1)
    t_bwd = lambda j: 2 * nt - 1 - jnp.maximum(j, nt)
    t_both = lambda j: jnp.where(j < nt, j, 2 * nt - 1 - j)

    def direction(t):
        rows = pl.BlockSpec((1, SCAN_TILE, B_K_WIDTH), lambda bi, j: (bi, t(j), 0))
        cols = pl.BlockSpec((1, B_K_WIDTH, SCAN_TILE), lambda bi, j: (bi, 0, t(j)))
        decs = pl.BlockSpec((1, dec_blocks, DEC_ROWS, B_K_WIDTH), lambda bi, j: (bi, t(j), 0, 0))
        return [rows, cols, decs]

    const = lambda shape: pl.BlockSpec(shape, lambda bi, j: (0,) * len(shape))
    state = pl.BlockSpec((1, B_HEADS, B_KEY_DIM, B_VAL_DIM), lambda bi, j: (bi, 0, 0, 0))
    out_rows = lambda w_: pl.BlockSpec((1, SCAN_TILE, w_), lambda bi, j: (bi, t_bwd(j), 0))
    in_specs = direction(t_fwd) + direction(t_bwd) + [
        pl.BlockSpec((1, SCAN_TILE, B_V_WIDTH), lambda bi, j: (bi, t_both(j), 0)),
        state, state, out_rows(B_V_WIDTH), out_rows(D),
        pl.BlockSpec((1, 1, 3 * D), lambda bi, j: (bi, 0, 0)),
        const((1, B_V_WIDTH)),
        pl.BlockSpec((B_V_WIDTH, D), lambda bi, j: (0, 0), pipeline_mode=pl.Buffered(1)),
        const((1, D))]
    return pl.pallas_call(
        functools.partial(_gla_scan_kernel, nt),
        grid=(b, 2 * nt),
        in_specs=in_specs,
        out_specs=out_rows(D),
        out_shape=jax.ShapeDtypeStruct((b, s, D), F32),
        scratch_shapes=[pltpu.VMEM((B_HEADS, B_KEY_DIM, B_VAL_DIM), F32),
                        pltpu.VMEM((nt, SCAN_TILE, B_V_WIDTH), F32),
                        pltpu.VMEM((B_V_WIDTH, D), BF16)],
        compiler_params=_params("arbitrary", "arbitrary", vmem_limit=SCAN_VMEM_LIMIT),
        name="gla_scan",
    )(*fwd, *bwd, v, s_f, s_b, sg, x, mod3, head_g.reshape(1, B_V_WIDTH), w_out, final_g.reshape(1, D))


def _rope_tables(n_tokens):
    rows_n = n_tokens // GRID_W
    row = np.repeat(np.arange(rows_n, dtype=np.float32), GRID_W)
    col = np.tile(np.arange(GRID_W, dtype=np.float32), rows_n)
    inv_freq = (np.float32(ROPE_BASE) ** (-np.arange(ROPE_FREQS, dtype=np.float32) / np.float32(ROPE_FREQS)))
    inv_freq = inv_freq.astype(np.float32)
    ang = np.stack([row[:, None] * inv_freq, col[:, None] * inv_freq], axis=1)
    cos, sin = np.cos(ang).astype(np.float32), np.sin(ang).astype(np.float32)
    zero = np.zeros_like(sin)
    tile = lambda t: np.tile(t.reshape(n_tokens, A_HEAD_DIM), (1, LANES // A_HEAD_DIM))
    return (tile(np.stack([cos, cos], axis=2)),
            tile(np.stack([-sin, zero], axis=2)),
            tile(np.stack([zero, sin], axis=2)))


def _pair_heads(t, axis):
    shape = t.shape
    t = t.reshape(shape[:axis] + (A_KV_HEADS, A_GROUP, -1) + shape[axis + 1:])
    return jnp.swapaxes(t, axis, axis + 1).reshape(shape)


def kernel(x, c, ctx, c_ctx, l0_norm_g, l0_w_ada, l0_b_ada, l0_w_in, l0_sink, l0_w_out, l1_norm_g, l1_w_ada, l1_b_ada, l1_w_in, l1_wa1_f, l1_wa2_f, l1_ba_f, l1_wa1_b, l1_wa2_b, l1_ba_b, l1_head_norm_g, l1_w_out, final_norm_g):
    b, s, _ = x.shape
    ctx_row = b

    cvec = jnp.concatenate([c, c_ctx[None, :], jnp.zeros((MOD_ROWS - b - 1, D), F32)], axis=0)
    mod0, mod1 = (m.reshape(MOD_ROWS, 1, 3 * D) for m in _modulation(cvec, l0_w_ada, l0_b_ada, l1_w_ada, l1_b_ada))

    sink = _pair_heads(l0_sink.astype(F32), 0)
    kc, vct, xc1, w0_in, w0_out = _ctx_attention(sink, ctx, mod0, ctx_row, l0_norm_g, l0_w_in, l0_w_out)
    qt, k, vt, sg = _inproj_attn(x, mod0, l0_norm_g, w0_in, _rope_tables(s))
    x1 = _attention(sink, qt, k, vt, kc, vct, sg, x, mod0, w0_out)

    wa1 = jnp.concatenate([l1_wa1_f, l1_wa1_b, jnp.zeros((D, LANES - 2 * GATE_RANK), F32)], axis=1).astype(BF16)
    wa2 = jnp.zeros((LANES, 2 * B_K_WIDTH), F32)
    wa2 = wa2.at[0:GATE_RANK, 0:B_K_WIDTH].set(l1_wa2_f)
    wa2 = wa2.at[GATE_RANK:2 * GATE_RANK, B_K_WIDTH:].set(l1_wa2_b).astype(BF16)
    ba = jnp.concatenate([l1_ba_f, l1_ba_b]).reshape(1, 2 * B_K_WIDTH)
    s_f, s_b, w1 = _inproj_gla(xc1, mod1, ctx_row, l1_norm_g, l1_w_in, wa1, wa2, ba, ctx.shape[1], False)
    outs = _inproj_gla(x1, mod1, None, l1_norm_g, w1, wa1, wa2, ba, CHUNK, True)
    v1, sg1 = outs[6:]
    return _gla_scan(outs[0:3], outs[3:6], v1, s_f, s_b, sg1, x1, mod1,
                     l1_head_norm_g, l1_w_out, final_norm_g)
```

```python
import functools

import jax
import jax.numpy as jnp
import numpy as np
from jax import lax
from jax.experimental import pallas as pl
from jax.experimental.pallas import tpu as pltpu

F32 = jnp.float32
BF16 = jnp.bfloat16

D = 1024
GRID_W = 64
EPS = 1e-6
NEG_INF = -1e30

A_HEADS = 16
A_KV_HEADS = 2
A_GROUP = A_HEADS // A_KV_HEADS
A_HEAD_DIM = 64
A_WIDTH = A_HEADS * A_HEAD_DIM
A_KV_WIDTH = A_KV_HEADS * A_HEAD_DIM
BLOCK = 128
ROPE_BASE = 10000.0
ROPE_FREQS = A_HEAD_DIM // 4
Q_SCALE = A_HEAD_DIM ** -0.5
LOG2_E = 1.4426950408889634

B_HEADS = 4
B_K_WIDTH = D // 2
B_V_WIDTH = D
B_KEY_DIM = B_K_WIDTH // B_HEADS
B_VAL_DIM = B_V_WIDTH // B_HEADS
GATE_RANK = 16
GATE_TEMP = 16.0
CHUNK = 64
K_SCALE = B_KEY_DIM ** -0.5

LANES = 128
MXU_DIM = 256
MOD_ROWS = 16
ROW_TILE = 1024
A_ROW_TILE = 1024
SCAN_TILE = 1024
SUB_ROWS = 256
DEC_ROWS = ROW_TILE // CHUNK
SUM_ROWS = 16
A_BLOCKS_PER_STEP = 8
VMEM_LIMIT = 40 * 1024 * 1024
SCAN_VMEM_LIMIT = 56 * 1024 * 1024


def _params(*sem, vmem_limit=VMEM_LIMIT):
    return pltpu.CompilerParams(dimension_semantics=sem, vmem_limit_bytes=vmem_limit)


def _silu(x):
    return x / (1.0 + jnp.exp(-x))


def _dot(a, b):
    return jnp.dot(a, b, preferred_element_type=F32)


def _dot_nt(a, b):
    return lax.dot_general(a, b, (((1,), (1,)), ((), ())), preferred_element_type=F32)


def _dot_tn(a, b):
    return lax.dot_general(a, b, (((0,), (0,)), ((), ())), preferred_element_type=F32)


def _norm_mod(x, g, m):
    ms = jnp.mean(x * x, axis=-1, keepdims=True)
    y = x * lax.rsqrt(ms + EPS) * g
    return y * (1.0 + m[:, D:2 * D]) + m[:, 0:D]


def _cast_once(src_ref, dst_ref):
    @pl.when((pl.program_id(0) == 0) & (pl.program_id(1) == 0))
    def _():
        def body(i, carry):
            rows = pl.ds(pl.multiple_of(i * LANES, LANES), LANES)
            dst_ref[rows, :] = src_ref[rows, :].astype(dst_ref.dtype)
            return carry
        lax.fori_loop(0, src_ref.shape[0] // LANES, body, 0)


def _mod_kernel(n_tiles, c_ref, w0_ref, b0_ref, w1_ref, b1_ref, o0_ref, o1_ref):
    s = _silu(c_ref[...]).astype(BF16)
    j = pl.program_id(0)

    @pl.when(j < n_tiles)
    def _():
        o0_ref[...] = _dot(s, w0_ref[...].astype(BF16)) + b0_ref[...]

    @pl.when(j >= n_tiles)
    def _():
        o1_ref[...] = _dot(s, w1_ref[...].astype(BF16)) + b1_ref[...]


def _modulation(cvec, w_ada0, b_ada0, w_ada1, b_ada1):
    n = w_ada0.shape[1] // D
    first = lambda j: (0, jnp.minimum(j, n - 1))
    second = lambda j: (0, jnp.maximum(j - n, 0))
    out = jax.ShapeDtypeStruct((MOD_ROWS, n * D), F32)
    return pl.pallas_call(
        functools.partial(_mod_kernel, n),
        grid=(2 * n,),
        in_specs=[
            pl.BlockSpec((MOD_ROWS, D), lambda j: (0, 0)),
            pl.BlockSpec((D, D), first), pl.BlockSpec((1, D), first),
            pl.BlockSpec((D, D), second), pl.BlockSpec((1, D), second),
        ],
        out_specs=(pl.BlockSpec((MOD_ROWS, D), first), pl.BlockSpec((MOD_ROWS, D), second)),
        out_shape=(out, out),
        compiler_params=_params("arbitrary"),
        name="modulation",
    )(cvec, w_ada0, b_ada0.reshape(1, n * D), w_ada1, b_ada1.reshape(1, n * D))


def _rope(t, cos, sin_hi, sin_lo):
    return t * cos + pltpu.roll(t, LANES - ROPE_FREQS, 1) * sin_hi + pltpu.roll(t, ROPE_FREQS, 1) * sin_lo


def _pair_cast_columns(src_ref, dst_ref):
    @pl.when((pl.program_id(0) == 0) & (pl.program_id(1) == 0))
    def _():
        kv_lo, kv_hi = A_WIDTH, A_WIDTH + 2 * A_KV_WIDTH

        def body(i, carry):
            rows = pl.ds(pl.multiple_of(i * LANES, LANES), LANES)
            src = src_ref[rows, :]
            dst_ref[rows, kv_lo:kv_hi] = src[:, kv_lo:kv_hi].astype(BF16)
            for base in (0, kv_hi):
                for p in range(A_GROUP):
                    a = base + p * A_HEAD_DIM
                    b = base + (p + A_GROUP) * A_HEAD_DIM
                    pair = jnp.concatenate([src[:, a:a + A_HEAD_DIM], src[:, b:b + A_HEAD_DIM]], axis=1)
                    dst_ref[rows, base + p * LANES:base + (p + 1) * LANES] = pair.astype(BF16)
            return carry
        lax.fori_loop(0, src_ref.shape[0] // LANES, body, 0)


def _inproj_attn_kernel(rope, x_ref, m_ref, g_ref, w_ref, *rest):
    wq_ref = w_ref.at[:, 0:A_WIDTH]
    wkv_ref = w_ref.at[:, A_WIDTH:A_WIDTH + 2 * A_KV_WIDTH]
    wg_ref = w_ref.at[:, A_WIDTH + 2 * A_KV_WIDTH:]
    if rope:
        cos_ref, shi_ref, slo_ref, qt_ref, k_ref, vt_ref, sg_ref = rest
    else:
        qt_ref, k_ref, vt_ref, sg_ref = rest
    assert x_ref.shape[1] % SUB_ROWS == 0
    for r in range(x_ref.shape[1] // SUB_ROWS):
        rows = slice(r * SUB_ROWS, (r + 1) * SUB_ROWS)
        if rope:
            pos = pl.ds(pl.multiple_of(pl.program_id(1) * x_ref.shape[1] + r * SUB_ROWS, SUB_ROWS), SUB_ROWS)
            cos, shi, slo = cos_ref[pos], shi_ref[pos], slo_ref[pos]
        hb = _norm_mod(x_ref[0, rows], g_ref[...], m_ref[0]).astype(BF16)
        q = _dot(hb, wq_ref[...])
        for j in range(A_WIDTH // LANES):
            qj = q[:, j * LANES:(j + 1) * LANES]
            if rope:
                qj = _rope(qj, cos, shi, slo)
            qt_ref[0, j * LANES:(j + 1) * LANES, rows] = (qj * (Q_SCALE * LOG2_E)).T.astype(BF16)
        kv = _dot(hb, wkv_ref[...])
        k = kv[:, 0:A_KV_WIDTH]
        if rope:
            k = _rope(k, cos, shi, slo)
        k_ref[0, rows] = k.astype(BF16)
        vt_ref[0, :, rows] = kv[:, A_KV_WIDTH:].T.astype(BF16)
        g = _dot(hb, wg_ref[...])
        sg_ref[0, rows] = _silu(g).astype(BF16)


def _inproj_attn(x, mod3, norm_g, w_in, tables):
    b, s, _ = x.shape
    tm = min(A_ROW_TILE, s)
    in_specs = [
        pl.BlockSpec((1, tm, D), lambda bi, i: (bi, i, 0)),
        pl.BlockSpec((1, 1, 3 * D), lambda bi, i: (bi, 0, 0)),
        pl.BlockSpec((1, D), lambda bi, i: (0, 0)),
        pl.BlockSpec(w_in.shape, lambda bi, i: (0, 0)),
    ] + [pl.BlockSpec((s, LANES), lambda bi, i: (0, 0))] * 3
    args = [x, mod3, norm_g.reshape(1, D), w_in, *tables]
    out_shape = (
        jax.ShapeDtypeStruct((b, A_WIDTH, s), BF16),
        jax.ShapeDtypeStruct((b, s, A_KV_WIDTH), BF16),
        jax.ShapeDtypeStruct((b, A_KV_WIDTH, s), BF16),
        jax.ShapeDtypeStruct((b, s, A_WIDTH), BF16),
    )
    out_specs = (
        pl.BlockSpec((1, A_WIDTH, tm), lambda bi, i: (bi, 0, i)),
        pl.BlockSpec((1, tm, A_KV_WIDTH), lambda bi, i: (bi, i, 0)),
        pl.BlockSpec((1, A_KV_WIDTH, tm), lambda bi, i: (bi, 0, i)),
        pl.BlockSpec((1, tm, A_WIDTH), lambda bi, i: (bi, i, 0)),
    )
    return pl.pallas_call(
        functools.partial(_inproj_attn_kernel, True),
        grid=(b, s // tm),
        in_specs=in_specs,
        out_specs=out_specs,
        out_shape=out_shape,
        compiler_params=_params("parallel", "parallel"),
        name="inproj_attn",
    )(*args)


def _attn_block(kwin, vtw, qts, sinks, ok_prev, ok_next):
    n_keys, nq = kwin.shape[0], qts.shape[1]
    lane = lax.broadcasted_iota(jnp.int32, kwin.shape, 1)
    kbd = jnp.concatenate([jnp.where(lane < A_HEAD_DIM, kwin, jnp.zeros_like(kwin)),
                           jnp.where(lane >= A_HEAD_DIM, kwin, jnp.zeros_like(kwin))], axis=0)
    ones = jnp.where(lax.broadcasted_iota(jnp.int32, (SUM_ROWS, n_keys), 0) == 0, 1.0, 0.0).astype(BF16)

    st = _dot(kbd, qts).astype(BF16)
    neg = jnp.asarray(NEG_INF, BF16)
    outs = []
    for hh in range(2):
        sh = st[hh * n_keys:(hh + 1) * n_keys]
        if ok_prev is not None:
            parts = [jnp.where(ok_prev, sh[0:BLOCK], neg),
                     sh[BLOCK:2 * BLOCK],
                     jnp.where(ok_next, sh[2 * BLOCK:3 * BLOCK], neg),
                     sh[3 * BLOCK:]]
        else:
            parts = [sh]
        mx = functools.reduce(jnp.maximum, [jnp.max(t, axis=0, keepdims=True) for t in parts])
        mxb = jnp.maximum(mx.astype(F32), sinks[hh]).astype(BF16)
        mx = mxb.astype(F32)
        probs = jnp.concatenate([jnp.exp2(t - mxb) for t in parts], axis=0)
        vt_h = jnp.concatenate([vtw[hh * A_HEAD_DIM:(hh + 1) * A_HEAD_DIM], ones], axis=0)
        ot = _dot(vt_h, probs)
        den = ot[A_HEAD_DIM:A_HEAD_DIM + 1] + jnp.exp2(sinks[hh] - mx)
        outs.append(ot[0:A_HEAD_DIM] * (1.0 / den))
    return jnp.concatenate(outs, axis=0)


def _pair_cast_rows(src_ref, dst_ref):
    @pl.when((pl.program_id(0) == 0) & (pl.program_id(1) == 0))
    def _():
        for p in range(A_GROUP):
            for half, h in enumerate((p, p + A_GROUP)):
                lo = p * LANES + half * A_HEAD_DIM
                dst_ref[lo:lo + A_HEAD_DIM, :] = src_ref[h * A_HEAD_DIM:(h + 1) * A_HEAD_DIM, :].astype(BF16)


def _attn_kernel(local, n_steps, sink_ref, qt_ref, *rest):
    n_blk = rest[-1].shape[1] // BLOCK
    if local:
        (kp_ref, kc_ref, kn_ref, vp_ref, vc_ref, vn_ref, kx_ref, vx_ref,
         sg_ref, x_ref, m_ref, w_ref, o_ref) = rest
        last = slice((n_blk - 1) * BLOCK, n_blk * BLOCK)
        k_blocks = ([kp_ref[0, last]] + [kc_ref[0, i * BLOCK:(i + 1) * BLOCK] for i in range(n_blk)]
                    + [kn_ref[0, 0:BLOCK]])
        v_blocks = ([vp_ref[0, :, last]] + [vc_ref[0, :, i * BLOCK:(i + 1) * BLOCK] for i in range(n_blk)]
                    + [vn_ref[0, :, 0:BLOCK]])
    else:
        kx_ref, vx_ref, sg_ref, x_ref, m_ref, w_ref, o_ref = rest
    step = pl.program_id(1)
    nq = A_GROUP * BLOCK
    chunk_of = lax.broadcasted_iota(jnp.int32, (1, nq), 1) // BLOCK
    sinks = []
    for hh in range(2):
        sk = jnp.full((1, nq), sink_ref[hh] * LOG2_E, F32)
        for c in range(1, A_GROUP):
            sk = jnp.where(chunk_of == c, sink_ref[2 * c + hh] * LOG2_E, sk)
        sinks.append(sk)
    if local:
        kj = lax.broadcasted_iota(jnp.int32, (BLOCK, nq), 0)
        qi = lax.broadcasted_iota(jnp.int32, (BLOCK, nq), 1) % BLOCK

    rows_out = []
    for blk in range(n_blk):
        qts = jnp.concatenate([qt_ref[0, c * LANES:(c + 1) * LANES, blk * BLOCK:(blk + 1) * BLOCK]
                               for c in range(A_GROUP)], axis=1)
        if local:
            kwin = jnp.concatenate(k_blocks[blk:blk + 3] + [kx_ref[0]], axis=0)
            vtw = jnp.concatenate(v_blocks[blk:blk + 3] + [vx_ref[0]], axis=1)
            ok_prev = (kj >= qi) & (step > 0) if blk == 0 else (kj >= qi)
            ok_next = (kj <= qi) & (step < n_steps - 1) if blk == n_blk - 1 else (kj <= qi)
        else:
            kwin, vtw, ok_prev, ok_next = kx_ref[0], vx_ref[0], None, None
        ot = _attn_block(kwin, vtw, qts, sinks, ok_prev, ok_next)
        rows = slice(blk * BLOCK, (blk + 1) * BLOCK)
        outs = []
        for c in range(A_GROUP):
            sg = sg_ref[0, rows, c * LANES:(c + 1) * LANES].astype(F32)
            outs.append((ot[:, c * BLOCK:(c + 1) * BLOCK].T * sg).astype(BF16))
        rows_out.append(jnp.concatenate(outs, axis=1))
    y = _dot(jnp.concatenate(rows_out, axis=0), w_ref[...])
    gate = m_ref[0][:, 2 * D:3 * D]
    o_ref[0] = x_ref[0] + gate * y


def _attention(sink, qt, k, vt, kx, vxt, sg, x, mod3, w_out):
    b, s, _ = sg.shape
    rows = A_BLOCKS_PER_STEP * BLOCK
    assert s % rows == 0
    ns = s // rows
    n_ctx = kx.shape[1]
    blk = lambda w: pl.BlockSpec((1, rows, w), lambda bi, i: (bi, i, 0))
    blk_t = lambda w: pl.BlockSpec((1, w, rows), lambda bi, i: (bi, 0, i))
    lo = lambda i: jnp.maximum(i - 1, 0)
    hi = lambda i: jnp.minimum(i + 1, ns - 1)
    in_specs = [pl.BlockSpec(memory_space=pltpu.SMEM), blk_t(A_WIDTH),
                pl.BlockSpec((1, rows, A_KV_WIDTH), lambda bi, i: (bi, lo(i), 0)),
                blk(A_KV_WIDTH),
                pl.BlockSpec((1, rows, A_KV_WIDTH), lambda bi, i: (bi, hi(i), 0)),
                pl.BlockSpec((1, A_KV_WIDTH, rows), lambda bi, i: (bi, 0, lo(i))),
                blk_t(A_KV_WIDTH),
                pl.BlockSpec((1, A_KV_WIDTH, rows), lambda bi, i: (bi, 0, hi(i))),
                pl.BlockSpec((1, n_ctx, A_KV_WIDTH), lambda bi, i: (bi, 0, 0)),
                pl.BlockSpec((1, A_KV_WIDTH, n_ctx), lambda bi, i: (bi, 0, 0)),
                blk(A_WIDTH), blk(D),
                pl.BlockSpec((1, 1, 3 * D), lambda bi, i: (bi, 0, 0)),
                pl.BlockSpec((A_WIDTH, D), lambda bi, i: (0, 0))]
    return pl.pallas_call(
        functools.partial(_attn_kernel, True, ns),
        grid=(b, ns),
        in_specs=in_specs,
        out_specs=blk(D),
        out_shape=jax.ShapeDtypeStruct((b, s, D), F32),
        compiler_params=_params("parallel", "parallel"),
        name="attn_local",
    )(sink, qt, k, k, k, vt, vt, vt, kx, vxt, sg, x, mod3, w_out)


def _ctx_attn_kernel(sink_ref, x_ref, m_ref, g_ref, win_ref, wout_ref, k_ref, vt_ref, o_ref,
                     win_bf_ref, wout_bf_ref, qt_scr, sg_scr):
    _pair_cast_columns(win_ref, win_bf_ref)
    _pair_cast_rows(wout_ref, wout_bf_ref)
    _inproj_attn_kernel(False, x_ref, m_ref, g_ref, win_bf_ref, qt_scr, k_ref, vt_ref, sg_scr)
    _attn_kernel(False, 1, sink_ref, qt_scr, k_ref, vt_ref, sg_scr, x_ref, m_ref, wout_bf_ref, o_ref)


def _ctx_attention(sink, xc, mod3, mod_row, norm_g, w_in, w_out):
    b, n, _ = xc.shape
    const = lambda shape, **kw: pl.BlockSpec(shape, lambda bi, i: (0,) * len(shape), **kw)
    return pl.pallas_call(
        _ctx_attn_kernel,
        grid=(b, 1),
        in_specs=[pl.BlockSpec(memory_space=pltpu.SMEM),
                  pl.BlockSpec((1, n, D), lambda bi, i: (bi, 0, 0)),
                  pl.BlockSpec((1, 1, 3 * D), lambda bi, i: (mod_row, 0, 0)),
                  const((1, D)),
                  const(w_in.shape, pipeline_mode=pl.Buffered(1)),
                  const(w_out.shape, pipeline_mode=pl.Buffered(1))],
        out_specs=(pl.BlockSpec((1, n, A_KV_WIDTH), lambda bi, i: (bi, 0, 0)),
                   pl.BlockSpec((1, A_KV_WIDTH, n), lambda bi, i: (bi, 0, 0)),
                   pl.BlockSpec((1, n, D), lambda bi, i: (bi, 0, 0)),
                   const(w_in.shape), const(w_out.shape)),
        out_shape=(jax.ShapeDtypeStruct((b, n, A_KV_WIDTH), BF16),
                   jax.ShapeDtypeStruct((b, A_KV_WIDTH, n), BF16),
                   jax.ShapeDtypeStruct((b, n, D), F32),
                   jax.ShapeDtypeStruct(w_in.shape, BF16), jax.ShapeDtypeStruct(w_out.shape, BF16)),
        scratch_shapes=[pltpu.VMEM((1, A_WIDTH, n), BF16), pltpu.VMEM((1, n, A_WIDTH), BF16)],
        compiler_params=_params("arbitrary", "arbitrary"),
        name="ctx_attn",
    )(sink, xc, mod3, norm_g.reshape(1, D), w_in, w_out)


def _chunk_cumsum(x, tri):
    return _dot(tri, x.astype(BF16))


def _inproj_gla_kernel(chunk, with_q, x_ref, m_ref, g_ref, w_in_ref, wa1_ref, wa2_ref, ba_ref, tri_ref, *outs):
    if with_q:
        w_ref = w_in_ref
        per_dir = (outs[0:3], outs[3:6])
        v_ref, sg_ref = outs[6:]
    else:
        w_ref = outs[2]
        _cast_once(w_in_ref, w_ref)
        per_dir = ((outs[0],), (outs[1],))
    tm = x_ref.shape[1]
    hb = _norm_mod(x_ref[0], g_ref[...], m_ref[0]).astype(BF16)
    r = _dot(hb, wa1_ref[...])
    z = _dot(r.astype(BF16), wa2_ref[...]) + ba_ref[...]
    e = jnp.exp2(jnp.abs(z) * -LOG2_E)
    la = (jnp.minimum(z, 0.0) * LOG2_E - jnp.log2(1.0 + e)) * (1.0 / GATE_TEMP)
    k = _dot(hb, w_ref[:, B_K_WIDTH:2 * B_K_WIDTH])
    if with_q:
        q = _dot(hb, w_ref[:, 0:B_K_WIDTH]) * K_SCALE
    n_chunks = tm // chunk
    k_ends = []
    for reverse in (False, True):
        lad = la[:, B_K_WIDTH:] if reverse else la[:, 0:B_K_WIDTH]
        refs = per_dir[1] if reverse else per_dir[0]
        tots, kis, k_end = [], [], None
        tri = tri_ref[1 if reverse else 0]
        tb = tri.shape[0]
        cum_all = jnp.concatenate([_chunk_cumsum(lad[r * tb:(r + 1) * tb], tri) for r in range(tm // tb)], axis=0)
        for c in range(n_chunks):
            rows = slice(c * chunk, (c + 1) * chunk)
            cum = cum_all[rows]
            tot = cum[0:1] if reverse else cum[chunk - 1:chunk]
            if not with_q:
                k_end = (k[rows] * jnp.exp2(tot - cum)).astype(BF16)
                continue
            refs[0][0, rows] = (q[rows] * jnp.exp2(cum)).astype(BF16)
            kis.append(k[rows] * jnp.exp2(-cum))
            tots.append(tot)
            if c % 2 == 1:
                pair = slice((c - 1) * chunk, (c + 1) * chunk)
                refs[1][0, :, pair] = jnp.concatenate(kis[-2:], axis=0).T.astype(BF16)
        if with_q:
            pad = [jnp.zeros((DEC_ROWS - n_chunks, B_K_WIDTH), F32)] if n_chunks < DEC_ROWS else []
            refs[2][0, 0] = jnp.exp2(jnp.concatenate(tots + pad, axis=0))
        else:
            k_ends.append(k_end)
    if with_q:
        sg_ref[0] = _silu(_dot(hb, w_ref[:, 2 * B_K_WIDTH + B_V_WIDTH:])).astype(BF16)
    v = _dot(hb, w_ref[:, 2 * B_K_WIDTH:2 * B_K_WIDTH + B_V_WIDTH]).astype(BF16)
    if with_q:
        v_ref[0] = v
    else:
        for refs, k_end in zip(per_dir, k_ends):
            for h in range(B_HEADS):
                ks = slice(h * B_KEY_DIM, (h + 1) * B_KEY_DIM)
                refs[0][0, h] = _dot_tn(k_end[:, ks], v[:, h * B_VAL_DIM:(h + 1) * B_VAL_DIM])


def _inproj_gla(x, mod3, mod_row, norm_g, w, wa1, wa2, ba, chunk, with_q):
    b, s, _ = x.shape
    tm = min(ROW_TILE, s)
    nt = s // tm
    cpt = tm // chunk
    tb = max(chunk, MXU_DIM)
    assert tm % tb == 0 and tb % chunk == 0
    if mod_row is None:
        mod_map = lambda bi, i: (bi, 0, 0)
    else:
        mod_map = lambda bi, i: (mod_row, 0, 0)
    const = lambda shape, **kw: pl.BlockSpec(shape, lambda bi, i: (0,) * len(shape), **kw)
    in_specs = [
        pl.BlockSpec((1, tm, D), lambda bi, i: (bi, i, 0)),
        pl.BlockSpec((1, 1, 3 * D), mod_map),
        const((1, D)), const(w.shape, **({} if with_q else {"pipeline_mode": pl.Buffered(1)})),
        const(wa1.shape), const(wa2.shape), const(ba.shape),
        const((2, tb, tb)),
    ]
    t_idx = np.arange(tb)
    same = (t_idx[:, None] // chunk) == (t_idx[None, :] // chunk)
    lower = same & (t_idx[None, :] <= t_idx[:, None])
    tri = jnp.asarray(np.stack([lower, lower.T]), dtype=BF16)
    rows = lambda w_: (jax.ShapeDtypeStruct((b, s, w_), BF16),
                       pl.BlockSpec((1, tm, w_), lambda bi, i: (bi, i, 0)))
    cols = (jax.ShapeDtypeStruct((b, B_K_WIDTH, s), BF16),
            pl.BlockSpec((1, B_K_WIDTH, tm), lambda bi, i: (bi, 0, i)))
    decs = (jax.ShapeDtypeStruct((b, nt, DEC_ROWS, B_K_WIDTH), F32),
            pl.BlockSpec((1, 1, DEC_ROWS, B_K_WIDTH), lambda bi, i: (bi, i, 0, 0)))
    if with_q:
        assert cpt <= DEC_ROWS and cpt % 2 == 0
        outs = [rows(B_K_WIDTH), cols, decs] * 2 + [rows(B_V_WIDTH)] * 2
    else:
        assert nt == 1 and cpt == 1
        state = (jax.ShapeDtypeStruct((b, B_HEADS, B_KEY_DIM, B_VAL_DIM), F32),
                 pl.BlockSpec((1, B_HEADS, B_KEY_DIM, B_VAL_DIM), lambda bi, i: (bi, 0, 0, 0)))
        outs = [state] * 2 + [(jax.ShapeDtypeStruct(w.shape, BF16), const(w.shape))]
    return pl.pallas_call(
        functools.partial(_inproj_gla_kernel, chunk, with_q),
        grid=(b, nt),
        in_specs=in_specs,
        out_specs=tuple(o[1] for o in outs),
        out_shape=tuple(o[0] for o in outs),
        compiler_params=_params("parallel", "parallel") if with_q else _params("arbitrary", "arbitrary"),
        name="inproj_gla" if with_q else "inproj_gla_ctx",
    )(x, mod3, norm_g.reshape(1, D), w, wa1, wa2, ba, tri)


def _scan_tile(reverse, qd_ref, kit_ref, dec_ref, v_ref, st_ref):
    pair_rows = 2 * CHUNK
    ti = lax.broadcasted_iota(jnp.int32, (pair_rows, pair_rows), 0)
    si = lax.broadcasted_iota(jnp.int32, (pair_rows, pair_rows), 1)
    same = (ti // CHUNK) == (si // CHUNK)
    if reverse:
        use_inv = same & (ti <= si)
        use_end = (ti < CHUNK) & (si >= CHUNK)
        second_row = lax.broadcasted_iota(jnp.int32, (pair_rows, B_KEY_DIM), 0) < CHUNK
        first_col = lax.broadcasted_iota(jnp.int32, (B_KEY_DIM, pair_rows), 1) >= CHUNK
    else:
        use_inv = same & (ti >= si)
        use_end = (ti >= CHUNK) & (si < CHUNK)
        second_row = lax.broadcasted_iota(jnp.int32, (pair_rows, B_KEY_DIM), 0) >= CHUNK
        first_col = lax.broadcasted_iota(jnp.int32, (B_KEY_DIM, pair_rows), 1) < CHUNK
    n_pairs = SCAN_TILE // pair_rows
    order = range(n_pairs - 1, -1, -1) if reverse else range(n_pairs)
    n_dec = dec_ref.shape[1] * DEC_ROWS
    dec = dec_ref[0].reshape(n_dec, B_K_WIDTH)
    dect = jnp.concatenate([dec, jnp.zeros((LANES - n_dec, B_K_WIDTH), F32)], axis=0).T
    heads = []
    for h in range(B_HEADS):
        ks = slice(h * B_KEY_DIM, (h + 1) * B_KEY_DIM)
        vs = slice(h * B_VAL_DIM, (h + 1) * B_VAL_DIM)
        st = st_ref[h]
        o_rows = [None] * n_pairs
        for p in order:
            rows = slice(p * pair_rows, (p + 1) * pair_rows)
            c_first, c_second = (2 * p + 1, 2 * p) if reverse else (2 * p, 2 * p + 1)
            qd = qd_ref[0, rows, ks]
            kit = kit_ref[0, ks, rows]
            vh = v_ref[0, rows, vs]
            d_first, d_second = dect[ks, c_first:c_first + 1], dect[ks, c_second:c_second + 1]
            ki32 = kit.astype(F32)
            s2 = _dot(qd, jnp.concatenate([kit, (ki32 * d_first).astype(BF16)], axis=1))
            a = jnp.where(use_inv, s2[:, 0:pair_rows], jnp.where(use_end, s2[:, pair_rows:], 0.0)).astype(BF16)
            q_pair = jnp.where(second_row, qd.astype(F32) * dec[c_first:c_first + 1, ks], qd.astype(F32))
            k_pair = ki32 * jnp.where(first_col, d_first * d_second, d_second)
            o_rows[p] = _dot(a, vh) + _dot(q_pair.astype(BF16), st.astype(BF16))
            st = st * (d_first * d_second) + _dot(k_pair.astype(BF16), vh)
        st_ref[h] = st
        heads.append(jnp.concatenate(o_rows, axis=0))
    return jnp.concatenate(heads, axis=1)


def _gla_scan_kernel(nt, qdf_ref, kitf_ref, decf_ref, qdb_ref, kitb_ref, decb_ref,
                     v_ref, sf_ref, sb_ref, sg_ref, x_ref, m_ref, hn_ref, w32_ref, fn_ref,
                     o_ref, st_ref, of_ref, w_ref):
    _cast_once(w32_ref, w_ref)
    j = pl.program_id(1)

    @pl.when(j == 0)
    def _():
        st_ref[...] = sf_ref[0]

    @pl.when(j == nt)
    def _():
        st_ref[...] = sb_ref[0]

    @pl.when(j < nt)
    def _():
        of_ref[j] = _scan_tile(False, qdf_ref, kitf_ref, decf_ref, v_ref, st_ref)

    @pl.when(j >= nt)
    def _():
        o = _scan_tile(True, qdb_ref, kitb_ref, decb_ref, v_ref, st_ref) + of_ref[2 * nt - 1 - j]
        normed = []
        for h in range(B_HEADS):
            oh = o[:, h * B_VAL_DIM:(h + 1) * B_VAL_DIM]
            normed.append(oh * lax.rsqrt(jnp.mean(oh * oh, axis=-1, keepdims=True) + EPS))
        of = jnp.concatenate(normed, axis=1) * hn_ref[...]
        y = _dot((of * sg_ref[0].astype(F32)).astype(BF16), w_ref[...])
        xn = x_ref[0] + m_ref[0][:, 2 * D:3 * D] * y
        ms = jnp.mean(xn * xn, axis=-1, keepdims=True)
        o_ref[0] = xn * lax.rsqrt(ms + EPS) * fn_ref[...]


def _gla_scan(fwd, bwd, v, s_f, s_b, sg, x, mod3, head_g, w_out, final_g):
    b, s, _ = v.shape
    assert SCAN_TILE % ROW_TILE == 0 and ROW_TILE // CHUNK == DEC_ROWS
    dec_blocks = SCAN_TILE // ROW_TILE
    nt = s // SCAN_TILE
    t_fwd = lambda j: jnp.minimum(j, nt - 1)
    t_bwd = lambda j: 2 * nt - 1 - jnp.maximum(j, nt)
    t_both = lambda j: jnp.where(j < nt, j, 2 * nt - 1 - j)

    def direction(t):
        rows = pl.BlockSpec((1, SCAN_TILE, B_K_WIDTH), lambda bi, j: (bi, t(j), 0))
        cols = pl.BlockSpec((1, B_K_WIDTH, SCAN_TILE), lambda bi, j: (bi, 0, t(j)))
        decs = pl.BlockSpec((1, dec_blocks, DEC_ROWS, B_K_WIDTH), lambda bi, j: (bi, t(j), 0, 0))
        return [rows, cols, decs]

    const = lambda shape: pl.BlockSpec(shape, lambda bi, j: (0,) * len(shape))
    state = pl.BlockSpec((1, B_HEADS, B_KEY_DIM, B_VAL_DIM), lambda bi, j: (bi, 0, 0, 0))
    out_rows = lambda w_: pl.BlockSpec((1, SCAN_TILE, w_), lambda bi, j: (bi, t_bwd(j), 0))
    in_specs = direction(t_fwd) + direction(t_bwd) + [
        pl.BlockSpec((1, SCAN_TILE, B_V_WIDTH), lambda bi, j: (bi, t_both(j), 0)),
        state, state, out_rows(B_V_WIDTH), out_rows(D),
        pl.BlockSpec((1, 1, 3 * D), lambda bi, j: (bi, 0, 0)),
        const((1, B_V_WIDTH)),
        pl.BlockSpec((B_V_WIDTH, D), lambda bi, j: (0, 0), pipeline_mode=pl.Buffered(1)),
        const((1, D))]
    return pl.pallas_call(
        functools.partial(_gla_scan_kernel, nt),
        grid=(b, 2 * nt),
        in_specs=in_specs,
        out_specs=out_rows(D),
        out_shape=jax.ShapeDtypeStruct((b, s, D), F32),
        scratch_shapes=[pltpu.VMEM((B_HEADS, B_KEY_DIM, B_VAL_DIM), F32),
                        pltpu.VMEM((nt, SCAN_TILE, B_V_WIDTH), F32),
                        pltpu.VMEM((B_V_WIDTH, D), BF16)],
        compiler_params=_params("arbitrary", "arbitrary", vmem_limit=SCAN_VMEM_LIMIT),
        name="gla_scan",
    )(*fwd, *bwd, v, s_f, s_b, sg, x, mod3, head_g.reshape(1, B_V_WIDTH), w_out, final_g.reshape(1, D))


def _rope_tables(n_tokens):
    rows_n = n_tokens // GRID_W
    row = np.repeat(np.arange(rows_n, dtype=np.float32), GRID_W)
    col = np.tile(np.arange(GRID_W, dtype=np.float32), rows_n)
    inv_freq = (np.float32(ROPE_BASE) ** (-np.arange(ROPE_FREQS, dtype=np.float32) / np.float32(ROPE_FREQS)))
    inv_freq = inv_freq.astype(np.float32)
    ang = np.stack([row[:, None] * inv_freq, col[:, None] * inv_freq], axis=1)
    cos, sin = np.cos(ang).astype(np.float32), np.sin(ang).astype(np.float32)
    zero = np.zeros_like(sin)
    tile = lambda t: np.tile(t.reshape(n_tokens, A_HEAD_DIM), (1, LANES // A_HEAD_DIM))
    return (tile(np.stack([cos, cos], axis=2)),
            tile(np.stack([-sin, zero], axis=2)),
            tile(np.stack([zero, sin], axis=2)))


def _pair_heads(t, axis):
    shape = t.shape
    t = t.reshape(shape[:axis] + (A_KV_HEADS, A_GROUP, -1) + shape[axis + 1:])
    return jnp.swapaxes(t, axis, axis + 1).reshape(shape)


def kernel(x, c, ctx, c_ctx, l0_norm_g, l0_w_ada, l0_b_ada, l0_w_in, l0_sink, l0_w_out, l1_norm_g, l1_w_ada, l1_b_ada, l1_w_in, l1_wa1_f, l1_wa2_f, l1_ba_f, l1_wa1_b, l1_wa2_b, l1_ba_b, l1_head_norm_g, l1_w_out, final_norm_g):
    b, s, _ = x.shape
    ctx_row = b

    cvec = jnp.concatenate([c, c_ctx[None, :], jnp.zeros((MOD_ROWS - b - 1, D), F32)], axis=0)
    mod0, mod1 = (m.reshape(MOD_ROWS, 1, 3 * D) for m in _modulation(cvec, l0_w_ada, l0_b_ada, l1_w_ada, l1_b_ada))

    sink = _pair_heads(l0_sink.astype(F32), 0)
    kc, vct, xc1, w0_in, w0_out = _ctx_attention(sink, ctx, mod0, ctx_row, l0_norm_g, l0_w_in, l0_w_out)
    qt, k, vt, sg = _inproj_attn(x, mod0, l0_norm_g, w0_in, _rope_tables(s))
    x1 = _attention(sink, qt, k, vt, kc, vct, sg, x, mod0, w0_out)

    wa1 = jnp.concatenate([l1_wa1_f, l1_wa1_b, jnp.zeros((D, LANES - 2 * GATE_RANK), F32)], axis=1).astype(BF16)
    wa2 = jnp.zeros((LANES, 2 * B_K_WIDTH), F32)
    wa2 = wa2.at[0:GATE_RANK, 0:B_K_WIDTH].set(l1_wa2_f)
    wa2 = wa2.at[GATE_RANK:2 * GATE_RANK, B_K_WIDTH:].set(l1_wa2_b).astype(BF16)
    ba = jnp.concatenate([l1_ba_f, l1_ba_b]).reshape(1, 2 * B_K_WIDTH)
    s_f, s_b, w1 = _inproj_gla(xc1, mod1, ctx_row, l1_norm_g, l1_w_in, wa1, wa2, ba, ctx.shape[1], False)
    outs = _inproj_gla(x1, mod1, None, l1_norm_g, w1, wa1, wa2, ba, CHUNK, True)
    v1, sg1 = outs[6:]
    return _gla_scan(outs[0:3], outs[3:6], v1, s_f, s_b, sg1, x1, mod1,
                     l1_head_norm_g, l1_w_out, final_norm_g)
```

```python
import functools

import jax
import jax.numpy as jnp
import numpy as np
from jax import lax
from jax.experimental import pallas as pl
from jax.experimental.pallas import tpu as pltpu

F32 = jnp.float32
BF16 = jnp.bfloat16

D = 1024
GRID_W = 64
EPS = 1e-6
NEG_INF = -1e30

A_HEADS = 16
A_KV_HEADS = 2
A_GROUP = A_HEADS // A_KV_HEADS
A_HEAD_DIM = 64
A_WIDTH = A_HEADS * A_HEAD_DIM
A_KV_WIDTH = A_KV_HEADS * A_HEAD_DIM
BLOCK = 128
ROPE_BASE = 10000.0
ROPE_FREQS = A_HEAD_DIM // 4
Q_SCALE = A_HEAD_DIM ** -0.5
LOG2_E = 1.4426950408889634

B_HEADS = 4
B_K_WIDTH = D // 2
B_V_WIDTH = D
B_KEY_DIM = B_K_WIDTH // B_HEADS
B_VAL_DIM = B_V_WIDTH // B_HEADS
GATE_RANK = 16
GATE_TEMP = 16.0
CHUNK = 64
K_SCALE = B_KEY_DIM ** -0.5

LANES = 128
MXU_DIM = 256
MOD_ROWS = 16
ROW_TILE = 1024
A_ROW_TILE = 1024
SCAN_TILE = 1024
SUB_ROWS = 256
DEC_ROWS = ROW_TILE // CHUNK
SUM_ROWS = 16
A_BLOCKS_PER_STEP = 8
VMEM_LIMIT = 48 * 1024 * 1024
SMALL_VMEM_LIMIT = 40 * 1024 * 1024
SCAN_VMEM_LIMIT = 56 * 1024 * 1024


def _params(*sem, vmem_limit=VMEM_LIMIT):
    return pltpu.CompilerParams(dimension_semantics=sem, vmem_limit_bytes=vmem_limit)


def _silu(x):
    return x / (1.0 + jnp.exp(-x))


def _dot(a, b):
    return jnp.dot(a, b, preferred_element_type=F32)


def _dot_nt(a, b):
    return lax.dot_general(a, b, (((1,), (1,)), ((), ())), preferred_element_type=F32)


def _dot_tn(a, b):
    return lax.dot_general(a, b, (((0,), (0,)), ((), ())), preferred_element_type=F32)


def _norm_mod(x, g, m):
    ms = jnp.mean(x * x, axis=-1, keepdims=True)
    y = x * lax.rsqrt(ms + EPS) * g
    return y * (1.0 + m[:, D:2 * D]) + m[:, 0:D]


def _cast_once(src_ref, dst_ref):
    @pl.when((pl.program_id(0) == 0) & (pl.program_id(1) == 0))
    def _():
        def body(i, carry):
            rows = pl.ds(pl.multiple_of(i * LANES, LANES), LANES)
            dst_ref[rows, :] = src_ref[rows, :].astype(dst_ref.dtype)
            return carry
        lax.fori_loop(0, src_ref.shape[0] // LANES, body, 0)


def _mod_kernel(n_tiles, c_ref, w0_ref, b0_ref, w1_ref, b1_ref, o0_ref, o1_ref):
    s = _silu(c_ref[...]).astype(BF16)
    j = pl.program_id(0)

    @pl.when(j < n_tiles)
    def _():
        o0_ref[...] = _dot(s, w0_ref[...].astype(BF16)) + b0_ref[...]

    @pl.when(j >= n_tiles)
    def _():
        o1_ref[...] = _dot(s, w1_ref[...].astype(BF16)) + b1_ref[...]


def _modulation(cvec, w_ada0, b_ada0, w_ada1, b_ada1):
    n = w_ada0.shape[1] // D
    first = lambda j: (0, jnp.minimum(j, n - 1))
    second = lambda j: (0, jnp.maximum(j - n, 0))
    out = jax.ShapeDtypeStruct((MOD_ROWS, n * D), F32)
    return pl.pallas_call(
        functools.partial(_mod_kernel, n),
        grid=(2 * n,),
        in_specs=[
            pl.BlockSpec((MOD_ROWS, D), lambda j: (0, 0)),
            pl.BlockSpec((D, D), first), pl.BlockSpec((1, D), first),
            pl.BlockSpec((D, D), second), pl.BlockSpec((1, D), second),
        ],
        out_specs=(pl.BlockSpec((MOD_ROWS, D), first), pl.BlockSpec((MOD_ROWS, D), second)),
        out_shape=(out, out),
        compiler_params=_params("arbitrary", vmem_limit=SMALL_VMEM_LIMIT),
        name="modulation",
    )(cvec, w_ada0, b_ada0.reshape(1, n * D), w_ada1, b_ada1.reshape(1, n * D))


def _rope(t, cos, sin_hi, sin_lo):
    return t * cos + pltpu.roll(t, LANES - ROPE_FREQS, 1) * sin_hi + pltpu.roll(t, ROPE_FREQS, 1) * sin_lo


def _pair_cast_columns(src_ref, dst_ref):
    @pl.when((pl.program_id(0) == 0) & (pl.program_id(1) == 0))
    def _():
        kv_lo, kv_hi = A_WIDTH, A_WIDTH + 2 * A_KV_WIDTH

        def body(i, carry):
            rows = pl.ds(pl.multiple_of(i * LANES, LANES), LANES)
            src = src_ref[rows, :]
            dst_ref[rows, kv_lo:kv_hi] = src[:, kv_lo:kv_hi].astype(BF16)
            for base in (0, kv_hi):
                for p in range(A_GROUP):
                    a = base + p * A_HEAD_DIM
                    b = base + (p + A_GROUP) * A_HEAD_DIM
                    pair = jnp.concatenate([src[:, a:a + A_HEAD_DIM], src[:, b:b + A_HEAD_DIM]], axis=1)
                    dst_ref[rows, base + p * LANES:base + (p + 1) * LANES] = pair.astype(BF16)
            return carry
        lax.fori_loop(0, src_ref.shape[0] // LANES, body, 0)


def _inproj_attn_kernel(rope, x_ref, m_ref, g_ref, w_ref, *rest):
    wq_ref = w_ref.at[:, 0:A_WIDTH]
    wkv_ref = w_ref.at[:, A_WIDTH:A_WIDTH + 2 * A_KV_WIDTH]
    wg_ref = w_ref.at[:, A_WIDTH + 2 * A_KV_WIDTH:]
    if rope:
        cos_ref, shi_ref, slo_ref, qt_ref, k_ref, vt_ref, sg_ref = rest
    else:
        qt_ref, k_ref, vt_ref, sg_ref = rest
    assert x_ref.shape[1] % SUB_ROWS == 0
    for r in range(x_ref.shape[1] // SUB_ROWS):
        rows = slice(r * SUB_ROWS, (r + 1) * SUB_ROWS)
        if rope:
            pos = pl.ds(pl.multiple_of(pl.program_id(1) * x_ref.shape[1] + r * SUB_ROWS, SUB_ROWS), SUB_ROWS)
            cos, shi, slo = cos_ref[pos], shi_ref[pos], slo_ref[pos]
        hb = _norm_mod(x_ref[0, rows], g_ref[...], m_ref[0]).astype(BF16)
        q = _dot(hb, wq_ref[...])
        for j in range(A_WIDTH // LANES):
            qj = q[:, j * LANES:(j + 1) * LANES]
            if rope:
                qj = _rope(qj, cos, shi, slo)
            qt_ref[0, j * LANES:(j + 1) * LANES, rows] = (qj * (Q_SCALE * LOG2_E)).T.astype(BF16)
        kv = _dot(hb, wkv_ref[...])
        k = kv[:, 0:A_KV_WIDTH]
        if rope:
            k = _rope(k, cos, shi, slo)
        k_ref[0, rows] = k.astype(BF16)
        vt_ref[0, :, rows] = kv[:, A_KV_WIDTH:].T.astype(BF16)
        g = _dot(hb, wg_ref[...])
        sg_ref[0, rows] = _silu(g).astype(BF16)


def _inproj_attn(x, mod3, norm_g, w_in, tables):
    b, s, _ = x.shape
    tm = min(A_ROW_TILE, s)
    in_specs = [
        pl.BlockSpec((1, tm, D), lambda bi, i: (bi, i, 0)),
        pl.BlockSpec((1, 1, 3 * D), lambda bi, i: (bi, 0, 0)),
        pl.BlockSpec((1, D), lambda bi, i: (0, 0)),
        pl.BlockSpec(w_in.shape, lambda bi, i: (0, 0)),
    ] + [pl.BlockSpec((s, LANES), lambda bi, i: (0, 0))] * 3
    args = [x, mod3, norm_g.reshape(1, D), w_in, *tables]
    out_shape = (
        jax.ShapeDtypeStruct((b, A_WIDTH, s), BF16),
        jax.ShapeDtypeStruct((b, s, A_KV_WIDTH), BF16),
        jax.ShapeDtypeStruct((b, A_KV_WIDTH, s), BF16),
        jax.ShapeDtypeStruct((b, s, A_WIDTH), BF16),
    )
    out_specs = (
        pl.BlockSpec((1, A_WIDTH, tm), lambda bi, i: (bi, 0, i)),
        pl.BlockSpec((1, tm, A_KV_WIDTH), lambda bi, i: (bi, i, 0)),
        pl.BlockSpec((1, A_KV_WIDTH, tm), lambda bi, i: (bi, 0, i)),
        pl.BlockSpec((1, tm, A_WIDTH), lambda bi, i: (bi, i, 0)),
    )
    return pl.pallas_call(
        functools.partial(_inproj_attn_kernel, True),
        grid=(b, s // tm),
        in_specs=in_specs,
        out_specs=out_specs,
        out_shape=out_shape,
        compiler_params=_params("parallel", "parallel"),
        name="inproj_attn",
    )(*args)


def _attn_block(kwin, vtw, qts, sinks, ok_prev, ok_next):
    n_keys, nq = kwin.shape[0], qts.shape[1]
    lane = lax.broadcasted_iota(jnp.int32, kwin.shape, 1)
    kbd = jnp.concatenate([jnp.where(lane < A_HEAD_DIM, kwin, jnp.zeros_like(kwin)),
                           jnp.where(lane >= A_HEAD_DIM, kwin, jnp.zeros_like(kwin))], axis=0)
    ones = jnp.where(lax.broadcasted_iota(jnp.int32, (SUM_ROWS, n_keys), 0) == 0, 1.0, 0.0).astype(BF16)

    st = _dot(kbd, qts).astype(BF16)
    neg = jnp.asarray(NEG_INF, BF16)
    outs = []
    for hh in range(2):
        sh = st[hh * n_keys:(hh + 1) * n_keys]
        if ok_prev is not None:
            parts = [jnp.where(ok_prev, sh[0:BLOCK], neg),
                     sh[BLOCK:2 * BLOCK],
                     jnp.where(ok_next, sh[2 * BLOCK:3 * BLOCK], neg),
                     sh[3 * BLOCK:]]
        else:
            parts = [sh]
        mx = functools.reduce(jnp.maximum, [jnp.max(t, axis=0, keepdims=True) for t in parts])
        mxb = jnp.maximum(mx.astype(F32), sinks[hh]).astype(BF16)
        mx = mxb.astype(F32)
        probs = jnp.concatenate([jnp.exp2(t - mxb) for t in parts], axis=0)
        vt_h = jnp.concatenate([vtw[hh * A_HEAD_DIM:(hh + 1) * A_HEAD_DIM], ones], axis=0)
        ot = _dot(vt_h, probs)
        den = ot[A_HEAD_DIM:A_HEAD_DIM + 1] + jnp.exp2(sinks[hh] - mx)
        outs.append(ot[0:A_HEAD_DIM] * (1.0 / den))
    return jnp.concatenate(outs, axis=0)


def _pair_cast_rows(src_ref, dst_ref):
    @pl.when((pl.program_id(0) == 0) & (pl.program_id(1) == 0))
    def _():
        for p in range(A_GROUP):
            for half, h in enumerate((p, p + A_GROUP)):
                lo = p * LANES + half * A_HEAD_DIM
                dst_ref[lo:lo + A_HEAD_DIM, :] = src_ref[h * A_HEAD_DIM:(h + 1) * A_HEAD_DIM, :].astype(BF16)


def _attn_kernel(local, n_steps, sink_ref, qt_ref, *rest):
    n_blk = rest[-1].shape[1] // BLOCK
    if local:
        (kp_ref, kc_ref, kn_ref, vp_ref, vc_ref, vn_ref, kx_ref, vx_ref,
         sg_ref, x_ref, m_ref, w_ref, o_ref) = rest
        last = slice((n_blk - 1) * BLOCK, n_blk * BLOCK)
        k_blocks = ([kp_ref[0, last]] + [kc_ref[0, i * BLOCK:(i + 1) * BLOCK] for i in range(n_blk)]
                    + [kn_ref[0, 0:BLOCK]])
        v_blocks = ([vp_ref[0, :, last]] + [vc_ref[0, :, i * BLOCK:(i + 1) * BLOCK] for i in range(n_blk)]
                    + [vn_ref[0, :, 0:BLOCK]])
    else:
        kx_ref, vx_ref, sg_ref, x_ref, m_ref, w_ref, o_ref = rest
    step = pl.program_id(1)
    nq = A_GROUP * BLOCK
    chunk_of = lax.broadcasted_iota(jnp.int32, (1, nq), 1) // BLOCK
    sinks = []
    for hh in range(2):
        sk = jnp.full((1, nq), sink_ref[hh] * LOG2_E, F32)
        for c in range(1, A_GROUP):
            sk = jnp.where(chunk_of == c, sink_ref[2 * c + hh] * LOG2_E, sk)
        sinks.append(sk)
    if local:
        kj = lax.broadcasted_iota(jnp.int32, (BLOCK, nq), 0)
        qi = lax.broadcasted_iota(jnp.int32, (BLOCK, nq), 1) % BLOCK

    rows_out = []
    for blk in range(n_blk):
        qts = jnp.concatenate([qt_ref[0, c * LANES:(c + 1) * LANES, blk * BLOCK:(blk + 1) * BLOCK]
                               for c in range(A_GROUP)], axis=1)
        if local:
            kwin = jnp.concatenate(k_blocks[blk:blk + 3] + [kx_ref[0]], axis=0)
            vtw = jnp.concatenate(v_blocks[blk:blk + 3] + [vx_ref[0]], axis=1)
            ok_prev = (kj >= qi) & (step > 0) if blk == 0 else (kj >= qi)
            ok_next = (kj <= qi) & (step < n_steps - 1) if blk == n_blk - 1 else (kj <= qi)
        else:
            kwin, vtw, ok_prev, ok_next = kx_ref[0], vx_ref[0], None, None
        ot = _attn_block(kwin, vtw, qts, sinks, ok_prev, ok_next)
        rows = slice(blk * BLOCK, (blk + 1) * BLOCK)
        outs = []
        for c in range(A_GROUP):
            sg = sg_ref[0, rows, c * LANES:(c + 1) * LANES].astype(F32)
            outs.append((ot[:, c * BLOCK:(c + 1) * BLOCK].T * sg).astype(BF16))
        rows_out.append(jnp.concatenate(outs, axis=1))
    y = _dot(jnp.concatenate(rows_out, axis=0), w_ref[...])
    gate = m_ref[0][:, 2 * D:3 * D]
    o_ref[0] = x_ref[0] + gate * y


def _attention(sink, qt, k, vt, kx, vxt, sg, x, mod3, w_out):
    b, s, _ = sg.shape
    rows = A_BLOCKS_PER_STEP * BLOCK
    assert s % rows == 0
    ns = s // rows
    n_ctx = kx.shape[1]
    blk = lambda w: pl.BlockSpec((1, rows, w), lambda bi, i: (bi, i, 0))
    blk_t = lambda w: pl.BlockSpec((1, w, rows), lambda bi, i: (bi, 0, i))
    lo = lambda i: jnp.maximum(i - 1, 0)
    hi = lambda i: jnp.minimum(i + 1, ns - 1)
    in_specs = [pl.BlockSpec(memory_space=pltpu.SMEM), blk_t(A_WIDTH),
                pl.BlockSpec((1, rows, A_KV_WIDTH), lambda bi, i: (bi, lo(i), 0)),
                blk(A_KV_WIDTH),
                pl.BlockSpec((1, rows, A_KV_WIDTH), lambda bi, i: (bi, hi(i), 0)),
                pl.BlockSpec((1, A_KV_WIDTH, rows), lambda bi, i: (bi, 0, lo(i))),
                blk_t(A_KV_WIDTH),
                pl.BlockSpec((1, A_KV_WIDTH, rows), lambda bi, i: (bi, 0, hi(i))),
                pl.BlockSpec((1, n_ctx, A_KV_WIDTH), lambda bi, i: (bi, 0, 0)),
                pl.BlockSpec((1, A_KV_WIDTH, n_ctx), lambda bi, i: (bi, 0, 0)),
                blk(A_WIDTH), blk(D),
                pl.BlockSpec((1, 1, 3 * D), lambda bi, i: (bi, 0, 0)),
                pl.BlockSpec((A_WIDTH, D), lambda bi, i: (0, 0))]
    return pl.pallas_call(
        functools.partial(_attn_kernel, True, ns),
        grid=(b, ns),
        in_specs=in_specs,
        out_specs=blk(D),
        out_shape=jax.ShapeDtypeStruct((b, s, D), F32),
        compiler_params=_params("parallel", "parallel"),
        name="attn_local",
    )(sink, qt, k, k, k, vt, vt, vt, kx, vxt, sg, x, mod3, w_out)


def _ctx_attn_kernel(sink_ref, x_ref, m_ref, g_ref, win_ref, wout_ref, k_ref, vt_ref, o_ref,
                     win_bf_ref, wout_bf_ref, qt_scr, sg_scr):
    _pair_cast_columns(win_ref, win_bf_ref)
    _pair_cast_rows(wout_ref, wout_bf_ref)
    _inproj_attn_kernel(False, x_ref, m_ref, g_ref, win_bf_ref, qt_scr, k_ref, vt_ref, sg_scr)
    _attn_kernel(False, 1, sink_ref, qt_scr, k_ref, vt_ref, sg_scr, x_ref, m_ref, wout_bf_ref, o_ref)


def _ctx_attention(sink, xc, mod3, mod_row, norm_g, w_in, w_out):
    b, n, _ = xc.shape
    const = lambda shape, **kw: pl.BlockSpec(shape, lambda bi, i: (0,) * len(shape), **kw)
    return pl.pallas_call(
        _ctx_attn_kernel,
        grid=(b, 1),
        in_specs=[pl.BlockSpec(memory_space=pltpu.SMEM),
                  pl.BlockSpec((1, n, D), lambda bi, i: (bi, 0, 0)),
                  pl.BlockSpec((1, 1, 3 * D), lambda bi, i: (mod_row, 0, 0)),
                  const((1, D)),
                  const(w_in.shape, pipeline_mode=pl.Buffered(1)),
                  const(w_out.shape, pipeline_mode=pl.Buffered(1))],
        out_specs=(pl.BlockSpec((1, n, A_KV_WIDTH), lambda bi, i: (bi, 0, 0)),
                   pl.BlockSpec((1, A_KV_WIDTH, n), lambda bi, i: (bi, 0, 0)),
                   pl.BlockSpec((1, n, D), lambda bi, i: (bi, 0, 0)),
                   const(w_in.shape), const(w_out.shape)),
        out_shape=(jax.ShapeDtypeStruct((b, n, A_KV_WIDTH), BF16),
                   jax.ShapeDtypeStruct((b, A_KV_WIDTH, n), BF16),
                   jax.ShapeDtypeStruct((b, n, D), F32),
                   jax.ShapeDtypeStruct(w_in.shape, BF16), jax.ShapeDtypeStruct(w_out.shape, BF16)),
        scratch_shapes=[pltpu.VMEM((1, A_WIDTH, n), BF16), pltpu.VMEM((1, n, A_WIDTH), BF16)],
        compiler_params=_params("arbitrary", "arbitrary"),
        name="ctx_attn",
    )(sink, xc, mod3, norm_g.reshape(1, D), w_in, w_out)


def _chunk_cumsum(x, tri):
    return _dot(tri, x.astype(BF16))


def _inproj_gla_kernel(chunk, with_q, x_ref, m_ref, g_ref, w_in_ref, wa1_ref, wa2_ref, ba_ref, tri_ref, *outs):
    if with_q:
        w_ref = w_in_ref
        per_dir = (outs[0:3], outs[3:6])
        v_ref, sg_ref = outs[6:]
    else:
        w_ref = outs[2]
        _cast_once(w_in_ref, w_ref)
        per_dir = ((outs[0],), (outs[1],))
    tm = x_ref.shape[1]
    hb = _norm_mod(x_ref[0], g_ref[...], m_ref[0]).astype(BF16)
    r = _dot(hb, wa1_ref[...])
    z = _dot(r.astype(BF16), wa2_ref[...]) + ba_ref[...]
    e = jnp.exp2(jnp.abs(z) * -LOG2_E)
    la = (jnp.minimum(z, 0.0) * LOG2_E - jnp.log2(1.0 + e)) * (1.0 / GATE_TEMP)
    k = _dot(hb, w_ref[:, B_K_WIDTH:2 * B_K_WIDTH])
    if with_q:
        q = _dot(hb, w_ref[:, 0:B_K_WIDTH]) * K_SCALE
    n_chunks = tm // chunk
    k_ends = []
    for reverse in (False, True):
        lad = la[:, B_K_WIDTH:] if reverse else la[:, 0:B_K_WIDTH]
        refs = per_dir[1] if reverse else per_dir[0]
        tots, kis, k_end = [], [], None
        tri = tri_ref[1 if reverse else 0]
        tb = tri.shape[0]
        cum_all = jnp.concatenate([_chunk_cumsum(lad[r * tb:(r + 1) * tb], tri) for r in range(tm // tb)], axis=0)
        for c in range(n_chunks):
            rows = slice(c * chunk, (c + 1) * chunk)
            cum = cum_all[rows]
            tot = cum[0:1] if reverse else cum[chunk - 1:chunk]
            if not with_q:
                k_end = (k[rows] * jnp.exp2(tot - cum)).astype(BF16)
                continue
            refs[0][0, rows] = (q[rows] * jnp.exp2(cum)).astype(BF16)
            kis.append(k[rows] * jnp.exp2(-cum))
            tots.append(tot)
            if c % 2 == 1:
                pair = slice((c - 1) * chunk, (c + 1) * chunk)
                refs[1][0, :, pair] = jnp.concatenate(kis[-2:], axis=0).T.astype(BF16)
        if with_q:
            pad = [jnp.zeros((DEC_ROWS - n_chunks, B_K_WIDTH), F32)] if n_chunks < DEC_ROWS else []
            refs[2][0, 0] = jnp.exp2(jnp.concatenate(tots + pad, axis=0))
        else:
            k_ends.append(k_end)
    if with_q:
        sg_ref[0] = _silu(_dot(hb, w_ref[:, 2 * B_K_WIDTH + B_V_WIDTH:])).astype(BF16)
    v = _dot(hb, w_ref[:, 2 * B_K_WIDTH:2 * B_K_WIDTH + B_V_WIDTH]).astype(BF16)
    if with_q:
        v_ref[0] = v
    else:
        for refs, k_end in zip(per_dir, k_ends):
            for h in range(B_HEADS):
                ks = slice(h * B_KEY_DIM, (h + 1) * B_KEY_DIM)
                refs[0][0, h] = _dot_tn(k_end[:, ks], v[:, h * B_VAL_DIM:(h + 1) * B_VAL_DIM])


def _inproj_gla(x, mod3, mod_row, norm_g, w, wa1, wa2, ba, chunk, with_q):
    b, s, _ = x.shape
    tm = min(ROW_TILE, s)
    nt = s // tm
    cpt = tm // chunk
    tb = max(chunk, MXU_DIM)
    assert tm % tb == 0 and tb % chunk == 0
    if mod_row is None:
        mod_map = lambda bi, i: (bi, 0, 0)
    else:
        mod_map = lambda bi, i: (mod_row, 0, 0)
    const = lambda shape, **kw: pl.BlockSpec(shape, lambda bi, i: (0,) * len(shape), **kw)
    in_specs = [
        pl.BlockSpec((1, tm, D), lambda bi, i: (bi, i, 0)),
        pl.BlockSpec((1, 1, 3 * D), mod_map),
        const((1, D)), const(w.shape, **({} if with_q else {"pipeline_mode": pl.Buffered(1)})),
        const(wa1.shape), const(wa2.shape), const(ba.shape),
        const((2, tb, tb)),
    ]
    t_idx = np.arange(tb)
    same = (t_idx[:, None] // chunk) == (t_idx[None, :] // chunk)
    lower = same & (t_idx[None, :] <= t_idx[:, None])
    tri = jnp.asarray(np.stack([lower, lower.T]), dtype=BF16)
    rows = lambda w_: (jax.ShapeDtypeStruct((b, s, w_), BF16),
                       pl.BlockSpec((1, tm, w_), lambda bi, i: (bi, i, 0)))
    cols = (jax.ShapeDtypeStruct((b, B_K_WIDTH, s), BF16),
            pl.BlockSpec((1, B_K_WIDTH, tm), lambda bi, i: (bi, 0, i)))
    decs = (jax.ShapeDtypeStruct((b, nt, DEC_ROWS, B_K_WIDTH), F32),
            pl.BlockSpec((1, 1, DEC_ROWS, B_K_WIDTH), lambda bi, i: (bi, i, 0, 0)))
    if with_q:
        assert cpt <= DEC_ROWS and cpt % 2 == 0
        outs = [rows(B_K_WIDTH), cols, decs] * 2 + [rows(B_V_WIDTH)] * 2
    else:
        assert nt == 1 and cpt == 1
        state = (jax.ShapeDtypeStruct((b, B_HEADS, B_KEY_DIM, B_VAL_DIM), F32),
                 pl.BlockSpec((1, B_HEADS, B_KEY_DIM, B_VAL_DIM), lambda bi, i: (bi, 0, 0, 0)))
        outs = [state] * 2 + [(jax.ShapeDtypeStruct(w.shape, BF16), const(w.shape))]
    return pl.pallas_call(
        functools.partial(_inproj_gla_kernel, chunk, with_q),
        grid=(b, nt),
        in_specs=in_specs,
        out_specs=tuple(o[1] for o in outs),
        out_shape=tuple(o[0] for o in outs),
        compiler_params=(_params("parallel", "parallel") if with_q
                         else _params("arbitrary", "arbitrary", vmem_limit=SMALL_VMEM_LIMIT)),
        name="inproj_gla" if with_q else "inproj_gla_ctx",
    )(x, mod3, norm_g.reshape(1, D), w, wa1, wa2, ba, tri)


def _scan_tile(reverse, qd_ref, kit_ref, dec_ref, v_ref, st_ref):
    pair_rows = 2 * CHUNK
    ti = lax.broadcasted_iota(jnp.int32, (pair_rows, pair_rows), 0)
    si = lax.broadcasted_iota(jnp.int32, (pair_rows, pair_rows), 1)
    same = (ti // CHUNK) == (si // CHUNK)
    if reverse:
        use_inv = same & (ti <= si)
        use_end = (ti < CHUNK) & (si >= CHUNK)
        second_row = lax.broadcasted_iota(jnp.int32, (pair_rows, B_KEY_DIM), 0) < CHUNK
        first_col = lax.broadcasted_iota(jnp.int32, (B_KEY_DIM, pair_rows), 1) >= CHUNK
    else:
        use_inv = same & (ti >= si)
        use_end = (ti >= CHUNK) & (si < CHUNK)
        second_row = lax.broadcasted_iota(jnp.int32, (pair_rows, B_KEY_DIM), 0) >= CHUNK
        first_col = lax.broadcasted_iota(jnp.int32, (B_KEY_DIM, pair_rows), 1) < CHUNK
    n_pairs = SCAN_TILE // pair_rows
    order = range(n_pairs - 1, -1, -1) if reverse else range(n_pairs)
    n_dec = dec_ref.shape[1] * DEC_ROWS
    dec = dec_ref[0].reshape(n_dec, B_K_WIDTH)
    dect = jnp.concatenate([dec, jnp.zeros((LANES - n_dec, B_K_WIDTH), F32)], axis=0).T
    heads = []
    for h in range(B_HEADS):
        ks = slice(h * B_KEY_DIM, (h + 1) * B_KEY_DIM)
        vs = slice(h * B_VAL_DIM, (h + 1) * B_VAL_DIM)
        st = st_ref[h]
        o_rows = [None] * n_pairs
        for p in order:
            rows = slice(p * pair_rows, (p + 1) * pair_rows)
            c_first, c_second = (2 * p + 1, 2 * p) if reverse else (2 * p, 2 * p + 1)
            qd = qd_ref[0, rows, ks]
            kit = kit_ref[0, ks, rows]
            vh = v_ref[0, rows, vs]
            d_first, d_second = dect[ks, c_first:c_first + 1], dect[ks, c_second:c_second + 1]
            ki32 = kit.astype(F32)
            s2 = _dot(qd, jnp.concatenate([kit, (ki32 * d_first).astype(BF16)], axis=1))
            a = jnp.where(use_inv, s2[:, 0:pair_rows], jnp.where(use_end, s2[:, pair_rows:], 0.0)).astype(BF16)
            q_pair = jnp.where(second_row, qd.astype(F32) * dec[c_first:c_first + 1, ks], qd.astype(F32))
            k_pair = ki32 * jnp.where(first_col, d_first * d_second, d_second)
            o_rows[p] = _dot(a, vh) + _dot(q_pair.astype(BF16), st.astype(BF16))
            st = st * (d_first * d_second) + _dot(k_pair.astype(BF16), vh)
        st_ref[h] = st
        heads.append(jnp.concatenate(o_rows, axis=0))
    return jnp.concatenate(heads, axis=1)


def _gla_scan_kernel(nt, qdf_ref, kitf_ref, decf_ref, qdb_ref, kitb_ref, decb_ref,
                     v_ref, sf_ref, sb_ref, sg_ref, x_ref, m_ref, hn_ref, w32_ref, fn_ref,
                     o_ref, st_ref, of_ref, w_ref):
    _cast_once(w32_ref, w_ref)
    j = pl.program_id(1)

    @pl.when(j == 0)
    def _():
        st_ref[...] = sf_ref[0]

    @pl.when(j == nt)
    def _():
        st_ref[...] = sb_ref[0]

    @pl.when(j < nt)
    def _():
        of_ref[j] = _scan_tile(False, qdf_ref, kitf_ref, decf_ref, v_ref, st_ref)

    @pl.when(j >= nt)
    def _():
        o = _scan_tile(True, qdb_ref, kitb_ref, decb_ref, v_ref, st_ref) + of_ref[2 * nt - 1 - j]
        normed = []
        for h in range(B_HEADS):
            oh = o[:, h * B_VAL_DIM:(h + 1) * B_VAL_DIM]
            normed.append(oh * lax.rsqrt(jnp.mean(oh * oh, axis=-1, keepdims=True) + EPS))
        of = jnp.concatenate(normed, axis=1) * hn_ref[...]
        y = _dot((of * sg_ref[0].astype(F32)).astype(BF16), w_ref[...])
        xn = x_ref[0] + m_ref[0][:, 2 * D:3 * D] * y
        ms = jnp.mean(xn * xn, axis=-1, keepdims=True)
        o_ref[0] = xn * lax.rsqrt(ms + EPS) * fn_ref[...]


def _gla_scan(fwd, bwd, v, s_f, s_b, sg, x, mod3, head_g, w_out, final_g):
    b, s, _ = v.shape
    assert SCAN_TILE % ROW_TILE == 0 and ROW_TILE // CHUNK == DEC_ROWS
    dec_blocks = SCAN_TILE // ROW_TILE
    nt = s // SCAN_TILE
    t_fwd = lambda j: jnp.minimum(j, nt - 1)
    t_bwd = lambda j: 2 * nt - 1 - jnp.maximum(j, nt)
    t_both = lambda j: jnp.where(j < nt, j, 2 * nt - 1 - j)

    def direction(t):
        rows = pl.BlockSpec((1, SCAN_TILE, B_K_WIDTH), lambda bi, j: (bi, t(j), 0))
        cols = pl.BlockSpec((1, B_K_WIDTH, SCAN_TILE), lambda bi, j: (bi, 0, t(j)))
        decs = pl.BlockSpec((1, dec_blocks, DEC_ROWS, B_K_WIDTH), lambda bi, j: (bi, t(j), 0, 0))
        return [rows, cols, decs]

    const = lambda shape: pl.BlockSpec(shape, lambda bi, j: (0,) * len(shape))
    state = pl.BlockSpec((1, B_HEADS, B_KEY_DIM, B_VAL_DIM), lambda bi, j: (bi, 0, 0, 0))
    out_rows = lambda w_: pl.BlockSpec((1, SCAN_TILE, w_), lambda bi, j: (bi, t_bwd(j), 0))
    in_specs = direction(t_fwd) + direction(t_bwd) + [
        pl.BlockSpec((1, SCAN_TILE, B_V_WIDTH), lambda bi, j: (bi, t_both(j), 0)),
        state, state, out_rows(B_V_WIDTH), out_rows(D),
        pl.BlockSpec((1, 1, 3 * D), lambda bi, j: (bi, 0, 0)),
        const((1, B_V_WIDTH)),
        pl.BlockSpec((B_V_WIDTH, D), lambda bi, j: (0, 0), pipeline_mode=pl.Buffered(1)),
        const((1, D))]
    return pl.pallas_call(
        functools.partial(_gla_scan_kernel, nt),
        grid=(b, 2 * nt),
        in_specs=in_specs,
        out_specs=out_rows(D),
        out_shape=jax.ShapeDtypeStruct((b, s, D), F32),
        scratch_shapes=[pltpu.VMEM((B_HEADS, B_KEY_DIM, B_VAL_DIM), F32),
                        pltpu.VMEM((nt, SCAN_TILE, B_V_WIDTH), F32),
                        pltpu.VMEM((B_V_WIDTH, D), BF16)],
        compiler_params=_params("arbitrary", "arbitrary", vmem_limit=SCAN_VMEM_LIMIT),
        name="gla_scan",
    )(*fwd, *bwd, v, s_f, s_b, sg, x, mod3, head_g.reshape(1, B_V_WIDTH), w_out, final_g.reshape(1, D))


def _rope_tables(n_tokens):
    rows_n = n_tokens // GRID_W
    row = np.repeat(np.arange(rows_n, dtype=np.float32), GRID_W)
    col = np.tile(np.arange(GRID_W, dtype=np.float32), rows_n)
    inv_freq = (np.float32(ROPE_BASE) ** (-np.arange(ROPE_FREQS, dtype=np.float32) / np.float32(ROPE_FREQS)))
    inv_freq = inv_freq.astype(np.float32)
    ang = np.stack([row[:, None] * inv_freq, col[:, None] * inv_freq], axis=1)
    cos, sin = np.cos(ang).astype(np.float32), np.sin(ang).astype(np.float32)
    zero = np.zeros_like(sin)
    tile = lambda t: np.tile(t.reshape(n_tokens, A_HEAD_DIM), (1, LANES // A_HEAD_DIM))
    return (tile(np.stack([cos, cos], axis=2)),
            tile(np.stack([-sin, zero], axis=2)),
            tile(np.stack([zero, sin], axis=2)))


def _pair_heads(t, axis):
    shape = t.shape
    t = t.reshape(shape[:axis] + (A_KV_HEADS, A_GROUP, -1) + shape[axis + 1:])
    return jnp.swapaxes(t, axis, axis + 1).reshape(shape)


def kernel(x, c, ctx, c_ctx, l0_norm_g, l0_w_ada, l0_b_ada, l0_w_in, l0_sink, l0_w_out, l1_norm_g, l1_w_ada, l1_b_ada, l1_w_in, l1_wa1_f, l1_wa2_f, l1_ba_f, l1_wa1_b, l1_wa2_b, l1_ba_b, l1_head_norm_g, l1_w_out, final_norm_g):
    b, s, _ = x.shape
    ctx_row = b

    cvec = jnp.concatenate([c, c_ctx[None, :], jnp.zeros((MOD_ROWS - b - 1, D), F32)], axis=0)
    mod0, mod1 = (m.reshape(MOD_ROWS, 1, 3 * D) for m in _modulation(cvec, l0_w_ada, l0_b_ada, l1_w_ada, l1_b_ada))

    sink = _pair_heads(l0_sink.astype(F32), 0)
    kc, vct, xc1, w0_in, w0_out = _ctx_attention(sink, ctx, mod0, ctx_row, l0_norm_g, l0_w_in, l0_w_out)
    qt, k, vt, sg = _inproj_attn(x, mod0, l0_norm_g, w0_in, _rope_tables(s))
    x1 = _attention(sink, qt, k, vt, kc, vct, sg, x, mod0, w0_out)

    wa1 = jnp.concatenate([l1_wa1_f, l1_wa1_b, jnp.zeros((D, LANES - 2 * GATE_RANK), F32)], axis=1).astype(BF16)
    wa2 = jnp.zeros((LANES, 2 * B_K_WIDTH), F32)
    wa2 = wa2.at[0:GATE_RANK, 0:B_K_WIDTH].set(l1_wa2_f)
    wa2 = wa2.at[GATE_RANK:2 * GATE_RANK, B_K_WIDTH:].set(l1_wa2_b).astype(BF16)
    ba = jnp.concatenate([l1_ba_f, l1_ba_b]).reshape(1, 2 * B_K_WIDTH)
    s_f, s_b, w1 = _inproj_gla(xc1, mod1, ctx_row, l1_norm_g, l1_w_in, wa1, wa2, ba, ctx.shape[1], False)
    outs = _inproj_gla(x1, mod1, None, l1_norm_g, w1, wa1, wa2, ba, CHUNK, True)
    v1, sg1 = outs[6:]
    return _gla_scan(outs[0:3], outs[3:6], v1, s_f, s_b, sg1, x1, mod1,
                     l1_head_norm_g, l1_w_out, final_norm_g)
```

```python
import functools

import jax
import jax.numpy as jnp
import numpy as np
from jax import lax
from jax.experimental import pallas as pl
from jax.experimental.pallas import tpu as pltpu

F32 = jnp.float32
BF16 = jnp.bfloat16

D = 1024
GRID_W = 64
EPS = 1e-6
NEG_INF = -1e30

A_HEADS = 16
A_KV_HEADS = 2
A_GROUP = A_HEADS // A_KV_HEADS
A_HEAD_DIM = 64
A_WIDTH = A_HEADS * A_HEAD_DIM
A_KV_WIDTH = A_KV_HEADS * A_HEAD_DIM
BLOCK = 128
ROPE_BASE = 10000.0
ROPE_FREQS = A_HEAD_DIM // 4
Q_SCALE = A_HEAD_DIM ** -0.5
LOG2_E = 1.4426950408889634

B_HEADS = 4
B_K_WIDTH = D // 2
B_V_WIDTH = D
B_KEY_DIM = B_K_WIDTH // B_HEADS
B_VAL_DIM = B_V_WIDTH // B_HEADS
GATE_RANK = 16
GATE_TEMP = 16.0
CHUNK = 64
K_SCALE = B_KEY_DIM ** -0.5

LANES = 128
MXU_DIM = 256
MOD_ROWS = 16
ROW_TILE = 1024
A_ROW_TILE = 1024
SCAN_TILE = 1024
SUB_ROWS = 256
DEC_ROWS = ROW_TILE // CHUNK
SUM_ROWS = 16
A_BLOCKS_PER_STEP = 8
VMEM_LIMIT = 48 * 1024 * 1024
SMALL_VMEM_LIMIT = 40 * 1024 * 1024
SCAN_VMEM_LIMIT = 56 * 1024 * 1024


def _params(*sem, vmem_limit=VMEM_LIMIT):
    return pltpu.CompilerParams(dimension_semantics=sem, vmem_limit_bytes=vmem_limit)


def _silu(x):
    return x / (1.0 + jnp.exp(-x))


def _dot(a, b):
    return jnp.dot(a, b, preferred_element_type=F32)


def _dot_nt(a, b):
    return lax.dot_general(a, b, (((1,), (1,)), ((), ())), preferred_element_type=F32)


def _dot_tn(a, b):
    return lax.dot_general(a, b, (((0,), (0,)), ((), ())), preferred_element_type=F32)


def _norm_mod(x, g, m):
    ms = jnp.mean(x * x, axis=-1, keepdims=True)
    y = x * lax.rsqrt(ms + EPS) * g
    return y * (1.0 + m[:, D:2 * D]) + m[:, 0:D]


def _cast_once(src_ref, dst_ref):
    @pl.when((pl.program_id(0) == 0) & (pl.program_id(1) == 0))
    def _():
        def body(i, carry):
            rows = pl.ds(pl.multiple_of(i * LANES, LANES), LANES)
            dst_ref[rows, :] = src_ref[rows, :].astype(dst_ref.dtype)
            return carry
        lax.fori_loop(0, src_ref.shape[0] // LANES, body, 0)


def _mod_kernel(n_tiles, c_ref, w0_ref, b0_ref, w1_ref, b1_ref, o0_ref, o1_ref):
    s = _silu(c_ref[...]).astype(BF16)
    j = pl.program_id(0)

    @pl.when(j < n_tiles)
    def _():
        o0_ref[...] = _dot(s, w0_ref[...].astype(BF16)) + b0_ref[...]

    @pl.when(j >= n_tiles)
    def _():
        o1_ref[...] = _dot(s, w1_ref[...].astype(BF16)) + b1_ref[...]


def _modulation(cvec, w_ada0, b_ada0, w_ada1, b_ada1):
    n = w_ada0.shape[1] // D
    first = lambda j: (0, jnp.minimum(j, n - 1))
    second = lambda j: (0, jnp.maximum(j - n, 0))
    out = jax.ShapeDtypeStruct((MOD_ROWS, n * D), F32)
    return pl.pallas_call(
        functools.partial(_mod_kernel, n),
        grid=(2 * n,),
        in_specs=[
            pl.BlockSpec((MOD_ROWS, D), lambda j: (0, 0)),
            pl.BlockSpec((D, D), first), pl.BlockSpec((1, D), first),
            pl.BlockSpec((D, D), second), pl.BlockSpec((1, D), second),
        ],
        out_specs=(pl.BlockSpec((MOD_ROWS, D), first), pl.BlockSpec((MOD_ROWS, D), second)),
        out_shape=(out, out),
        compiler_params=_params("arbitrary", vmem_limit=SMALL_VMEM_LIMIT),
        name="modulation",
    )(cvec, w_ada0, b_ada0.reshape(1, n * D), w_ada1, b_ada1.reshape(1, n * D))


def _rope(t, cos, sin_hi, sin_lo):
    return t * cos + pltpu.roll(t, LANES - ROPE_FREQS, 1) * sin_hi + pltpu.roll(t, ROPE_FREQS, 1) * sin_lo


def _pair_cast_columns(src_ref, dst_ref):
    @pl.when((pl.program_id(0) == 0) & (pl.program_id(1) == 0))
    def _():
        kv_lo, kv_hi = A_WIDTH, A_WIDTH + 2 * A_KV_WIDTH

        def body(i, carry):
            rows = pl.ds(pl.multiple_of(i * LANES, LANES), LANES)
            src = src_ref[rows, :]
            dst_ref[rows, kv_lo:kv_hi] = src[:, kv_lo:kv_hi].astype(BF16)
            for base in (0, kv_hi):
                for p in range(A_GROUP):
                    a = base + p * A_HEAD_DIM
                    b = base + (p + A_GROUP) * A_HEAD_DIM
                    pair = jnp.concatenate([src[:, a:a + A_HEAD_DIM], src[:, b:b + A_HEAD_DIM]], axis=1)
                    dst_ref[rows, base + p * LANES:base + (p + 1) * LANES] = pair.astype(BF16)
            return carry
        lax.fori_loop(0, src_ref.shape[0] // LANES, body, 0)


def _inproj_attn_kernel(rope, x_ref, m_ref, g_ref, w_ref, *rest):
    wq_ref = w_ref.at[:, 0:A_WIDTH]
    wkv_ref = w_ref.at[:, A_WIDTH:A_WIDTH + 2 * A_KV_WIDTH]
    wg_ref = w_ref.at[:, A_WIDTH + 2 * A_KV_WIDTH:]
    if rope:
        cos_ref, shi_ref, slo_ref, qt_ref, k_ref, vt_ref, sg_ref = rest
    else:
        qt_ref, k_ref, vt_ref, sg_ref = rest
    assert x_ref.shape[1] % SUB_ROWS == 0
    for r in range(x_ref.shape[1] // SUB_ROWS):
        rows = slice(r * SUB_ROWS, (r + 1) * SUB_ROWS)
        if rope:
            pos = pl.ds(pl.multiple_of(pl.program_id(1) * x_ref.shape[1] + r * SUB_ROWS, SUB_ROWS), SUB_ROWS)
            cos, shi, slo = cos_ref[pos], shi_ref[pos], slo_ref[pos]
        hb = _norm_mod(x_ref[0, rows], g_ref[...], m_ref[0]).astype(BF16)
        q = _dot(hb, wq_ref[...])
        for j in range(A_WIDTH // LANES):
            qj = q[:, j * LANES:(j + 1) * LANES]
            if rope:
                qj = _rope(qj, cos, shi, slo)
            qt_ref[0, j * LANES:(j + 1) * LANES, rows] = (qj * (Q_SCALE * LOG2_E)).T.astype(BF16)
        kv = _dot(hb, wkv_ref[...])
        k = kv[:, 0:A_KV_WIDTH]
        if rope:
            k = _rope(k, cos, shi, slo)
        k_ref[0, rows] = k.astype(BF16)
        vt_ref[0, :, rows] = kv[:, A_KV_WIDTH:].T.astype(BF16)
        g = _dot(hb, wg_ref[...])
        sg_ref[0, rows] = _silu(g).astype(BF16)


def _inproj_attn(x, mod3, norm_g, w_in, tables):
    b, s, _ = x.shape
    tm = min(A_ROW_TILE, s)
    in_specs = [
        pl.BlockSpec((1, tm, D), lambda bi, i: (bi, i, 0)),
        pl.BlockSpec((1, 1, 3 * D), lambda bi, i: (bi, 0, 0)),
        pl.BlockSpec((1, D), lambda bi, i: (0, 0)),
        pl.BlockSpec(w_in.shape, lambda bi, i: (0, 0)),
    ] + [pl.BlockSpec((s, LANES), lambda bi, i: (0, 0))] * 3
    args = [x, mod3, norm_g.reshape(1, D), w_in, *tables]
    out_shape = (
        jax.ShapeDtypeStruct((b, A_WIDTH, s), BF16),
        jax.ShapeDtypeStruct((b, s, A_KV_WIDTH), BF16),
        jax.ShapeDtypeStruct((b, A_KV_WIDTH, s), BF16),
        jax.ShapeDtypeStruct((b, s, A_WIDTH), BF16),
    )
    out_specs = (
        pl.BlockSpec((1, A_WIDTH, tm), lambda bi, i: (bi, 0, i)),
        pl.BlockSpec((1, tm, A_KV_WIDTH), lambda bi, i: (bi, i, 0)),
        pl.BlockSpec((1, A_KV_WIDTH, tm), lambda bi, i: (bi, 0, i)),
        pl.BlockSpec((1, tm, A_WIDTH), lambda bi, i: (bi, i, 0)),
    )
    return pl.pallas_call(
        functools.partial(_inproj_attn_kernel, True),
        grid=(b, s // tm),
        in_specs=in_specs,
        out_specs=out_specs,
        out_shape=out_shape,
        compiler_params=_params("parallel", "parallel"),
        name="inproj_attn",
    )(*args)


def _attn_block(kwin, vtw, qts, sinks, ok_prev, ok_next):
    n_keys, nq = kwin.shape[0], qts.shape[1]
    lane = lax.broadcasted_iota(jnp.int32, kwin.shape, 1)
    kbd = jnp.concatenate([jnp.where(lane < A_HEAD_DIM, kwin, jnp.zeros_like(kwin)),
                           jnp.where(lane >= A_HEAD_DIM, kwin, jnp.zeros_like(kwin))], axis=0)
    ones = jnp.where(lax.broadcasted_iota(jnp.int32, (SUM_ROWS, n_keys), 0) == 0, 1.0, 0.0).astype(BF16)

    st = _dot(kbd, qts).astype(BF16)
    neg = jnp.asarray(NEG_INF, BF16)
    outs = []
    for hh in range(2):
        sh = st[hh * n_keys:(hh + 1) * n_keys]
        if ok_prev is not None:
            parts = [jnp.where(ok_prev, sh[0:BLOCK], neg),
                     sh[BLOCK:2 * BLOCK],
                     jnp.where(ok_next, sh[2 * BLOCK:3 * BLOCK], neg),
                     sh[3 * BLOCK:]]
        else:
            parts = [sh]
        mx = functools.reduce(jnp.maximum, [jnp.max(t, axis=0, keepdims=True) for t in parts])
        mxb = jnp.maximum(mx.astype(F32), sinks[hh]).astype(BF16)
        mx = mxb.astype(F32)
        probs = jnp.concatenate([jnp.exp2(t - mxb) for t in parts], axis=0)
        vt_h = jnp.concatenate([vtw[hh * A_HEAD_DIM:(hh + 1) * A_HEAD_DIM], ones], axis=0)
        ot = _dot(vt_h, probs)
        den = ot[A_HEAD_DIM:A_HEAD_DIM + 1] + jnp.exp2(sinks[hh] - mx)
        outs.append(ot[0:A_HEAD_DIM] * (1.0 / den))
    return jnp.concatenate(outs, axis=0)


def _pair_cast_rows(src_ref, dst_ref):
    @pl.when((pl.program_id(0) == 0) & (pl.program_id(1) == 0))
    def _():
        for p in range(A_GROUP):
            for half, h in enumerate((p, p + A_GROUP)):
                lo = p * LANES + half * A_HEAD_DIM
                dst_ref[lo:lo + A_HEAD_DIM, :] = src_ref[h * A_HEAD_DIM:(h + 1) * A_HEAD_DIM, :].astype(BF16)


def _attn_kernel(local, n_steps, sink_ref, qt_ref, *rest):
    n_blk = rest[-1].shape[1] // BLOCK
    if local:
        (kp_ref, kc_ref, kn_ref, vp_ref, vc_ref, vn_ref, kx_ref, vx_ref,
         sg_ref, x_ref, m_ref, w_ref, o_ref) = rest
        last = slice((n_blk - 1) * BLOCK, n_blk * BLOCK)
        k_blocks = ([kp_ref[0, last]] + [kc_ref[0, i * BLOCK:(i + 1) * BLOCK] for i in range(n_blk)]
                    + [kn_ref[0, 0:BLOCK]])
        v_blocks = ([vp_ref[0, :, last]] + [vc_ref[0, :, i * BLOCK:(i + 1) * BLOCK] for i in range(n_blk)]
                    + [vn_ref[0, :, 0:BLOCK]])
    else:
        kx_ref, vx_ref, sg_ref, x_ref, m_ref, w_ref, o_ref = rest
    step = pl.program_id(1)
    nq = A_GROUP * BLOCK
    chunk_of = lax.broadcasted_iota(jnp.int32, (1, nq), 1) // BLOCK
    sinks = []
    for hh in range(2):
        sk = jnp.full((1, nq), sink_ref[hh] * LOG2_E, F32)
        for c in range(1, A_GROUP):
            sk = jnp.where(chunk_of == c, sink_ref[2 * c + hh] * LOG2_E, sk)
        sinks.append(sk)
    if local:
        kj = lax.broadcasted_iota(jnp.int32, (BLOCK, nq), 0)
        qi = lax.broadcasted_iota(jnp.int32, (BLOCK, nq), 1) % BLOCK

    rows_out = []
    for blk in range(n_blk):
        qts = jnp.concatenate([qt_ref[0, c * LANES:(c + 1) * LANES, blk * BLOCK:(blk + 1) * BLOCK]
                               for c in range(A_GROUP)], axis=1)
        if local:
            kwin = jnp.concatenate(k_blocks[blk:blk + 3] + [kx_ref[0]], axis=0)
            vtw = jnp.concatenate(v_blocks[blk:blk + 3] + [vx_ref[0]], axis=1)
            ok_prev = (kj >= qi) & (step > 0) if blk == 0 else (kj >= qi)
            ok_next = (kj <= qi) & (step < n_steps - 1) if blk == n_blk - 1 else (kj <= qi)
        else:
            kwin, vtw, ok_prev, ok_next = kx_ref[0], vx_ref[0], None, None
        ot = _attn_block(kwin, vtw, qts, sinks, ok_prev, ok_next)
        rows = slice(blk * BLOCK, (blk + 1) * BLOCK)
        outs = []
        for c in range(A_GROUP):
            sg = sg_ref[0, rows, c * LANES:(c + 1) * LANES].astype(F32)
            outs.append((ot[:, c * BLOCK:(c + 1) * BLOCK].T * sg).astype(BF16))
        rows_out.append(jnp.concatenate(outs, axis=1))
    y = _dot(jnp.concatenate(rows_out, axis=0), w_ref[...])
    gate = m_ref[0][:, 2 * D:3 * D]
    o_ref[0] = x_ref[0] + gate * y


def _attention(sink, qt, k, vt, kx, vxt, sg, x, mod3, w_out):
    b, s, _ = sg.shape
    rows = A_BLOCKS_PER_STEP * BLOCK
    assert s % rows == 0
    ns = s // rows
    n_ctx = kx.shape[1]
    blk = lambda w: pl.BlockSpec((1, rows, w), lambda bi, i: (bi, i, 0))
    blk_t = lambda w: pl.BlockSpec((1, w, rows), lambda bi, i: (bi, 0, i))
    lo = lambda i: jnp.maximum(i - 1, 0)
    hi = lambda i: jnp.minimum(i + 1, ns - 1)
    in_specs = [pl.BlockSpec(memory_space=pltpu.SMEM), blk_t(A_WIDTH),
                pl.BlockSpec((1, rows, A_KV_WIDTH), lambda bi, i: (bi, lo(i), 0)),
                blk(A_KV_WIDTH),
                pl.BlockSpec((1, rows, A_KV_WIDTH), lambda bi, i: (bi, hi(i), 0)),
                pl.BlockSpec((1, A_KV_WIDTH, rows), lambda bi, i: (bi, 0, lo(i))),
                blk_t(A_KV_WIDTH),
                pl.BlockSpec((1, A_KV_WIDTH, rows), lambda bi, i: (bi, 0, hi(i))),
                pl.BlockSpec((1, n_ctx, A_KV_WIDTH), lambda bi, i: (bi, 0, 0)),
                pl.BlockSpec((1, A_KV_WIDTH, n_ctx), lambda bi, i: (bi, 0, 0)),
                blk(A_WIDTH), blk(D),
                pl.BlockSpec((1, 1, 3 * D), lambda bi, i: (bi, 0, 0)),
                pl.BlockSpec((A_WIDTH, D), lambda bi, i: (0, 0))]
    return pl.pallas_call(
        functools.partial(_attn_kernel, True, ns),
        grid=(b, ns),
        in_specs=in_specs,
        out_specs=blk(D),
        out_shape=jax.ShapeDtypeStruct((b, s, D), F32),
        compiler_params=_params("parallel", "parallel"),
        name="attn_local",
    )(sink, qt, k, k, k, vt, vt, vt, kx, vxt, sg, x, mod3, w_out)


def _ctx_attn_kernel(sink_ref, x_ref, m_ref, g_ref, win_ref, wout_ref, k_ref, vt_ref, o_ref,
                     win_bf_ref, wout_bf_ref, qt_scr, sg_scr):
    _pair_cast_columns(win_ref, win_bf_ref)
    _pair_cast_rows(wout_ref, wout_bf_ref)
    _inproj_attn_kernel(False, x_ref, m_ref, g_ref, win_bf_ref, qt_scr, k_ref, vt_ref, sg_scr)
    _attn_kernel(False, 1, sink_ref, qt_scr, k_ref, vt_ref, sg_scr, x_ref, m_ref, wout_bf_ref, o_ref)


def _ctx_attention(sink, xc, mod3, mod_row, norm_g, w_in, w_out):
    b, n, _ = xc.shape
    const = lambda shape, **kw: pl.BlockSpec(shape, lambda bi, i: (0,) * len(shape), **kw)
    return pl.pallas_call(
        _ctx_attn_kernel,
        grid=(b, 1),
        in_specs=[pl.BlockSpec(memory_space=pltpu.SMEM),
                  pl.BlockSpec((1, n, D), lambda bi, i: (bi, 0, 0)),
                  pl.BlockSpec((1, 1, 3 * D), lambda bi, i: (mod_row, 0, 0)),
                  const((1, D)),
                  const(w_in.shape, pipeline_mode=pl.Buffered(1)),
                  const(w_out.shape, pipeline_mode=pl.Buffered(1))],
        out_specs=(pl.BlockSpec((1, n, A_KV_WIDTH), lambda bi, i: (bi, 0, 0)),
                   pl.BlockSpec((1, A_KV_WIDTH, n), lambda bi, i: (bi, 0, 0)),
                   pl.BlockSpec((1, n, D), lambda bi, i: (bi, 0, 0)),
                   const(w_in.shape), const(w_out.shape)),
        out_shape=(jax.ShapeDtypeStruct((b, n, A_KV_WIDTH), BF16),
                   jax.ShapeDtypeStruct((b, A_KV_WIDTH, n), BF16),
                   jax.ShapeDtypeStruct((b, n, D), F32),
                   jax.ShapeDtypeStruct(w_in.shape, BF16), jax.ShapeDtypeStruct(w_out.shape, BF16)),
        scratch_shapes=[pltpu.VMEM((1, A_WIDTH, n), BF16), pltpu.VMEM((1, n, A_WIDTH), BF16)],
        compiler_params=_params("arbitrary", "arbitrary"),
        name="ctx_attn",
    )(sink, xc, mod3, norm_g.reshape(1, D), w_in, w_out)


def _chunk_cumsum(x, tri):
    return _dot(tri, x.astype(BF16))


def _inproj_gla_kernel(chunk, with_q, x_ref, m_ref, g_ref, w_in_ref, wa1_ref, wa2_ref, ba_ref, tri_ref, *outs):
    if with_q:
        w_ref = w_in_ref
        per_dir = (outs[0:3], outs[3:6])
        v_ref, sg_ref = outs[6:]
    else:
        w_ref = outs[2]
        _cast_once(w_in_ref, w_ref)
        per_dir = ((outs[0],), (outs[1],))
    tm = x_ref.shape[1]
    hb = _norm_mod(x_ref[0], g_ref[...], m_ref[0]).astype(BF16)
    r = _dot(hb, wa1_ref[...])
    z = _dot(r.astype(BF16), wa2_ref[...]) + ba_ref[...]
    e = jnp.exp2(jnp.abs(z) * -LOG2_E)
    la = (jnp.minimum(z, 0.0) * LOG2_E - jnp.log2(1.0 + e)) * (1.0 / GATE_TEMP)
    k = _dot(hb, w_ref[:, B_K_WIDTH:2 * B_K_WIDTH])
    if with_q:
        q = _dot(hb, w_ref[:, 0:B_K_WIDTH]) * K_SCALE
    n_chunks = tm // chunk
    k_ends = []
    for reverse in (False, True):
        lad = la[:, B_K_WIDTH:] if reverse else la[:, 0:B_K_WIDTH]
        refs = per_dir[1] if reverse else per_dir[0]
        tots, kis, k_end = [], [], None
        tri = tri_ref[1 if reverse else 0]
        tb = tri.shape[0]
        cum_all = jnp.concatenate([_chunk_cumsum(lad[r * tb:(r + 1) * tb], tri) for r in range(tm // tb)], axis=0)
        for c in range(n_chunks):
            rows = slice(c * chunk, (c + 1) * chunk)
            cum = cum_all[rows]
            tot = cum[0:1] if reverse else cum[chunk - 1:chunk]
            if not with_q:
                k_end = (k[rows] * jnp.exp2(tot - cum)).astype(BF16)
                continue
            refs[0][0, rows] = (q[rows] * jnp.exp2(cum)).astype(BF16)
            kis.append(k[rows] * jnp.exp2(-cum))
            tots.append(tot)
            if c % 2 == 1:
                pair = slice((c - 1) * chunk, (c + 1) * chunk)
                refs[1][0, :, pair] = jnp.concatenate(kis[-2:], axis=0).T.astype(BF16)
        if with_q:
            pad = [jnp.zeros((DEC_ROWS - n_chunks, B_K_WIDTH), F32)] if n_chunks < DEC_ROWS else []
            refs[2][0, 0] = jnp.exp2(jnp.concatenate(tots + pad, axis=0))
        else:
            k_ends.append(k_end)
    if with_q:
        sg_ref[0] = _silu(_dot(hb, w_ref[:, 2 * B_K_WIDTH + B_V_WIDTH:])).astype(BF16)
    v = _dot(hb, w_ref[:, 2 * B_K_WIDTH:2 * B_K_WIDTH + B_V_WIDTH]).astype(BF16)
    if with_q:
        v_ref[0] = v
    else:
        for refs, k_end in zip(per_dir, k_ends):
            for h in range(B_HEADS):
                ks = slice(h * B_KEY_DIM, (h + 1) * B_KEY_DIM)
                refs[0][0, h] = _dot_tn(k_end[:, ks], v[:, h * B_VAL_DIM:(h + 1) * B_VAL_DIM])


def _inproj_gla(x, mod3, mod_row, norm_g, w, wa1, wa2, ba, chunk, with_q):
    b, s, _ = x.shape
    tm = min(ROW_TILE, s)
    nt = s // tm
    cpt = tm // chunk
    tb = max(chunk, MXU_DIM)
    assert tm % tb == 0 and tb % chunk == 0
    if mod_row is None:
        mod_map = lambda bi, i: (bi, 0, 0)
    else:
        mod_map = lambda bi, i: (mod_row, 0, 0)
    const = lambda shape, **kw: pl.BlockSpec(shape, lambda bi, i: (0,) * len(shape), **kw)
    in_specs = [
        pl.BlockSpec((1, tm, D), lambda bi, i: (bi, i, 0)),
        pl.BlockSpec((1, 1, 3 * D), mod_map),
        const((1, D)), const(w.shape, **({} if with_q else {"pipeline_mode": pl.Buffered(1)})),
        const(wa1.shape), const(wa2.shape), const(ba.shape),
        const((2, tb, tb)),
    ]
    t_idx = np.arange(tb)
    same = (t_idx[:, None] // chunk) == (t_idx[None, :] // chunk)
    lower = same & (t_idx[None, :] <= t_idx[:, None])
    tri = jnp.asarray(np.stack([lower, lower.T]), dtype=BF16)
    rows = lambda w_: (jax.ShapeDtypeStruct((b, s, w_), BF16),
                       pl.BlockSpec((1, tm, w_), lambda bi, i: (bi, i, 0)))
    cols = (jax.ShapeDtypeStruct((b, B_K_WIDTH, s), BF16),
            pl.BlockSpec((1, B_K_WIDTH, tm), lambda bi, i: (bi, 0, i)))
    decs = (jax.ShapeDtypeStruct((b, nt, DEC_ROWS, B_K_WIDTH), F32),
            pl.BlockSpec((1, 1, DEC_ROWS, B_K_WIDTH), lambda bi, i: (bi, i, 0, 0)))
    if with_q:
        assert cpt <= DEC_ROWS and cpt % 2 == 0
        outs = [rows(B_K_WIDTH), cols, decs] * 2 + [rows(B_V_WIDTH)] * 2
    else:
        assert nt == 1 and cpt == 1
        state = (jax.ShapeDtypeStruct((b, B_HEADS, B_KEY_DIM, B_VAL_DIM), F32),
                 pl.BlockSpec((1, B_HEADS, B_KEY_DIM, B_VAL_DIM), lambda bi, i: (bi, 0, 0, 0)))
        outs = [state] * 2 + [(jax.ShapeDtypeStruct(w.shape, BF16), const(w.shape))]
    return pl.pallas_call(
        functools.partial(_inproj_gla_kernel, chunk, with_q),
        grid=(b, nt),
        in_specs=in_specs,
        out_specs=tuple(o[1] for o in outs),
        out_shape=tuple(o[0] for o in outs),
        compiler_params=_params("parallel", "parallel") if with_q else _params("arbitrary", "arbitrary"),
        name="inproj_gla" if with_q else "inproj_gla_ctx",
    )(x, mod3, norm_g.reshape(1, D), w, wa1, wa2, ba, tri)


def _scan_tile(reverse, qd_ref, kit_ref, dec_ref, v_ref, st_ref):
    pair_rows = 2 * CHUNK
    ti = lax.broadcasted_iota(jnp.int32, (pair_rows, pair_rows), 0)
    si = lax.broadcasted_iota(jnp.int32, (pair_rows, pair_rows), 1)
    same = (ti // CHUNK) == (si // CHUNK)
    if reverse:
        use_inv = same & (ti <= si)
        use_end = (ti < CHUNK) & (si >= CHUNK)
        second_row = lax.broadcasted_iota(jnp.int32, (pair_rows, B_KEY_DIM), 0) < CHUNK
        first_col = lax.broadcasted_iota(jnp.int32, (B_KEY_DIM, pair_rows), 1) >= CHUNK
    else:
        use_inv = same & (ti >= si)
        use_end = (ti >= CHUNK) & (si < CHUNK)
        second_row = lax.broadcasted_iota(jnp.int32, (pair_rows, B_KEY_DIM), 0) >= CHUNK
        first_col = lax.broadcasted_iota(jnp.int32, (B_KEY_DIM, pair_rows), 1) < CHUNK
    n_pairs = SCAN_TILE // pair_rows
    order = range(n_pairs - 1, -1, -1) if reverse else range(n_pairs)
    n_dec = dec_ref.shape[1] * DEC_ROWS
    dec = dec_ref[0].reshape(n_dec, B_K_WIDTH)
    dect = jnp.concatenate([dec, jnp.zeros((LANES - n_dec, B_K_WIDTH), F32)], axis=0).T
    heads = []
    for h in range(B_HEADS):
        ks = slice(h * B_KEY_DIM, (h + 1) * B_KEY_DIM)
        vs = slice(h * B_VAL_DIM, (h + 1) * B_VAL_DIM)
        st = st_ref[h]
        o_rows = [None] * n_pairs
        for p in order:
            rows = slice(p * pair_rows, (p + 1) * pair_rows)
            c_first, c_second = (2 * p + 1, 2 * p) if reverse else (2 * p, 2 * p + 1)
            qd = qd_ref[0, rows, ks]
            kit = kit_ref[0, ks, rows]
            vh = v_ref[0, rows, vs]
            d_first, d_second = dect[ks, c_first:c_first + 1], dect[ks, c_second:c_second + 1]
            ki32 = kit.astype(F32)
            s2 = _dot(qd, jnp.concatenate([kit, (ki32 * d_first).astype(BF16)], axis=1))
            a = jnp.where(use_inv, s2[:, 0:pair_rows], jnp.where(use_end, s2[:, pair_rows:], 0.0)).astype(BF16)
            q_pair = jnp.where(second_row, qd.astype(F32) * dec[c_first:c_first + 1, ks], qd.astype(F32))
            k_pair = ki32 * jnp.where(first_col, d_first * d_second, d_second)
            o_rows[p] = _dot(a, vh) + _dot(q_pair.astype(BF16), st.astype(BF16))
            st = st * (d_first * d_second) + _dot(k_pair.astype(BF16), vh)
        st_ref[h] = st
        heads.append(jnp.concatenate(o_rows, axis=0))
    return jnp.concatenate(heads, axis=1)


def _gla_scan_kernel(nt, qdf_ref, kitf_ref, decf_ref, qdb_ref, kitb_ref, decb_ref,
                     v_ref, sf_ref, sb_ref, sg_ref, x_ref, m_ref, hn_ref, w32_ref, fn_ref,
                     o_ref, st_ref, of_ref, w_ref):
    _cast_once(w32_ref, w_ref)
    j = pl.program_id(1)

    @pl.when(j == 0)
    def _():
        st_ref[...] = sf_ref[0]

    @pl.when(j == nt)
    def _():
        st_ref[...] = sb_ref[0]

    @pl.when(j < nt)
    def _():
        of_ref[j] = _scan_tile(False, qdf_ref, kitf_ref, decf_ref, v_ref, st_ref)

    @pl.when(j >= nt)
    def _():
        o = _scan_tile(True, qdb_ref, kitb_ref, decb_ref, v_ref, st_ref) + of_ref[2 * nt - 1 - j]
        normed = []
        for h in range(B_HEADS):
            oh = o[:, h * B_VAL_DIM:(h + 1) * B_VAL_DIM]
            normed.append(oh * lax.rsqrt(jnp.mean(oh * oh, axis=-1, keepdims=True) + EPS))
        of = jnp.concatenate(normed, axis=1) * hn_ref[...]
        y = _dot((of * sg_ref[0].astype(F32)).astype(BF16), w_ref[...])
        xn = x_ref[0] + m_ref[0][:, 2 * D:3 * D] * y
        ms = jnp.mean(xn * xn, axis=-1, keepdims=True)
        o_ref[0] = xn * lax.rsqrt(ms + EPS) * fn_ref[...]


def _gla_scan(fwd, bwd, v, s_f, s_b, sg, x, mod3, head_g, w_out, final_g):
    b, s, _ = v.shape
    assert SCAN_TILE % ROW_TILE == 0 and ROW_TILE // CHUNK == DEC_ROWS
    dec_blocks = SCAN_TILE // ROW_TILE
    nt = s // SCAN_TILE
    t_fwd = lambda j: jnp.minimum(j, nt - 1)
    t_bwd = lambda j: 2 * nt - 1 - jnp.maximum(j, nt)
    t_both = lambda j: jnp.where(j < nt, j, 2 * nt - 1 - j)

    def direction(t):
        rows = pl.BlockSpec((1, SCAN_TILE, B_K_WIDTH), lambda bi, j: (bi, t(j), 0))
        cols = pl.BlockSpec((1, B_K_WIDTH, SCAN_TILE), lambda bi, j: (bi, 0, t(j)))
        decs = pl.BlockSpec((1, dec_blocks, DEC_ROWS, B_K_WIDTH), lambda bi, j: (bi, t(j), 0, 0))
        return [rows, cols, decs]

    const = lambda shape: pl.BlockSpec(shape, lambda bi, j: (0,) * len(shape))
    state = pl.BlockSpec((1, B_HEADS, B_KEY_DIM, B_VAL_DIM), lambda bi, j: (bi, 0, 0, 0))
    out_rows = lambda w_: pl.BlockSpec((1, SCAN_TILE, w_), lambda bi, j: (bi, t_bwd(j), 0))
    in_specs = direction(t_fwd) + direction(t_bwd) + [
        pl.BlockSpec((1, SCAN_TILE, B_V_WIDTH), lambda bi, j: (bi, t_both(j), 0)),
        state, state, out_rows(B_V_WIDTH), out_rows(D),
        pl.BlockSpec((1, 1, 3 * D), lambda bi, j: (bi, 0, 0)),
        const((1, B_V_WIDTH)),
        pl.BlockSpec((B_V_WIDTH, D), lambda bi, j: (0, 0), pipeline_mode=pl.Buffered(1)),
        const((1, D))]
    return pl.pallas_call(
        functools.partial(_gla_scan_kernel, nt),
        grid=(b, 2 * nt),
        in_specs=in_specs,
        out_specs=out_rows(D),
        out_shape=jax.ShapeDtypeStruct((b, s, D), F32),
        scratch_shapes=[pltpu.VMEM((B_HEADS, B_KEY_DIM, B_VAL_DIM), F32),
                        pltpu.VMEM((nt, SCAN_TILE, B_V_WIDTH), F32),
                        pltpu.VMEM((B_V_WIDTH, D), BF16)],
        compiler_params=_params("arbitrary", "arbitrary", vmem_limit=SCAN_VMEM_LIMIT),
        name="gla_scan",
    )(*fwd, *bwd, v, s_f, s_b, sg, x, mod3, head_g.reshape(1, B_V_WIDTH), w_out, final_g.reshape(1, D))


def _rope_tables(n_tokens):
    rows_n = n_tokens // GRID_W
    row = np.repeat(np.arange(rows_n, dtype=np.float32), GRID_W)
    col = np.tile(np.arange(GRID_W, dtype=np.float32), rows_n)
    inv_freq = (np.float32(ROPE_BASE) ** (-np.arange(ROPE_FREQS, dtype=np.float32) / np.float32(ROPE_FREQS)))
    inv_freq = inv_freq.astype(np.float32)
    ang = np.stack([row[:, None] * inv_freq, col[:, None] * inv_freq], axis=1)
    cos, sin = np.cos(ang).astype(np.float32), np.sin(ang).astype(np.float32)
    zero = np.zeros_like(sin)
    tile = lambda t: np.tile(t.reshape(n_tokens, A_HEAD_DIM), (1, LANES // A_HEAD_DIM))
    return (tile(np.stack([cos, cos], axis=2)),
            tile(np.stack([-sin, zero], axis=2)),
            tile(np.stack([zero, sin], axis=2)))


def _pair_heads(t, axis):
    shape = t.shape
    t = t.reshape(shape[:axis] + (A_KV_HEADS, A_GROUP, -1) + shape[axis + 1:])
    return jnp.swapaxes(t, axis, axis + 1).reshape(shape)


def kernel(x, c, ctx, c_ctx, l0_norm_g, l0_w_ada, l0_b_ada, l0_w_in, l0_sink, l0_w_out, l1_norm_g, l1_w_ada, l1_b_ada, l1_w_in, l1_wa1_f, l1_wa2_f, l1_ba_f, l1_wa1_b, l1_wa2_b, l1_ba_b, l1_head_norm_g, l1_w_out, final_norm_g):
    b, s, _ = x.shape
    ctx_row = b

    cvec = jnp.concatenate([c, c_ctx[None, :], jnp.zeros((MOD_ROWS - b - 1, D), F32)], axis=0)
    mod0, mod1 = (m.reshape(MOD_ROWS, 1, 3 * D) for m in _modulation(cvec, l0_w_ada, l0_b_ada, l1_w_ada, l1_b_ada))

    sink = _pair_heads(l0_sink.astype(F32), 0)
    kc, vct, xc1, w0_in, w0_out = _ctx_attention(sink, ctx, mod0, ctx_row, l0_norm_g, l0_w_in, l0_w_out)
    qt, k, vt, sg = _inproj_attn(x, mod0, l0_norm_g, w0_in, _rope_tables(s))
    x1 = _attention(sink, qt, k, vt, kc, vct, sg, x, mod0, w0_out)

    wa1 = jnp.concatenate([l1_wa1_f, l1_wa1_b, jnp.zeros((D, LANES - 2 * GATE_RANK), F32)], axis=1).astype(BF16)
    wa2 = jnp.zeros((LANES, 2 * B_K_WIDTH), F32)
    wa2 = wa2.at[0:GATE_RANK, 0:B_K_WIDTH].set(l1_wa2_f)
    wa2 = wa2.at[GATE_RANK:2 * GATE_RANK, B_K_WIDTH:].set(l1_wa2_b).astype(BF16)
    ba = jnp.concatenate([l1_ba_f, l1_ba_b]).reshape(1, 2 * B_K_WIDTH)
    s_f, s_b, w1 = _inproj_gla(xc1, mod1, ctx_row, l1_norm_g, l1_w_in, wa1, wa2, ba, ctx.shape[1], False)
    outs = _inproj_gla(x1, mod1, None, l1_norm_g, w1, wa1, wa2, ba, CHUNK, True)
    v1, sg1 = outs[6:]
    return _gla_scan(outs[0:3], outs[3:6], v1, s_f, s_b, sg1, x1, mod1,
                     l1_head_norm_g, l1_w_out, final_norm_g)
```

```python
import functools

import jax
import jax.numpy as jnp
import numpy as np
from jax import lax
from jax.experimental import pallas as pl
from jax.experimental.pallas import tpu as pltpu

F32 = jnp.float32
BF16 = jnp.bfloat16

D = 1024
GRID_W = 64
EPS = 1e-6
NEG_INF = -1e30

A_HEADS = 16
A_KV_HEADS = 2
A_GROUP = A_HEADS // A_KV_HEADS
A_HEAD_DIM = 64
A_WIDTH = A_HEADS * A_HEAD_DIM
A_KV_WIDTH = A_KV_HEADS * A_HEAD_DIM
BLOCK = 128
ROPE_BASE = 10000.0
ROPE_FREQS = A_HEAD_DIM // 4
Q_SCALE = A_HEAD_DIM ** -0.5
LOG2_E = 1.4426950408889634

B_HEADS = 4
B_K_WIDTH = D // 2
B_V_WIDTH = D
B_KEY_DIM = B_K_WIDTH // B_HEADS
B_VAL_DIM = B_V_WIDTH // B_HEADS
GATE_RANK = 16
GATE_TEMP = 16.0
CHUNK = 64
K_SCALE = B_KEY_DIM ** -0.5

LANES = 128
MXU_DIM = 256
MOD_ROWS = 16
ROW_TILE = 1024
A_ROW_TILE = 1024
SCAN_TILE = 1024
SUB_ROWS = 256
DEC_ROWS = ROW_TILE // CHUNK
SUM_ROWS = 16
A_BLOCKS_PER_STEP = 8
VMEM_LIMIT = 48 * 1024 * 1024
SCAN_VMEM_LIMIT = 54 * 1024 * 1024


def _params(*sem, vmem_limit=VMEM_LIMIT):
    return pltpu.CompilerParams(dimension_semantics=sem, vmem_limit_bytes=vmem_limit)


def _silu(x):
    return x / (1.0 + jnp.exp(-x))


def _dot(a, b):
    return jnp.dot(a, b, preferred_element_type=F32)


def _dot_nt(a, b):
    return lax.dot_general(a, b, (((1,), (1,)), ((), ())), preferred_element_type=F32)


def _dot_tn(a, b):
    return lax.dot_general(a, b, (((0,), (0,)), ((), ())), preferred_element_type=F32)


def _norm_mod(x, g, m):
    ms = jnp.mean(x * x, axis=-1, keepdims=True)
    y = x * lax.rsqrt(ms + EPS) * g
    return y * (1.0 + m[:, D:2 * D]) + m[:, 0:D]


def _cast_once(src_ref, dst_ref):
    @pl.when((pl.program_id(0) == 0) & (pl.program_id(1) == 0))
    def _():
        def body(i, carry):
            rows = pl.ds(pl.multiple_of(i * LANES, LANES), LANES)
            dst_ref[rows, :] = src_ref[rows, :].astype(dst_ref.dtype)
            return carry
        lax.fori_loop(0, src_ref.shape[0] // LANES, body, 0)


def _mod_kernel(n_tiles, c_ref, w0_ref, b0_ref, w1_ref, b1_ref, o0_ref, o1_ref):
    s = _silu(c_ref[...]).astype(BF16)
    j = pl.program_id(0)

    @pl.when(j < n_tiles)
    def _():
        o0_ref[...] = _dot(s, w0_ref[...].astype(BF16)) + b0_ref[...]

    @pl.when(j >= n_tiles)
    def _():
        o1_ref[...] = _dot(s, w1_ref[...].astype(BF16)) + b1_ref[...]


def _modulation(cvec, w_ada0, b_ada0, w_ada1, b_ada1):
    n = w_ada0.shape[1] // D
    first = lambda j: (0, jnp.minimum(j, n - 1))
    second = lambda j: (0, jnp.maximum(j - n, 0))
    out = jax.ShapeDtypeStruct((MOD_ROWS, n * D), F32)
    return pl.pallas_call(
        functools.partial(_mod_kernel, n),
        grid=(2 * n,),
        in_specs=[
            pl.BlockSpec((MOD_ROWS, D), lambda j: (0, 0)),
            pl.BlockSpec((D, D), first), pl.BlockSpec((1, D), first),
            pl.BlockSpec((D, D), second), pl.BlockSpec((1, D), second),
        ],
        out_specs=(pl.BlockSpec((MOD_ROWS, D), first), pl.BlockSpec((MOD_ROWS, D), second)),
        out_shape=(out, out),
        compiler_params=_params("arbitrary"),
        name="modulation",
    )(cvec, w_ada0, b_ada0.reshape(1, n * D), w_ada1, b_ada1.reshape(1, n * D))


def _rope(t, cos, sin_hi, sin_lo):
    return t * cos + pltpu.roll(t, LANES - ROPE_FREQS, 1) * sin_hi + pltpu.roll(t, ROPE_FREQS, 1) * sin_lo


def _pair_cast_columns(src_ref, dst_ref):
    @pl.when((pl.program_id(0) == 0) & (pl.program_id(1) == 0))
    def _():
        kv_lo, kv_hi = A_WIDTH, A_WIDTH + 2 * A_KV_WIDTH

        def body(i, carry):
            rows = pl.ds(pl.multiple_of(i * LANES, LANES), LANES)
            src = src_ref[rows, :]
            dst_ref[rows, kv_lo:kv_hi] = src[:, kv_lo:kv_hi].astype(BF16)
            for base in (0, kv_hi):
                for p in range(A_GROUP):
                    a = base + p * A_HEAD_DIM
                    b = base + (p + A_GROUP) * A_HEAD_DIM
                    pair = jnp.concatenate([src[:, a:a + A_HEAD_DIM], src[:, b:b + A_HEAD_DIM]], axis=1)
                    dst_ref[rows, base + p * LANES:base + (p + 1) * LANES] = pair.astype(BF16)
            return carry
        lax.fori_loop(0, src_ref.shape[0] // LANES, body, 0)


def _inproj_attn_kernel(rope, x_ref, m_ref, g_ref, w_ref, *rest):
    wq_ref = w_ref.at[:, 0:A_WIDTH]
    wkv_ref = w_ref.at[:, A_WIDTH:A_WIDTH + 2 * A_KV_WIDTH]
    wg_ref = w_ref.at[:, A_WIDTH + 2 * A_KV_WIDTH:]
    if rope:
        cos_ref, shi_ref, slo_ref, qt_ref, k_ref, vt_ref, sg_ref = rest
    else:
        qt_ref, k_ref, vt_ref, sg_ref = rest
    assert x_ref.shape[1] % SUB_ROWS == 0
    for r in range(x_ref.shape[1] // SUB_ROWS):
        rows = slice(r * SUB_ROWS, (r + 1) * SUB_ROWS)
        if rope:
            pos = pl.ds(pl.multiple_of(pl.program_id(1) * x_ref.shape[1] + r * SUB_ROWS, SUB_ROWS), SUB_ROWS)
            cos, shi, slo = cos_ref[pos], shi_ref[pos], slo_ref[pos]
        hb = _norm_mod(x_ref[0, rows], g_ref[...], m_ref[0]).astype(BF16)
        q = _dot(hb, wq_ref[...])
        for j in range(A_WIDTH // LANES):
            qj = q[:, j * LANES:(j + 1) * LANES]
            if rope:
                qj = _rope(qj, cos, shi, slo)
            qt_ref[0, j * LANES:(j + 1) * LANES, rows] = (qj * (Q_SCALE * LOG2_E)).T.astype(BF16)
        kv = _dot(hb, wkv_ref[...])
        k = kv[:, 0:A_KV_WIDTH]
        if rope:
            k = _rope(k, cos, shi, slo)
        k_ref[0, rows] = k.astype(BF16)
        vt_ref[0, :, rows] = kv[:, A_KV_WIDTH:].T.astype(BF16)
        g = _dot(hb, wg_ref[...])
        sg_ref[0, rows] = _silu(g).astype(BF16)


def _inproj_attn(x, mod3, norm_g, w_in, tables):
    b, s, _ = x.shape
    tm = min(A_ROW_TILE, s)
    in_specs = [
        pl.BlockSpec((1, tm, D), lambda bi, i: (bi, i, 0)),
        pl.BlockSpec((1, 1, 3 * D), lambda bi, i: (bi, 0, 0)),
        pl.BlockSpec((1, D), lambda bi, i: (0, 0)),
        pl.BlockSpec(w_in.shape, lambda bi, i: (0, 0)),
    ] + [pl.BlockSpec((s, LANES), lambda bi, i: (0, 0))] * 3
    args = [x, mod3, norm_g.reshape(1, D), w_in, *tables]
    out_shape = (
        jax.ShapeDtypeStruct((b, A_WIDTH, s), BF16),
        jax.ShapeDtypeStruct((b, s, A_KV_WIDTH), BF16),
        jax.ShapeDtypeStruct((b, A_KV_WIDTH, s), BF16),
        jax.ShapeDtypeStruct((b, s, A_WIDTH), BF16),
    )
    out_specs = (
        pl.BlockSpec((1, A_WIDTH, tm), lambda bi, i: (bi, 0, i)),
        pl.BlockSpec((1, tm, A_KV_WIDTH), lambda bi, i: (bi, i, 0)),
        pl.BlockSpec((1, A_KV_WIDTH, tm), lambda bi, i: (bi, 0, i)),
        pl.BlockSpec((1, tm, A_WIDTH), lambda bi, i: (bi, i, 0)),
    )
    return pl.pallas_call(
        functools.partial(_inproj_attn_kernel, True),
        grid=(b, s // tm),
        in_specs=in_specs,
        out_specs=out_specs,
        out_shape=out_shape,
        compiler_params=_params("parallel", "parallel"),
        name="inproj_attn",
    )(*args)


def _attn_block(kwin, vtw, qts, sinks, ok_prev, ok_next):
    n_keys, nq = kwin.shape[0], qts.shape[1]
    lane = lax.broadcasted_iota(jnp.int32, kwin.shape, 1)
    kbd = jnp.concatenate([jnp.where(lane < A_HEAD_DIM, kwin, jnp.zeros_like(kwin)),
                           jnp.where(lane >= A_HEAD_DIM, kwin, jnp.zeros_like(kwin))], axis=0)
    ones = jnp.where(lax.broadcasted_iota(jnp.int32, (SUM_ROWS, n_keys), 0) == 0, 1.0, 0.0).astype(BF16)

    st = _dot(kbd, qts).astype(BF16)
    neg = jnp.asarray(NEG_INF, BF16)
    outs = []
    for hh in range(2):
        sh = st[hh * n_keys:(hh + 1) * n_keys]
        if ok_prev is not None:
            parts = [jnp.where(ok_prev, sh[0:BLOCK], neg),
                     sh[BLOCK:2 * BLOCK],
                     jnp.where(ok_next, sh[2 * BLOCK:3 * BLOCK], neg),
                     sh[3 * BLOCK:]]
        else:
            parts = [sh]
        mx = functools.reduce(jnp.maximum, [jnp.max(t, axis=0, keepdims=True) for t in parts])
        mxb = jnp.maximum(mx.astype(F32), sinks[hh]).astype(BF16)
        mx = mxb.astype(F32)
        probs = jnp.concatenate([jnp.exp2(t - mxb) for t in parts], axis=0)
        vt_h = jnp.concatenate([vtw[hh * A_HEAD_DIM:(hh + 1) * A_HEAD_DIM], ones], axis=0)
        ot = _dot(vt_h, probs)
        den = ot[A_HEAD_DIM:A_HEAD_DIM + 1] + jnp.exp2(sinks[hh] - mx)
        outs.append(ot[0:A_HEAD_DIM] * (1.0 / den))
    return jnp.concatenate(outs, axis=0)


def _pair_cast_rows(src_ref, dst_ref):
    @pl.when((pl.program_id(0) == 0) & (pl.program_id(1) == 0))
    def _():
        for p in range(A_GROUP):
            for half, h in enumerate((p, p + A_GROUP)):
                lo = p * LANES + half * A_HEAD_DIM
                dst_ref[lo:lo + A_HEAD_DIM, :] = src_ref[h * A_HEAD_DIM:(h + 1) * A_HEAD_DIM, :].astype(BF16)


def _attn_kernel(local, n_steps, sink_ref, qt_ref, *rest):
    n_blk = rest[-1].shape[1] // BLOCK
    if local:
        (kp_ref, kc_ref, kn_ref, vp_ref, vc_ref, vn_ref, kx_ref, vx_ref,
         sg_ref, x_ref, m_ref, w_ref, o_ref) = rest
        last = slice((n_blk - 1) * BLOCK, n_blk * BLOCK)
        k_blocks = ([kp_ref[0, last]] + [kc_ref[0, i * BLOCK:(i + 1) * BLOCK] for i in range(n_blk)]
                    + [kn_ref[0, 0:BLOCK]])
        v_blocks = ([vp_ref[0, :, last]] + [vc_ref[0, :, i * BLOCK:(i + 1) * BLOCK] for i in range(n_blk)]
                    + [vn_ref[0, :, 0:BLOCK]])
    else:
        kx_ref, vx_ref, sg_ref, x_ref, m_ref, w_ref, o_ref = rest
    step = pl.program_id(1)
    nq = A_GROUP * BLOCK
    chunk_of = lax.broadcasted_iota(jnp.int32, (1, nq), 1) // BLOCK
    sinks = []
    for hh in range(2):
        sk = jnp.full((1, nq), sink_ref[hh] * LOG2_E, F32)
        for c in range(1, A_GROUP):
            sk = jnp.where(chunk_of == c, sink_ref[2 * c + hh] * LOG2_E, sk)
        sinks.append(sk)
    if local:
        kj = lax.broadcasted_iota(jnp.int32, (BLOCK, nq), 0)
        qi = lax.broadcasted_iota(jnp.int32, (BLOCK, nq), 1) % BLOCK

    rows_out = []
    for blk in range(n_blk):
        qts = jnp.concatenate([qt_ref[0, c * LANES:(c + 1) * LANES, blk * BLOCK:(blk + 1) * BLOCK]
                               for c in range(A_GROUP)], axis=1)
        if local:
            kwin = jnp.concatenate(k_blocks[blk:blk + 3] + [kx_ref[0]], axis=0)
            vtw = jnp.concatenate(v_blocks[blk:blk + 3] + [vx_ref[0]], axis=1)
            ok_prev = (kj >= qi) & (step > 0) if blk == 0 else (kj >= qi)
            ok_next = (kj <= qi) & (step < n_steps - 1) if blk == n_blk - 1 else (kj <= qi)
        else:
            kwin, vtw, ok_prev, ok_next = kx_ref[0], vx_ref[0], None, None
        ot = _attn_block(kwin, vtw, qts, sinks, ok_prev, ok_next)
        rows = slice(blk * BLOCK, (blk + 1) * BLOCK)
        outs = []
        for c in range(A_GROUP):
            sg = sg_ref[0, rows, c * LANES:(c + 1) * LANES].astype(F32)
            outs.append((ot[:, c * BLOCK:(c + 1) * BLOCK].T * sg).astype(BF16))
        rows_out.append(jnp.concatenate(outs, axis=1))
    y = _dot(jnp.concatenate(rows_out, axis=0), w_ref[...])
    gate = m_ref[0][:, 2 * D:3 * D]
    o_ref[0] = x_ref[0] + gate * y


def _attention(sink, qt, k, vt, kx, vxt, sg, x, mod3, w_out):
    b, s, _ = sg.shape
    rows = A_BLOCKS_PER_STEP * BLOCK
    assert s % rows == 0
    ns = s // rows
    n_ctx = kx.shape[1]
    blk = lambda w: pl.BlockSpec((1, rows, w), lambda bi, i: (bi, i, 0))
    blk_t = lambda w: pl.BlockSpec((1, w, rows), lambda bi, i: (bi, 0, i))
    lo = lambda i: jnp.maximum(i - 1, 0)
    hi = lambda i: jnp.minimum(i + 1, ns - 1)
    in_specs = [pl.BlockSpec(memory_space=pltpu.SMEM), blk_t(A_WIDTH),
                pl.BlockSpec((1, rows, A_KV_WIDTH), lambda bi, i: (bi, lo(i), 0)),
                blk(A_KV_WIDTH),
                pl.BlockSpec((1, rows, A_KV_WIDTH), lambda bi, i: (bi, hi(i), 0)),
                pl.BlockSpec((1, A_KV_WIDTH, rows), lambda bi, i: (bi, 0, lo(i))),
                blk_t(A_KV_WIDTH),
                pl.BlockSpec((1, A_KV_WIDTH, rows), lambda bi, i: (bi, 0, hi(i))),
                pl.BlockSpec((1, n_ctx, A_KV_WIDTH), lambda bi, i: (bi, 0, 0)),
                pl.BlockSpec((1, A_KV_WIDTH, n_ctx), lambda bi, i: (bi, 0, 0)),
                blk(A_WIDTH), blk(D),
                pl.BlockSpec((1, 1, 3 * D), lambda bi, i: (bi, 0, 0)),
                pl.BlockSpec((A_WIDTH, D), lambda bi, i: (0, 0))]
    return pl.pallas_call(
        functools.partial(_attn_kernel, True, ns),
        grid=(b, ns),
        in_specs=in_specs,
        out_specs=blk(D),
        out_shape=jax.ShapeDtypeStruct((b, s, D), F32),
        compiler_params=_params("parallel", "parallel"),
        name="attn_local",
    )(sink, qt, k, k, k, vt, vt, vt, kx, vxt, sg, x, mod3, w_out)


def _ctx_attn_kernel(sink_ref, x_ref, m_ref, g_ref, win_ref, wout_ref, k_ref, vt_ref, o_ref,
                     win_bf_ref, wout_bf_ref, qt_scr, sg_scr):
    _pair_cast_columns(win_ref, win_bf_ref)
    _pair_cast_rows(wout_ref, wout_bf_ref)
    _inproj_attn_kernel(False, x_ref, m_ref, g_ref, win_bf_ref, qt_scr, k_ref, vt_ref, sg_scr)
    _attn_kernel(False, 1, sink_ref, qt_scr, k_ref, vt_ref, sg_scr, x_ref, m_ref, wout_bf_ref, o_ref)


def _ctx_attention(sink, xc, mod3, mod_row, norm_g, w_in, w_out):
    b, n, _ = xc.shape
    const = lambda shape, **kw: pl.BlockSpec(shape, lambda bi, i: (0,) * len(shape), **kw)
    return pl.pallas_call(
        _ctx_attn_kernel,
        grid=(b, 1),
        in_specs=[pl.BlockSpec(memory_space=pltpu.SMEM),
                  pl.BlockSpec((1, n, D), lambda bi, i: (bi, 0, 0)),
                  pl.BlockSpec((1, 1, 3 * D), lambda bi, i: (mod_row, 0, 0)),
                  const((1, D)),
                  const(w_in.shape, pipeline_mode=pl.Buffered(1)),
                  const(w_out.shape, pipeline_mode=pl.Buffered(1))],
        out_specs=(pl.BlockSpec((1, n, A_KV_WIDTH), lambda bi, i: (bi, 0, 0)),
                   pl.BlockSpec((1, A_KV_WIDTH, n), lambda bi, i: (bi, 0, 0)),
                   pl.BlockSpec((1, n, D), lambda bi, i: (bi, 0, 0)),
                   const(w_in.shape), const(w_out.shape)),
        out_shape=(jax.ShapeDtypeStruct((b, n, A_KV_WIDTH), BF16),
                   jax.ShapeDtypeStruct((b, A_KV_WIDTH, n), BF16),
                   jax.ShapeDtypeStruct((b, n, D), F32),
                   jax.ShapeDtypeStruct(w_in.shape, BF16), jax.ShapeDtypeStruct(w_out.shape, BF16)),
        scratch_shapes=[pltpu.VMEM((1, A_WIDTH, n), BF16), pltpu.VMEM((1, n, A_WIDTH), BF16)],
        compiler_params=_params("arbitrary", "arbitrary"),
        name="ctx_attn",
    )(sink, xc, mod3, norm_g.reshape(1, D), w_in, w_out)


def _chunk_cumsum(x, tri):
    return _dot(tri, x.astype(BF16))


def _inproj_gla_kernel(chunk, with_q, x_ref, m_ref, g_ref, w_in_ref, wa1_ref, wa2_ref, ba_ref, tri_ref, *outs):
    if with_q:
        w_ref = w_in_ref
        per_dir = (outs[0:3], outs[3:6])
        v_ref, sg_ref = outs[6:]
    else:
        w_ref = outs[2]
        _cast_once(w_in_ref, w_ref)
        per_dir = ((outs[0],), (outs[1],))
    tm = x_ref.shape[1]
    hb = _norm_mod(x_ref[0], g_ref[...], m_ref[0]).astype(BF16)
    r = _dot(hb, wa1_ref[...])
    z = _dot(r.astype(BF16), wa2_ref[...]) + ba_ref[...]
    e = jnp.exp2(jnp.abs(z) * -LOG2_E)
    la = (jnp.minimum(z, 0.0) * LOG2_E - jnp.log2(1.0 + e)) * (1.0 / GATE_TEMP)
    k = _dot(hb, w_ref[:, B_K_WIDTH:2 * B_K_WIDTH])
    if with_q:
        q = _dot(hb, w_ref[:, 0:B_K_WIDTH]) * K_SCALE
    n_chunks = tm // chunk
    k_ends = []
    for reverse in (False, True):
        lad = la[:, B_K_WIDTH:] if reverse else la[:, 0:B_K_WIDTH]
        refs = per_dir[1] if reverse else per_dir[0]
        tots, kis, k_end = [], [], None
        tri = tri_ref[1 if reverse else 0]
        tb = tri.shape[0]
        cum_all = jnp.concatenate([_chunk_cumsum(lad[r * tb:(r + 1) * tb], tri) for r in range(tm // tb)], axis=0)
        for c in range(n_chunks):
            rows = slice(c * chunk, (c + 1) * chunk)
            cum = cum_all[rows]
            tot = cum[0:1] if reverse else cum[chunk - 1:chunk]
            if not with_q:
                k_end = (k[rows] * jnp.exp2(tot - cum)).astype(BF16)
                continue
            refs[0][0, rows] = (q[rows] * jnp.exp2(cum)).astype(BF16)
            kis.append(k[rows] * jnp.exp2(-cum))
            tots.append(tot)
            if c % 2 == 1:
                pair = slice((c - 1) * chunk, (c + 1) * chunk)
                refs[1][0, :, pair] = jnp.concatenate(kis[-2:], axis=0).T.astype(BF16)
        if with_q:
            pad = [jnp.zeros((DEC_ROWS - n_chunks, B_K_WIDTH), F32)] if n_chunks < DEC_ROWS else []
            refs[2][0, 0] = jnp.exp2(jnp.concatenate(tots + pad, axis=0))
        else:
            k_ends.append(k_end)
    if with_q:
        sg_ref[0] = _silu(_dot(hb, w_ref[:, 2 * B_K_WIDTH + B_V_WIDTH:])).astype(BF16)
    v = _dot(hb, w_ref[:, 2 * B_K_WIDTH:2 * B_K_WIDTH + B_V_WIDTH]).astype(BF16)
    if with_q:
        v_ref[0] = v
    else:
        for refs, k_end in zip(per_dir, k_ends):
            for h in range(B_HEADS):
                ks = slice(h * B_KEY_DIM, (h + 1) * B_KEY_DIM)
                refs[0][0, h] = _dot_tn(k_end[:, ks], v[:, h * B_VAL_DIM:(h + 1) * B_VAL_DIM])


def _inproj_gla(x, mod3, mod_row, norm_g, w, wa1, wa2, ba, chunk, with_q):
    b, s, _ = x.shape
    tm = min(ROW_TILE, s)
    nt = s // tm
    cpt = tm // chunk
    tb = max(chunk, MXU_DIM)
    assert tm % tb == 0 and tb % chunk == 0
    if mod_row is None:
        mod_map = lambda bi, i: (bi, 0, 0)
    else:
        mod_map = lambda bi, i: (mod_row, 0, 0)
    const = lambda shape, **kw: pl.BlockSpec(shape, lambda bi, i: (0,) * len(shape), **kw)
    in_specs = [
        pl.BlockSpec((1, tm, D), lambda bi, i: (bi, i, 0)),
        pl.BlockSpec((1, 1, 3 * D), mod_map),
        const((1, D)), const(w.shape, **({} if with_q else {"pipeline_mode": pl.Buffered(1)})),
        const(wa1.shape), const(wa2.shape), const(ba.shape),
        const((2, tb, tb)),
    ]
    t_idx = np.arange(tb)
    same = (t_idx[:, None] // chunk) == (t_idx[None, :] // chunk)
    lower = same & (t_idx[None, :] <= t_idx[:, None])
    tri = jnp.asarray(np.stack([lower, lower.T]), dtype=BF16)
    rows = lambda w_: (jax.ShapeDtypeStruct((b, s, w_), BF16),
                       pl.BlockSpec((1, tm, w_), lambda bi, i: (bi, i, 0)))
    cols = (jax.ShapeDtypeStruct((b, B_K_WIDTH, s), BF16),
            pl.BlockSpec((1, B_K_WIDTH, tm), lambda bi, i: (bi, 0, i)))
    decs = (jax.ShapeDtypeStruct((b, nt, DEC_ROWS, B_K_WIDTH), F32),
            pl.BlockSpec((1, 1, DEC_ROWS, B_K_WIDTH), lambda bi, i: (bi, i, 0, 0)))
    if with_q:
        assert cpt <= DEC_ROWS and cpt % 2 == 0
        outs = [rows(B_K_WIDTH), cols, decs] * 2 + [rows(B_V_WIDTH)] * 2
    else:
        assert nt == 1 and cpt == 1
        state = (jax.ShapeDtypeStruct((b, B_HEADS, B_KEY_DIM, B_VAL_DIM), F32),
                 pl.BlockSpec((1, B_HEADS, B_KEY_DIM, B_VAL_DIM), lambda bi, i: (bi, 0, 0, 0)))
        outs = [state] * 2 + [(jax.ShapeDtypeStruct(w.shape, BF16), const(w.shape))]
    return pl.pallas_call(
        functools.partial(_inproj_gla_kernel, chunk, with_q),
        grid=(b, nt),
        in_specs=in_specs,
        out_specs=tuple(o[1] for o in outs),
        out_shape=tuple(o[0] for o in outs),
        compiler_params=_params("parallel", "parallel") if with_q else _params("arbitrary", "arbitrary"),
        name="inproj_gla" if with_q else "inproj_gla_ctx",
    )(x, mod3, norm_g.reshape(1, D), w, wa1, wa2, ba, tri)


def _scan_tile(reverse, qd_ref, kit_ref, dec_ref, v_ref, st_ref):
    pair_rows = 2 * CHUNK
    ti = lax.broadcasted_iota(jnp.int32, (pair_rows, pair_rows), 0)
    si = lax.broadcasted_iota(jnp.int32, (pair_rows, pair_rows), 1)
    same = (ti // CHUNK) == (si // CHUNK)
    if reverse:
        use_inv = same & (ti <= si)
        use_end = (ti < CHUNK) & (si >= CHUNK)
        second_row = lax.broadcasted_iota(jnp.int32, (pair_rows, B_KEY_DIM), 0) < CHUNK
        first_col = lax.broadcasted_iota(jnp.int32, (B_KEY_DIM, pair_rows), 1) >= CHUNK
    else:
        use_inv = same & (ti >= si)
        use_end = (ti >= CHUNK) & (si < CHUNK)
        second_row = lax.broadcasted_iota(jnp.int32, (pair_rows, B_KEY_DIM), 0) >= CHUNK
        first_col = lax.broadcasted_iota(jnp.int32, (B_KEY_DIM, pair_rows), 1) < CHUNK
    n_pairs = SCAN_TILE // pair_rows
    order = range(n_pairs - 1, -1, -1) if reverse else range(n_pairs)
    n_dec = dec_ref.shape[1] * DEC_ROWS
    dec = dec_ref[0].reshape(n_dec, B_K_WIDTH)
    dect = jnp.concatenate([dec, jnp.zeros((LANES - n_dec, B_K_WIDTH), F32)], axis=0).T
    heads = []
    for h in range(B_HEADS):
        ks = slice(h * B_KEY_DIM, (h + 1) * B_KEY_DIM)
        vs = slice(h * B_VAL_DIM, (h + 1) * B_VAL_DIM)
        st = st_ref[h]
        o_rows = [None] * n_pairs
        for p in order:
            rows = slice(p * pair_rows, (p + 1) * pair_rows)
            c_first, c_second = (2 * p + 1, 2 * p) if reverse else (2 * p, 2 * p + 1)
            qd = qd_ref[0, rows, ks]
            kit = kit_ref[0, ks, rows]
            vh = v_ref[0, rows, vs]
            d_first, d_second = dect[ks, c_first:c_first + 1], dect[ks, c_second:c_second + 1]
            ki32 = kit.astype(F32)
            s2 = _dot(qd, jnp.concatenate([kit, (ki32 * d_first).astype(BF16)], axis=1))
            a = jnp.where(use_inv, s2[:, 0:pair_rows], jnp.where(use_end, s2[:, pair_rows:], 0.0)).astype(BF16)
            q_pair = jnp.where(second_row, qd.astype(F32) * dec[c_first:c_first + 1, ks], qd.astype(F32))
            k_pair = ki32 * jnp.where(first_col, d_first * d_second, d_second)
            o_rows[p] = _dot(a, vh) + _dot(q_pair.astype(BF16), st.astype(BF16))
            st = st * (d_first * d_second) + _dot(k_pair.astype(BF16), vh)
        st_ref[h] = st
        heads.append(jnp.concatenate(o_rows, axis=0))
    return jnp.concatenate(heads, axis=1)


def _gla_scan_kernel(nt, qdf_ref, kitf_ref, decf_ref, qdb_ref, kitb_ref, decb_ref,
                     v_ref, sf_ref, sb_ref, sg_ref, x_ref, m_ref, hn_ref, w32_ref, fn_ref,
                     o_ref, st_ref, of_ref, w_ref):
    _cast_once(w32_ref, w_ref)
    j = pl.program_id(1)

    @pl.when(j == 0)
    def _():
        st_ref[...] = sf_ref[0]

    @pl.when(j == nt)
    def _():
        st_ref[...] = sb_ref[0]

    @pl.when(j < nt)
    def _():
        of_ref[j] = _scan_tile(False, qdf_ref, kitf_ref, decf_ref, v_ref, st_ref)

    @pl.when(j >= nt)
    def _():
        o = _scan_tile(True, qdb_ref, kitb_ref, decb_ref, v_ref, st_ref) + of_ref[2 * nt - 1 - j]
        normed = []
        for h in range(B_HEADS):
            oh = o[:, h * B_VAL_DIM:(h + 1) * B_VAL_DIM]
            normed.append(oh * lax.rsqrt(jnp.mean(oh * oh, axis=-1, keepdims=True) + EPS))
        of = jnp.concatenate(normed, axis=1) * hn_ref[...]
        y = _dot((of * sg_ref[0].astype(F32)).astype(BF16), w_ref[...])
        xn = x_ref[0] + m_ref[0][:, 2 * D:3 * D] * y
        ms = jnp.mean(xn * xn, axis=-1, keepdims=True)
        o_ref[0] = xn * lax.rsqrt(ms + EPS) * fn_ref[...]


def _gla_scan(fwd, bwd, v, s_f, s_b, sg, x, mod3, head_g, w_out, final_g):
    b, s, _ = v.shape
    assert SCAN_TILE % ROW_TILE == 0 and ROW_TILE // CHUNK == DEC_ROWS
    dec_blocks = SCAN_TILE // ROW_TILE
    nt = s // SCAN_TILE
    t_fwd = lambda j: jnp.minimum(j, nt - 1)
    t_bwd = lambda j: 2 * nt - 1 - jnp.maximum(j, nt)
    t_both = lambda j: jnp.where(j < nt, j, 2 * nt - 1 - j)

    def direction(t):
        rows = pl.BlockSpec((1, SCAN_TILE, B_K_WIDTH), lambda bi, j: (bi, t(j), 0))
        cols = pl.BlockSpec((1, B_K_WIDTH, SCAN_TILE), lambda bi, j: (bi, 0, t(j)))
        decs = pl.BlockSpec((1, dec_blocks, DEC_ROWS, B_K_WIDTH), lambda bi, j: (bi, t(j), 0, 0))
        return [rows, cols, decs]

    const = lambda shape: pl.BlockSpec(shape, lambda bi, j: (0,) * len(shape))
    state = pl.BlockSpec((1, B_HEADS, B_KEY_DIM, B_VAL_DIM), lambda bi, j: (bi, 0, 0, 0))
    out_rows = lambda w_: pl.BlockSpec((1, SCAN_TILE, w_), lambda bi, j: (bi, t_bwd(j), 0))
    in_specs = direction(t_fwd) + direction(t_bwd) + [
        pl.BlockSpec((1, SCAN_TILE, B_V_WIDTH), lambda bi, j: (bi, t_both(j), 0)),
        state, state, out_rows(B_V_WIDTH), out_rows(D),
        pl.BlockSpec((1, 1, 3 * D), lambda bi, j: (bi, 0, 0)),
        const((1, B_V_WIDTH)),
        pl.BlockSpec((B_V_WIDTH, D), lambda bi, j: (0, 0), pipeline_mode=pl.Buffered(1)),
        const((1, D))]
    return pl.pallas_call(
        functools.partial(_gla_scan_kernel, nt),
        grid=(b, 2 * nt),
        in_specs=in_specs,
        out_specs=out_rows(D),
        out_shape=jax.ShapeDtypeStruct((b, s, D), F32),
        scratch_shapes=[pltpu.VMEM((B_HEADS, B_KEY_DIM, B_VAL_DIM), F32),
                        pltpu.VMEM((nt, SCAN_TILE, B_V_WIDTH), F32),
                        pltpu.VMEM((B_V_WIDTH, D), BF16)],
        compiler_params=_params("arbitrary", "arbitrary", vmem_limit=SCAN_VMEM_LIMIT),
        name="gla_scan",
    )(*fwd, *bwd, v, s_f, s_b, sg, x, mod3, head_g.reshape(1, B_V_WIDTH), w_out, final_g.reshape(1, D))


def _rope_tables(n_tokens):
    rows_n = n_tokens // GRID_W
    row = np.repeat(np.arange(rows_n, dtype=np.float32), GRID_W)
    col = np.tile(np.arange(GRID_W, dtype=np.float32), rows_n)
    inv_freq = (np.float32(ROPE_BASE) ** (-np.arange(ROPE_FREQS, dtype=np.float32) / np.float32(ROPE_FREQS)))
    inv_freq = inv_freq.astype(np.float32)
    ang = np.stack([row[:, None] * inv_freq, col[:, None] * inv_freq], axis=1)
    cos, sin = np.cos(ang).astype(np.float32), np.sin(ang).astype(np.float32)
    zero = np.zeros_like(sin)
    tile = lambda t: np.tile(t.reshape(n_tokens, A_HEAD_DIM), (1, LANES // A_HEAD_DIM))
    return (tile(np.stack([cos, cos], axis=2)),
            tile(np.stack([-sin, zero], axis=2)),
            tile(np.stack([zero, sin], axis=2)))


def _pair_heads(t, axis):
    shape = t.shape
    t = t.reshape(shape[:axis] + (A_KV_HEADS, A_GROUP, -1) + shape[axis + 1:])
    return jnp.swapaxes(t, axis, axis + 1).reshape(shape)


def kernel(x, c, ctx, c_ctx, l0_norm_g, l0_w_ada, l0_b_ada, l0_w_in, l0_sink, l0_w_out, l1_norm_g, l1_w_ada, l1_b_ada, l1_w_in, l1_wa1_f, l1_wa2_f, l1_ba_f, l1_wa1_b, l1_wa2_b, l1_ba_b, l1_head_norm_g, l1_w_out, final_norm_g):
    b, s, _ = x.shape
    ctx_row = b

    cvec = jnp.concatenate([c, c_ctx[None, :], jnp.zeros((MOD_ROWS - b - 1, D), F32)], axis=0)
    mod0, mod1 = (m.reshape(MOD_ROWS, 1, 3 * D) for m in _modulation(cvec, l0_w_ada, l0_b_ada, l1_w_ada, l1_b_ada))

    sink = _pair_heads(l0_sink.astype(F32), 0)
    kc, vct, xc1, w0_in, w0_out = _ctx_attention(sink, ctx, mod0, ctx_row, l0_norm_g, l0_w_in, l0_w_out)
    qt, k, vt, sg = _inproj_attn(x, mod0, l0_norm_g, w0_in, _rope_tables(s))
    x1 = _attention(sink, qt, k, vt, kc, vct, sg, x, mod0, w0_out)

    wa1 = jnp.concatenate([l1_wa1_f, l1_wa1_b, jnp.zeros((D, LANES - 2 * GATE_RANK), F32)], axis=1).astype(BF16)
    wa2 = jnp.zeros((LANES, 2 * B_K_WIDTH), F32)
    wa2 = wa2.at[0:GATE_RANK, 0:B_K_WIDTH].set(l1_wa2_f)
    wa2 = wa2.at[GATE_RANK:2 * GATE_RANK, B_K_WIDTH:].set(l1_wa2_b).astype(BF16)
    ba = jnp.concatenate([l1_ba_f, l1_ba_b]).reshape(1, 2 * B_K_WIDTH)
    s_f, s_b, w1 = _inproj_gla(xc1, mod1, ctx_row, l1_norm_g, l1_w_in, wa1, wa2, ba, ctx.shape[1], False)
    outs = _inproj_gla(x1, mod1, None, l1_norm_g, w1, wa1, wa2, ba, CHUNK, True)
    v1, sg1 = outs[6:]
    return _gla_scan(outs[0:3], outs[3:6], v1, s_f, s_b, sg1, x1, mod1,
                     l1_head_norm_g, l1_w_out, final_norm_g)
```
